```python
import math
import jax, jax.numpy as jnp
from jax import lax
import numpy as np

D_MODEL = 1024
BATCH = 32
SEQ = 256
DEPTH = 1
DEC_BATCH = 2
DEC_SEQ = 1024
PAST_LEN = 512

GRID_W = 64
N_HEADS = 8
Q_LORA = 512
KV_LORA = 256
QK_NOPE = 64
QK_ROPE = 32
V_HEAD = 64
QK_HEAD = QK_NOPE + QK_ROPE
ATTN_SCALE = QK_HEAD ** -0.5
ROPE_BASE = 10000.0
Q_BLOCK = 128
FN_GROUPS = 4
FN_GROUP_W = 128
FN_WIDTH = FN_GROUPS * FN_GROUP_W
N_GROUPS = 4
EXPERTS_PER_GROUP = 4
N_EXPERTS = N_GROUPS * EXPERTS_PER_GROUP
TOP_K_EXPERTS = 2
D_EXPERT = 512
EPS = 1e-6
IN_WIDTH = Q_LORA + KV_LORA + QK_ROPE + FN_WIDTH + 2 * D_MODEL
IN_OFFSETS = (Q_LORA, Q_LORA + KV_LORA, Q_LORA + KV_LORA + QK_ROPE, Q_LORA + KV_LORA + QK_ROPE + FN_WIDTH, Q_LORA + KV_LORA + QK_ROPE + FN_WIDTH + D_MODEL)

kernel_name = "hybrid_mla_fnet_hmoe_diffusion_step"


def rmsnorm(x, g):
    xf = x.astype(jnp.float32)
    y = xf * lax.rsqrt(jnp.mean(xf * xf, axis=-1, keepdims=True) + EPS)
    return (y * g).astype(x.dtype)


def adaln(cond, w_mod, b_mod):
    m = jax.nn.silu(cond) @ w_mod + b_mod
    return jnp.split(m[:, None, :], 6, axis=-1)


def modulate(x, g, shift, scale):
    return rmsnorm(x, g) * (1 + scale) + shift


def rope_axis(x, pos):
    half = x.shape[-1] // 2
    freqs = ROPE_BASE ** (-jnp.arange(half, dtype=jnp.float32) / half)
    ang = pos.astype(jnp.float32)[:, None] * freqs[None, :]
    cos = jnp.cos(ang)[None, :, None, :]
    sin = jnp.sin(ang)[None, :, None, :]
    x1 = x[..., :half].astype(jnp.float32)
    x2 = x[..., half:].astype(jnp.float32)
    return jnp.concatenate([x1 * cos - x2 * sin, x1 * sin + x2 * cos], axis=-1).astype(x.dtype)


def rope_2d(x):
    n = x.shape[1]
    rows = n // GRID_W
    row = jnp.repeat(jnp.arange(rows), GRID_W)
    col = jnp.tile(jnp.arange(GRID_W), rows)
    half = QK_ROPE // 2
    x_nope = x[..., :QK_NOPE]
    x_row = x[..., QK_NOPE:QK_NOPE + half]
    x_col = x[..., QK_NOPE + half:]
    return jnp.concatenate([x_nope, rope_axis(x_row, row), rope_axis(x_col, col)], axis=-1)


def mixer_inputs(x, shift, scale, norm_g, w_in):
    h = modulate(x, norm_g, shift, scale)
    return jnp.split(h @ w_in, list(IN_OFFSETS), axis=-1)


def mla_query(q_a, q_a_g, w_q_b, q_norm_g):
    b, s, _ = q_a.shape
    q = (rmsnorm(q_a, q_a_g) @ w_q_b).reshape(b, s, N_HEADS, QK_HEAD)
    return rmsnorm(q, q_norm_g)


def mla_kv(ckv, kpe, w_kv_b, k_norm_g):
    b, l, _ = ckv.shape
    kv = (ckv @ w_kv_b).reshape(b, l, N_HEADS, QK_NOPE + V_HEAD)
    k_nope, v = kv[..., :QK_NOPE], kv[..., QK_NOPE:]
    k_pe = jnp.broadcast_to(kpe[:, :, None, :], (b, l, N_HEADS, QK_ROPE))
    k = rmsnorm(jnp.concatenate([k_nope, k_pe], axis=-1), k_norm_g)
    return k, v


def attend(q, k, v):
    b, sq, h, dq = q.shape
    nb = sq // Q_BLOCK
    qb = q.reshape(b, nb, Q_BLOCK, h, dq).transpose(1, 0, 2, 3, 4)

    def one_block(qi):
        s = jnp.einsum('bqhd,bkhd->bhqk', qi, k).astype(jnp.float32) * ATTN_SCALE
        p = jax.nn.softmax(s, axis=-1)
        return jnp.einsum('bhqk,bkhd->bqhd', p.astype(v.dtype), v)

    o = lax.map(one_block, qb)
    return o.transpose(1, 0, 2, 3, 4).reshape(b, sq, h * V_HEAD)


def fourier_mix(f):
    b, s, _ = f.shape
    fg = f.reshape(b, s, FN_GROUPS, FN_GROUP_W).astype(jnp.float32)
    y = jnp.fft.fft2(fg, axes=(1, 3), norm='ortho').real
    return y.reshape(b, s, FN_WIDTH).astype(f.dtype)


def merge_branches(attn, fn, gate_a, gate_f, w_attn_o, w_fnet, w_out):
    u = jax.nn.sigmoid(gate_a) * (attn @ w_attn_o) + jax.nn.sigmoid(gate_f) * (fourier_mix(fn) @ w_fnet)
    return u @ w_out


def hier_moe(h, w_rg, b_rg, w_re, b_re, w_gate, w_up, w_down):
    b, s, d = h.shape
    n = b * s
    t = h.reshape(n, d)
    g_prob = jax.nn.softmax((t @ w_rg).astype(jnp.float32) + b_rg, axis=-1)
    g_top_p, g_top_i = lax.top_k(g_prob, 1)
    e_logit = ((t @ w_re).astype(jnp.float32) + b_re).reshape(n, N_GROUPS, EXPERTS_PER_GROUP)
    e_sel = e_logit[jnp.arange(n), g_top_i[:, 0]]
    e_top_l, e_top_i = lax.top_k(e_sel, TOP_K_EXPERTS)
    e_w = jax.nn.softmax(e_top_l, axis=-1) * g_top_p
    expert_idx = g_top_i * EXPERTS_PER_GROUP + e_top_i
    combine = jnp.sum(jax.nn.one_hot(expert_idx, N_EXPERTS, dtype=jnp.float32) * e_w[..., None], axis=1)
    a = jnp.einsum('nd,edf->nef', t, w_gate)
    u = jnp.einsum('nd,edf->nef', t, w_up)
    act = jax.nn.silu(a) * u * combine.astype(t.dtype)[..., None]
    y = jnp.einsum('nef,efd->nd', act, w_down)
    return y.reshape(b, s, d)


def setup_inputs(seed: int = 0) -> dict:
    key = jax.random.key(seed)
    ks = jax.random.split(key, 32)

    def nrm(k, shape, scale):
        return jax.random.normal(k, shape, jnp.float32) * scale

    def gain(k, shape):
        return 1.0 + 0.02 * jax.random.normal(k, shape, jnp.float32)

    L = DEPTH
    return {
        'x_prompt': nrm(ks[0], (BATCH, SEQ, D_MODEL), 1.0),
        'x_sample': nrm(ks[1], (DEC_BATCH, DEC_SEQ, D_MODEL), 1.0),
        'cache_ckv': nrm(ks[2], (DEC_BATCH, DEPTH, PAST_LEN, KV_LORA), 1.0),
        'cache_kpe': nrm(ks[3], (DEC_BATCH, DEPTH, PAST_LEN, QK_ROPE), 1.0),
        'c': nrm(ks[4], (DEC_BATCH, D_MODEL), 1.0),
        'c_ctx': nrm(ks[5], (D_MODEL,), 1.0),
        'w_mod': nrm(ks[6], (L, D_MODEL, 6 * D_MODEL), D_MODEL ** -0.5),
        'b_mod': nrm(ks[7], (L, 6 * D_MODEL), 0.02),
        'norm1_g': gain(ks[8], (L, D_MODEL)),
        'w_in': nrm(ks[9], (L, D_MODEL, IN_WIDTH), D_MODEL ** -0.5),
        'q_a_norm_g': gain(ks[10], (L, Q_LORA)),
        'w_q_b': nrm(ks[11], (L, Q_LORA, N_HEADS * QK_HEAD), Q_LORA ** -0.5),
        'kv_a_norm_g': gain(ks[12], (L, KV_LORA)),
        'w_kv_b': nrm(ks[13], (L, KV_LORA, N_HEADS * (QK_NOPE + V_HEAD)), KV_LORA ** -0.5),
        'q_norm_g': gain(ks[14], (L, QK_HEAD)),
        'k_norm_g': gain(ks[15], (L, QK_HEAD)),
        'w_attn_o': nrm(ks[16], (L, N_HEADS * V_HEAD, D_MODEL), (N_HEADS * V_HEAD) ** -0.5),
        'w_fnet': nrm(ks[17], (L, FN_WIDTH, D_MODEL), FN_WIDTH ** -0.5),
        'w_out': nrm(ks[18], (L, D_MODEL, D_MODEL), D_MODEL ** -0.5),
        'norm2_g': gain(ks[19], (L, D_MODEL)),
        'w_router_group': nrm(ks[20], (L, D_MODEL, N_GROUPS), D_MODEL ** -0.5),
        'b_router_group': nrm(ks[21], (L, N_GROUPS), 0.01),
        'w_router_expert': nrm(ks[22], (L, D_MODEL, N_EXPERTS), D_MODEL ** -0.5),
        'b_router_expert': nrm(ks[23], (L, N_EXPERTS), 0.01),
        'w_exp_gate': nrm(ks[24], (L, N_EXPERTS, D_MODEL, D_EXPERT), D_MODEL ** -0.5),
        'w_exp_up': nrm(ks[25], (L, N_EXPERTS, D_MODEL, D_EXPERT), D_MODEL ** -0.5),
        'w_exp_down': nrm(ks[26], (L, N_EXPERTS, D_EXPERT, D_MODEL), D_EXPERT ** -0.5),
    }


def reference(x_prompt, x_sample, cache_ckv, cache_kpe, c, c_ctx, w_mod, b_mod, norm1_g, w_in, q_a_norm_g, w_q_b,
              kv_a_norm_g, w_kv_b, q_norm_g, k_norm_g, w_attn_o, w_fnet, w_out, norm2_g, w_router_group,
              b_router_group, w_router_expert, b_router_expert, w_exp_gate, w_exp_up, w_exp_down):
    xp = x_prompt
    xs = x_sample
    ckv_layers = []
    kpe_layers = []
    for l in range(DEPTH):
        mod_ctx = adaln(c_ctx[None, :], w_mod[l], b_mod[l])
        mod_lat = adaln(c, w_mod[l], b_mod[l])
        moe_args = (w_router_group[l], b_router_group[l], w_router_expert[l], b_router_expert[l],
                    w_exp_gate[l], w_exp_up[l], w_exp_down[l])

        q_a, kv_a, kpe, fn, ga, gf = mixer_inputs(xp, mod_ctx[0], mod_ctx[1], norm1_g[l], w_in[l])
        ckv = rmsnorm(kv_a, kv_a_norm_g[l])
        q = mla_query(q_a, q_a_norm_g[l], w_q_b[l], q_norm_g[l])
        k, v = mla_kv(ckv, kpe, w_kv_b[l], k_norm_g[l])
        attn = attend(q, k, v)
        xp = xp + mod_ctx[2] * merge_branches(attn, fn, ga, gf, w_attn_o[l], w_fnet[l], w_out[l])
        xp = xp + mod_ctx[5] * hier_moe(modulate(xp, norm2_g[l], mod_ctx[3], mod_ctx[4]), *moe_args)
        ckv_layers.append(ckv)
        kpe_layers.append(kpe)

        q_a, kv_a, kpe_s, fn, ga, gf = mixer_inputs(xs, mod_lat[0], mod_lat[1], norm1_g[l], w_in[l])
        ckv_s = rmsnorm(kv_a, kv_a_norm_g[l])
        q = rope_2d(mla_query(q_a, q_a_norm_g[l], w_q_b[l], q_norm_g[l]))
        k_lat, v_lat = mla_kv(ckv_s, kpe_s, w_kv_b[l], k_norm_g[l])
        k_lat = rope_2d(k_lat)
        k_ctx, v_ctx = mla_kv(cache_ckv[:, l], cache_kpe[:, l], w_kv_b[l], k_norm_g[l])
        attn = attend(q, jnp.concatenate([k_ctx, k_lat], axis=1), jnp.concatenate([v_ctx, v_lat], axis=1))
        xs = xs + mod_lat[2] * merge_branches(attn, fn, ga, gf, w_attn_o[l], w_fnet[l], w_out[l])
        xs = xs + mod_lat[5] * hier_moe(modulate(xs, norm2_g[l], mod_lat[3], mod_lat[4]), *moe_args)

    new_ckv = jnp.stack(ckv_layers, axis=1)
    new_kpe = jnp.stack(kpe_layers, axis=1)
    return (xp, xs, new_ckv, new_kpe)
```

```python
import functools
import math

import numpy as np
import jax
import jax.numpy as jnp
from jax import lax
from jax.experimental import pallas as pl
from jax.experimental.pallas import tpu as pltpu

D_MODEL = 1024
GRID_W = 64
N_HEADS = 8
Q_LORA = 512
KV_LORA = 256
QK_NOPE = 64
QK_ROPE = 32
V_HEAD = 64
QK_HEAD = QK_NOPE + QK_ROPE
ATTN_SCALE = QK_HEAD ** -0.5
ROPE_BASE = 10000.0
FN_GROUPS = 4
FN_GROUP_W = 128
FN_WIDTH = FN_GROUPS * FN_GROUP_W
N_GROUPS = 4
EXPERTS_PER_GROUP = 4
N_EXPERTS = N_GROUPS * EXPERTS_PER_GROUP
D_EXPERT = 512
EPS = 1e-6

LANES = 128
HEAD_SLOT = LANES
QK_WIDTH = N_HEADS * HEAD_SLOT
V_WIDTH = N_HEADS * V_HEAD
C_QA = 0
C_KVA = C_QA + Q_LORA
C_KPE = C_KVA + KV_LORA
C_FN = C_KPE + LANES
C_GA = C_FN + FN_WIDTH
C_GF = C_GA + D_MODEL
IN_PACKED = C_GF + D_MODEL
ROUTER_ROWS = 32
MOD_ROWS = 8
VMEM_LIMIT = 56 * 1024 * 1024

TM = 256
TM_MOE = 1024

BF16 = jnp.bfloat16
F32 = jnp.float32


def _cparams(sem):
    return pltpu.CompilerParams(dimension_semantics=sem, vmem_limit_bytes=VMEM_LIMIT)


def _dot(a, b):
    return jnp.dot(a, b, preferred_element_type=F32)


def _dot_nt(a, b):
    return lax.dot_general(a, b, (((1,), (1,)), ((), ())), preferred_element_type=F32)


def _sigmoid(x):
    return 1.0 / (1.0 + jnp.exp(-x))


@functools.lru_cache(maxsize=None)
def _rope_tables(n_pos):
    half = QK_ROPE // 2
    quarter = half // 2
    freqs = ROPE_BASE ** (-np.arange(quarter, dtype=np.float64) / quarter)
    pos = np.arange(n_pos)
    row = (pos // GRID_W).astype(np.float64)
    col = (pos % GRID_W).astype(np.float64)
    cos_t = np.ones((n_pos, LANES), np.float64)
    sin_a = np.zeros((n_pos, LANES), np.float64)
    sin_b = np.zeros((n_pos, LANES), np.float64)
    for base, p in ((QK_NOPE, row), (QK_NOPE + half, col)):
        ang = p[:, None] * freqs[None, :]
        cos_t[:, base:base + quarter] = np.cos(ang)
        cos_t[:, base + quarter:base + half] = np.cos(ang)
        sin_a[:, base:base + quarter] = -np.sin(ang)
        sin_b[:, base + quarter:base + half] = np.sin(ang)
    return (cos_t.astype(np.float32), sin_a.astype(np.float32), sin_b.astype(np.float32))


@functools.lru_cache(maxsize=None)
def _dft_tables(n):
    k = np.arange(n)
    ang = 2.0 * np.pi * ((k[:, None] * k[None, :]) % n) / n
    s = 1.0 / math.sqrt(n)
    return (np.cos(ang) * s).astype(np.float32), (-np.sin(ang) * s).astype(np.float32)


def _adaln_kernel(cond_ref, w_ref, b_ref, o_ref):
    c = cond_ref[...]
    s = c * _sigmoid(c)
    o_ref[...] = jnp.dot(s, w_ref[...], preferred_element_type=F32,
                         precision=lax.Precision.HIGHEST) + b_ref[...]


def _adaln(cond8, w_mod, b_mod):
    n = w_mod.shape[1]
    tn = D_MODEL
    return pl.pallas_call(
        _adaln_kernel,
        grid=(n // tn,),
        in_specs=[pl.BlockSpec((MOD_ROWS, D_MODEL), lambda j: (0, 0)),
                  pl.BlockSpec((D_MODEL, tn), lambda j: (0, j)),
                  pl.BlockSpec((1, tn), lambda j: (0, j))],
        out_specs=pl.BlockSpec((MOD_ROWS, tn), lambda j: (0, j)),
        out_shape=jax.ShapeDtypeStruct((MOD_ROWS, n), F32),
        compiler_params=_cparams(("arbitrary",)),
        name="adaln",
    )(cond8, w_mod, b_mod)


def _rms(x, width):
    return lax.rsqrt(jnp.sum(x * x, axis=-1, keepdims=True) * (1.0 / width) + EPS)


def _rope(x, cos_t, sin_a, sin_b):
    return x * cos_t + pltpu.roll(x, LANES - 8, 1) * sin_a + pltpu.roll(x, 8, 1) * sin_b


def _inproj_kernel(*refs, rope, emit_cache):
    it = iter(refs)
    x_ref, mod_ref, g1_ref, win_ref, qag_ref, wqb_ref, kvg_ref, wkb_ref, wvb_ref = (
        next(it) for _ in range(9))
    qg_ref, kg_ref, dft_ref = next(it), next(it), next(it)
    if rope:
        cos_ref, sa_ref, sb_ref = next(it), next(it), next(it)
    q_ref, k_ref, v_ref, fcs_ref, sga_ref, sgf_ref = (next(it) for _ in range(6))
    if emit_cache:
        ckv_ref, kpe_ref = next(it), next(it)

    x = x_ref[...]
    shift = mod_ref[0, 0:1, :]
    scale = mod_ref[0, 1:2, :]
    h = (x * _rms(x, D_MODEL) * g1_ref[...]) * (1.0 + scale) + shift
    hb = h.astype(BF16)

    if rope:
        cos_t, sin_a, sin_b = cos_ref[...], sa_ref[...], sb_ref[...]

    qa = _dot(hb, win_ref[:, C_QA:C_QA + Q_LORA])
    qn = (qa * _rms(qa, Q_LORA) * qag_ref[...]).astype(BF16)
    q = _dot(qn, wqb_ref[...])
    qg = qg_ref[...] * ATTN_SCALE
    for hd in range(N_HEADS):
        qh = q[:, hd * HEAD_SLOT:(hd + 1) * HEAD_SLOT]
        qh = qh * _rms(qh, QK_HEAD) * qg
        if rope:
            qh = _rope(qh, cos_t, sin_a, sin_b)
        q_ref[:, hd * HEAD_SLOT:(hd + 1) * HEAD_SLOT] = qh.astype(BF16)

    kva = _dot(hb, win_ref[:, C_KVA:C_KVA + KV_LORA])
    ckv = kva * _rms(kva, KV_LORA) * kvg_ref[...]
    kpe = _dot(hb, win_ref[:, C_KPE:C_KPE + LANES])
    if emit_cache:
        ckv_ref[...] = ckv
        kpe_ref[...] = kpe[:, QK_NOPE:QK_NOPE + QK_ROPE]
    _emit_kv(ckv.astype(BF16), kpe, wkb_ref, wvb_ref, kg_ref,
             (cos_t, sin_a, sin_b) if rope else None, k_ref, v_ref)

    fn = _dot(hb, win_ref[:, C_FN:C_FN + FN_WIDTH]).astype(BF16)
    for g in range(FN_GROUPS):
        cs = _dot(fn[:, g * FN_GROUP_W:(g + 1) * FN_GROUP_W], dft_ref[...])
        fcs_ref[:, g * FN_GROUP_W:(g + 1) * FN_GROUP_W] = cs[:, :FN_GROUP_W].astype(BF16)
        fcs_ref[:, FN_WIDTH + g * FN_GROUP_W:FN_WIDTH + (g + 1) * FN_GROUP_W] = (
            cs[:, FN_GROUP_W:].astype(BF16))

    sga_ref[...] = _sigmoid(_dot(hb, win_ref[:, C_GA:C_GA + D_MODEL])).astype(BF16)
    sgf_ref[...] = _sigmoid(_dot(hb, win_ref[:, C_GF:C_GF + D_MODEL])).astype(BF16)


def _emit_kv(ckvb, kpe, wkb_ref, wvb_ref, kg_ref, rope_tabs, k_ref, v_ref):
    kg = kg_ref[...]
    v_ref[...] = _dot(ckvb, wvb_ref[...]).astype(BF16)
    kn = _dot(ckvb, wkb_ref[...])
    pe_ss = jnp.sum(kpe * kpe, axis=-1, keepdims=True)
    pe_g = kpe * kg
    if rope_tabs is not None:
        pe_g = _rope(pe_g, *rope_tabs)
    for hd in range(N_HEADS):
        knh = kn[:, hd * HEAD_SLOT:(hd + 1) * HEAD_SLOT]
        ss = jnp.sum(knh * knh, axis=-1, keepdims=True) + pe_ss
        r = lax.rsqrt(ss * (1.0 / QK_HEAD) + EPS)
        k_ref[:, hd * HEAD_SLOT:(hd + 1) * HEAD_SLOT] = ((knh * kg + pe_g) * r).astype(BF16)


def _const_spec(shape):
    return pl.BlockSpec(shape, lambda i: (0,) * len(shape))


def _inproj(x2d, mod3, mod_row_fn, wts, rope_tabs, emit_cache):
    n = x2d.shape[0]
    rope = rope_tabs is not None
    tiles_per_seq = None if not rope else rope_tabs[0].shape[0] // TM
    in_specs = [pl.BlockSpec((TM, D_MODEL), lambda i: (i, 0)),
                pl.BlockSpec((1, 6, D_MODEL), lambda i: (mod_row_fn(i), 0, 0)),
                _const_spec((1, D_MODEL)),
                _const_spec((D_MODEL, IN_PACKED)),
                _const_spec((1, Q_LORA)),
                _const_spec((Q_LORA, QK_WIDTH)),
                _const_spec((1, KV_LORA)),
                _const_spec((KV_LORA, QK_WIDTH)),
                _const_spec((KV_LORA, V_WIDTH)),
                _const_spec((1, HEAD_SLOT)),
                _const_spec((1, HEAD_SLOT)),
                _const_spec((FN_GROUP_W, 2 * FN_GROUP_W))]
    args = [x2d, mod3, wts["g1"], wts["w_in"], wts["qag"], wts["w_qb"], wts["kvg"],
            wts["w_kb"], wts["w_vb"], wts["qg"], wts["kg"], wts["dft_c"]]
    if rope:
        in_specs += [pl.BlockSpec((TM, LANES), lambda i: (i % tiles_per_seq, 0))] * 3
        args += list(rope_tabs)
    out_shape = [jax.ShapeDtypeStruct((n, QK_WIDTH), BF16),
                 jax.ShapeDtypeStruct((n, QK_WIDTH), BF16),
                 jax.ShapeDtypeStruct((n, V_WIDTH), BF16),
                 jax.ShapeDtypeStruct((n, 2 * FN_WIDTH), BF16),
                 jax.ShapeDtypeStruct((n, D_MODEL), BF16),
                 jax.ShapeDtypeStruct((n, D_MODEL), BF16)]
    out_specs = [pl.BlockSpec((TM, s.shape[1]), lambda i: (i, 0)) for s in out_shape]
    if emit_cache:
        out_shape += [jax.ShapeDtypeStruct((n, KV_LORA), F32),
                      jax.ShapeDtypeStruct((n, QK_ROPE), F32)]
        out_specs += [pl.BlockSpec((TM, KV_LORA), lambda i: (i, 0)),
                      pl.BlockSpec((TM, QK_ROPE), lambda i: (i, 0))]
    return pl.pallas_call(
        functools.partial(_inproj_kernel, rope=rope, emit_cache=emit_cache),
        grid=(n // TM,),
        in_specs=in_specs,
        out_specs=out_specs,
        out_shape=out_shape,
        compiler_params=_cparams(("parallel",)),
        name="inproj_lat" if rope else "inproj_ctx",
    )(*args)


def _cache_kv_kernel(ckv_ref, kpe_ref, wkb_ref, wvb_ref, kg_ref, k_ref, v_ref):
    _emit_kv(ckv_ref[...].astype(BF16), kpe_ref[...], wkb_ref, wvb_ref, kg_ref, None, k_ref, v_ref)


def _cache_kv(ckv2d, kpe_slot2d, wts):
    n = ckv2d.shape[0]
    return pl.pallas_call(
        _cache_kv_kernel,
        grid=(n // TM,),
        in_specs=[pl.BlockSpec((TM, KV_LORA), lambda i: (i, 0)),
                  pl.BlockSpec((TM, LANES), lambda i: (i, 0)),
                  _const_spec((KV_LORA, QK_WIDTH)),
                  _const_spec((KV_LORA, V_WIDTH)),
                  _const_spec((1, HEAD_SLOT))],
        out_specs=[pl.BlockSpec((TM, QK_WIDTH), lambda i: (i, 0)),
                   pl.BlockSpec((TM, V_WIDTH), lambda i: (i, 0))],
        out_shape=[jax.ShapeDtypeStruct((n, QK_WIDTH), BF16),
                   jax.ShapeDtypeStruct((n, V_WIDTH), BF16)],
        compiler_params=_cparams(("parallel",)),
        name="cache_kv",
    )(ckv2d, kpe_slot2d, wts["w_kb"], wts["w_vb"], wts["kg"])


def _attn_kernel(*refs, n_kv):
    q_ref = refs[0]
    k_refs = refs[1:1 + n_kv]
    v_refs = refs[1 + n_kv:1 + 2 * n_kv]
    o_ref = refs[1 + 2 * n_kv]
    for hd in range(N_HEADS):
        qh = q_ref[0, :, hd * HEAD_SLOT:(hd + 1) * HEAD_SLOT]
        s = [_dot_nt(qh, k[0, :, hd * HEAD_SLOT:(hd + 1) * HEAD_SLOT]) for k in k_refs]
        m = s[0].max(axis=-1, keepdims=True)
        for sj in s[1:]:
            m = jnp.maximum(m, sj.max(axis=-1, keepdims=True))
        p = [jnp.exp(sj - m) for sj in s]
        l = p[0].sum(axis=-1, keepdims=True)
        for pj in p[1:]:
            l = l + pj.sum(axis=-1, keepdims=True)
        o = _dot(p[0].astype(BF16), v_refs[0][0, :, hd * V_HEAD:(hd + 1) * V_HEAD])
        for pj, v in zip(p[1:], v_refs[1:]):
            o = o + _dot(pj.astype(BF16), v[0, :, hd * V_HEAD:(hd + 1) * V_HEAD])
        o_ref[0, :, hd * V_HEAD:(hd + 1) * V_HEAD] = (o / l).astype(BF16)


def _attention(q3, ks, vs, tq, name):
    b, sq, _ = q3.shape
    n_kv = len(ks)
    in_specs = [pl.BlockSpec((1, tq, QK_WIDTH), lambda bi, qi: (bi, qi, 0))]
    in_specs += [pl.BlockSpec((1, k.shape[1], QK_WIDTH), lambda bi, qi: (bi, 0, 0)) for k in ks]
    in_specs += [pl.BlockSpec((1, v.shape[1], V_WIDTH), lambda bi, qi: (bi, 0, 0)) for v in vs]
    return pl.pallas_call(
        functools.partial(_attn_kernel, n_kv=n_kv),
        grid=(b, sq // tq),
        in_specs=in_specs,
        out_specs=pl.BlockSpec((1, tq, V_WIDTH), lambda bi, qi: (bi, qi, 0)),
        out_shape=jax.ShapeDtypeStruct((b, sq, V_WIDTH), BF16),
        compiler_params=_cparams(("parallel", "parallel")),
        name=name,
    )(q3, *ks, *vs)


def _fourier_kernel(cs_ref, ns_ref, f_ref, o_ref):
    xc = f_ref[0, :, :FN_WIDTH]
    xs = f_ref[0, :, FN_WIDTH:]
    o_ref[0] = (_dot(cs_ref[...], xc) + _dot(ns_ref[...], xs)).astype(BF16)


def _fourier(fcs3, cs, ns, tr, name):
    b, s, _ = fcs3.shape
    return pl.pallas_call(
        _fourier_kernel,
        grid=(b, s // tr),
        in_specs=[pl.BlockSpec((tr, s), lambda bi, ri: (ri, 0)),
                  pl.BlockSpec((tr, s), lambda bi, ri: (ri, 0)),
                  pl.BlockSpec((1, s, 2 * FN_WIDTH), lambda bi, ri: (bi, 0, 0))],
        out_specs=pl.BlockSpec((1, tr, FN_WIDTH), lambda bi, ri: (bi, ri, 0)),
        out_shape=jax.ShapeDtypeStruct((b, s, FN_WIDTH), BF16),
        compiler_params=_cparams(("parallel", "parallel")),
        name=name,
    )(cs, ns, fcs3)


def _merge_kernel(x_ref, attn_ref, fm_ref, sga_ref, sgf_ref, mod_ref, wao_ref, wfn_ref, wout_ref,
                  g2_ref, wr_ref, br_ref, x1_ref, h2_ref, comb_ref):
    a = _dot(attn_ref[...], wao_ref[...])
    f = _dot(fm_ref[...], wfn_ref[...])
    u = sga_ref[...].astype(F32) * a + sgf_ref[...].astype(F32) * f
    y = _dot(u.astype(BF16), wout_ref[...])
    x1 = x_ref[...] + mod_ref[0, 2:3, :] * y
    x1_ref[...] = x1
    h2 = (x1 * _rms(x1, D_MODEL) * g2_ref[...]) * (1.0 + mod_ref[0, 4:5, :]) + mod_ref[0, 3:4, :]
    h2_hi = h2.astype(BF16)
    h2_ref[...] = h2_hi
    h2_lo = (h2 - h2_hi.astype(F32)).astype(BF16)

    wr = wr_ref[...]
    wr_hi = wr.astype(BF16)
    wr_lo = (wr - wr_hi.astype(F32)).astype(BF16)
    lt = (_dot_nt(wr_hi, h2_hi) + _dot_nt(wr_lo, h2_hi) + _dot_nt(wr_hi, h2_lo)) + br_ref[...]

    g = [lt[j:j + 1, :] for j in range(N_GROUPS)]
    gmax = functools.reduce(jnp.maximum, g)
    gsum = functools.reduce(lambda p, q: p + q, [jnp.exp(gj - gmax) for gj in g])
    p_top = 1.0 / gsum
    gidx = jnp.full(gmax.shape, N_GROUPS - 1, jnp.int32)
    for j in range(N_GROUPS - 2, -1, -1):
        gidx = jnp.where(g[j] == gmax, j, gidx)

    e = [lt[N_GROUPS + j:N_GROUPS + j + 1, :] for j in range(N_EXPERTS)]
    sel = []
    for j in range(EXPERTS_PER_GROUP):
        v = e[(N_GROUPS - 1) * EXPERTS_PER_GROUP + j]
        for gi in range(N_GROUPS - 2, -1, -1):
            v = jnp.where(gidx == gi, e[gi * EXPERTS_PER_GROUP + j], v)
        sel.append(v)
    m1 = functools.reduce(jnp.maximum, sel)
    i1 = jnp.full(m1.shape, EXPERTS_PER_GROUP - 1, jnp.int32)
    for j in range(EXPERTS_PER_GROUP - 2, -1, -1):
        i1 = jnp.where(sel[j] == m1, j, i1)
    rest = [jnp.where(i1 == j, -jnp.inf, sel[j]) for j in range(EXPERTS_PER_GROUP)]
    m2 = functools.reduce(jnp.maximum, rest)
    i2 = jnp.full(m2.shape, EXPERTS_PER_GROUP - 1, jnp.int32)
    for j in range(EXPERTS_PER_GROUP - 2, -1, -1):
        i2 = jnp.where(rest[j] == m2, j, i2)
    t = jnp.exp(m2 - m1)
    w1 = p_top / (1.0 + t)
    w2 = p_top * t / (1.0 + t)
    id1 = gidx * EXPERTS_PER_GROUP + i1
    id2 = gidx * EXPERTS_PER_GROUP + i2

    row = lax.broadcasted_iota(jnp.int32, (LANES, lt.shape[1]), 0)
    comb_t = jnp.where(row == id1, w1, 0.0) + jnp.where(row == id2, w2, 0.0)
    comb_ref[...] = comb_t.T


def _merge(x2d, attn2d, fm2d, sga, sgf, mod3, mod_row_fn, wts):
    n = x2d.shape[0]
    tok = lambda w: pl.BlockSpec((TM, w), lambda i: (i, 0))
    return pl.pallas_call(
        _merge_kernel,
        grid=(n // TM,),
        in_specs=[tok(D_MODEL), tok(V_WIDTH), tok(FN_WIDTH), tok(D_MODEL), tok(D_MODEL),
                  pl.BlockSpec((1, 6, D_MODEL), lambda i: (mod_row_fn(i), 0, 0)),
                  _const_spec((V_WIDTH, D_MODEL)),
                  _const_spec((FN_WIDTH, D_MODEL)),
                  _const_spec((D_MODEL, D_MODEL)),
                  _const_spec((1, D_MODEL)),
                  _const_spec((ROUTER_ROWS, D_MODEL)),
                  _const_spec((ROUTER_ROWS, 1))],
        out_specs=[tok(D_MODEL), tok(D_MODEL), tok(LANES)],
        out_shape=[jax.ShapeDtypeStruct((n, D_MODEL), F32),
                   jax.ShapeDtypeStruct((n, D_MODEL), BF16),
                   jax.ShapeDtypeStruct((n, LANES), F32)],
        compiler_params=_cparams(("parallel",)),
        name="merge",
    )(x2d, attn2d, fm2d, sga, sgf, mod3, wts["w_ao"], wts["w_fn"], wts["w_out"], wts["g2"],
      wts["w_r"], wts["b_r"])


def _moe_kernel(x1_ref, h2_ref, comb_ref, mod_ref, wg_ref, wu_ref, wd_ref, o_ref, acc_ref):
    e = pl.program_id(1)

    @pl.when(e == 0)
    def _():
        acc_ref[...] = jnp.zeros_like(acc_ref)

    h2 = h2_ref[...]
    a = _dot(h2, wg_ref[0])
    u = _dot(h2, wu_ref[0])
    comb = comb_ref[...]
    lane = lax.broadcasted_iota(jnp.int32, comb.shape, 1)
    c = jnp.sum(jnp.where(lane == e, comb, 0.0), axis=-1, keepdims=True)
    act = (a * _sigmoid(a)) * u * c
    acc_ref[...] += _dot(act.astype(BF16), wd_ref[0])

    @pl.when(e == N_EXPERTS - 1)
    def _():
        o_ref[...] = x1_ref[...] + mod_ref[0, 5:6, :] * acc_ref[...]


def _moe(x1, h2, comb, mod3, mod_row_fn, w_gate, w_up, w_down):
    n = x1.shape[0]
    tm = TM_MOE
    tok = lambda w: pl.BlockSpec((tm, w), lambda i, e: (i, 0))
    return pl.pallas_call(
        _moe_kernel,
        grid=(n // tm, N_EXPERTS),
        in_specs=[tok(D_MODEL), tok(D_MODEL), tok(LANES),
                  pl.BlockSpec((1, 6, D_MODEL), lambda i, e: (mod_row_fn(i * (tm // TM)), 0, 0)),
                  pl.BlockSpec((1, D_MODEL, D_EXPERT), lambda i, e: (e, 0, 0)),
                  pl.BlockSpec((1, D_MODEL, D_EXPERT), lambda i, e: (e, 0, 0)),
                  pl.BlockSpec((1, D_EXPERT, D_MODEL), lambda i, e: (e, 0, 0))],
        out_specs=tok(D_MODEL),
        out_shape=jax.ShapeDtypeStruct((n, D_MODEL), F32),
        scratch_shapes=[pltpu.VMEM((tm, D_MODEL), F32)],
        compiler_params=_cparams(("parallel", "arbitrary")),
        name="moe",
    )(x1, h2, comb, mod3, w_gate, w_up, w_down)


def _pack_weights(l, w_in, norm1_g, q_a_norm_g, w_q_b, kv_a_norm_g, w_kv_b, q_norm_g, k_norm_g,
                  w_attn_o, w_fnet, w_out, norm2_g, w_router_group, b_router_group,
                  w_router_expert, b_router_expert):
    wi = w_in[l]
    kpe_cols = jnp.pad(wi[:, C_KPE:C_KPE + QK_ROPE], ((0, 0), (QK_NOPE, LANES - QK_HEAD)))
    w_in_p = jnp.concatenate([wi[:, :C_KPE], kpe_cols, wi[:, C_KPE + QK_ROPE:]], axis=1)
    w_qb = jnp.pad(w_q_b[l].reshape(Q_LORA, N_HEADS, QK_HEAD),
                   ((0, 0), (0, 0), (0, HEAD_SLOT - QK_HEAD))).reshape(Q_LORA, QK_WIDTH)
    wkv = w_kv_b[l].reshape(KV_LORA, N_HEADS, QK_NOPE + V_HEAD)
    w_kb = jnp.pad(wkv[:, :, :QK_NOPE],
                   ((0, 0), (0, 0), (0, HEAD_SLOT - QK_NOPE))).reshape(KV_LORA, QK_WIDTH)
    w_vb = wkv[:, :, QK_NOPE:].reshape(KV_LORA, V_WIDTH)
    pad_g = lambda g: jnp.pad(g, (0, HEAD_SLOT - QK_HEAD)).reshape(1, HEAD_SLOT)
    w_r = jnp.concatenate([w_router_group[l].T, w_router_expert[l].T,
                           jnp.zeros((ROUTER_ROWS - N_GROUPS - N_EXPERTS, D_MODEL), F32)], axis=0)
    b_r = jnp.concatenate([b_router_group[l], b_router_expert[l],
                           jnp.zeros((ROUTER_ROWS - N_GROUPS - N_EXPERTS,), F32)]).reshape(ROUTER_ROWS, 1)
    dft_c, dft_ns = _dft_tables(FN_GROUP_W)
    return {
        "g1": norm1_g[l].reshape(1, D_MODEL),
        "w_in": w_in_p.astype(BF16),
        "qag": q_a_norm_g[l].reshape(1, Q_LORA),
        "w_qb": w_qb.astype(BF16),
        "kvg": kv_a_norm_g[l].reshape(1, KV_LORA),
        "w_kb": w_kb.astype(BF16),
        "w_vb": w_vb.astype(BF16),
        "qg": pad_g(q_norm_g[l]),
        "kg": pad_g(k_norm_g[l]),
        "dft_c": jnp.concatenate([jnp.asarray(dft_c), -jnp.asarray(dft_ns)], axis=1).astype(BF16),
        "w_ao": w_attn_o[l].astype(BF16),
        "w_fn": w_fnet[l].astype(BF16),
        "w_out": w_out[l].astype(BF16),
        "g2": norm2_g[l].reshape(1, D_MODEL),
        "w_r": w_r,
        "b_r": b_r,
    }


def _layer(xp, xs, cache_ckv_l, cache_kpe_l, mod3, wts, w_gate, w_up, w_down):
    bp, sp, _ = xp.shape
    bs, ss, _ = xs.shape
    past = cache_ckv_l.shape[1]
    ctx_row = lambda i: 0
    lat_row = lambda i: 1 + i // (ss // TM)

    xp2 = xp.reshape(bp * sp, D_MODEL)
    q, k, v, fcs, sga, sgf, ckv, kpe = _inproj(xp2, mod3, ctx_row, wts, None, True)
    attn = _attention(q.reshape(bp, sp, QK_WIDTH), [k.reshape(bp, sp, QK_WIDTH)],
                      [v.reshape(bp, sp, V_WIDTH)], sp, "attn_ctx")
    cs, ns = (jnp.asarray(t).astype(BF16) for t in _dft_tables(sp))
    fm = _fourier(fcs.reshape(bp, sp, 2 * FN_WIDTH), cs, ns, sp, "fourier_ctx")
    x1, h2, comb = _merge(xp2, attn.reshape(bp * sp, V_WIDTH), fm.reshape(bp * sp, FN_WIDTH),
                          sga, sgf, mod3, ctx_row, wts)
    yp = _moe(x1, h2, comb, mod3, ctx_row, w_gate, w_up, w_down).reshape(bp, sp, D_MODEL)

    xs2 = xs.reshape(bs * ss, D_MODEL)
    rope_tabs = tuple(jnp.asarray(t) for t in _rope_tables(ss))
    q, k, v, fcs, sga, sgf = _inproj(xs2, mod3, lat_row, wts, rope_tabs, False)
    kpe_slot = jnp.pad(cache_kpe_l, ((0, 0), (0, 0), (QK_NOPE, LANES - QK_HEAD)))
    kc, vc = _cache_kv(cache_ckv_l.reshape(bs * past, KV_LORA), kpe_slot.reshape(bs * past, LANES), wts)
    attn = _attention(q.reshape(bs, ss, QK_WIDTH),
                      [kc.reshape(bs, past, QK_WIDTH), k.reshape(bs, ss, QK_WIDTH)],
                      [vc.reshape(bs, past, V_WIDTH), v.reshape(bs, ss, V_WIDTH)], TM, "attn_lat")
    cs, ns = (jnp.asarray(t).astype(BF16) for t in _dft_tables(ss))
    fm = _fourier(fcs.reshape(bs, ss, 2 * FN_WIDTH), cs, ns, TM, "fourier_lat")
    x1, h2, comb = _merge(xs2, attn.reshape(bs * ss, V_WIDTH), fm.reshape(bs * ss, FN_WIDTH),
                          sga, sgf, mod3, lat_row, wts)
    ys = _moe(x1, h2, comb, mod3, lat_row, w_gate, w_up, w_down).reshape(bs, ss, D_MODEL)

    return yp, ys, ckv.reshape(bp, sp, KV_LORA), kpe.reshape(bp, sp, QK_ROPE)


def kernel(x_prompt, x_sample, cache_ckv, cache_kpe, c, c_ctx, w_mod, b_mod, norm1_g, w_in, q_a_norm_g, w_q_b, kv_a_norm_g, w_kv_b, q_norm_g, k_norm_g, w_attn_o, w_fnet, w_out, norm2_g, w_router_group, b_router_group, w_router_expert, b_router_expert, w_exp_gate, w_exp_up, w_exp_down):
    depth = w_mod.shape[0]
    n_lat = c.shape[0]
    assert 1 + n_lat <= MOD_ROWS
    cond8 = jnp.concatenate([c_ctx[None, :], c, jnp.zeros((MOD_ROWS - 1 - n_lat, D_MODEL), F32)], axis=0)
    xp, xs = x_prompt, x_sample
    ckv_layers, kpe_layers = [], []
    for l in range(depth):
        mod3 = _adaln(cond8, w_mod[l], b_mod[l].reshape(1, -1)).reshape(MOD_ROWS, 6, D_MODEL)
        wts = _pack_weights(l, w_in, norm1_g, q_a_norm_g, w_q_b, kv_a_norm_g, w_kv_b, q_norm_g,
                            k_norm_g, w_attn_o, w_fnet, w_out, norm2_g, w_router_group,
                            b_router_group, w_router_expert, b_router_expert)
        xp, xs, ckv, kpe = _layer(xp, xs, cache_ckv[:, l], cache_kpe[:, l], mod3, wts,
                                  w_exp_gate[l].astype(BF16), w_exp_up[l].astype(BF16),
                                  w_exp_down[l].astype(BF16))
        ckv_layers.append(ckv)
        kpe_layers.append(kpe)
    return xp, xs, jnp.stack(ckv_layers, axis=1), jnp.stack(kpe_layers, axis=1)
```

```python
import functools
import math

import numpy as np
import jax
import jax.numpy as jnp
from jax import lax
from jax.experimental import pallas as pl
from jax.experimental.pallas import tpu as pltpu

D_MODEL = 1024
GRID_W = 64
N_HEADS = 8
Q_LORA = 512
KV_LORA = 256
QK_NOPE = 64
QK_ROPE = 32
V_HEAD = 64
QK_HEAD = QK_NOPE + QK_ROPE
ATTN_SCALE = QK_HEAD ** -0.5
ROPE_BASE = 10000.0
FN_GROUPS = 4
FN_GROUP_W = 128
FN_WIDTH = FN_GROUPS * FN_GROUP_W
N_GROUPS = 4
EXPERTS_PER_GROUP = 4
N_EXPERTS = N_GROUPS * EXPERTS_PER_GROUP
D_EXPERT = 512
EPS = 1e-6

LANES = 128
HEAD_SLOT = LANES
QK_WIDTH = N_HEADS * HEAD_SLOT
V_WIDTH = N_HEADS * V_HEAD
C_QA = 0
C_KVA = C_QA + Q_LORA
C_KPE = C_KVA + KV_LORA
C_FN = C_KPE + LANES
C_GA = C_FN + FN_WIDTH
C_GF = C_GA + D_MODEL
IN_PACKED = C_GF + D_MODEL
ROUTER_ROWS = 32
MOD_ROWS = 8
VMEM_LIMIT = 56 * 1024 * 1024

TM = 256
SEG_ROWS = 8
X1E_WIDTH = D_MODEL + LANES
MOE_TILE_CTX = 256
MOE_TILE_LAT = 128

BF16 = jnp.bfloat16
F32 = jnp.float32


def _cparams(sem):
    return pltpu.CompilerParams(dimension_semantics=sem, vmem_limit_bytes=VMEM_LIMIT)


def _dot(a, b):
    return jnp.dot(a, b, preferred_element_type=F32)


def _dot_nt(a, b):
    return lax.dot_general(a, b, (((1,), (1,)), ((), ())), preferred_element_type=F32)


def _sigmoid(x):
    return 1.0 / (1.0 + jnp.exp(-x))


@functools.lru_cache(maxsize=None)
def _rope_tables(n_pos):
    half = QK_ROPE // 2
    quarter = half // 2
    freqs = ROPE_BASE ** (-np.arange(quarter, dtype=np.float64) / quarter)
    pos = np.arange(n_pos)
    row = (pos // GRID_W).astype(np.float64)
    col = (pos % GRID_W).astype(np.float64)
    cos_t = np.ones((n_pos, LANES), np.float64)
    sin_a = np.zeros((n_pos, LANES), np.float64)
    sin_b = np.zeros((n_pos, LANES), np.float64)
    for base, p in ((QK_NOPE, row), (QK_NOPE + half, col)):
        ang = p[:, None] * freqs[None, :]
        cos_t[:, base:base + quarter] = np.cos(ang)
        cos_t[:, base + quarter:base + half] = np.cos(ang)
        sin_a[:, base:base + quarter] = -np.sin(ang)
        sin_b[:, base + quarter:base + half] = np.sin(ang)
    return (cos_t.astype(np.float32), sin_a.astype(np.float32), sin_b.astype(np.float32))


@functools.lru_cache(maxsize=None)
def _dft_tables(n):
    k = np.arange(n)
    ang = 2.0 * np.pi * ((k[:, None] * k[None, :]) % n) / n
    s = 1.0 / math.sqrt(n)
    return (np.cos(ang) * s).astype(np.float32), (-np.sin(ang) * s).astype(np.float32)


def _adaln_kernel(cond_ref, w_ref, b_ref, o_ref):
    c = cond_ref[...]
    s = c * _sigmoid(c)
    o_ref[...] = jnp.dot(s, w_ref[...], preferred_element_type=F32,
                         precision=lax.Precision.HIGHEST) + b_ref[...]


def _adaln(cond8, w_mod, b_mod):
    n = w_mod.shape[1]
    tn = D_MODEL
    return pl.pallas_call(
        _adaln_kernel,
        grid=(n // tn,),
        in_specs=[pl.BlockSpec((MOD_ROWS, D_MODEL), lambda j: (0, 0)),
                  pl.BlockSpec((D_MODEL, tn), lambda j: (0, j)),
                  pl.BlockSpec((1, tn), lambda j: (0, j))],
        out_specs=pl.BlockSpec((MOD_ROWS, tn), lambda j: (0, j)),
        out_shape=jax.ShapeDtypeStruct((MOD_ROWS, n), F32),
        compiler_params=_cparams(("arbitrary",)),
        name="adaln",
    )(cond8, w_mod, b_mod)


def _rms(x, width):
    return lax.rsqrt(jnp.sum(x * x, axis=-1, keepdims=True) * (1.0 / width) + EPS)


def _rope(x, cos_t, sin_a, sin_b):
    return x * cos_t + pltpu.roll(x, LANES - 8, 1) * sin_a + pltpu.roll(x, 8, 1) * sin_b


def _inproj_kernel(*refs, rope, emit_cache):
    it = iter(refs)
    x_ref, mod_ref, g1_ref, win_ref, qag_ref, wqb_ref, kvg_ref, wkb_ref, wvb_ref = (
        next(it) for _ in range(9))
    qg_ref, kg_ref, dft_ref = next(it), next(it), next(it)
    if rope:
        cos_ref, sa_ref, sb_ref = next(it), next(it), next(it)
    q_ref, k_ref, v_ref, fcs_ref, sga_ref, sgf_ref = (next(it) for _ in range(6))
    if emit_cache:
        ckv_ref, kpe_ref = next(it), next(it)

    x = x_ref[...]
    shift = mod_ref[0, 0:1, :]
    scale = mod_ref[0, 1:2, :]
    h = (x * _rms(x, D_MODEL) * g1_ref[...]) * (1.0 + scale) + shift
    hb = h.astype(BF16)

    if rope:
        cos_t, sin_a, sin_b = cos_ref[...], sa_ref[...], sb_ref[...]

    qa = _dot(hb, win_ref[:, C_QA:C_QA + Q_LORA])
    qn = (qa * _rms(qa, Q_LORA) * qag_ref[...]).astype(BF16)
    q = _dot(qn, wqb_ref[...])
    qg = qg_ref[...] * ATTN_SCALE
    for hd in range(N_HEADS):
        qh = q[:, hd * HEAD_SLOT:(hd + 1) * HEAD_SLOT]
        qh = qh * _rms(qh, QK_HEAD) * qg
        if rope:
            qh = _rope(qh, cos_t, sin_a, sin_b)
        q_ref[:, hd * HEAD_SLOT:(hd + 1) * HEAD_SLOT] = qh.astype(BF16)

    kva = _dot(hb, win_ref[:, C_KVA:C_KVA + KV_LORA])
    ckv = kva * _rms(kva, KV_LORA) * kvg_ref[...]
    kpe = _dot(hb, win_ref[:, C_KPE:C_KPE + LANES])
    if emit_cache:
        ckv_ref[...] = ckv
        kpe_ref[...] = kpe[:, QK_NOPE:QK_NOPE + QK_ROPE]
    _emit_kv(ckv.astype(BF16), kpe, wkb_ref, wvb_ref, kg_ref,
             (cos_t, sin_a, sin_b) if rope else None, k_ref, v_ref)

    fn = _dot(hb, win_ref[:, C_FN:C_FN + FN_WIDTH]).astype(BF16)
    for g in range(FN_GROUPS):
        cs = _dot(fn[:, g * FN_GROUP_W:(g + 1) * FN_GROUP_W], dft_ref[...])
        fcs_ref[:, g * FN_GROUP_W:(g + 1) * FN_GROUP_W] = cs[:, :FN_GROUP_W].astype(BF16)
        fcs_ref[:, FN_WIDTH + g * FN_GROUP_W:FN_WIDTH + (g + 1) * FN_GROUP_W] = (
            cs[:, FN_GROUP_W:].astype(BF16))

    sga_ref[...] = _sigmoid(_dot(hb, win_ref[:, C_GA:C_GA + D_MODEL])).astype(BF16)
    sgf_ref[...] = _sigmoid(_dot(hb, win_ref[:, C_GF:C_GF + D_MODEL])).astype(BF16)


def _emit_kv(ckvb, kpe, wkb_ref, wvb_ref, kg_ref, rope_tabs, k_ref, v_ref):
    kg = kg_ref[...]
    v_ref[...] = _dot(ckvb, wvb_ref[...]).astype(BF16)
    kn = _dot(ckvb, wkb_ref[...])
    pe_ss = jnp.sum(kpe * kpe, axis=-1, keepdims=True)
    pe_g = kpe * kg
    if rope_tabs is not None:
        pe_g = _rope(pe_g, *rope_tabs)
    for hd in range(N_HEADS):
        knh = kn[:, hd * HEAD_SLOT:(hd + 1) * HEAD_SLOT]
        ss = jnp.sum(knh * knh, axis=-1, keepdims=True) + pe_ss
        r = lax.rsqrt(ss * (1.0 / QK_HEAD) + EPS)
        k_ref[:, hd * HEAD_SLOT:(hd + 1) * HEAD_SLOT] = ((knh * kg + pe_g) * r).astype(BF16)


def _const_spec(shape):
    return pl.BlockSpec(shape, lambda i: (0,) * len(shape))


def _inproj(x2d, mod3, mod_row_fn, wts, rope_tabs, emit_cache):
    n = x2d.shape[0]
    rope = rope_tabs is not None
    tiles_per_seq = None if not rope else rope_tabs[0].shape[0] // TM
    in_specs = [pl.BlockSpec((TM, D_MODEL), lambda i: (i, 0)),
                pl.BlockSpec((1, 6, D_MODEL), lambda i: (mod_row_fn(i), 0, 0)),
                _const_spec((1, D_MODEL)),
                _const_spec((D_MODEL, IN_PACKED)),
                _const_spec((1, Q_LORA)),
                _const_spec((Q_LORA, QK_WIDTH)),
                _const_spec((1, KV_LORA)),
                _const_spec((KV_LORA, QK_WIDTH)),
                _const_spec((KV_LORA, V_WIDTH)),
                _const_spec((1, HEAD_SLOT)),
                _const_spec((1, HEAD_SLOT)),
                _const_spec((FN_GROUP_W, 2 * FN_GROUP_W))]
    args = [x2d, mod3, wts["g1"], wts["w_in"], wts["qag"], wts["w_qb"], wts["kvg"],
            wts["w_kb"], wts["w_vb"], wts["qg"], wts["kg"], wts["dft_c"]]
    if rope:
        in_specs += [pl.BlockSpec((TM, LANES), lambda i: (i % tiles_per_seq, 0))] * 3
        args += list(rope_tabs)
    out_shape = [jax.ShapeDtypeStruct((n, QK_WIDTH), BF16),
                 jax.ShapeDtypeStruct((n, QK_WIDTH), BF16),
                 jax.ShapeDtypeStruct((n, V_WIDTH), BF16),
                 jax.ShapeDtypeStruct((n, 2 * FN_WIDTH), BF16),
                 jax.ShapeDtypeStruct((n, D_MODEL), BF16),
                 jax.ShapeDtypeStruct((n, D_MODEL), BF16)]
    out_specs = [pl.BlockSpec((TM, s.shape[1]), lambda i: (i, 0)) for s in out_shape]
    if emit_cache:
        out_shape += [jax.ShapeDtypeStruct((n, KV_LORA), F32),
                      jax.ShapeDtypeStruct((n, QK_ROPE), F32)]
        out_specs += [pl.BlockSpec((TM, KV_LORA), lambda i: (i, 0)),
                      pl.BlockSpec((TM, QK_ROPE), lambda i: (i, 0))]
    return pl.pallas_call(
        functools.partial(_inproj_kernel, rope=rope, emit_cache=emit_cache),
        grid=(n // TM,),
        in_specs=in_specs,
        out_specs=out_specs,
        out_shape=out_shape,
        compiler_params=_cparams(("parallel",)),
        name="inproj_lat" if rope else "inproj_ctx",
    )(*args)


def _cache_kv_kernel(ckv_ref, kpe_ref, wkb_ref, wvb_ref, kg_ref, k_ref, v_ref):
    _emit_kv(ckv_ref[...].astype(BF16), kpe_ref[...], wkb_ref, wvb_ref, kg_ref, None, k_ref, v_ref)


def _cache_kv(ckv2d, kpe_slot2d, wts):
    n = ckv2d.shape[0]
    return pl.pallas_call(
        _cache_kv_kernel,
        grid=(n // TM,),
        in_specs=[pl.BlockSpec((TM, KV_LORA), lambda i: (i, 0)),
                  pl.BlockSpec((TM, LANES), lambda i: (i, 0)),
                  _const_spec((KV_LORA, QK_WIDTH)),
                  _const_spec((KV_LORA, V_WIDTH)),
                  _const_spec((1, HEAD_SLOT))],
        out_specs=[pl.BlockSpec((TM, QK_WIDTH), lambda i: (i, 0)),
                   pl.BlockSpec((TM, V_WIDTH), lambda i: (i, 0))],
        out_shape=[jax.ShapeDtypeStruct((n, QK_WIDTH), BF16),
                   jax.ShapeDtypeStruct((n, V_WIDTH), BF16)],
        compiler_params=_cparams(("parallel",)),
        name="cache_kv",
    )(ckv2d, kpe_slot2d, wts["w_kb"], wts["w_vb"], wts["kg"])


def _attn_kernel(*refs, n_kv):
    q_ref = refs[0]
    k_refs = refs[1:1 + n_kv]
    v_refs = refs[1 + n_kv:1 + 2 * n_kv]
    o_ref = refs[1 + 2 * n_kv]
    for hd in range(N_HEADS):
        qh = q_ref[0, :, hd * HEAD_SLOT:(hd + 1) * HEAD_SLOT]
        s = [_dot_nt(qh, k[0, :, hd * HEAD_SLOT:(hd + 1) * HEAD_SLOT]) for k in k_refs]
        m = s[0].max(axis=-1, keepdims=True)
        for sj in s[1:]:
            m = jnp.maximum(m, sj.max(axis=-1, keepdims=True))
        p = [jnp.exp(sj - m) for sj in s]
        l = p[0].sum(axis=-1, keepdims=True)
        for pj in p[1:]:
            l = l + pj.sum(axis=-1, keepdims=True)
        o = _dot(p[0].astype(BF16), v_refs[0][0, :, hd * V_HEAD:(hd + 1) * V_HEAD])
        for pj, v in zip(p[1:], v_refs[1:]):
            o = o + _dot(pj.astype(BF16), v[0, :, hd * V_HEAD:(hd + 1) * V_HEAD])
        o_ref[0, :, hd * V_HEAD:(hd + 1) * V_HEAD] = (o / l).astype(BF16)


def _attention(q3, ks, vs, tq, name):
    b, sq, _ = q3.shape
    n_kv = len(ks)
    in_specs = [pl.BlockSpec((1, tq, QK_WIDTH), lambda bi, qi: (bi, qi, 0))]
    in_specs += [pl.BlockSpec((1, k.shape[1], QK_WIDTH), lambda bi, qi: (bi, 0, 0)) for k in ks]
    in_specs += [pl.BlockSpec((1, v.shape[1], V_WIDTH), lambda bi, qi: (bi, 0, 0)) for v in vs]
    return pl.pallas_call(
        functools.partial(_attn_kernel, n_kv=n_kv),
        grid=(b, sq // tq),
        in_specs=in_specs,
        out_specs=pl.BlockSpec((1, tq, V_WIDTH), lambda bi, qi: (bi, qi, 0)),
        out_shape=jax.ShapeDtypeStruct((b, sq, V_WIDTH), BF16),
        compiler_params=_cparams(("parallel", "parallel")),
        name=name,
    )(q3, *ks, *vs)


def _fourier_kernel(cs_ref, ns_ref, f_ref, o_ref):
    xc = f_ref[0, :, :FN_WIDTH]
    xs = f_ref[0, :, FN_WIDTH:]
    o_ref[0] = (_dot(cs_ref[...], xc) + _dot(ns_ref[...], xs)).astype(BF16)


def _fourier(fcs3, cs, ns, tr, name):
    b, s, _ = fcs3.shape
    return pl.pallas_call(
        _fourier_kernel,
        grid=(b, s // tr),
        in_specs=[pl.BlockSpec((tr, s), lambda bi, ri: (ri, 0)),
                  pl.BlockSpec((tr, s), lambda bi, ri: (ri, 0)),
                  pl.BlockSpec((1, s, 2 * FN_WIDTH), lambda bi, ri: (bi, 0, 0))],
        out_specs=pl.BlockSpec((1, tr, FN_WIDTH), lambda bi, ri: (bi, ri, 0)),
        out_shape=jax.ShapeDtypeStruct((b, s, FN_WIDTH), BF16),
        compiler_params=_cparams(("parallel", "parallel")),
        name=name,
    )(cs, ns, fcs3)


def _modulate2(x1, g2_ref, mod_ref):
    return (x1 * _rms(x1, D_MODEL) * g2_ref[...]) * (1.0 + mod_ref[0, 4:5, :]) + mod_ref[0, 3:4, :]


def _merge_kernel(x_ref, attn_ref, fm_ref, sga_ref, sgf_ref, mod_ref, wao_ref, wfn_ref, wout_ref,
                  g2_ref, wr_ref, br_ref, x1e_ref, seg_ref, rank_ref, cnt_ref, carry_ref, *,
                  n_mod, tiles_per_seq):
    i = pl.program_id(0)

    @pl.when(i == 0)
    def _():
        carry_ref[...] = jnp.zeros_like(carry_ref)

    a = _dot(attn_ref[...], wao_ref[...])
    f = _dot(fm_ref[...], wfn_ref[...])
    u = sga_ref[...].astype(F32) * a + sgf_ref[...].astype(F32) * f
    y = _dot(u.astype(BF16), wout_ref[...])
    x1 = x_ref[...] + mod_ref[0, 2:3, :] * y
    x1e_ref[:, :D_MODEL] = x1
    h2 = _modulate2(x1, g2_ref, mod_ref)
    h2_hi = h2.astype(BF16)
    h2_lo = (h2 - h2_hi.astype(F32)).astype(BF16)

    wr = wr_ref[...]
    wr_hi = wr.astype(BF16)
    wr_lo = (wr - wr_hi.astype(F32)).astype(BF16)
    lt = (_dot_nt(wr_hi, h2_hi) + _dot_nt(wr_lo, h2_hi) + _dot_nt(wr_hi, h2_lo)) + br_ref[...]

    g = [lt[j:j + 1, :] for j in range(N_GROUPS)]
    gmax = functools.reduce(jnp.maximum, g)
    gsum = functools.reduce(lambda p, q: p + q, [jnp.exp(gj - gmax) for gj in g])
    p_top = 1.0 / gsum
    gidx = jnp.full(gmax.shape, N_GROUPS - 1, jnp.int32)
    for j in range(N_GROUPS - 2, -1, -1):
        gidx = jnp.where(g[j] == gmax, j, gidx)

    e = [lt[N_GROUPS + j:N_GROUPS + j + 1, :] for j in range(N_EXPERTS)]
    sel = []
    for j in range(EXPERTS_PER_GROUP):
        v = e[(N_GROUPS - 1) * EXPERTS_PER_GROUP + j]
        for gi in range(N_GROUPS - 2, -1, -1):
            v = jnp.where(gidx == gi, e[gi * EXPERTS_PER_GROUP + j], v)
        sel.append(v)
    m1 = functools.reduce(jnp.maximum, sel)
    i1 = jnp.full(m1.shape, EXPERTS_PER_GROUP - 1, jnp.int32)
    for j in range(EXPERTS_PER_GROUP - 2, -1, -1):
        i1 = jnp.where(sel[j] == m1, j, i1)
    rest = [jnp.where(i1 == j, -jnp.inf, sel[j]) for j in range(EXPERTS_PER_GROUP)]
    m2 = functools.reduce(jnp.maximum, rest)
    i2 = jnp.full(m2.shape, EXPERTS_PER_GROUP - 1, jnp.int32)
    for j in range(EXPERTS_PER_GROUP - 2, -1, -1):
        i2 = jnp.where(rest[j] == m2, j, i2)
    t = jnp.exp(m2 - m1)
    w1 = p_top / (1.0 + t)
    w2 = p_top * t / (1.0 + t)
    id1 = gidx * EXPERTS_PER_GROUP + i1
    id2 = gidx * EXPERTS_PER_GROUP + i2

    row = lax.broadcasted_iota(jnp.int32, (LANES, lt.shape[1]), 0)
    comb_t = jnp.where(row == id1, w1, 0.0) + jnp.where(row == id2, w2, 0.0)
    x1e_ref[:, D_MODEL:] = comb_t.T

    tm = lt.shape[1]
    seg = gidx * n_mod
    if n_mod > 1:
        seg = seg + i // tiles_per_seq
    onehot = jnp.where(lax.broadcasted_iota(jnp.int32, (SEG_ROWS, tm), 0) == seg, 1.0, 0.0)
    before = (lax.broadcasted_iota(jnp.int32, (tm, tm), 0)
              < lax.broadcasted_iota(jnp.int32, (tm, tm), 1))
    prefix = _dot(onehot.astype(BF16), jnp.where(before, 1.0, 0.0).astype(BF16))
    carry = carry_ref[...]
    rank = jnp.sum(onehot * (prefix + carry[:, 0:1]), axis=0, keepdims=True)
    seg_ref[...] = seg
    rank_ref[...] = rank.astype(jnp.int32)
    carry = carry + jnp.sum(onehot, axis=1, keepdims=True)
    carry_ref[...] = carry
    cnt_ref[...] = carry.astype(jnp.int32)


def _merge(x2d, attn2d, fm2d, sga, sgf, mod3, mod_row_fn, wts, n_mod):
    n = x2d.shape[0]
    tok = lambda w: pl.BlockSpec((TM, w), lambda i: (i, 0))
    return pl.pallas_call(
        functools.partial(_merge_kernel, n_mod=n_mod, tiles_per_seq=n // n_mod // TM),
        grid=(n // TM,),
        in_specs=[tok(D_MODEL), tok(V_WIDTH), tok(FN_WIDTH), tok(D_MODEL), tok(D_MODEL),
                  pl.BlockSpec((1, 6, D_MODEL), lambda i: (mod_row_fn(i), 0, 0)),
                  _const_spec((V_WIDTH, D_MODEL)),
                  _const_spec((FN_WIDTH, D_MODEL)),
                  _const_spec((D_MODEL, D_MODEL)),
                  _const_spec((1, D_MODEL)),
                  _const_spec((ROUTER_ROWS, D_MODEL)),
                  _const_spec((ROUTER_ROWS, 1))],
        out_specs=[tok(X1E_WIDTH), pl.BlockSpec((1, TM), lambda i: (0, i)),
                   pl.BlockSpec((1, TM), lambda i: (0, i)), _const_spec((SEG_ROWS, LANES))],
        out_shape=[jax.ShapeDtypeStruct((n, X1E_WIDTH), F32),
                   jax.ShapeDtypeStruct((1, n), jnp.int32),
                   jax.ShapeDtypeStruct((1, n), jnp.int32),
                   jax.ShapeDtypeStruct((SEG_ROWS, LANES), jnp.int32)],
        scratch_shapes=[pltpu.VMEM((SEG_ROWS, LANES), F32)],
        compiler_params=_cparams(("arbitrary",)),
        name="merge",
    )(x2d, attn2d, fm2d, sga, sgf, mod3, wts["w_ao"], wts["w_fn"], wts["w_out"], wts["g2"],
      wts["w_r"], wts["b_r"])


def _row_copy(src_hbm, dst_hbm, src_row, dst_row, sem):
    return pltpu.make_async_copy(src_hbm.at[src_row], dst_hbm.at[dst_row], sem)


def _dispatch_kernel(pos_ref, lo_ref, hi_ref, x_hbm, zrow_hbm, o_hbm, sem, *, n_fill):
    n = x_hbm.shape[0]

    def start(t, c):
        _row_copy(x_hbm, o_hbm, t, pos_ref[t], sem.at[0]).start()
        return c

    def wait(t, c):
        _row_copy(x_hbm, o_hbm, 0, 0, sem.at[0]).wait()
        return c

    def start_zero(r, c):
        _row_copy(zrow_hbm, o_hbm, 0, r, sem.at[1]).start()
        return c

    def wait_zero(r, c):
        _row_copy(zrow_hbm, o_hbm, 0, 0, sem.at[1]).wait()
        return c

    lax.fori_loop(0, n, start, 0, unroll=8)
    for s in range(n_fill):
        lax.fori_loop(lo_ref[s], hi_ref[s], start_zero, 0)
    lax.fori_loop(0, n, wait, 0, unroll=8)
    for s in range(n_fill):
        lax.fori_loop(lo_ref[s], hi_ref[s], wait_zero, 0)


def _dispatch(x1e, pos, fill_lo, fill_hi, rows):
    zeros = jnp.zeros((1, X1E_WIDTH), F32)
    return pl.pallas_call(
        functools.partial(_dispatch_kernel, n_fill=fill_lo.shape[0]),
        grid_spec=pltpu.PrefetchScalarGridSpec(
            num_scalar_prefetch=3,
            grid=(1,),
            in_specs=[pl.BlockSpec(memory_space=pl.ANY), pl.BlockSpec(memory_space=pl.ANY)],
            out_specs=pl.BlockSpec(memory_space=pl.ANY),
            scratch_shapes=[pltpu.SemaphoreType.DMA((2,))]),
        out_shape=jax.ShapeDtypeStruct((rows, X1E_WIDTH), F32),
        compiler_params=_cparams(("arbitrary",)),
        name="moe_dispatch",
    )(pos, fill_lo, fill_hi, x1e, zeros)


def _unsort_kernel(pos_ref, y_hbm, o_hbm, sem):
    n = o_hbm.shape[0]

    def start(t, c):
        _row_copy(y_hbm, o_hbm, pos_ref[t], t, sem.at[0]).start()
        return c

    lax.fori_loop(0, n, start, 0, unroll=8)

    def wait(t, c):
        _row_copy(y_hbm, o_hbm, 0, 0, sem.at[0]).wait()
        return c

    lax.fori_loop(0, n, wait, 0, unroll=8)


def _unsort(y_sorted, pos, n):
    return pl.pallas_call(
        _unsort_kernel,
        grid_spec=pltpu.PrefetchScalarGridSpec(
            num_scalar_prefetch=1,
            grid=(1,),
            in_specs=[pl.BlockSpec(memory_space=pl.ANY)],
            out_specs=pl.BlockSpec(memory_space=pl.ANY),
            scratch_shapes=[pltpu.SemaphoreType.DMA((1,))]),
        out_shape=jax.ShapeDtypeStruct((n, D_MODEL), F32),
        compiler_params=_cparams(("arbitrary",)),
        name="moe_unsort",
    )(pos, y_sorted)


def _moe_kernel(grp_ref, modrow_ref, on_ref, xe_ref, mod_ref, g2_ref, wg_ref, wu_ref, wd_ref, o_ref):
    i = pl.program_id(0)

    @pl.when(on_ref[i] == 0)
    def _():
        o_ref[...] = jnp.zeros_like(o_ref)

    @pl.when(on_ref[i] == 1)
    def _():
        x1 = xe_ref[:, :D_MODEL]
        comb = xe_ref[:, D_MODEL:]
        h2 = _modulate2(x1, g2_ref, mod_ref).astype(BF16)
        lane = lax.broadcasted_iota(jnp.int32, comb.shape, 1)
        first = grp_ref[i] * EXPERTS_PER_GROUP
        acc = None
        for j in range(EXPERTS_PER_GROUP):
            a = _dot(h2, wg_ref[j])
            u = _dot(h2, wu_ref[j])
            c = jnp.sum(jnp.where(lane == first + j, comb, 0.0), axis=-1, keepdims=True)
            act = (a * _sigmoid(a)) * u * c
            y = _dot(act.astype(BF16), wd_ref[j])
            acc = y if acc is None else acc + y
        o_ref[...] = x1 + mod_ref[0, 5:6, :] * acc


def _moe(x_sorted, maps, mod3, g2, w_gate, w_up, w_down, tile):
    rows = x_sorted.shape[0]
    w_spec = lambda shape: pl.BlockSpec((EXPERTS_PER_GROUP,) + shape,
                                        lambda i, grp, mrow, on: (grp[i], 0, 0))
    return pl.pallas_call(
        _moe_kernel,
        grid_spec=pltpu.PrefetchScalarGridSpec(
            num_scalar_prefetch=3,
            grid=(rows // tile,),
            in_specs=[pl.BlockSpec((tile, X1E_WIDTH), lambda i, grp, mrow, on: (i, 0)),
                      pl.BlockSpec((1, 6, D_MODEL), lambda i, grp, mrow, on: (mrow[i], 0, 0)),
                      pl.BlockSpec((1, D_MODEL), lambda i, grp, mrow, on: (0, 0)),
                      w_spec((D_MODEL, D_EXPERT)), w_spec((D_MODEL, D_EXPERT)),
                      w_spec((D_EXPERT, D_MODEL))],
            out_specs=pl.BlockSpec((tile, D_MODEL), lambda i, grp, mrow, on: (i, 0))),
        out_shape=jax.ShapeDtypeStruct((rows, D_MODEL), F32),
        compiler_params=_cparams(("arbitrary",)),
        name="moe",
    )(*maps, x_sorted, mod3, g2, w_gate, w_up, w_down)


def _sort_plan(seg, rank, cnt, n_mod, row_base, tile, n_tiles):
    n_tile_seg = (cnt + tile - 1) // tile
    tile_end = jnp.cumsum(n_tile_seg)
    tile_start = tile_end - n_tile_seg
    total = tile_end[-1]
    pos = (tile_start * tile)[seg] + rank
    i = jnp.arange(n_tiles, dtype=jnp.int32)
    tile_seg = jnp.sum((jnp.minimum(i, total - 1)[:, None] >= tile_end[None, :]).astype(jnp.int32),
                       axis=1)
    maps = ((tile_seg // n_mod).astype(jnp.int32), (row_base + tile_seg % n_mod).astype(jnp.int32),
            (i < total).astype(jnp.int32))
    fill_lo = jnp.concatenate([tile_start * tile + cnt, total[None] * tile]).astype(jnp.int32)
    fill_hi = jnp.concatenate([tile_end * tile, jnp.full((1,), n_tiles * tile)]).astype(jnp.int32)
    return pos.astype(jnp.int32), maps, fill_lo, fill_hi


def _sparse_moe(x1e, seg, rank, cnt8, mod3, g2, w_gate, w_up, w_down, n_mod, row_base, tile):
    n = x1e.shape[0]
    n_seg = N_GROUPS * n_mod
    n_tiles = n // tile + n_seg
    pos, maps, fill_lo, fill_hi = _sort_plan(seg.reshape(n), rank.reshape(n), cnt8[:n_seg, 0],
                                             n_mod, row_base, tile, n_tiles)
    x_sorted = _dispatch(x1e, pos, fill_lo, fill_hi, n_tiles * tile)
    y_sorted = _moe(x_sorted, maps, mod3, g2, w_gate, w_up, w_down, tile)
    return _unsort(y_sorted, pos, n)


def _pack_weights(l, w_in, norm1_g, q_a_norm_g, w_q_b, kv_a_norm_g, w_kv_b, q_norm_g, k_norm_g,
                  w_attn_o, w_fnet, w_out, norm2_g, w_router_group, b_router_group,
                  w_router_expert, b_router_expert):
    wi = w_in[l]
    kpe_cols = jnp.pad(wi[:, C_KPE:C_KPE + QK_ROPE], ((0, 0), (QK_NOPE, LANES - QK_HEAD)))
    w_in_p = jnp.concatenate([wi[:, :C_KPE], kpe_cols, wi[:, C_KPE + QK_ROPE:]], axis=1)
    w_qb = jnp.pad(w_q_b[l].reshape(Q_LORA, N_HEADS, QK_HEAD),
                   ((0, 0), (0, 0), (0, HEAD_SLOT - QK_HEAD))).reshape(Q_LORA, QK_WIDTH)
    wkv = w_kv_b[l].reshape(KV_LORA, N_HEADS, QK_NOPE + V_HEAD)
    w_kb = jnp.pad(wkv[:, :, :QK_NOPE],
                   ((0, 0), (0, 0), (0, HEAD_SLOT - QK_NOPE))).reshape(KV_LORA, QK_WIDTH)
    w_vb = wkv[:, :, QK_NOPE:].reshape(KV_LORA, V_WIDTH)
    pad_g = lambda g: jnp.pad(g, (0, HEAD_SLOT - QK_HEAD)).reshape(1, HEAD_SLOT)
    w_r = jnp.concatenate([w_router_group[l].T, w_router_expert[l].T,
                           jnp.zeros((ROUTER_ROWS - N_GROUPS - N_EXPERTS, D_MODEL), F32)], axis=0)
    b_r = jnp.concatenate([b_router_group[l], b_router_expert[l],
                           jnp.zeros((ROUTER_ROWS - N_GROUPS - N_EXPERTS,), F32)]).reshape(ROUTER_ROWS, 1)
    dft_c, dft_ns = _dft_tables(FN_GROUP_W)
    return {
        "g1": norm1_g[l].reshape(1, D_MODEL),
        "w_in": w_in_p.astype(BF16),
        "qag": q_a_norm_g[l].reshape(1, Q_LORA),
        "w_qb": w_qb.astype(BF16),
        "kvg": kv_a_norm_g[l].reshape(1, KV_LORA),
        "w_kb": w_kb.astype(BF16),
        "w_vb": w_vb.astype(BF16),
        "qg": pad_g(q_norm_g[l]),
        "kg": pad_g(k_norm_g[l]),
        "dft_c": jnp.concatenate([jnp.asarray(dft_c), -jnp.asarray(dft_ns)], axis=1).astype(BF16),
        "w_ao": w_attn_o[l].astype(BF16),
        "w_fn": w_fnet[l].astype(BF16),
        "w_out": w_out[l].astype(BF16),
        "g2": norm2_g[l].reshape(1, D_MODEL),
        "w_r": w_r,
        "b_r": b_r,
    }


def _layer(xp, xs, cache_ckv_l, cache_kpe_l, mod3, wts, w_gate, w_up, w_down):
    bp, sp, _ = xp.shape
    bs, ss, _ = xs.shape
    past = cache_ckv_l.shape[1]
    ctx_row = lambda i: 0
    lat_row = lambda i: 1 + i // (ss // TM)

    xp2 = xp.reshape(bp * sp, D_MODEL)
    q, k, v, fcs, sga, sgf, ckv, kpe = _inproj(xp2, mod3, ctx_row, wts, None, True)
    attn = _attention(q.reshape(bp, sp, QK_WIDTH), [k.reshape(bp, sp, QK_WIDTH)],
                      [v.reshape(bp, sp, V_WIDTH)], sp, "attn_ctx")
    cs, ns = (jnp.asarray(t).astype(BF16) for t in _dft_tables(sp))
    fm = _fourier(fcs.reshape(bp, sp, 2 * FN_WIDTH), cs, ns, sp, "fourier_ctx")
    x1e, seg, rank, cnt = _merge(xp2, attn.reshape(bp * sp, V_WIDTH), fm.reshape(bp * sp, FN_WIDTH),
                                 sga, sgf, mod3, ctx_row, wts, 1)
    yp = _sparse_moe(x1e, seg, rank, cnt, mod3, wts["g2"], w_gate, w_up, w_down, 1, 0,
                     MOE_TILE_CTX).reshape(bp, sp, D_MODEL)

    xs2 = xs.reshape(bs * ss, D_MODEL)
    rope_tabs = tuple(jnp.asarray(t) for t in _rope_tables(ss))
    q, k, v, fcs, sga, sgf = _inproj(xs2, mod3, lat_row, wts, rope_tabs, False)
    kpe_slot = jnp.pad(cache_kpe_l, ((0, 0), (0, 0), (QK_NOPE, LANES - QK_HEAD)))
    kc, vc = _cache_kv(cache_ckv_l.reshape(bs * past, KV_LORA), kpe_slot.reshape(bs * past, LANES), wts)
    attn = _attention(q.reshape(bs, ss, QK_WIDTH),
                      [kc.reshape(bs, past, QK_WIDTH), k.reshape(bs, ss, QK_WIDTH)],
                      [vc.reshape(bs, past, V_WIDTH), v.reshape(bs, ss, V_WIDTH)], TM, "attn_lat")
    cs, ns = (jnp.asarray(t).astype(BF16) for t in _dft_tables(ss))
    fm = _fourier(fcs.reshape(bs, ss, 2 * FN_WIDTH), cs, ns, TM, "fourier_lat")
    assert N_GROUPS * bs <= SEG_ROWS
    x1e, seg, rank, cnt = _merge(xs2, attn.reshape(bs * ss, V_WIDTH), fm.reshape(bs * ss, FN_WIDTH),
                                 sga, sgf, mod3, lat_row, wts, bs)
    ys = _sparse_moe(x1e, seg, rank, cnt, mod3, wts["g2"], w_gate, w_up, w_down, bs, 1,
                     MOE_TILE_LAT).reshape(bs, ss, D_MODEL)

    return yp, ys, ckv.reshape(bp, sp, KV_LORA), kpe.reshape(bp, sp, QK_ROPE)


def kernel(x_prompt, x_sample, cache_ckv, cache_kpe, c, c_ctx, w_mod, b_mod, norm1_g, w_in, q_a_norm_g, w_q_b, kv_a_norm_g, w_kv_b, q_norm_g, k_norm_g, w_attn_o, w_fnet, w_out, norm2_g, w_router_group, b_router_group, w_router_expert, b_router_expert, w_exp_gate, w_exp_up, w_exp_down):
    depth = w_mod.shape[0]
    n_lat = c.shape[0]
    assert 1 + n_lat <= MOD_ROWS
    cond8 = jnp.concatenate([c_ctx[None, :], c, jnp.zeros((MOD_ROWS - 1 - n_lat, D_MODEL), F32)], axis=0)
    xp, xs = x_prompt, x_sample
    ckv_layers, kpe_layers = [], []
    for l in range(depth):
        mod3 = _adaln(cond8, w_mod[l], b_mod[l].reshape(1, -1)).reshape(MOD_ROWS, 6, D_MODEL)
        wts = _pack_weights(l, w_in, norm1_g, q_a_norm_g, w_q_b, kv_a_norm_g, w_kv_b, q_norm_g,
                            k_norm_g, w_attn_o, w_fnet, w_out, norm2_g, w_router_group,
                            b_router_group, w_router_expert, b_router_expert)
        xp, xs, ckv, kpe = _layer(xp, xs, cache_ckv[:, l], cache_kpe[:, l], mod3, wts,
                                  w_exp_gate[l].astype(BF16), w_exp_up[l].astype(BF16),
                                  w_exp_down[l].astype(BF16))
        ckv_layers.append(ckv)
        kpe_layers.append(kpe)
    return xp, xs, jnp.stack(ckv_layers, axis=1), jnp.stack(kpe_layers, axis=1)
```

```python
import functools
import math

import numpy as np
import jax
import jax.numpy as jnp
from jax import lax
from jax.experimental import pallas as pl
from jax.experimental.pallas import tpu as pltpu

D_MODEL = 1024
GRID_W = 64
N_HEADS = 8
Q_LORA = 512
KV_LORA = 256
QK_NOPE = 64
QK_ROPE = 32
V_HEAD = 64
QK_HEAD = QK_NOPE + QK_ROPE
ATTN_SCALE = QK_HEAD ** -0.5
ROPE_BASE = 10000.0
FN_GROUPS = 4
FN_GROUP_W = 128
FN_WIDTH = FN_GROUPS * FN_GROUP_W
N_GROUPS = 4
EXPERTS_PER_GROUP = 4
N_EXPERTS = N_GROUPS * EXPERTS_PER_GROUP
D_EXPERT = 512
EPS = 1e-6

LANES = 128
HEAD_SLOT = LANES
QK_WIDTH = N_HEADS * HEAD_SLOT
V_WIDTH = N_HEADS * V_HEAD
C_QA = 0
C_KVA = C_QA + Q_LORA
C_KPE = C_KVA + KV_LORA
C_FN = C_KPE + LANES
C_GA = C_FN + FN_WIDTH
C_GF = C_GA + D_MODEL
IN_PACKED = C_GF + D_MODEL
SUBLANES = 8
ROUTER_ROWS = SUBLANES
MOD_ROWS = SUBLANES
VMEM_LIMIT = 56 * 1024 * 1024

TM = 256
SEG_ROWS = SUBLANES
TOKEN_ROWS = D_MODEL // LANES
MOE_TILE_CTX = 256
MOE_TILE_LAT = 128

BF16 = jnp.bfloat16
F32 = jnp.float32


def _cparams(sem):
    return pltpu.CompilerParams(dimension_semantics=sem, vmem_limit_bytes=VMEM_LIMIT)


def _dot(a, b):
    return jnp.dot(a, b, preferred_element_type=F32)


def _dot_nt(a, b):
    return lax.dot_general(a, b, (((1,), (1,)), ((), ())), preferred_element_type=F32)


def _sigmoid(x):
    return 1.0 / (1.0 + jnp.exp(-x))


@functools.lru_cache(maxsize=None)
def _rope_tables(n_pos):
    half = QK_ROPE // 2
    quarter = half // 2
    freqs = ROPE_BASE ** (-np.arange(quarter, dtype=np.float64) / quarter)
    pos = np.arange(n_pos)
    row = (pos // GRID_W).astype(np.float64)
    col = (pos % GRID_W).astype(np.float64)
    cos_t = np.ones((n_pos, LANES), np.float64)
    sin_a = np.zeros((n_pos, LANES), np.float64)
    sin_b = np.zeros((n_pos, LANES), np.float64)
    for base, p in ((QK_NOPE, row), (QK_NOPE + half, col)):
        ang = p[:, None] * freqs[None, :]
        cos_t[:, base:base + quarter] = np.cos(ang)
        cos_t[:, base + quarter:base + half] = np.cos(ang)
        sin_a[:, base:base + quarter] = -np.sin(ang)
        sin_b[:, base + quarter:base + half] = np.sin(ang)
    return (cos_t.astype(np.float32), sin_a.astype(np.float32), sin_b.astype(np.float32))


@functools.lru_cache(maxsize=None)
def _dft_tables(n):
    k = np.arange(n)
    ang = 2.0 * np.pi * ((k[:, None] * k[None, :]) % n) / n
    s = 1.0 / math.sqrt(n)
    return (np.cos(ang) * s).astype(np.float32), (-np.sin(ang) * s).astype(np.float32)


def _adaln_kernel(cond_ref, w_ref, b_ref, o_ref):
    c = cond_ref[...]
    s = c * _sigmoid(c)
    o_ref[...] = jnp.dot(s, w_ref[...], preferred_element_type=F32,
                         precision=lax.Precision.HIGHEST) + b_ref[...]


def _adaln(cond8, w_mod, b_mod):
    n = w_mod.shape[1]
    tn = D_MODEL
    return pl.pallas_call(
        _adaln_kernel,
        grid=(n // tn,),
        in_specs=[pl.BlockSpec((MOD_ROWS, D_MODEL), lambda j: (0, 0)),
                  pl.BlockSpec((D_MODEL, tn), lambda j: (0, j)),
                  pl.BlockSpec((1, tn), lambda j: (0, j))],
        out_specs=pl.BlockSpec((MOD_ROWS, tn), lambda j: (0, j)),
        out_shape=jax.ShapeDtypeStruct((MOD_ROWS, n), F32),
        compiler_params=_cparams(("arbitrary",)),
        name="adaln",
    )(cond8, w_mod, b_mod)


def _rms(x, width):
    return lax.rsqrt(jnp.sum(x * x, axis=-1, keepdims=True) * (1.0 / width) + EPS)


def _rope(x, cos_t, sin_a, sin_b):
    return x * cos_t + pltpu.roll(x, LANES - 8, 1) * sin_a + pltpu.roll(x, 8, 1) * sin_b


def _inproj_kernel(*refs, rope, emit_cache):
    it = iter(refs)
    x_ref, mod_ref, g1_ref, win_ref, qag_ref, wqb_ref, kvg_ref, wkb_ref, wvb_ref = (
        next(it) for _ in range(9))
    qg_ref, kg_ref, dft_ref = next(it), next(it), next(it)
    if rope:
        cos_ref, sa_ref, sb_ref = next(it), next(it), next(it)
    q_ref, k_ref, v_ref, fcs_ref, sga_ref, sgf_ref = (next(it) for _ in range(6))
    if emit_cache:
        ckv_ref, kpe_ref = next(it), next(it)

    x = x_ref[...]
    shift = mod_ref[0, 0:1, :]
    scale = mod_ref[0, 1:2, :]
    h = (x * _rms(x, D_MODEL) * g1_ref[...]) * (1.0 + scale) + shift
    hb = h.astype(BF16)

    if rope:
        cos_t, sin_a, sin_b = cos_ref[...], sa_ref[...], sb_ref[...]

    qa = _dot(hb, win_ref[:, C_QA:C_QA + Q_LORA])
    qn = (qa * _rms(qa, Q_LORA) * qag_ref[...]).astype(BF16)
    q = _dot(qn, wqb_ref[...])
    qg = qg_ref[...] * ATTN_SCALE
    for hd in range(N_HEADS):
        qh = q[:, hd * HEAD_SLOT:(hd + 1) * HEAD_SLOT]
        qh = qh * _rms(qh, QK_HEAD) * qg
        if rope:
            qh = _rope(qh, cos_t, sin_a, sin_b)
        q_ref[:, hd * HEAD_SLOT:(hd + 1) * HEAD_SLOT] = qh.astype(BF16)

    kva = _dot(hb, win_ref[:, C_KVA:C_KVA + KV_LORA])
    ckv = kva * _rms(kva, KV_LORA) * kvg_ref[...]
    kpe = _dot(hb, win_ref[:, C_KPE:C_KPE + LANES])
    if emit_cache:
        ckv_ref[...] = ckv
        kpe_ref[...] = kpe[:, QK_NOPE:QK_NOPE + QK_ROPE]
    _emit_kv(ckv.astype(BF16), kpe, wkb_ref, wvb_ref, kg_ref,
             (cos_t, sin_a, sin_b) if rope else None, k_ref, v_ref)

    fn = _dot(hb, win_ref[:, C_FN:C_FN + FN_WIDTH]).astype(BF16)
    for g in range(FN_GROUPS):
        cs = _dot(fn[:, g * FN_GROUP_W:(g + 1) * FN_GROUP_W], dft_ref[...])
        fcs_ref[:, g * FN_GROUP_W:(g + 1) * FN_GROUP_W] = cs[:, :FN_GROUP_W].astype(BF16)
        fcs_ref[:, FN_WIDTH + g * FN_GROUP_W:FN_WIDTH + (g + 1) * FN_GROUP_W] = (
            cs[:, FN_GROUP_W:].astype(BF16))

    sga_ref[...] = _sigmoid(_dot(hb, win_ref[:, C_GA:C_GA + D_MODEL])).astype(BF16)
    sgf_ref[...] = _sigmoid(_dot(hb, win_ref[:, C_GF:C_GF + D_MODEL])).astype(BF16)


def _emit_kv(ckvb, kpe, wkb_ref, wvb_ref, kg_ref, rope_tabs, k_ref, v_ref):
    kg = kg_ref[...]
    v_ref[...] = _dot(ckvb, wvb_ref[...]).astype(BF16)
    kn = _dot(ckvb, wkb_ref[...])
    pe_ss = jnp.sum(kpe * kpe, axis=-1, keepdims=True)
    pe_g = kpe * kg
    if rope_tabs is not None:
        pe_g = _rope(pe_g, *rope_tabs)
    for hd in range(N_HEADS):
        knh = kn[:, hd * HEAD_SLOT:(hd + 1) * HEAD_SLOT]
        ss = jnp.sum(knh * knh, axis=-1, keepdims=True) + pe_ss
        r = lax.rsqrt(ss * (1.0 / QK_HEAD) + EPS)
        k_ref[:, hd * HEAD_SLOT:(hd + 1) * HEAD_SLOT] = ((knh * kg + pe_g) * r).astype(BF16)


def _const_spec(shape):
    return pl.BlockSpec(shape, lambda i: (0,) * len(shape))


def _inproj(x2d, mod3, mod_row_fn, wts, rope_tabs, emit_cache):
    n = x2d.shape[0]
    rope = rope_tabs is not None
    tiles_per_seq = None if not rope else rope_tabs[0].shape[0] // TM
    in_specs = [pl.BlockSpec((TM, D_MODEL), lambda i: (i, 0)),
                pl.BlockSpec((1, 6, D_MODEL), lambda i: (mod_row_fn(i), 0, 0)),
                _const_spec((1, D_MODEL)),
                _const_spec((D_MODEL, IN_PACKED)),
                _const_spec((1, Q_LORA)),
                _const_spec((Q_LORA, QK_WIDTH)),
                _const_spec((1, KV_LORA)),
                _const_spec((KV_LORA, QK_WIDTH)),
                _const_spec((KV_LORA, V_WIDTH)),
                _const_spec((1, HEAD_SLOT)),
                _const_spec((1, HEAD_SLOT)),
                _const_spec((FN_GROUP_W, 2 * FN_GROUP_W))]
    args = [x2d, mod3, wts["g1"], wts["w_in"], wts["qag"], wts["w_qb"], wts["kvg"],
            wts["w_kb"], wts["w_vb"], wts["qg"], wts["kg"], wts["dft_c"]]
    if rope:
        in_specs += [pl.BlockSpec((TM, LANES), lambda i: (i % tiles_per_seq, 0))] * 3
        args += list(rope_tabs)
    out_shape = [jax.ShapeDtypeStruct((n, QK_WIDTH), BF16),
                 jax.ShapeDtypeStruct((n, QK_WIDTH), BF16),
                 jax.ShapeDtypeStruct((n, V_WIDTH), BF16),
                 jax.ShapeDtypeStruct((n, 2 * FN_WIDTH), BF16),
                 jax.ShapeDtypeStruct((n, D_MODEL), BF16),
                 jax.ShapeDtypeStruct((n, D_MODEL), BF16)]
    out_specs = [pl.BlockSpec((TM, s.shape[1]), lambda i: (i, 0)) for s in out_shape]
    if emit_cache:
        out_shape += [jax.ShapeDtypeStruct((n, KV_LORA), F32),
                      jax.ShapeDtypeStruct((n, QK_ROPE), F32)]
        out_specs += [pl.BlockSpec((TM, KV_LORA), lambda i: (i, 0)),
                      pl.BlockSpec((TM, QK_ROPE), lambda i: (i, 0))]
    return pl.pallas_call(
        functools.partial(_inproj_kernel, rope=rope, emit_cache=emit_cache),
        grid=(n // TM,),
        in_specs=in_specs,
        out_specs=out_specs,
        out_shape=out_shape,
        compiler_params=_cparams(("parallel",)),
        name="inproj_lat" if rope else "inproj_ctx",
    )(*args)


def _cache_kv_kernel(ckv_ref, kpe_ref, wkb_ref, wvb_ref, kg_ref, k_ref, v_ref):
    _emit_kv(ckv_ref[...].astype(BF16), kpe_ref[...], wkb_ref, wvb_ref, kg_ref, None, k_ref, v_ref)


def _cache_kv(ckv2d, kpe_slot2d, wts):
    n = ckv2d.shape[0]
    return pl.pallas_call(
        _cache_kv_kernel,
        grid=(n // TM,),
        in_specs=[pl.BlockSpec((TM, KV_LORA), lambda i: (i, 0)),
                  pl.BlockSpec((TM, LANES), lambda i: (i, 0)),
                  _const_spec((KV_LORA, QK_WIDTH)),
                  _const_spec((KV_LORA, V_WIDTH)),
                  _const_spec((1, HEAD_SLOT))],
        out_specs=[pl.BlockSpec((TM, QK_WIDTH), lambda i: (i, 0)),
                   pl.BlockSpec((TM, V_WIDTH), lambda i: (i, 0))],
        out_shape=[jax.ShapeDtypeStruct((n, QK_WIDTH), BF16),
                   jax.ShapeDtypeStruct((n, V_WIDTH), BF16)],
        compiler_params=_cparams(("parallel",)),
        name="cache_kv",
    )(ckv2d, kpe_slot2d, wts["w_kb"], wts["w_vb"], wts["kg"])


def _attn_kernel(*refs, n_kv):
    q_ref = refs[0]
    k_refs = refs[1:1 + n_kv]
    v_refs = refs[1 + n_kv:1 + 2 * n_kv]
    o_ref = refs[1 + 2 * n_kv]
    for hd in range(N_HEADS):
        qh = q_ref[0, :, hd * HEAD_SLOT:(hd + 1) * HEAD_SLOT]
        s = [_dot_nt(qh, k[0, :, hd * HEAD_SLOT:(hd + 1) * HEAD_SLOT]) for k in k_refs]
        m = s[0].max(axis=-1, keepdims=True)
        for sj in s[1:]:
            m = jnp.maximum(m, sj.max(axis=-1, keepdims=True))
        p = [jnp.exp(sj - m) for sj in s]
        l = p[0].sum(axis=-1, keepdims=True)
        for pj in p[1:]:
            l = l + pj.sum(axis=-1, keepdims=True)
        o = _dot(p[0].astype(BF16), v_refs[0][0, :, hd * V_HEAD:(hd + 1) * V_HEAD])
        for pj, v in zip(p[1:], v_refs[1:]):
            o = o + _dot(pj.astype(BF16), v[0, :, hd * V_HEAD:(hd + 1) * V_HEAD])
        o_ref[0, :, hd * V_HEAD:(hd + 1) * V_HEAD] = (o / l).astype(BF16)


def _attention(q3, ks, vs, tq, name):
    b, sq, _ = q3.shape
    n_kv = len(ks)
    in_specs = [pl.BlockSpec((1, tq, QK_WIDTH), lambda bi, qi: (bi, qi, 0))]
    in_specs += [pl.BlockSpec((1, k.shape[1], QK_WIDTH), lambda bi, qi: (bi, 0, 0)) for k in ks]
    in_specs += [pl.BlockSpec((1, v.shape[1], V_WIDTH), lambda bi, qi: (bi, 0, 0)) for v in vs]
    return pl.pallas_call(
        functools.partial(_attn_kernel, n_kv=n_kv),
        grid=(b, sq // tq),
        in_specs=in_specs,
        out_specs=pl.BlockSpec((1, tq, V_WIDTH), lambda bi, qi: (bi, qi, 0)),
        out_shape=jax.ShapeDtypeStruct((b, sq, V_WIDTH), BF16),
        compiler_params=_cparams(("parallel", "parallel")),
        name=name,
    )(q3, *ks, *vs)


def _fourier_kernel(cs_ref, ns_ref, f_ref, o_ref):
    xc = f_ref[0, :, :FN_WIDTH]
    xs = f_ref[0, :, FN_WIDTH:]
    o_ref[0] = (_dot(cs_ref[...], xc) + _dot(ns_ref[...], xs)).astype(BF16)


def _fourier(fcs3, cs, ns, tr, name):
    b, s, _ = fcs3.shape
    return pl.pallas_call(
        _fourier_kernel,
        grid=(b, s // tr),
        in_specs=[pl.BlockSpec((tr, s), lambda bi, ri: (ri, 0)),
                  pl.BlockSpec((tr, s), lambda bi, ri: (ri, 0)),
                  pl.BlockSpec((1, s, 2 * FN_WIDTH), lambda bi, ri: (bi, 0, 0))],
        out_specs=pl.BlockSpec((1, tr, FN_WIDTH), lambda bi, ri: (bi, ri, 0)),
        out_shape=jax.ShapeDtypeStruct((b, s, FN_WIDTH), BF16),
        compiler_params=_cparams(("parallel", "parallel")),
        name=name,
    )(cs, ns, fcs3)


def _modulate2(x1, g2_ref, mod_ref):
    return (x1 * _rms(x1, D_MODEL) * g2_ref[...]) * (1.0 + mod_ref[0, 4:5, :]) + mod_ref[0, 3:4, :]


def _split_bf16(x):
    hi = x.astype(BF16)
    return hi, (x - hi.astype(F32)).astype(BF16)


def _router_logits(w, b, h_hi, h_lo):
    w_hi, w_lo = _split_bf16(w)
    return (_dot_nt(w_hi, h_hi) + _dot_nt(w_lo, h_hi) + _dot_nt(w_hi, h_lo)) + b


def _rows(x, n):
    return [x[j:j + 1, :] for j in range(n)]


def _first_argmax(rows, top):
    idx = jnp.full(top.shape, len(rows) - 1, jnp.int32)
    for j in range(len(rows) - 2, -1, -1):
        idx = jnp.where(rows[j] == top, j, idx)
    return idx


def _store_token_major(ref, x, tm):
    for s in range(TOKEN_ROWS):
        ref[pl.ds(s, tm, stride=TOKEN_ROWS), :] = x[:, s * LANES:(s + 1) * LANES]


def _load_token_major(ref, tm):
    return jnp.concatenate([ref[pl.ds(s, tm, stride=TOKEN_ROWS), :] for s in range(TOKEN_ROWS)],
                           axis=1)


def _merge_kernel(x_ref, attn_ref, fm_ref, sga_ref, sgf_ref, mod_ref, wao_ref, wfn_ref, wout_ref,
                  g2_ref, wrg_ref, brg_ref, x1_ref, seg_ref, rank_ref, cnt_ref, carry_ref, *,
                  n_mod, tiles_per_seq):
    i = pl.program_id(0)

    @pl.when(i == 0)
    def _():
        carry_ref[...] = jnp.zeros_like(carry_ref)

    a = _dot(attn_ref[...], wao_ref[...])
    f = _dot(fm_ref[...], wfn_ref[...])
    u = sga_ref[...].astype(F32) * a + sgf_ref[...].astype(F32) * f
    y = _dot(u.astype(BF16), wout_ref[...])
    x1 = x_ref[...] + mod_ref[0, 2:3, :] * y
    tm = x1.shape[0]
    _store_token_major(x1_ref, x1, tm)

    h2_hi, h2_lo = _split_bf16(_modulate2(x1, g2_ref, mod_ref))
    g = _rows(_router_logits(wrg_ref[...], brg_ref[...], h2_hi, h2_lo), N_GROUPS)
    gidx = _first_argmax(g, functools.reduce(jnp.maximum, g))

    seg = gidx * n_mod
    if n_mod > 1:
        seg = seg + i // tiles_per_seq
    onehot = jnp.where(lax.broadcasted_iota(jnp.int32, (SEG_ROWS, tm), 0) == seg, 1.0, 0.0)
    before = (lax.broadcasted_iota(jnp.int32, (tm, tm), 0)
              < lax.broadcasted_iota(jnp.int32, (tm, tm), 1))
    prefix = _dot(onehot.astype(BF16), jnp.where(before, 1.0, 0.0).astype(BF16))
    carry = carry_ref[...]
    rank = jnp.sum(onehot * (prefix + carry[:, 0:1]), axis=0, keepdims=True)
    seg_ref[...] = seg
    rank_ref[...] = rank.astype(jnp.int32)
    carry = carry + jnp.sum(onehot, axis=1, keepdims=True)
    carry_ref[...] = carry
    cnt_ref[...] = carry.astype(jnp.int32)


def _merge(x2d, attn2d, fm2d, sga, sgf, mod3, mod_row_fn, wts, n_mod):
    n = x2d.shape[0]
    tok = lambda w: pl.BlockSpec((TM, w), lambda i: (i, 0))
    return pl.pallas_call(
        functools.partial(_merge_kernel, n_mod=n_mod, tiles_per_seq=n // n_mod // TM),
        grid=(n // TM,),
        in_specs=[tok(D_MODEL), tok(V_WIDTH), tok(FN_WIDTH), tok(D_MODEL), tok(D_MODEL),
                  pl.BlockSpec((1, 6, D_MODEL), lambda i: (mod_row_fn(i), 0, 0)),
                  _const_spec((V_WIDTH, D_MODEL)),
                  _const_spec((FN_WIDTH, D_MODEL)),
                  _const_spec((D_MODEL, D_MODEL)),
                  _const_spec((1, D_MODEL)),
                  _const_spec((ROUTER_ROWS, D_MODEL)),
                  _const_spec((ROUTER_ROWS, 1))],
        out_specs=[pl.BlockSpec((TM * TOKEN_ROWS, LANES), lambda i: (i, 0)),
                   pl.BlockSpec((1, TM), lambda i: (0, i)),
                   pl.BlockSpec((1, TM), lambda i: (0, i)), _const_spec((SEG_ROWS, LANES))],
        out_shape=[jax.ShapeDtypeStruct((n * TOKEN_ROWS, LANES), F32),
                   jax.ShapeDtypeStruct((1, n), jnp.int32),
                   jax.ShapeDtypeStruct((1, n), jnp.int32),
                   jax.ShapeDtypeStruct((SEG_ROWS, LANES), jnp.int32)],
        scratch_shapes=[pltpu.VMEM((SEG_ROWS, LANES), F32)],
        compiler_params=_cparams(("arbitrary",)),
        name="merge",
    )(x2d, attn2d, fm2d, sga, sgf, mod3, wts["w_ao"], wts["w_fn"], wts["w_out"], wts["g2"],
      wts["w_rg"], wts["b_rg"])


def _token_rows(ref, t):
    start = t * TOKEN_ROWS
    if not isinstance(t, int):
        start = pl.multiple_of(start, TOKEN_ROWS)
    return ref.at[pl.ds(start, TOKEN_ROWS)]


def _row_copy(src, dst, src_tok, dst_tok, sem):
    return pltpu.make_async_copy(_token_rows(src, src_tok), _token_rows(dst, dst_tok), sem)


def _dispatch_kernel(pos_ref, lo_ref, hi_ref, x_hbm, zrow_hbm, o_hbm, sem, *, n_fill):
    n = x_hbm.shape[0] // TOKEN_ROWS

    def start(t, c):
        _row_copy(x_hbm, o_hbm, t, pos_ref[t], sem.at[0]).start()
        return c

    def wait(t, c):
        _row_copy(x_hbm, o_hbm, 0, 0, sem.at[0]).wait()
        return c

    def start_zero(r, c):
        _row_copy(zrow_hbm, o_hbm, 0, r, sem.at[1]).start()
        return c

    def wait_zero(r, c):
        _row_copy(zrow_hbm, o_hbm, 0, 0, sem.at[1]).wait()
        return c

    lax.fori_loop(0, n, start, 0, unroll=8)
    for s in range(n_fill):
        lax.fori_loop(lo_ref[s], hi_ref[s], start_zero, 0)
    lax.fori_loop(0, n, wait, 0, unroll=8)
    for s in range(n_fill):
        lax.fori_loop(lo_ref[s], hi_ref[s], wait_zero, 0)


def _dispatch(x1tm, pos, fill_lo, fill_hi, n_sorted):
    zeros = jnp.zeros((TOKEN_ROWS, LANES), F32)
    return pl.pallas_call(
        functools.partial(_dispatch_kernel, n_fill=fill_lo.shape[0]),
        grid_spec=pltpu.PrefetchScalarGridSpec(
            num_scalar_prefetch=3,
            grid=(1,),
            in_specs=[pl.BlockSpec(memory_space=pl.ANY), pl.BlockSpec(memory_space=pl.ANY)],
            out_specs=pl.BlockSpec(memory_space=pl.ANY),
            scratch_shapes=[pltpu.SemaphoreType.DMA((2,))]),
        out_shape=jax.ShapeDtypeStruct((n_sorted * TOKEN_ROWS, LANES), F32),
        compiler_params=_cparams(("arbitrary",)),
        name="moe_dispatch",
    )(pos, fill_lo, fill_hi, x1tm, zeros)


def _combine_kernel(pos_ref, y_hbm, o_ref, buf, sem):
    i = pl.program_id(0)
    tm = o_ref.shape[0]

    def gather(tile_idx, slot):
        def start(r, c):
            _row_copy(y_hbm, buf.at[slot], pos_ref[tile_idx * tm + r], r, sem.at[slot]).start()
            return c

        lax.fori_loop(0, tm, start, 0, unroll=8)

    @pl.when(i == 0)
    def _():
        gather(0, 0)

    @pl.when(i + 1 < pl.num_programs(0))
    def _():
        gather(i + 1, (i + 1) % 2)

    slot = i % 2

    def wait(r, c):
        _row_copy(y_hbm, buf.at[slot], 0, 0, sem.at[slot]).wait()
        return c

    lax.fori_loop(0, tm, wait, 0, unroll=8)
    o_ref[...] = _load_token_major(buf.at[slot], tm)


def _combine(y_sorted, pos, n):
    return pl.pallas_call(
        _combine_kernel,
        grid_spec=pltpu.PrefetchScalarGridSpec(
            num_scalar_prefetch=1,
            grid=(n // TM,),
            in_specs=[pl.BlockSpec(memory_space=pl.ANY)],
            out_specs=pl.BlockSpec((TM, D_MODEL), lambda i, pos: (i, 0)),
            scratch_shapes=[pltpu.VMEM((2, TM * TOKEN_ROWS, LANES), F32),
                            pltpu.SemaphoreType.DMA((2,))]),
        out_shape=jax.ShapeDtypeStruct((n, D_MODEL), F32),
        compiler_params=_cparams(("arbitrary",)),
        name="moe_combine",
    )(pos, y_sorted)


def _moe_kernel(grp_ref, modrow_ref, on_ref, x_ref, mod_ref, g2_ref, wrg_ref, brg_ref, wre_ref,
                bre_ref, wg_ref, wu_ref, wd_ref, o_ref, *, tile):
    i = pl.program_id(0)

    @pl.when(on_ref[i] == 0)
    def _():
        o_ref[...] = jnp.zeros_like(o_ref)

    @pl.when(on_ref[i] == 1)
    def _():
        x1 = _load_token_major(x_ref, tile)
        h2 = _modulate2(x1, g2_ref, mod_ref)
        h2_hi, h2_lo = _split_bf16(h2)

        g = _rows(_router_logits(wrg_ref[...], brg_ref[...], h2_hi, h2_lo), N_GROUPS)
        gmax = functools.reduce(jnp.maximum, g)
        p_top = 1.0 / functools.reduce(lambda p, q: p + q, [jnp.exp(gj - gmax) for gj in g])
        e = _rows(_router_logits(wre_ref[0], bre_ref[0], h2_hi, h2_lo), EXPERTS_PER_GROUP)
        m1 = functools.reduce(jnp.maximum, e)
        i1 = _first_argmax(e, m1)
        rest = [jnp.where(i1 == j, -jnp.inf, e[j]) for j in range(EXPERTS_PER_GROUP)]
        m2 = functools.reduce(jnp.maximum, rest)
        i2 = _first_argmax(rest, m2)
        t = jnp.exp(m2 - m1)
        w1 = p_top / (1.0 + t)
        w2 = p_top * t / (1.0 + t)
        row = lax.broadcasted_iota(jnp.int32, (LANES, tile), 0)
        comb = (jnp.where(row == i1, w1, 0.0) + jnp.where(row == i2, w2, 0.0)).T

        acc = None
        for j in range(EXPERTS_PER_GROUP):
            a = _dot(h2_hi, wg_ref[j])
            u = _dot(h2_hi, wu_ref[j])
            act = (a * _sigmoid(a)) * u * comb[:, j:j + 1]
            y = _dot(act.astype(BF16), wd_ref[j])
            acc = y if acc is None else acc + y
        _store_token_major(o_ref, x1 + mod_ref[0, 5:6, :] * acc, tile)


def _moe(x_sorted, maps, mod3, wts, w_gate, w_up, w_down, tile):
    rows = x_sorted.shape[0]
    const = lambda shape: pl.BlockSpec(shape, lambda i, grp, mrow, on: (0,) * len(shape))
    by_group = lambda shape: pl.BlockSpec(shape, lambda i, grp, mrow, on: (grp[i],) + (0,) * (len(shape) - 1))
    tok = pl.BlockSpec((tile * TOKEN_ROWS, LANES), lambda i, grp, mrow, on: (i, 0))
    return pl.pallas_call(
        functools.partial(_moe_kernel, tile=tile),
        grid_spec=pltpu.PrefetchScalarGridSpec(
            num_scalar_prefetch=3,
            grid=(rows // (tile * TOKEN_ROWS),),
            in_specs=[tok,
                      pl.BlockSpec((1, 6, D_MODEL), lambda i, grp, mrow, on: (mrow[i], 0, 0)),
                      const((1, D_MODEL)),
                      const((ROUTER_ROWS, D_MODEL)), const((ROUTER_ROWS, 1)),
                      by_group((1, ROUTER_ROWS, D_MODEL)), by_group((1, ROUTER_ROWS, 1)),
                      by_group((EXPERTS_PER_GROUP, D_MODEL, D_EXPERT)),
                      by_group((EXPERTS_PER_GROUP, D_MODEL, D_EXPERT)),
                      by_group((EXPERTS_PER_GROUP, D_EXPERT, D_MODEL))],
            out_specs=tok),
        out_shape=jax.ShapeDtypeStruct((rows, LANES), F32),
        compiler_params=_cparams(("arbitrary",)),
        name="moe",
    )(*maps, x_sorted, mod3, wts["g2"], wts["w_rg"], wts["b_rg"], wts["w_re"], wts["b_re"],
      w_gate, w_up, w_down)


def _sort_plan(seg, rank, cnt, n_mod, row_base, tile, n_tiles):
    n_tile_seg = (cnt + tile - 1) // tile
    tile_end = jnp.cumsum(n_tile_seg)
    tile_start = tile_end - n_tile_seg
    total = tile_end[-1]
    pos = (tile_start * tile)[seg] + rank
    i = jnp.arange(n_tiles, dtype=jnp.int32)
    tile_seg = jnp.sum((jnp.minimum(i, total - 1)[:, None] >= tile_end[None, :]).astype(jnp.int32),
                       axis=1)
    maps = ((tile_seg // n_mod).astype(jnp.int32), (row_base + tile_seg % n_mod).astype(jnp.int32),
            (i < total).astype(jnp.int32))
    fill_lo = jnp.concatenate([tile_start * tile + cnt, total[None] * tile]).astype(jnp.int32)
    fill_hi = jnp.concatenate([tile_end * tile, jnp.full((1,), n_tiles * tile)]).astype(jnp.int32)
    return pos.astype(jnp.int32), maps, fill_lo, fill_hi


def _sparse_moe(x1tm, seg, rank, cnt8, mod3, wts, w_gate, w_up, w_down, n_mod, row_base, tile):
    n = x1tm.shape[0] // TOKEN_ROWS
    n_seg = N_GROUPS * n_mod
    n_tiles = n // tile + n_seg
    pos, maps, fill_lo, fill_hi = _sort_plan(seg.reshape(n), rank.reshape(n), cnt8[:n_seg, 0],
                                             n_mod, row_base, tile, n_tiles)
    x_sorted = _dispatch(x1tm, pos, fill_lo, fill_hi, n_tiles * tile)
    y_sorted = _moe(x_sorted, maps, mod3, wts, w_gate, w_up, w_down, tile)
    return _combine(y_sorted, pos, n)


def _pack_weights(l, w_in, norm1_g, q_a_norm_g, w_q_b, kv_a_norm_g, w_kv_b, q_norm_g, k_norm_g,
                  w_attn_o, w_fnet, w_out, norm2_g, w_router_group, b_router_group,
                  w_router_expert, b_router_expert):
    wi = w_in[l]
    kpe_cols = jnp.pad(wi[:, C_KPE:C_KPE + QK_ROPE], ((0, 0), (QK_NOPE, LANES - QK_HEAD)))
    w_in_p = jnp.concatenate([wi[:, :C_KPE], kpe_cols, wi[:, C_KPE + QK_ROPE:]], axis=1)
    w_qb = jnp.pad(w_q_b[l].reshape(Q_LORA, N_HEADS, QK_HEAD),
                   ((0, 0), (0, 0), (0, HEAD_SLOT - QK_HEAD))).reshape(Q_LORA, QK_WIDTH)
    wkv = w_kv_b[l].reshape(KV_LORA, N_HEADS, QK_NOPE + V_HEAD)
    w_kb = jnp.pad(wkv[:, :, :QK_NOPE],
                   ((0, 0), (0, 0), (0, HEAD_SLOT - QK_NOPE))).reshape(KV_LORA, QK_WIDTH)
    w_vb = wkv[:, :, QK_NOPE:].reshape(KV_LORA, V_WIDTH)
    pad_g = lambda g: jnp.pad(g, (0, HEAD_SLOT - QK_HEAD)).reshape(1, HEAD_SLOT)
    w_rg = jnp.pad(w_router_group[l].T, ((0, ROUTER_ROWS - N_GROUPS), (0, 0)))
    b_rg = jnp.pad(b_router_group[l], (0, ROUTER_ROWS - N_GROUPS)).reshape(ROUTER_ROWS, 1)
    w_re = jnp.pad(w_router_expert[l].T.reshape(N_GROUPS, EXPERTS_PER_GROUP, D_MODEL),
                   ((0, 0), (0, ROUTER_ROWS - EXPERTS_PER_GROUP), (0, 0)))
    b_re = jnp.pad(b_router_expert[l].reshape(N_GROUPS, EXPERTS_PER_GROUP),
                   ((0, 0), (0, ROUTER_ROWS - EXPERTS_PER_GROUP))).reshape(N_GROUPS, ROUTER_ROWS, 1)
    dft_c, dft_ns = _dft_tables(FN_GROUP_W)
    return {
        "g1": norm1_g[l].reshape(1, D_MODEL),
        "w_in": w_in_p.astype(BF16),
        "qag": q_a_norm_g[l].reshape(1, Q_LORA),
        "w_qb": w_qb.astype(BF16),
        "kvg": kv_a_norm_g[l].reshape(1, KV_LORA),
        "w_kb": w_kb.astype(BF16),
        "w_vb": w_vb.astype(BF16),
        "qg": pad_g(q_norm_g[l]),
        "kg": pad_g(k_norm_g[l]),
        "dft_c": jnp.concatenate([jnp.asarray(dft_c), -jnp.asarray(dft_ns)], axis=1).astype(BF16),
        "w_ao": w_attn_o[l].astype(BF16),
        "w_fn": w_fnet[l].astype(BF16),
        "w_out": w_out[l].astype(BF16),
        "g2": norm2_g[l].reshape(1, D_MODEL),
        "w_rg": w_rg,
        "b_rg": b_rg,
        "w_re": w_re,
        "b_re": b_re,
    }


def _layer(xp, xs, cache_ckv_l, cache_kpe_l, mod3, wts, w_gate, w_up, w_down):
    bp, sp, _ = xp.shape
    bs, ss, _ = xs.shape
    past = cache_ckv_l.shape[1]
    ctx_row = lambda i: 0
    lat_row = lambda i: 1 + i // (ss // TM)

    xp2 = xp.reshape(bp * sp, D_MODEL)
    q, k, v, fcs, sga, sgf, ckv, kpe = _inproj(xp2, mod3, ctx_row, wts, None, True)
    attn = _attention(q.reshape(bp, sp, QK_WIDTH), [k.reshape(bp, sp, QK_WIDTH)],
                      [v.reshape(bp, sp, V_WIDTH)], sp, "attn_ctx")
    cs, ns = (jnp.asarray(t).astype(BF16) for t in _dft_tables(sp))
    fm = _fourier(fcs.reshape(bp, sp, 2 * FN_WIDTH), cs, ns, sp, "fourier_ctx")
    x1tm, seg, rank, cnt = _merge(xp2, attn.reshape(bp * sp, V_WIDTH), fm.reshape(bp * sp, FN_WIDTH),
                                  sga, sgf, mod3, ctx_row, wts, 1)
    yp = _sparse_moe(x1tm, seg, rank, cnt, mod3, wts, w_gate, w_up, w_down, 1, 0,
                     MOE_TILE_CTX).reshape(bp, sp, D_MODEL)

    xs2 = xs.reshape(bs * ss, D_MODEL)
    rope_tabs = tuple(jnp.asarray(t) for t in _rope_tables(ss))
    q, k, v, fcs, sga, sgf = _inproj(xs2, mod3, lat_row, wts, rope_tabs, False)
    kpe_slot = jnp.pad(cache_kpe_l, ((0, 0), (0, 0), (QK_NOPE, LANES - QK_HEAD)))
    kc, vc = _cache_kv(cache_ckv_l.reshape(bs * past, KV_LORA), kpe_slot.reshape(bs * past, LANES), wts)
    attn = _attention(q.reshape(bs, ss, QK_WIDTH),
                      [kc.reshape(bs, past, QK_WIDTH), k.reshape(bs, ss, QK_WIDTH)],
                      [vc.reshape(bs, past, V_WIDTH), v.reshape(bs, ss, V_WIDTH)], TM, "attn_lat")
    cs, ns = (jnp.asarray(t).astype(BF16) for t in _dft_tables(ss))
    fm = _fourier(fcs.reshape(bs, ss, 2 * FN_WIDTH), cs, ns, TM, "fourier_lat")
    assert N_GROUPS * bs <= SEG_ROWS
    x1tm, seg, rank, cnt = _merge(xs2, attn.reshape(bs * ss, V_WIDTH), fm.reshape(bs * ss, FN_WIDTH),
                                  sga, sgf, mod3, lat_row, wts, bs)
    ys = _sparse_moe(x1tm, seg, rank, cnt, mod3, wts, w_gate, w_up, w_down, bs, 1,
                     MOE_TILE_LAT).reshape(bs, ss, D_MODEL)

    return yp, ys, ckv.reshape(bp, sp, KV_LORA), kpe.reshape(bp, sp, QK_ROPE)


def kernel(x_prompt, x_sample, cache_ckv, cache_kpe, c, c_ctx, w_mod, b_mod, norm1_g, w_in, q_a_norm_g, w_q_b, kv_a_norm_g, w_kv_b, q_norm_g, k_norm_g, w_attn_o, w_fnet, w_out, norm2_g, w_router_group, b_router_group, w_router_expert, b_router_expert, w_exp_gate, w_exp_up, w_exp_down):
    depth = w_mod.shape[0]
    n_lat = c.shape[0]
    assert 1 + n_lat <= MOD_ROWS
    cond8 = jnp.concatenate([c_ctx[None, :], c, jnp.zeros((MOD_ROWS - 1 - n_lat, D_MODEL), F32)], axis=0)
    xp, xs = x_prompt, x_sample
    ckv_layers, kpe_layers = [], []
    for l in range(depth):
        mod3 = _adaln(cond8, w_mod[l], b_mod[l].reshape(1, -1)).reshape(MOD_ROWS, 6, D_MODEL)
        wts = _pack_weights(l, w_in, norm1_g, q_a_norm_g, w_q_b, kv_a_norm_g, w_kv_b, q_norm_g,
                            k_norm_g, w_attn_o, w_fnet, w_out, norm2_g, w_router_group,
                            b_router_group, w_router_expert, b_router_expert)
        xp, xs, ckv, kpe = _layer(xp, xs, cache_ckv[:, l], cache_kpe[:, l], mod3, wts,
                                  w_exp_gate[l].astype(BF16), w_exp_up[l].astype(BF16),
                                  w_exp_down[l].astype(BF16))
        ckv_layers.append(ckv)
        kpe_layers.append(kpe)
    return xp, xs, jnp.stack(ckv_layers, axis=1), jnp.stack(kpe_layers, axis=1)
```

```python
import functools
import math

import numpy as np
import jax
import jax.numpy as jnp
from jax import lax
from jax.experimental import pallas as pl
from jax.experimental.pallas import tpu as pltpu

D_MODEL = 1024
GRID_W = 64
N_HEADS = 8
Q_LORA = 512
KV_LORA = 256
QK_NOPE = 64
QK_ROPE = 32
V_HEAD = 64
QK_HEAD = QK_NOPE + QK_ROPE
ATTN_SCALE = QK_HEAD ** -0.5
ROPE_BASE = 10000.0
FN_GROUPS = 4
FN_GROUP_W = 128
FN_WIDTH = FN_GROUPS * FN_GROUP_W
N_GROUPS = 4
EXPERTS_PER_GROUP = 4
N_EXPERTS = N_GROUPS * EXPERTS_PER_GROUP
D_EXPERT = 512
EPS = 1e-6

LANES = 128
HEAD_SLOT = LANES
QK_WIDTH = N_HEADS * HEAD_SLOT
V_WIDTH = N_HEADS * V_HEAD
C_QA = 0
C_KVA = C_QA + Q_LORA
C_KPE = C_KVA + KV_LORA
C_FN = C_KPE + LANES
C_GA = C_FN + FN_WIDTH
C_GF = C_GA + D_MODEL
IN_PACKED = C_GF + D_MODEL
SUBLANES = 8
ROUTER_ROWS = SUBLANES
MOD_ROWS = SUBLANES
VMEM_LIMIT = 56 * 1024 * 1024

TM = 256
SEG_ROWS = SUBLANES
TOKEN_ROWS = D_MODEL // LANES
MOE_TILE_CTX = 256
MOE_TILE_LAT = 128

BF16 = jnp.bfloat16
F32 = jnp.float32


def _cparams(sem):
    return pltpu.CompilerParams(dimension_semantics=sem, vmem_limit_bytes=VMEM_LIMIT)


def _dot(a, b):
    return jnp.dot(a, b, preferred_element_type=F32)


def _dot_nt(a, b):
    return lax.dot_general(a, b, (((1,), (1,)), ((), ())), preferred_element_type=F32)


def _sigmoid(x):
    return 1.0 / (1.0 + jnp.exp(-x))


@functools.lru_cache(maxsize=None)
def _rope_tables(n_pos):
    half = QK_ROPE // 2
    quarter = half // 2
    freqs = ROPE_BASE ** (-np.arange(quarter, dtype=np.float64) / quarter)
    pos = np.arange(n_pos)
    row = (pos // GRID_W).astype(np.float64)
    col = (pos % GRID_W).astype(np.float64)
    cos_t = np.ones((n_pos, LANES), np.float64)
    sin_a = np.zeros((n_pos, LANES), np.float64)
    sin_b = np.zeros((n_pos, LANES), np.float64)
    for base, p in ((QK_NOPE, row), (QK_NOPE + half, col)):
        ang = p[:, None] * freqs[None, :]
        cos_t[:, base:base + quarter] = np.cos(ang)
        cos_t[:, base + quarter:base + half] = np.cos(ang)
        sin_a[:, base:base + quarter] = -np.sin(ang)
        sin_b[:, base + quarter:base + half] = np.sin(ang)
    return (cos_t.astype(np.float32), sin_a.astype(np.float32), sin_b.astype(np.float32))


@functools.lru_cache(maxsize=None)
def _dft_tables(n):
    k = np.arange(n)
    ang = 2.0 * np.pi * ((k[:, None] * k[None, :]) % n) / n
    s = 1.0 / math.sqrt(n)
    return (np.cos(ang) * s).astype(np.float32), (-np.sin(ang) * s).astype(np.float32)


def _adaln_kernel(cond_ref, w_ref, b_ref, o_ref):
    c = cond_ref[...]
    s = c * _sigmoid(c)
    o_ref[...] = jnp.dot(s, w_ref[...], preferred_element_type=F32,
                         precision=lax.Precision.HIGHEST) + b_ref[...]


def _adaln(cond8, w_mod, b_mod):
    n = w_mod.shape[1]
    tn = D_MODEL
    return pl.pallas_call(
        _adaln_kernel,
        grid=(n // tn,),
        in_specs=[pl.BlockSpec((MOD_ROWS, D_MODEL), lambda j: (0, 0)),
                  pl.BlockSpec((D_MODEL, tn), lambda j: (0, j)),
                  pl.BlockSpec((1, tn), lambda j: (0, j))],
        out_specs=pl.BlockSpec((MOD_ROWS, tn), lambda j: (0, j)),
        out_shape=jax.ShapeDtypeStruct((MOD_ROWS, n), F32),
        compiler_params=_cparams(("arbitrary",)),
        name="adaln",
    )(cond8, w_mod, b_mod)


def _rms(x, width):
    return lax.rsqrt(jnp.sum(x * x, axis=-1, keepdims=True) * (1.0 / width) + EPS)


def _rope(x, cos_t, sin_a, sin_b):
    return x * cos_t + pltpu.roll(x, LANES - 8, 1) * sin_a + pltpu.roll(x, 8, 1) * sin_b


def _inproj_kernel(*refs, rope, emit_cache):
    it = iter(refs)
    x_ref, mod_ref, g1_ref, win_ref, qag_ref, wqb_ref, kvg_ref, wkb_ref, wvb_ref = (
        next(it) for _ in range(9))
    qg_ref, kg_ref, dft_ref = next(it), next(it), next(it)
    if rope:
        cos_ref, sa_ref, sb_ref = next(it), next(it), next(it)
    q_ref, k_ref, v_ref, fcs_ref, sga_ref, sgf_ref = (next(it) for _ in range(6))
    if emit_cache:
        ckv_ref, kpe_ref = next(it), next(it)

    x = x_ref[...]
    shift = mod_ref[0, 0:1, :]
    scale = mod_ref[0, 1:2, :]
    h = (x * _rms(x, D_MODEL) * g1_ref[...]) * (1.0 + scale) + shift
    hb = h.astype(BF16)

    if rope:
        cos_t, sin_a, sin_b = cos_ref[...], sa_ref[...], sb_ref[...]

    qa = _dot(hb, win_ref[:, C_QA:C_QA + Q_LORA])
    qn = (qa * _rms(qa, Q_LORA) * qag_ref[...]).astype(BF16)
    q = _dot(qn, wqb_ref[...])
    qg = qg_ref[...] * ATTN_SCALE
    for hd in range(N_HEADS):
        qh = q[:, hd * HEAD_SLOT:(hd + 1) * HEAD_SLOT]
        qh = qh * _rms(qh, QK_HEAD) * qg
        if rope:
            qh = _rope(qh, cos_t, sin_a, sin_b)
        q_ref[:, hd * HEAD_SLOT:(hd + 1) * HEAD_SLOT] = qh.astype(BF16)

    kva = _dot(hb, win_ref[:, C_KVA:C_KVA + KV_LORA])
    ckv = kva * _rms(kva, KV_LORA) * kvg_ref[...]
    kpe = _dot(hb, win_ref[:, C_KPE:C_KPE + LANES])
    if emit_cache:
        ckv_ref[...] = ckv
        kpe_ref[...] = kpe[:, QK_NOPE:QK_NOPE + QK_ROPE]
    _emit_kv(ckv.astype(BF16), kpe, wkb_ref, wvb_ref, kg_ref,
             (cos_t, sin_a, sin_b) if rope else None, k_ref, v_ref)

    fn = _dot(hb, win_ref[:, C_FN:C_FN + FN_WIDTH]).astype(BF16)
    for g in range(FN_GROUPS):
        cs = _dot(fn[:, g * FN_GROUP_W:(g + 1) * FN_GROUP_W], dft_ref[...])
        fcs_ref[:, g * FN_GROUP_W:(g + 1) * FN_GROUP_W] = cs[:, :FN_GROUP_W].astype(BF16)
        fcs_ref[:, FN_WIDTH + g * FN_GROUP_W:FN_WIDTH + (g + 1) * FN_GROUP_W] = (
            cs[:, FN_GROUP_W:].astype(BF16))

    sga_ref[...] = _sigmoid(_dot(hb, win_ref[:, C_GA:C_GA + D_MODEL])).astype(BF16)
    sgf_ref[...] = _sigmoid(_dot(hb, win_ref[:, C_GF:C_GF + D_MODEL])).astype(BF16)


def _emit_kv(ckvb, kpe, wkb_ref, wvb_ref, kg_ref, rope_tabs, k_ref, v_ref):
    kg = kg_ref[...]
    v_ref[...] = _dot(ckvb, wvb_ref[...]).astype(BF16)
    kn = _dot(ckvb, wkb_ref[...])
    pe_ss = jnp.sum(kpe * kpe, axis=-1, keepdims=True)
    pe_g = kpe * kg
    if rope_tabs is not None:
        pe_g = _rope(pe_g, *rope_tabs)
    for hd in range(N_HEADS):
        knh = kn[:, hd * HEAD_SLOT:(hd + 1) * HEAD_SLOT]
        ss = jnp.sum(knh * knh, axis=-1, keepdims=True) + pe_ss
        r = lax.rsqrt(ss * (1.0 / QK_HEAD) + EPS)
        k_ref[:, hd * HEAD_SLOT:(hd + 1) * HEAD_SLOT] = ((knh * kg + pe_g) * r).astype(BF16)


def _const_spec(shape):
    return pl.BlockSpec(shape, lambda i: (0,) * len(shape))


def _inproj(x2d, mod3, mod_row_fn, wts, rope_tabs, emit_cache):
    n = x2d.shape[0]
    rope = rope_tabs is not None
    tiles_per_seq = None if not rope else rope_tabs[0].shape[0] // TM
    in_specs = [pl.BlockSpec((TM, D_MODEL), lambda i: (i, 0)),
                pl.BlockSpec((1, 6, D_MODEL), lambda i: (mod_row_fn(i), 0, 0)),
                _const_spec((1, D_MODEL)),
                _const_spec((D_MODEL, IN_PACKED)),
                _const_spec((1, Q_LORA)),
                _const_spec((Q_LORA, QK_WIDTH)),
                _const_spec((1, KV_LORA)),
                _const_spec((KV_LORA, QK_WIDTH)),
                _const_spec((KV_LORA, V_WIDTH)),
                _const_spec((1, HEAD_SLOT)),
                _const_spec((1, HEAD_SLOT)),
                _const_spec((FN_GROUP_W, 2 * FN_GROUP_W))]
    args = [x2d, mod3, wts["g1"], wts["w_in"], wts["qag"], wts["w_qb"], wts["kvg"],
            wts["w_kb"], wts["w_vb"], wts["qg"], wts["kg"], wts["dft_c"]]
    if rope:
        in_specs += [pl.BlockSpec((TM, LANES), lambda i: (i % tiles_per_seq, 0))] * 3
        args += list(rope_tabs)
    out_shape = [jax.ShapeDtypeStruct((n, QK_WIDTH), BF16),
                 jax.ShapeDtypeStruct((n, QK_WIDTH), BF16),
                 jax.ShapeDtypeStruct((n, V_WIDTH), BF16),
                 jax.ShapeDtypeStruct((n, 2 * FN_WIDTH), BF16),
                 jax.ShapeDtypeStruct((n, D_MODEL), BF16),
                 jax.ShapeDtypeStruct((n, D_MODEL), BF16)]
    out_specs = [pl.BlockSpec((TM, s.shape[1]), lambda i: (i, 0)) for s in out_shape]
    if emit_cache:
        out_shape += [jax.ShapeDtypeStruct((n, KV_LORA), F32),
                      jax.ShapeDtypeStruct((n, QK_ROPE), F32)]
        out_specs += [pl.BlockSpec((TM, KV_LORA), lambda i: (i, 0)),
                      pl.BlockSpec((TM, QK_ROPE), lambda i: (i, 0))]
    return pl.pallas_call(
        functools.partial(_inproj_kernel, rope=rope, emit_cache=emit_cache),
        grid=(n // TM,),
        in_specs=in_specs,
        out_specs=out_specs,
        out_shape=out_shape,
        compiler_params=_cparams(("parallel",)),
        name="inproj_lat" if rope else "inproj_ctx",
    )(*args)


def _cache_kv_kernel(ckv_ref, kpe_ref, wkb_ref, wvb_ref, kg_ref, k_ref, v_ref):
    _emit_kv(ckv_ref[...].astype(BF16), kpe_ref[...], wkb_ref, wvb_ref, kg_ref, None, k_ref, v_ref)


def _cache_kv(ckv2d, kpe_slot2d, wts):
    n = ckv2d.shape[0]
    return pl.pallas_call(
        _cache_kv_kernel,
        grid=(n // TM,),
        in_specs=[pl.BlockSpec((TM, KV_LORA), lambda i: (i, 0)),
                  pl.BlockSpec((TM, LANES), lambda i: (i, 0)),
                  _const_spec((KV_LORA, QK_WIDTH)),
                  _const_spec((KV_LORA, V_WIDTH)),
                  _const_spec((1, HEAD_SLOT))],
        out_specs=[pl.BlockSpec((TM, QK_WIDTH), lambda i: (i, 0)),
                   pl.BlockSpec((TM, V_WIDTH), lambda i: (i, 0))],
        out_shape=[jax.ShapeDtypeStruct((n, QK_WIDTH), BF16),
                   jax.ShapeDtypeStruct((n, V_WIDTH), BF16)],
        compiler_params=_cparams(("parallel",)),
        name="cache_kv",
    )(ckv2d, kpe_slot2d, wts["w_kb"], wts["w_vb"], wts["kg"])


def _attn_kernel(*refs, n_kv):
    q_ref = refs[0]
    k_refs = refs[1:1 + n_kv]
    v_refs = refs[1 + n_kv:1 + 2 * n_kv]
    o_ref = refs[1 + 2 * n_kv]
    for hd in range(N_HEADS):
        qh = q_ref[0, :, hd * HEAD_SLOT:(hd + 1) * HEAD_SLOT]
        s = [_dot_nt(qh, k[0, :, hd * HEAD_SLOT:(hd + 1) * HEAD_SLOT]) for k in k_refs]
        m = s[0].max(axis=-1, keepdims=True)
        for sj in s[1:]:
            m = jnp.maximum(m, sj.max(axis=-1, keepdims=True))
        p = [jnp.exp(sj - m) for sj in s]
        l = p[0].sum(axis=-1, keepdims=True)
        for pj in p[1:]:
            l = l + pj.sum(axis=-1, keepdims=True)
        o = _dot(p[0].astype(BF16), v_refs[0][0, :, hd * V_HEAD:(hd + 1) * V_HEAD])
        for pj, v in zip(p[1:], v_refs[1:]):
            o = o + _dot(pj.astype(BF16), v[0, :, hd * V_HEAD:(hd + 1) * V_HEAD])
        o_ref[0, :, hd * V_HEAD:(hd + 1) * V_HEAD] = (o / l).astype(BF16)


def _attention(q3, ks, vs, tq, name):
    b, sq, _ = q3.shape
    n_kv = len(ks)
    in_specs = [pl.BlockSpec((1, tq, QK_WIDTH), lambda bi, qi: (bi, qi, 0))]
    in_specs += [pl.BlockSpec((1, k.shape[1], QK_WIDTH), lambda bi, qi: (bi, 0, 0)) for k in ks]
    in_specs += [pl.BlockSpec((1, v.shape[1], V_WIDTH), lambda bi, qi: (bi, 0, 0)) for v in vs]
    return pl.pallas_call(
        functools.partial(_attn_kernel, n_kv=n_kv),
        grid=(b, sq // tq),
        in_specs=in_specs,
        out_specs=pl.BlockSpec((1, tq, V_WIDTH), lambda bi, qi: (bi, qi, 0)),
        out_shape=jax.ShapeDtypeStruct((b, sq, V_WIDTH), BF16),
        compiler_params=_cparams(("parallel", "parallel")),
        name=name,
    )(q3, *ks, *vs)


def _fourier_kernel(cs_ref, ns_ref, f_ref, o_ref):
    xc = f_ref[0, :, :FN_WIDTH]
    xs = f_ref[0, :, FN_WIDTH:]
    o_ref[0] = (_dot(cs_ref[...], xc) + _dot(ns_ref[...], xs)).astype(BF16)


def _fourier(fcs3, cs, ns, tr, name):
    b, s, _ = fcs3.shape
    return pl.pallas_call(
        _fourier_kernel,
        grid=(b, s // tr),
        in_specs=[pl.BlockSpec((tr, s), lambda bi, ri: (ri, 0)),
                  pl.BlockSpec((tr, s), lambda bi, ri: (ri, 0)),
                  pl.BlockSpec((1, s, 2 * FN_WIDTH), lambda bi, ri: (bi, 0, 0))],
        out_specs=pl.BlockSpec((1, tr, FN_WIDTH), lambda bi, ri: (bi, ri, 0)),
        out_shape=jax.ShapeDtypeStruct((b, s, FN_WIDTH), BF16),
        compiler_params=_cparams(("parallel", "parallel")),
        name=name,
    )(cs, ns, fcs3)


def _modulate2(x1, g2_ref, mod_ref):
    return (x1 * _rms(x1, D_MODEL) * g2_ref[...]) * (1.0 + mod_ref[0, 4:5, :]) + mod_ref[0, 3:4, :]


def _split_bf16(x):
    hi = x.astype(BF16)
    return hi, (x - hi.astype(F32)).astype(BF16)


def _router_logits(w, b, h_hi, h_lo):
    w_hi, w_lo = _split_bf16(w)
    return (_dot_nt(w_hi, h_hi) + _dot_nt(w_lo, h_hi) + _dot_nt(w_hi, h_lo)) + b


def _rows(x, n):
    return [x[j:j + 1, :] for j in range(n)]


def _first_argmax(rows, top):
    idx = jnp.full(top.shape, len(rows) - 1, jnp.int32)
    for j in range(len(rows) - 2, -1, -1):
        idx = jnp.where(rows[j] == top, j, idx)
    return idx


def _store_token_major(ref, x, tm):
    for s in range(TOKEN_ROWS):
        ref[pl.ds(s, tm, stride=TOKEN_ROWS), :] = x[:, s * LANES:(s + 1) * LANES]


def _load_token_major(ref, tm):
    return jnp.concatenate([ref[pl.ds(s, tm, stride=TOKEN_ROWS), :] for s in range(TOKEN_ROWS)],
                           axis=1)


def _merge_kernel(x_ref, attn_ref, fm_ref, sga_ref, sgf_ref, mod_ref, wao_ref, wfn_ref, wout_ref,
                  g2_ref, wrg_ref, brg_ref, x1_ref, seg_ref, rank_ref, cnt_ref, carry_ref, *,
                  n_mod, tiles_per_seq):
    i = pl.program_id(0)

    @pl.when(i == 0)
    def _():
        carry_ref[...] = jnp.zeros_like(carry_ref)

    a = _dot(attn_ref[...], wao_ref[...])
    f = _dot(fm_ref[...], wfn_ref[...])
    u = sga_ref[...].astype(F32) * a + sgf_ref[...].astype(F32) * f
    y = _dot(u.astype(BF16), wout_ref[...])
    x1 = x_ref[...] + mod_ref[0, 2:3, :] * y
    tm = x1.shape[0]
    _store_token_major(x1_ref, x1, tm)

    h2_hi, h2_lo = _split_bf16(_modulate2(x1, g2_ref, mod_ref))
    g = _rows(_router_logits(wrg_ref[...], brg_ref[...], h2_hi, h2_lo), N_GROUPS)
    gidx = _first_argmax(g, functools.reduce(jnp.maximum, g))

    seg = gidx * n_mod
    if n_mod > 1:
        seg = seg + i // tiles_per_seq
    onehot = jnp.where(lax.broadcasted_iota(jnp.int32, (SEG_ROWS, tm), 0) == seg, 1.0, 0.0)
    before = (lax.broadcasted_iota(jnp.int32, (tm, tm), 0)
              < lax.broadcasted_iota(jnp.int32, (tm, tm), 1))
    prefix = _dot(onehot.astype(BF16), jnp.where(before, 1.0, 0.0).astype(BF16))
    carry = carry_ref[...]
    rank = jnp.sum(onehot * (prefix + carry[:, 0:1]), axis=0, keepdims=True)
    seg_ref[...] = seg
    rank_ref[...] = rank.astype(jnp.int32)
    carry = carry + jnp.sum(onehot, axis=1, keepdims=True)
    carry_ref[...] = carry
    cnt_ref[...] = carry.astype(jnp.int32)


def _merge(x2d, attn2d, fm2d, sga, sgf, mod3, mod_row_fn, wts, n_mod):
    n = x2d.shape[0]
    tok = lambda w: pl.BlockSpec((TM, w), lambda i: (i, 0))
    return pl.pallas_call(
        functools.partial(_merge_kernel, n_mod=n_mod, tiles_per_seq=n // n_mod // TM),
        grid=(n // TM,),
        in_specs=[tok(D_MODEL), tok(V_WIDTH), tok(FN_WIDTH), tok(D_MODEL), tok(D_MODEL),
                  pl.BlockSpec((1, 6, D_MODEL), lambda i: (mod_row_fn(i), 0, 0)),
                  _const_spec((V_WIDTH, D_MODEL)),
                  _const_spec((FN_WIDTH, D_MODEL)),
                  _const_spec((D_MODEL, D_MODEL)),
                  _const_spec((1, D_MODEL)),
                  _const_spec((ROUTER_ROWS, D_MODEL)),
                  _const_spec((ROUTER_ROWS, 1))],
        out_specs=[pl.BlockSpec((TM * TOKEN_ROWS, LANES), lambda i: (i, 0)),
                   pl.BlockSpec((1, TM), lambda i: (0, i)),
                   pl.BlockSpec((1, TM), lambda i: (0, i)), _const_spec((SEG_ROWS, LANES))],
        out_shape=[jax.ShapeDtypeStruct((n * TOKEN_ROWS, LANES), F32),
                   jax.ShapeDtypeStruct((1, n), jnp.int32),
                   jax.ShapeDtypeStruct((1, n), jnp.int32),
                   jax.ShapeDtypeStruct((SEG_ROWS, LANES), jnp.int32)],
        scratch_shapes=[pltpu.VMEM((SEG_ROWS, LANES), F32)],
        compiler_params=_cparams(("arbitrary",)),
        name="merge",
    )(x2d, attn2d, fm2d, sga, sgf, mod3, wts["w_ao"], wts["w_fn"], wts["w_out"], wts["g2"],
      wts["w_rg"], wts["b_rg"])


def _token_rows(ref, t):
    start = t * TOKEN_ROWS
    if not isinstance(t, int):
        start = pl.multiple_of(start, TOKEN_ROWS)
    return ref.at[pl.ds(start, TOKEN_ROWS)]


def _row_copy(src, dst, src_tok, dst_tok, sem):
    return pltpu.make_async_copy(_token_rows(src, src_tok), _token_rows(dst, dst_tok), sem)


def _combine_kernel(pos_ref, y_hbm, o_ref, buf, sem):
    i = pl.program_id(0)
    tm = o_ref.shape[0]

    def gather(tile_idx, slot):
        def start(r, c):
            _row_copy(y_hbm, buf.at[slot], pos_ref[tile_idx * tm + r], r, sem.at[slot]).start()
            return c

        lax.fori_loop(0, tm, start, 0, unroll=8)

    @pl.when(i == 0)
    def _():
        gather(0, 0)

    @pl.when(i + 1 < pl.num_programs(0))
    def _():
        gather(i + 1, (i + 1) % 2)

    slot = i % 2

    def wait(r, c):
        _row_copy(y_hbm, buf.at[slot], 0, 0, sem.at[slot]).wait()
        return c

    lax.fori_loop(0, tm, wait, 0, unroll=8)
    o_ref[...] = _load_token_major(buf.at[slot], tm)


def _combine(y_sorted, pos, n):
    return pl.pallas_call(
        _combine_kernel,
        grid_spec=pltpu.PrefetchScalarGridSpec(
            num_scalar_prefetch=1,
            grid=(n // TM,),
            in_specs=[pl.BlockSpec(memory_space=pl.ANY)],
            out_specs=pl.BlockSpec((TM, D_MODEL), lambda i, pos: (i, 0)),
            scratch_shapes=[pltpu.VMEM((2, TM * TOKEN_ROWS, LANES), F32),
                            pltpu.SemaphoreType.DMA((2,))]),
        out_shape=jax.ShapeDtypeStruct((n, D_MODEL), F32),
        compiler_params=_cparams(("arbitrary",)),
        name="moe_combine",
    )(pos, y_sorted)


def _moe_kernel(grp_ref, modrow_ref, on_ref, src_ref, x_hbm, mod_ref, g2_ref, wrg_ref, brg_ref,
                wre_ref, bre_ref, wg_ref, wu_ref, wd_ref, o_ref, xbuf, sem, *, tile):
    i = pl.program_id(0)

    def gather(tile_idx, slot):
        def start(r, c):
            _row_copy(x_hbm, xbuf.at[slot], src_ref[tile_idx * tile + r], r, sem.at[slot]).start()
            return c

        lax.fori_loop(0, tile, start, 0, unroll=8)

    @pl.when(jnp.logical_and(i == 0, on_ref[0] == 1))
    def _():
        gather(0, 0)

    nxt = jnp.minimum(i + 1, pl.num_programs(0) - 1)

    @pl.when(jnp.logical_and(i + 1 < pl.num_programs(0), on_ref[nxt] == 1))
    def _():
        gather(i + 1, (i + 1) % 2)

    @pl.when(on_ref[i] == 0)
    def _():
        o_ref[...] = jnp.zeros_like(o_ref)

    @pl.when(on_ref[i] == 1)
    def _():
        slot = i % 2

        def wait(r, c):
            _row_copy(x_hbm, xbuf.at[slot], 0, 0, sem.at[slot]).wait()
            return c

        lax.fori_loop(0, tile, wait, 0, unroll=8)
        x1 = _load_token_major(xbuf.at[slot], tile)
        h2 = _modulate2(x1, g2_ref, mod_ref)
        h2_hi, h2_lo = _split_bf16(h2)

        g = _rows(_router_logits(wrg_ref[...], brg_ref[...], h2_hi, h2_lo), N_GROUPS)
        gmax = functools.reduce(jnp.maximum, g)
        p_top = 1.0 / functools.reduce(lambda p, q: p + q, [jnp.exp(gj - gmax) for gj in g])
        e = _rows(_router_logits(wre_ref[0], bre_ref[0], h2_hi, h2_lo), EXPERTS_PER_GROUP)
        m1 = functools.reduce(jnp.maximum, e)
        i1 = _first_argmax(e, m1)
        rest = [jnp.where(i1 == j, -jnp.inf, e[j]) for j in range(EXPERTS_PER_GROUP)]
        m2 = functools.reduce(jnp.maximum, rest)
        i2 = _first_argmax(rest, m2)
        t = jnp.exp(m2 - m1)
        w1 = p_top / (1.0 + t)
        w2 = p_top * t / (1.0 + t)
        row = lax.broadcasted_iota(jnp.int32, (LANES, tile), 0)
        comb = (jnp.where(row == i1, w1, 0.0) + jnp.where(row == i2, w2, 0.0)).T

        acc = None
        for j in range(EXPERTS_PER_GROUP):
            a = _dot(h2_hi, wg_ref[j])
            u = _dot(h2_hi, wu_ref[j])
            act = (a * _sigmoid(a)) * u * comb[:, j:j + 1]
            y = _dot(act.astype(BF16), wd_ref[j])
            acc = y if acc is None else acc + y
        _store_token_major(o_ref, x1 + mod_ref[0, 5:6, :] * acc, tile)


def _moe(x1tm, src, maps, mod3, wts, w_gate, w_up, w_down, tile):
    n_tiles = src.shape[0] // tile
    const = lambda shape: pl.BlockSpec(shape, lambda i, grp, mrow, on, src: (0,) * len(shape))
    by_group = lambda shape: pl.BlockSpec(
        shape, lambda i, grp, mrow, on, src: (grp[i],) + (0,) * (len(shape) - 1))
    return pl.pallas_call(
        functools.partial(_moe_kernel, tile=tile),
        grid_spec=pltpu.PrefetchScalarGridSpec(
            num_scalar_prefetch=4,
            grid=(n_tiles,),
            in_specs=[pl.BlockSpec(memory_space=pl.ANY),
                      pl.BlockSpec((1, 6, D_MODEL), lambda i, grp, mrow, on, src: (mrow[i], 0, 0)),
                      const((1, D_MODEL)),
                      const((ROUTER_ROWS, D_MODEL)), const((ROUTER_ROWS, 1)),
                      by_group((1, ROUTER_ROWS, D_MODEL)), by_group((1, ROUTER_ROWS, 1)),
                      by_group((EXPERTS_PER_GROUP, D_MODEL, D_EXPERT)),
                      by_group((EXPERTS_PER_GROUP, D_MODEL, D_EXPERT)),
                      by_group((EXPERTS_PER_GROUP, D_EXPERT, D_MODEL))],
            out_specs=pl.BlockSpec((tile * TOKEN_ROWS, LANES), lambda i, grp, mrow, on, src: (i, 0)),
            scratch_shapes=[pltpu.VMEM((2, tile * TOKEN_ROWS, LANES), F32),
                            pltpu.SemaphoreType.DMA((2,))]),
        out_shape=jax.ShapeDtypeStruct((n_tiles * tile * TOKEN_ROWS, LANES), F32),
        compiler_params=_cparams(("arbitrary",)),
        name="moe",
    )(*maps, src, x1tm, mod3, wts["g2"], wts["w_rg"], wts["b_rg"], wts["w_re"], wts["b_re"],
      w_gate, w_up, w_down)


def _sort_plan(seg, rank, cnt, n_mod, row_base, tile, n_tiles):
    n = seg.shape[0]
    n_tile_seg = (cnt + tile - 1) // tile
    tile_end = jnp.cumsum(n_tile_seg)
    tile_start = tile_end - n_tile_seg
    total = tile_end[-1]
    pos = ((tile_start * tile)[seg] + rank).astype(jnp.int32)
    src = jnp.zeros((n_tiles * tile,), jnp.int32).at[pos].set(jnp.arange(n, dtype=jnp.int32))
    i = jnp.arange(n_tiles, dtype=jnp.int32)
    tile_seg = jnp.sum((jnp.minimum(i, total - 1)[:, None] >= tile_end[None, :]).astype(jnp.int32),
                       axis=1)
    maps = ((tile_seg // n_mod).astype(jnp.int32), (row_base + tile_seg % n_mod).astype(jnp.int32),
            (i < total).astype(jnp.int32))
    return pos, src, maps


def _sparse_moe(x1tm, seg, rank, cnt8, mod3, wts, w_gate, w_up, w_down, n_mod, row_base, tile):
    n = x1tm.shape[0] // TOKEN_ROWS
    n_seg = N_GROUPS * n_mod
    n_tiles = n // tile + n_seg
    pos, src, maps = _sort_plan(seg.reshape(n), rank.reshape(n), cnt8[:n_seg, 0], n_mod, row_base,
                                tile, n_tiles)
    y_sorted = _moe(x1tm, src, maps, mod3, wts, w_gate, w_up, w_down, tile)
    return _combine(y_sorted, pos, n)


def _pack_weights(l, w_in, norm1_g, q_a_norm_g, w_q_b, kv_a_norm_g, w_kv_b, q_norm_g, k_norm_g,
                  w_attn_o, w_fnet, w_out, norm2_g, w_router_group, b_router_group,
                  w_router_expert, b_router_expert):
    wi = w_in[l]
    kpe_cols = jnp.pad(wi[:, C_KPE:C_KPE + QK_ROPE], ((0, 0), (QK_NOPE, LANES - QK_HEAD)))
    w_in_p = jnp.concatenate([wi[:, :C_KPE], kpe_cols, wi[:, C_KPE + QK_ROPE:]], axis=1)
    w_qb = jnp.pad(w_q_b[l].reshape(Q_LORA, N_HEADS, QK_HEAD),
                   ((0, 0), (0, 0), (0, HEAD_SLOT - QK_HEAD))).reshape(Q_LORA, QK_WIDTH)
    wkv = w_kv_b[l].reshape(KV_LORA, N_HEADS, QK_NOPE + V_HEAD)
    w_kb = jnp.pad(wkv[:, :, :QK_NOPE],
                   ((0, 0), (0, 0), (0, HEAD_SLOT - QK_NOPE))).reshape(KV_LORA, QK_WIDTH)
    w_vb = wkv[:, :, QK_NOPE:].reshape(KV_LORA, V_WIDTH)
    pad_g = lambda g: jnp.pad(g, (0, HEAD_SLOT - QK_HEAD)).reshape(1, HEAD_SLOT)
    w_rg = jnp.pad(w_router_group[l].T, ((0, ROUTER_ROWS - N_GROUPS), (0, 0)))
    b_rg = jnp.pad(b_router_group[l], (0, ROUTER_ROWS - N_GROUPS)).reshape(ROUTER_ROWS, 1)
    w_re = jnp.pad(w_router_expert[l].T.reshape(N_GROUPS, EXPERTS_PER_GROUP, D_MODEL),
                   ((0, 0), (0, ROUTER_ROWS - EXPERTS_PER_GROUP), (0, 0)))
    b_re = jnp.pad(b_router_expert[l].reshape(N_GROUPS, EXPERTS_PER_GROUP),
                   ((0, 0), (0, ROUTER_ROWS - EXPERTS_PER_GROUP))).reshape(N_GROUPS, ROUTER_ROWS, 1)
    dft_c, dft_ns = _dft_tables(FN_GROUP_W)
    return {
        "g1": norm1_g[l].reshape(1, D_MODEL),
        "w_in": w_in_p.astype(BF16),
        "qag": q_a_norm_g[l].reshape(1, Q_LORA),
        "w_qb": w_qb.astype(BF16),
        "kvg": kv_a_norm_g[l].reshape(1, KV_LORA),
        "w_kb": w_kb.astype(BF16),
        "w_vb": w_vb.astype(BF16),
        "qg": pad_g(q_norm_g[l]),
        "kg": pad_g(k_norm_g[l]),
        "dft_c": jnp.concatenate([jnp.asarray(dft_c), -jnp.asarray(dft_ns)], axis=1).astype(BF16),
        "w_ao": w_attn_o[l].astype(BF16),
        "w_fn": w_fnet[l].astype(BF16),
        "w_out": w_out[l].astype(BF16),
        "g2": norm2_g[l].reshape(1, D_MODEL),
        "w_rg": w_rg,
        "b_rg": b_rg,
        "w_re": w_re,
        "b_re": b_re,
    }


def _layer(xp, xs, cache_ckv_l, cache_kpe_l, mod3, wts, w_gate, w_up, w_down):
    bp, sp, _ = xp.shape
    bs, ss, _ = xs.shape
    past = cache_ckv_l.shape[1]
    ctx_row = lambda i: 0
    lat_row = lambda i: 1 + i // (ss // TM)

    xp2 = xp.reshape(bp * sp, D_MODEL)
    q, k, v, fcs, sga, sgf, ckv, kpe = _inproj(xp2, mod3, ctx_row, wts, None, True)
    attn = _attention(q.reshape(bp, sp, QK_WIDTH), [k.reshape(bp, sp, QK_WIDTH)],
                      [v.reshape(bp, sp, V_WIDTH)], sp, "attn_ctx")
    cs, ns = (jnp.asarray(t).astype(BF16) for t in _dft_tables(sp))
    fm = _fourier(fcs.reshape(bp, sp, 2 * FN_WIDTH), cs, ns, sp, "fourier_ctx")
    x1tm, seg, rank, cnt = _merge(xp2, attn.reshape(bp * sp, V_WIDTH), fm.reshape(bp * sp, FN_WIDTH),
                                  sga, sgf, mod3, ctx_row, wts, 1)
    yp = _sparse_moe(x1tm, seg, rank, cnt, mod3, wts, w_gate, w_up, w_down, 1, 0,
                     MOE_TILE_CTX).reshape(bp, sp, D_MODEL)

    xs2 = xs.reshape(bs * ss, D_MODEL)
    rope_tabs = tuple(jnp.asarray(t) for t in _rope_tables(ss))
    q, k, v, fcs, sga, sgf = _inproj(xs2, mod3, lat_row, wts, rope_tabs, False)
    kpe_slot = jnp.pad(cache_kpe_l, ((0, 0), (0, 0), (QK_NOPE, LANES - QK_HEAD)))
    kc, vc = _cache_kv(cache_ckv_l.reshape(bs * past, KV_LORA), kpe_slot.reshape(bs * past, LANES), wts)
    attn = _attention(q.reshape(bs, ss, QK_WIDTH),
                      [kc.reshape(bs, past, QK_WIDTH), k.reshape(bs, ss, QK_WIDTH)],
                      [vc.reshape(bs, past, V_WIDTH), v.reshape(bs, ss, V_WIDTH)], TM, "attn_lat")
    cs, ns = (jnp.asarray(t).astype(BF16) for t in _dft_tables(ss))
    fm = _fourier(fcs.reshape(bs, ss, 2 * FN_WIDTH), cs, ns, TM, "fourier_lat")
    assert N_GROUPS * bs <= SEG_ROWS
    x1tm, seg, rank, cnt = _merge(xs2, attn.reshape(bs * ss, V_WIDTH), fm.reshape(bs * ss, FN_WIDTH),
                                  sga, sgf, mod3, lat_row, wts, bs)
    ys = _sparse_moe(x1tm, seg, rank, cnt, mod3, wts, w_gate, w_up, w_down, bs, 1,
                     MOE_TILE_LAT).reshape(bs, ss, D_MODEL)

    return yp, ys, ckv.reshape(bp, sp, KV_LORA), kpe.reshape(bp, sp, QK_ROPE)


def kernel(x_prompt, x_sample, cache_ckv, cache_kpe, c, c_ctx, w_mod, b_mod, norm1_g, w_in, q_a_norm_g, w_q_b, kv_a_norm_g, w_kv_b, q_norm_g, k_norm_g, w_attn_o, w_fnet, w_out, norm2_g, w_router_group, b_router_group, w_router_expert, b_router_expert, w_exp_gate, w_exp_up, w_exp_down):
    depth = w_mod.shape[0]
    n_lat = c.shape[0]
    assert 1 + n_lat <= MOD_ROWS
    cond8 = jnp.concatenate([c_ctx[None, :], c, jnp.zeros((MOD_ROWS - 1 - n_lat, D_MODEL), F32)], axis=0)
    xp, xs = x_prompt, x_sample
    ckv_layers, kpe_layers = [], []
    for l in range(depth):
        mod3 = _adaln(cond8, w_mod[l], b_mod[l].reshape(1, -1)).reshape(MOD_ROWS, 6, D_MODEL)
        wts = _pack_weights(l, w_in, norm1_g, q_a_norm_g, w_q_b, kv_a_norm_g, w_kv_b, q_norm_g,
                            k_norm_g, w_attn_o, w_fnet, w_out, norm2_g, w_router_group,
                            b_router_group, w_router_expert, b_router_expert)
        xp, xs, ckv, kpe = _layer(xp, xs, cache_ckv[:, l], cache_kpe[:, l], mod3, wts,
                                  w_exp_gate[l].astype(BF16), w_exp_up[l].astype(BF16),
                                  w_exp_down[l].astype(BF16))
        ckv_layers.append(ckv)
        kpe_layers.append(kpe)
    return xp, xs, jnp.stack(ckv_layers, axis=1), jnp.stack(kpe_layers, axis=1)
```

```python
import functools
import math

import numpy as np
import jax
import jax.numpy as jnp
from jax import lax
from jax.experimental import pallas as pl
from jax.experimental.pallas import tpu as pltpu

D_MODEL = 1024
GRID_W = 64
N_HEADS = 8
Q_LORA = 512
KV_LORA = 256
QK_NOPE = 64
QK_ROPE = 32
V_HEAD = 64
QK_HEAD = QK_NOPE + QK_ROPE
ATTN_SCALE = QK_HEAD ** -0.5
ROPE_BASE = 10000.0
FN_GROUPS = 4
FN_GROUP_W = 128
FN_WIDTH = FN_GROUPS * FN_GROUP_W
N_GROUPS = 4
EXPERTS_PER_GROUP = 4
N_EXPERTS = N_GROUPS * EXPERTS_PER_GROUP
D_EXPERT = 512
EPS = 1e-6

LANES = 128
HEAD_SLOT = LANES
QK_WIDTH = N_HEADS * HEAD_SLOT
V_WIDTH = N_HEADS * V_HEAD
C_QA = 0
C_KVA = C_QA + Q_LORA
C_KPE = C_KVA + KV_LORA
C_FN = C_KPE + LANES
C_GA = C_FN + FN_WIDTH
C_GF = C_GA + D_MODEL
IN_PACKED = C_GF + D_MODEL
SUBLANES = 8
ROUTER_ROWS = SUBLANES
MOD_ROWS = SUBLANES
VMEM_LIMIT = 56 * 1024 * 1024

TM = 512
TQ_LAT = 256
SEQ_PER_STEP_CTX = 4
SEG_ROWS = SUBLANES
TOKEN_ROWS = D_MODEL // LANES
MOE_TILE_CTX = 256
MOE_TILE_LAT = 128

BF16 = jnp.bfloat16
F32 = jnp.float32


def _cparams(sem):
    return pltpu.CompilerParams(dimension_semantics=sem, vmem_limit_bytes=VMEM_LIMIT)


def _dot(a, b):
    return jnp.dot(a, b, preferred_element_type=F32)


def _dot_nt(a, b):
    return lax.dot_general(a, b, (((1,), (1,)), ((), ())), preferred_element_type=F32)


def _sigmoid(x):
    return 1.0 / (1.0 + jnp.exp(-x))


@functools.lru_cache(maxsize=None)
def _rope_tables(n_pos):
    half = QK_ROPE // 2
    quarter = half // 2
    freqs = ROPE_BASE ** (-np.arange(quarter, dtype=np.float64) / quarter)
    pos = np.arange(n_pos)
    row = (pos // GRID_W).astype(np.float64)
    col = (pos % GRID_W).astype(np.float64)
    cos_t = np.ones((n_pos, LANES), np.float64)
    sin_a = np.zeros((n_pos, LANES), np.float64)
    sin_b = np.zeros((n_pos, LANES), np.float64)
    for base, p in ((QK_NOPE, row), (QK_NOPE + half, col)):
        ang = p[:, None] * freqs[None, :]
        cos_t[:, base:base + quarter] = np.cos(ang)
        cos_t[:, base + quarter:base + half] = np.cos(ang)
        sin_a[:, base:base + quarter] = -np.sin(ang)
        sin_b[:, base + quarter:base + half] = np.sin(ang)
    return (cos_t.astype(np.float32), sin_a.astype(np.float32), sin_b.astype(np.float32))


@functools.lru_cache(maxsize=None)
def _dft_tables(n):
    k = np.arange(n)
    ang = 2.0 * np.pi * ((k[:, None] * k[None, :]) % n) / n
    s = 1.0 / math.sqrt(n)
    return (np.cos(ang) * s).astype(np.float32), (-np.sin(ang) * s).astype(np.float32)


def _adaln_kernel(cond_ref, w_ref, b_ref, o_ref):
    c = cond_ref[...]
    s = c * _sigmoid(c)
    o_ref[...] = jnp.dot(s, w_ref[...], preferred_element_type=F32,
                         precision=lax.Precision.HIGHEST) + b_ref[...]


def _adaln(cond8, w_mod, b_mod):
    n = w_mod.shape[1]
    tn = D_MODEL
    return pl.pallas_call(
        _adaln_kernel,
        grid=(n // tn,),
        in_specs=[pl.BlockSpec((MOD_ROWS, D_MODEL), lambda j: (0, 0)),
                  pl.BlockSpec((D_MODEL, tn), lambda j: (0, j)),
                  pl.BlockSpec((1, tn), lambda j: (0, j))],
        out_specs=pl.BlockSpec((MOD_ROWS, tn), lambda j: (0, j)),
        out_shape=jax.ShapeDtypeStruct((MOD_ROWS, n), F32),
        compiler_params=_cparams(("arbitrary",)),
        name="adaln",
    )(cond8, w_mod, b_mod)


def _rms(x, width):
    return lax.rsqrt(jnp.sum(x * x, axis=-1, keepdims=True) * (1.0 / width) + EPS)


def _rope(x, cos_t, sin_a, sin_b):
    return x * cos_t + pltpu.roll(x, LANES - 8, 1) * sin_a + pltpu.roll(x, 8, 1) * sin_b


def _inproj_kernel(*refs, rope, emit_cache):
    it = iter(refs)
    x_ref, mod_ref, g1_ref, win_ref, qag_ref, wqb_ref, kvg_ref, wkb_ref, wvb_ref = (
        next(it) for _ in range(9))
    qg_ref, kg_ref, dft_ref = next(it), next(it), next(it)
    if rope:
        cos_ref, sa_ref, sb_ref = next(it), next(it), next(it)
    q_ref, k_ref, v_ref, fcs_ref, sga_ref, sgf_ref = (next(it) for _ in range(6))
    if emit_cache:
        ckv_ref, kpe_ref = next(it), next(it)

    x = x_ref[...]
    shift = mod_ref[0, 0:1, :]
    scale = mod_ref[0, 1:2, :]
    h = (x * _rms(x, D_MODEL) * g1_ref[...]) * (1.0 + scale) + shift
    hb = h.astype(BF16)

    if rope:
        cos_t, sin_a, sin_b = cos_ref[...], sa_ref[...], sb_ref[...]

    qa = _dot(hb, win_ref[:, C_QA:C_QA + Q_LORA])
    qn = (qa * _rms(qa, Q_LORA) * qag_ref[...]).astype(BF16)
    q = _dot(qn, wqb_ref[...])
    qg = qg_ref[...] * ATTN_SCALE
    for hd in range(N_HEADS):
        qh = q[:, hd * HEAD_SLOT:(hd + 1) * HEAD_SLOT]
        qh = qh * _rms(qh, QK_HEAD) * qg
        if rope:
            qh = _rope(qh, cos_t, sin_a, sin_b)
        q_ref[:, hd * HEAD_SLOT:(hd + 1) * HEAD_SLOT] = qh.astype(BF16)

    kva = _dot(hb, win_ref[:, C_KVA:C_KVA + KV_LORA])
    ckv = kva * _rms(kva, KV_LORA) * kvg_ref[...]
    kpe = _dot(hb, win_ref[:, C_KPE:C_KPE + LANES])
    if emit_cache:
        ckv_ref[...] = ckv
        kpe_ref[...] = kpe[:, QK_NOPE:QK_NOPE + QK_ROPE]
    _emit_kv(ckv.astype(BF16), kpe, wkb_ref, wvb_ref, kg_ref,
             (cos_t, sin_a, sin_b) if rope else None, k_ref, v_ref)

    fn = _dot(hb, win_ref[:, C_FN:C_FN + FN_WIDTH]).astype(BF16)
    for g in range(FN_GROUPS):
        cs = _dot(fn[:, g * FN_GROUP_W:(g + 1) * FN_GROUP_W], dft_ref[...])
        fcs_ref[:, g * FN_GROUP_W:(g + 1) * FN_GROUP_W] = cs[:, :FN_GROUP_W].astype(BF16)
        fcs_ref[:, FN_WIDTH + g * FN_GROUP_W:FN_WIDTH + (g + 1) * FN_GROUP_W] = (
            cs[:, FN_GROUP_W:].astype(BF16))

    sga_ref[...] = _sigmoid(_dot(hb, win_ref[:, C_GA:C_GA + D_MODEL])).astype(BF16)
    sgf_ref[...] = _sigmoid(_dot(hb, win_ref[:, C_GF:C_GF + D_MODEL])).astype(BF16)


def _emit_kv(ckvb, kpe, wkb_ref, wvb_ref, kg_ref, rope_tabs, k_ref, v_ref):
    kg = kg_ref[...]
    v_ref[...] = _dot(ckvb, wvb_ref[...]).astype(BF16)
    kn = _dot(ckvb, wkb_ref[...])
    pe_ss = jnp.sum(kpe * kpe, axis=-1, keepdims=True)
    pe_g = kpe * kg
    if rope_tabs is not None:
        pe_g = _rope(pe_g, *rope_tabs)
    for hd in range(N_HEADS):
        knh = kn[:, hd * HEAD_SLOT:(hd + 1) * HEAD_SLOT]
        ss = jnp.sum(knh * knh, axis=-1, keepdims=True) + pe_ss
        r = lax.rsqrt(ss * (1.0 / QK_HEAD) + EPS)
        k_ref[:, hd * HEAD_SLOT:(hd + 1) * HEAD_SLOT] = ((knh * kg + pe_g) * r).astype(BF16)


def _const_spec(shape):
    return pl.BlockSpec(shape, lambda i: (0,) * len(shape))


def _inproj(x2d, mod3, mod_row_fn, wts, rope_tabs, emit_cache):
    n = x2d.shape[0]
    rope = rope_tabs is not None
    tiles_per_seq = None if not rope else rope_tabs[0].shape[0] // TM
    in_specs = [pl.BlockSpec((TM, D_MODEL), lambda i: (i, 0)),
                pl.BlockSpec((1, 6, D_MODEL), lambda i: (mod_row_fn(i), 0, 0)),
                _const_spec((1, D_MODEL)),
                _const_spec((D_MODEL, IN_PACKED)),
                _const_spec((1, Q_LORA)),
                _const_spec((Q_LORA, QK_WIDTH)),
                _const_spec((1, KV_LORA)),
                _const_spec((KV_LORA, QK_WIDTH)),
                _const_spec((KV_LORA, V_WIDTH)),
                _const_spec((1, HEAD_SLOT)),
                _const_spec((1, HEAD_SLOT)),
                _const_spec((FN_GROUP_W, 2 * FN_GROUP_W))]
    args = [x2d, mod3, wts["g1"], wts["w_in"], wts["qag"], wts["w_qb"], wts["kvg"],
            wts["w_kb"], wts["w_vb"], wts["qg"], wts["kg"], wts["dft_c"]]
    if rope:
        in_specs += [pl.BlockSpec((TM, LANES), lambda i: (i % tiles_per_seq, 0))] * 3
        args += list(rope_tabs)
    out_shape = [jax.ShapeDtypeStruct((n, QK_WIDTH), BF16),
                 jax.ShapeDtypeStruct((n, QK_WIDTH), BF16),
                 jax.ShapeDtypeStruct((n, V_WIDTH), BF16),
                 jax.ShapeDtypeStruct((n, 2 * FN_WIDTH), BF16),
                 jax.ShapeDtypeStruct((n, D_MODEL), BF16),
                 jax.ShapeDtypeStruct((n, D_MODEL), BF16)]
    out_specs = [pl.BlockSpec((TM, s.shape[1]), lambda i: (i, 0)) for s in out_shape]
    if emit_cache:
        out_shape += [jax.ShapeDtypeStruct((n, KV_LORA), F32),
                      jax.ShapeDtypeStruct((n, QK_ROPE), F32)]
        out_specs += [pl.BlockSpec((TM, KV_LORA), lambda i: (i, 0)),
                      pl.BlockSpec((TM, QK_ROPE), lambda i: (i, 0))]
    return pl.pallas_call(
        functools.partial(_inproj_kernel, rope=rope, emit_cache=emit_cache),
        grid=(n // TM,),
        in_specs=in_specs,
        out_specs=out_specs,
        out_shape=out_shape,
        compiler_params=_cparams(("parallel",)),
        name="inproj_lat" if rope else "inproj_ctx",
    )(*args)


def _cache_kv_kernel(ckv_ref, kpe_ref, wkb_ref, wvb_ref, kg_ref, k_ref, v_ref):
    _emit_kv(ckv_ref[...].astype(BF16), kpe_ref[...], wkb_ref, wvb_ref, kg_ref, None, k_ref, v_ref)


def _cache_kv(ckv2d, kpe_slot2d, wts):
    n = ckv2d.shape[0]
    return pl.pallas_call(
        _cache_kv_kernel,
        grid=(n // TM,),
        in_specs=[pl.BlockSpec((TM, KV_LORA), lambda i: (i, 0)),
                  pl.BlockSpec((TM, LANES), lambda i: (i, 0)),
                  _const_spec((KV_LORA, QK_WIDTH)),
                  _const_spec((KV_LORA, V_WIDTH)),
                  _const_spec((1, HEAD_SLOT))],
        out_specs=[pl.BlockSpec((TM, QK_WIDTH), lambda i: (i, 0)),
                   pl.BlockSpec((TM, V_WIDTH), lambda i: (i, 0))],
        out_shape=[jax.ShapeDtypeStruct((n, QK_WIDTH), BF16),
                   jax.ShapeDtypeStruct((n, V_WIDTH), BF16)],
        compiler_params=_cparams(("parallel",)),
        name="cache_kv",
    )(ckv2d, kpe_slot2d, wts["w_kb"], wts["w_vb"], wts["kg"])


def _attn_kernel(*refs, n_kv, n_b):
    q_ref = refs[0]
    k_refs = refs[1:1 + n_kv]
    v_refs = refs[1 + n_kv:1 + 2 * n_kv]
    o_ref = refs[1 + 2 * n_kv]
    for b in range(n_b):
        for hd in range(N_HEADS):
            qh = q_ref[b, :, hd * HEAD_SLOT:(hd + 1) * HEAD_SLOT]
            s = [_dot_nt(qh, k[b, :, hd * HEAD_SLOT:(hd + 1) * HEAD_SLOT]) for k in k_refs]
            m = s[0].max(axis=-1, keepdims=True)
            for sj in s[1:]:
                m = jnp.maximum(m, sj.max(axis=-1, keepdims=True))
            p = [jnp.exp(sj - m) for sj in s]
            l = p[0].sum(axis=-1, keepdims=True)
            for pj in p[1:]:
                l = l + pj.sum(axis=-1, keepdims=True)
            o = _dot(p[0].astype(BF16), v_refs[0][b, :, hd * V_HEAD:(hd + 1) * V_HEAD])
            for pj, v in zip(p[1:], v_refs[1:]):
                o = o + _dot(pj.astype(BF16), v[b, :, hd * V_HEAD:(hd + 1) * V_HEAD])
            o_ref[b, :, hd * V_HEAD:(hd + 1) * V_HEAD] = (o / l).astype(BF16)


def _attention(q3, ks, vs, tq, n_b, name):
    b, sq, _ = q3.shape
    n_kv = len(ks)
    in_specs = [pl.BlockSpec((n_b, tq, QK_WIDTH), lambda bi, qi: (bi, qi, 0))]
    in_specs += [pl.BlockSpec((n_b, k.shape[1], QK_WIDTH), lambda bi, qi: (bi, 0, 0)) for k in ks]
    in_specs += [pl.BlockSpec((n_b, v.shape[1], V_WIDTH), lambda bi, qi: (bi, 0, 0)) for v in vs]
    return pl.pallas_call(
        functools.partial(_attn_kernel, n_kv=n_kv, n_b=n_b),
        grid=(b // n_b, sq // tq),
        in_specs=in_specs,
        out_specs=pl.BlockSpec((n_b, tq, V_WIDTH), lambda bi, qi: (bi, qi, 0)),
        out_shape=jax.ShapeDtypeStruct((b, sq, V_WIDTH), BF16),
        compiler_params=_cparams(("parallel", "parallel")),
        name=name,
    )(q3, *ks, *vs)


def _fourier_kernel(cs_ref, ns_ref, f_ref, o_ref, *, n_b):
    for b in range(n_b):
        xc = f_ref[b, :, :FN_WIDTH]
        xs = f_ref[b, :, FN_WIDTH:]
        o_ref[b] = (_dot(cs_ref[...], xc) + _dot(ns_ref[...], xs)).astype(BF16)


def _fourier(fcs3, cs, ns, tr, n_b, name):
    b, s, _ = fcs3.shape
    return pl.pallas_call(
        functools.partial(_fourier_kernel, n_b=n_b),
        grid=(b // n_b, s // tr),
        in_specs=[pl.BlockSpec((tr, s), lambda bi, ri: (ri, 0)),
                  pl.BlockSpec((tr, s), lambda bi, ri: (ri, 0)),
                  pl.BlockSpec((n_b, s, 2 * FN_WIDTH), lambda bi, ri: (bi, 0, 0))],
        out_specs=pl.BlockSpec((n_b, tr, FN_WIDTH), lambda bi, ri: (bi, ri, 0)),
        out_shape=jax.ShapeDtypeStruct((b, s, FN_WIDTH), BF16),
        compiler_params=_cparams(("parallel", "parallel")),
        name=name,
    )(cs, ns, fcs3)


def _modulate2(x1, g2_ref, mod_ref):
    return (x1 * _rms(x1, D_MODEL) * g2_ref[...]) * (1.0 + mod_ref[0, 4:5, :]) + mod_ref[0, 3:4, :]


def _split_bf16(x):
    hi = x.astype(BF16)
    return hi, (x - hi.astype(F32)).astype(BF16)


def _router_logits(w, b, h_hi, h_lo):
    w_hi, w_lo = _split_bf16(w)
    return (_dot_nt(w_hi, h_hi) + _dot_nt(w_lo, h_hi) + _dot_nt(w_hi, h_lo)) + b


def _rows(x, n):
    return [x[j:j + 1, :] for j in range(n)]


def _first_argmax(rows, top):
    idx = jnp.full(top.shape, len(rows) - 1, jnp.int32)
    for j in range(len(rows) - 2, -1, -1):
        idx = jnp.where(rows[j] == top, j, idx)
    return idx


def _store_token_major(ref, x, tm):
    for s in range(TOKEN_ROWS):
        ref[pl.ds(s, tm, stride=TOKEN_ROWS), :] = x[:, s * LANES:(s + 1) * LANES]


def _load_token_major(ref, tm):
    return jnp.concatenate([ref[pl.ds(s, tm, stride=TOKEN_ROWS), :] for s in range(TOKEN_ROWS)],
                           axis=1)


def _merge_kernel(x_ref, attn_ref, fm_ref, sga_ref, sgf_ref, mod_ref, wao_ref, wfn_ref, wout_ref,
                  g2_ref, wrg_ref, brg_ref, x1_ref, seg_ref, rank_ref, cnt_ref, carry_ref, *,
                  n_mod, tiles_per_seq):
    i = pl.program_id(0)

    @pl.when(i == 0)
    def _():
        carry_ref[...] = jnp.zeros_like(carry_ref)

    a = _dot(attn_ref[...], wao_ref[...])
    f = _dot(fm_ref[...], wfn_ref[...])
    u = sga_ref[...].astype(F32) * a + sgf_ref[...].astype(F32) * f
    y = _dot(u.astype(BF16), wout_ref[...])
    x1 = x_ref[...] + mod_ref[0, 2:3, :] * y
    tm = x1.shape[0]
    _store_token_major(x1_ref, x1, tm)

    h2_hi, h2_lo = _split_bf16(_modulate2(x1, g2_ref, mod_ref))
    g = _rows(_router_logits(wrg_ref[...], brg_ref[...], h2_hi, h2_lo), N_GROUPS)
    gidx = _first_argmax(g, functools.reduce(jnp.maximum, g))

    seg = gidx * n_mod
    if n_mod > 1:
        seg = seg + i // tiles_per_seq
    onehot = jnp.where(lax.broadcasted_iota(jnp.int32, (SEG_ROWS, tm), 0) == seg, 1.0, 0.0)
    before = (lax.broadcasted_iota(jnp.int32, (tm, tm), 0)
              < lax.broadcasted_iota(jnp.int32, (tm, tm), 1))
    prefix = _dot(onehot.astype(BF16), jnp.where(before, 1.0, 0.0).astype(BF16))
    carry = carry_ref[...]
    rank = jnp.sum(onehot * (prefix + carry[:, 0:1]), axis=0, keepdims=True)
    seg_ref[...] = seg
    rank_ref[...] = rank.astype(jnp.int32)
    carry = carry + jnp.sum(onehot, axis=1, keepdims=True)
    carry_ref[...] = carry
    cnt_ref[...] = carry.astype(jnp.int32)


def _merge(x2d, attn2d, fm2d, sga, sgf, mod3, mod_row_fn, wts, n_mod):
    n = x2d.shape[0]
    tok = lambda w: pl.BlockSpec((TM, w), lambda i: (i, 0))
    return pl.pallas_call(
        functools.partial(_merge_kernel, n_mod=n_mod, tiles_per_seq=n // n_mod // TM),
        grid=(n // TM,),
        in_specs=[tok(D_MODEL), tok(V_WIDTH), tok(FN_WIDTH), tok(D_MODEL), tok(D_MODEL),
                  pl.BlockSpec((1, 6, D_MODEL), lambda i: (mod_row_fn(i), 0, 0)),
                  _const_spec((V_WIDTH, D_MODEL)),
                  _const_spec((FN_WIDTH, D_MODEL)),
                  _const_spec((D_MODEL, D_MODEL)),
                  _const_spec((1, D_MODEL)),
                  _const_spec((ROUTER_ROWS, D_MODEL)),
                  _const_spec((ROUTER_ROWS, 1))],
        out_specs=[pl.BlockSpec((TM * TOKEN_ROWS, LANES), lambda i: (i, 0)),
                   pl.BlockSpec((1, TM), lambda i: (0, i)),
                   pl.BlockSpec((1, TM), lambda i: (0, i)), _const_spec((SEG_ROWS, LANES))],
        out_shape=[jax.ShapeDtypeStruct((n * TOKEN_ROWS, LANES), F32),
                   jax.ShapeDtypeStruct((1, n), jnp.int32),
                   jax.ShapeDtypeStruct((1, n), jnp.int32),
                   jax.ShapeDtypeStruct((SEG_ROWS, LANES), jnp.int32)],
        scratch_shapes=[pltpu.VMEM((SEG_ROWS, LANES), F32)],
        compiler_params=_cparams(("arbitrary",)),
        name="merge",
    )(x2d, attn2d, fm2d, sga, sgf, mod3, wts["w_ao"], wts["w_fn"], wts["w_out"], wts["g2"],
      wts["w_rg"], wts["b_rg"])


def _token_rows(ref, t):
    start = t * TOKEN_ROWS
    if not isinstance(t, int):
        start = pl.multiple_of(start, TOKEN_ROWS)
    return ref.at[pl.ds(start, TOKEN_ROWS)]


def _row_copy(src, dst, src_tok, dst_tok, sem):
    return pltpu.make_async_copy(_token_rows(src, src_tok), _token_rows(dst, dst_tok), sem)


def _combine_kernel(pos_ref, y_hbm, o_ref, buf, sem):
    i = pl.program_id(0)
    tm = o_ref.shape[0]

    def gather(tile_idx, slot):
        def start(r, c):
            _row_copy(y_hbm, buf.at[slot], pos_ref[tile_idx * tm + r], r, sem.at[slot]).start()
            return c

        lax.fori_loop(0, tm, start, 0, unroll=8)

    @pl.when(i == 0)
    def _():
        gather(0, 0)

    @pl.when(i + 1 < pl.num_programs(0))
    def _():
        gather(i + 1, (i + 1) % 2)

    slot = i % 2

    def wait(r, c):
        _row_copy(y_hbm, buf.at[slot], 0, 0, sem.at[slot]).wait()
        return c

    lax.fori_loop(0, tm, wait, 0, unroll=8)
    o_ref[...] = _load_token_major(buf.at[slot], tm)


def _combine(y_sorted, pos, n):
    return pl.pallas_call(
        _combine_kernel,
        grid_spec=pltpu.PrefetchScalarGridSpec(
            num_scalar_prefetch=1,
            grid=(n // TM,),
            in_specs=[pl.BlockSpec(memory_space=pl.ANY)],
            out_specs=pl.BlockSpec((TM, D_MODEL), lambda i, pos: (i, 0)),
            scratch_shapes=[pltpu.VMEM((2, TM * TOKEN_ROWS, LANES), F32),
                            pltpu.SemaphoreType.DMA((2,))]),
        out_shape=jax.ShapeDtypeStruct((n, D_MODEL), F32),
        compiler_params=_cparams(("arbitrary",)),
        name="moe_combine",
    )(pos, y_sorted)


def _moe_kernel(grp_ref, modrow_ref, on_ref, src_ref, x_hbm, mod_ref, g2_ref, wrg_ref, brg_ref,
                wre_ref, bre_ref, wg_ref, wu_ref, wd_ref, o_ref, xbuf, sem, *, tile):
    i = pl.program_id(0)

    def gather(tile_idx, slot):
        def start(r, c):
            _row_copy(x_hbm, xbuf.at[slot], src_ref[tile_idx * tile + r], r, sem.at[slot]).start()
            return c

        lax.fori_loop(0, tile, start, 0, unroll=8)

    @pl.when(jnp.logical_and(i == 0, on_ref[0] == 1))
    def _():
        gather(0, 0)

    nxt = jnp.minimum(i + 1, pl.num_programs(0) - 1)

    @pl.when(jnp.logical_and(i + 1 < pl.num_programs(0), on_ref[nxt] == 1))
    def _():
        gather(i + 1, (i + 1) % 2)

    @pl.when(on_ref[i] == 0)
    def _():
        o_ref[...] = jnp.zeros_like(o_ref)

    @pl.when(on_ref[i] == 1)
    def _():
        slot = i % 2

        def wait(r, c):
            _row_copy(x_hbm, xbuf.at[slot], 0, 0, sem.at[slot]).wait()
            return c

        lax.fori_loop(0, tile, wait, 0, unroll=8)
        x1 = _load_token_major(xbuf.at[slot], tile)
        h2 = _modulate2(x1, g2_ref, mod_ref)
        h2_hi, h2_lo = _split_bf16(h2)

        g = _rows(_router_logits(wrg_ref[...], brg_ref[...], h2_hi, h2_lo), N_GROUPS)
        gmax = functools.reduce(jnp.maximum, g)
        p_top = 1.0 / functools.reduce(lambda p, q: p + q, [jnp.exp(gj - gmax) for gj in g])
        e = _rows(_router_logits(wre_ref[0], bre_ref[0], h2_hi, h2_lo), EXPERTS_PER_GROUP)
        m1 = functools.reduce(jnp.maximum, e)
        i1 = _first_argmax(e, m1)
        rest = [jnp.where(i1 == j, -jnp.inf, e[j]) for j in range(EXPERTS_PER_GROUP)]
        m2 = functools.reduce(jnp.maximum, rest)
        i2 = _first_argmax(rest, m2)
        t = jnp.exp(m2 - m1)
        w1 = p_top / (1.0 + t)
        w2 = p_top * t / (1.0 + t)
        row = lax.broadcasted_iota(jnp.int32, (LANES, tile), 0)
        comb = (jnp.where(row == i1, w1, 0.0) + jnp.where(row == i2, w2, 0.0)).T

        acc = None
        for j in range(EXPERTS_PER_GROUP):
            a = _dot(h2_hi, wg_ref[j])
            u = _dot(h2_hi, wu_ref[j])
            act = (a * _sigmoid(a)) * u * comb[:, j:j + 1]
            y = _dot(act.astype(BF16), wd_ref[j])
            acc = y if acc is None else acc + y
        _store_token_major(o_ref, x1 + mod_ref[0, 5:6, :] * acc, tile)


def _moe(x1tm, src, maps, mod3, wts, w_gate, w_up, w_down, tile):
    n_tiles = src.shape[0] // tile
    const = lambda shape: pl.BlockSpec(shape, lambda i, grp, mrow, on, src: (0,) * len(shape))
    by_group = lambda shape: pl.BlockSpec(
        shape, lambda i, grp, mrow, on, src: (grp[i],) + (0,) * (len(shape) - 1))
    return pl.pallas_call(
        functools.partial(_moe_kernel, tile=tile),
        grid_spec=pltpu.PrefetchScalarGridSpec(
            num_scalar_prefetch=4,
            grid=(n_tiles,),
            in_specs=[pl.BlockSpec(memory_space=pl.ANY),
                      pl.BlockSpec((1, 6, D_MODEL), lambda i, grp, mrow, on, src: (mrow[i], 0, 0)),
                      const((1, D_MODEL)),
                      const((ROUTER_ROWS, D_MODEL)), const((ROUTER_ROWS, 1)),
                      by_group((1, ROUTER_ROWS, D_MODEL)), by_group((1, ROUTER_ROWS, 1)),
                      by_group((EXPERTS_PER_GROUP, D_MODEL, D_EXPERT)),
                      by_group((EXPERTS_PER_GROUP, D_MODEL, D_EXPERT)),
                      by_group((EXPERTS_PER_GROUP, D_EXPERT, D_MODEL))],
            out_specs=pl.BlockSpec((tile * TOKEN_ROWS, LANES), lambda i, grp, mrow, on, src: (i, 0)),
            scratch_shapes=[pltpu.VMEM((2, tile * TOKEN_ROWS, LANES), F32),
                            pltpu.SemaphoreType.DMA((2,))]),
        out_shape=jax.ShapeDtypeStruct((n_tiles * tile * TOKEN_ROWS, LANES), F32),
        compiler_params=_cparams(("arbitrary",)),
        name="moe",
    )(*maps, src, x1tm, mod3, wts["g2"], wts["w_rg"], wts["b_rg"], wts["w_re"], wts["b_re"],
      w_gate, w_up, w_down)


def _invert_kernel(pos_ref, src_ref, *, n):
    def zero(p, c):
        src_ref[p] = 0
        return c

    lax.fori_loop(0, src_ref.shape[0], zero, 0, unroll=8)

    def put(t, c):
        src_ref[pos_ref[t]] = t
        return c

    lax.fori_loop(0, n, put, 0, unroll=8)


def _invert(pos, n_sorted):
    return pl.pallas_call(
        functools.partial(_invert_kernel, n=pos.shape[0]),
        grid_spec=pltpu.PrefetchScalarGridSpec(
            num_scalar_prefetch=1, grid=(1,), in_specs=[],
            out_specs=pl.BlockSpec(memory_space=pltpu.SMEM)),
        out_shape=jax.ShapeDtypeStruct((n_sorted,), jnp.int32),
        name="moe_invert",
    )(pos)


def _sort_plan(seg, rank, cnt, n_mod, row_base, tile, n_tiles):
    n = seg.shape[0]
    n_tile_seg = (cnt + tile - 1) // tile
    tile_end = jnp.cumsum(n_tile_seg)
    tile_start = tile_end - n_tile_seg
    total = tile_end[-1]
    pos = ((tile_start * tile)[seg] + rank).astype(jnp.int32)
    src = _invert(pos, n_tiles * tile)
    i = jnp.arange(n_tiles, dtype=jnp.int32)
    tile_seg = jnp.sum((jnp.minimum(i, total - 1)[:, None] >= tile_end[None, :]).astype(jnp.int32),
                       axis=1)
    maps = ((tile_seg // n_mod).astype(jnp.int32), (row_base + tile_seg % n_mod).astype(jnp.int32),
            (i < total).astype(jnp.int32))
    return pos, src, maps


def _sparse_moe(x1tm, seg, rank, cnt8, mod3, wts, w_gate, w_up, w_down, n_mod, row_base, tile):
    n = x1tm.shape[0] // TOKEN_ROWS
    n_seg = N_GROUPS * n_mod
    n_tiles = n // tile + n_seg
    pos, src, maps = _sort_plan(seg.reshape(n), rank.reshape(n), cnt8[:n_seg, 0], n_mod, row_base,
                                tile, n_tiles)
    y_sorted = _moe(x1tm, src, maps, mod3, wts, w_gate, w_up, w_down, tile)
    return _combine(y_sorted, pos, n)


def _pack_weights(l, w_in, norm1_g, q_a_norm_g, w_q_b, kv_a_norm_g, w_kv_b, q_norm_g, k_norm_g,
                  w_attn_o, w_fnet, w_out, norm2_g, w_router_group, b_router_group,
                  w_router_expert, b_router_expert):
    wi = w_in[l]
    kpe_cols = jnp.pad(wi[:, C_KPE:C_KPE + QK_ROPE], ((0, 0), (QK_NOPE, LANES - QK_HEAD)))
    w_in_p = jnp.concatenate([wi[:, :C_KPE], kpe_cols, wi[:, C_KPE + QK_ROPE:]], axis=1)
    w_qb = jnp.pad(w_q_b[l].reshape(Q_LORA, N_HEADS, QK_HEAD),
                   ((0, 0), (0, 0), (0, HEAD_SLOT - QK_HEAD))).reshape(Q_LORA, QK_WIDTH)
    wkv = w_kv_b[l].reshape(KV_LORA, N_HEADS, QK_NOPE + V_HEAD)
    w_kb = jnp.pad(wkv[:, :, :QK_NOPE],
                   ((0, 0), (0, 0), (0, HEAD_SLOT - QK_NOPE))).reshape(KV_LORA, QK_WIDTH)
    w_vb = wkv[:, :, QK_NOPE:].reshape(KV_LORA, V_WIDTH)
    pad_g = lambda g: jnp.pad(g, (0, HEAD_SLOT - QK_HEAD)).reshape(1, HEAD_SLOT)
    w_rg = jnp.pad(w_router_group[l].T, ((0, ROUTER_ROWS - N_GROUPS), (0, 0)))
    b_rg = jnp.pad(b_router_group[l], (0, ROUTER_ROWS - N_GROUPS)).reshape(ROUTER_ROWS, 1)
    w_re = jnp.pad(w_router_expert[l].T.reshape(N_GROUPS, EXPERTS_PER_GROUP, D_MODEL),
                   ((0, 0), (0, ROUTER_ROWS - EXPERTS_PER_GROUP), (0, 0)))
    b_re = jnp.pad(b_router_expert[l].reshape(N_GROUPS, EXPERTS_PER_GROUP),
                   ((0, 0), (0, ROUTER_ROWS - EXPERTS_PER_GROUP))).reshape(N_GROUPS, ROUTER_ROWS, 1)
    dft_c, dft_ns = _dft_tables(FN_GROUP_W)
    return {
        "g1": norm1_g[l].reshape(1, D_MODEL),
        "w_in": w_in_p.astype(BF16),
        "qag": q_a_norm_g[l].reshape(1, Q_LORA),
        "w_qb": w_qb.astype(BF16),
        "kvg": kv_a_norm_g[l].reshape(1, KV_LORA),
        "w_kb": w_kb.astype(BF16),
        "w_vb": w_vb.astype(BF16),
        "qg": pad_g(q_norm_g[l]),
        "kg": pad_g(k_norm_g[l]),
        "dft_c": jnp.concatenate([jnp.asarray(dft_c), -jnp.asarray(dft_ns)], axis=1).astype(BF16),
        "w_ao": w_attn_o[l].astype(BF16),
        "w_fn": w_fnet[l].astype(BF16),
        "w_out": w_out[l].astype(BF16),
        "g2": norm2_g[l].reshape(1, D_MODEL),
        "w_rg": w_rg,
        "b_rg": b_rg,
        "w_re": w_re,
        "b_re": b_re,
    }


def _layer(xp, xs, cache_ckv_l, cache_kpe_l, mod3, wts, w_gate, w_up, w_down):
    bp, sp, _ = xp.shape
    bs, ss, _ = xs.shape
    past = cache_ckv_l.shape[1]
    ctx_row = lambda i: 0
    lat_row = lambda i: 1 + i // (ss // TM)

    xp2 = xp.reshape(bp * sp, D_MODEL)
    q, k, v, fcs, sga, sgf, ckv, kpe = _inproj(xp2, mod3, ctx_row, wts, None, True)
    attn = _attention(q.reshape(bp, sp, QK_WIDTH), [k.reshape(bp, sp, QK_WIDTH)],
                      [v.reshape(bp, sp, V_WIDTH)], sp, 1, "attn_ctx")
    cs, ns = (jnp.asarray(t).astype(BF16) for t in _dft_tables(sp))
    fm = _fourier(fcs.reshape(bp, sp, 2 * FN_WIDTH), cs, ns, sp, SEQ_PER_STEP_CTX, "fourier_ctx")
    x1tm, seg, rank, cnt = _merge(xp2, attn.reshape(bp * sp, V_WIDTH), fm.reshape(bp * sp, FN_WIDTH),
                                  sga, sgf, mod3, ctx_row, wts, 1)
    yp = _sparse_moe(x1tm, seg, rank, cnt, mod3, wts, w_gate, w_up, w_down, 1, 0,
                     MOE_TILE_CTX).reshape(bp, sp, D_MODEL)

    xs2 = xs.reshape(bs * ss, D_MODEL)
    rope_tabs = tuple(jnp.asarray(t) for t in _rope_tables(ss))
    q, k, v, fcs, sga, sgf = _inproj(xs2, mod3, lat_row, wts, rope_tabs, False)
    kpe_slot = jnp.pad(cache_kpe_l, ((0, 0), (0, 0), (QK_NOPE, LANES - QK_HEAD)))
    kc, vc = _cache_kv(cache_ckv_l.reshape(bs * past, KV_LORA), kpe_slot.reshape(bs * past, LANES), wts)
    attn = _attention(q.reshape(bs, ss, QK_WIDTH),
                      [kc.reshape(bs, past, QK_WIDTH), k.reshape(bs, ss, QK_WIDTH)],
                      [vc.reshape(bs, past, V_WIDTH), v.reshape(bs, ss, V_WIDTH)], TQ_LAT, 1, "attn_lat")
    cs, ns = (jnp.asarray(t).astype(BF16) for t in _dft_tables(ss))
    fm = _fourier(fcs.reshape(bs, ss, 2 * FN_WIDTH), cs, ns, TQ_LAT, 1, "fourier_lat")
    assert N_GROUPS * bs <= SEG_ROWS
    x1tm, seg, rank, cnt = _merge(xs2, attn.reshape(bs * ss, V_WIDTH), fm.reshape(bs * ss, FN_WIDTH),
                                  sga, sgf, mod3, lat_row, wts, bs)
    ys = _sparse_moe(x1tm, seg, rank, cnt, mod3, wts, w_gate, w_up, w_down, bs, 1,
                     MOE_TILE_LAT).reshape(bs, ss, D_MODEL)

    return yp, ys, ckv.reshape(bp, sp, KV_LORA), kpe.reshape(bp, sp, QK_ROPE)


def kernel(x_prompt, x_sample, cache_ckv, cache_kpe, c, c_ctx, w_mod, b_mod, norm1_g, w_in, q_a_norm_g, w_q_b, kv_a_norm_g, w_kv_b, q_norm_g, k_norm_g, w_attn_o, w_fnet, w_out, norm2_g, w_router_group, b_router_group, w_router_expert, b_router_expert, w_exp_gate, w_exp_up, w_exp_down):
    depth = w_mod.shape[0]
    n_lat = c.shape[0]
    assert 1 + n_lat <= MOD_ROWS
    cond8 = jnp.concatenate([c_ctx[None, :], c, jnp.zeros((MOD_ROWS - 1 - n_lat, D_MODEL), F32)], axis=0)
    xp, xs = x_prompt, x_sample
    ckv_layers, kpe_layers = [], []
    for l in range(depth):
        mod3 = _adaln(cond8, w_mod[l], b_mod[l].reshape(1, -1)).reshape(MOD_ROWS, 6, D_MODEL)
        wts = _pack_weights(l, w_in, norm1_g, q_a_norm_g, w_q_b, kv_a_norm_g, w_kv_b, q_norm_g,
                            k_norm_g, w_attn_o, w_fnet, w_out, norm2_g, w_router_group,
                            b_router_group, w_router_expert, b_router_expert)
        xp, xs, ckv, kpe = _layer(xp, xs, cache_ckv[:, l], cache_kpe[:, l], mod3, wts,
                                  w_exp_gate[l].astype(BF16), w_exp_up[l].astype(BF16),
                                  w_exp_down[l].astype(BF16))
        ckv_layers.append(ckv)
        kpe_layers.append(kpe)
    return xp, xs, jnp.stack(ckv_layers, axis=1), jnp.stack(kpe_layers, axis=1)
```

```python
import functools
import math

import numpy as np
import jax
import jax.numpy as jnp
from jax import lax
from jax.experimental import pallas as pl
from jax.experimental.pallas import tpu as pltpu

D_MODEL = 1024
GRID_W = 64
N_HEADS = 8
Q_LORA = 512
KV_LORA = 256
QK_NOPE = 64
QK_ROPE = 32
V_HEAD = 64
QK_HEAD = QK_NOPE + QK_ROPE
ATTN_SCALE = QK_HEAD ** -0.5
ROPE_BASE = 10000.0
FN_GROUPS = 4
FN_GROUP_W = 128
FN_WIDTH = FN_GROUPS * FN_GROUP_W
N_GROUPS = 4
EXPERTS_PER_GROUP = 4
N_EXPERTS = N_GROUPS * EXPERTS_PER_GROUP
D_EXPERT = 512
EPS = 1e-6

LANES = 128
HEAD_SLOT = LANES
QK_WIDTH = N_HEADS * HEAD_SLOT
V_WIDTH = N_HEADS * V_HEAD
C_QA = 0
C_KVA = C_QA + Q_LORA
C_KPE = C_KVA + KV_LORA
C_FN = C_KPE + LANES
C_GA = C_FN + FN_WIDTH
C_GF = C_GA + D_MODEL
IN_PACKED = C_GF + D_MODEL
SUBLANES = 8
ROUTER_ROWS = SUBLANES
MOD_ROWS = SUBLANES
VMEM_LIMIT = 56 * 1024 * 1024

TM = 512
TQ_LAT = 256
SEQ_PER_STEP_CTX = 4
SEG_ROWS = SUBLANES
TOKEN_ROWS = D_MODEL // LANES
MOE_TILE_CTX = 256
MOE_TILE_LAT = 256

BF16 = jnp.bfloat16
F32 = jnp.float32


def _cparams(sem):
    return pltpu.CompilerParams(dimension_semantics=sem, vmem_limit_bytes=VMEM_LIMIT)


def _dot(a, b):
    return jnp.dot(a, b, preferred_element_type=F32)


def _dot_nt(a, b):
    return lax.dot_general(a, b, (((1,), (1,)), ((), ())), preferred_element_type=F32)


def _sigmoid(x):
    return 1.0 / (1.0 + jnp.exp(-x))


@functools.lru_cache(maxsize=None)
def _rope_tables(n_pos):
    half = QK_ROPE // 2
    quarter = half // 2
    freqs = ROPE_BASE ** (-np.arange(quarter, dtype=np.float64) / quarter)
    pos = np.arange(n_pos)
    row = (pos // GRID_W).astype(np.float64)
    col = (pos % GRID_W).astype(np.float64)
    cos_t = np.ones((n_pos, LANES), np.float64)
    sin_a = np.zeros((n_pos, LANES), np.float64)
    sin_b = np.zeros((n_pos, LANES), np.float64)
    for base, p in ((QK_NOPE, row), (QK_NOPE + half, col)):
        ang = p[:, None] * freqs[None, :]
        cos_t[:, base:base + quarter] = np.cos(ang)
        cos_t[:, base + quarter:base + half] = np.cos(ang)
        sin_a[:, base:base + quarter] = -np.sin(ang)
        sin_b[:, base + quarter:base + half] = np.sin(ang)
    return (cos_t.astype(np.float32), sin_a.astype(np.float32), sin_b.astype(np.float32))


@functools.lru_cache(maxsize=None)
def _dft_tables(n):
    k = np.arange(n)
    ang = 2.0 * np.pi * ((k[:, None] * k[None, :]) % n) / n
    s = 1.0 / math.sqrt(n)
    return (np.cos(ang) * s).astype(np.float32), (-np.sin(ang) * s).astype(np.float32)


def _adaln_kernel(cond_ref, w_ref, b_ref, o_ref):
    c = cond_ref[...]
    s = c * _sigmoid(c)
    o_ref[...] = jnp.dot(s, w_ref[...], preferred_element_type=F32,
                         precision=lax.Precision.HIGHEST) + b_ref[...]


def _adaln(cond8, w_mod, b_mod):
    n = w_mod.shape[1]
    tn = D_MODEL
    return pl.pallas_call(
        _adaln_kernel,
        grid=(n // tn,),
        in_specs=[pl.BlockSpec((MOD_ROWS, D_MODEL), lambda j: (0, 0)),
                  pl.BlockSpec((D_MODEL, tn), lambda j: (0, j)),
                  pl.BlockSpec((1, tn), lambda j: (0, j))],
        out_specs=pl.BlockSpec((MOD_ROWS, tn), lambda j: (0, j)),
        out_shape=jax.ShapeDtypeStruct((MOD_ROWS, n), F32),
        compiler_params=_cparams(("arbitrary",)),
        name="adaln",
    )(cond8, w_mod, b_mod)


def _rms(x, width):
    return lax.rsqrt(jnp.sum(x * x, axis=-1, keepdims=True) * (1.0 / width) + EPS)


def _rope(x, cos_t, sin_a, sin_b):
    return x * cos_t + pltpu.roll(x, LANES - 8, 1) * sin_a + pltpu.roll(x, 8, 1) * sin_b


def _inproj_kernel(*refs, rope, emit_cache):
    it = iter(refs)
    x_ref, mod_ref, g1_ref, win_ref, qag_ref, wqb_ref, kvg_ref, wkb_ref, wvb_ref = (
        next(it) for _ in range(9))
    qg_ref, kg_ref, dft_ref = next(it), next(it), next(it)
    if rope:
        cos_ref, sa_ref, sb_ref = next(it), next(it), next(it)
    q_ref, k_ref, v_ref, fcs_ref, sga_ref, sgf_ref = (next(it) for _ in range(6))
    if emit_cache:
        ckv_ref, kpe_ref = next(it), next(it)

    x = x_ref[...]
    shift = mod_ref[0, 0:1, :]
    scale = mod_ref[0, 1:2, :]
    h = (x * _rms(x, D_MODEL) * g1_ref[...]) * (1.0 + scale) + shift
    hb = h.astype(BF16)

    if rope:
        cos_t, sin_a, sin_b = cos_ref[...], sa_ref[...], sb_ref[...]

    qa = _dot(hb, win_ref[:, C_QA:C_QA + Q_LORA])
    qn = (qa * _rms(qa, Q_LORA) * qag_ref[...]).astype(BF16)
    q = _dot(qn, wqb_ref[...])
    qg = qg_ref[...] * ATTN_SCALE
    for hd in range(N_HEADS):
        qh = q[:, hd * HEAD_SLOT:(hd + 1) * HEAD_SLOT]
        qh = qh * _rms(qh, QK_HEAD) * qg
        if rope:
            qh = _rope(qh, cos_t, sin_a, sin_b)
        q_ref[:, hd * HEAD_SLOT:(hd + 1) * HEAD_SLOT] = qh.astype(BF16)

    kva = _dot(hb, win_ref[:, C_KVA:C_KVA + KV_LORA])
    ckv = kva * _rms(kva, KV_LORA) * kvg_ref[...]
    kpe = _dot(hb, win_ref[:, C_KPE:C_KPE + LANES])
    if emit_cache:
        ckv_ref[...] = ckv
        kpe_ref[...] = kpe[:, QK_NOPE:QK_NOPE + QK_ROPE]
    _emit_kv(ckv.astype(BF16), kpe, wkb_ref, wvb_ref, kg_ref,
             (cos_t, sin_a, sin_b) if rope else None, k_ref, v_ref)

    fn = _dot(hb, win_ref[:, C_FN:C_FN + FN_WIDTH]).astype(BF16)
    for g in range(FN_GROUPS):
        cs = _dot(fn[:, g * FN_GROUP_W:(g + 1) * FN_GROUP_W], dft_ref[...])
        fcs_ref[:, g * FN_GROUP_W:(g + 1) * FN_GROUP_W] = cs[:, :FN_GROUP_W].astype(BF16)
        fcs_ref[:, FN_WIDTH + g * FN_GROUP_W:FN_WIDTH + (g + 1) * FN_GROUP_W] = (
            cs[:, FN_GROUP_W:].astype(BF16))

    sga_ref[...] = _sigmoid(_dot(hb, win_ref[:, C_GA:C_GA + D_MODEL])).astype(BF16)
    sgf_ref[...] = _sigmoid(_dot(hb, win_ref[:, C_GF:C_GF + D_MODEL])).astype(BF16)


def _emit_kv(ckvb, kpe, wkb_ref, wvb_ref, kg_ref, rope_tabs, k_ref, v_ref):
    kg = kg_ref[...]
    v_ref[...] = _dot(ckvb, wvb_ref[...]).astype(BF16)
    kn = _dot(ckvb, wkb_ref[...])
    pe_ss = jnp.sum(kpe * kpe, axis=-1, keepdims=True)
    pe_g = kpe * kg
    if rope_tabs is not None:
        pe_g = _rope(pe_g, *rope_tabs)
    for hd in range(N_HEADS):
        knh = kn[:, hd * HEAD_SLOT:(hd + 1) * HEAD_SLOT]
        ss = jnp.sum(knh * knh, axis=-1, keepdims=True) + pe_ss
        r = lax.rsqrt(ss * (1.0 / QK_HEAD) + EPS)
        k_ref[:, hd * HEAD_SLOT:(hd + 1) * HEAD_SLOT] = ((knh * kg + pe_g) * r).astype(BF16)


def _const_spec(shape):
    return pl.BlockSpec(shape, lambda i: (0,) * len(shape))


def _inproj(x2d, mod3, mod_row_fn, wts, rope_tabs, emit_cache):
    n = x2d.shape[0]
    rope = rope_tabs is not None
    tiles_per_seq = None if not rope else rope_tabs[0].shape[0] // TM
    in_specs = [pl.BlockSpec((TM, D_MODEL), lambda i: (i, 0)),
                pl.BlockSpec((1, 6, D_MODEL), lambda i: (mod_row_fn(i), 0, 0)),
                _const_spec((1, D_MODEL)),
                _const_spec((D_MODEL, IN_PACKED)),
                _const_spec((1, Q_LORA)),
                _const_spec((Q_LORA, QK_WIDTH)),
                _const_spec((1, KV_LORA)),
                _const_spec((KV_LORA, QK_WIDTH)),
                _const_spec((KV_LORA, V_WIDTH)),
                _const_spec((1, HEAD_SLOT)),
                _const_spec((1, HEAD_SLOT)),
                _const_spec((FN_GROUP_W, 2 * FN_GROUP_W))]
    args = [x2d, mod3, wts["g1"], wts["w_in"], wts["qag"], wts["w_qb"], wts["kvg"],
            wts["w_kb"], wts["w_vb"], wts["qg"], wts["kg"], wts["dft_c"]]
    if rope:
        in_specs += [pl.BlockSpec((TM, LANES), lambda i: (i % tiles_per_seq, 0))] * 3
        args += list(rope_tabs)
    out_shape = [jax.ShapeDtypeStruct((n, QK_WIDTH), BF16),
                 jax.ShapeDtypeStruct((n, QK_WIDTH), BF16),
                 jax.ShapeDtypeStruct((n, V_WIDTH), BF16),
                 jax.ShapeDtypeStruct((n, 2 * FN_WIDTH), BF16),
                 jax.ShapeDtypeStruct((n, D_MODEL), BF16),
                 jax.ShapeDtypeStruct((n, D_MODEL), BF16)]
    out_specs = [pl.BlockSpec((TM, s.shape[1]), lambda i: (i, 0)) for s in out_shape]
    if emit_cache:
        out_shape += [jax.ShapeDtypeStruct((n, KV_LORA), F32),
                      jax.ShapeDtypeStruct((n, QK_ROPE), F32)]
        out_specs += [pl.BlockSpec((TM, KV_LORA), lambda i: (i, 0)),
                      pl.BlockSpec((TM, QK_ROPE), lambda i: (i, 0))]
    return pl.pallas_call(
        functools.partial(_inproj_kernel, rope=rope, emit_cache=emit_cache),
        grid=(n // TM,),
        in_specs=in_specs,
        out_specs=out_specs,
        out_shape=out_shape,
        compiler_params=_cparams(("parallel",)),
        name="inproj_lat" if rope else "inproj_ctx",
    )(*args)


def _cache_kv_kernel(ckv_ref, kpe_ref, wkb_ref, wvb_ref, kg_ref, k_ref, v_ref):
    _emit_kv(ckv_ref[...].astype(BF16), kpe_ref[...], wkb_ref, wvb_ref, kg_ref, None, k_ref, v_ref)


def _cache_kv(ckv2d, kpe_slot2d, wts):
    n = ckv2d.shape[0]
    return pl.pallas_call(
        _cache_kv_kernel,
        grid=(n // TM,),
        in_specs=[pl.BlockSpec((TM, KV_LORA), lambda i: (i, 0)),
                  pl.BlockSpec((TM, LANES), lambda i: (i, 0)),
                  _const_spec((KV_LORA, QK_WIDTH)),
                  _const_spec((KV_LORA, V_WIDTH)),
                  _const_spec((1, HEAD_SLOT))],
        out_specs=[pl.BlockSpec((TM, QK_WIDTH), lambda i: (i, 0)),
                   pl.BlockSpec((TM, V_WIDTH), lambda i: (i, 0))],
        out_shape=[jax.ShapeDtypeStruct((n, QK_WIDTH), BF16),
                   jax.ShapeDtypeStruct((n, V_WIDTH), BF16)],
        compiler_params=_cparams(("parallel",)),
        name="cache_kv",
    )(ckv2d, kpe_slot2d, wts["w_kb"], wts["w_vb"], wts["kg"])


def _attn_kernel(*refs, n_kv, n_b):
    q_ref = refs[0]
    k_refs = refs[1:1 + n_kv]
    v_refs = refs[1 + n_kv:1 + 2 * n_kv]
    o_ref = refs[1 + 2 * n_kv]
    for b in range(n_b):
        for hd in range(N_HEADS):
            qh = q_ref[b, :, hd * HEAD_SLOT:(hd + 1) * HEAD_SLOT]
            s = [_dot_nt(qh, k[b, :, hd * HEAD_SLOT:(hd + 1) * HEAD_SLOT]) for k in k_refs]
            m = s[0].max(axis=-1, keepdims=True)
            for sj in s[1:]:
                m = jnp.maximum(m, sj.max(axis=-1, keepdims=True))
            p = [jnp.exp(sj - m) for sj in s]
            l = p[0].sum(axis=-1, keepdims=True)
            for pj in p[1:]:
                l = l + pj.sum(axis=-1, keepdims=True)
            o = _dot(p[0].astype(BF16), v_refs[0][b, :, hd * V_HEAD:(hd + 1) * V_HEAD])
            for pj, v in zip(p[1:], v_refs[1:]):
                o = o + _dot(pj.astype(BF16), v[b, :, hd * V_HEAD:(hd + 1) * V_HEAD])
            o_ref[b, :, hd * V_HEAD:(hd + 1) * V_HEAD] = (o / l).astype(BF16)


def _attention(q3, ks, vs, tq, n_b, name):
    b, sq, _ = q3.shape
    n_kv = len(ks)
    in_specs = [pl.BlockSpec((n_b, tq, QK_WIDTH), lambda bi, qi: (bi, qi, 0))]
    in_specs += [pl.BlockSpec((n_b, k.shape[1], QK_WIDTH), lambda bi, qi: (bi, 0, 0)) for k in ks]
    in_specs += [pl.BlockSpec((n_b, v.shape[1], V_WIDTH), lambda bi, qi: (bi, 0, 0)) for v in vs]
    return pl.pallas_call(
        functools.partial(_attn_kernel, n_kv=n_kv, n_b=n_b),
        grid=(b // n_b, sq // tq),
        in_specs=in_specs,
        out_specs=pl.BlockSpec((n_b, tq, V_WIDTH), lambda bi, qi: (bi, qi, 0)),
        out_shape=jax.ShapeDtypeStruct((b, sq, V_WIDTH), BF16),
        compiler_params=_cparams(("parallel", "parallel")),
        name=name,
    )(q3, *ks, *vs)


def _fourier_kernel(cs_ref, ns_ref, f_ref, o_ref, *, n_b):
    for b in range(n_b):
        xc = f_ref[b, :, :FN_WIDTH]
        xs = f_ref[b, :, FN_WIDTH:]
        o_ref[b] = (_dot(cs_ref[...], xc) + _dot(ns_ref[...], xs)).astype(BF16)


def _fourier(fcs3, cs, ns, tr, n_b, name):
    b, s, _ = fcs3.shape
    return pl.pallas_call(
        functools.partial(_fourier_kernel, n_b=n_b),
        grid=(b // n_b, s // tr),
        in_specs=[pl.BlockSpec((tr, s), lambda bi, ri: (ri, 0)),
                  pl.BlockSpec((tr, s), lambda bi, ri: (ri, 0)),
                  pl.BlockSpec((n_b, s, 2 * FN_WIDTH), lambda bi, ri: (bi, 0, 0))],
        out_specs=pl.BlockSpec((n_b, tr, FN_WIDTH), lambda bi, ri: (bi, ri, 0)),
        out_shape=jax.ShapeDtypeStruct((b, s, FN_WIDTH), BF16),
        compiler_params=_cparams(("parallel", "parallel")),
        name=name,
    )(cs, ns, fcs3)


def _modulate2(x1, g2_ref, shift, scale):
    return (x1 * _rms(x1, D_MODEL) * g2_ref[...]) * (1.0 + scale) + shift


def _split_bf16(x):
    hi = x.astype(BF16)
    return hi, (x - hi.astype(F32)).astype(BF16)


def _router_logits(w, b, h_hi, h_lo):
    w_hi, w_lo = _split_bf16(w)
    return (_dot_nt(w_hi, h_hi) + _dot_nt(w_lo, h_hi) + _dot_nt(w_hi, h_lo)) + b


def _rows(x, n):
    return [x[j:j + 1, :] for j in range(n)]


def _first_argmax(rows, top):
    idx = jnp.full(top.shape, len(rows) - 1, jnp.int32)
    for j in range(len(rows) - 2, -1, -1):
        idx = jnp.where(rows[j] == top, j, idx)
    return idx


def _store_token_major(ref, x, tm):
    for s in range(TOKEN_ROWS):
        ref[pl.ds(s, tm, stride=TOKEN_ROWS), :] = x[:, s * LANES:(s + 1) * LANES]


def _load_token_major(ref, tm):
    return jnp.concatenate([ref[pl.ds(s, tm, stride=TOKEN_ROWS), :] for s in range(TOKEN_ROWS)],
                           axis=1)


def _merge_kernel(x_ref, attn_ref, fm_ref, sga_ref, sgf_ref, mod_ref, wao_ref, wfn_ref, wout_ref,
                  g2_ref, wrg_ref, brg_ref, x1_ref, seg_ref, rank_ref, cnt_ref, carry_ref, *,
                  n_mod, tiles_per_seq):
    i = pl.program_id(0)

    @pl.when(i == 0)
    def _():
        carry_ref[...] = jnp.zeros_like(carry_ref)

    a = _dot(attn_ref[...], wao_ref[...])
    f = _dot(fm_ref[...], wfn_ref[...])
    u = sga_ref[...].astype(F32) * a + sgf_ref[...].astype(F32) * f
    y = _dot(u.astype(BF16), wout_ref[...])
    x1 = x_ref[...] + mod_ref[0, 2:3, :] * y
    tm = x1.shape[0]
    _store_token_major(x1_ref, x1, tm)

    h2_hi, h2_lo = _split_bf16(_modulate2(x1, g2_ref, mod_ref[0, 3:4, :], mod_ref[0, 4:5, :]))
    g = _rows(_router_logits(wrg_ref[...], brg_ref[...], h2_hi, h2_lo), N_GROUPS)
    gidx = _first_argmax(g, functools.reduce(jnp.maximum, g))

    seg = gidx * n_mod
    if n_mod > 1:
        seg = seg + i // tiles_per_seq
    onehot = jnp.where(lax.broadcasted_iota(jnp.int32, (SEG_ROWS, tm), 0) == seg, 1.0, 0.0)
    before = (lax.broadcasted_iota(jnp.int32, (tm, tm), 0)
              < lax.broadcasted_iota(jnp.int32, (tm, tm), 1))
    prefix = _dot(onehot.astype(BF16), jnp.where(before, 1.0, 0.0).astype(BF16))
    carry = carry_ref[...]
    rank = jnp.sum(onehot * (prefix + carry[:, 0:1]), axis=0, keepdims=True)
    seg_ref[...] = seg
    rank_ref[...] = rank.astype(jnp.int32)
    carry = carry + jnp.sum(onehot, axis=1, keepdims=True)
    carry_ref[...] = carry
    cnt_ref[...] = carry.astype(jnp.int32)


def _merge(x2d, attn2d, fm2d, sga, sgf, mod3, mod_row_fn, wts, n_mod):
    n = x2d.shape[0]
    tok = lambda w: pl.BlockSpec((TM, w), lambda i: (i, 0))
    return pl.pallas_call(
        functools.partial(_merge_kernel, n_mod=n_mod, tiles_per_seq=n // n_mod // TM),
        grid=(n // TM,),
        in_specs=[tok(D_MODEL), tok(V_WIDTH), tok(FN_WIDTH), tok(D_MODEL), tok(D_MODEL),
                  pl.BlockSpec((1, 6, D_MODEL), lambda i: (mod_row_fn(i), 0, 0)),
                  _const_spec((V_WIDTH, D_MODEL)),
                  _const_spec((FN_WIDTH, D_MODEL)),
                  _const_spec((D_MODEL, D_MODEL)),
                  _const_spec((1, D_MODEL)),
                  _const_spec((ROUTER_ROWS, D_MODEL)),
                  _const_spec((ROUTER_ROWS, 1))],
        out_specs=[pl.BlockSpec((TM * TOKEN_ROWS, LANES), lambda i: (i, 0)),
                   pl.BlockSpec((1, TM), lambda i: (0, i)),
                   pl.BlockSpec((1, TM), lambda i: (0, i)), _const_spec((SEG_ROWS, LANES))],
        out_shape=[jax.ShapeDtypeStruct((n * TOKEN_ROWS, LANES), F32),
                   jax.ShapeDtypeStruct((1, n), jnp.int32),
                   jax.ShapeDtypeStruct((1, n), jnp.int32),
                   jax.ShapeDtypeStruct((SEG_ROWS, LANES), jnp.int32)],
        scratch_shapes=[pltpu.VMEM((SEG_ROWS, LANES), F32)],
        compiler_params=_cparams(("arbitrary",)),
        name="merge",
    )(x2d, attn2d, fm2d, sga, sgf, mod3, wts["w_ao"], wts["w_fn"], wts["w_out"], wts["g2"],
      wts["w_rg"], wts["b_rg"])


def _token_rows(ref, t):
    start = t * TOKEN_ROWS
    if not isinstance(t, int):
        start = pl.multiple_of(start, TOKEN_ROWS)
    return ref.at[pl.ds(start, TOKEN_ROWS)]


def _row_copy(src, dst, src_tok, dst_tok, sem):
    return pltpu.make_async_copy(_token_rows(src, src_tok), _token_rows(dst, dst_tok), sem)


def _combine_kernel(pos_ref, y_hbm, o_ref, buf, sem):
    i = pl.program_id(0)
    tm = o_ref.shape[0]

    def gather(tile_idx, slot):
        def start(r, c):
            _row_copy(y_hbm, buf.at[slot], pos_ref[tile_idx * tm + r], r, sem.at[slot]).start()
            return c

        lax.fori_loop(0, tm, start, 0, unroll=8)

    @pl.when(i == 0)
    def _():
        gather(0, 0)

    @pl.when(i + 1 < pl.num_programs(0))
    def _():
        gather(i + 1, (i + 1) % 2)

    slot = i % 2

    def wait(r, c):
        _row_copy(y_hbm, buf.at[slot], 0, 0, sem.at[slot]).wait()
        return c

    lax.fori_loop(0, tm, wait, 0, unroll=8)
    o_ref[...] = _load_token_major(buf.at[slot], tm)


def _combine(y_sorted, pos, n):
    return pl.pallas_call(
        _combine_kernel,
        grid_spec=pltpu.PrefetchScalarGridSpec(
            num_scalar_prefetch=1,
            grid=(n // TM,),
            in_specs=[pl.BlockSpec(memory_space=pl.ANY)],
            out_specs=pl.BlockSpec((TM, D_MODEL), lambda i, pos: (i, 0)),
            scratch_shapes=[pltpu.VMEM((2, TM * TOKEN_ROWS, LANES), F32),
                            pltpu.SemaphoreType.DMA((2,))]),
        out_shape=jax.ShapeDtypeStruct((n, D_MODEL), F32),
        compiler_params=_cparams(("arbitrary",)),
        name="moe_combine",
    )(pos, y_sorted)


def _moe_kernel(grp_ref, split_ref, on_ref, src_ref, x_hbm, mod_ref, g2_ref, wrg_ref, brg_ref,
                wre_ref, bre_ref, wg_ref, wu_ref, wd_ref, o_ref, xbuf, sem, *, tile, n_mod, row_base):
    i = pl.program_id(0)

    def mod_row(j):
        first = mod_ref[row_base, j:j + 1, :]
        if n_mod == 1:
            return first
        rows = lax.broadcasted_iota(jnp.int32, (tile, 1), 0)
        return jnp.where(rows < split_ref[i], first, mod_ref[row_base + 1, j:j + 1, :])

    slot = i % 2

    def start_row(tile_idx, dst_slot, r):
        _row_copy(x_hbm, xbuf.at[dst_slot], src_ref[tile_idx * tile + r], r, sem.at[dst_slot]).start()

    def wait_tile():
        def wait(r, c):
            _row_copy(x_hbm, xbuf.at[slot], 0, 0, sem.at[slot]).wait()
            return c

        lax.fori_loop(0, tile, wait, 0, unroll=8)

    @pl.when(i == 0)
    def _():
        def start(r, c):
            start_row(0, 0, r)
            return c

        lax.fori_loop(0, tile, start, 0, unroll=8)

    @pl.when(on_ref[i] == 0)
    def _():
        o_ref[...] = jnp.zeros_like(o_ref)

    @pl.when(jnp.logical_and(on_ref[i] == 0, on_ref[jnp.maximum(i - 1, 0)] == 1))
    def _():
        wait_tile()

    @pl.when(on_ref[i] == 1)
    def _():
        wait_tile()
        x1 = _load_token_major(xbuf.at[slot], tile)
        for r in range(tile):
            start_row(i + 1, 1 - slot, r)
        h2_hi, h2_lo = _split_bf16(_modulate2(x1, g2_ref, mod_row(3), mod_row(4)))

        g = _rows(_router_logits(wrg_ref[...], brg_ref[...], h2_hi, h2_lo), N_GROUPS)
        gmax = functools.reduce(jnp.maximum, g)
        p_top = 1.0 / functools.reduce(lambda p, q: p + q, [jnp.exp(gj - gmax) for gj in g])
        e = _rows(_router_logits(wre_ref[0], bre_ref[0], h2_hi, h2_lo), EXPERTS_PER_GROUP)
        m1 = functools.reduce(jnp.maximum, e)
        i1 = _first_argmax(e, m1)
        rest = [jnp.where(i1 == j, -jnp.inf, e[j]) for j in range(EXPERTS_PER_GROUP)]
        m2 = functools.reduce(jnp.maximum, rest)
        i2 = _first_argmax(rest, m2)
        t = jnp.exp(m2 - m1)
        w1 = p_top / (1.0 + t)
        w2 = p_top * t / (1.0 + t)
        row = lax.broadcasted_iota(jnp.int32, (LANES, tile), 0)
        comb = (jnp.where(row == i1, w1, 0.0) + jnp.where(row == i2, w2, 0.0)).T

        acc = None
        for j in range(EXPERTS_PER_GROUP):
            a = _dot(h2_hi, wg_ref[j])
            u = _dot(h2_hi, wu_ref[j])
            act = (a * _sigmoid(a)) * u * comb[:, j:j + 1]
            y = _dot(act.astype(BF16), wd_ref[j])
            acc = y if acc is None else acc + y
        _store_token_major(o_ref, x1 + mod_row(5) * acc, tile)


def _moe(x1tm, src, maps, mod3, wts, w_gate, w_up, w_down, tile, n_mod, row_base):
    n_tiles = src.shape[0] // tile
    const = lambda shape: pl.BlockSpec(shape, lambda i, grp, split, on, src: (0,) * len(shape))
    by_group = lambda shape: pl.BlockSpec(
        shape, lambda i, grp, split, on, src: (grp[i],) + (0,) * (len(shape) - 1))
    return pl.pallas_call(
        functools.partial(_moe_kernel, tile=tile, n_mod=n_mod, row_base=row_base),
        grid_spec=pltpu.PrefetchScalarGridSpec(
            num_scalar_prefetch=4,
            grid=(n_tiles,),
            in_specs=[pl.BlockSpec(memory_space=pl.ANY),
                      const((MOD_ROWS, 6, D_MODEL)),
                      const((1, D_MODEL)),
                      const((ROUTER_ROWS, D_MODEL)), const((ROUTER_ROWS, 1)),
                      by_group((1, ROUTER_ROWS, D_MODEL)), by_group((1, ROUTER_ROWS, 1)),
                      by_group((EXPERTS_PER_GROUP, D_MODEL, D_EXPERT)),
                      by_group((EXPERTS_PER_GROUP, D_MODEL, D_EXPERT)),
                      by_group((EXPERTS_PER_GROUP, D_EXPERT, D_MODEL))],
            out_specs=pl.BlockSpec((tile * TOKEN_ROWS, LANES), lambda i, grp, split, on, src: (i, 0)),
            scratch_shapes=[pltpu.VMEM((2, tile * TOKEN_ROWS, LANES), F32),
                            pltpu.SemaphoreType.DMA((2,))]),
        out_shape=jax.ShapeDtypeStruct((n_tiles * tile * TOKEN_ROWS, LANES), F32),
        compiler_params=_cparams(("arbitrary",)),
        name="moe",
    )(*maps, src, x1tm, mod3, wts["g2"], wts["w_rg"], wts["b_rg"], wts["w_re"], wts["b_re"],
      w_gate, w_up, w_down)


def _invert_kernel(pos_ref, src_ref, *, n):
    def zero(p, c):
        src_ref[p] = 0
        return c

    lax.fori_loop(0, src_ref.shape[0], zero, 0, unroll=8)

    def put(t, c):
        src_ref[pos_ref[t]] = t
        return c

    lax.fori_loop(0, n, put, 0, unroll=8)


def _invert(pos, n_sorted):
    return pl.pallas_call(
        functools.partial(_invert_kernel, n=pos.shape[0]),
        grid_spec=pltpu.PrefetchScalarGridSpec(
            num_scalar_prefetch=1, grid=(1,), in_specs=[],
            out_specs=pl.BlockSpec(memory_space=pltpu.SMEM)),
        out_shape=jax.ShapeDtypeStruct((n_sorted,), jnp.int32),
        name="moe_invert",
    )(pos)


def _sort_plan(seg, rank, cnt, n_mod, tile, n_tiles):
    cnt = cnt.reshape(N_GROUPS, n_mod)
    n_tile_grp = (jnp.sum(cnt, axis=1) + tile - 1) // tile
    tile_end = jnp.cumsum(n_tile_grp)
    tile_start = tile_end - n_tile_grp
    total = tile_end[-1]
    seg_base = (tile_start * tile)[:, None] + jnp.cumsum(cnt, axis=1) - cnt
    pos = (seg_base.reshape(-1)[seg] + rank).astype(jnp.int32)
    src = _invert(pos, n_tiles * tile)
    i = jnp.arange(n_tiles, dtype=jnp.int32)
    grp = jnp.sum((jnp.minimum(i, total - 1)[:, None] >= tile_end[None, :]).astype(jnp.int32), axis=1)
    split = jnp.clip(cnt[grp, 0] - (i - tile_start[grp]) * tile, 0, tile)
    maps = (grp.astype(jnp.int32), split.astype(jnp.int32), (i < total).astype(jnp.int32))
    return pos, src, maps


def _sparse_moe(x1tm, seg, rank, cnt8, mod3, wts, w_gate, w_up, w_down, n_mod, row_base, tile):
    assert n_mod <= 2
    n = x1tm.shape[0] // TOKEN_ROWS
    n_tiles = n // tile + N_GROUPS + 1
    pos, src, maps = _sort_plan(seg.reshape(n), rank.reshape(n), cnt8[:N_GROUPS * n_mod, 0], n_mod,
                                tile, n_tiles)
    y_sorted = _moe(x1tm, src, maps, mod3, wts, w_gate, w_up, w_down, tile, n_mod, row_base)
    return _combine(y_sorted, pos, n)


def _pack_weights(l, w_in, norm1_g, q_a_norm_g, w_q_b, kv_a_norm_g, w_kv_b, q_norm_g, k_norm_g,
                  w_attn_o, w_fnet, w_out, norm2_g, w_router_group, b_router_group,
                  w_router_expert, b_router_expert):
    wi = w_in[l]
    kpe_cols = jnp.pad(wi[:, C_KPE:C_KPE + QK_ROPE], ((0, 0), (QK_NOPE, LANES - QK_HEAD)))
    w_in_p = jnp.concatenate([wi[:, :C_KPE], kpe_cols, wi[:, C_KPE + QK_ROPE:]], axis=1)
    w_qb = jnp.pad(w_q_b[l].reshape(Q_LORA, N_HEADS, QK_HEAD),
                   ((0, 0), (0, 0), (0, HEAD_SLOT - QK_HEAD))).reshape(Q_LORA, QK_WIDTH)
    wkv = w_kv_b[l].reshape(KV_LORA, N_HEADS, QK_NOPE + V_HEAD)
    w_kb = jnp.pad(wkv[:, :, :QK_NOPE],
                   ((0, 0), (0, 0), (0, HEAD_SLOT - QK_NOPE))).reshape(KV_LORA, QK_WIDTH)
    w_vb = wkv[:, :, QK_NOPE:].reshape(KV_LORA, V_WIDTH)
    pad_g = lambda g: jnp.pad(g, (0, HEAD_SLOT - QK_HEAD)).reshape(1, HEAD_SLOT)
    w_rg = jnp.pad(w_router_group[l].T, ((0, ROUTER_ROWS - N_GROUPS), (0, 0)))
    b_rg = jnp.pad(b_router_group[l], (0, ROUTER_ROWS - N_GROUPS)).reshape(ROUTER_ROWS, 1)
    w_re = jnp.pad(w_router_expert[l].T.reshape(N_GROUPS, EXPERTS_PER_GROUP, D_MODEL),
                   ((0, 0), (0, ROUTER_ROWS - EXPERTS_PER_GROUP), (0, 0)))
    b_re = jnp.pad(b_router_expert[l].reshape(N_GROUPS, EXPERTS_PER_GROUP),
                   ((0, 0), (0, ROUTER_ROWS - EXPERTS_PER_GROUP))).reshape(N_GROUPS, ROUTER_ROWS, 1)
    dft_c, dft_ns = _dft_tables(FN_GROUP_W)
    return {
        "g1": norm1_g[l].reshape(1, D_MODEL),
        "w_in": w_in_p.astype(BF16),
        "qag": q_a_norm_g[l].reshape(1, Q_LORA),
        "w_qb": w_qb.astype(BF16),
        "kvg": kv_a_norm_g[l].reshape(1, KV_LORA),
        "w_kb": w_kb.astype(BF16),
        "w_vb": w_vb.astype(BF16),
        "qg": pad_g(q_norm_g[l]),
        "kg": pad_g(k_norm_g[l]),
        "dft_c": jnp.concatenate([jnp.asarray(dft_c), -jnp.asarray(dft_ns)], axis=1).astype(BF16),
        "w_ao": w_attn_o[l].astype(BF16),
        "w_fn": w_fnet[l].astype(BF16),
        "w_out": w_out[l].astype(BF16),
        "g2": norm2_g[l].reshape(1, D_MODEL),
        "w_rg": w_rg,
        "b_rg": b_rg,
        "w_re": w_re,
        "b_re": b_re,
    }


def _layer(xp, xs, cache_ckv_l, cache_kpe_l, mod3, wts, w_gate, w_up, w_down):
    bp, sp, _ = xp.shape
    bs, ss, _ = xs.shape
    past = cache_ckv_l.shape[1]
    ctx_row = lambda i: 0
    lat_row = lambda i: 1 + i // (ss // TM)

    xp2 = xp.reshape(bp * sp, D_MODEL)
    q, k, v, fcs, sga, sgf, ckv, kpe = _inproj(xp2, mod3, ctx_row, wts, None, True)
    attn = _attention(q.reshape(bp, sp, QK_WIDTH), [k.reshape(bp, sp, QK_WIDTH)],
                      [v.reshape(bp, sp, V_WIDTH)], sp, 1, "attn_ctx")
    cs, ns = (jnp.asarray(t).astype(BF16) for t in _dft_tables(sp))
    fm = _fourier(fcs.reshape(bp, sp, 2 * FN_WIDTH), cs, ns, sp, SEQ_PER_STEP_CTX, "fourier_ctx")
    x1tm, seg, rank, cnt = _merge(xp2, attn.reshape(bp * sp, V_WIDTH), fm.reshape(bp * sp, FN_WIDTH),
                                  sga, sgf, mod3, ctx_row, wts, 1)
    yp = _sparse_moe(x1tm, seg, rank, cnt, mod3, wts, w_gate, w_up, w_down, 1, 0,
                     MOE_TILE_CTX).reshape(bp, sp, D_MODEL)

    xs2 = xs.reshape(bs * ss, D_MODEL)
    rope_tabs = tuple(jnp.asarray(t) for t in _rope_tables(ss))
    q, k, v, fcs, sga, sgf = _inproj(xs2, mod3, lat_row, wts, rope_tabs, False)
    kpe_slot = jnp.pad(cache_kpe_l, ((0, 0), (0, 0), (QK_NOPE, LANES - QK_HEAD)))
    kc, vc = _cache_kv(cache_ckv_l.reshape(bs * past, KV_LORA), kpe_slot.reshape(bs * past, LANES), wts)
    attn = _attention(q.reshape(bs, ss, QK_WIDTH),
                      [kc.reshape(bs, past, QK_WIDTH), k.reshape(bs, ss, QK_WIDTH)],
                      [vc.reshape(bs, past, V_WIDTH), v.reshape(bs, ss, V_WIDTH)], TQ_LAT, 1, "attn_lat")
    cs, ns = (jnp.asarray(t).astype(BF16) for t in _dft_tables(ss))
    fm = _fourier(fcs.reshape(bs, ss, 2 * FN_WIDTH), cs, ns, TQ_LAT, 1, "fourier_lat")
    assert N_GROUPS * bs <= SEG_ROWS
    x1tm, seg, rank, cnt = _merge(xs2, attn.reshape(bs * ss, V_WIDTH), fm.reshape(bs * ss, FN_WIDTH),
                                  sga, sgf, mod3, lat_row, wts, bs)
    ys = _sparse_moe(x1tm, seg, rank, cnt, mod3, wts, w_gate, w_up, w_down, bs, 1,
                     MOE_TILE_LAT).reshape(bs, ss, D_MODEL)

    return yp, ys, ckv.reshape(bp, sp, KV_LORA), kpe.reshape(bp, sp, QK_ROPE)


def kernel(x_prompt, x_sample, cache_ckv, cache_kpe, c, c_ctx, w_mod, b_mod, norm1_g, w_in, q_a_norm_g, w_q_b, kv_a_norm_g, w_kv_b, q_norm_g, k_norm_g, w_attn_o, w_fnet, w_out, norm2_g, w_router_group, b_router_group, w_router_expert, b_router_expert, w_exp_gate, w_exp_up, w_exp_down):
    depth = w_mod.shape[0]
    n_lat = c.shape[0]
    assert 1 + n_lat <= MOD_ROWS
    cond8 = jnp.concatenate([c_ctx[None, :], c, jnp.zeros((MOD_ROWS - 1 - n_lat, D_MODEL), F32)], axis=0)
    xp, xs = x_prompt, x_sample
    ckv_layers, kpe_layers = [], []
    for l in range(depth):
        mod3 = _adaln(cond8, w_mod[l], b_mod[l].reshape(1, -1)).reshape(MOD_ROWS, 6, D_MODEL)
        wts = _pack_weights(l, w_in, norm1_g, q_a_norm_g, w_q_b, kv_a_norm_g, w_kv_b, q_norm_g,
                            k_norm_g, w_attn_o, w_fnet, w_out, norm2_g, w_router_group,
                            b_router_group, w_router_expert, b_router_expert)
        xp, xs, ckv, kpe = _layer(xp, xs, cache_ckv[:, l], cache_kpe[:, l], mod3, wts,
                                  w_exp_gate[l].astype(BF16), w_exp_up[l].astype(BF16),
                                  w_exp_down[l].astype(BF16))
        ckv_layers.append(ckv)
        kpe_layers.append(kpe)
    return xp, xs, jnp.stack(ckv_layers, axis=1), jnp.stack(kpe_layers, axis=1)
```

```python
import functools
import math

import numpy as np
import jax
import jax.numpy as jnp
from jax import lax
from jax.experimental import pallas as pl
from jax.experimental.pallas import tpu as pltpu

D_MODEL = 1024
GRID_W = 64
N_HEADS = 8
Q_LORA = 512
KV_LORA = 256
QK_NOPE = 64
QK_ROPE = 32
V_HEAD = 64
QK_HEAD = QK_NOPE + QK_ROPE
ATTN_SCALE = QK_HEAD ** -0.5
ROPE_BASE = 10000.0
FN_GROUPS = 4
FN_GROUP_W = 128
FN_WIDTH = FN_GROUPS * FN_GROUP_W
N_GROUPS = 4
EXPERTS_PER_GROUP = 4
N_EXPERTS = N_GROUPS * EXPERTS_PER_GROUP
D_EXPERT = 512
EPS = 1e-6

LANES = 128
HEAD_SLOT = LANES
QK_WIDTH = N_HEADS * HEAD_SLOT
V_WIDTH = N_HEADS * V_HEAD
C_QA = 0
C_KVA = C_QA + Q_LORA
C_KPE = C_KVA + KV_LORA
C_FN = C_KPE + LANES
C_GA = C_FN + FN_WIDTH
C_GF = C_GA + D_MODEL
IN_PACKED = C_GF + D_MODEL
SUBLANES = 8
ROUTER_ROWS = SUBLANES
MOD_ROWS = SUBLANES
VMEM_LIMIT = 56 * 1024 * 1024

TM = 512
TQ_LAT = 256
SEQ_PER_STEP_CTX = 4
SEG_ROWS = SUBLANES
TOKEN_ROWS = D_MODEL // LANES
MOE_TILE_CTX = 256
MOE_TILE_LAT = 256

BF16 = jnp.bfloat16
F32 = jnp.float32


def _cparams(sem):
    return pltpu.CompilerParams(dimension_semantics=sem, vmem_limit_bytes=VMEM_LIMIT)


def _dot(a, b):
    return jnp.dot(a, b, preferred_element_type=F32)


def _dot_nt(a, b):
    return lax.dot_general(a, b, (((1,), (1,)), ((), ())), preferred_element_type=F32)


def _sigmoid(x):
    return 1.0 / (1.0 + jnp.exp(-x))


@functools.lru_cache(maxsize=None)
def _rope_tables(n_pos):
    half = QK_ROPE // 2
    quarter = half // 2
    freqs = ROPE_BASE ** (-np.arange(quarter, dtype=np.float64) / quarter)
    pos = np.arange(n_pos)
    row = (pos // GRID_W).astype(np.float64)
    col = (pos % GRID_W).astype(np.float64)
    cos_t = np.ones((n_pos, LANES), np.float64)
    sin_a = np.zeros((n_pos, LANES), np.float64)
    sin_b = np.zeros((n_pos, LANES), np.float64)
    for base, p in ((QK_NOPE, row), (QK_NOPE + half, col)):
        ang = p[:, None] * freqs[None, :]
        cos_t[:, base:base + quarter] = np.cos(ang)
        cos_t[:, base + quarter:base + half] = np.cos(ang)
        sin_a[:, base:base + quarter] = -np.sin(ang)
        sin_b[:, base + quarter:base + half] = np.sin(ang)
    return (cos_t.astype(np.float32), sin_a.astype(np.float32), sin_b.astype(np.float32))


@functools.lru_cache(maxsize=None)
def _dft_tables(n):
    k = np.arange(n)
    ang = 2.0 * np.pi * ((k[:, None] * k[None, :]) % n) / n
    s = 1.0 / math.sqrt(n)
    return (np.cos(ang) * s).astype(np.float32), (-np.sin(ang) * s).astype(np.float32)


def _adaln_kernel(cond_ref, w_ref, b_ref, o_ref):
    c = cond_ref[...]
    s = c * _sigmoid(c)
    o_ref[...] = jnp.dot(s, w_ref[...], preferred_element_type=F32,
                         precision=lax.Precision.HIGHEST) + b_ref[...]


def _adaln(cond8, w_mod, b_mod):
    n = w_mod.shape[1]
    tn = D_MODEL
    return pl.pallas_call(
        _adaln_kernel,
        grid=(n // tn,),
        in_specs=[pl.BlockSpec((MOD_ROWS, D_MODEL), lambda j: (0, 0)),
                  pl.BlockSpec((D_MODEL, tn), lambda j: (0, j)),
                  pl.BlockSpec((1, tn), lambda j: (0, j))],
        out_specs=pl.BlockSpec((MOD_ROWS, tn), lambda j: (0, j)),
        out_shape=jax.ShapeDtypeStruct((MOD_ROWS, n), F32),
        compiler_params=_cparams(("arbitrary",)),
        name="adaln",
    )(cond8, w_mod, b_mod)


def _rms(x, width):
    return lax.rsqrt(jnp.sum(x * x, axis=-1, keepdims=True) * (1.0 / width) + EPS)


def _rope(x, cos_t, sin_a, sin_b):
    return x * cos_t + pltpu.roll(x, LANES - 8, 1) * sin_a + pltpu.roll(x, 8, 1) * sin_b


def _inproj_kernel(*refs, rope, emit_cache):
    it = iter(refs)
    x_ref, mod_ref, g1_ref, win_ref, qag_ref, wqb_ref, kvg_ref, wkb_ref, wvb_ref = (
        next(it) for _ in range(9))
    qg_ref, kg_ref, dft_ref = next(it), next(it), next(it)
    if rope:
        cos_ref, sa_ref, sb_ref = next(it), next(it), next(it)
    q_ref, k_ref, v_ref, fcs_ref, sga_ref, sgf_ref = (next(it) for _ in range(6))
    if emit_cache:
        ckv_ref, kpe_ref = next(it), next(it)

    x = x_ref[...]
    shift = mod_ref[0, 0:1, :]
    scale = mod_ref[0, 1:2, :]
    h = (x * _rms(x, D_MODEL) * g1_ref[...]) * (1.0 + scale) + shift
    hb = h.astype(BF16)

    if rope:
        cos_t, sin_a, sin_b = cos_ref[...], sa_ref[...], sb_ref[...]

    qa = _dot(hb, win_ref[:, C_QA:C_QA + Q_LORA])
    qn = (qa * _rms(qa, Q_LORA) * qag_ref[...]).astype(BF16)
    q = _dot(qn, wqb_ref[...])
    qg = qg_ref[...] * ATTN_SCALE
    for hd in range(N_HEADS):
        qh = q[:, hd * HEAD_SLOT:(hd + 1) * HEAD_SLOT]
        qh = qh * _rms(qh, QK_HEAD) * qg
        if rope:
            qh = _rope(qh, cos_t, sin_a, sin_b)
        q_ref[:, hd * HEAD_SLOT:(hd + 1) * HEAD_SLOT] = qh.astype(BF16)

    kva = _dot(hb, win_ref[:, C_KVA:C_KVA + KV_LORA])
    ckv = kva * _rms(kva, KV_LORA) * kvg_ref[...]
    kpe = _dot(hb, win_ref[:, C_KPE:C_KPE + LANES])
    if emit_cache:
        ckv_ref[...] = ckv
        kpe_ref[...] = kpe[:, QK_NOPE:QK_NOPE + QK_ROPE]
    _emit_kv(ckv.astype(BF16), kpe, wkb_ref, wvb_ref, kg_ref,
             (cos_t, sin_a, sin_b) if rope else None, k_ref, v_ref)

    fn = _dot(hb, win_ref[:, C_FN:C_FN + FN_WIDTH]).astype(BF16)
    for g in range(FN_GROUPS):
        cs = _dot(fn[:, g * FN_GROUP_W:(g + 1) * FN_GROUP_W], dft_ref[...])
        fcs_ref[:, g * FN_GROUP_W:(g + 1) * FN_GROUP_W] = cs[:, :FN_GROUP_W].astype(BF16)
        fcs_ref[:, FN_WIDTH + g * FN_GROUP_W:FN_WIDTH + (g + 1) * FN_GROUP_W] = (
            cs[:, FN_GROUP_W:].astype(BF16))

    sga_ref[...] = _sigmoid(_dot(hb, win_ref[:, C_GA:C_GA + D_MODEL])).astype(BF16)
    sgf_ref[...] = _sigmoid(_dot(hb, win_ref[:, C_GF:C_GF + D_MODEL])).astype(BF16)


def _emit_kv(ckvb, kpe, wkb_ref, wvb_ref, kg_ref, rope_tabs, k_ref, v_ref):
    kg = kg_ref[...]
    v_ref[...] = _dot_nt(wvb_ref[...], ckvb).astype(BF16)
    kn = _dot(ckvb, wkb_ref[...])
    pe_ss = jnp.sum(kpe * kpe, axis=-1, keepdims=True)
    pe_g = kpe * kg
    if rope_tabs is not None:
        pe_g = _rope(pe_g, *rope_tabs)
    for hd in range(N_HEADS):
        knh = kn[:, hd * HEAD_SLOT:(hd + 1) * HEAD_SLOT]
        ss = jnp.sum(knh * knh, axis=-1, keepdims=True) + pe_ss
        r = lax.rsqrt(ss * (1.0 / QK_HEAD) + EPS)
        k_ref[:, hd * HEAD_SLOT:(hd + 1) * HEAD_SLOT] = ((knh * kg + pe_g) * r).astype(BF16)


def _const_spec(shape):
    return pl.BlockSpec(shape, lambda i: (0,) * len(shape))


def _inproj(x2d, mod3, mod_row_fn, wts, rope_tabs, emit_cache):
    n = x2d.shape[0]
    rope = rope_tabs is not None
    tiles_per_seq = None if not rope else rope_tabs[0].shape[0] // TM
    in_specs = [pl.BlockSpec((TM, D_MODEL), lambda i: (i, 0)),
                pl.BlockSpec((1, 6, D_MODEL), lambda i: (mod_row_fn(i), 0, 0)),
                _const_spec((1, D_MODEL)),
                _const_spec((D_MODEL, IN_PACKED)),
                _const_spec((1, Q_LORA)),
                _const_spec((Q_LORA, QK_WIDTH)),
                _const_spec((1, KV_LORA)),
                _const_spec((KV_LORA, QK_WIDTH)),
                _const_spec((V_WIDTH, KV_LORA)),
                _const_spec((1, HEAD_SLOT)),
                _const_spec((1, HEAD_SLOT)),
                _const_spec((FN_GROUP_W, 2 * FN_GROUP_W))]
    args = [x2d, mod3, wts["g1"], wts["w_in"], wts["qag"], wts["w_qb"], wts["kvg"],
            wts["w_kb"], wts["w_vb"], wts["qg"], wts["kg"], wts["dft_c"]]
    if rope:
        in_specs += [pl.BlockSpec((TM, LANES), lambda i: (i % tiles_per_seq, 0))] * 3
        args += list(rope_tabs)
    out_shape = [jax.ShapeDtypeStruct((n, QK_WIDTH), BF16),
                 jax.ShapeDtypeStruct((n, QK_WIDTH), BF16),
                 jax.ShapeDtypeStruct((V_WIDTH, n), BF16),
                 jax.ShapeDtypeStruct((n, 2 * FN_WIDTH), BF16),
                 jax.ShapeDtypeStruct((n, D_MODEL), BF16),
                 jax.ShapeDtypeStruct((n, D_MODEL), BF16)]
    out_specs = [pl.BlockSpec((TM, s.shape[1]), lambda i: (i, 0)) for s in out_shape]
    out_specs[2] = pl.BlockSpec((V_WIDTH, TM), lambda i: (0, i))
    if emit_cache:
        out_shape += [jax.ShapeDtypeStruct((n, KV_LORA), F32),
                      jax.ShapeDtypeStruct((n, QK_ROPE), F32)]
        out_specs += [pl.BlockSpec((TM, KV_LORA), lambda i: (i, 0)),
                      pl.BlockSpec((TM, QK_ROPE), lambda i: (i, 0))]
    return pl.pallas_call(
        functools.partial(_inproj_kernel, rope=rope, emit_cache=emit_cache),
        grid=(n // TM,),
        in_specs=in_specs,
        out_specs=out_specs,
        out_shape=out_shape,
        compiler_params=_cparams(("parallel",)),
        name="inproj_lat" if rope else "inproj_ctx",
    )(*args)


def _cache_kv_kernel(ckv_ref, kpe_ref, wkb_ref, wvb_ref, kg_ref, k_ref, v_ref):
    _emit_kv(ckv_ref[...].astype(BF16), kpe_ref[...], wkb_ref, wvb_ref, kg_ref, None, k_ref, v_ref)


def _cache_kv(ckv2d, kpe_slot2d, wts):
    n = ckv2d.shape[0]
    return pl.pallas_call(
        _cache_kv_kernel,
        grid=(n // TM,),
        in_specs=[pl.BlockSpec((TM, KV_LORA), lambda i: (i, 0)),
                  pl.BlockSpec((TM, LANES), lambda i: (i, 0)),
                  _const_spec((KV_LORA, QK_WIDTH)),
                  _const_spec((V_WIDTH, KV_LORA)),
                  _const_spec((1, HEAD_SLOT))],
        out_specs=[pl.BlockSpec((TM, QK_WIDTH), lambda i: (i, 0)),
                   pl.BlockSpec((V_WIDTH, TM), lambda i: (0, i))],
        out_shape=[jax.ShapeDtypeStruct((n, QK_WIDTH), BF16),
                   jax.ShapeDtypeStruct((V_WIDTH, n), BF16)],
        compiler_params=_cparams(("parallel",)),
        name="cache_kv",
    )(ckv2d, kpe_slot2d, wts["w_kb"], wts["w_vb"], wts["kg"])


def _attn_kernel(*refs, n_kv):
    q_ref = refs[0]
    k_refs = refs[1:1 + n_kv]
    vt_refs = refs[1 + n_kv:1 + 2 * n_kv]
    o_ref = refs[1 + 2 * n_kv]
    head = lambda hd: slice(hd * HEAD_SLOT, (hd + 1) * HEAD_SLOT)
    st = [jnp.stack([_dot_nt(k[0, :, head(hd)], q_ref[0, :, head(hd)]) for hd in range(N_HEADS)])
          for k in k_refs]
    m = functools.reduce(jnp.maximum, [sj.max(axis=1, keepdims=True) for sj in st])
    p = [jnp.exp(sj - m) for sj in st]
    l = functools.reduce(lambda a, b: a + b, [pj.sum(axis=1, keepdims=True) for pj in p])
    outs = []
    for hd in range(N_HEADS):
        o = functools.reduce(lambda a, b: a + b,
                             [_dot(vt[hd * V_HEAD:(hd + 1) * V_HEAD, :], pj[hd].astype(BF16))
                              for vt, pj in zip(vt_refs, p)])
        outs.append(o / l[hd])
    o_ref[0] = jnp.concatenate(outs, axis=0).T.astype(BF16)


def _attention(q3, ks, vts, tq, name):
    b, sq, _ = q3.shape
    n_kv = len(ks)
    in_specs = [pl.BlockSpec((1, tq, QK_WIDTH), lambda bi, qi: (bi, qi, 0))]
    in_specs += [pl.BlockSpec((1, k.shape[1], QK_WIDTH), lambda bi, qi: (bi, 0, 0)) for k in ks]
    in_specs += [pl.BlockSpec((V_WIDTH, k.shape[1]), lambda bi, qi: (0, bi)) for k in ks]
    return pl.pallas_call(
        functools.partial(_attn_kernel, n_kv=n_kv),
        grid=(b, sq // tq),
        in_specs=in_specs,
        out_specs=pl.BlockSpec((1, tq, V_WIDTH), lambda bi, qi: (bi, qi, 0)),
        out_shape=jax.ShapeDtypeStruct((b, sq, V_WIDTH), BF16),
        compiler_params=_cparams(("parallel", "parallel")),
        name=name,
    )(q3, *ks, *vts)


def _fourier_kernel(cs_ref, ns_ref, f_ref, o_ref, *, n_b):
    for b in range(n_b):
        xc = f_ref[b, :, :FN_WIDTH]
        xs = f_ref[b, :, FN_WIDTH:]
        o_ref[b] = (_dot(cs_ref[...], xc) + _dot(ns_ref[...], xs)).astype(BF16)


def _fourier(fcs3, cs, ns, tr, n_b, name):
    b, s, _ = fcs3.shape
    return pl.pallas_call(
        functools.partial(_fourier_kernel, n_b=n_b),
        grid=(b // n_b, s // tr),
        in_specs=[pl.BlockSpec((tr, s), lambda bi, ri: (ri, 0)),
                  pl.BlockSpec((tr, s), lambda bi, ri: (ri, 0)),
                  pl.BlockSpec((n_b, s, 2 * FN_WIDTH), lambda bi, ri: (bi, 0, 0))],
        out_specs=pl.BlockSpec((n_b, tr, FN_WIDTH), lambda bi, ri: (bi, ri, 0)),
        out_shape=jax.ShapeDtypeStruct((b, s, FN_WIDTH), BF16),
        compiler_params=_cparams(("parallel", "parallel")),
        name=name,
    )(cs, ns, fcs3)


def _modulate2(x1, g2_ref, shift, scale):
    return (x1 * _rms(x1, D_MODEL) * g2_ref[...]) * (1.0 + scale) + shift


def _split_bf16(x):
    hi = x.astype(BF16)
    return hi, (x - hi.astype(F32)).astype(BF16)


def _router_logits(w, b, h_hi, h_lo):
    w_hi, w_lo = _split_bf16(w)
    return (_dot_nt(w_hi, h_hi) + _dot_nt(w_lo, h_hi) + _dot_nt(w_hi, h_lo)) + b


def _rows(x, n):
    return [x[j:j + 1, :] for j in range(n)]


def _first_argmax(rows, top):
    idx = jnp.full(top.shape, len(rows) - 1, jnp.int32)
    for j in range(len(rows) - 2, -1, -1):
        idx = jnp.where(rows[j] == top, j, idx)
    return idx


def _store_token_major(ref, x, tm):
    for s in range(TOKEN_ROWS):
        ref[pl.ds(s, tm, stride=TOKEN_ROWS), :] = x[:, s * LANES:(s + 1) * LANES]


def _load_token_major(ref, tm):
    return jnp.concatenate([ref[pl.ds(s, tm, stride=TOKEN_ROWS), :] for s in range(TOKEN_ROWS)],
                           axis=1)


def _merge_kernel(x_ref, attn_ref, fm_ref, sga_ref, sgf_ref, mod_ref, wao_ref, wfn_ref, wout_ref,
                  g2_ref, wrg_ref, brg_ref, x1_ref, seg_ref, rank_ref, cnt_ref, carry_ref, *,
                  n_mod, tiles_per_seq):
    i = pl.program_id(0)

    @pl.when(i == 0)
    def _():
        carry_ref[...] = jnp.zeros_like(carry_ref)

    a = _dot(attn_ref[...], wao_ref[...])
    f = _dot(fm_ref[...], wfn_ref[...])
    u = sga_ref[...].astype(F32) * a + sgf_ref[...].astype(F32) * f
    y = _dot(u.astype(BF16), wout_ref[...])
    x1 = x_ref[...] + mod_ref[0, 2:3, :] * y
    tm = x1.shape[0]
    _store_token_major(x1_ref, x1, tm)

    h2_hi, h2_lo = _split_bf16(_modulate2(x1, g2_ref, mod_ref[0, 3:4, :], mod_ref[0, 4:5, :]))
    g = _rows(_router_logits(wrg_ref[...], brg_ref[...], h2_hi, h2_lo), N_GROUPS)
    gidx = _first_argmax(g, functools.reduce(jnp.maximum, g))

    seg = gidx * n_mod
    if n_mod > 1:
        seg = seg + i // tiles_per_seq
    onehot = jnp.where(lax.broadcasted_iota(jnp.int32, (SEG_ROWS, tm), 0) == seg, 1.0, 0.0)
    before = (lax.broadcasted_iota(jnp.int32, (tm, tm), 0)
              < lax.broadcasted_iota(jnp.int32, (tm, tm), 1))
    prefix = _dot(onehot.astype(BF16), jnp.where(before, 1.0, 0.0).astype(BF16))
    carry = carry_ref[...]
    rank = jnp.sum(onehot * (prefix + carry[:, 0:1]), axis=0, keepdims=True)
    seg_ref[...] = seg
    rank_ref[...] = rank.astype(jnp.int32)
    carry = carry + jnp.sum(onehot, axis=1, keepdims=True)
    carry_ref[...] = carry
    cnt_ref[...] = carry.astype(jnp.int32)


def _merge(x2d, attn2d, fm2d, sga, sgf, mod3, mod_row_fn, wts, n_mod):
    n = x2d.shape[0]
    tok = lambda w: pl.BlockSpec((TM, w), lambda i: (i, 0))
    return pl.pallas_call(
        functools.partial(_merge_kernel, n_mod=n_mod, tiles_per_seq=n // n_mod // TM),
        grid=(n // TM,),
        in_specs=[tok(D_MODEL), tok(V_WIDTH), tok(FN_WIDTH), tok(D_MODEL), tok(D_MODEL),
                  pl.BlockSpec((1, 6, D_MODEL), lambda i: (mod_row_fn(i), 0, 0)),
                  _const_spec((V_WIDTH, D_MODEL)),
                  _const_spec((FN_WIDTH, D_MODEL)),
                  _const_spec((D_MODEL, D_MODEL)),
                  _const_spec((1, D_MODEL)),
                  _const_spec((ROUTER_ROWS, D_MODEL)),
                  _const_spec((ROUTER_ROWS, 1))],
        out_specs=[pl.BlockSpec((TM * TOKEN_ROWS, LANES), lambda i: (i, 0)),
                   pl.BlockSpec((1, TM), lambda i: (0, i)),
                   pl.BlockSpec((1, TM), lambda i: (0, i)), _const_spec((SEG_ROWS, LANES))],
        out_shape=[jax.ShapeDtypeStruct((n * TOKEN_ROWS, LANES), F32),
                   jax.ShapeDtypeStruct((1, n), jnp.int32),
                   jax.ShapeDtypeStruct((1, n), jnp.int32),
                   jax.ShapeDtypeStruct((SEG_ROWS, LANES), jnp.int32)],
        scratch_shapes=[pltpu.VMEM((SEG_ROWS, LANES), F32)],
        compiler_params=_cparams(("arbitrary",)),
        name="merge",
    )(x2d, attn2d, fm2d, sga, sgf, mod3, wts["w_ao"], wts["w_fn"], wts["w_out"], wts["g2"],
      wts["w_rg"], wts["b_rg"])


def _token_rows(ref, t):
    start = t * TOKEN_ROWS
    if not isinstance(t, int):
        start = pl.multiple_of(start, TOKEN_ROWS)
    return ref.at[pl.ds(start, TOKEN_ROWS)]


def _row_copy(src, dst, src_tok, dst_tok, sem):
    return pltpu.make_async_copy(_token_rows(src, src_tok), _token_rows(dst, dst_tok), sem)


def _combine_kernel(pos_ref, y_hbm, o_ref, buf, sem):
    i = pl.program_id(0)
    tm = o_ref.shape[0]

    def gather(tile_idx, slot):
        def start(r, c):
            _row_copy(y_hbm, buf.at[slot], pos_ref[tile_idx * tm + r], r, sem.at[slot]).start()
            return c

        lax.fori_loop(0, tm, start, 0, unroll=8)

    @pl.when(i == 0)
    def _():
        gather(0, 0)

    @pl.when(i + 1 < pl.num_programs(0))
    def _():
        gather(i + 1, (i + 1) % 2)

    slot = i % 2

    def wait(r, c):
        _row_copy(y_hbm, buf.at[slot], 0, 0, sem.at[slot]).wait()
        return c

    lax.fori_loop(0, tm, wait, 0, unroll=8)
    o_ref[...] = _load_token_major(buf.at[slot], tm)


def _combine(y_sorted, pos, n):
    return pl.pallas_call(
        _combine_kernel,
        grid_spec=pltpu.PrefetchScalarGridSpec(
            num_scalar_prefetch=1,
            grid=(n // TM,),
            in_specs=[pl.BlockSpec(memory_space=pl.ANY)],
            out_specs=pl.BlockSpec((TM, D_MODEL), lambda i, pos: (i, 0)),
            scratch_shapes=[pltpu.VMEM((2, TM * TOKEN_ROWS, LANES), F32),
                            pltpu.SemaphoreType.DMA((2,))]),
        out_shape=jax.ShapeDtypeStruct((n, D_MODEL), F32),
        compiler_params=_cparams(("arbitrary",)),
        name="moe_combine",
    )(pos, y_sorted)


def _moe_kernel(grp_ref, split_ref, on_ref, src_ref, x_hbm, mod_ref, g2_ref, wrg_ref, brg_ref,
                wre_ref, bre_ref, wg_ref, wu_ref, wd_ref, o_ref, xbuf, sem, *, tile, n_mod, row_base):
    i = pl.program_id(0)

    def mod_row(j):
        first = mod_ref[row_base, j:j + 1, :]
        if n_mod == 1:
            return first
        rows = lax.broadcasted_iota(jnp.int32, (tile, 1), 0)
        return jnp.where(rows < split_ref[i], first, mod_ref[row_base + 1, j:j + 1, :])

    def gather(tile_idx, slot):
        def start(r, c):
            _row_copy(x_hbm, xbuf.at[slot], src_ref[tile_idx * tile + r], r, sem.at[slot]).start()
            return c

        lax.fori_loop(0, tile, start, 0, unroll=8)

    @pl.when(jnp.logical_and(i == 0, on_ref[0] == 1))
    def _():
        gather(0, 0)

    nxt = jnp.minimum(i + 1, pl.num_programs(0) - 1)

    @pl.when(jnp.logical_and(i + 1 < pl.num_programs(0), on_ref[nxt] == 1))
    def _():
        gather(i + 1, (i + 1) % 2)

    @pl.when(on_ref[i] == 0)
    def _():
        o_ref[...] = jnp.zeros_like(o_ref)

    @pl.when(on_ref[i] == 1)
    def _():
        slot = i % 2

        def wait(r, c):
            _row_copy(x_hbm, xbuf.at[slot], 0, 0, sem.at[slot]).wait()
            return c

        lax.fori_loop(0, tile, wait, 0, unroll=8)
        x1 = _load_token_major(xbuf.at[slot], tile)
        h2_hi, h2_lo = _split_bf16(_modulate2(x1, g2_ref, mod_row(3), mod_row(4)))

        g = _rows(_router_logits(wrg_ref[...], brg_ref[...], h2_hi, h2_lo), N_GROUPS)
        gmax = functools.reduce(jnp.maximum, g)
        p_top = 1.0 / functools.reduce(lambda p, q: p + q, [jnp.exp(gj - gmax) for gj in g])
        e = _rows(_router_logits(wre_ref[0], bre_ref[0], h2_hi, h2_lo), EXPERTS_PER_GROUP)
        m1 = functools.reduce(jnp.maximum, e)
        i1 = _first_argmax(e, m1)
        rest = [jnp.where(i1 == j, -jnp.inf, e[j]) for j in range(EXPERTS_PER_GROUP)]
        m2 = functools.reduce(jnp.maximum, rest)
        i2 = _first_argmax(rest, m2)
        t = jnp.exp(m2 - m1)
        w1 = p_top / (1.0 + t)
        w2 = p_top * t / (1.0 + t)
        row = lax.broadcasted_iota(jnp.int32, (LANES, tile), 0)
        comb = (jnp.where(row == i1, w1, 0.0) + jnp.where(row == i2, w2, 0.0)).T

        acc = None
        for j in range(EXPERTS_PER_GROUP):
            a = _dot(h2_hi, wg_ref[j])
            u = _dot(h2_hi, wu_ref[j])
            act = (a * _sigmoid(a)) * u * comb[:, j:j + 1]
            y = _dot(act.astype(BF16), wd_ref[j])
            acc = y if acc is None else acc + y
        _store_token_major(o_ref, x1 + mod_row(5) * acc, tile)


def _moe(x1tm, src, maps, mod3, wts, w_gate, w_up, w_down, tile, n_mod, row_base):
    n_tiles = src.shape[0] // tile
    const = lambda shape: pl.BlockSpec(shape, lambda i, grp, split, on, src: (0,) * len(shape))
    by_group = lambda shape: pl.BlockSpec(
        shape, lambda i, grp, split, on, src: (grp[i],) + (0,) * (len(shape) - 1))
    return pl.pallas_call(
        functools.partial(_moe_kernel, tile=tile, n_mod=n_mod, row_base=row_base),
        grid_spec=pltpu.PrefetchScalarGridSpec(
            num_scalar_prefetch=4,
            grid=(n_tiles,),
            in_specs=[pl.BlockSpec(memory_space=pl.ANY),
                      const((MOD_ROWS, 6, D_MODEL)),
                      const((1, D_MODEL)),
                      const((ROUTER_ROWS, D_MODEL)), const((ROUTER_ROWS, 1)),
                      by_group((1, ROUTER_ROWS, D_MODEL)), by_group((1, ROUTER_ROWS, 1)),
                      by_group((EXPERTS_PER_GROUP, D_MODEL, D_EXPERT)),
                      by_group((EXPERTS_PER_GROUP, D_MODEL, D_EXPERT)),
                      by_group((EXPERTS_PER_GROUP, D_EXPERT, D_MODEL))],
            out_specs=pl.BlockSpec((tile * TOKEN_ROWS, LANES), lambda i, grp, split, on, src: (i, 0)),
            scratch_shapes=[pltpu.VMEM((2, tile * TOKEN_ROWS, LANES), F32),
                            pltpu.SemaphoreType.DMA((2,))]),
        out_shape=jax.ShapeDtypeStruct((n_tiles * tile * TOKEN_ROWS, LANES), F32),
        compiler_params=_cparams(("arbitrary",)),
        name="moe",
    )(*maps, src, x1tm, mod3, wts["g2"], wts["w_rg"], wts["b_rg"], wts["w_re"], wts["b_re"],
      w_gate, w_up, w_down)


def _invert_kernel(pos_ref, src_ref, *, n):
    def zero(p, c):
        src_ref[p] = 0
        return c

    lax.fori_loop(0, src_ref.shape[0], zero, 0, unroll=8)

    def put(t, c):
        src_ref[pos_ref[t]] = t
        return c

    lax.fori_loop(0, n, put, 0, unroll=8)


def _invert(pos, n_sorted):
    return pl.pallas_call(
        functools.partial(_invert_kernel, n=pos.shape[0]),
        grid_spec=pltpu.PrefetchScalarGridSpec(
            num_scalar_prefetch=1, grid=(1,), in_specs=[],
            out_specs=pl.BlockSpec(memory_space=pltpu.SMEM)),
        out_shape=jax.ShapeDtypeStruct((n_sorted,), jnp.int32),
        name="moe_invert",
    )(pos)


def _sort_plan(seg, rank, cnt, n_mod, tile, n_tiles):
    cnt = cnt.reshape(N_GROUPS, n_mod)
    n_tile_grp = (jnp.sum(cnt, axis=1) + tile - 1) // tile
    tile_end = jnp.cumsum(n_tile_grp)
    tile_start = tile_end - n_tile_grp
    total = tile_end[-1]
    seg_base = (tile_start * tile)[:, None] + jnp.cumsum(cnt, axis=1) - cnt
    pos = (seg_base.reshape(-1)[seg] + rank).astype(jnp.int32)
    src = _invert(pos, n_tiles * tile)
    i = jnp.arange(n_tiles, dtype=jnp.int32)
    grp = jnp.sum((jnp.minimum(i, total - 1)[:, None] >= tile_end[None, :]).astype(jnp.int32), axis=1)
    split = jnp.clip(cnt[grp, 0] - (i - tile_start[grp]) * tile, 0, tile)
    maps = (grp.astype(jnp.int32), split.astype(jnp.int32), (i < total).astype(jnp.int32))
    return pos, src, maps


def _sparse_moe(x1tm, seg, rank, cnt8, mod3, wts, w_gate, w_up, w_down, n_mod, row_base, tile):
    assert n_mod <= 2
    n = x1tm.shape[0] // TOKEN_ROWS
    n_tiles = n // tile + N_GROUPS
    pos, src, maps = _sort_plan(seg.reshape(n), rank.reshape(n), cnt8[:N_GROUPS * n_mod, 0], n_mod,
                                tile, n_tiles)
    y_sorted = _moe(x1tm, src, maps, mod3, wts, w_gate, w_up, w_down, tile, n_mod, row_base)
    return _combine(y_sorted, pos, n)


def _pack_weights(l, w_in, norm1_g, q_a_norm_g, w_q_b, kv_a_norm_g, w_kv_b, q_norm_g, k_norm_g,
                  w_attn_o, w_fnet, w_out, norm2_g, w_router_group, b_router_group,
                  w_router_expert, b_router_expert):
    wi = w_in[l]
    kpe_cols = jnp.pad(wi[:, C_KPE:C_KPE + QK_ROPE], ((0, 0), (QK_NOPE, LANES - QK_HEAD)))
    w_in_p = jnp.concatenate([wi[:, :C_KPE], kpe_cols, wi[:, C_KPE + QK_ROPE:]], axis=1)
    w_qb = jnp.pad(w_q_b[l].reshape(Q_LORA, N_HEADS, QK_HEAD),
                   ((0, 0), (0, 0), (0, HEAD_SLOT - QK_HEAD))).reshape(Q_LORA, QK_WIDTH)
    wkv = w_kv_b[l].reshape(KV_LORA, N_HEADS, QK_NOPE + V_HEAD)
    w_kb = jnp.pad(wkv[:, :, :QK_NOPE],
                   ((0, 0), (0, 0), (0, HEAD_SLOT - QK_NOPE))).reshape(KV_LORA, QK_WIDTH)
    w_vb = wkv[:, :, QK_NOPE:].reshape(KV_LORA, V_WIDTH)
    pad_g = lambda g: jnp.pad(g, (0, HEAD_SLOT - QK_HEAD)).reshape(1, HEAD_SLOT)
    w_rg = jnp.pad(w_router_group[l].T, ((0, ROUTER_ROWS - N_GROUPS), (0, 0)))
    b_rg = jnp.pad(b_router_group[l], (0, ROUTER_ROWS - N_GROUPS)).reshape(ROUTER_ROWS, 1)
    w_re = jnp.pad(w_router_expert[l].T.reshape(N_GROUPS, EXPERTS_PER_GROUP, D_MODEL),
                   ((0, 0), (0, ROUTER_ROWS - EXPERTS_PER_GROUP), (0, 0)))
    b_re = jnp.pad(b_router_expert[l].reshape(N_GROUPS, EXPERTS_PER_GROUP),
                   ((0, 0), (0, ROUTER_ROWS - EXPERTS_PER_GROUP))).reshape(N_GROUPS, ROUTER_ROWS, 1)
    dft_c, dft_ns = _dft_tables(FN_GROUP_W)
    return {
        "g1": norm1_g[l].reshape(1, D_MODEL),
        "w_in": w_in_p.astype(BF16),
        "qag": q_a_norm_g[l].reshape(1, Q_LORA),
        "w_qb": w_qb.astype(BF16),
        "kvg": kv_a_norm_g[l].reshape(1, KV_LORA),
        "w_kb": w_kb.astype(BF16),
        "w_vb": w_vb.T.astype(BF16),
        "qg": pad_g(q_norm_g[l]),
        "kg": pad_g(k_norm_g[l]),
        "dft_c": jnp.concatenate([jnp.asarray(dft_c), -jnp.asarray(dft_ns)], axis=1).astype(BF16),
        "w_ao": w_attn_o[l].astype(BF16),
        "w_fn": w_fnet[l].astype(BF16),
        "w_out": w_out[l].astype(BF16),
        "g2": norm2_g[l].reshape(1, D_MODEL),
        "w_rg": w_rg,
        "b_rg": b_rg,
        "w_re": w_re,
        "b_re": b_re,
    }


def _layer(xp, xs, cache_ckv_l, cache_kpe_l, mod3, wts, w_gate, w_up, w_down):
    bp, sp, _ = xp.shape
    bs, ss, _ = xs.shape
    past = cache_ckv_l.shape[1]
    ctx_row = lambda i: 0
    lat_row = lambda i: 1 + i // (ss // TM)

    xp2 = xp.reshape(bp * sp, D_MODEL)
    q, k, v, fcs, sga, sgf, ckv, kpe = _inproj(xp2, mod3, ctx_row, wts, None, True)
    attn = _attention(q.reshape(bp, sp, QK_WIDTH), [k.reshape(bp, sp, QK_WIDTH)],
                      [v], sp, "attn_ctx")
    cs, ns = (jnp.asarray(t).astype(BF16) for t in _dft_tables(sp))
    fm = _fourier(fcs.reshape(bp, sp, 2 * FN_WIDTH), cs, ns, sp, SEQ_PER_STEP_CTX, "fourier_ctx")
    x1tm, seg, rank, cnt = _merge(xp2, attn.reshape(bp * sp, V_WIDTH), fm.reshape(bp * sp, FN_WIDTH),
                                  sga, sgf, mod3, ctx_row, wts, 1)
    yp = _sparse_moe(x1tm, seg, rank, cnt, mod3, wts, w_gate, w_up, w_down, 1, 0,
                     MOE_TILE_CTX).reshape(bp, sp, D_MODEL)

    xs2 = xs.reshape(bs * ss, D_MODEL)
    rope_tabs = tuple(jnp.asarray(t) for t in _rope_tables(ss))
    q, k, v, fcs, sga, sgf = _inproj(xs2, mod3, lat_row, wts, rope_tabs, False)
    kpe_slot = jnp.pad(cache_kpe_l, ((0, 0), (0, 0), (QK_NOPE, LANES - QK_HEAD)))
    kc, vc = _cache_kv(cache_ckv_l.reshape(bs * past, KV_LORA), kpe_slot.reshape(bs * past, LANES), wts)
    attn = _attention(q.reshape(bs, ss, QK_WIDTH),
                      [kc.reshape(bs, past, QK_WIDTH), k.reshape(bs, ss, QK_WIDTH)],
                      [vc, v], TQ_LAT, "attn_lat")
    cs, ns = (jnp.asarray(t).astype(BF16) for t in _dft_tables(ss))
    fm = _fourier(fcs.reshape(bs, ss, 2 * FN_WIDTH), cs, ns, TQ_LAT, 1, "fourier_lat")
    assert N_GROUPS * bs <= SEG_ROWS
    x1tm, seg, rank, cnt = _merge(xs2, attn.reshape(bs * ss, V_WIDTH), fm.reshape(bs * ss, FN_WIDTH),
                                  sga, sgf, mod3, lat_row, wts, bs)
    ys = _sparse_moe(x1tm, seg, rank, cnt, mod3, wts, w_gate, w_up, w_down, bs, 1,
                     MOE_TILE_LAT).reshape(bs, ss, D_MODEL)

    return yp, ys, ckv.reshape(bp, sp, KV_LORA), kpe.reshape(bp, sp, QK_ROPE)


def kernel(x_prompt, x_sample, cache_ckv, cache_kpe, c, c_ctx, w_mod, b_mod, norm1_g, w_in, q_a_norm_g, w_q_b, kv_a_norm_g, w_kv_b, q_norm_g, k_norm_g, w_attn_o, w_fnet, w_out, norm2_g, w_router_group, b_router_group, w_router_expert, b_router_expert, w_exp_gate, w_exp_up, w_exp_down):
    depth = w_mod.shape[0]
    n_lat = c.shape[0]
    assert 1 + n_lat <= MOD_ROWS
    cond8 = jnp.concatenate([c_ctx[None, :], c, jnp.zeros((MOD_ROWS - 1 - n_lat, D_MODEL), F32)], axis=0)
    xp, xs = x_prompt, x_sample
    ckv_layers, kpe_layers = [], []
    for l in range(depth):
        mod3 = _adaln(cond8, w_mod[l], b_mod[l].reshape(1, -1)).reshape(MOD_ROWS, 6, D_MODEL)
        wts = _pack_weights(l, w_in, norm1_g, q_a_norm_g, w_q_b, kv_a_norm_g, w_kv_b, q_norm_g,
                            k_norm_g, w_attn_o, w_fnet, w_out, norm2_g, w_router_group,
                            b_router_group, w_router_expert, b_router_expert)
        xp, xs, ckv, kpe = _layer(xp, xs, cache_ckv[:, l], cache_kpe[:, l], mod3, wts,
                                  w_exp_gate[l].astype(BF16), w_exp_up[l].astype(BF16),
                                  w_exp_down[l].astype(BF16))
        ckv_layers.append(ckv)
        kpe_layers.append(kpe)
    return xp, xs, jnp.stack(ckv_layers, axis=1), jnp.stack(kpe_layers, axis=1)
```

```python
import functools
import math

import numpy as np
import jax
import jax.numpy as jnp
from jax import lax
from jax.experimental import pallas as pl
from jax.experimental.pallas import tpu as pltpu

D_MODEL = 1024
GRID_W = 64
N_HEADS = 8
Q_LORA = 512
KV_LORA = 256
QK_NOPE = 64
QK_ROPE = 32
V_HEAD = 64
QK_HEAD = QK_NOPE + QK_ROPE
ATTN_SCALE = QK_HEAD ** -0.5
ROPE_BASE = 10000.0
FN_GROUPS = 4
FN_GROUP_W = 128
FN_WIDTH = FN_GROUPS * FN_GROUP_W
N_GROUPS = 4
EXPERTS_PER_GROUP = 4
N_EXPERTS = N_GROUPS * EXPERTS_PER_GROUP
D_EXPERT = 512
EPS = 1e-6

LANES = 128
HEAD_SLOT = LANES
QK_WIDTH = N_HEADS * HEAD_SLOT
V_WIDTH = N_HEADS * V_HEAD
C_QA = 0
C_KVA = C_QA + Q_LORA
C_KPE = C_KVA + KV_LORA
C_FN = C_KPE + LANES
C_GA = C_FN + FN_WIDTH
C_GF = C_GA + D_MODEL
IN_PACKED = C_GF + D_MODEL
SUBLANES = 8
ROUTER_ROWS = SUBLANES
MOD_ROWS = SUBLANES
VMEM_LIMIT = 56 * 1024 * 1024

TM = 512
TQ_LAT = 256
SEQ_PER_STEP_CTX = 4
SEG_ROWS = SUBLANES
TOKEN_ROWS = D_MODEL // LANES
MOE_TILE_CTX = 256
MOE_TILE_LAT = 256

BF16 = jnp.bfloat16
F32 = jnp.float32


def _cparams(sem):
    return pltpu.CompilerParams(dimension_semantics=sem, vmem_limit_bytes=VMEM_LIMIT)


def _dot(a, b):
    return jnp.dot(a, b, preferred_element_type=F32)


def _dot_nt(a, b):
    return lax.dot_general(a, b, (((1,), (1,)), ((), ())), preferred_element_type=F32)


def _sigmoid(x):
    return 1.0 / (1.0 + jnp.exp(-x))


def _split_bf16(x):
    hi = x.astype(BF16)
    return hi, (x - hi.astype(F32)).astype(BF16)


@functools.lru_cache(maxsize=None)
def _rope_tables(n_pos):
    half = QK_ROPE // 2
    quarter = half // 2
    freqs = ROPE_BASE ** (-np.arange(quarter, dtype=np.float64) / quarter)
    pos = np.arange(n_pos)
    row = (pos // GRID_W).astype(np.float64)
    col = (pos % GRID_W).astype(np.float64)
    cos_t = np.ones((n_pos, LANES), np.float64)
    sin_a = np.zeros((n_pos, LANES), np.float64)
    sin_b = np.zeros((n_pos, LANES), np.float64)
    for base, p in ((QK_NOPE, row), (QK_NOPE + half, col)):
        ang = p[:, None] * freqs[None, :]
        cos_t[:, base:base + quarter] = np.cos(ang)
        cos_t[:, base + quarter:base + half] = np.cos(ang)
        sin_a[:, base:base + quarter] = -np.sin(ang)
        sin_b[:, base + quarter:base + half] = np.sin(ang)
    return (cos_t.astype(np.float32), sin_a.astype(np.float32), sin_b.astype(np.float32))


@functools.lru_cache(maxsize=None)
def _dft_tables(n):
    k = np.arange(n)
    ang = 2.0 * np.pi * ((k[:, None] * k[None, :]) % n) / n
    s = 1.0 / math.sqrt(n)
    return (np.cos(ang) * s).astype(np.float32), (-np.sin(ang) * s).astype(np.float32)


def _adaln_kernel(cond_ref, w_ref, b_ref, o_ref):
    c = cond_ref[...]
    s_hi, s_lo = _split_bf16(c * _sigmoid(c))
    w_hi, w_lo = _split_bf16(w_ref[...])
    y = _dot(jnp.concatenate([s_hi, s_lo], axis=0), w_hi)
    o_ref[...] = (y[:MOD_ROWS] + y[MOD_ROWS:]) + _dot(s_hi, w_lo) + b_ref[...]


def _adaln(cond8, w_mod, b_mod):
    n = w_mod.shape[1]
    tn = D_MODEL
    return pl.pallas_call(
        _adaln_kernel,
        grid=(n // tn,),
        in_specs=[pl.BlockSpec((MOD_ROWS, D_MODEL), lambda j: (0, 0)),
                  pl.BlockSpec((D_MODEL, tn), lambda j: (0, j)),
                  pl.BlockSpec((1, tn), lambda j: (0, j))],
        out_specs=pl.BlockSpec((MOD_ROWS, tn), lambda j: (0, j)),
        out_shape=jax.ShapeDtypeStruct((MOD_ROWS, n), F32),
        compiler_params=_cparams(("arbitrary",)),
        name="adaln",
    )(cond8, w_mod, b_mod)


def _rms(x, width):
    return lax.rsqrt(jnp.sum(x * x, axis=-1, keepdims=True) * (1.0 / width) + EPS)


def _rope(x, cos_t, sin_a, sin_b):
    return x * cos_t + pltpu.roll(x, LANES - 8, 1) * sin_a + pltpu.roll(x, 8, 1) * sin_b


def _inproj_kernel(*refs, rope, emit_cache):
    it = iter(refs)
    x_ref, mod_ref, g1_ref, win_ref, qag_ref, wqb_ref, kvg_ref, wkb_ref, wvb_ref = (
        next(it) for _ in range(9))
    qg_ref, kg_ref, dft_ref = next(it), next(it), next(it)
    if rope:
        cos_ref, sa_ref, sb_ref = next(it), next(it), next(it)
    q_ref, k_ref, v_ref, fcs_ref, sga_ref, sgf_ref = (next(it) for _ in range(6))
    if emit_cache:
        ckv_ref, kpe_ref = next(it), next(it)

    x = x_ref[...]
    shift = mod_ref[0, 0:1, :]
    scale = mod_ref[0, 1:2, :]
    h = (x * _rms(x, D_MODEL) * g1_ref[...]) * (1.0 + scale) + shift
    hb = h.astype(BF16)

    if rope:
        cos_t, sin_a, sin_b = cos_ref[...], sa_ref[...], sb_ref[...]

    qa = _dot(hb, win_ref[:, C_QA:C_QA + Q_LORA])
    qn = (qa * _rms(qa, Q_LORA) * qag_ref[...]).astype(BF16)
    q = _dot(qn, wqb_ref[...])
    qg = qg_ref[...] * ATTN_SCALE
    for hd in range(N_HEADS):
        qh = q[:, hd * HEAD_SLOT:(hd + 1) * HEAD_SLOT]
        qh = qh * _rms(qh, QK_HEAD) * qg
        if rope:
            qh = _rope(qh, cos_t, sin_a, sin_b)
        q_ref[:, hd * HEAD_SLOT:(hd + 1) * HEAD_SLOT] = qh.astype(BF16)

    kva = _dot(hb, win_ref[:, C_KVA:C_KVA + KV_LORA])
    ckv = kva * _rms(kva, KV_LORA) * kvg_ref[...]
    kpe = _dot(hb, win_ref[:, C_KPE:C_KPE + LANES])
    if emit_cache:
        ckv_ref[...] = ckv
        kpe_ref[...] = kpe[:, QK_NOPE:QK_NOPE + QK_ROPE]
    _emit_kv(ckv.astype(BF16), kpe, wkb_ref, wvb_ref, kg_ref,
             (cos_t, sin_a, sin_b) if rope else None, k_ref, v_ref)

    fn = _dot(hb, win_ref[:, C_FN:C_FN + FN_WIDTH]).astype(BF16)
    for g in range(FN_GROUPS):
        cs = _dot(fn[:, g * FN_GROUP_W:(g + 1) * FN_GROUP_W], dft_ref[...])
        fcs_ref[:, g * FN_GROUP_W:(g + 1) * FN_GROUP_W] = cs[:, :FN_GROUP_W].astype(BF16)
        fcs_ref[:, FN_WIDTH + g * FN_GROUP_W:FN_WIDTH + (g + 1) * FN_GROUP_W] = (
            cs[:, FN_GROUP_W:].astype(BF16))

    sga_ref[...] = _sigmoid(_dot(hb, win_ref[:, C_GA:C_GA + D_MODEL])).astype(BF16)
    sgf_ref[...] = _sigmoid(_dot(hb, win_ref[:, C_GF:C_GF + D_MODEL])).astype(BF16)


def _emit_kv(ckvb, kpe, wkb_ref, wvb_ref, kg_ref, rope_tabs, k_ref, v_ref):
    kg = kg_ref[...]
    v_ref[...] = _dot_nt(wvb_ref[...], ckvb).astype(BF16)
    kn = _dot(ckvb, wkb_ref[...])
    pe_ss = jnp.sum(kpe * kpe, axis=-1, keepdims=True)
    pe_g = kpe * kg
    if rope_tabs is not None:
        pe_g = _rope(pe_g, *rope_tabs)
    for hd in range(N_HEADS):
        knh = kn[:, hd * HEAD_SLOT:(hd + 1) * HEAD_SLOT]
        ss = jnp.sum(knh * knh, axis=-1, keepdims=True) + pe_ss
        r = lax.rsqrt(ss * (1.0 / QK_HEAD) + EPS)
        k_ref[:, hd * HEAD_SLOT:(hd + 1) * HEAD_SLOT] = ((knh * kg + pe_g) * r).astype(BF16)


def _const_spec(shape):
    return pl.BlockSpec(shape, lambda i: (0,) * len(shape))


def _inproj(x2d, mod3, mod_row_fn, wts, rope_tabs, emit_cache):
    n = x2d.shape[0]
    rope = rope_tabs is not None
    tiles_per_seq = None if not rope else rope_tabs[0].shape[0] // TM
    in_specs = [pl.BlockSpec((TM, D_MODEL), lambda i: (i, 0)),
                pl.BlockSpec((1, 6, D_MODEL), lambda i: (mod_row_fn(i), 0, 0)),
                _const_spec((1, D_MODEL)),
                _const_spec((D_MODEL, IN_PACKED)),
                _const_spec((1, Q_LORA)),
                _const_spec((Q_LORA, QK_WIDTH)),
                _const_spec((1, KV_LORA)),
                _const_spec((KV_LORA, QK_WIDTH)),
                _const_spec((V_WIDTH, KV_LORA)),
                _const_spec((1, HEAD_SLOT)),
                _const_spec((1, HEAD_SLOT)),
                _const_spec((FN_GROUP_W, 2 * FN_GROUP_W))]
    args = [x2d, mod3, wts["g1"], wts["w_in"], wts["qag"], wts["w_qb"], wts["kvg"],
            wts["w_kb"], wts["w_vb"], wts["qg"], wts["kg"], wts["dft_c"]]
    if rope:
        in_specs += [pl.BlockSpec((TM, LANES), lambda i: (i % tiles_per_seq, 0))] * 3
        args += list(rope_tabs)
    out_shape = [jax.ShapeDtypeStruct((n, QK_WIDTH), BF16),
                 jax.ShapeDtypeStruct((n, QK_WIDTH), BF16),
                 jax.ShapeDtypeStruct((V_WIDTH, n), BF16),
                 jax.ShapeDtypeStruct((n, 2 * FN_WIDTH), BF16),
                 jax.ShapeDtypeStruct((n, D_MODEL), BF16),
                 jax.ShapeDtypeStruct((n, D_MODEL), BF16)]
    out_specs = [pl.BlockSpec((TM, s.shape[1]), lambda i: (i, 0)) for s in out_shape]
    out_specs[2] = pl.BlockSpec((V_WIDTH, TM), lambda i: (0, i))
    if emit_cache:
        out_shape += [jax.ShapeDtypeStruct((n, KV_LORA), F32),
                      jax.ShapeDtypeStruct((n, QK_ROPE), F32)]
        out_specs += [pl.BlockSpec((TM, KV_LORA), lambda i: (i, 0)),
                      pl.BlockSpec((TM, QK_ROPE), lambda i: (i, 0))]
    return pl.pallas_call(
        functools.partial(_inproj_kernel, rope=rope, emit_cache=emit_cache),
        grid=(n // TM,),
        in_specs=in_specs,
        out_specs=out_specs,
        out_shape=out_shape,
        compiler_params=_cparams(("parallel",)),
        name="inproj_lat" if rope else "inproj_ctx",
    )(*args)


def _cache_kv_kernel(ckv_ref, kpe_ref, wkb_ref, wvb_ref, kg_ref, k_ref, v_ref):
    _emit_kv(ckv_ref[...].astype(BF16), kpe_ref[...], wkb_ref, wvb_ref, kg_ref, None, k_ref, v_ref)


def _cache_kv(ckv2d, kpe_slot2d, wts):
    n = ckv2d.shape[0]
    return pl.pallas_call(
        _cache_kv_kernel,
        grid=(n // TM,),
        in_specs=[pl.BlockSpec((TM, KV_LORA), lambda i: (i, 0)),
                  pl.BlockSpec((TM, LANES), lambda i: (i, 0)),
                  _const_spec((KV_LORA, QK_WIDTH)),
                  _const_spec((V_WIDTH, KV_LORA)),
                  _const_spec((1, HEAD_SLOT))],
        out_specs=[pl.BlockSpec((TM, QK_WIDTH), lambda i: (i, 0)),
                   pl.BlockSpec((V_WIDTH, TM), lambda i: (0, i))],
        out_shape=[jax.ShapeDtypeStruct((n, QK_WIDTH), BF16),
                   jax.ShapeDtypeStruct((V_WIDTH, n), BF16)],
        compiler_params=_cparams(("parallel",)),
        name="cache_kv",
    )(ckv2d, kpe_slot2d, wts["w_kb"], wts["w_vb"], wts["kg"])


def _attn_kernel(*refs, n_kv):
    q_ref = refs[0]
    k_refs = refs[1:1 + n_kv]
    vt_refs = refs[1 + n_kv:1 + 2 * n_kv]
    o_ref = refs[1 + 2 * n_kv]
    head = lambda hd: slice(hd * HEAD_SLOT, (hd + 1) * HEAD_SLOT)
    st = [jnp.stack([_dot_nt(k[0, :, head(hd)], q_ref[0, :, head(hd)]) for hd in range(N_HEADS)])
          for k in k_refs]
    m = functools.reduce(jnp.maximum, [sj.max(axis=1, keepdims=True) for sj in st])
    p = [jnp.exp(sj - m) for sj in st]
    l = functools.reduce(lambda a, b: a + b, [pj.sum(axis=1, keepdims=True) for pj in p])
    outs = []
    for hd in range(N_HEADS):
        o = functools.reduce(lambda a, b: a + b,
                             [_dot(vt[hd * V_HEAD:(hd + 1) * V_HEAD, :], pj[hd].astype(BF16))
                              for vt, pj in zip(vt_refs, p)])
        outs.append(o / l[hd])
    o_ref[0] = jnp.concatenate(outs, axis=0).T.astype(BF16)


def _attention(q3, ks, vts, tq, name):
    b, sq, _ = q3.shape
    n_kv = len(ks)
    in_specs = [pl.BlockSpec((1, tq, QK_WIDTH), lambda bi, qi: (bi, qi, 0))]
    in_specs += [pl.BlockSpec((1, k.shape[1], QK_WIDTH), lambda bi, qi: (bi, 0, 0)) for k in ks]
    in_specs += [pl.BlockSpec((V_WIDTH, k.shape[1]), lambda bi, qi: (0, bi)) for k in ks]
    return pl.pallas_call(
        functools.partial(_attn_kernel, n_kv=n_kv),
        grid=(b, sq // tq),
        in_specs=in_specs,
        out_specs=pl.BlockSpec((1, tq, V_WIDTH), lambda bi, qi: (bi, qi, 0)),
        out_shape=jax.ShapeDtypeStruct((b, sq, V_WIDTH), BF16),
        compiler_params=_cparams(("parallel", "parallel")),
        name=name,
    )(q3, *ks, *vts)


def _fourier_kernel(cs_ref, ns_ref, f_ref, o_ref, *, n_b):
    for b in range(n_b):
        xc = f_ref[b, :, :FN_WIDTH]
        xs = f_ref[b, :, FN_WIDTH:]
        o_ref[b] = (_dot(cs_ref[...], xc) + _dot(ns_ref[...], xs)).astype(BF16)


def _fourier(fcs3, cs, ns, tr, n_b, name):
    b, s, _ = fcs3.shape
    return pl.pallas_call(
        functools.partial(_fourier_kernel, n_b=n_b),
        grid=(b // n_b, s // tr),
        in_specs=[pl.BlockSpec((tr, s), lambda bi, ri: (ri, 0)),
                  pl.BlockSpec((tr, s), lambda bi, ri: (ri, 0)),
                  pl.BlockSpec((n_b, s, 2 * FN_WIDTH), lambda bi, ri: (bi, 0, 0))],
        out_specs=pl.BlockSpec((n_b, tr, FN_WIDTH), lambda bi, ri: (bi, ri, 0)),
        out_shape=jax.ShapeDtypeStruct((b, s, FN_WIDTH), BF16),
        compiler_params=_cparams(("parallel", "parallel")),
        name=name,
    )(cs, ns, fcs3)


def _modulate2(x1, g2_ref, shift, scale):
    return (x1 * _rms(x1, D_MODEL) * g2_ref[...]) * (1.0 + scale) + shift


def _router_logits(w, b, h_hi, h_lo):
    rows = w.shape[0]
    w_hi, w_lo = _split_bf16(w)
    y = _dot_nt(jnp.concatenate([w_hi, w_lo], axis=0), h_hi)
    return (y[:rows] + y[rows:]) + _dot_nt(w_hi, h_lo) + b


def _rows(x, n):
    return [x[j:j + 1, :] for j in range(n)]


def _first_argmax(rows, top):
    idx = jnp.full(top.shape, len(rows) - 1, jnp.int32)
    for j in range(len(rows) - 2, -1, -1):
        idx = jnp.where(rows[j] == top, j, idx)
    return idx


def _store_token_major(ref, x, tm):
    for s in range(TOKEN_ROWS):
        ref[pl.ds(s, tm, stride=TOKEN_ROWS), :] = x[:, s * LANES:(s + 1) * LANES]


def _load_token_major(ref, tm):
    return jnp.concatenate([ref[pl.ds(s, tm, stride=TOKEN_ROWS), :] for s in range(TOKEN_ROWS)],
                           axis=1)


def _merge_kernel(x_ref, attn_ref, fm_ref, sga_ref, sgf_ref, mod_ref, wao_ref, wfn_ref, wout_ref,
                  g2_ref, wrg_ref, brg_ref, x1_ref, seg_ref, rank_ref, cnt_ref, carry_ref, *,
                  n_mod, tiles_per_seq):
    i = pl.program_id(0)

    @pl.when(i == 0)
    def _():
        carry_ref[...] = jnp.zeros_like(carry_ref)

    a = _dot(attn_ref[...], wao_ref[...])
    f = _dot(fm_ref[...], wfn_ref[...])
    u = sga_ref[...].astype(F32) * a + sgf_ref[...].astype(F32) * f
    y = _dot(u.astype(BF16), wout_ref[...])
    x1 = x_ref[...] + mod_ref[0, 2:3, :] * y
    tm = x1.shape[0]
    _store_token_major(x1_ref, x1, tm)

    h2_hi, h2_lo = _split_bf16(_modulate2(x1, g2_ref, mod_ref[0, 3:4, :], mod_ref[0, 4:5, :]))
    g = _rows(_router_logits(wrg_ref[...], brg_ref[...], h2_hi, h2_lo), N_GROUPS)
    gidx = _first_argmax(g, functools.reduce(jnp.maximum, g))

    seg = gidx * n_mod
    if n_mod > 1:
        seg = seg + i // tiles_per_seq
    onehot = jnp.where(lax.broadcasted_iota(jnp.int32, (SEG_ROWS, tm), 0) == seg, 1.0, 0.0)
    before = (lax.broadcasted_iota(jnp.int32, (tm, tm), 0)
              < lax.broadcasted_iota(jnp.int32, (tm, tm), 1))
    prefix = _dot(onehot.astype(BF16), jnp.where(before, 1.0, 0.0).astype(BF16))
    carry = carry_ref[...]
    rank = jnp.sum(onehot * (prefix + carry[:, 0:1]), axis=0, keepdims=True)
    seg_ref[...] = seg
    rank_ref[...] = rank.astype(jnp.int32)
    carry = carry + jnp.sum(onehot, axis=1, keepdims=True)
    carry_ref[...] = carry
    cnt_ref[...] = carry.astype(jnp.int32)


def _merge(x2d, attn2d, fm2d, sga, sgf, mod3, mod_row_fn, wts, n_mod):
    n = x2d.shape[0]
    tok = lambda w: pl.BlockSpec((TM, w), lambda i: (i, 0))
    return pl.pallas_call(
        functools.partial(_merge_kernel, n_mod=n_mod, tiles_per_seq=n // n_mod // TM),
        grid=(n // TM,),
        in_specs=[tok(D_MODEL), tok(V_WIDTH), tok(FN_WIDTH), tok(D_MODEL), tok(D_MODEL),
                  pl.BlockSpec((1, 6, D_MODEL), lambda i: (mod_row_fn(i), 0, 0)),
                  _const_spec((V_WIDTH, D_MODEL)),
                  _const_spec((FN_WIDTH, D_MODEL)),
                  _const_spec((D_MODEL, D_MODEL)),
                  _const_spec((1, D_MODEL)),
                  _const_spec((ROUTER_ROWS, D_MODEL)),
                  _const_spec((ROUTER_ROWS, 1))],
        out_specs=[pl.BlockSpec((TM * TOKEN_ROWS, LANES), lambda i: (i, 0)),
                   pl.BlockSpec((1, TM), lambda i: (0, i)),
                   pl.BlockSpec((1, TM), lambda i: (0, i)), _const_spec((SEG_ROWS, LANES))],
        out_shape=[jax.ShapeDtypeStruct((n * TOKEN_ROWS, LANES), F32),
                   jax.ShapeDtypeStruct((1, n), jnp.int32),
                   jax.ShapeDtypeStruct((1, n), jnp.int32),
                   jax.ShapeDtypeStruct((SEG_ROWS, LANES), jnp.int32)],
        scratch_shapes=[pltpu.VMEM((SEG_ROWS, LANES), F32)],
        compiler_params=_cparams(("arbitrary",)),
        name="merge",
    )(x2d, attn2d, fm2d, sga, sgf, mod3, wts["w_ao"], wts["w_fn"], wts["w_out"], wts["g2"],
      wts["w_rg"], wts["b_rg"])


def _token_rows(ref, t):
    start = t * TOKEN_ROWS
    if not isinstance(t, int):
        start = pl.multiple_of(start, TOKEN_ROWS)
    return ref.at[pl.ds(start, TOKEN_ROWS)]


def _row_copy(src, dst, src_tok, dst_tok, sem):
    return pltpu.make_async_copy(_token_rows(src, src_tok), _token_rows(dst, dst_tok), sem)


def _combine_kernel(pos_ref, y_hbm, o_ref, buf, sem):
    i = pl.program_id(0)
    tm = o_ref.shape[0]

    def gather(tile_idx, slot):
        def start(r, c):
            _row_copy(y_hbm, buf.at[slot], pos_ref[tile_idx * tm + r], r, sem.at[slot]).start()
            return c

        lax.fori_loop(0, tm, start, 0, unroll=32)

    @pl.when(i == 0)
    def _():
        gather(0, 0)

    @pl.when(i + 1 < pl.num_programs(0))
    def _():
        gather(i + 1, (i + 1) % 2)

    slot = i % 2

    def wait(r, c):
        _row_copy(y_hbm, buf.at[slot], 0, 0, sem.at[slot]).wait()
        return c

    lax.fori_loop(0, tm, wait, 0, unroll=8)
    o_ref[...] = _load_token_major(buf.at[slot], tm)


def _combine(y_sorted, pos, n):
    return pl.pallas_call(
        _combine_kernel,
        grid_spec=pltpu.PrefetchScalarGridSpec(
            num_scalar_prefetch=1,
            grid=(n // TM,),
            in_specs=[pl.BlockSpec(memory_space=pl.ANY)],
            out_specs=pl.BlockSpec((TM, D_MODEL), lambda i, pos: (i, 0)),
            scratch_shapes=[pltpu.VMEM((2, TM * TOKEN_ROWS, LANES), F32),
                            pltpu.SemaphoreType.DMA((2,))]),
        out_shape=jax.ShapeDtypeStruct((n, D_MODEL), F32),
        compiler_params=_cparams(("arbitrary",)),
        name="moe_combine",
    )(pos, y_sorted)


def _moe_kernel(grp_ref, split_ref, on_ref, src_ref, x_hbm, mod_ref, g2_ref, wrg_ref, brg_ref,
                wre_ref, bre_ref, wg_ref, wu_ref, wd_ref, o_ref, xbuf, sem, *, tile, n_mod, row_base):
    i = pl.program_id(0)

    def mod_row(j):
        first = mod_ref[row_base, j:j + 1, :]
        if n_mod == 1:
            return first
        rows = lax.broadcasted_iota(jnp.int32, (tile, 1), 0)
        return jnp.where(rows < split_ref[i], first, mod_ref[row_base + 1, j:j + 1, :])

    def gather(tile_idx, slot):
        def start(r, c):
            _row_copy(x_hbm, xbuf.at[slot], src_ref[tile_idx * tile + r], r, sem.at[slot]).start()
            return c

        lax.fori_loop(0, tile, start, 0, unroll=32)

    @pl.when(jnp.logical_and(i == 0, on_ref[0] == 1))
    def _():
        gather(0, 0)

    nxt = jnp.minimum(i + 1, pl.num_programs(0) - 1)

    @pl.when(jnp.logical_and(i + 1 < pl.num_programs(0), on_ref[nxt] == 1))
    def _():
        gather(i + 1, (i + 1) % 2)

    @pl.when(on_ref[i] == 0)
    def _():
        o_ref[...] = jnp.zeros_like(o_ref)

    @pl.when(on_ref[i] == 1)
    def _():
        slot = i % 2

        def wait(r, c):
            _row_copy(x_hbm, xbuf.at[slot], 0, 0, sem.at[slot]).wait()
            return c

        lax.fori_loop(0, tile, wait, 0, unroll=8)
        x1 = _load_token_major(xbuf.at[slot], tile)
        h2_hi, h2_lo = _split_bf16(_modulate2(x1, g2_ref, mod_row(3), mod_row(4)))

        logits = _router_logits(jnp.concatenate([wrg_ref[...], wre_ref[0]], axis=0),
                                jnp.concatenate([brg_ref[...], bre_ref[0]], axis=0), h2_hi, h2_lo)
        g = _rows(logits, N_GROUPS)
        gmax = functools.reduce(jnp.maximum, g)
        p_top = 1.0 / functools.reduce(lambda p, q: p + q, [jnp.exp(gj - gmax) for gj in g])
        e = _rows(logits[ROUTER_ROWS:], EXPERTS_PER_GROUP)
        m1 = functools.reduce(jnp.maximum, e)
        i1 = _first_argmax(e, m1)
        rest = [jnp.where(i1 == j, -jnp.inf, e[j]) for j in range(EXPERTS_PER_GROUP)]
        m2 = functools.reduce(jnp.maximum, rest)
        i2 = _first_argmax(rest, m2)
        t = jnp.exp(m2 - m1)
        w1 = p_top / (1.0 + t)
        w2 = p_top * t / (1.0 + t)
        row = lax.broadcasted_iota(jnp.int32, (LANES, tile), 0)
        comb = (jnp.where(row == i1, w1, 0.0) + jnp.where(row == i2, w2, 0.0)).T

        acc = None
        for j in range(EXPERTS_PER_GROUP):
            a = _dot(h2_hi, wg_ref[j])
            u = _dot(h2_hi, wu_ref[j])
            act = (a * _sigmoid(a)) * u * comb[:, j:j + 1]
            y = _dot(act.astype(BF16), wd_ref[j])
            acc = y if acc is None else acc + y
        _store_token_major(o_ref, x1 + mod_row(5) * acc, tile)


def _moe(x1tm, src, maps, mod3, wts, w_gate, w_up, w_down, tile, n_mod, row_base):
    n_tiles = src.shape[0] // tile
    const = lambda shape: pl.BlockSpec(shape, lambda i, grp, split, on, src: (0,) * len(shape))
    by_group = lambda shape: pl.BlockSpec(
        shape, lambda i, grp, split, on, src: (grp[i],) + (0,) * (len(shape) - 1))
    return pl.pallas_call(
        functools.partial(_moe_kernel, tile=tile, n_mod=n_mod, row_base=row_base),
        grid_spec=pltpu.PrefetchScalarGridSpec(
            num_scalar_prefetch=4,
            grid=(n_tiles,),
            in_specs=[pl.BlockSpec(memory_space=pl.ANY),
                      const((MOD_ROWS, 6, D_MODEL)),
                      const((1, D_MODEL)),
                      const((ROUTER_ROWS, D_MODEL)), const((ROUTER_ROWS, 1)),
                      by_group((1, ROUTER_ROWS, D_MODEL)), by_group((1, ROUTER_ROWS, 1)),
                      by_group((EXPERTS_PER_GROUP, D_MODEL, D_EXPERT)),
                      by_group((EXPERTS_PER_GROUP, D_MODEL, D_EXPERT)),
                      by_group((EXPERTS_PER_GROUP, D_EXPERT, D_MODEL))],
            out_specs=pl.BlockSpec((tile * TOKEN_ROWS, LANES), lambda i, grp, split, on, src: (i, 0)),
            scratch_shapes=[pltpu.VMEM((2, tile * TOKEN_ROWS, LANES), F32),
                            pltpu.SemaphoreType.DMA((2,))]),
        out_shape=jax.ShapeDtypeStruct((n_tiles * tile * TOKEN_ROWS, LANES), F32),
        compiler_params=_cparams(("arbitrary",)),
        name="moe",
    )(*maps, src, x1tm, mod3, wts["g2"], wts["w_rg"], wts["b_rg"], wts["w_re"], wts["b_re"],
      w_gate, w_up, w_down)


def _invert_kernel(pos_ref, src_ref, *, n):
    def zero(p, c):
        src_ref[p] = 0
        return c

    lax.fori_loop(0, src_ref.shape[0], zero, 0, unroll=8)

    def put(t, c):
        src_ref[pos_ref[t]] = t
        return c

    lax.fori_loop(0, n, put, 0, unroll=8)


def _invert(pos, n_sorted):
    return pl.pallas_call(
        functools.partial(_invert_kernel, n=pos.shape[0]),
        grid_spec=pltpu.PrefetchScalarGridSpec(
            num_scalar_prefetch=1, grid=(1,), in_specs=[],
            out_specs=pl.BlockSpec(memory_space=pltpu.SMEM)),
        out_shape=jax.ShapeDtypeStruct((n_sorted,), jnp.int32),
        name="moe_invert",
    )(pos)


def _sort_plan(seg, rank, cnt, n_mod, tile, n_tiles):
    cnt = cnt.reshape(N_GROUPS, n_mod)
    n_tile_grp = (jnp.sum(cnt, axis=1) + tile - 1) // tile
    tile_end = jnp.cumsum(n_tile_grp)
    tile_start = tile_end - n_tile_grp
    total = tile_end[-1]
    seg_base = (tile_start * tile)[:, None] + jnp.cumsum(cnt, axis=1) - cnt
    pos = (seg_base.reshape(-1)[seg] + rank).astype(jnp.int32)
    src = _invert(pos, n_tiles * tile)
    i = jnp.arange(n_tiles, dtype=jnp.int32)
    grp = jnp.sum((jnp.minimum(i, total - 1)[:, None] >= tile_end[None, :]).astype(jnp.int32), axis=1)
    split = jnp.clip(cnt[grp, 0] - (i - tile_start[grp]) * tile, 0, tile)
    maps = (grp.astype(jnp.int32), split.astype(jnp.int32), (i < total).astype(jnp.int32))
    return pos, src, maps


def _sparse_moe(x1tm, seg, rank, cnt8, mod3, wts, w_gate, w_up, w_down, n_mod, row_base, tile):
    assert n_mod <= 2
    n = x1tm.shape[0] // TOKEN_ROWS
    n_tiles = n // tile + N_GROUPS
    pos, src, maps = _sort_plan(seg.reshape(n), rank.reshape(n), cnt8[:N_GROUPS * n_mod, 0], n_mod,
                                tile, n_tiles)
    y_sorted = _moe(x1tm, src, maps, mod3, wts, w_gate, w_up, w_down, tile, n_mod, row_base)
    return _combine(y_sorted, pos, n)


def _pack_weights(l, w_in, norm1_g, q_a_norm_g, w_q_b, kv_a_norm_g, w_kv_b, q_norm_g, k_norm_g,
                  w_attn_o, w_fnet, w_out, norm2_g, w_router_group, b_router_group,
                  w_router_expert, b_router_expert):
    wi = w_in[l]
    kpe_cols = jnp.pad(wi[:, C_KPE:C_KPE + QK_ROPE], ((0, 0), (QK_NOPE, LANES - QK_HEAD)))
    w_in_p = jnp.concatenate([wi[:, :C_KPE], kpe_cols, wi[:, C_KPE + QK_ROPE:]], axis=1)
    w_qb = jnp.pad(w_q_b[l].reshape(Q_LORA, N_HEADS, QK_HEAD),
                   ((0, 0), (0, 0), (0, HEAD_SLOT - QK_HEAD))).reshape(Q_LORA, QK_WIDTH)
    wkv = w_kv_b[l].reshape(KV_LORA, N_HEADS, QK_NOPE + V_HEAD)
    w_kb = jnp.pad(wkv[:, :, :QK_NOPE],
                   ((0, 0), (0, 0), (0, HEAD_SLOT - QK_NOPE))).reshape(KV_LORA, QK_WIDTH)
    w_vb = wkv[:, :, QK_NOPE:].reshape(KV_LORA, V_WIDTH)
    pad_g = lambda g: jnp.pad(g, (0, HEAD_SLOT - QK_HEAD)).reshape(1, HEAD_SLOT)
    w_rg = jnp.pad(w_router_group[l].T, ((0, ROUTER_ROWS - N_GROUPS), (0, 0)))
    b_rg = jnp.pad(b_router_group[l], (0, ROUTER_ROWS - N_GROUPS)).reshape(ROUTER_ROWS, 1)
    w_re = jnp.pad(w_router_expert[l].T.reshape(N_GROUPS, EXPERTS_PER_GROUP, D_MODEL),
                   ((0, 0), (0, ROUTER_ROWS - EXPERTS_PER_GROUP), (0, 0)))
    b_re = jnp.pad(b_router_expert[l].reshape(N_GROUPS, EXPERTS_PER_GROUP),
                   ((0, 0), (0, ROUTER_ROWS - EXPERTS_PER_GROUP))).reshape(N_GROUPS, ROUTER_ROWS, 1)
    dft_c, dft_ns = _dft_tables(FN_GROUP_W)
    return {
        "g1": norm1_g[l].reshape(1, D_MODEL),
        "w_in": w_in_p.astype(BF16),
        "qag": q_a_norm_g[l].reshape(1, Q_LORA),
        "w_qb": w_qb.astype(BF16),
        "kvg": kv_a_norm_g[l].reshape(1, KV_LORA),
        "w_kb": w_kb.astype(BF16),
        "w_vb": w_vb.T.astype(BF16),
        "qg": pad_g(q_norm_g[l]),
        "kg": pad_g(k_norm_g[l]),
        "dft_c": jnp.concatenate([jnp.asarray(dft_c), -jnp.asarray(dft_ns)], axis=1).astype(BF16),
        "w_ao": w_attn_o[l].astype(BF16),
        "w_fn": w_fnet[l].astype(BF16),
        "w_out": w_out[l].astype(BF16),
        "g2": norm2_g[l].reshape(1, D_MODEL),
        "w_rg": w_rg,
        "b_rg": b_rg,
        "w_re": w_re,
        "b_re": b_re,
    }


def _layer(xp, xs, cache_ckv_l, cache_kpe_l, mod3, wts, w_gate, w_up, w_down):
    bp, sp, _ = xp.shape
    bs, ss, _ = xs.shape
    past = cache_ckv_l.shape[1]
    ctx_row = lambda i: 0
    lat_row = lambda i: 1 + i // (ss // TM)

    xp2 = xp.reshape(bp * sp, D_MODEL)
    q, k, v, fcs, sga, sgf, ckv, kpe = _inproj(xp2, mod3, ctx_row, wts, None, True)
    attn = _attention(q.reshape(bp, sp, QK_WIDTH), [k.reshape(bp, sp, QK_WIDTH)],
                      [v], sp, "attn_ctx")
    cs, ns = (jnp.asarray(t).astype(BF16) for t in _dft_tables(sp))
    fm = _fourier(fcs.reshape(bp, sp, 2 * FN_WIDTH), cs, ns, sp, SEQ_PER_STEP_CTX, "fourier_ctx")
    x1tm, seg, rank, cnt = _merge(xp2, attn.reshape(bp * sp, V_WIDTH), fm.reshape(bp * sp, FN_WIDTH),
                                  sga, sgf, mod3, ctx_row, wts, 1)
    yp = _sparse_moe(x1tm, seg, rank, cnt, mod3, wts, w_gate, w_up, w_down, 1, 0,
                     MOE_TILE_CTX).reshape(bp, sp, D_MODEL)

    xs2 = xs.reshape(bs * ss, D_MODEL)
    rope_tabs = tuple(jnp.asarray(t) for t in _rope_tables(ss))
    q, k, v, fcs, sga, sgf = _inproj(xs2, mod3, lat_row, wts, rope_tabs, False)
    kpe_slot = jnp.pad(cache_kpe_l, ((0, 0), (0, 0), (QK_NOPE, LANES - QK_HEAD)))
    kc, vc = _cache_kv(cache_ckv_l.reshape(bs * past, KV_LORA), kpe_slot.reshape(bs * past, LANES), wts)
    attn = _attention(q.reshape(bs, ss, QK_WIDTH),
                      [kc.reshape(bs, past, QK_WIDTH), k.reshape(bs, ss, QK_WIDTH)],
                      [vc, v], TQ_LAT, "attn_lat")
    cs, ns = (jnp.asarray(t).astype(BF16) for t in _dft_tables(ss))
    fm = _fourier(fcs.reshape(bs, ss, 2 * FN_WIDTH), cs, ns, TQ_LAT, 1, "fourier_lat")
    assert N_GROUPS * bs <= SEG_ROWS
    x1tm, seg, rank, cnt = _merge(xs2, attn.reshape(bs * ss, V_WIDTH), fm.reshape(bs * ss, FN_WIDTH),
                                  sga, sgf, mod3, lat_row, wts, bs)
    ys = _sparse_moe(x1tm, seg, rank, cnt, mod3, wts, w_gate, w_up, w_down, bs, 1,
                     MOE_TILE_LAT).reshape(bs, ss, D_MODEL)

    return yp, ys, ckv.reshape(bp, sp, KV_LORA), kpe.reshape(bp, sp, QK_ROPE)


def kernel(x_prompt, x_sample, cache_ckv, cache_kpe, c, c_ctx, w_mod, b_mod, norm1_g, w_in, q_a_norm_g, w_q_b, kv_a_norm_g, w_kv_b, q_norm_g, k_norm_g, w_attn_o, w_fnet, w_out, norm2_g, w_router_group, b_router_group, w_router_expert, b_router_expert, w_exp_gate, w_exp_up, w_exp_down):
    depth = w_mod.shape[0]
    n_lat = c.shape[0]
    assert 1 + n_lat <= MOD_ROWS
    cond8 = jnp.concatenate([c_ctx[None, :], c, jnp.zeros((MOD_ROWS - 1 - n_lat, D_MODEL), F32)], axis=0)
    xp, xs = x_prompt, x_sample
    ckv_layers, kpe_layers = [], []
    for l in range(depth):
        mod3 = _adaln(cond8, w_mod[l], b_mod[l].reshape(1, -1)).reshape(MOD_ROWS, 6, D_MODEL)
        wts = _pack_weights(l, w_in, norm1_g, q_a_norm_g, w_q_b, kv_a_norm_g, w_kv_b, q_norm_g,
                            k_norm_g, w_attn_o, w_fnet, w_out, norm2_g, w_router_group,
                            b_router_group, w_router_expert, b_router_expert)
        xp, xs, ckv, kpe = _layer(xp, xs, cache_ckv[:, l], cache_kpe[:, l], mod3, wts,
                                  w_exp_gate[l].astype(BF16), w_exp_up[l].astype(BF16),
                                  w_exp_down[l].astype(BF16))
        ckv_layers.append(ckv)
        kpe_layers.append(kpe)
    return xp, xs, jnp.stack(ckv_layers, axis=1), jnp.stack(kpe_layers, axis=1)
```

```python
import functools
import math

import numpy as np
import jax
import jax.numpy as jnp
from jax import lax
from jax.experimental import pallas as pl
from jax.experimental.pallas import tpu as pltpu

D_MODEL = 1024
GRID_W = 64
N_HEADS = 8
Q_LORA = 512
KV_LORA = 256
QK_NOPE = 64
QK_ROPE = 32
V_HEAD = 64
QK_HEAD = QK_NOPE + QK_ROPE
ATTN_SCALE = QK_HEAD ** -0.5
ROPE_BASE = 10000.0
FN_GROUPS = 4
FN_GROUP_W = 128
FN_WIDTH = FN_GROUPS * FN_GROUP_W
N_GROUPS = 4
EXPERTS_PER_GROUP = 4
N_EXPERTS = N_GROUPS * EXPERTS_PER_GROUP
D_EXPERT = 512
EPS = 1e-6

LANES = 128
HEAD_SLOT = LANES
QK_WIDTH = N_HEADS * HEAD_SLOT
V_WIDTH = N_HEADS * V_HEAD
C_QA = 0
C_KVA = C_QA + Q_LORA
C_KPE = C_KVA + KV_LORA
C_FN = C_KPE + LANES
C_GA = C_FN + FN_WIDTH
C_GF = C_GA + D_MODEL
IN_PACKED = C_GF + D_MODEL
SUBLANES = 8
ROUTER_ROWS = SUBLANES
MOD_ROWS = SUBLANES
VMEM_LIMIT = 56 * 1024 * 1024

TM = 512
TQ_LAT = 256
SEQ_PER_STEP_CTX = 4
SEG_ROWS = SUBLANES
TOKEN_ROWS = D_MODEL // LANES
MOE_TILE = 256

BF16 = jnp.bfloat16
F32 = jnp.float32


def _cparams(sem):
    return pltpu.CompilerParams(dimension_semantics=sem, vmem_limit_bytes=VMEM_LIMIT)


def _dot(a, b):
    return jnp.dot(a, b, preferred_element_type=F32)


def _dot_nt(a, b):
    return lax.dot_general(a, b, (((1,), (1,)), ((), ())), preferred_element_type=F32)


def _sigmoid(x):
    return 1.0 / (1.0 + jnp.exp(-x))


def _split_bf16(x):
    hi = x.astype(BF16)
    return hi, (x - hi.astype(F32)).astype(BF16)


@functools.lru_cache(maxsize=None)
def _rope_tables(n_pos):
    half = QK_ROPE // 2
    quarter = half // 2
    freqs = ROPE_BASE ** (-np.arange(quarter, dtype=np.float64) / quarter)
    pos = np.arange(n_pos)
    row = (pos // GRID_W).astype(np.float64)
    col = (pos % GRID_W).astype(np.float64)
    cos_t = np.ones((n_pos, LANES), np.float64)
    sin_a = np.zeros((n_pos, LANES), np.float64)
    sin_b = np.zeros((n_pos, LANES), np.float64)
    for base, p in ((QK_NOPE, row), (QK_NOPE + half, col)):
        ang = p[:, None] * freqs[None, :]
        cos_t[:, base:base + quarter] = np.cos(ang)
        cos_t[:, base + quarter:base + half] = np.cos(ang)
        sin_a[:, base:base + quarter] = -np.sin(ang)
        sin_b[:, base + quarter:base + half] = np.sin(ang)
    return (cos_t.astype(np.float32), sin_a.astype(np.float32), sin_b.astype(np.float32))


@functools.lru_cache(maxsize=None)
def _dft_tables(n):
    k = np.arange(n)
    ang = 2.0 * np.pi * ((k[:, None] * k[None, :]) % n) / n
    s = 1.0 / math.sqrt(n)
    return (np.cos(ang) * s).astype(np.float32), (-np.sin(ang) * s).astype(np.float32)


def _adaln_kernel(cond_ref, w_ref, b_ref, o_ref):
    c = cond_ref[...]
    s_hi, s_lo = _split_bf16(c * _sigmoid(c))
    w_hi, w_lo = _split_bf16(w_ref[...])
    y = _dot(jnp.concatenate([s_hi, s_lo], axis=0), w_hi)
    o_ref[...] = (y[:MOD_ROWS] + y[MOD_ROWS:]) + _dot(s_hi, w_lo) + b_ref[...]


def _adaln(cond8, w_mod, b_mod):
    n = w_mod.shape[1]
    tn = D_MODEL
    return pl.pallas_call(
        _adaln_kernel,
        grid=(n // tn,),
        in_specs=[pl.BlockSpec((MOD_ROWS, D_MODEL), lambda j: (0, 0)),
                  pl.BlockSpec((D_MODEL, tn), lambda j: (0, j)),
                  pl.BlockSpec((1, tn), lambda j: (0, j))],
        out_specs=pl.BlockSpec((MOD_ROWS, tn), lambda j: (0, j)),
        out_shape=jax.ShapeDtypeStruct((MOD_ROWS, n), F32),
        compiler_params=_cparams(("arbitrary",)),
        name="adaln",
    )(cond8, w_mod, b_mod)


def _rms(x, width):
    return lax.rsqrt(jnp.sum(x * x, axis=-1, keepdims=True) * (1.0 / width) + EPS)


def _rope(x, cos_t, sin_a, sin_b):
    return x * cos_t + pltpu.roll(x, LANES - 8, 1) * sin_a + pltpu.roll(x, 8, 1) * sin_b


def _inproj_kernel(*refs, rope, emit_cache):
    it = iter(refs)
    x_ref, mod_ref, g1_ref, win_ref, qag_ref, wqb_ref, kvg_ref, wkb_ref, wvb_ref = (
        next(it) for _ in range(9))
    qg_ref, kg_ref, dft_ref = next(it), next(it), next(it)
    if rope:
        cos_ref, sa_ref, sb_ref = next(it), next(it), next(it)
    q_ref, k_ref, v_ref, fcs_ref, sga_ref, sgf_ref = (next(it) for _ in range(6))
    if emit_cache:
        ckv_ref, kpe_ref = next(it), next(it)

    x = x_ref[...]
    shift = mod_ref[0, 0:1, :]
    scale = mod_ref[0, 1:2, :]
    h = (x * _rms(x, D_MODEL) * g1_ref[...]) * (1.0 + scale) + shift
    hb = h.astype(BF16)

    if rope:
        cos_t, sin_a, sin_b = cos_ref[...], sa_ref[...], sb_ref[...]

    qa = _dot(hb, win_ref[:, C_QA:C_QA + Q_LORA])
    qn = (qa * _rms(qa, Q_LORA) * qag_ref[...]).astype(BF16)
    q = _dot(qn, wqb_ref[...])
    qg = qg_ref[...] * ATTN_SCALE
    for hd in range(N_HEADS):
        qh = q[:, hd * HEAD_SLOT:(hd + 1) * HEAD_SLOT]
        qh = qh * _rms(qh, QK_HEAD) * qg
        if rope:
            qh = _rope(qh, cos_t, sin_a, sin_b)
        q_ref[:, hd * HEAD_SLOT:(hd + 1) * HEAD_SLOT] = qh.astype(BF16)

    kva = _dot(hb, win_ref[:, C_KVA:C_KVA + KV_LORA])
    ckv = kva * _rms(kva, KV_LORA) * kvg_ref[...]
    kpe = _dot(hb, win_ref[:, C_KPE:C_KPE + LANES])
    if emit_cache:
        ckv_ref[...] = ckv
        kpe_ref[...] = kpe[:, QK_NOPE:QK_NOPE + QK_ROPE]
    _emit_kv(ckv.astype(BF16), kpe, wkb_ref, wvb_ref, kg_ref,
             (cos_t, sin_a, sin_b) if rope else None, k_ref, v_ref)

    fn = _dot(hb, win_ref[:, C_FN:C_FN + FN_WIDTH]).astype(BF16)
    for g in range(FN_GROUPS):
        cs = _dot(fn[:, g * FN_GROUP_W:(g + 1) * FN_GROUP_W], dft_ref[...])
        fcs_ref[:, g * FN_GROUP_W:(g + 1) * FN_GROUP_W] = cs[:, :FN_GROUP_W].astype(BF16)
        fcs_ref[:, FN_WIDTH + g * FN_GROUP_W:FN_WIDTH + (g + 1) * FN_GROUP_W] = (
            cs[:, FN_GROUP_W:].astype(BF16))

    sga_ref[...] = _sigmoid(_dot(hb, win_ref[:, C_GA:C_GA + D_MODEL])).astype(BF16)
    sgf_ref[...] = _sigmoid(_dot(hb, win_ref[:, C_GF:C_GF + D_MODEL])).astype(BF16)


def _emit_kv(ckvb, kpe, wkb_ref, wvb_ref, kg_ref, rope_tabs, k_ref, v_ref):
    kg = kg_ref[...]
    v_ref[...] = _dot_nt(wvb_ref[...], ckvb).astype(BF16)
    kn = _dot(ckvb, wkb_ref[...])
    pe_ss = jnp.sum(kpe * kpe, axis=-1, keepdims=True)
    pe_g = kpe * kg
    if rope_tabs is not None:
        pe_g = _rope(pe_g, *rope_tabs)
    for hd in range(N_HEADS):
        knh = kn[:, hd * HEAD_SLOT:(hd + 1) * HEAD_SLOT]
        ss = jnp.sum(knh * knh, axis=-1, keepdims=True) + pe_ss
        r = lax.rsqrt(ss * (1.0 / QK_HEAD) + EPS)
        k_ref[:, hd * HEAD_SLOT:(hd + 1) * HEAD_SLOT] = ((knh * kg + pe_g) * r).astype(BF16)


def _const_spec(shape):
    return pl.BlockSpec(shape, lambda i: (0,) * len(shape))


def _inproj(x2d, mod3, mod_row_fn, wts, rope_tabs, emit_cache):
    n = x2d.shape[0]
    rope = rope_tabs is not None
    tiles_per_seq = None if not rope else rope_tabs[0].shape[0] // TM
    in_specs = [pl.BlockSpec((TM, D_MODEL), lambda i: (i, 0)),
                pl.BlockSpec((1, 6, D_MODEL), lambda i: (mod_row_fn(i), 0, 0)),
                _const_spec((1, D_MODEL)),
                _const_spec((D_MODEL, IN_PACKED)),
                _const_spec((1, Q_LORA)),
                _const_spec((Q_LORA, QK_WIDTH)),
                _const_spec((1, KV_LORA)),
                _const_spec((KV_LORA, QK_WIDTH)),
                _const_spec((V_WIDTH, KV_LORA)),
                _const_spec((1, HEAD_SLOT)),
                _const_spec((1, HEAD_SLOT)),
                _const_spec((FN_GROUP_W, 2 * FN_GROUP_W))]
    args = [x2d, mod3, wts["g1"], wts["w_in"], wts["qag"], wts["w_qb"], wts["kvg"],
            wts["w_kb"], wts["w_vb"], wts["qg"], wts["kg"], wts["dft_c"]]
    if rope:
        in_specs += [pl.BlockSpec((TM, LANES), lambda i: (i % tiles_per_seq, 0))] * 3
        args += list(rope_tabs)
    out_shape = [jax.ShapeDtypeStruct((n, QK_WIDTH), BF16),
                 jax.ShapeDtypeStruct((n, QK_WIDTH), BF16),
                 jax.ShapeDtypeStruct((V_WIDTH, n), BF16),
                 jax.ShapeDtypeStruct((n, 2 * FN_WIDTH), BF16),
                 jax.ShapeDtypeStruct((n, D_MODEL), BF16),
                 jax.ShapeDtypeStruct((n, D_MODEL), BF16)]
    out_specs = [pl.BlockSpec((TM, s.shape[1]), lambda i: (i, 0)) for s in out_shape]
    out_specs[2] = pl.BlockSpec((V_WIDTH, TM), lambda i: (0, i))
    if emit_cache:
        out_shape += [jax.ShapeDtypeStruct((n, KV_LORA), F32),
                      jax.ShapeDtypeStruct((n, QK_ROPE), F32)]
        out_specs += [pl.BlockSpec((TM, KV_LORA), lambda i: (i, 0)),
                      pl.BlockSpec((TM, QK_ROPE), lambda i: (i, 0))]
    return pl.pallas_call(
        functools.partial(_inproj_kernel, rope=rope, emit_cache=emit_cache),
        grid=(n // TM,),
        in_specs=in_specs,
        out_specs=out_specs,
        out_shape=out_shape,
        compiler_params=_cparams(("parallel",)),
        name="inproj_lat" if rope else "inproj_ctx",
    )(*args)


def _cache_kv_kernel(ckv_ref, kpe_ref, wkb_ref, wvb_ref, kg_ref, k_ref, v_ref):
    _emit_kv(ckv_ref[...].astype(BF16), kpe_ref[...], wkb_ref, wvb_ref, kg_ref, None, k_ref, v_ref)


def _cache_kv(ckv2d, kpe_slot2d, wts):
    n = ckv2d.shape[0]
    return pl.pallas_call(
        _cache_kv_kernel,
        grid=(n // TM,),
        in_specs=[pl.BlockSpec((TM, KV_LORA), lambda i: (i, 0)),
                  pl.BlockSpec((TM, LANES), lambda i: (i, 0)),
                  _const_spec((KV_LORA, QK_WIDTH)),
                  _const_spec((V_WIDTH, KV_LORA)),
                  _const_spec((1, HEAD_SLOT))],
        out_specs=[pl.BlockSpec((TM, QK_WIDTH), lambda i: (i, 0)),
                   pl.BlockSpec((V_WIDTH, TM), lambda i: (0, i))],
        out_shape=[jax.ShapeDtypeStruct((n, QK_WIDTH), BF16),
                   jax.ShapeDtypeStruct((V_WIDTH, n), BF16)],
        compiler_params=_cparams(("parallel",)),
        name="cache_kv",
    )(ckv2d, kpe_slot2d, wts["w_kb"], wts["w_vb"], wts["kg"])


def _attn_kernel(*refs, n_kv):
    q_ref = refs[0]
    k_refs = refs[1:1 + n_kv]
    vt_refs = refs[1 + n_kv:1 + 2 * n_kv]
    o_ref = refs[1 + 2 * n_kv]
    head = lambda hd: slice(hd * HEAD_SLOT, (hd + 1) * HEAD_SLOT)
    st = [jnp.stack([_dot_nt(k[0, :, head(hd)], q_ref[0, :, head(hd)]) for hd in range(N_HEADS)])
          for k in k_refs]
    m = functools.reduce(jnp.maximum, [sj.max(axis=1, keepdims=True) for sj in st])
    p = [jnp.exp(sj - m) for sj in st]
    l = functools.reduce(lambda a, b: a + b, [pj.sum(axis=1, keepdims=True) for pj in p])
    outs = []
    for hd in range(N_HEADS):
        o = functools.reduce(lambda a, b: a + b,
                             [_dot(vt[hd * V_HEAD:(hd + 1) * V_HEAD, :], pj[hd].astype(BF16))
                              for vt, pj in zip(vt_refs, p)])
        outs.append(o / l[hd])
    o_ref[0] = jnp.concatenate(outs, axis=0).T.astype(BF16)


def _attention(q3, ks, vts, tq, name):
    b, sq, _ = q3.shape
    n_kv = len(ks)
    in_specs = [pl.BlockSpec((1, tq, QK_WIDTH), lambda bi, qi: (bi, qi, 0))]
    in_specs += [pl.BlockSpec((1, k.shape[1], QK_WIDTH), lambda bi, qi: (bi, 0, 0)) for k in ks]
    in_specs += [pl.BlockSpec((V_WIDTH, k.shape[1]), lambda bi, qi: (0, bi)) for k in ks]
    return pl.pallas_call(
        functools.partial(_attn_kernel, n_kv=n_kv),
        grid=(b, sq // tq),
        in_specs=in_specs,
        out_specs=pl.BlockSpec((1, tq, V_WIDTH), lambda bi, qi: (bi, qi, 0)),
        out_shape=jax.ShapeDtypeStruct((b, sq, V_WIDTH), BF16),
        compiler_params=_cparams(("parallel", "parallel")),
        name=name,
    )(q3, *ks, *vts)


def _fourier_kernel(cs_ref, ns_ref, f_ref, o_ref, *, n_b):
    for b in range(n_b):
        xc = f_ref[b, :, :FN_WIDTH]
        xs = f_ref[b, :, FN_WIDTH:]
        o_ref[b] = (_dot(cs_ref[...], xc) + _dot(ns_ref[...], xs)).astype(BF16)


def _fourier(fcs3, cs, ns, tr, n_b, name):
    b, s, _ = fcs3.shape
    return pl.pallas_call(
        functools.partial(_fourier_kernel, n_b=n_b),
        grid=(b // n_b, s // tr),
        in_specs=[pl.BlockSpec((tr, s), lambda bi, ri: (ri, 0)),
                  pl.BlockSpec((tr, s), lambda bi, ri: (ri, 0)),
                  pl.BlockSpec((n_b, s, 2 * FN_WIDTH), lambda bi, ri: (bi, 0, 0))],
        out_specs=pl.BlockSpec((n_b, tr, FN_WIDTH), lambda bi, ri: (bi, ri, 0)),
        out_shape=jax.ShapeDtypeStruct((b, s, FN_WIDTH), BF16),
        compiler_params=_cparams(("parallel", "parallel")),
        name=name,
    )(cs, ns, fcs3)


def _router_logits(w, b, h_hi, h_lo):
    rows = w.shape[0]
    w_hi, w_lo = _split_bf16(w)
    y = _dot_nt(jnp.concatenate([w_hi, w_lo], axis=0), h_hi)
    return (y[:rows] + y[rows:]) + _dot_nt(w_hi, h_lo) + b


def _rows(x, n):
    return [x[j:j + 1, :] for j in range(n)]


def _first_argmax(rows, top):
    idx = jnp.full(top.shape, len(rows) - 1, jnp.int32)
    for j in range(len(rows) - 2, -1, -1):
        idx = jnp.where(rows[j] == top, j, idx)
    return idx


def _store_token_major(ref, x, tm):
    for s in range(TOKEN_ROWS):
        ref[pl.ds(s, tm, stride=TOKEN_ROWS), :] = x[:, s * LANES:(s + 1) * LANES]


def _load_token_major(ref, tm):
    return jnp.concatenate([ref[pl.ds(s, tm, stride=TOKEN_ROWS), :] for s in range(TOKEN_ROWS)],
                           axis=1)


def _merge_kernel(x_ref, attn_ref, fm_ref, sga_ref, sgf_ref, mod_ref, wao_ref, wfn_ref, wout_ref,
                  g2_ref, wrg_ref, brg_ref, x1_ref, h2_ref, grp_ref, rank_ref, cnt_ref, carry_ref):
    i = pl.program_id(0)

    @pl.when(i == 0)
    def _():
        carry_ref[...] = jnp.zeros_like(carry_ref)

    a = _dot(attn_ref[...], wao_ref[...])
    f = _dot(fm_ref[...], wfn_ref[...])
    u = sga_ref[...].astype(F32) * a + sgf_ref[...].astype(F32) * f
    y = _dot(u.astype(BF16), wout_ref[...])
    x1 = x_ref[...] + mod_ref[0, 2:3, :] * y
    x1_ref[...] = x1
    tm = x1.shape[0]
    h2 = (x1 * _rms(x1, D_MODEL) * g2_ref[...]) * (1.0 + mod_ref[0, 4:5, :]) + mod_ref[0, 3:4, :]
    _store_token_major(h2_ref, h2, tm)

    h2_hi, h2_lo = _split_bf16(h2)
    g = _rows(_router_logits(wrg_ref[...], brg_ref[...], h2_hi, h2_lo), N_GROUPS)
    gidx = _first_argmax(g, functools.reduce(jnp.maximum, g))

    onehot = jnp.where(lax.broadcasted_iota(jnp.int32, (SEG_ROWS, tm), 0) == gidx, 1.0, 0.0)
    before = (lax.broadcasted_iota(jnp.int32, (tm, tm), 0)
              < lax.broadcasted_iota(jnp.int32, (tm, tm), 1))
    prefix = _dot(onehot.astype(BF16), jnp.where(before, 1.0, 0.0).astype(BF16))
    carry = carry_ref[...]
    rank = jnp.sum(onehot * (prefix + carry[:, 0:1]), axis=0, keepdims=True)
    grp_ref[...] = gidx
    rank_ref[...] = rank.astype(jnp.int32)
    carry = carry + jnp.sum(onehot, axis=1, keepdims=True)
    carry_ref[...] = carry
    cnt_ref[...] = carry.astype(jnp.int32)


def _merge(x2d, attn2d, fm2d, sga, sgf, mod3, mod_row_fn, wts):
    n = x2d.shape[0]
    tok = lambda w: pl.BlockSpec((TM, w), lambda i: (i, 0))
    return pl.pallas_call(
        _merge_kernel,
        grid=(n // TM,),
        in_specs=[tok(D_MODEL), tok(V_WIDTH), tok(FN_WIDTH), tok(D_MODEL), tok(D_MODEL),
                  pl.BlockSpec((1, 6, D_MODEL), lambda i: (mod_row_fn(i), 0, 0)),
                  _const_spec((V_WIDTH, D_MODEL)),
                  _const_spec((FN_WIDTH, D_MODEL)),
                  _const_spec((D_MODEL, D_MODEL)),
                  _const_spec((1, D_MODEL)),
                  _const_spec((ROUTER_ROWS, D_MODEL)),
                  _const_spec((ROUTER_ROWS, 1))],
        out_specs=[tok(D_MODEL),
                   pl.BlockSpec((TM * TOKEN_ROWS, LANES), lambda i: (i, 0)),
                   pl.BlockSpec((1, TM), lambda i: (0, i)),
                   pl.BlockSpec((1, TM), lambda i: (0, i)), _const_spec((SEG_ROWS, LANES))],
        out_shape=[jax.ShapeDtypeStruct((n, D_MODEL), F32),
                   jax.ShapeDtypeStruct((n * TOKEN_ROWS, LANES), F32),
                   jax.ShapeDtypeStruct((1, n), jnp.int32),
                   jax.ShapeDtypeStruct((1, n), jnp.int32),
                   jax.ShapeDtypeStruct((SEG_ROWS, LANES), jnp.int32)],
        scratch_shapes=[pltpu.VMEM((SEG_ROWS, LANES), F32)],
        compiler_params=_cparams(("arbitrary",)),
        name="merge",
    )(x2d, attn2d, fm2d, sga, sgf, mod3, wts["w_ao"], wts["w_fn"], wts["w_out"], wts["g2"],
      wts["w_rg"], wts["b_rg"])


def _token_rows(ref, t):
    start = t * TOKEN_ROWS
    if not isinstance(t, int):
        start = pl.multiple_of(start, TOKEN_ROWS)
    return ref.at[pl.ds(start, TOKEN_ROWS)]


def _row_copy(src, dst, src_tok, dst_tok, sem):
    return pltpu.make_async_copy(_token_rows(src, src_tok), _token_rows(dst, dst_tok), sem)


def _combine_kernel(pos_ref, y_hbm, x1_ref, mod_ref, o_ref, buf, sem):
    i = pl.program_id(0)
    tm = o_ref.shape[0]

    def gather(tile_idx, slot):
        def start(r, c):
            _row_copy(y_hbm, buf.at[slot], pos_ref[tile_idx * tm + r], r, sem.at[slot]).start()
            return c

        lax.fori_loop(0, tm, start, 0, unroll=32)

    @pl.when(i == 0)
    def _():
        gather(0, 0)

    @pl.when(i + 1 < pl.num_programs(0))
    def _():
        gather(i + 1, (i + 1) % 2)

    slot = i % 2

    def wait(r, c):
        _row_copy(y_hbm, buf.at[slot], 0, 0, sem.at[slot]).wait()
        return c

    lax.fori_loop(0, tm, wait, 0, unroll=8)
    o_ref[...] = x1_ref[...] + mod_ref[0, 5:6, :] * _load_token_major(buf.at[slot], tm)


def _combine(y_sorted, pos, x1, mod3, mod_row_fn):
    n = x1.shape[0]
    return pl.pallas_call(
        _combine_kernel,
        grid_spec=pltpu.PrefetchScalarGridSpec(
            num_scalar_prefetch=1,
            grid=(n // TM,),
            in_specs=[pl.BlockSpec(memory_space=pl.ANY),
                      pl.BlockSpec((TM, D_MODEL), lambda i, pos: (i, 0)),
                      pl.BlockSpec((1, 6, D_MODEL), lambda i, pos: (mod_row_fn(i), 0, 0))],
            out_specs=pl.BlockSpec((TM, D_MODEL), lambda i, pos: (i, 0)),
            scratch_shapes=[pltpu.VMEM((2, TM * TOKEN_ROWS, LANES), F32),
                            pltpu.SemaphoreType.DMA((2,))]),
        out_shape=jax.ShapeDtypeStruct((n, D_MODEL), F32),
        compiler_params=_cparams(("arbitrary",)),
        name="moe_combine",
    )(pos, y_sorted, x1, mod3)


def _moe_kernel(grp_ref, on_ref, src_ref, h_hbm, wrg_ref, brg_ref, wre_ref, bre_ref, wg_ref, wu_ref,
                wd_ref, o_ref, hbuf, sem, *, tile):
    i = pl.program_id(0)

    def gather(tile_idx, slot):
        def start(r, c):
            _row_copy(h_hbm, hbuf.at[slot], src_ref[tile_idx * tile + r], r, sem.at[slot]).start()
            return c

        lax.fori_loop(0, tile, start, 0, unroll=32)

    @pl.when(jnp.logical_and(i == 0, on_ref[0] == 1))
    def _():
        gather(0, 0)

    nxt = jnp.minimum(i + 1, pl.num_programs(0) - 1)

    @pl.when(jnp.logical_and(i + 1 < pl.num_programs(0), on_ref[nxt] == 1))
    def _():
        gather(i + 1, (i + 1) % 2)

    @pl.when(on_ref[i] == 0)
    def _():
        o_ref[...] = jnp.zeros_like(o_ref)

    @pl.when(on_ref[i] == 1)
    def _():
        slot = i % 2

        def wait(r, c):
            _row_copy(h_hbm, hbuf.at[slot], 0, 0, sem.at[slot]).wait()
            return c

        lax.fori_loop(0, tile, wait, 0, unroll=8)
        h2_hi, h2_lo = _split_bf16(_load_token_major(hbuf.at[slot], tile))

        logits = _router_logits(jnp.concatenate([wrg_ref[...], wre_ref[0]], axis=0),
                                jnp.concatenate([brg_ref[...], bre_ref[0]], axis=0), h2_hi, h2_lo)
        g = _rows(logits, N_GROUPS)
        gmax = functools.reduce(jnp.maximum, g)
        p_top = 1.0 / functools.reduce(lambda p, q: p + q, [jnp.exp(gj - gmax) for gj in g])
        e = _rows(logits[ROUTER_ROWS:], EXPERTS_PER_GROUP)
        m1 = functools.reduce(jnp.maximum, e)
        i1 = _first_argmax(e, m1)
        rest = [jnp.where(i1 == j, -jnp.inf, e[j]) for j in range(EXPERTS_PER_GROUP)]
        m2 = functools.reduce(jnp.maximum, rest)
        i2 = _first_argmax(rest, m2)
        t = jnp.exp(m2 - m1)
        w1 = p_top / (1.0 + t)
        w2 = p_top * t / (1.0 + t)
        row = lax.broadcasted_iota(jnp.int32, (LANES, tile), 0)
        comb = (jnp.where(row == i1, w1, 0.0) + jnp.where(row == i2, w2, 0.0)).T

        acc = None
        for j in range(EXPERTS_PER_GROUP):
            a = _dot(h2_hi, wg_ref[j])
            u = _dot(h2_hi, wu_ref[j])
            act = (a * _sigmoid(a)) * u * comb[:, j:j + 1]
            y = _dot(act.astype(BF16), wd_ref[j])
            acc = y if acc is None else acc + y
        _store_token_major(o_ref, acc, tile)


def _moe(h2tm, src, maps, wts, w_gate, w_up, w_down, tile):
    n_tiles = src.shape[0] // tile
    const = lambda shape: pl.BlockSpec(shape, lambda i, grp, on, src: (0,) * len(shape))
    by_group = lambda shape: pl.BlockSpec(
        shape, lambda i, grp, on, src: (grp[i],) + (0,) * (len(shape) - 1))
    return pl.pallas_call(
        functools.partial(_moe_kernel, tile=tile),
        grid_spec=pltpu.PrefetchScalarGridSpec(
            num_scalar_prefetch=3,
            grid=(n_tiles,),
            in_specs=[pl.BlockSpec(memory_space=pl.ANY),
                      const((ROUTER_ROWS, D_MODEL)), const((ROUTER_ROWS, 1)),
                      by_group((1, ROUTER_ROWS, D_MODEL)), by_group((1, ROUTER_ROWS, 1)),
                      by_group((EXPERTS_PER_GROUP, D_MODEL, D_EXPERT)),
                      by_group((EXPERTS_PER_GROUP, D_MODEL, D_EXPERT)),
                      by_group((EXPERTS_PER_GROUP, D_EXPERT, D_MODEL))],
            out_specs=pl.BlockSpec((tile * TOKEN_ROWS, LANES), lambda i, grp, on, src: (i, 0)),
            scratch_shapes=[pltpu.VMEM((2, tile * TOKEN_ROWS, LANES), F32),
                            pltpu.SemaphoreType.DMA((2,))]),
        out_shape=jax.ShapeDtypeStruct((n_tiles * tile * TOKEN_ROWS, LANES), F32),
        compiler_params=_cparams(("arbitrary",)),
        name="moe",
    )(*maps, src, h2tm, wts["w_rg"], wts["b_rg"], wts["w_re"], wts["b_re"], w_gate, w_up, w_down)


def _invert_kernel(pos_ref, lo_ref, hi_ref, src_ref, *, n, n_fill):
    def zero(p, c):
        src_ref[p] = 0
        return c

    for s in range(n_fill):
        lax.fori_loop(lo_ref[s], hi_ref[s], zero, 0)

    def put(t, c):
        src_ref[pos_ref[t]] = t
        return c

    lax.fori_loop(0, n, put, 0, unroll=8)


def _invert(pos, fill_lo, fill_hi, n_sorted):
    return pl.pallas_call(
        functools.partial(_invert_kernel, n=pos.shape[0], n_fill=fill_lo.shape[0]),
        grid_spec=pltpu.PrefetchScalarGridSpec(
            num_scalar_prefetch=3, grid=(1,), in_specs=[],
            out_specs=pl.BlockSpec(memory_space=pltpu.SMEM)),
        out_shape=jax.ShapeDtypeStruct((n_sorted,), jnp.int32),
        name="moe_invert",
    )(pos, fill_lo, fill_hi)


def _sort_plan(grp, rank, cnt, tile, n_tiles):
    n_tile_grp = (cnt + tile - 1) // tile
    tile_end = jnp.cumsum(n_tile_grp)
    tile_start = tile_end - n_tile_grp
    total = tile_end[-1]
    pos = ((tile_start * tile)[grp] + rank).astype(jnp.int32)
    fill_lo = jnp.concatenate([tile_start * tile + cnt, total[None] * tile]).astype(jnp.int32)
    fill_hi = jnp.concatenate([tile_end * tile, jnp.full((1,), n_tiles * tile)]).astype(jnp.int32)
    src = _invert(pos, fill_lo, fill_hi, n_tiles * tile)
    i = jnp.arange(n_tiles, dtype=jnp.int32)
    tile_grp = jnp.sum((jnp.minimum(i, total - 1)[:, None] >= tile_end[None, :]).astype(jnp.int32),
                       axis=1)
    return pos, src, (tile_grp.astype(jnp.int32), (i < total).astype(jnp.int32))


def _sparse_moe(x1, h2tm, grp, rank, cnt8, mod3, mod_row_fn, wts, w_gate, w_up, w_down, tile):
    n = x1.shape[0]
    n_tiles = n // tile + N_GROUPS
    pos, src, maps = _sort_plan(grp.reshape(n), rank.reshape(n), cnt8[:N_GROUPS, 0], tile, n_tiles)
    y_sorted = _moe(h2tm, src, maps, wts, w_gate, w_up, w_down, tile)
    return _combine(y_sorted, pos, x1, mod3, mod_row_fn)


def _pack_weights(l, w_in, norm1_g, q_a_norm_g, w_q_b, kv_a_norm_g, w_kv_b, q_norm_g, k_norm_g,
                  w_attn_o, w_fnet, w_out, norm2_g, w_router_group, b_router_group,
                  w_router_expert, b_router_expert):
    wi = w_in[l]
    kpe_cols = jnp.pad(wi[:, C_KPE:C_KPE + QK_ROPE], ((0, 0), (QK_NOPE, LANES - QK_HEAD)))
    w_in_p = jnp.concatenate([wi[:, :C_KPE], kpe_cols, wi[:, C_KPE + QK_ROPE:]], axis=1)
    w_qb = jnp.pad(w_q_b[l].reshape(Q_LORA, N_HEADS, QK_HEAD),
                   ((0, 0), (0, 0), (0, HEAD_SLOT - QK_HEAD))).reshape(Q_LORA, QK_WIDTH)
    wkv = w_kv_b[l].reshape(KV_LORA, N_HEADS, QK_NOPE + V_HEAD)
    w_kb = jnp.pad(wkv[:, :, :QK_NOPE],
                   ((0, 0), (0, 0), (0, HEAD_SLOT - QK_NOPE))).reshape(KV_LORA, QK_WIDTH)
    w_vb = wkv[:, :, QK_NOPE:].reshape(KV_LORA, V_WIDTH)
    pad_g = lambda g: jnp.pad(g, (0, HEAD_SLOT - QK_HEAD)).reshape(1, HEAD_SLOT)
    w_rg = jnp.pad(w_router_group[l].T, ((0, ROUTER_ROWS - N_GROUPS), (0, 0)))
    b_rg = jnp.pad(b_router_group[l], (0, ROUTER_ROWS - N_GROUPS)).reshape(ROUTER_ROWS, 1)
    w_re = jnp.pad(w_router_expert[l].T.reshape(N_GROUPS, EXPERTS_PER_GROUP, D_MODEL),
                   ((0, 0), (0, ROUTER_ROWS - EXPERTS_PER_GROUP), (0, 0)))
    b_re = jnp.pad(b_router_expert[l].reshape(N_GROUPS, EXPERTS_PER_GROUP),
                   ((0, 0), (0, ROUTER_ROWS - EXPERTS_PER_GROUP))).reshape(N_GROUPS, ROUTER_ROWS, 1)
    dft_c, dft_ns = _dft_tables(FN_GROUP_W)
    return {
        "g1": norm1_g[l].reshape(1, D_MODEL),
        "w_in": w_in_p.astype(BF16),
        "qag": q_a_norm_g[l].reshape(1, Q_LORA),
        "w_qb": w_qb.astype(BF16),
        "kvg": kv_a_norm_g[l].reshape(1, KV_LORA),
        "w_kb": w_kb.astype(BF16),
        "w_vb": w_vb.T.astype(BF16),
        "qg": pad_g(q_norm_g[l]),
        "kg": pad_g(k_norm_g[l]),
        "dft_c": jnp.concatenate([jnp.asarray(dft_c), -jnp.asarray(dft_ns)], axis=1).astype(BF16),
        "w_ao": w_attn_o[l].astype(BF16),
        "w_fn": w_fnet[l].astype(BF16),
        "w_out": w_out[l].astype(BF16),
        "g2": norm2_g[l].reshape(1, D_MODEL),
        "w_rg": w_rg,
        "b_rg": b_rg,
        "w_re": w_re,
        "b_re": b_re,
    }


def _layer(xp, xs, cache_ckv_l, cache_kpe_l, mod3, wts, w_gate, w_up, w_down):
    bp, sp, _ = xp.shape
    bs, ss, _ = xs.shape
    past = cache_ckv_l.shape[1]
    ctx_row = lambda i: 0
    lat_row = lambda i: 1 + i // (ss // TM)

    xp2 = xp.reshape(bp * sp, D_MODEL)
    q, k, v, fcs, sga, sgf, ckv, kpe = _inproj(xp2, mod3, ctx_row, wts, None, True)
    attn = _attention(q.reshape(bp, sp, QK_WIDTH), [k.reshape(bp, sp, QK_WIDTH)],
                      [v], sp, "attn_ctx")
    cs, ns = (jnp.asarray(t).astype(BF16) for t in _dft_tables(sp))
    fm = _fourier(fcs.reshape(bp, sp, 2 * FN_WIDTH), cs, ns, sp, SEQ_PER_STEP_CTX, "fourier_ctx")
    x1, h2tm, grp, rank, cnt = _merge(xp2, attn.reshape(bp * sp, V_WIDTH),
                                      fm.reshape(bp * sp, FN_WIDTH), sga, sgf, mod3, ctx_row, wts)
    yp = _sparse_moe(x1, h2tm, grp, rank, cnt, mod3, ctx_row, wts, w_gate, w_up, w_down,
                     MOE_TILE).reshape(bp, sp, D_MODEL)

    xs2 = xs.reshape(bs * ss, D_MODEL)
    rope_tabs = tuple(jnp.asarray(t) for t in _rope_tables(ss))
    q, k, v, fcs, sga, sgf = _inproj(xs2, mod3, lat_row, wts, rope_tabs, False)
    kpe_slot = jnp.pad(cache_kpe_l, ((0, 0), (0, 0), (QK_NOPE, LANES - QK_HEAD)))
    kc, vc = _cache_kv(cache_ckv_l.reshape(bs * past, KV_LORA), kpe_slot.reshape(bs * past, LANES), wts)
    attn = _attention(q.reshape(bs, ss, QK_WIDTH),
                      [kc.reshape(bs, past, QK_WIDTH), k.reshape(bs, ss, QK_WIDTH)],
                      [vc, v], TQ_LAT, "attn_lat")
    cs, ns = (jnp.asarray(t).astype(BF16) for t in _dft_tables(ss))
    fm = _fourier(fcs.reshape(bs, ss, 2 * FN_WIDTH), cs, ns, TQ_LAT, 1, "fourier_lat")
    x1, h2tm, grp, rank, cnt = _merge(xs2, attn.reshape(bs * ss, V_WIDTH),
                                      fm.reshape(bs * ss, FN_WIDTH), sga, sgf, mod3, lat_row, wts)
    ys = _sparse_moe(x1, h2tm, grp, rank, cnt, mod3, lat_row, wts, w_gate, w_up, w_down,
                     MOE_TILE).reshape(bs, ss, D_MODEL)

    return yp, ys, ckv.reshape(bp, sp, KV_LORA), kpe.reshape(bp, sp, QK_ROPE)


def kernel(x_prompt, x_sample, cache_ckv, cache_kpe, c, c_ctx, w_mod, b_mod, norm1_g, w_in, q_a_norm_g, w_q_b, kv_a_norm_g, w_kv_b, q_norm_g, k_norm_g, w_attn_o, w_fnet, w_out, norm2_g, w_router_group, b_router_group, w_router_expert, b_router_expert, w_exp_gate, w_exp_up, w_exp_down):
    depth = w_mod.shape[0]
    n_lat = c.shape[0]
    assert 1 + n_lat <= MOD_ROWS
    cond8 = jnp.concatenate([c_ctx[None, :], c, jnp.zeros((MOD_ROWS - 1 - n_lat, D_MODEL), F32)], axis=0)
    xp, xs = x_prompt, x_sample
    ckv_layers, kpe_layers = [], []
    for l in range(depth):
        mod3 = _adaln(cond8, w_mod[l], b_mod[l].reshape(1, -1)).reshape(MOD_ROWS, 6, D_MODEL)
        wts = _pack_weights(l, w_in, norm1_g, q_a_norm_g, w_q_b, kv_a_norm_g, w_kv_b, q_norm_g,
                            k_norm_g, w_attn_o, w_fnet, w_out, norm2_g, w_router_group,
                            b_router_group, w_router_expert, b_router_expert)
        xp, xs, ckv, kpe = _layer(xp, xs, cache_ckv[:, l], cache_kpe[:, l], mod3, wts,
                                  w_exp_gate[l].astype(BF16), w_exp_up[l].astype(BF16),
                                  w_exp_down[l].astype(BF16))
        ckv_layers.append(ckv)
        kpe_layers.append(kpe)
    return xp, xs, jnp.stack(ckv_layers, axis=1), jnp.stack(kpe_layers, axis=1)
```

```python
import functools
import math

import numpy as np
import jax
import jax.numpy as jnp
from jax import lax
from jax.experimental import pallas as pl
from jax.experimental.pallas import tpu as pltpu

D_MODEL = 1024
GRID_W = 64
N_HEADS = 8
Q_LORA = 512
KV_LORA = 256
QK_NOPE = 64
QK_ROPE = 32
V_HEAD = 64
QK_HEAD = QK_NOPE + QK_ROPE
ATTN_SCALE = QK_HEAD ** -0.5
ROPE_BASE = 10000.0
FN_GROUPS = 4
FN_GROUP_W = 128
FN_WIDTH = FN_GROUPS * FN_GROUP_W
N_GROUPS = 4
EXPERTS_PER_GROUP = 4
N_EXPERTS = N_GROUPS * EXPERTS_PER_GROUP
D_EXPERT = 512
EPS = 1e-6

LANES = 128
HEAD_SLOT = LANES
QK_WIDTH = N_HEADS * HEAD_SLOT
V_WIDTH = N_HEADS * V_HEAD
C_QA = 0
C_KVA = C_QA + Q_LORA
C_KPE = C_KVA + KV_LORA
C_FN = C_KPE + LANES
C_GA = C_FN + FN_WIDTH
C_GF = C_GA + D_MODEL
IN_PACKED = C_GF + D_MODEL
SUBLANES = 8
ROUTER_ROWS = SUBLANES
MOD_ROWS = SUBLANES
VMEM_LIMIT = 56 * 1024 * 1024

TM = 512
TQ_LAT = 256
SEQ_PER_STEP_CTX = 4
SEG_ROWS = SUBLANES
TOKEN_ROWS = D_MODEL // LANES
MOE_TILE = 256

BF16 = jnp.bfloat16
F32 = jnp.float32


def _cparams(sem):
    return pltpu.CompilerParams(dimension_semantics=sem, vmem_limit_bytes=VMEM_LIMIT)


def _dot(a, b):
    return jnp.dot(a, b, preferred_element_type=F32)


def _dot_nt(a, b):
    return lax.dot_general(a, b, (((1,), (1,)), ((), ())), preferred_element_type=F32)


def _sigmoid(x):
    return 1.0 / (1.0 + jnp.exp(-x))


def _split_bf16(x):
    hi = x.astype(BF16)
    return hi, (x - hi.astype(F32)).astype(BF16)


@functools.lru_cache(maxsize=None)
def _rope_tables(n_pos):
    half = QK_ROPE // 2
    quarter = half // 2
    freqs = ROPE_BASE ** (-np.arange(quarter, dtype=np.float64) / quarter)
    pos = np.arange(n_pos)
    row = (pos // GRID_W).astype(np.float64)
    col = (pos % GRID_W).astype(np.float64)
    cos_t = np.ones((n_pos, LANES), np.float64)
    sin_a = np.zeros((n_pos, LANES), np.float64)
    sin_b = np.zeros((n_pos, LANES), np.float64)
    for base, p in ((QK_NOPE, row), (QK_NOPE + half, col)):
        ang = p[:, None] * freqs[None, :]
        cos_t[:, base:base + quarter] = np.cos(ang)
        cos_t[:, base + quarter:base + half] = np.cos(ang)
        sin_a[:, base:base + quarter] = -np.sin(ang)
        sin_b[:, base + quarter:base + half] = np.sin(ang)
    return (cos_t.astype(np.float32), sin_a.astype(np.float32), sin_b.astype(np.float32))


@functools.lru_cache(maxsize=None)
def _dft_tables(n):
    k = np.arange(n)
    ang = 2.0 * np.pi * ((k[:, None] * k[None, :]) % n) / n
    s = 1.0 / math.sqrt(n)
    return (np.cos(ang) * s).astype(np.float32), (-np.sin(ang) * s).astype(np.float32)


def _adaln_kernel(cond_ref, w_ref, b_ref, o_ref):
    c = cond_ref[...]
    s_hi, s_lo = _split_bf16(c * _sigmoid(c))
    w_hi, w_lo = _split_bf16(w_ref[...])
    y = _dot(jnp.concatenate([s_hi, s_lo], axis=0), w_hi)
    o_ref[...] = (y[:MOD_ROWS] + y[MOD_ROWS:]) + _dot(s_hi, w_lo) + b_ref[...]


def _adaln(cond8, w_mod, b_mod):
    n = w_mod.shape[1]
    tn = D_MODEL
    return pl.pallas_call(
        _adaln_kernel,
        grid=(n // tn,),
        in_specs=[pl.BlockSpec((MOD_ROWS, D_MODEL), lambda j: (0, 0)),
                  pl.BlockSpec((D_MODEL, tn), lambda j: (0, j)),
                  pl.BlockSpec((1, tn), lambda j: (0, j))],
        out_specs=pl.BlockSpec((MOD_ROWS, tn), lambda j: (0, j)),
        out_shape=jax.ShapeDtypeStruct((MOD_ROWS, n), F32),
        compiler_params=_cparams(("arbitrary",)),
        name="adaln",
    )(cond8, w_mod, b_mod)


def _rms(x, width):
    return lax.rsqrt(jnp.sum(x * x, axis=-1, keepdims=True) * (1.0 / width) + EPS)


def _rope(x, cos_t, sin_a, sin_b):
    return x * cos_t + pltpu.roll(x, LANES - 8, 1) * sin_a + pltpu.roll(x, 8, 1) * sin_b


def _inproj_kernel(*refs, rope, emit_cache, n_cast):
    it = iter(refs)
    x_ref, mod_ref, g1_ref, win_ref, qag_ref, wqb_ref, kvg_ref, wkb_ref, wvb_ref = (
        next(it) for _ in range(9))
    qg_ref, kg_ref, dft_ref = next(it), next(it), next(it)
    if rope:
        cos_ref, sa_ref, sb_ref = next(it), next(it), next(it)
    cast_in = [next(it) for _ in range(n_cast)]
    q_ref, k_ref, v_ref, fcs_ref, sga_ref, sgf_ref = (next(it) for _ in range(6))
    if emit_cache:
        ckv_ref, kpe_ref = next(it), next(it)
    cast_out = [next(it) for _ in range(n_cast)]

    for src, dst in zip(cast_in, cast_out):
        dst[...] = src[...].astype(BF16)

    x = x_ref[...]
    shift = mod_ref[0, 0:1, :]
    scale = mod_ref[0, 1:2, :]
    h = (x * _rms(x, D_MODEL) * g1_ref[...]) * (1.0 + scale) + shift
    hb = h.astype(BF16)

    if rope:
        cos_t, sin_a, sin_b = cos_ref[...], sa_ref[...], sb_ref[...]

    qa = _dot(hb, win_ref[:, C_QA:C_QA + Q_LORA])
    qn = (qa * _rms(qa, Q_LORA) * qag_ref[...]).astype(BF16)
    q = _dot(qn, wqb_ref[...])
    qg = qg_ref[...] * ATTN_SCALE
    for hd in range(N_HEADS):
        qh = q[:, hd * HEAD_SLOT:(hd + 1) * HEAD_SLOT]
        qh = qh * _rms(qh, QK_HEAD) * qg
        if rope:
            qh = _rope(qh, cos_t, sin_a, sin_b)
        q_ref[:, hd * HEAD_SLOT:(hd + 1) * HEAD_SLOT] = qh.astype(BF16)

    kva = _dot(hb, win_ref[:, C_KVA:C_KVA + KV_LORA])
    ckv = kva * _rms(kva, KV_LORA) * kvg_ref[...]
    kpe = _dot(hb, win_ref[:, C_KPE:C_KPE + LANES])
    if emit_cache:
        ckv_ref[...] = ckv
        kpe_ref[...] = kpe[:, QK_NOPE:QK_NOPE + QK_ROPE]
    _emit_kv(ckv.astype(BF16), kpe, wkb_ref, wvb_ref, kg_ref,
             (cos_t, sin_a, sin_b) if rope else None, k_ref, v_ref)

    fn = _dot(hb, win_ref[:, C_FN:C_FN + FN_WIDTH]).astype(BF16)
    for g in range(FN_GROUPS):
        cs = _dot(fn[:, g * FN_GROUP_W:(g + 1) * FN_GROUP_W], dft_ref[...])
        fcs_ref[:, g * FN_GROUP_W:(g + 1) * FN_GROUP_W] = cs[:, :FN_GROUP_W].astype(BF16)
        fcs_ref[:, FN_WIDTH + g * FN_GROUP_W:FN_WIDTH + (g + 1) * FN_GROUP_W] = (
            cs[:, FN_GROUP_W:].astype(BF16))

    sga_ref[...] = _sigmoid(_dot(hb, win_ref[:, C_GA:C_GA + D_MODEL])).astype(BF16)
    sgf_ref[...] = _sigmoid(_dot(hb, win_ref[:, C_GF:C_GF + D_MODEL])).astype(BF16)


def _emit_kv(ckvb, kpe, wkb_ref, wvb_ref, kg_ref, rope_tabs, k_ref, v_ref):
    kg = kg_ref[...]
    v_ref[...] = _dot_nt(wvb_ref[...], ckvb).astype(BF16)
    kn = _dot(ckvb, wkb_ref[...])
    pe_ss = jnp.sum(kpe * kpe, axis=-1, keepdims=True)
    pe_g = kpe * kg
    if rope_tabs is not None:
        pe_g = _rope(pe_g, *rope_tabs)
    for hd in range(N_HEADS):
        knh = kn[:, hd * HEAD_SLOT:(hd + 1) * HEAD_SLOT]
        ss = jnp.sum(knh * knh, axis=-1, keepdims=True) + pe_ss
        r = lax.rsqrt(ss * (1.0 / QK_HEAD) + EPS)
        k_ref[:, hd * HEAD_SLOT:(hd + 1) * HEAD_SLOT] = ((knh * kg + pe_g) * r).astype(BF16)


def _const_spec(shape):
    return pl.BlockSpec(shape, lambda i: (0,) * len(shape))


def _inproj(x2d, mod3, mod_row_fn, wts, rope_tabs, emit_cache, cast=()):
    n = x2d.shape[0]
    assert all(w.shape[0] == n // TM for w in cast)
    rope = rope_tabs is not None
    tiles_per_seq = None if not rope else rope_tabs[0].shape[0] // TM
    in_specs = [pl.BlockSpec((TM, D_MODEL), lambda i: (i, 0)),
                pl.BlockSpec((1, 6, D_MODEL), lambda i: (mod_row_fn(i), 0, 0)),
                _const_spec((1, D_MODEL)),
                _const_spec((D_MODEL, IN_PACKED)),
                _const_spec((1, Q_LORA)),
                _const_spec((Q_LORA, QK_WIDTH)),
                _const_spec((1, KV_LORA)),
                _const_spec((KV_LORA, QK_WIDTH)),
                _const_spec((V_WIDTH, KV_LORA)),
                _const_spec((1, HEAD_SLOT)),
                _const_spec((1, HEAD_SLOT)),
                _const_spec((FN_GROUP_W, 2 * FN_GROUP_W))]
    args = [x2d, mod3, wts["g1"], wts["w_in"], wts["qag"], wts["w_qb"], wts["kvg"],
            wts["w_kb"], wts["w_vb"], wts["qg"], wts["kg"], wts["dft_c"]]
    if rope:
        in_specs += [pl.BlockSpec((TM, LANES), lambda i: (i % tiles_per_seq, 0))] * 3
        args += list(rope_tabs)
    cast_specs = [pl.BlockSpec((1,) + w.shape[1:], lambda i: (i, 0, 0)) for w in cast]
    in_specs += cast_specs
    args += list(cast)
    out_shape = [jax.ShapeDtypeStruct((n, QK_WIDTH), BF16),
                 jax.ShapeDtypeStruct((n, QK_WIDTH), BF16),
                 jax.ShapeDtypeStruct((V_WIDTH, n), BF16),
                 jax.ShapeDtypeStruct((n, 2 * FN_WIDTH), BF16),
                 jax.ShapeDtypeStruct((n, D_MODEL), BF16),
                 jax.ShapeDtypeStruct((n, D_MODEL), BF16)]
    out_specs = [pl.BlockSpec((TM, s.shape[1]), lambda i: (i, 0)) for s in out_shape]
    out_specs[2] = pl.BlockSpec((V_WIDTH, TM), lambda i: (0, i))
    if emit_cache:
        out_shape += [jax.ShapeDtypeStruct((n, KV_LORA), F32),
                      jax.ShapeDtypeStruct((n, QK_ROPE), F32)]
        out_specs += [pl.BlockSpec((TM, KV_LORA), lambda i: (i, 0)),
                      pl.BlockSpec((TM, QK_ROPE), lambda i: (i, 0))]
    out_shape += [jax.ShapeDtypeStruct(w.shape, BF16) for w in cast]
    out_specs += cast_specs
    return pl.pallas_call(
        functools.partial(_inproj_kernel, rope=rope, emit_cache=emit_cache, n_cast=len(cast)),
        grid=(n // TM,),
        in_specs=in_specs,
        out_specs=out_specs,
        out_shape=out_shape,
        compiler_params=_cparams(("parallel",)),
        name="inproj_lat" if rope else "inproj_ctx",
    )(*args)


def _cache_kv_kernel(ckv_ref, kpe_ref, wkb_ref, wvb_ref, kg_ref, k_ref, v_ref):
    _emit_kv(ckv_ref[...].astype(BF16), kpe_ref[...], wkb_ref, wvb_ref, kg_ref, None, k_ref, v_ref)


def _cache_kv(ckv2d, kpe_slot2d, wts):
    n = ckv2d.shape[0]
    return pl.pallas_call(
        _cache_kv_kernel,
        grid=(n // TM,),
        in_specs=[pl.BlockSpec((TM, KV_LORA), lambda i: (i, 0)),
                  pl.BlockSpec((TM, LANES), lambda i: (i, 0)),
                  _const_spec((KV_LORA, QK_WIDTH)),
                  _const_spec((V_WIDTH, KV_LORA)),
                  _const_spec((1, HEAD_SLOT))],
        out_specs=[pl.BlockSpec((TM, QK_WIDTH), lambda i: (i, 0)),
                   pl.BlockSpec((V_WIDTH, TM), lambda i: (0, i))],
        out_shape=[jax.ShapeDtypeStruct((n, QK_WIDTH), BF16),
                   jax.ShapeDtypeStruct((V_WIDTH, n), BF16)],
        compiler_params=_cparams(("parallel",)),
        name="cache_kv",
    )(ckv2d, kpe_slot2d, wts["w_kb"], wts["w_vb"], wts["kg"])


def _attn_kernel(*refs, n_kv):
    q_ref = refs[0]
    k_refs = refs[1:1 + n_kv]
    vt_refs = refs[1 + n_kv:1 + 2 * n_kv]
    o_ref = refs[1 + 2 * n_kv]
    head = lambda hd: slice(hd * HEAD_SLOT, (hd + 1) * HEAD_SLOT)
    st = [jnp.stack([_dot_nt(k[0, :, head(hd)], q_ref[0, :, head(hd)]) for hd in range(N_HEADS)])
          for k in k_refs]
    m = functools.reduce(jnp.maximum, [sj.max(axis=1, keepdims=True) for sj in st])
    p = [jnp.exp(sj - m) for sj in st]
    l = functools.reduce(lambda a, b: a + b, [pj.sum(axis=1, keepdims=True) for pj in p])
    outs = []
    for hd in range(N_HEADS):
        o = functools.reduce(lambda a, b: a + b,
                             [_dot(vt[hd * V_HEAD:(hd + 1) * V_HEAD, :], pj[hd].astype(BF16))
                              for vt, pj in zip(vt_refs, p)])
        outs.append(o / l[hd])
    o_ref[0] = jnp.concatenate(outs, axis=0).T.astype(BF16)


def _attention(q3, ks, vts, tq, name):
    b, sq, _ = q3.shape
    n_kv = len(ks)
    in_specs = [pl.BlockSpec((1, tq, QK_WIDTH), lambda bi, qi: (bi, qi, 0))]
    in_specs += [pl.BlockSpec((1, k.shape[1], QK_WIDTH), lambda bi, qi: (bi, 0, 0)) for k in ks]
    in_specs += [pl.BlockSpec((V_WIDTH, k.shape[1]), lambda bi, qi: (0, bi)) for k in ks]
    return pl.pallas_call(
        functools.partial(_attn_kernel, n_kv=n_kv),
        grid=(b, sq // tq),
        in_specs=in_specs,
        out_specs=pl.BlockSpec((1, tq, V_WIDTH), lambda bi, qi: (bi, qi, 0)),
        out_shape=jax.ShapeDtypeStruct((b, sq, V_WIDTH), BF16),
        compiler_params=_cparams(("parallel", "parallel")),
        name=name,
    )(q3, *ks, *vts)


def _fourier_kernel(cs_ref, ns_ref, f_ref, o_ref, *, n_b):
    for b in range(n_b):
        xc = f_ref[b, :, :FN_WIDTH]
        xs = f_ref[b, :, FN_WIDTH:]
        o_ref[b] = (_dot(cs_ref[...], xc) + _dot(ns_ref[...], xs)).astype(BF16)


def _fourier(fcs3, cs, ns, tr, n_b, name):
    b, s, _ = fcs3.shape
    return pl.pallas_call(
        functools.partial(_fourier_kernel, n_b=n_b),
        grid=(b // n_b, s // tr),
        in_specs=[pl.BlockSpec((tr, s), lambda bi, ri: (ri, 0)),
                  pl.BlockSpec((tr, s), lambda bi, ri: (ri, 0)),
                  pl.BlockSpec((n_b, s, 2 * FN_WIDTH), lambda bi, ri: (bi, 0, 0))],
        out_specs=pl.BlockSpec((n_b, tr, FN_WIDTH), lambda bi, ri: (bi, ri, 0)),
        out_shape=jax.ShapeDtypeStruct((b, s, FN_WIDTH), BF16),
        compiler_params=_cparams(("parallel", "parallel")),
        name=name,
    )(cs, ns, fcs3)


def _router_logits(w, b, h_hi, h_lo):
    rows = w.shape[0]
    w_hi, w_lo = _split_bf16(w)
    y = _dot_nt(jnp.concatenate([w_hi, w_lo], axis=0), h_hi)
    return (y[:rows] + y[rows:]) + _dot_nt(w_hi, h_lo) + b


def _rows(x, n):
    return [x[j:j + 1, :] for j in range(n)]


def _first_argmax(rows, top):
    idx = jnp.full(top.shape, len(rows) - 1, jnp.int32)
    for j in range(len(rows) - 2, -1, -1):
        idx = jnp.where(rows[j] == top, j, idx)
    return idx


def _store_token_major(ref, x, tm):
    for s in range(TOKEN_ROWS):
        ref[pl.ds(s, tm, stride=TOKEN_ROWS), :] = x[:, s * LANES:(s + 1) * LANES]


def _load_token_major(ref, tm):
    return jnp.concatenate([ref[pl.ds(s, tm, stride=TOKEN_ROWS), :] for s in range(TOKEN_ROWS)],
                           axis=1)


def _merge_kernel(x_ref, attn_ref, fm_ref, sga_ref, sgf_ref, mod_ref, wao_ref, wfn_ref, wout_ref,
                  g2_ref, wrg_ref, brg_ref, x1_ref, h2_ref, grp_ref, rank_ref, cnt_ref, carry_ref):
    i = pl.program_id(0)

    @pl.when(i == 0)
    def _():
        carry_ref[...] = jnp.zeros_like(carry_ref)

    a = _dot(attn_ref[...], wao_ref[...])
    f = _dot(fm_ref[...], wfn_ref[...])
    u = sga_ref[...].astype(F32) * a + sgf_ref[...].astype(F32) * f
    y = _dot(u.astype(BF16), wout_ref[...])
    x1 = x_ref[...] + mod_ref[0, 2:3, :] * y
    x1_ref[...] = x1
    tm = x1.shape[0]
    h2 = (x1 * _rms(x1, D_MODEL) * g2_ref[...]) * (1.0 + mod_ref[0, 4:5, :]) + mod_ref[0, 3:4, :]
    _store_token_major(h2_ref, h2, tm)

    h2_hi, h2_lo = _split_bf16(h2)
    g = _rows(_router_logits(wrg_ref[...], brg_ref[...], h2_hi, h2_lo), N_GROUPS)
    gidx = _first_argmax(g, functools.reduce(jnp.maximum, g))

    onehot = jnp.where(lax.broadcasted_iota(jnp.int32, (SEG_ROWS, tm), 0) == gidx, 1.0, 0.0)
    before = (lax.broadcasted_iota(jnp.int32, (tm, tm), 0)
              < lax.broadcasted_iota(jnp.int32, (tm, tm), 1))
    prefix = _dot(onehot.astype(BF16), jnp.where(before, 1.0, 0.0).astype(BF16))
    carry = carry_ref[...]
    rank = jnp.sum(onehot * (prefix + carry[:, 0:1]), axis=0, keepdims=True)
    grp_ref[...] = gidx
    rank_ref[...] = rank.astype(jnp.int32)
    carry = carry + jnp.sum(onehot, axis=1, keepdims=True)
    carry_ref[...] = carry
    cnt_ref[...] = carry.astype(jnp.int32)


def _merge(x2d, attn2d, fm2d, sga, sgf, mod3, mod_row_fn, wts):
    n = x2d.shape[0]
    tok = lambda w: pl.BlockSpec((TM, w), lambda i: (i, 0))
    return pl.pallas_call(
        _merge_kernel,
        grid=(n // TM,),
        in_specs=[tok(D_MODEL), tok(V_WIDTH), tok(FN_WIDTH), tok(D_MODEL), tok(D_MODEL),
                  pl.BlockSpec((1, 6, D_MODEL), lambda i: (mod_row_fn(i), 0, 0)),
                  _const_spec((V_WIDTH, D_MODEL)),
                  _const_spec((FN_WIDTH, D_MODEL)),
                  _const_spec((D_MODEL, D_MODEL)),
                  _const_spec((1, D_MODEL)),
                  _const_spec((ROUTER_ROWS, D_MODEL)),
                  _const_spec((ROUTER_ROWS, 1))],
        out_specs=[tok(D_MODEL),
                   pl.BlockSpec((TM * TOKEN_ROWS, LANES), lambda i: (i, 0)),
                   pl.BlockSpec((1, TM), lambda i: (0, i)),
                   pl.BlockSpec((1, TM), lambda i: (0, i)), _const_spec((SEG_ROWS, LANES))],
        out_shape=[jax.ShapeDtypeStruct((n, D_MODEL), F32),
                   jax.ShapeDtypeStruct((n * TOKEN_ROWS, LANES), F32),
                   jax.ShapeDtypeStruct((1, n), jnp.int32),
                   jax.ShapeDtypeStruct((1, n), jnp.int32),
                   jax.ShapeDtypeStruct((SEG_ROWS, LANES), jnp.int32)],
        scratch_shapes=[pltpu.VMEM((SEG_ROWS, LANES), F32)],
        compiler_params=_cparams(("arbitrary",)),
        name="merge",
    )(x2d, attn2d, fm2d, sga, sgf, mod3, wts["w_ao"], wts["w_fn"], wts["w_out"], wts["g2"],
      wts["w_rg"], wts["b_rg"])


def _token_rows(ref, t):
    start = t * TOKEN_ROWS
    if not isinstance(t, int):
        start = pl.multiple_of(start, TOKEN_ROWS)
    return ref.at[pl.ds(start, TOKEN_ROWS)]


def _row_copy(src, dst, src_tok, dst_tok, sem):
    return pltpu.make_async_copy(_token_rows(src, src_tok), _token_rows(dst, dst_tok), sem)


def _combine_kernel(pos_ref, y_hbm, x1_ref, mod_ref, o_ref, buf, sem):
    i = pl.program_id(0)
    tm = o_ref.shape[0]

    def gather(tile_idx, slot):
        def start(r, c):
            _row_copy(y_hbm, buf.at[slot], pos_ref[tile_idx * tm + r], r, sem.at[slot]).start()
            return c

        lax.fori_loop(0, tm, start, 0, unroll=32)

    @pl.when(i == 0)
    def _():
        gather(0, 0)

    @pl.when(i + 1 < pl.num_programs(0))
    def _():
        gather(i + 1, (i + 1) % 2)

    slot = i % 2

    def wait(r, c):
        _row_copy(y_hbm, buf.at[slot], 0, 0, sem.at[slot]).wait()
        return c

    lax.fori_loop(0, tm, wait, 0, unroll=8)
    o_ref[...] = x1_ref[...] + mod_ref[0, 5:6, :] * _load_token_major(buf.at[slot], tm)


def _combine(y_sorted, pos, x1, mod3, mod_row_fn):
    n = x1.shape[0]
    return pl.pallas_call(
        _combine_kernel,
        grid_spec=pltpu.PrefetchScalarGridSpec(
            num_scalar_prefetch=1,
            grid=(n // TM,),
            in_specs=[pl.BlockSpec(memory_space=pl.ANY),
                      pl.BlockSpec((TM, D_MODEL), lambda i, pos: (i, 0)),
                      pl.BlockSpec((1, 6, D_MODEL), lambda i, pos: (mod_row_fn(i), 0, 0))],
            out_specs=pl.BlockSpec((TM, D_MODEL), lambda i, pos: (i, 0)),
            scratch_shapes=[pltpu.VMEM((2, TM * TOKEN_ROWS, LANES), F32),
                            pltpu.SemaphoreType.DMA((2,))]),
        out_shape=jax.ShapeDtypeStruct((n, D_MODEL), F32),
        compiler_params=_cparams(("arbitrary",)),
        name="moe_combine",
    )(pos, y_sorted, x1, mod3)


def _moe_kernel(grp_ref, on_ref, src_ref, h_hbm, wrg_ref, brg_ref, wre_ref, bre_ref, wg_ref, wu_ref,
                wd_ref, o_ref, hbuf, sem, *, tile):
    i = pl.program_id(0)

    def gather(tile_idx, slot):
        def start(r, c):
            _row_copy(h_hbm, hbuf.at[slot], src_ref[tile_idx * tile + r], r, sem.at[slot]).start()
            return c

        lax.fori_loop(0, tile, start, 0, unroll=32)

    @pl.when(jnp.logical_and(i == 0, on_ref[0] == 1))
    def _():
        gather(0, 0)

    nxt = jnp.minimum(i + 1, pl.num_programs(0) - 1)

    @pl.when(jnp.logical_and(i + 1 < pl.num_programs(0), on_ref[nxt] == 1))
    def _():
        gather(i + 1, (i + 1) % 2)

    @pl.when(on_ref[i] == 0)
    def _():
        o_ref[...] = jnp.zeros_like(o_ref)

    @pl.when(on_ref[i] == 1)
    def _():
        slot = i % 2

        def wait(r, c):
            _row_copy(h_hbm, hbuf.at[slot], 0, 0, sem.at[slot]).wait()
            return c

        lax.fori_loop(0, tile, wait, 0, unroll=8)
        h2_hi, h2_lo = _split_bf16(_load_token_major(hbuf.at[slot], tile))

        logits = _router_logits(jnp.concatenate([wrg_ref[...], wre_ref[0]], axis=0),
                                jnp.concatenate([brg_ref[...], bre_ref[0]], axis=0), h2_hi, h2_lo)
        g = _rows(logits, N_GROUPS)
        gmax = functools.reduce(jnp.maximum, g)
        p_top = 1.0 / functools.reduce(lambda p, q: p + q, [jnp.exp(gj - gmax) for gj in g])
        e = _rows(logits[ROUTER_ROWS:], EXPERTS_PER_GROUP)
        m1 = functools.reduce(jnp.maximum, e)
        i1 = _first_argmax(e, m1)
        rest = [jnp.where(i1 == j, -jnp.inf, e[j]) for j in range(EXPERTS_PER_GROUP)]
        m2 = functools.reduce(jnp.maximum, rest)
        i2 = _first_argmax(rest, m2)
        t = jnp.exp(m2 - m1)
        w1 = p_top / (1.0 + t)
        w2 = p_top * t / (1.0 + t)
        row = lax.broadcasted_iota(jnp.int32, (LANES, tile), 0)
        comb = (jnp.where(row == i1, w1, 0.0) + jnp.where(row == i2, w2, 0.0)).T

        acc = None
        for j in range(EXPERTS_PER_GROUP):
            a = _dot(h2_hi, wg_ref[j])
            u = _dot(h2_hi, wu_ref[j])
            act = (a * _sigmoid(a)) * u * comb[:, j:j + 1]
            y = _dot(act.astype(BF16), wd_ref[j])
            acc = y if acc is None else acc + y
        _store_token_major(o_ref, acc, tile)


def _moe(h2tm, src, maps, wts, w_gate, w_up, w_down, tile):
    n_tiles = src.shape[0] // tile
    const = lambda shape: pl.BlockSpec(shape, lambda i, grp, on, src: (0,) * len(shape))
    by_group = lambda shape: pl.BlockSpec(
        shape, lambda i, grp, on, src: (grp[i],) + (0,) * (len(shape) - 1))
    return pl.pallas_call(
        functools.partial(_moe_kernel, tile=tile),
        grid_spec=pltpu.PrefetchScalarGridSpec(
            num_scalar_prefetch=3,
            grid=(n_tiles,),
            in_specs=[pl.BlockSpec(memory_space=pl.ANY),
                      const((ROUTER_ROWS, D_MODEL)), const((ROUTER_ROWS, 1)),
                      by_group((1, ROUTER_ROWS, D_MODEL)), by_group((1, ROUTER_ROWS, 1)),
                      by_group((EXPERTS_PER_GROUP, D_MODEL, D_EXPERT)),
                      by_group((EXPERTS_PER_GROUP, D_MODEL, D_EXPERT)),
                      by_group((EXPERTS_PER_GROUP, D_EXPERT, D_MODEL))],
            out_specs=pl.BlockSpec((tile * TOKEN_ROWS, LANES), lambda i, grp, on, src: (i, 0)),
            scratch_shapes=[pltpu.VMEM((2, tile * TOKEN_ROWS, LANES), F32),
                            pltpu.SemaphoreType.DMA((2,))]),
        out_shape=jax.ShapeDtypeStruct((n_tiles * tile * TOKEN_ROWS, LANES), F32),
        compiler_params=_cparams(("arbitrary",)),
        name="moe",
    )(*maps, src, h2tm, wts["w_rg"], wts["b_rg"], wts["w_re"], wts["b_re"], w_gate, w_up, w_down)


def _invert_kernel(pos_ref, lo_ref, hi_ref, src_ref, *, n, n_fill):
    def zero(p, c):
        src_ref[p] = 0
        return c

    for s in range(n_fill):
        lax.fori_loop(lo_ref[s], hi_ref[s], zero, 0)

    def put(t, c):
        src_ref[pos_ref[t]] = t
        return c

    lax.fori_loop(0, n, put, 0, unroll=8)


def _invert(pos, fill_lo, fill_hi, n_sorted):
    return pl.pallas_call(
        functools.partial(_invert_kernel, n=pos.shape[0], n_fill=fill_lo.shape[0]),
        grid_spec=pltpu.PrefetchScalarGridSpec(
            num_scalar_prefetch=3, grid=(1,), in_specs=[],
            out_specs=pl.BlockSpec(memory_space=pltpu.SMEM)),
        out_shape=jax.ShapeDtypeStruct((n_sorted,), jnp.int32),
        name="moe_invert",
    )(pos, fill_lo, fill_hi)


def _sort_plan(grp, rank, cnt, tile, n_tiles):
    n_tile_grp = (cnt + tile - 1) // tile
    tile_end = jnp.cumsum(n_tile_grp)
    tile_start = tile_end - n_tile_grp
    total = tile_end[-1]
    pos = ((tile_start * tile)[grp] + rank).astype(jnp.int32)
    fill_lo = jnp.concatenate([tile_start * tile + cnt, total[None] * tile]).astype(jnp.int32)
    fill_hi = jnp.concatenate([tile_end * tile, jnp.full((1,), n_tiles * tile)]).astype(jnp.int32)
    src = _invert(pos, fill_lo, fill_hi, n_tiles * tile)
    i = jnp.arange(n_tiles, dtype=jnp.int32)
    tile_grp = jnp.sum((jnp.minimum(i, total - 1)[:, None] >= tile_end[None, :]).astype(jnp.int32),
                       axis=1)
    return pos, src, (tile_grp.astype(jnp.int32), (i < total).astype(jnp.int32))


def _sparse_moe(x1, h2tm, grp, rank, cnt8, mod3, mod_row_fn, wts, w_gate, w_up, w_down, tile):
    n = x1.shape[0]
    n_tiles = n // tile + N_GROUPS
    pos, src, maps = _sort_plan(grp.reshape(n), rank.reshape(n), cnt8[:N_GROUPS, 0], tile, n_tiles)
    y_sorted = _moe(h2tm, src, maps, wts, w_gate, w_up, w_down, tile)
    return _combine(y_sorted, pos, x1, mod3, mod_row_fn)


def _pack_weights(l, w_in, norm1_g, q_a_norm_g, w_q_b, kv_a_norm_g, w_kv_b, q_norm_g, k_norm_g,
                  w_attn_o, w_fnet, w_out, norm2_g, w_router_group, b_router_group,
                  w_router_expert, b_router_expert):
    wi = w_in[l]
    kpe_cols = jnp.pad(wi[:, C_KPE:C_KPE + QK_ROPE], ((0, 0), (QK_NOPE, LANES - QK_HEAD)))
    w_in_p = jnp.concatenate([wi[:, :C_KPE], kpe_cols, wi[:, C_KPE + QK_ROPE:]], axis=1)
    w_qb = jnp.pad(w_q_b[l].reshape(Q_LORA, N_HEADS, QK_HEAD),
                   ((0, 0), (0, 0), (0, HEAD_SLOT - QK_HEAD))).reshape(Q_LORA, QK_WIDTH)
    wkv = w_kv_b[l].reshape(KV_LORA, N_HEADS, QK_NOPE + V_HEAD)
    w_kb = jnp.pad(wkv[:, :, :QK_NOPE],
                   ((0, 0), (0, 0), (0, HEAD_SLOT - QK_NOPE))).reshape(KV_LORA, QK_WIDTH)
    w_vb = wkv[:, :, QK_NOPE:].reshape(KV_LORA, V_WIDTH)
    pad_g = lambda g: jnp.pad(g, (0, HEAD_SLOT - QK_HEAD)).reshape(1, HEAD_SLOT)
    w_rg = jnp.pad(w_router_group[l].T, ((0, ROUTER_ROWS - N_GROUPS), (0, 0)))
    b_rg = jnp.pad(b_router_group[l], (0, ROUTER_ROWS - N_GROUPS)).reshape(ROUTER_ROWS, 1)
    w_re = jnp.pad(w_router_expert[l].T.reshape(N_GROUPS, EXPERTS_PER_GROUP, D_MODEL),
                   ((0, 0), (0, ROUTER_ROWS - EXPERTS_PER_GROUP), (0, 0)))
    b_re = jnp.pad(b_router_expert[l].reshape(N_GROUPS, EXPERTS_PER_GROUP),
                   ((0, 0), (0, ROUTER_ROWS - EXPERTS_PER_GROUP))).reshape(N_GROUPS, ROUTER_ROWS, 1)
    dft_c, dft_ns = _dft_tables(FN_GROUP_W)
    return {
        "g1": norm1_g[l].reshape(1, D_MODEL),
        "w_in": w_in_p.astype(BF16),
        "qag": q_a_norm_g[l].reshape(1, Q_LORA),
        "w_qb": w_qb.astype(BF16),
        "kvg": kv_a_norm_g[l].reshape(1, KV_LORA),
        "w_kb": w_kb.astype(BF16),
        "w_vb": w_vb.T.astype(BF16),
        "qg": pad_g(q_norm_g[l]),
        "kg": pad_g(k_norm_g[l]),
        "dft_c": jnp.concatenate([jnp.asarray(dft_c), -jnp.asarray(dft_ns)], axis=1).astype(BF16),
        "w_ao": w_attn_o[l].astype(BF16),
        "w_fn": w_fnet[l].astype(BF16),
        "w_out": w_out[l].astype(BF16),
        "g2": norm2_g[l].reshape(1, D_MODEL),
        "w_rg": w_rg,
        "b_rg": b_rg,
        "w_re": w_re,
        "b_re": b_re,
    }


def _layer(xp, xs, cache_ckv_l, cache_kpe_l, mod3, wts, experts):
    bp, sp, _ = xp.shape
    bs, ss, _ = xs.shape
    past = cache_ckv_l.shape[1]
    ctx_row = lambda i: 0
    lat_row = lambda i: 1 + i // (ss // TM)

    xp2 = xp.reshape(bp * sp, D_MODEL)
    ride = (bp * sp) // TM == N_EXPERTS
    outs = _inproj(xp2, mod3, ctx_row, wts, None, True, experts if ride else ())
    q, k, v, fcs, sga, sgf, ckv, kpe = outs[:8]
    w_gate, w_up, w_down = outs[8:] if ride else (w.astype(BF16) for w in experts)
    attn = _attention(q.reshape(bp, sp, QK_WIDTH), [k.reshape(bp, sp, QK_WIDTH)],
                      [v], sp, "attn_ctx")
    cs, ns = (jnp.asarray(t).astype(BF16) for t in _dft_tables(sp))
    fm = _fourier(fcs.reshape(bp, sp, 2 * FN_WIDTH), cs, ns, sp, SEQ_PER_STEP_CTX, "fourier_ctx")
    x1, h2tm, grp, rank, cnt = _merge(xp2, attn.reshape(bp * sp, V_WIDTH),
                                      fm.reshape(bp * sp, FN_WIDTH), sga, sgf, mod3, ctx_row, wts)
    yp = _sparse_moe(x1, h2tm, grp, rank, cnt, mod3, ctx_row, wts, w_gate, w_up, w_down,
                     MOE_TILE).reshape(bp, sp, D_MODEL)

    xs2 = xs.reshape(bs * ss, D_MODEL)
    rope_tabs = tuple(jnp.asarray(t) for t in _rope_tables(ss))
    q, k, v, fcs, sga, sgf = _inproj(xs2, mod3, lat_row, wts, rope_tabs, False)
    kpe_slot = jnp.pad(cache_kpe_l, ((0, 0), (0, 0), (QK_NOPE, LANES - QK_HEAD)))
    kc, vc = _cache_kv(cache_ckv_l.reshape(bs * past, KV_LORA), kpe_slot.reshape(bs * past, LANES), wts)
    attn = _attention(q.reshape(bs, ss, QK_WIDTH),
                      [kc.reshape(bs, past, QK_WIDTH), k.reshape(bs, ss, QK_WIDTH)],
                      [vc, v], TQ_LAT, "attn_lat")
    cs, ns = (jnp.asarray(t).astype(BF16) for t in _dft_tables(ss))
    fm = _fourier(fcs.reshape(bs, ss, 2 * FN_WIDTH), cs, ns, TQ_LAT, 1, "fourier_lat")
    x1, h2tm, grp, rank, cnt = _merge(xs2, attn.reshape(bs * ss, V_WIDTH),
                                      fm.reshape(bs * ss, FN_WIDTH), sga, sgf, mod3, lat_row, wts)
    ys = _sparse_moe(x1, h2tm, grp, rank, cnt, mod3, lat_row, wts, w_gate, w_up, w_down,
                     MOE_TILE).reshape(bs, ss, D_MODEL)

    return yp, ys, ckv.reshape(bp, sp, KV_LORA), kpe.reshape(bp, sp, QK_ROPE)


def kernel(x_prompt, x_sample, cache_ckv, cache_kpe, c, c_ctx, w_mod, b_mod, norm1_g, w_in, q_a_norm_g, w_q_b, kv_a_norm_g, w_kv_b, q_norm_g, k_norm_g, w_attn_o, w_fnet, w_out, norm2_g, w_router_group, b_router_group, w_router_expert, b_router_expert, w_exp_gate, w_exp_up, w_exp_down):
    depth = w_mod.shape[0]
    n_lat = c.shape[0]
    assert 1 + n_lat <= MOD_ROWS
    cond8 = jnp.concatenate([c_ctx[None, :], c, jnp.zeros((MOD_ROWS - 1 - n_lat, D_MODEL), F32)], axis=0)
    xp, xs = x_prompt, x_sample
    ckv_layers, kpe_layers = [], []
    for l in range(depth):
        mod3 = _adaln(cond8, w_mod[l], b_mod[l].reshape(1, -1)).reshape(MOD_ROWS, 6, D_MODEL)
        wts = _pack_weights(l, w_in, norm1_g, q_a_norm_g, w_q_b, kv_a_norm_g, w_kv_b, q_norm_g,
                            k_norm_g, w_attn_o, w_fnet, w_out, norm2_g, w_router_group,
                            b_router_group, w_router_expert, b_router_expert)
        xp, xs, ckv, kpe = _layer(xp, xs, cache_ckv[:, l], cache_kpe[:, l], mod3, wts,
                                  (w_exp_gate[l], w_exp_up[l], w_exp_down[l]))
        ckv_layers.append(ckv)
        kpe_layers.append(kpe)
    return xp, xs, jnp.stack(ckv_layers, axis=1), jnp.stack(kpe_layers, axis=1)
```

```python
import functools
import math

import numpy as np
import jax
import jax.numpy as jnp
from jax import lax
from jax.experimental import pallas as pl
from jax.experimental.pallas import tpu as pltpu

D_MODEL = 1024
GRID_W = 64
N_HEADS = 8
Q_LORA = 512
KV_LORA = 256
QK_NOPE = 64
QK_ROPE = 32
V_HEAD = 64
QK_HEAD = QK_NOPE + QK_ROPE
ATTN_SCALE = QK_HEAD ** -0.5
ROPE_BASE = 10000.0
FN_GROUPS = 4
FN_GROUP_W = 128
FN_WIDTH = FN_GROUPS * FN_GROUP_W
N_GROUPS = 4
EXPERTS_PER_GROUP = 4
N_EXPERTS = N_GROUPS * EXPERTS_PER_GROUP
D_EXPERT = 512
EPS = 1e-6

LANES = 128
HEAD_SLOT = LANES
QK_WIDTH = N_HEADS * HEAD_SLOT
V_WIDTH = N_HEADS * V_HEAD
C_QA = 0
C_KVA = C_QA + Q_LORA
C_KPE = C_KVA + KV_LORA
C_FN = C_KPE + LANES
C_GA = C_FN + FN_WIDTH
C_GF = C_GA + D_MODEL
IN_PACKED = C_GF + D_MODEL
SUBLANES = 8
ROUTER_ROWS = SUBLANES
MOD_ROWS = SUBLANES
VMEM_LIMIT = 56 * 1024 * 1024

TM = 512
TQ_LAT = 256
SEQ_PER_STEP_CTX = 4
SEG_ROWS = SUBLANES
TOKEN_ROWS = D_MODEL // LANES
MOE_TILE = 256
ADALN_STEPS = 8

BF16 = jnp.bfloat16
F32 = jnp.float32


def _cparams(sem):
    return pltpu.CompilerParams(dimension_semantics=sem, vmem_limit_bytes=VMEM_LIMIT)


def _dot(a, b):
    return jnp.dot(a, b, preferred_element_type=F32)


def _dot_nt(a, b):
    return lax.dot_general(a, b, (((1,), (1,)), ((), ())), preferred_element_type=F32)


def _sigmoid(x):
    return 1.0 / (1.0 + jnp.exp(-x))


def _split_bf16(x):
    hi = x.astype(BF16)
    return hi, (x - hi.astype(F32)).astype(BF16)


@functools.lru_cache(maxsize=None)
def _rope_tables(n_pos):
    half = QK_ROPE // 2
    quarter = half // 2
    freqs = ROPE_BASE ** (-np.arange(quarter, dtype=np.float64) / quarter)
    pos = np.arange(n_pos)
    row = (pos // GRID_W).astype(np.float64)
    col = (pos % GRID_W).astype(np.float64)
    cos_t = np.ones((n_pos, LANES), np.float64)
    sin_a = np.zeros((n_pos, LANES), np.float64)
    sin_b = np.zeros((n_pos, LANES), np.float64)
    for base, p in ((QK_NOPE, row), (QK_NOPE + half, col)):
        ang = p[:, None] * freqs[None, :]
        cos_t[:, base:base + quarter] = np.cos(ang)
        cos_t[:, base + quarter:base + half] = np.cos(ang)
        sin_a[:, base:base + quarter] = -np.sin(ang)
        sin_b[:, base + quarter:base + half] = np.sin(ang)
    return (cos_t.astype(np.float32), sin_a.astype(np.float32), sin_b.astype(np.float32))


@functools.lru_cache(maxsize=None)
def _dft_tables(n):
    k = np.arange(n)
    ang = 2.0 * np.pi * ((k[:, None] * k[None, :]) % n) / n
    s = 1.0 / math.sqrt(n)
    return (np.cos(ang) * s).astype(np.float32), (-np.sin(ang) * s).astype(np.float32)


def _adaln_kernel(cond_ref, w_ref, b_ref, win_ref, o_ref, winp_ref):
    c = cond_ref[...]
    s_hi, s_lo = _split_bf16(c * _sigmoid(c))
    w_hi, w_lo = _split_bf16(w_ref[...])
    y = _dot(jnp.concatenate([s_hi, s_lo], axis=0), w_hi)
    o_ref[...] = (y[:MOD_ROWS] + y[MOD_ROWS:]) + _dot(s_hi, w_lo) + b_ref[...]

    winp_ref[:, :C_KPE] = win_ref[:, :C_KPE].astype(BF16)
    winp_ref[:, C_KPE:C_FN] = jnp.zeros((winp_ref.shape[0], LANES), BF16)
    winp_ref[:, C_KPE + QK_NOPE:C_KPE + QK_HEAD] = win_ref[:, C_KPE:C_KPE + QK_ROPE].astype(BF16)
    winp_ref[:, C_FN:] = win_ref[:, C_KPE + QK_ROPE:].astype(BF16)


def _adaln(cond8, w_mod, b_mod, w_in):
    n = w_mod.shape[1]
    steps = ADALN_STEPS
    tn, tr = n // steps, D_MODEL // steps
    return pl.pallas_call(
        _adaln_kernel,
        grid=(steps,),
        in_specs=[pl.BlockSpec((MOD_ROWS, D_MODEL), lambda j: (0, 0)),
                  pl.BlockSpec((D_MODEL, tn), lambda j: (0, j)),
                  pl.BlockSpec((1, tn), lambda j: (0, j)),
                  pl.BlockSpec((tr, w_in.shape[1]), lambda j: (j, 0))],
        out_specs=[pl.BlockSpec((MOD_ROWS, tn), lambda j: (0, j)),
                   pl.BlockSpec((tr, IN_PACKED), lambda j: (j, 0))],
        out_shape=[jax.ShapeDtypeStruct((MOD_ROWS, n), F32),
                   jax.ShapeDtypeStruct((D_MODEL, IN_PACKED), BF16)],
        compiler_params=_cparams(("arbitrary",)),
        name="adaln",
    )(cond8, w_mod, b_mod, w_in)


def _rms(x, width):
    return lax.rsqrt(jnp.sum(x * x, axis=-1, keepdims=True) * (1.0 / width) + EPS)


def _rope(x, cos_t, sin_a, sin_b):
    return x * cos_t + pltpu.roll(x, LANES - 8, 1) * sin_a + pltpu.roll(x, 8, 1) * sin_b


def _inproj_kernel(*refs, rope, emit_cache, n_cast):
    it = iter(refs)
    x_ref, mod_ref, g1_ref, win_ref, qag_ref, wqb_ref, kvg_ref, wkb_ref, wvb_ref = (
        next(it) for _ in range(9))
    qg_ref, kg_ref, dft_ref = next(it), next(it), next(it)
    if rope:
        cos_ref, sa_ref, sb_ref = next(it), next(it), next(it)
    cast_in = [next(it) for _ in range(n_cast)]
    q_ref, k_ref, v_ref, fcs_ref, sga_ref, sgf_ref = (next(it) for _ in range(6))
    if emit_cache:
        ckv_ref, kpe_ref = next(it), next(it)
    cast_out = [next(it) for _ in range(n_cast)]

    for src, dst in zip(cast_in, cast_out):
        dst[...] = src[...].astype(BF16)

    x = x_ref[...]
    shift = mod_ref[0, 0:1, :]
    scale = mod_ref[0, 1:2, :]
    h = (x * _rms(x, D_MODEL) * g1_ref[...]) * (1.0 + scale) + shift
    hb = h.astype(BF16)

    if rope:
        cos_t, sin_a, sin_b = cos_ref[...], sa_ref[...], sb_ref[...]

    qa = _dot(hb, win_ref[:, C_QA:C_QA + Q_LORA])
    qn = (qa * _rms(qa, Q_LORA) * qag_ref[...]).astype(BF16)
    q = _dot(qn, wqb_ref[...])
    qg = qg_ref[...] * ATTN_SCALE
    for hd in range(N_HEADS):
        qh = q[:, hd * HEAD_SLOT:(hd + 1) * HEAD_SLOT]
        qh = qh * _rms(qh, QK_HEAD) * qg
        if rope:
            qh = _rope(qh, cos_t, sin_a, sin_b)
        q_ref[:, hd * HEAD_SLOT:(hd + 1) * HEAD_SLOT] = qh.astype(BF16)

    kva = _dot(hb, win_ref[:, C_KVA:C_KVA + KV_LORA])
    ckv = kva * _rms(kva, KV_LORA) * kvg_ref[...]
    kpe = _dot(hb, win_ref[:, C_KPE:C_KPE + LANES])
    if emit_cache:
        ckv_ref[...] = ckv
        kpe_ref[...] = kpe[:, QK_NOPE:QK_NOPE + QK_ROPE]
    _emit_kv(ckv.astype(BF16), kpe, wkb_ref, wvb_ref, kg_ref,
             (cos_t, sin_a, sin_b) if rope else None, k_ref, v_ref)

    fn = _dot(hb, win_ref[:, C_FN:C_FN + FN_WIDTH]).astype(BF16)
    for g in range(FN_GROUPS):
        cs = _dot(fn[:, g * FN_GROUP_W:(g + 1) * FN_GROUP_W], dft_ref[...])
        fcs_ref[:, g * FN_GROUP_W:(g + 1) * FN_GROUP_W] = cs[:, :FN_GROUP_W].astype(BF16)
        fcs_ref[:, FN_WIDTH + g * FN_GROUP_W:FN_WIDTH + (g + 1) * FN_GROUP_W] = (
            cs[:, FN_GROUP_W:].astype(BF16))

    sga_ref[...] = _sigmoid(_dot(hb, win_ref[:, C_GA:C_GA + D_MODEL])).astype(BF16)
    sgf_ref[...] = _sigmoid(_dot(hb, win_ref[:, C_GF:C_GF + D_MODEL])).astype(BF16)


def _emit_kv(ckvb, kpe, wkb_ref, wvb_ref, kg_ref, rope_tabs, k_ref, v_ref):
    kg = kg_ref[...]
    v_ref[...] = _dot_nt(wvb_ref[...], ckvb).astype(BF16)
    kn = _dot(ckvb, wkb_ref[...])
    pe_ss = jnp.sum(kpe * kpe, axis=-1, keepdims=True)
    pe_g = kpe * kg
    if rope_tabs is not None:
        pe_g = _rope(pe_g, *rope_tabs)
    for hd in range(N_HEADS):
        knh = kn[:, hd * HEAD_SLOT:(hd + 1) * HEAD_SLOT]
        ss = jnp.sum(knh * knh, axis=-1, keepdims=True) + pe_ss
        r = lax.rsqrt(ss * (1.0 / QK_HEAD) + EPS)
        k_ref[:, hd * HEAD_SLOT:(hd + 1) * HEAD_SLOT] = ((knh * kg + pe_g) * r).astype(BF16)


def _const_spec(shape):
    return pl.BlockSpec(shape, lambda i: (0,) * len(shape))


def _inproj(x2d, mod3, mod_row_fn, wts, rope_tabs, emit_cache, cast=()):
    n = x2d.shape[0]
    assert all(w.shape[0] == n // TM for w in cast)
    rope = rope_tabs is not None
    tiles_per_seq = None if not rope else rope_tabs[0].shape[0] // TM
    in_specs = [pl.BlockSpec((TM, D_MODEL), lambda i: (i, 0)),
                pl.BlockSpec((1, 6, D_MODEL), lambda i: (mod_row_fn(i), 0, 0)),
                _const_spec((1, D_MODEL)),
                _const_spec((D_MODEL, IN_PACKED)),
                _const_spec((1, Q_LORA)),
                _const_spec((Q_LORA, QK_WIDTH)),
                _const_spec((1, KV_LORA)),
                _const_spec((KV_LORA, QK_WIDTH)),
                _const_spec((V_WIDTH, KV_LORA)),
                _const_spec((1, HEAD_SLOT)),
                _const_spec((1, HEAD_SLOT)),
                _const_spec((FN_GROUP_W, 2 * FN_GROUP_W))]
    args = [x2d, mod3, wts["g1"], wts["w_in"], wts["qag"], wts["w_qb"], wts["kvg"],
            wts["w_kb"], wts["w_vb"], wts["qg"], wts["kg"], wts["dft_c"]]
    if rope:
        in_specs += [pl.BlockSpec((TM, LANES), lambda i: (i % tiles_per_seq, 0))] * 3
        args += list(rope_tabs)
    cast_specs = [pl.BlockSpec((1,) + w.shape[1:], lambda i: (i, 0, 0)) for w in cast]
    in_specs += cast_specs
    args += list(cast)
    out_shape = [jax.ShapeDtypeStruct((n, QK_WIDTH), BF16),
                 jax.ShapeDtypeStruct((n, QK_WIDTH), BF16),
                 jax.ShapeDtypeStruct((V_WIDTH, n), BF16),
                 jax.ShapeDtypeStruct((n, 2 * FN_WIDTH), BF16),
                 jax.ShapeDtypeStruct((n, D_MODEL), BF16),
                 jax.ShapeDtypeStruct((n, D_MODEL), BF16)]
    out_specs = [pl.BlockSpec((TM, s.shape[1]), lambda i: (i, 0)) for s in out_shape]
    out_specs[2] = pl.BlockSpec((V_WIDTH, TM), lambda i: (0, i))
    if emit_cache:
        out_shape += [jax.ShapeDtypeStruct((n, KV_LORA), F32),
                      jax.ShapeDtypeStruct((n, QK_ROPE), F32)]
        out_specs += [pl.BlockSpec((TM, KV_LORA), lambda i: (i, 0)),
                      pl.BlockSpec((TM, QK_ROPE), lambda i: (i, 0))]
    out_shape += [jax.ShapeDtypeStruct(w.shape, BF16) for w in cast]
    out_specs += cast_specs
    return pl.pallas_call(
        functools.partial(_inproj_kernel, rope=rope, emit_cache=emit_cache, n_cast=len(cast)),
        grid=(n // TM,),
        in_specs=in_specs,
        out_specs=out_specs,
        out_shape=out_shape,
        compiler_params=_cparams(("parallel",)),
        name="inproj_lat" if rope else "inproj_ctx",
    )(*args)


def _cache_kv_kernel(ckv_ref, kpe_ref, wkb_ref, wvb_ref, kg_ref, k_ref, v_ref):
    _emit_kv(ckv_ref[...].astype(BF16), kpe_ref[...], wkb_ref, wvb_ref, kg_ref, None, k_ref, v_ref)


def _cache_kv(ckv2d, kpe_slot2d, wts):
    n = ckv2d.shape[0]
    return pl.pallas_call(
        _cache_kv_kernel,
        grid=(n // TM,),
        in_specs=[pl.BlockSpec((TM, KV_LORA), lambda i: (i, 0)),
                  pl.BlockSpec((TM, LANES), lambda i: (i, 0)),
                  _const_spec((KV_LORA, QK_WIDTH)),
                  _const_spec((V_WIDTH, KV_LORA)),
                  _const_spec((1, HEAD_SLOT))],
        out_specs=[pl.BlockSpec((TM, QK_WIDTH), lambda i: (i, 0)),
                   pl.BlockSpec((V_WIDTH, TM), lambda i: (0, i))],
        out_shape=[jax.ShapeDtypeStruct((n, QK_WIDTH), BF16),
                   jax.ShapeDtypeStruct((V_WIDTH, n), BF16)],
        compiler_params=_cparams(("parallel",)),
        name="cache_kv",
    )(ckv2d, kpe_slot2d, wts["w_kb"], wts["w_vb"], wts["kg"])


def _attn_kernel(*refs, n_kv):
    q_ref = refs[0]
    k_refs = refs[1:1 + n_kv]
    vt_refs = refs[1 + n_kv:1 + 2 * n_kv]
    o_ref = refs[1 + 2 * n_kv]
    head = lambda hd: slice(hd * HEAD_SLOT, (hd + 1) * HEAD_SLOT)
    st = [jnp.stack([_dot_nt(k[0, :, head(hd)], q_ref[0, :, head(hd)]) for hd in range(N_HEADS)])
          for k in k_refs]
    m = functools.reduce(jnp.maximum, [sj.max(axis=1, keepdims=True) for sj in st])
    p = [jnp.exp(sj - m) for sj in st]
    l = functools.reduce(lambda a, b: a + b, [pj.sum(axis=1, keepdims=True) for pj in p])
    outs = []
    for hd in range(N_HEADS):
        o = functools.reduce(lambda a, b: a + b,
                             [_dot(vt[hd * V_HEAD:(hd + 1) * V_HEAD, :], pj[hd].astype(BF16))
                              for vt, pj in zip(vt_refs, p)])
        outs.append(o / l[hd])
    o_ref[0] = jnp.concatenate(outs, axis=0).T.astype(BF16)


def _attention(q3, ks, vts, tq, name):
    b, sq, _ = q3.shape
    n_kv = len(ks)
    in_specs = [pl.BlockSpec((1, tq, QK_WIDTH), lambda bi, qi: (bi, qi, 0))]
    in_specs += [pl.BlockSpec((1, k.shape[1], QK_WIDTH), lambda bi, qi: (bi, 0, 0)) for k in ks]
    in_specs += [pl.BlockSpec((V_WIDTH, k.shape[1]), lambda bi, qi: (0, bi)) for k in ks]
    return pl.pallas_call(
        functools.partial(_attn_kernel, n_kv=n_kv),
        grid=(b, sq // tq),
        in_specs=in_specs,
        out_specs=pl.BlockSpec((1, tq, V_WIDTH), lambda bi, qi: (bi, qi, 0)),
        out_shape=jax.ShapeDtypeStruct((b, sq, V_WIDTH), BF16),
        compiler_params=_cparams(("parallel", "parallel")),
        name=name,
    )(q3, *ks, *vts)


def _fourier_kernel(cs_ref, ns_ref, f_ref, o_ref, *, n_b):
    for b in range(n_b):
        xc = f_ref[b, :, :FN_WIDTH]
        xs = f_ref[b, :, FN_WIDTH:]
        o_ref[b] = (_dot(cs_ref[...], xc) + _dot(ns_ref[...], xs)).astype(BF16)


def _fourier(fcs3, cs, ns, tr, n_b, name):
    b, s, _ = fcs3.shape
    return pl.pallas_call(
        functools.partial(_fourier_kernel, n_b=n_b),
        grid=(b // n_b, s // tr),
        in_specs=[pl.BlockSpec((tr, s), lambda bi, ri: (ri, 0)),
                  pl.BlockSpec((tr, s), lambda bi, ri: (ri, 0)),
                  pl.BlockSpec((n_b, s, 2 * FN_WIDTH), lambda bi, ri: (bi, 0, 0))],
        out_specs=pl.BlockSpec((n_b, tr, FN_WIDTH), lambda bi, ri: (bi, ri, 0)),
        out_shape=jax.ShapeDtypeStruct((b, s, FN_WIDTH), BF16),
        compiler_params=_cparams(("parallel", "parallel")),
        name=name,
    )(cs, ns, fcs3)


def _router_logits(w, b, h_hi, h_lo):
    rows = w.shape[0]
    w_hi, w_lo = _split_bf16(w)
    y = _dot_nt(jnp.concatenate([w_hi, w_lo], axis=0), h_hi)
    return (y[:rows] + y[rows:]) + _dot_nt(w_hi, h_lo) + b


def _rows(x, n):
    return [x[j:j + 1, :] for j in range(n)]


def _first_argmax(rows, top):
    idx = jnp.full(top.shape, len(rows) - 1, jnp.int32)
    for j in range(len(rows) - 2, -1, -1):
        idx = jnp.where(rows[j] == top, j, idx)
    return idx


def _store_token_major(ref, x, tm):
    for s in range(TOKEN_ROWS):
        ref[pl.ds(s, tm, stride=TOKEN_ROWS), :] = x[:, s * LANES:(s + 1) * LANES]


def _load_token_major(ref, tm):
    return jnp.concatenate([ref[pl.ds(s, tm, stride=TOKEN_ROWS), :] for s in range(TOKEN_ROWS)],
                           axis=1)


def _merge_kernel(x_ref, attn_ref, fm_ref, sga_ref, sgf_ref, mod_ref, wao_ref, wfn_ref, wout_ref,
                  g2_ref, wrg_ref, brg_ref, x1_ref, h2_ref, grp_ref, rank_ref, cnt_ref, carry_ref):
    i = pl.program_id(0)

    @pl.when(i == 0)
    def _():
        carry_ref[...] = jnp.zeros_like(carry_ref)

    a = _dot(attn_ref[...], wao_ref[...])
    f = _dot(fm_ref[...], wfn_ref[...])
    u = sga_ref[...].astype(F32) * a + sgf_ref[...].astype(F32) * f
    y = _dot(u.astype(BF16), wout_ref[...])
    x1 = x_ref[...] + mod_ref[0, 2:3, :] * y
    x1_ref[...] = x1
    tm = x1.shape[0]
    h2 = (x1 * _rms(x1, D_MODEL) * g2_ref[...]) * (1.0 + mod_ref[0, 4:5, :]) + mod_ref[0, 3:4, :]
    _store_token_major(h2_ref, h2, tm)

    h2_hi, h2_lo = _split_bf16(h2)
    g = _rows(_router_logits(wrg_ref[...], brg_ref[...], h2_hi, h2_lo), N_GROUPS)
    gidx = _first_argmax(g, functools.reduce(jnp.maximum, g))

    onehot = jnp.where(lax.broadcasted_iota(jnp.int32, (SEG_ROWS, tm), 0) == gidx, 1.0, 0.0)
    before = (lax.broadcasted_iota(jnp.int32, (tm, tm), 0)
              < lax.broadcasted_iota(jnp.int32, (tm, tm), 1))
    prefix = _dot(onehot.astype(BF16), jnp.where(before, 1.0, 0.0).astype(BF16))
    carry = carry_ref[...]
    rank = jnp.sum(onehot * (prefix + carry[:, 0:1]), axis=0, keepdims=True)
    grp_ref[...] = gidx
    rank_ref[...] = rank.astype(jnp.int32)
    carry = carry + jnp.sum(onehot, axis=1, keepdims=True)
    carry_ref[...] = carry
    cnt_ref[...] = carry.astype(jnp.int32)


def _merge(x2d, attn2d, fm2d, sga, sgf, mod3, mod_row_fn, wts):
    n = x2d.shape[0]
    tok = lambda w: pl.BlockSpec((TM, w), lambda i: (i, 0))
    return pl.pallas_call(
        _merge_kernel,
        grid=(n // TM,),
        in_specs=[tok(D_MODEL), tok(V_WIDTH), tok(FN_WIDTH), tok(D_MODEL), tok(D_MODEL),
                  pl.BlockSpec((1, 6, D_MODEL), lambda i: (mod_row_fn(i), 0, 0)),
                  _const_spec((V_WIDTH, D_MODEL)),
                  _const_spec((FN_WIDTH, D_MODEL)),
                  _const_spec((D_MODEL, D_MODEL)),
                  _const_spec((1, D_MODEL)),
                  _const_spec((ROUTER_ROWS, D_MODEL)),
                  _const_spec((ROUTER_ROWS, 1))],
        out_specs=[tok(D_MODEL),
                   pl.BlockSpec((TM * TOKEN_ROWS, LANES), lambda i: (i, 0)),
                   pl.BlockSpec((1, TM), lambda i: (0, i)),
                   pl.BlockSpec((1, TM), lambda i: (0, i)), _const_spec((SEG_ROWS, LANES))],
        out_shape=[jax.ShapeDtypeStruct((n, D_MODEL), F32),
                   jax.ShapeDtypeStruct((n * TOKEN_ROWS, LANES), F32),
                   jax.ShapeDtypeStruct((1, n), jnp.int32),
                   jax.ShapeDtypeStruct((1, n), jnp.int32),
                   jax.ShapeDtypeStruct((SEG_ROWS, LANES), jnp.int32)],
        scratch_shapes=[pltpu.VMEM((SEG_ROWS, LANES), F32)],
        compiler_params=_cparams(("arbitrary",)),
        name="merge",
    )(x2d, attn2d, fm2d, sga, sgf, mod3, wts["w_ao"], wts["w_fn"], wts["w_out"], wts["g2"],
      wts["w_rg"], wts["b_rg"])


def _token_rows(ref, t):
    start = t * TOKEN_ROWS
    if not isinstance(t, int):
        start = pl.multiple_of(start, TOKEN_ROWS)
    return ref.at[pl.ds(start, TOKEN_ROWS)]


def _row_copy(src, dst, src_tok, dst_tok, sem):
    return pltpu.make_async_copy(_token_rows(src, src_tok), _token_rows(dst, dst_tok), sem)


def _wait_tile(src_hbm, buf, sem):
    pltpu.make_async_copy(src_hbm.at[pl.ds(0, buf.shape[0])], buf, sem).wait()


def _combine_kernel(pos_ref, y_hbm, x1_ref, mod_ref, o_ref, buf, sem):
    i = pl.program_id(0)
    tm = o_ref.shape[0]

    def gather(tile_idx, slot):
        def start(r, c):
            _row_copy(y_hbm, buf.at[slot], pos_ref[tile_idx * tm + r], r, sem.at[slot]).start()
            return c

        lax.fori_loop(0, tm, start, 0, unroll=32)

    @pl.when(i == 0)
    def _():
        gather(0, 0)

    @pl.when(i + 1 < pl.num_programs(0))
    def _():
        gather(i + 1, (i + 1) % 2)

    slot = i % 2

    _wait_tile(y_hbm, buf.at[slot], sem.at[slot])
    o_ref[...] = x1_ref[...] + mod_ref[0, 5:6, :] * _load_token_major(buf.at[slot], tm)


def _combine(y_sorted, pos, x1, mod3, mod_row_fn):
    n = x1.shape[0]
    return pl.pallas_call(
        _combine_kernel,
        grid_spec=pltpu.PrefetchScalarGridSpec(
            num_scalar_prefetch=1,
            grid=(n // TM,),
            in_specs=[pl.BlockSpec(memory_space=pl.ANY),
                      pl.BlockSpec((TM, D_MODEL), lambda i, pos: (i, 0)),
                      pl.BlockSpec((1, 6, D_MODEL), lambda i, pos: (mod_row_fn(i), 0, 0))],
            out_specs=pl.BlockSpec((TM, D_MODEL), lambda i, pos: (i, 0)),
            scratch_shapes=[pltpu.VMEM((2, TM * TOKEN_ROWS, LANES), F32),
                            pltpu.SemaphoreType.DMA((2,))]),
        out_shape=jax.ShapeDtypeStruct((n, D_MODEL), F32),
        compiler_params=_cparams(("arbitrary",)),
        name="moe_combine",
    )(pos, y_sorted, x1, mod3)


def _moe_kernel(grp_ref, on_ref, src_ref, h_hbm, wrg_ref, brg_ref, wre_ref, bre_ref, wg_ref, wu_ref,
                wd_ref, o_ref, hbuf, sem, *, tile):
    i = pl.program_id(0)

    def gather(tile_idx, slot):
        def start(r, c):
            _row_copy(h_hbm, hbuf.at[slot], src_ref[tile_idx * tile + r], r, sem.at[slot]).start()
            return c

        lax.fori_loop(0, tile, start, 0, unroll=32)

    @pl.when(jnp.logical_and(i == 0, on_ref[0] == 1))
    def _():
        gather(0, 0)

    nxt = jnp.minimum(i + 1, pl.num_programs(0) - 1)

    @pl.when(jnp.logical_and(i + 1 < pl.num_programs(0), on_ref[nxt] == 1))
    def _():
        gather(i + 1, (i + 1) % 2)

    @pl.when(on_ref[i] == 0)
    def _():
        o_ref[...] = jnp.zeros_like(o_ref)

    @pl.when(on_ref[i] == 1)
    def _():
        slot = i % 2

        _wait_tile(h_hbm, hbuf.at[slot], sem.at[slot])
        h2_hi, h2_lo = _split_bf16(_load_token_major(hbuf.at[slot], tile))

        logits = _router_logits(jnp.concatenate([wrg_ref[...], wre_ref[0]], axis=0),
                                jnp.concatenate([brg_ref[...], bre_ref[0]], axis=0), h2_hi, h2_lo)
        g = _rows(logits, N_GROUPS)
        gmax = functools.reduce(jnp.maximum, g)
        p_top = 1.0 / functools.reduce(lambda p, q: p + q, [jnp.exp(gj - gmax) for gj in g])
        e = _rows(logits[ROUTER_ROWS:], EXPERTS_PER_GROUP)
        m1 = functools.reduce(jnp.maximum, e)
        i1 = _first_argmax(e, m1)
        rest = [jnp.where(i1 == j, -jnp.inf, e[j]) for j in range(EXPERTS_PER_GROUP)]
        m2 = functools.reduce(jnp.maximum, rest)
        i2 = _first_argmax(rest, m2)
        t = jnp.exp(m2 - m1)
        w1 = p_top / (1.0 + t)
        w2 = p_top * t / (1.0 + t)
        row = lax.broadcasted_iota(jnp.int32, (LANES, tile), 0)
        comb = (jnp.where(row == i1, w1, 0.0) + jnp.where(row == i2, w2, 0.0)).T

        acc = None
        for j in range(EXPERTS_PER_GROUP):
            a = _dot(h2_hi, wg_ref[j])
            u = _dot(h2_hi, wu_ref[j])
            act = (a * _sigmoid(a)) * u * comb[:, j:j + 1]
            y = _dot(act.astype(BF16), wd_ref[j])
            acc = y if acc is None else acc + y
        _store_token_major(o_ref, acc, tile)


def _moe(h2tm, src, maps, wts, w_gate, w_up, w_down, tile):
    n_tiles = src.shape[0] // tile
    const = lambda shape: pl.BlockSpec(shape, lambda i, grp, on, src: (0,) * len(shape))
    by_group = lambda shape: pl.BlockSpec(
        shape, lambda i, grp, on, src: (grp[i],) + (0,) * (len(shape) - 1))
    return pl.pallas_call(
        functools.partial(_moe_kernel, tile=tile),
        grid_spec=pltpu.PrefetchScalarGridSpec(
            num_scalar_prefetch=3,
            grid=(n_tiles,),
            in_specs=[pl.BlockSpec(memory_space=pl.ANY),
                      const((ROUTER_ROWS, D_MODEL)), const((ROUTER_ROWS, 1)),
                      by_group((1, ROUTER_ROWS, D_MODEL)), by_group((1, ROUTER_ROWS, 1)),
                      by_group((EXPERTS_PER_GROUP, D_MODEL, D_EXPERT)),
                      by_group((EXPERTS_PER_GROUP, D_MODEL, D_EXPERT)),
                      by_group((EXPERTS_PER_GROUP, D_EXPERT, D_MODEL))],
            out_specs=pl.BlockSpec((tile * TOKEN_ROWS, LANES), lambda i, grp, on, src: (i, 0)),
            scratch_shapes=[pltpu.VMEM((2, tile * TOKEN_ROWS, LANES), F32),
                            pltpu.SemaphoreType.DMA((2,))]),
        out_shape=jax.ShapeDtypeStruct((n_tiles * tile * TOKEN_ROWS, LANES), F32),
        compiler_params=_cparams(("arbitrary",)),
        name="moe",
    )(*maps, src, h2tm, wts["w_rg"], wts["b_rg"], wts["w_re"], wts["b_re"], w_gate, w_up, w_down)


def _invert_kernel(pos_ref, lo_ref, hi_ref, src_ref, *, n, n_fill):
    def zero(p, c):
        src_ref[p] = 0
        return c

    for s in range(n_fill):
        lax.fori_loop(lo_ref[s], hi_ref[s], zero, 0)

    def put(t, c):
        src_ref[pos_ref[t]] = t
        return c

    lax.fori_loop(0, n, put, 0, unroll=8)


def _invert(pos, fill_lo, fill_hi, n_sorted):
    return pl.pallas_call(
        functools.partial(_invert_kernel, n=pos.shape[0], n_fill=fill_lo.shape[0]),
        grid_spec=pltpu.PrefetchScalarGridSpec(
            num_scalar_prefetch=3, grid=(1,), in_specs=[],
            out_specs=pl.BlockSpec(memory_space=pltpu.SMEM)),
        out_shape=jax.ShapeDtypeStruct((n_sorted,), jnp.int32),
        name="moe_invert",
    )(pos, fill_lo, fill_hi)


def _sort_plan(grp, rank, cnt, tile, n_tiles):
    n_tile_grp = (cnt + tile - 1) // tile
    tile_end = jnp.cumsum(n_tile_grp)
    tile_start = tile_end - n_tile_grp
    total = tile_end[-1]
    pos = ((tile_start * tile)[grp] + rank).astype(jnp.int32)
    fill_lo = jnp.concatenate([tile_start * tile + cnt, total[None] * tile]).astype(jnp.int32)
    fill_hi = jnp.concatenate([tile_end * tile, jnp.full((1,), n_tiles * tile)]).astype(jnp.int32)
    src = _invert(pos, fill_lo, fill_hi, n_tiles * tile)
    i = jnp.arange(n_tiles, dtype=jnp.int32)
    tile_grp = jnp.sum((jnp.minimum(i, total - 1)[:, None] >= tile_end[None, :]).astype(jnp.int32),
                       axis=1)
    return pos, src, (tile_grp.astype(jnp.int32), (i < total).astype(jnp.int32))


def _sparse_moe(x1, h2tm, grp, rank, cnt8, mod3, mod_row_fn, wts, w_gate, w_up, w_down, tile):
    n = x1.shape[0]
    n_tiles = n // tile + N_GROUPS
    pos, src, maps = _sort_plan(grp.reshape(n), rank.reshape(n), cnt8[:N_GROUPS, 0], tile, n_tiles)
    y_sorted = _moe(h2tm, src, maps, wts, w_gate, w_up, w_down, tile)
    return _combine(y_sorted, pos, x1, mod3, mod_row_fn)


def _pack_weights(l, norm1_g, q_a_norm_g, w_q_b, kv_a_norm_g, w_kv_b, q_norm_g, k_norm_g,
                  w_attn_o, w_fnet, w_out, norm2_g, w_router_group, b_router_group,
                  w_router_expert, b_router_expert):
    w_qb = jnp.pad(w_q_b[l].reshape(Q_LORA, N_HEADS, QK_HEAD),
                   ((0, 0), (0, 0), (0, HEAD_SLOT - QK_HEAD))).reshape(Q_LORA, QK_WIDTH)
    wkv = w_kv_b[l].reshape(KV_LORA, N_HEADS, QK_NOPE + V_HEAD)
    w_kb = jnp.pad(wkv[:, :, :QK_NOPE],
                   ((0, 0), (0, 0), (0, HEAD_SLOT - QK_NOPE))).reshape(KV_LORA, QK_WIDTH)
    w_vb = wkv[:, :, QK_NOPE:].reshape(KV_LORA, V_WIDTH)
    pad_g = lambda g: jnp.pad(g, (0, HEAD_SLOT - QK_HEAD)).reshape(1, HEAD_SLOT)
    w_rg = jnp.pad(w_router_group[l].T, ((0, ROUTER_ROWS - N_GROUPS), (0, 0)))
    b_rg = jnp.pad(b_router_group[l], (0, ROUTER_ROWS - N_GROUPS)).reshape(ROUTER_ROWS, 1)
    w_re = jnp.pad(w_router_expert[l].T.reshape(N_GROUPS, EXPERTS_PER_GROUP, D_MODEL),
                   ((0, 0), (0, ROUTER_ROWS - EXPERTS_PER_GROUP), (0, 0)))
    b_re = jnp.pad(b_router_expert[l].reshape(N_GROUPS, EXPERTS_PER_GROUP),
                   ((0, 0), (0, ROUTER_ROWS - EXPERTS_PER_GROUP))).reshape(N_GROUPS, ROUTER_ROWS, 1)
    dft_c, dft_ns = _dft_tables(FN_GROUP_W)
    return {
        "g1": norm1_g[l].reshape(1, D_MODEL),
        "qag": q_a_norm_g[l].reshape(1, Q_LORA),
        "w_qb": w_qb.astype(BF16),
        "kvg": kv_a_norm_g[l].reshape(1, KV_LORA),
        "w_kb": w_kb.astype(BF16),
        "w_vb": w_vb.T.astype(BF16),
        "qg": pad_g(q_norm_g[l]),
        "kg": pad_g(k_norm_g[l]),
        "dft_c": jnp.concatenate([jnp.asarray(dft_c), -jnp.asarray(dft_ns)], axis=1).astype(BF16),
        "w_ao": w_attn_o[l].astype(BF16),
        "w_fn": w_fnet[l].astype(BF16),
        "w_out": w_out[l].astype(BF16),
        "g2": norm2_g[l].reshape(1, D_MODEL),
        "w_rg": w_rg,
        "b_rg": b_rg,
        "w_re": w_re,
        "b_re": b_re,
    }


def _layer(xp, xs, cache_ckv_l, cache_kpe_l, mod3, wts, experts):
    bp, sp, _ = xp.shape
    bs, ss, _ = xs.shape
    past = cache_ckv_l.shape[1]
    ctx_row = lambda i: 0
    lat_row = lambda i: 1 + i // (ss // TM)

    xp2 = xp.reshape(bp * sp, D_MODEL)
    ride = (bp * sp) // TM == N_EXPERTS
    outs = _inproj(xp2, mod3, ctx_row, wts, None, True, experts if ride else ())
    q, k, v, fcs, sga, sgf, ckv, kpe = outs[:8]
    w_gate, w_up, w_down = outs[8:] if ride else (w.astype(BF16) for w in experts)
    attn = _attention(q.reshape(bp, sp, QK_WIDTH), [k.reshape(bp, sp, QK_WIDTH)],
                      [v], sp, "attn_ctx")
    cs, ns = (jnp.asarray(t).astype(BF16) for t in _dft_tables(sp))
    fm = _fourier(fcs.reshape(bp, sp, 2 * FN_WIDTH), cs, ns, sp, SEQ_PER_STEP_CTX, "fourier_ctx")
    x1, h2tm, grp, rank, cnt = _merge(xp2, attn.reshape(bp * sp, V_WIDTH),
                                      fm.reshape(bp * sp, FN_WIDTH), sga, sgf, mod3, ctx_row, wts)
    yp = _sparse_moe(x1, h2tm, grp, rank, cnt, mod3, ctx_row, wts, w_gate, w_up, w_down,
                     MOE_TILE).reshape(bp, sp, D_MODEL)

    xs2 = xs.reshape(bs * ss, D_MODEL)
    rope_tabs = tuple(jnp.asarray(t) for t in _rope_tables(ss))
    q, k, v, fcs, sga, sgf = _inproj(xs2, mod3, lat_row, wts, rope_tabs, False)
    kpe_slot = jnp.pad(cache_kpe_l, ((0, 0), (0, 0), (QK_NOPE, LANES - QK_HEAD)))
    kc, vc = _cache_kv(cache_ckv_l.reshape(bs * past, KV_LORA), kpe_slot.reshape(bs * past, LANES), wts)
    attn = _attention(q.reshape(bs, ss, QK_WIDTH),
                      [kc.reshape(bs, past, QK_WIDTH), k.reshape(bs, ss, QK_WIDTH)],
                      [vc, v], TQ_LAT, "attn_lat")
    cs, ns = (jnp.asarray(t).astype(BF16) for t in _dft_tables(ss))
    fm = _fourier(fcs.reshape(bs, ss, 2 * FN_WIDTH), cs, ns, TQ_LAT, 1, "fourier_lat")
    x1, h2tm, grp, rank, cnt = _merge(xs2, attn.reshape(bs * ss, V_WIDTH),
                                      fm.reshape(bs * ss, FN_WIDTH), sga, sgf, mod3, lat_row, wts)
    ys = _sparse_moe(x1, h2tm, grp, rank, cnt, mod3, lat_row, wts, w_gate, w_up, w_down,
                     MOE_TILE).reshape(bs, ss, D_MODEL)

    return yp, ys, ckv.reshape(bp, sp, KV_LORA), kpe.reshape(bp, sp, QK_ROPE)


def kernel(x_prompt, x_sample, cache_ckv, cache_kpe, c, c_ctx, w_mod, b_mod, norm1_g, w_in, q_a_norm_g, w_q_b, kv_a_norm_g, w_kv_b, q_norm_g, k_norm_g, w_attn_o, w_fnet, w_out, norm2_g, w_router_group, b_router_group, w_router_expert, b_router_expert, w_exp_gate, w_exp_up, w_exp_down):
    depth = w_mod.shape[0]
    n_lat = c.shape[0]
    assert 1 + n_lat <= MOD_ROWS
    cond8 = jnp.concatenate([c_ctx[None, :], c, jnp.zeros((MOD_ROWS - 1 - n_lat, D_MODEL), F32)], axis=0)
    xp, xs = x_prompt, x_sample
    ckv_layers, kpe_layers = [], []
    for l in range(depth):
        mod, w_in_p = _adaln(cond8, w_mod[l], b_mod[l].reshape(1, -1), w_in[l])
        mod3 = mod.reshape(MOD_ROWS, 6, D_MODEL)
        wts = _pack_weights(l, norm1_g, q_a_norm_g, w_q_b, kv_a_norm_g, w_kv_b, q_norm_g,
                            k_norm_g, w_attn_o, w_fnet, w_out, norm2_g, w_router_group,
                            b_router_group, w_router_expert, b_router_expert)
        wts["w_in"] = w_in_p
        xp, xs, ckv, kpe = _layer(xp, xs, cache_ckv[:, l], cache_kpe[:, l], mod3, wts,
                                  (w_exp_gate[l], w_exp_up[l], w_exp_down[l]))
        ckv_layers.append(ckv)
        kpe_layers.append(kpe)
    return xp, xs, jnp.stack(ckv_layers, axis=1), jnp.stack(kpe_layers, axis=1)
```

```python
import functools
import math

import numpy as np
import jax
import jax.numpy as jnp
from jax import lax
from jax.experimental import pallas as pl
from jax.experimental.pallas import tpu as pltpu

D_MODEL = 1024
GRID_W = 64
N_HEADS = 8
Q_LORA = 512
KV_LORA = 256
QK_NOPE = 64
QK_ROPE = 32
V_HEAD = 64
QK_HEAD = QK_NOPE + QK_ROPE
ATTN_SCALE = QK_HEAD ** -0.5
ROPE_BASE = 10000.0
FN_GROUPS = 4
FN_GROUP_W = 128
FN_WIDTH = FN_GROUPS * FN_GROUP_W
N_GROUPS = 4
EXPERTS_PER_GROUP = 4
N_EXPERTS = N_GROUPS * EXPERTS_PER_GROUP
D_EXPERT = 512
EPS = 1e-6

LANES = 128
HEAD_SLOT = LANES
QK_WIDTH = N_HEADS * HEAD_SLOT
V_WIDTH = N_HEADS * V_HEAD
C_QA = 0
C_KVA = C_QA + Q_LORA
C_KPE = C_KVA + KV_LORA
C_FN = C_KPE + LANES
C_GA = C_FN + FN_WIDTH
C_GF = C_GA + D_MODEL
IN_PACKED = C_GF + D_MODEL
SUBLANES = 8
ROUTER_ROWS = SUBLANES
MOD_ROWS = SUBLANES
VMEM_LIMIT = 56 * 1024 * 1024

TM = 512
TQ_LAT = 256
SEG_ROWS = SUBLANES
TOKEN_ROWS = D_MODEL // LANES
MOE_TILE = 256
ADALN_STEPS = 8

BF16 = jnp.bfloat16
F32 = jnp.float32


def _cparams(sem):
    return pltpu.CompilerParams(dimension_semantics=sem, vmem_limit_bytes=VMEM_LIMIT)


def _dot(a, b):
    return jnp.dot(a, b, preferred_element_type=F32)


def _dot_nt(a, b):
    return lax.dot_general(a, b, (((1,), (1,)), ((), ())), preferred_element_type=F32)


def _sigmoid(x):
    return 1.0 / (1.0 + jnp.exp(-x))


def _split_bf16(x):
    hi = x.astype(BF16)
    return hi, (x - hi.astype(F32)).astype(BF16)


@functools.lru_cache(maxsize=None)
def _rope_tables(n_pos):
    half = QK_ROPE // 2
    quarter = half // 2
    freqs = ROPE_BASE ** (-np.arange(quarter, dtype=np.float64) / quarter)
    pos = np.arange(n_pos)
    row = (pos // GRID_W).astype(np.float64)
    col = (pos % GRID_W).astype(np.float64)
    cos_t = np.ones((n_pos, LANES), np.float64)
    sin_a = np.zeros((n_pos, LANES), np.float64)
    sin_b = np.zeros((n_pos, LANES), np.float64)
    for base, p in ((QK_NOPE, row), (QK_NOPE + half, col)):
        ang = p[:, None] * freqs[None, :]
        cos_t[:, base:base + quarter] = np.cos(ang)
        cos_t[:, base + quarter:base + half] = np.cos(ang)
        sin_a[:, base:base + quarter] = -np.sin(ang)
        sin_b[:, base + quarter:base + half] = np.sin(ang)
    return (cos_t.astype(np.float32), sin_a.astype(np.float32), sin_b.astype(np.float32))


@functools.lru_cache(maxsize=None)
def _dft_tables(n):
    k = np.arange(n)
    ang = 2.0 * np.pi * ((k[:, None] * k[None, :]) % n) / n
    s = 1.0 / math.sqrt(n)
    return (np.cos(ang) * s).astype(np.float32), (-np.sin(ang) * s).astype(np.float32)


def _adaln_kernel(cond_ref, w_ref, b_ref, win_ref, o_ref, winp_ref):
    c = cond_ref[...]
    s_hi, s_lo = _split_bf16(c * _sigmoid(c))
    w_hi, w_lo = _split_bf16(w_ref[...])
    y = _dot(jnp.concatenate([s_hi, s_lo], axis=0), w_hi)
    o_ref[...] = (y[:MOD_ROWS] + y[MOD_ROWS:]) + _dot(s_hi, w_lo) + b_ref[...]

    winp_ref[:, :C_KPE] = win_ref[:, :C_KPE].astype(BF16)
    winp_ref[:, C_KPE:C_FN] = jnp.zeros((winp_ref.shape[0], LANES), BF16)
    winp_ref[:, C_KPE + QK_NOPE:C_KPE + QK_HEAD] = win_ref[:, C_KPE:C_KPE + QK_ROPE].astype(BF16)
    winp_ref[:, C_FN:] = win_ref[:, C_KPE + QK_ROPE:].astype(BF16)


def _adaln(cond8, w_mod, b_mod, w_in):
    n = w_mod.shape[1]
    steps = ADALN_STEPS
    tn, tr = n // steps, D_MODEL // steps
    return pl.pallas_call(
        _adaln_kernel,
        grid=(steps,),
        in_specs=[pl.BlockSpec((MOD_ROWS, D_MODEL), lambda j: (0, 0)),
                  pl.BlockSpec((D_MODEL, tn), lambda j: (0, j)),
                  pl.BlockSpec((1, tn), lambda j: (0, j)),
                  pl.BlockSpec((tr, w_in.shape[1]), lambda j: (j, 0))],
        out_specs=[pl.BlockSpec((MOD_ROWS, tn), lambda j: (0, j)),
                   pl.BlockSpec((tr, IN_PACKED), lambda j: (j, 0))],
        out_shape=[jax.ShapeDtypeStruct((MOD_ROWS, n), F32),
                   jax.ShapeDtypeStruct((D_MODEL, IN_PACKED), BF16)],
        compiler_params=_cparams(("arbitrary",)),
        name="adaln",
    )(cond8, w_mod, b_mod, w_in)


def _rms(x, width):
    return lax.rsqrt(jnp.sum(x * x, axis=-1, keepdims=True) * (1.0 / width) + EPS)


def _rope(x, cos_t, sin_a, sin_b):
    return x * cos_t + pltpu.roll(x, LANES - 8, 1) * sin_a + pltpu.roll(x, 8, 1) * sin_b


def _inproj_kernel(*refs, rope, emit_cache, n_cast, seq_len):
    it = iter(refs)
    x_ref, mod_ref, g1_ref, win_ref, qag_ref, wqb_ref, kvg_ref, wkb_ref, wvb_ref = (
        next(it) for _ in range(9))
    qg_ref, kg_ref, dft_ref = next(it), next(it), next(it)
    if rope:
        cos_ref, sa_ref, sb_ref = next(it), next(it), next(it)
    if seq_len:
        seq_cs_ref, seq_ns_ref = next(it), next(it)
    cast_in = [next(it) for _ in range(n_cast)]
    if seq_len:
        attn_ref, fm_ref, sga_ref, sgf_ref = (next(it) for _ in range(4))
    else:
        q_ref, k_ref, v_ref, fcs_ref, sga_ref, sgf_ref = (next(it) for _ in range(6))
    if emit_cache:
        ckv_ref, kpe_ref = next(it), next(it)
    cast_out = [next(it) for _ in range(n_cast)]
    if seq_len:
        q_ref, k_ref, v_ref, fcs_ref = (next(it) for _ in range(4))

    for src, dst in zip(cast_in, cast_out):
        dst[...] = src[...].astype(BF16)

    x = x_ref[...]
    shift = mod_ref[0, 0:1, :]
    scale = mod_ref[0, 1:2, :]
    h = (x * _rms(x, D_MODEL) * g1_ref[...]) * (1.0 + scale) + shift
    hb = h.astype(BF16)

    if rope:
        cos_t, sin_a, sin_b = cos_ref[...], sa_ref[...], sb_ref[...]

    qa = _dot(hb, win_ref[:, C_QA:C_QA + Q_LORA])
    qn = (qa * _rms(qa, Q_LORA) * qag_ref[...]).astype(BF16)
    q = _dot(qn, wqb_ref[...])
    qg = qg_ref[...] * ATTN_SCALE
    for hd in range(N_HEADS):
        qh = q[:, hd * HEAD_SLOT:(hd + 1) * HEAD_SLOT]
        qh = qh * _rms(qh, QK_HEAD) * qg
        if rope:
            qh = _rope(qh, cos_t, sin_a, sin_b)
        q_ref[:, hd * HEAD_SLOT:(hd + 1) * HEAD_SLOT] = qh.astype(BF16)

    kva = _dot(hb, win_ref[:, C_KVA:C_KVA + KV_LORA])
    ckv = kva * _rms(kva, KV_LORA) * kvg_ref[...]
    kpe = _dot(hb, win_ref[:, C_KPE:C_KPE + LANES])
    if emit_cache:
        ckv_ref[...] = ckv
        kpe_ref[...] = kpe[:, QK_NOPE:QK_NOPE + QK_ROPE]
    _emit_kv(ckv.astype(BF16), kpe, wkb_ref, wvb_ref, kg_ref,
             (cos_t, sin_a, sin_b) if rope else None, k_ref, v_ref)

    fn = _dot(hb, win_ref[:, C_FN:C_FN + FN_WIDTH]).astype(BF16)
    for g in range(FN_GROUPS):
        cs = _dot(fn[:, g * FN_GROUP_W:(g + 1) * FN_GROUP_W], dft_ref[...])
        fcs_ref[:, g * FN_GROUP_W:(g + 1) * FN_GROUP_W] = cs[:, :FN_GROUP_W].astype(BF16)
        fcs_ref[:, FN_WIDTH + g * FN_GROUP_W:FN_WIDTH + (g + 1) * FN_GROUP_W] = (
            cs[:, FN_GROUP_W:].astype(BF16))

    sga_ref[...] = _sigmoid(_dot(hb, win_ref[:, C_GA:C_GA + D_MODEL])).astype(BF16)
    sgf_ref[...] = _sigmoid(_dot(hb, win_ref[:, C_GF:C_GF + D_MODEL])).astype(BF16)

    if seq_len:
        for s0 in range(0, x.shape[0], seq_len):
            rows = pl.ds(s0, seq_len)
            attn_ref[rows, :] = _attend(q_ref.at[rows], [k_ref.at[rows]], [v_ref.at[:, rows]])
            fm_ref[rows, :] = (_dot(seq_cs_ref[...], fcs_ref[rows, :FN_WIDTH])
                               + _dot(seq_ns_ref[...], fcs_ref[rows, FN_WIDTH:])).astype(BF16)


def _emit_kv(ckvb, kpe, wkb_ref, wvb_ref, kg_ref, rope_tabs, k_ref, v_ref):
    kg = kg_ref[...]
    v_ref[...] = _dot_nt(wvb_ref[...], ckvb).astype(BF16)
    kn = _dot(ckvb, wkb_ref[...])
    pe_ss = jnp.sum(kpe * kpe, axis=-1, keepdims=True)
    pe_g = kpe * kg
    if rope_tabs is not None:
        pe_g = _rope(pe_g, *rope_tabs)
    for hd in range(N_HEADS):
        knh = kn[:, hd * HEAD_SLOT:(hd + 1) * HEAD_SLOT]
        ss = jnp.sum(knh * knh, axis=-1, keepdims=True) + pe_ss
        r = lax.rsqrt(ss * (1.0 / QK_HEAD) + EPS)
        k_ref[:, hd * HEAD_SLOT:(hd + 1) * HEAD_SLOT] = ((knh * kg + pe_g) * r).astype(BF16)


def _const_spec(shape):
    return pl.BlockSpec(shape, lambda i: (0,) * len(shape))


def _inproj(x2d, mod3, mod_row_fn, wts, rope_tabs, emit_cache, cast=(), seq=None):
    n = x2d.shape[0]
    assert all(w.shape[0] == n // TM for w in cast)
    assert seq is None or TM % seq[0] == 0
    rope = rope_tabs is not None
    tiles_per_seq = None if not rope else rope_tabs[0].shape[0] // TM
    in_specs = [pl.BlockSpec((TM, D_MODEL), lambda i: (i, 0)),
                pl.BlockSpec((1, 6, D_MODEL), lambda i: (mod_row_fn(i), 0, 0)),
                _const_spec((1, D_MODEL)),
                _const_spec((D_MODEL, IN_PACKED)),
                _const_spec((1, Q_LORA)),
                _const_spec((Q_LORA, QK_WIDTH)),
                _const_spec((1, KV_LORA)),
                _const_spec((KV_LORA, QK_WIDTH)),
                _const_spec((V_WIDTH, KV_LORA)),
                _const_spec((1, HEAD_SLOT)),
                _const_spec((1, HEAD_SLOT)),
                _const_spec((FN_GROUP_W, 2 * FN_GROUP_W))]
    args = [x2d, mod3, wts["g1"], wts["w_in"], wts["qag"], wts["w_qb"], wts["kvg"],
            wts["w_kb"], wts["w_vb"], wts["qg"], wts["kg"], wts["dft_c"]]
    if rope:
        in_specs += [pl.BlockSpec((TM, LANES), lambda i: (i % tiles_per_seq, 0))] * 3
        args += list(rope_tabs)
    if seq is not None:
        in_specs += [_const_spec((seq[0], seq[0]))] * 2
        args += [seq[1], seq[2]]
    cast_specs = [pl.BlockSpec((1,) + w.shape[1:], lambda i: (i, 0, 0)) for w in cast]
    in_specs += cast_specs
    args += list(cast)
    qkvf = [((TM, QK_WIDTH), (n, QK_WIDTH)), ((TM, QK_WIDTH), (n, QK_WIDTH)),
            ((V_WIDTH, TM), (V_WIDTH, n)), ((TM, 2 * FN_WIDTH), (n, 2 * FN_WIDTH))]
    if seq is None:
        widths = [None] * 4 + [D_MODEL, D_MODEL]
    else:
        widths = [V_WIDTH, FN_WIDTH, D_MODEL, D_MODEL]
    out_shape, out_specs = [], []
    for j, w in enumerate(widths):
        if w is None:
            out_shape.append(jax.ShapeDtypeStruct(qkvf[j][1], BF16))
            out_specs.append(pl.BlockSpec(qkvf[j][0], (lambda i: (0, i)) if j == 2 else (lambda i: (i, 0))))
        else:
            out_shape.append(jax.ShapeDtypeStruct((n, w), BF16))
            out_specs.append(pl.BlockSpec((TM, w), lambda i: (i, 0)))
    if emit_cache:
        out_shape += [jax.ShapeDtypeStruct((n, KV_LORA), F32),
                      jax.ShapeDtypeStruct((n, QK_ROPE), F32)]
        out_specs += [pl.BlockSpec((TM, KV_LORA), lambda i: (i, 0)),
                      pl.BlockSpec((TM, QK_ROPE), lambda i: (i, 0))]
    out_shape += [jax.ShapeDtypeStruct(w.shape, BF16) for w in cast]
    out_specs += cast_specs
    scratch = [] if seq is None else [pltpu.VMEM(blk, BF16) for blk, _ in qkvf]
    return pl.pallas_call(
        functools.partial(_inproj_kernel, rope=rope, emit_cache=emit_cache, n_cast=len(cast),
                          seq_len=None if seq is None else seq[0]),
        grid=(n // TM,),
        in_specs=in_specs,
        out_specs=out_specs,
        out_shape=out_shape,
        scratch_shapes=scratch,
        compiler_params=_cparams(("parallel",)),
        name="inproj_lat" if rope else "inproj_ctx",
    )(*args)


def _cache_kv_kernel(ckv_ref, kpe_ref, wkb_ref, wvb_ref, kg_ref, k_ref, v_ref):
    _emit_kv(ckv_ref[...].astype(BF16), kpe_ref[...], wkb_ref, wvb_ref, kg_ref, None, k_ref, v_ref)


def _cache_kv(ckv2d, kpe_slot2d, wts):
    n = ckv2d.shape[0]
    return pl.pallas_call(
        _cache_kv_kernel,
        grid=(n // TM,),
        in_specs=[pl.BlockSpec((TM, KV_LORA), lambda i: (i, 0)),
                  pl.BlockSpec((TM, LANES), lambda i: (i, 0)),
                  _const_spec((KV_LORA, QK_WIDTH)),
                  _const_spec((V_WIDTH, KV_LORA)),
                  _const_spec((1, HEAD_SLOT))],
        out_specs=[pl.BlockSpec((TM, QK_WIDTH), lambda i: (i, 0)),
                   pl.BlockSpec((V_WIDTH, TM), lambda i: (0, i))],
        out_shape=[jax.ShapeDtypeStruct((n, QK_WIDTH), BF16),
                   jax.ShapeDtypeStruct((V_WIDTH, n), BF16)],
        compiler_params=_cparams(("parallel",)),
        name="cache_kv",
    )(ckv2d, kpe_slot2d, wts["w_kb"], wts["w_vb"], wts["kg"])


def _attend(q, ks, vts):
    head = lambda hd: slice(hd * HEAD_SLOT, (hd + 1) * HEAD_SLOT)
    st = [jnp.stack([_dot_nt(k[:, head(hd)], q[:, head(hd)]) for hd in range(N_HEADS)]) for k in ks]
    m = functools.reduce(jnp.maximum, [sj.max(axis=1, keepdims=True) for sj in st])
    p = [jnp.exp(sj - m) for sj in st]
    l = functools.reduce(lambda a, b: a + b, [pj.sum(axis=1, keepdims=True) for pj in p])
    outs = []
    for hd in range(N_HEADS):
        o = functools.reduce(lambda a, b: a + b,
                             [_dot(vt[hd * V_HEAD:(hd + 1) * V_HEAD, :], pj[hd].astype(BF16))
                              for vt, pj in zip(vts, p)])
        outs.append(o / l[hd])
    return jnp.concatenate(outs, axis=0).T.astype(BF16)


def _attn_kernel(*refs, n_kv):
    q_ref = refs[0]
    k_refs = refs[1:1 + n_kv]
    vt_refs = refs[1 + n_kv:1 + 2 * n_kv]
    o_ref = refs[1 + 2 * n_kv]
    o_ref[0] = _attend(q_ref.at[0], [k.at[0] for k in k_refs], vt_refs)


def _attention(q3, ks, vts, tq, name):
    b, sq, _ = q3.shape
    n_kv = len(ks)
    in_specs = [pl.BlockSpec((1, tq, QK_WIDTH), lambda bi, qi: (bi, qi, 0))]
    in_specs += [pl.BlockSpec((1, k.shape[1], QK_WIDTH), lambda bi, qi: (bi, 0, 0)) for k in ks]
    in_specs += [pl.BlockSpec((V_WIDTH, k.shape[1]), lambda bi, qi: (0, bi)) for k in ks]
    return pl.pallas_call(
        functools.partial(_attn_kernel, n_kv=n_kv),
        grid=(b, sq // tq),
        in_specs=in_specs,
        out_specs=pl.BlockSpec((1, tq, V_WIDTH), lambda bi, qi: (bi, qi, 0)),
        out_shape=jax.ShapeDtypeStruct((b, sq, V_WIDTH), BF16),
        compiler_params=_cparams(("parallel", "parallel")),
        name=name,
    )(q3, *ks, *vts)


def _fourier_kernel(cs_ref, ns_ref, f_ref, o_ref, *, n_b):
    for b in range(n_b):
        xc = f_ref[b, :, :FN_WIDTH]
        xs = f_ref[b, :, FN_WIDTH:]
        o_ref[b] = (_dot(cs_ref[...], xc) + _dot(ns_ref[...], xs)).astype(BF16)


def _fourier(fcs3, cs, ns, tr, n_b, name):
    b, s, _ = fcs3.shape
    return pl.pallas_call(
        functools.partial(_fourier_kernel, n_b=n_b),
        grid=(b // n_b, s // tr),
        in_specs=[pl.BlockSpec((tr, s), lambda bi, ri: (ri, 0)),
                  pl.BlockSpec((tr, s), lambda bi, ri: (ri, 0)),
                  pl.BlockSpec((n_b, s, 2 * FN_WIDTH), lambda bi, ri: (bi, 0, 0))],
        out_specs=pl.BlockSpec((n_b, tr, FN_WIDTH), lambda bi, ri: (bi, ri, 0)),
        out_shape=jax.ShapeDtypeStruct((b, s, FN_WIDTH), BF16),
        compiler_params=_cparams(("parallel", "parallel")),
        name=name,
    )(cs, ns, fcs3)


def _router_logits(w, b, h_hi, h_lo):
    rows = w.shape[0]
    w_hi, w_lo = _split_bf16(w)
    y = _dot_nt(jnp.concatenate([w_hi, w_lo], axis=0), h_hi)
    return (y[:rows] + y[rows:]) + _dot_nt(w_hi, h_lo) + b


def _rows(x, n):
    return [x[j:j + 1, :] for j in range(n)]


def _first_argmax(rows, top):
    idx = jnp.full(top.shape, len(rows) - 1, jnp.int32)
    for j in range(len(rows) - 2, -1, -1):
        idx = jnp.where(rows[j] == top, j, idx)
    return idx


def _store_token_major(ref, x, tm):
    for s in range(TOKEN_ROWS):
        ref[pl.ds(s, tm, stride=TOKEN_ROWS), :] = x[:, s * LANES:(s + 1) * LANES]


def _load_token_major(ref, tm):
    return jnp.concatenate([ref[pl.ds(s, tm, stride=TOKEN_ROWS), :] for s in range(TOKEN_ROWS)],
                           axis=1)


def _merge_kernel(x_ref, attn_ref, fm_ref, sga_ref, sgf_ref, mod_ref, wao_ref, wfn_ref, wout_ref,
                  g2_ref, wrg_ref, brg_ref, x1_ref, h2_ref, grp_ref, rank_ref, cnt_ref, carry_ref):
    i = pl.program_id(0)

    @pl.when(i == 0)
    def _():
        carry_ref[...] = jnp.zeros_like(carry_ref)

    a = _dot(attn_ref[...], wao_ref[...])
    f = _dot(fm_ref[...], wfn_ref[...])
    u = sga_ref[...].astype(F32) * a + sgf_ref[...].astype(F32) * f
    y = _dot(u.astype(BF16), wout_ref[...])
    x1 = x_ref[...] + mod_ref[0, 2:3, :] * y
    x1_ref[...] = x1
    tm = x1.shape[0]
    h2 = (x1 * _rms(x1, D_MODEL) * g2_ref[...]) * (1.0 + mod_ref[0, 4:5, :]) + mod_ref[0, 3:4, :]
    _store_token_major(h2_ref, h2, tm)

    h2_hi, h2_lo = _split_bf16(h2)
    g = _rows(_router_logits(wrg_ref[...], brg_ref[...], h2_hi, h2_lo), N_GROUPS)
    gidx = _first_argmax(g, functools.reduce(jnp.maximum, g))

    onehot = jnp.where(lax.broadcasted_iota(jnp.int32, (SEG_ROWS, tm), 0) == gidx, 1.0, 0.0)
    before = (lax.broadcasted_iota(jnp.int32, (tm, tm), 0)
              < lax.broadcasted_iota(jnp.int32, (tm, tm), 1))
    prefix = _dot(onehot.astype(BF16), jnp.where(before, 1.0, 0.0).astype(BF16))
    carry = carry_ref[...]
    rank = jnp.sum(onehot * (prefix + carry[:, 0:1]), axis=0, keepdims=True)
    grp_ref[...] = gidx
    rank_ref[...] = rank.astype(jnp.int32)
    carry = carry + jnp.sum(onehot, axis=1, keepdims=True)
    carry_ref[...] = carry
    cnt_ref[...] = carry.astype(jnp.int32)


def _merge(x2d, attn2d, fm2d, sga, sgf, mod3, mod_row_fn, wts):
    n = x2d.shape[0]
    tok = lambda w: pl.BlockSpec((TM, w), lambda i: (i, 0))
    return pl.pallas_call(
        _merge_kernel,
        grid=(n // TM,),
        in_specs=[tok(D_MODEL), tok(V_WIDTH), tok(FN_WIDTH), tok(D_MODEL), tok(D_MODEL),
                  pl.BlockSpec((1, 6, D_MODEL), lambda i: (mod_row_fn(i), 0, 0)),
                  _const_spec((V_WIDTH, D_MODEL)),
                  _const_spec((FN_WIDTH, D_MODEL)),
                  _const_spec((D_MODEL, D_MODEL)),
                  _const_spec((1, D_MODEL)),
                  _const_spec((ROUTER_ROWS, D_MODEL)),
                  _const_spec((ROUTER_ROWS, 1))],
        out_specs=[tok(D_MODEL),
                   pl.BlockSpec((TM * TOKEN_ROWS, LANES), lambda i: (i, 0)),
                   pl.BlockSpec((1, TM), lambda i: (0, i)),
                   pl.BlockSpec((1, TM), lambda i: (0, i)), _const_spec((SEG_ROWS, LANES))],
        out_shape=[jax.ShapeDtypeStruct((n, D_MODEL), F32),
                   jax.ShapeDtypeStruct((n * TOKEN_ROWS, LANES), F32),
                   jax.ShapeDtypeStruct((1, n), jnp.int32),
                   jax.ShapeDtypeStruct((1, n), jnp.int32),
                   jax.ShapeDtypeStruct((SEG_ROWS, LANES), jnp.int32)],
        scratch_shapes=[pltpu.VMEM((SEG_ROWS, LANES), F32)],
        compiler_params=_cparams(("arbitrary",)),
        name="merge",
    )(x2d, attn2d, fm2d, sga, sgf, mod3, wts["w_ao"], wts["w_fn"], wts["w_out"], wts["g2"],
      wts["w_rg"], wts["b_rg"])


def _token_rows(ref, t):
    start = t * TOKEN_ROWS
    if not isinstance(t, int):
        start = pl.multiple_of(start, TOKEN_ROWS)
    return ref.at[pl.ds(start, TOKEN_ROWS)]


def _row_copy(src, dst, src_tok, dst_tok, sem):
    return pltpu.make_async_copy(_token_rows(src, src_tok), _token_rows(dst, dst_tok), sem)


def _wait_tile(src_hbm, buf, sem):
    pltpu.make_async_copy(src_hbm.at[pl.ds(0, buf.shape[0])], buf, sem).wait()


def _combine_kernel(pos_ref, y_hbm, x1_ref, mod_ref, o_ref, buf, sem):
    i = pl.program_id(0)
    tm = o_ref.shape[0]

    def gather(tile_idx, slot):
        def start(r, c):
            _row_copy(y_hbm, buf.at[slot], pos_ref[tile_idx * tm + r], r, sem.at[slot]).start()
            return c

        lax.fori_loop(0, tm, start, 0, unroll=32)

    @pl.when(i == 0)
    def _():
        gather(0, 0)

    @pl.when(i + 1 < pl.num_programs(0))
    def _():
        gather(i + 1, (i + 1) % 2)

    slot = i % 2

    _wait_tile(y_hbm, buf.at[slot], sem.at[slot])
    o_ref[...] = x1_ref[...] + mod_ref[0, 5:6, :] * _load_token_major(buf.at[slot], tm)


def _combine(y_sorted, pos, x1, mod3, mod_row_fn):
    n = x1.shape[0]
    return pl.pallas_call(
        _combine_kernel,
        grid_spec=pltpu.PrefetchScalarGridSpec(
            num_scalar_prefetch=1,
            grid=(n // TM,),
            in_specs=[pl.BlockSpec(memory_space=pl.ANY),
                      pl.BlockSpec((TM, D_MODEL), lambda i, pos: (i, 0)),
                      pl.BlockSpec((1, 6, D_MODEL), lambda i, pos: (mod_row_fn(i), 0, 0))],
            out_specs=pl.BlockSpec((TM, D_MODEL), lambda i, pos: (i, 0)),
            scratch_shapes=[pltpu.VMEM((2, TM * TOKEN_ROWS, LANES), F32),
                            pltpu.SemaphoreType.DMA((2,))]),
        out_shape=jax.ShapeDtypeStruct((n, D_MODEL), F32),
        compiler_params=_cparams(("arbitrary",)),
        name="moe_combine",
    )(pos, y_sorted, x1, mod3)


def _moe_kernel(grp_ref, on_ref, src_ref, h_hbm, wrg_ref, brg_ref, wre_ref, bre_ref, wg_ref, wu_ref,
                wd_ref, o_ref, hbuf, sem, *, tile):
    i = pl.program_id(0)

    def gather(tile_idx, slot):
        def start(r, c):
            _row_copy(h_hbm, hbuf.at[slot], src_ref[tile_idx * tile + r], r, sem.at[slot]).start()
            return c

        lax.fori_loop(0, tile, start, 0, unroll=32)

    @pl.when(jnp.logical_and(i == 0, on_ref[0] == 1))
    def _():
        gather(0, 0)

    nxt = jnp.minimum(i + 1, pl.num_programs(0) - 1)

    @pl.when(jnp.logical_and(i + 1 < pl.num_programs(0), on_ref[nxt] == 1))
    def _():
        gather(i + 1, (i + 1) % 2)

    @pl.when(on_ref[i] == 0)
    def _():
        o_ref[...] = jnp.zeros_like(o_ref)

    @pl.when(on_ref[i] == 1)
    def _():
        slot = i % 2

        _wait_tile(h_hbm, hbuf.at[slot], sem.at[slot])
        h2_hi, h2_lo = _split_bf16(_load_token_major(hbuf.at[slot], tile))

        logits = _router_logits(jnp.concatenate([wrg_ref[...], wre_ref[0]], axis=0),
                                jnp.concatenate([brg_ref[...], bre_ref[0]], axis=0), h2_hi, h2_lo)
        g = _rows(logits, N_GROUPS)
        gmax = functools.reduce(jnp.maximum, g)
        p_top = 1.0 / functools.reduce(lambda p, q: p + q, [jnp.exp(gj - gmax) for gj in g])
        e = _rows(logits[ROUTER_ROWS:], EXPERTS_PER_GROUP)
        m1 = functools.reduce(jnp.maximum, e)
        i1 = _first_argmax(e, m1)
        rest = [jnp.where(i1 == j, -jnp.inf, e[j]) for j in range(EXPERTS_PER_GROUP)]
        m2 = functools.reduce(jnp.maximum, rest)
        i2 = _first_argmax(rest, m2)
        t = jnp.exp(m2 - m1)
        w1 = p_top / (1.0 + t)
        w2 = p_top * t / (1.0 + t)
        row = lax.broadcasted_iota(jnp.int32, (LANES, tile), 0)
        comb = (jnp.where(row == i1, w1, 0.0) + jnp.where(row == i2, w2, 0.0)).T

        acc = None
        for j in range(EXPERTS_PER_GROUP):
            a = _dot(h2_hi, wg_ref[j])
            u = _dot(h2_hi, wu_ref[j])
            act = (a * _sigmoid(a)) * u * comb[:, j:j + 1]
            y = _dot(act.astype(BF16), wd_ref[j])
            acc = y if acc is None else acc + y
        _store_token_major(o_ref, acc, tile)


def _moe(h2tm, src, maps, wts, w_gate, w_up, w_down, tile):
    n_tiles = src.shape[0] // tile
    const = lambda shape: pl.BlockSpec(shape, lambda i, grp, on, src: (0,) * len(shape))
    by_group = lambda shape: pl.BlockSpec(
        shape, lambda i, grp, on, src: (grp[i],) + (0,) * (len(shape) - 1))
    return pl.pallas_call(
        functools.partial(_moe_kernel, tile=tile),
        grid_spec=pltpu.PrefetchScalarGridSpec(
            num_scalar_prefetch=3,
            grid=(n_tiles,),
            in_specs=[pl.BlockSpec(memory_space=pl.ANY),
                      const((ROUTER_ROWS, D_MODEL)), const((ROUTER_ROWS, 1)),
                      by_group((1, ROUTER_ROWS, D_MODEL)), by_group((1, ROUTER_ROWS, 1)),
                      by_group((EXPERTS_PER_GROUP, D_MODEL, D_EXPERT)),
                      by_group((EXPERTS_PER_GROUP, D_MODEL, D_EXPERT)),
                      by_group((EXPERTS_PER_GROUP, D_EXPERT, D_MODEL))],
            out_specs=pl.BlockSpec((tile * TOKEN_ROWS, LANES), lambda i, grp, on, src: (i, 0)),
            scratch_shapes=[pltpu.VMEM((2, tile * TOKEN_ROWS, LANES), F32),
                            pltpu.SemaphoreType.DMA((2,))]),
        out_shape=jax.ShapeDtypeStruct((n_tiles * tile * TOKEN_ROWS, LANES), F32),
        compiler_params=_cparams(("arbitrary",)),
        name="moe",
    )(*maps, src, h2tm, wts["w_rg"], wts["b_rg"], wts["w_re"], wts["b_re"], w_gate, w_up, w_down)


def _invert_kernel(pos_ref, lo_ref, hi_ref, src_ref, *, n, n_fill):
    def zero(p, c):
        src_ref[p] = 0
        return c

    for s in range(n_fill):
        lax.fori_loop(lo_ref[s], hi_ref[s], zero, 0)

    def put(t, c):
        src_ref[pos_ref[t]] = t
        return c

    lax.fori_loop(0, n, put, 0, unroll=8)


def _invert(pos, fill_lo, fill_hi, n_sorted):
    return pl.pallas_call(
        functools.partial(_invert_kernel, n=pos.shape[0], n_fill=fill_lo.shape[0]),
        grid_spec=pltpu.PrefetchScalarGridSpec(
            num_scalar_prefetch=3, grid=(1,), in_specs=[],
            out_specs=pl.BlockSpec(memory_space=pltpu.SMEM)),
        out_shape=jax.ShapeDtypeStruct((n_sorted,), jnp.int32),
        name="moe_invert",
    )(pos, fill_lo, fill_hi)


def _sort_plan(grp, rank, cnt, tile, n_tiles):
    n_tile_grp = (cnt + tile - 1) // tile
    tile_end = jnp.cumsum(n_tile_grp)
    tile_start = tile_end - n_tile_grp
    total = tile_end[-1]
    pos = ((tile_start * tile)[grp] + rank).astype(jnp.int32)
    fill_lo = jnp.concatenate([tile_start * tile + cnt, total[None] * tile]).astype(jnp.int32)
    fill_hi = jnp.concatenate([tile_end * tile, jnp.full((1,), n_tiles * tile)]).astype(jnp.int32)
    src = _invert(pos, fill_lo, fill_hi, n_tiles * tile)
    i = jnp.arange(n_tiles, dtype=jnp.int32)
    tile_grp = jnp.sum((jnp.minimum(i, total - 1)[:, None] >= tile_end[None, :]).astype(jnp.int32),
                       axis=1)
    return pos, src, (tile_grp.astype(jnp.int32), (i < total).astype(jnp.int32))


def _sparse_moe(x1, h2tm, grp, rank, cnt8, mod3, mod_row_fn, wts, w_gate, w_up, w_down, tile):
    n = x1.shape[0]
    n_tiles = n // tile + N_GROUPS
    pos, src, maps = _sort_plan(grp.reshape(n), rank.reshape(n), cnt8[:N_GROUPS, 0], tile, n_tiles)
    y_sorted = _moe(h2tm, src, maps, wts, w_gate, w_up, w_down, tile)
    return _combine(y_sorted, pos, x1, mod3, mod_row_fn)


def _pack_weights(l, norm1_g, q_a_norm_g, w_q_b, kv_a_norm_g, w_kv_b, q_norm_g, k_norm_g,
                  w_attn_o, w_fnet, w_out, norm2_g, w_router_group, b_router_group,
                  w_router_expert, b_router_expert):
    w_qb = jnp.pad(w_q_b[l].reshape(Q_LORA, N_HEADS, QK_HEAD),
                   ((0, 0), (0, 0), (0, HEAD_SLOT - QK_HEAD))).reshape(Q_LORA, QK_WIDTH)
    wkv = w_kv_b[l].reshape(KV_LORA, N_HEADS, QK_NOPE + V_HEAD)
    w_kb = jnp.pad(wkv[:, :, :QK_NOPE],
                   ((0, 0), (0, 0), (0, HEAD_SLOT - QK_NOPE))).reshape(KV_LORA, QK_WIDTH)
    w_vb = wkv[:, :, QK_NOPE:].reshape(KV_LORA, V_WIDTH)
    pad_g = lambda g: jnp.pad(g, (0, HEAD_SLOT - QK_HEAD)).reshape(1, HEAD_SLOT)
    w_rg = jnp.pad(w_router_group[l].T, ((0, ROUTER_ROWS - N_GROUPS), (0, 0)))
    b_rg = jnp.pad(b_router_group[l], (0, ROUTER_ROWS - N_GROUPS)).reshape(ROUTER_ROWS, 1)
    w_re = jnp.pad(w_router_expert[l].T.reshape(N_GROUPS, EXPERTS_PER_GROUP, D_MODEL),
                   ((0, 0), (0, ROUTER_ROWS - EXPERTS_PER_GROUP), (0, 0)))
    b_re = jnp.pad(b_router_expert[l].reshape(N_GROUPS, EXPERTS_PER_GROUP),
                   ((0, 0), (0, ROUTER_ROWS - EXPERTS_PER_GROUP))).reshape(N_GROUPS, ROUTER_ROWS, 1)
    dft_c, dft_ns = _dft_tables(FN_GROUP_W)
    return {
        "g1": norm1_g[l].reshape(1, D_MODEL),
        "qag": q_a_norm_g[l].reshape(1, Q_LORA),
        "w_qb": w_qb.astype(BF16),
        "kvg": kv_a_norm_g[l].reshape(1, KV_LORA),
        "w_kb": w_kb.astype(BF16),
        "w_vb": w_vb.T.astype(BF16),
        "qg": pad_g(q_norm_g[l]),
        "kg": pad_g(k_norm_g[l]),
        "dft_c": jnp.concatenate([jnp.asarray(dft_c), -jnp.asarray(dft_ns)], axis=1).astype(BF16),
        "w_ao": w_attn_o[l].astype(BF16),
        "w_fn": w_fnet[l].astype(BF16),
        "w_out": w_out[l].astype(BF16),
        "g2": norm2_g[l].reshape(1, D_MODEL),
        "w_rg": w_rg,
        "b_rg": b_rg,
        "w_re": w_re,
        "b_re": b_re,
    }


def _layer(xp, xs, cache_ckv_l, cache_kpe_l, mod3, wts, experts):
    bp, sp, _ = xp.shape
    bs, ss, _ = xs.shape
    past = cache_ckv_l.shape[1]
    ctx_row = lambda i: 0
    lat_row = lambda i: 1 + i // (ss // TM)

    xp2 = xp.reshape(bp * sp, D_MODEL)
    ride = (bp * sp) // TM == N_EXPERTS
    cs, ns = (jnp.asarray(t).astype(BF16) for t in _dft_tables(sp))
    outs = _inproj(xp2, mod3, ctx_row, wts, None, True, experts if ride else (), (sp, cs, ns))
    attn, fm, sga, sgf, ckv, kpe = outs[:6]
    w_gate, w_up, w_down = outs[6:] if ride else (w.astype(BF16) for w in experts)
    x1, h2tm, grp, rank, cnt = _merge(xp2, attn, fm, sga, sgf, mod3, ctx_row, wts)
    yp = _sparse_moe(x1, h2tm, grp, rank, cnt, mod3, ctx_row, wts, w_gate, w_up, w_down,
                     MOE_TILE).reshape(bp, sp, D_MODEL)

    xs2 = xs.reshape(bs * ss, D_MODEL)
    rope_tabs = tuple(jnp.asarray(t) for t in _rope_tables(ss))
    q, k, v, fcs, sga, sgf = _inproj(xs2, mod3, lat_row, wts, rope_tabs, False)
    kpe_slot = jnp.pad(cache_kpe_l, ((0, 0), (0, 0), (QK_NOPE, LANES - QK_HEAD)))
    kc, vc = _cache_kv(cache_ckv_l.reshape(bs * past, KV_LORA), kpe_slot.reshape(bs * past, LANES), wts)
    attn = _attention(q.reshape(bs, ss, QK_WIDTH),
                      [kc.reshape(bs, past, QK_WIDTH), k.reshape(bs, ss, QK_WIDTH)],
                      [vc, v], TQ_LAT, "attn_lat")
    cs, ns = (jnp.asarray(t).astype(BF16) for t in _dft_tables(ss))
    fm = _fourier(fcs.reshape(bs, ss, 2 * FN_WIDTH), cs, ns, TQ_LAT, 1, "fourier_lat")
    x1, h2tm, grp, rank, cnt = _merge(xs2, attn.reshape(bs * ss, V_WIDTH),
                                      fm.reshape(bs * ss, FN_WIDTH), sga, sgf, mod3, lat_row, wts)
    ys = _sparse_moe(x1, h2tm, grp, rank, cnt, mod3, lat_row, wts, w_gate, w_up, w_down,
                     MOE_TILE).reshape(bs, ss, D_MODEL)

    return yp, ys, ckv.reshape(bp, sp, KV_LORA), kpe.reshape(bp, sp, QK_ROPE)


def kernel(x_prompt, x_sample, cache_ckv, cache_kpe, c, c_ctx, w_mod, b_mod, norm1_g, w_in, q_a_norm_g, w_q_b, kv_a_norm_g, w_kv_b, q_norm_g, k_norm_g, w_attn_o, w_fnet, w_out, norm2_g, w_router_group, b_router_group, w_router_expert, b_router_expert, w_exp_gate, w_exp_up, w_exp_down):
    depth = w_mod.shape[0]
    n_lat = c.shape[0]
    assert 1 + n_lat <= MOD_ROWS
    cond8 = jnp.concatenate([c_ctx[None, :], c, jnp.zeros((MOD_ROWS - 1 - n_lat, D_MODEL), F32)], axis=0)
    xp, xs = x_prompt, x_sample
    ckv_layers, kpe_layers = [], []
    for l in range(depth):
        mod, w_in_p = _adaln(cond8, w_mod[l], b_mod[l].reshape(1, -1), w_in[l])
        mod3 = mod.reshape(MOD_ROWS, 6, D_MODEL)
        wts = _pack_weights(l, norm1_g, q_a_norm_g, w_q_b, kv_a_norm_g, w_kv_b, q_norm_g,
                            k_norm_g, w_attn_o, w_fnet, w_out, norm2_g, w_router_group,
                            b_router_group, w_router_expert, b_router_expert)
        wts["w_in"] = w_in_p
        xp, xs, ckv, kpe = _layer(xp, xs, cache_ckv[:, l], cache_kpe[:, l], mod3, wts,
                                  (w_exp_gate[l], w_exp_up[l], w_exp_down[l]))
        ckv_layers.append(ckv)
        kpe_layers.append(kpe)
    return xp, xs, jnp.stack(ckv_layers, axis=1), jnp.stack(kpe_layers, axis=1)
```

```python
import functools
import math

import numpy as np
import jax
import jax.numpy as jnp
from jax import lax
from jax.experimental import pallas as pl
from jax.experimental.pallas import tpu as pltpu

D_MODEL = 1024
GRID_W = 64
N_HEADS = 8
Q_LORA = 512
KV_LORA = 256
QK_NOPE = 64
QK_ROPE = 32
V_HEAD = 64
QK_HEAD = QK_NOPE + QK_ROPE
ATTN_SCALE = QK_HEAD ** -0.5
ROPE_BASE = 10000.0
FN_GROUPS = 4
FN_GROUP_W = 128
FN_WIDTH = FN_GROUPS * FN_GROUP_W
N_GROUPS = 4
EXPERTS_PER_GROUP = 4
N_EXPERTS = N_GROUPS * EXPERTS_PER_GROUP
D_EXPERT = 512
EPS = 1e-6

LANES = 128
HEAD_SLOT = LANES
QK_WIDTH = N_HEADS * HEAD_SLOT
V_WIDTH = N_HEADS * V_HEAD
C_QA = 0
C_KVA = C_QA + Q_LORA
C_KPE = C_KVA + KV_LORA
C_FN = C_KPE + LANES
C_GA = C_FN + FN_WIDTH
C_GF = C_GA + D_MODEL
IN_PACKED = C_GF + D_MODEL
SUBLANES = 8
ROUTER_ROWS = SUBLANES
MOD_ROWS = SUBLANES
VMEM_LIMIT = 56 * 1024 * 1024

TM = 512
TQ_LAT = 256
SEG_ROWS = SUBLANES
TOKEN_ROWS = D_MODEL // LANES
MOE_TILE = 256
ADALN_STEPS = 8

BF16 = jnp.bfloat16
F32 = jnp.float32


def _cparams(sem):
    return pltpu.CompilerParams(dimension_semantics=sem, vmem_limit_bytes=VMEM_LIMIT)


def _dot(a, b):
    return jnp.dot(a, b, preferred_element_type=F32)


def _dot_nt(a, b):
    return lax.dot_general(a, b, (((1,), (1,)), ((), ())), preferred_element_type=F32)


def _sigmoid(x):
    return 1.0 / (1.0 + jnp.exp(-x))


def _split_bf16(x):
    hi = x.astype(BF16)
    return hi, (x - hi.astype(F32)).astype(BF16)


@functools.lru_cache(maxsize=None)
def _rope_tables(n_pos):
    half = QK_ROPE // 2
    quarter = half // 2
    freqs = ROPE_BASE ** (-np.arange(quarter, dtype=np.float64) / quarter)
    pos = np.arange(n_pos)
    row = (pos // GRID_W).astype(np.float64)
    col = (pos % GRID_W).astype(np.float64)
    cos_t = np.ones((n_pos, LANES), np.float64)
    sin_a = np.zeros((n_pos, LANES), np.float64)
    sin_b = np.zeros((n_pos, LANES), np.float64)
    for base, p in ((QK_NOPE, row), (QK_NOPE + half, col)):
        ang = p[:, None] * freqs[None, :]
        cos_t[:, base:base + quarter] = np.cos(ang)
        cos_t[:, base + quarter:base + half] = np.cos(ang)
        sin_a[:, base:base + quarter] = -np.sin(ang)
        sin_b[:, base + quarter:base + half] = np.sin(ang)
    return (cos_t.astype(np.float32), sin_a.astype(np.float32), sin_b.astype(np.float32))


@functools.lru_cache(maxsize=None)
def _dft_tables(n):
    k = np.arange(n)
    ang = 2.0 * np.pi * ((k[:, None] * k[None, :]) % n) / n
    s = 1.0 / math.sqrt(n)
    return (np.cos(ang) * s).astype(np.float32), (-np.sin(ang) * s).astype(np.float32)


def _adaln_kernel(cond_ref, w_ref, b_ref, win_ref, o_ref, winp_ref):
    c = cond_ref[...]
    s_hi, s_lo = _split_bf16(c * _sigmoid(c))
    w_hi, w_lo = _split_bf16(w_ref[...])
    y = _dot(jnp.concatenate([s_hi, s_lo], axis=0), w_hi)
    o_ref[...] = (y[:MOD_ROWS] + y[MOD_ROWS:]) + _dot(s_hi, w_lo) + b_ref[...]

    winp_ref[:C_KPE, :] = win_ref[:C_KPE, :].astype(BF16)
    winp_ref[C_KPE:C_FN, :] = jnp.zeros((LANES, winp_ref.shape[1]), BF16)
    winp_ref[C_KPE + QK_NOPE:C_KPE + QK_HEAD, :] = win_ref[C_KPE:C_KPE + QK_ROPE, :].astype(BF16)
    winp_ref[C_FN:, :] = win_ref[C_KPE + QK_ROPE:, :].astype(BF16)


def _adaln(cond8, w_mod, b_mod, w_in_t):
    n = w_mod.shape[1]
    steps = ADALN_STEPS
    tn, tr = n // steps, D_MODEL // steps
    return pl.pallas_call(
        _adaln_kernel,
        grid=(steps,),
        in_specs=[pl.BlockSpec((MOD_ROWS, D_MODEL), lambda j: (0, 0)),
                  pl.BlockSpec((D_MODEL, tn), lambda j: (0, j)),
                  pl.BlockSpec((1, tn), lambda j: (0, j)),
                  pl.BlockSpec((w_in_t.shape[0], tr), lambda j: (0, j))],
        out_specs=[pl.BlockSpec((MOD_ROWS, tn), lambda j: (0, j)),
                   pl.BlockSpec((IN_PACKED, tr), lambda j: (0, j))],
        out_shape=[jax.ShapeDtypeStruct((MOD_ROWS, n), F32),
                   jax.ShapeDtypeStruct((IN_PACKED, D_MODEL), BF16)],
        compiler_params=_cparams(("arbitrary",)),
        name="adaln",
    )(cond8, w_mod, b_mod, w_in_t)


def _rms(x, width):
    return lax.rsqrt(jnp.sum(x * x, axis=-1, keepdims=True) * (1.0 / width) + EPS)


def _rope(x, cos_t, sin_a, sin_b):
    return x * cos_t + pltpu.roll(x, LANES - 8, 1) * sin_a + pltpu.roll(x, 8, 1) * sin_b


def _inproj_kernel(*refs, rope, emit_cache, n_cast, seq_len):
    it = iter(refs)
    x_ref, mod_ref, g1_ref, win_ref, qag_ref, wqb_ref, kvg_ref, wkb_ref, wvb_ref = (
        next(it) for _ in range(9))
    qg_ref, kg_ref, dft_ref = next(it), next(it), next(it)
    if rope:
        cos_ref, sa_ref, sb_ref = next(it), next(it), next(it)
    if seq_len:
        seq_cs_ref, seq_ns_ref = next(it), next(it)
    cast_in = [next(it) for _ in range(n_cast)]
    if seq_len:
        attn_ref, fm_ref, sga_ref, sgf_ref = (next(it) for _ in range(4))
    else:
        q_ref, k_ref, v_ref, fcs_ref, sga_ref, sgf_ref = (next(it) for _ in range(6))
    if emit_cache:
        ckv_ref, kpe_ref = next(it), next(it)
    cast_out = [next(it) for _ in range(n_cast)]
    if seq_len:
        q_ref, k_ref, v_ref, fcs_ref = (next(it) for _ in range(4))

    for src, dst in zip(cast_in, cast_out):
        dst[...] = src[...].astype(BF16)

    x = x_ref[...]
    shift = mod_ref[0, 0:1, :]
    scale = mod_ref[0, 1:2, :]
    h = (x * _rms(x, D_MODEL) * g1_ref[...]) * (1.0 + scale) + shift
    hb = h.astype(BF16)

    if rope:
        cos_t, sin_a, sin_b = cos_ref[...], sa_ref[...], sb_ref[...]

    qa = _dot_nt(hb, win_ref[C_QA:C_QA + Q_LORA, :])
    qn = (qa * _rms(qa, Q_LORA) * qag_ref[...]).astype(BF16)
    q = _dot(qn, wqb_ref[...])
    qg = qg_ref[...] * ATTN_SCALE
    for hd in range(N_HEADS):
        qh = q[:, hd * HEAD_SLOT:(hd + 1) * HEAD_SLOT]
        qh = qh * _rms(qh, QK_HEAD) * qg
        if rope:
            qh = _rope(qh, cos_t, sin_a, sin_b)
        q_ref[:, hd * HEAD_SLOT:(hd + 1) * HEAD_SLOT] = qh.astype(BF16)

    kva = _dot_nt(hb, win_ref[C_KVA:C_KVA + KV_LORA, :])
    ckv = kva * _rms(kva, KV_LORA) * kvg_ref[...]
    kpe = _dot_nt(hb, win_ref[C_KPE:C_KPE + LANES, :])
    if emit_cache:
        ckv_ref[...] = ckv
        kpe_t = kpe.T[QK_NOPE:QK_NOPE + QK_ROPE, :]
        for j in range(x.shape[0] // seq_len):
            kpe_ref[j] = kpe_t[:, j * seq_len:(j + 1) * seq_len]
    _emit_kv(ckv.astype(BF16), kpe, wkb_ref, wvb_ref, kg_ref,
             (cos_t, sin_a, sin_b) if rope else None, k_ref, v_ref)

    fn = _dot_nt(hb, win_ref[C_FN:C_FN + FN_WIDTH, :]).astype(BF16)
    for g in range(FN_GROUPS):
        cs = _dot(fn[:, g * FN_GROUP_W:(g + 1) * FN_GROUP_W], dft_ref[...])
        fcs_ref[:, g * FN_GROUP_W:(g + 1) * FN_GROUP_W] = cs[:, :FN_GROUP_W].astype(BF16)
        fcs_ref[:, FN_WIDTH + g * FN_GROUP_W:FN_WIDTH + (g + 1) * FN_GROUP_W] = (
            cs[:, FN_GROUP_W:].astype(BF16))

    sga_ref[...] = _sigmoid(_dot_nt(hb, win_ref[C_GA:C_GA + D_MODEL, :])).astype(BF16)
    sgf_ref[...] = _sigmoid(_dot_nt(hb, win_ref[C_GF:C_GF + D_MODEL, :])).astype(BF16)

    if seq_len:
        for s0 in range(0, x.shape[0], seq_len):
            rows = pl.ds(s0, seq_len)
            attn_ref[rows, :] = _attend(q_ref.at[rows], [k_ref.at[rows]], [v_ref.at[:, rows]])
            fm_ref[rows, :] = (_dot(seq_cs_ref[...], fcs_ref[rows, :FN_WIDTH])
                               + _dot(seq_ns_ref[...], fcs_ref[rows, FN_WIDTH:])).astype(BF16)


def _emit_kv(ckvb, kpe, wkb_ref, wvb_ref, kg_ref, rope_tabs, k_ref, v_ref):
    kg = kg_ref[...]
    v_ref[...] = _dot_nt(wvb_ref[...], ckvb).astype(BF16)
    kn = _dot(ckvb, wkb_ref[...])
    pe_ss = jnp.sum(kpe * kpe, axis=-1, keepdims=True)
    pe_g = kpe * kg
    if rope_tabs is not None:
        pe_g = _rope(pe_g, *rope_tabs)
    for hd in range(N_HEADS):
        knh = kn[:, hd * HEAD_SLOT:(hd + 1) * HEAD_SLOT]
        ss = jnp.sum(knh * knh, axis=-1, keepdims=True) + pe_ss
        r = lax.rsqrt(ss * (1.0 / QK_HEAD) + EPS)
        k_ref[:, hd * HEAD_SLOT:(hd + 1) * HEAD_SLOT] = ((knh * kg + pe_g) * r).astype(BF16)


def _const_spec(shape):
    return pl.BlockSpec(shape, lambda i: (0,) * len(shape))


def _inproj(x2d, mod3, mod_row_fn, wts, rope_tabs, emit_cache, cast=(), seq=None):
    n = x2d.shape[0]
    assert all(w.shape[0] == n // TM for w in cast)
    assert seq is None or TM % seq[0] == 0
    rope = rope_tabs is not None
    tiles_per_seq = None if not rope else rope_tabs[0].shape[0] // TM
    in_specs = [pl.BlockSpec((TM, D_MODEL), lambda i: (i, 0)),
                pl.BlockSpec((1, 6, D_MODEL), lambda i: (mod_row_fn(i), 0, 0)),
                _const_spec((1, D_MODEL)),
                _const_spec((IN_PACKED, D_MODEL)),
                _const_spec((1, Q_LORA)),
                _const_spec((Q_LORA, QK_WIDTH)),
                _const_spec((1, KV_LORA)),
                _const_spec((KV_LORA, QK_WIDTH)),
                _const_spec((V_WIDTH, KV_LORA)),
                _const_spec((1, HEAD_SLOT)),
                _const_spec((1, HEAD_SLOT)),
                _const_spec((FN_GROUP_W, 2 * FN_GROUP_W))]
    args = [x2d, mod3, wts["g1"], wts["w_in"], wts["qag"], wts["w_qb"], wts["kvg"],
            wts["w_kb"], wts["w_vb"], wts["qg"], wts["kg"], wts["dft_c"]]
    if rope:
        in_specs += [pl.BlockSpec((TM, LANES), lambda i: (i % tiles_per_seq, 0))] * 3
        args += list(rope_tabs)
    if seq is not None:
        in_specs += [_const_spec((seq[0], seq[0]))] * 2
        args += [seq[1], seq[2]]
    cast_specs = [pl.BlockSpec((1,) + w.shape[1:], lambda i: (i, 0, 0)) for w in cast]
    in_specs += cast_specs
    args += list(cast)
    qkvf = [((TM, QK_WIDTH), (n, QK_WIDTH)), ((TM, QK_WIDTH), (n, QK_WIDTH)),
            ((V_WIDTH, TM), (V_WIDTH, n)), ((TM, 2 * FN_WIDTH), (n, 2 * FN_WIDTH))]
    if seq is None:
        widths = [None] * 4 + [D_MODEL, D_MODEL]
    else:
        widths = [V_WIDTH, FN_WIDTH, D_MODEL, D_MODEL]
    out_shape, out_specs = [], []
    for j, w in enumerate(widths):
        if w is None:
            out_shape.append(jax.ShapeDtypeStruct(qkvf[j][1], BF16))
            out_specs.append(pl.BlockSpec(qkvf[j][0], (lambda i: (0, i)) if j == 2 else (lambda i: (i, 0))))
        else:
            out_shape.append(jax.ShapeDtypeStruct((n, w), BF16))
            out_specs.append(pl.BlockSpec((TM, w), lambda i: (i, 0)))
    if emit_cache:
        assert seq is not None
        out_shape += [jax.ShapeDtypeStruct((n, KV_LORA), F32),
                      jax.ShapeDtypeStruct((n // seq[0], QK_ROPE, seq[0]), F32)]
        out_specs += [pl.BlockSpec((TM, KV_LORA), lambda i: (i, 0)),
                      pl.BlockSpec((TM // seq[0], QK_ROPE, seq[0]), lambda i: (i, 0, 0))]
    out_shape += [jax.ShapeDtypeStruct(w.shape, BF16) for w in cast]
    out_specs += cast_specs
    scratch = [] if seq is None else [pltpu.VMEM(blk, BF16) for blk, _ in qkvf]
    return pl.pallas_call(
        functools.partial(_inproj_kernel, rope=rope, emit_cache=emit_cache, n_cast=len(cast),
                          seq_len=None if seq is None else seq[0]),
        grid=(n // TM,),
        in_specs=in_specs,
        out_specs=out_specs,
        out_shape=out_shape,
        scratch_shapes=scratch,
        compiler_params=_cparams(("parallel",)),
        name="inproj_lat" if rope else "inproj_ctx",
    )(*args)


def _cache_kv_kernel(ckv_ref, kpe_ref, wkb_ref, wvb_ref, kg_ref, k_ref, v_ref):
    _emit_kv(ckv_ref[...].astype(BF16), kpe_ref[...], wkb_ref, wvb_ref, kg_ref, None, k_ref, v_ref)


def _cache_kv(ckv2d, kpe_slot2d, wts):
    n = ckv2d.shape[0]
    return pl.pallas_call(
        _cache_kv_kernel,
        grid=(n // TM,),
        in_specs=[pl.BlockSpec((TM, KV_LORA), lambda i: (i, 0)),
                  pl.BlockSpec((TM, LANES), lambda i: (i, 0)),
                  _const_spec((KV_LORA, QK_WIDTH)),
                  _const_spec((V_WIDTH, KV_LORA)),
                  _const_spec((1, HEAD_SLOT))],
        out_specs=[pl.BlockSpec((TM, QK_WIDTH), lambda i: (i, 0)),
                   pl.BlockSpec((V_WIDTH, TM), lambda i: (0, i))],
        out_shape=[jax.ShapeDtypeStruct((n, QK_WIDTH), BF16),
                   jax.ShapeDtypeStruct((V_WIDTH, n), BF16)],
        compiler_params=_cparams(("parallel",)),
        name="cache_kv",
    )(ckv2d, kpe_slot2d, wts["w_kb"], wts["w_vb"], wts["kg"])


def _attend(q, ks, vts):
    head = lambda hd: slice(hd * HEAD_SLOT, (hd + 1) * HEAD_SLOT)
    st = [jnp.stack([_dot_nt(k[:, head(hd)], q[:, head(hd)]) for hd in range(N_HEADS)]) for k in ks]
    m = functools.reduce(jnp.maximum, [sj.max(axis=1, keepdims=True) for sj in st])
    p = [jnp.exp(sj - m) for sj in st]
    l = functools.reduce(lambda a, b: a + b, [pj.sum(axis=1, keepdims=True) for pj in p])
    outs = []
    for hd in range(N_HEADS):
        o = functools.reduce(lambda a, b: a + b,
                             [_dot(vt[hd * V_HEAD:(hd + 1) * V_HEAD, :], pj[hd].astype(BF16))
                              for vt, pj in zip(vts, p)])
        outs.append(o / l[hd])
    return jnp.concatenate(outs, axis=0).T.astype(BF16)


def _attn_kernel(*refs, n_kv):
    q_ref = refs[0]
    k_refs = refs[1:1 + n_kv]
    vt_refs = refs[1 + n_kv:1 + 2 * n_kv]
    o_ref = refs[1 + 2 * n_kv]
    o_ref[0] = _attend(q_ref.at[0], [k.at[0] for k in k_refs], vt_refs)


def _attention(q3, ks, vts, tq, name):
    b, sq, _ = q3.shape
    n_kv = len(ks)
    in_specs = [pl.BlockSpec((1, tq, QK_WIDTH), lambda bi, qi: (bi, qi, 0))]
    in_specs += [pl.BlockSpec((1, k.shape[1], QK_WIDTH), lambda bi, qi: (bi, 0, 0)) for k in ks]
    in_specs += [pl.BlockSpec((V_WIDTH, k.shape[1]), lambda bi, qi: (0, bi)) for k in ks]
    return pl.pallas_call(
        functools.partial(_attn_kernel, n_kv=n_kv),
        grid=(b, sq // tq),
        in_specs=in_specs,
        out_specs=pl.BlockSpec((1, tq, V_WIDTH), lambda bi, qi: (bi, qi, 0)),
        out_shape=jax.ShapeDtypeStruct((b, sq, V_WIDTH), BF16),
        compiler_params=_cparams(("parallel", "parallel")),
        name=name,
    )(q3, *ks, *vts)


def _fourier_kernel(cs_ref, ns_ref, f_ref, o_ref, *, n_b):
    for b in range(n_b):
        xc = f_ref[b, :, :FN_WIDTH]
        xs = f_ref[b, :, FN_WIDTH:]
        o_ref[b] = (_dot(cs_ref[...], xc) + _dot(ns_ref[...], xs)).astype(BF16)


def _fourier(fcs3, cs, ns, tr, n_b, name):
    b, s, _ = fcs3.shape
    return pl.pallas_call(
        functools.partial(_fourier_kernel, n_b=n_b),
        grid=(b // n_b, s // tr),
        in_specs=[pl.BlockSpec((tr, s), lambda bi, ri: (ri, 0)),
                  pl.BlockSpec((tr, s), lambda bi, ri: (ri, 0)),
                  pl.BlockSpec((n_b, s, 2 * FN_WIDTH), lambda bi, ri: (bi, 0, 0))],
        out_specs=pl.BlockSpec((n_b, tr, FN_WIDTH), lambda bi, ri: (bi, ri, 0)),
        out_shape=jax.ShapeDtypeStruct((b, s, FN_WIDTH), BF16),
        compiler_params=_cparams(("parallel", "parallel")),
        name=name,
    )(cs, ns, fcs3)


def _router_logits(w, b, h_hi, h_lo):
    rows = w.shape[0]
    w_hi, w_lo = _split_bf16(w)
    y = _dot_nt(jnp.concatenate([w_hi, w_lo], axis=0), h_hi)
    return (y[:rows] + y[rows:]) + _dot_nt(w_hi, h_lo) + b


def _rows(x, n):
    return [x[j:j + 1, :] for j in range(n)]


def _first_argmax(rows, top):
    idx = jnp.full(top.shape, len(rows) - 1, jnp.int32)
    for j in range(len(rows) - 2, -1, -1):
        idx = jnp.where(rows[j] == top, j, idx)
    return idx


def _store_token_major(ref, x, tm):
    for s in range(TOKEN_ROWS):
        ref[pl.ds(s, tm, stride=TOKEN_ROWS), :] = x[:, s * LANES:(s + 1) * LANES]


def _load_token_major(ref, tm):
    return jnp.concatenate([ref[pl.ds(s, tm, stride=TOKEN_ROWS), :] for s in range(TOKEN_ROWS)],
                           axis=1)


def _merge_kernel(x_ref, attn_ref, fm_ref, sga_ref, sgf_ref, mod_ref, wao_ref, wfn_ref, wout_ref,
                  g2_ref, wrg_ref, brg_ref, x1_ref, h2_ref, grp_ref, rank_ref, cnt_ref, carry_ref):
    i = pl.program_id(0)

    @pl.when(i == 0)
    def _():
        carry_ref[...] = jnp.zeros_like(carry_ref)

    a = _dot(attn_ref[...], wao_ref[...])
    f = _dot(fm_ref[...], wfn_ref[...])
    u = sga_ref[...].astype(F32) * a + sgf_ref[...].astype(F32) * f
    y = _dot(u.astype(BF16), wout_ref[...])
    x1 = x_ref[...] + mod_ref[0, 2:3, :] * y
    x1_ref[...] = x1
    tm = x1.shape[0]
    h2 = (x1 * _rms(x1, D_MODEL) * g2_ref[...]) * (1.0 + mod_ref[0, 4:5, :]) + mod_ref[0, 3:4, :]
    _store_token_major(h2_ref, h2, tm)

    h2_hi, h2_lo = _split_bf16(h2)
    g = _rows(_router_logits(wrg_ref[...], brg_ref[...], h2_hi, h2_lo), N_GROUPS)
    gidx = _first_argmax(g, functools.reduce(jnp.maximum, g))

    onehot = jnp.where(lax.broadcasted_iota(jnp.int32, (SEG_ROWS, tm), 0) == gidx, 1.0, 0.0)
    before = (lax.broadcasted_iota(jnp.int32, (tm, tm), 0)
              < lax.broadcasted_iota(jnp.int32, (tm, tm), 1))
    prefix = _dot(onehot.astype(BF16), jnp.where(before, 1.0, 0.0).astype(BF16))
    carry = carry_ref[...]
    rank = jnp.sum(onehot * (prefix + carry[:, 0:1]), axis=0, keepdims=True)
    grp_ref[...] = gidx
    rank_ref[...] = rank.astype(jnp.int32)
    carry = carry + jnp.sum(onehot, axis=1, keepdims=True)
    carry_ref[...] = carry
    cnt_ref[...] = carry.astype(jnp.int32)


def _merge(x2d, attn2d, fm2d, sga, sgf, mod3, mod_row_fn, wts):
    n = x2d.shape[0]
    tok = lambda w: pl.BlockSpec((TM, w), lambda i: (i, 0))
    return pl.pallas_call(
        _merge_kernel,
        grid=(n // TM,),
        in_specs=[tok(D_MODEL), tok(V_WIDTH), tok(FN_WIDTH), tok(D_MODEL), tok(D_MODEL),
                  pl.BlockSpec((1, 6, D_MODEL), lambda i: (mod_row_fn(i), 0, 0)),
                  _const_spec((V_WIDTH, D_MODEL)),
                  _const_spec((FN_WIDTH, D_MODEL)),
                  _const_spec((D_MODEL, D_MODEL)),
                  _const_spec((1, D_MODEL)),
                  _const_spec((ROUTER_ROWS, D_MODEL)),
                  _const_spec((ROUTER_ROWS, 1))],
        out_specs=[tok(D_MODEL),
                   pl.BlockSpec((TM * TOKEN_ROWS, LANES), lambda i: (i, 0)),
                   pl.BlockSpec((1, TM), lambda i: (0, i)),
                   pl.BlockSpec((1, TM), lambda i: (0, i)), _const_spec((SEG_ROWS, LANES))],
        out_shape=[jax.ShapeDtypeStruct((n, D_MODEL), F32),
                   jax.ShapeDtypeStruct((n * TOKEN_ROWS, LANES), F32),
                   jax.ShapeDtypeStruct((1, n), jnp.int32),
                   jax.ShapeDtypeStruct((1, n), jnp.int32),
                   jax.ShapeDtypeStruct((SEG_ROWS, LANES), jnp.int32)],
        scratch_shapes=[pltpu.VMEM((SEG_ROWS, LANES), F32)],
        compiler_params=_cparams(("arbitrary",)),
        name="merge",
    )(x2d, attn2d, fm2d, sga, sgf, mod3, wts["w_ao"], wts["w_fn"], wts["w_out"], wts["g2"],
      wts["w_rg"], wts["b_rg"])


def _token_rows(ref, t):
    start = t * TOKEN_ROWS
    if not isinstance(t, int):
        start = pl.multiple_of(start, TOKEN_ROWS)
    return ref.at[pl.ds(start, TOKEN_ROWS)]


def _row_copy(src, dst, src_tok, dst_tok, sem):
    return pltpu.make_async_copy(_token_rows(src, src_tok), _token_rows(dst, dst_tok), sem)


def _wait_tile(src_hbm, buf, sem):
    pltpu.make_async_copy(src_hbm.at[pl.ds(0, buf.shape[0])], buf, sem).wait()


def _combine_kernel(pos_ref, y_hbm, x1_ref, mod_ref, o_ref, buf, sem):
    i = pl.program_id(0)
    tm = o_ref.shape[0]

    def gather(tile_idx, slot):
        def start(r, c):
            _row_copy(y_hbm, buf.at[slot], pos_ref[tile_idx * tm + r], r, sem.at[slot]).start()
            return c

        lax.fori_loop(0, tm, start, 0, unroll=32)

    @pl.when(i == 0)
    def _():
        gather(0, 0)

    @pl.when(i + 1 < pl.num_programs(0))
    def _():
        gather(i + 1, (i + 1) % 2)

    slot = i % 2

    _wait_tile(y_hbm, buf.at[slot], sem.at[slot])
    o_ref[...] = x1_ref[...] + mod_ref[0, 5:6, :] * _load_token_major(buf.at[slot], tm)


def _combine(y_sorted, pos, x1, mod3, mod_row_fn):
    n = x1.shape[0]
    return pl.pallas_call(
        _combine_kernel,
        grid_spec=pltpu.PrefetchScalarGridSpec(
            num_scalar_prefetch=1,
            grid=(n // TM,),
            in_specs=[pl.BlockSpec(memory_space=pl.ANY),
                      pl.BlockSpec((TM, D_MODEL), lambda i, pos: (i, 0)),
                      pl.BlockSpec((1, 6, D_MODEL), lambda i, pos: (mod_row_fn(i), 0, 0))],
            out_specs=pl.BlockSpec((TM, D_MODEL), lambda i, pos: (i, 0)),
            scratch_shapes=[pltpu.VMEM((2, TM * TOKEN_ROWS, LANES), F32),
                            pltpu.SemaphoreType.DMA((2,))]),
        out_shape=jax.ShapeDtypeStruct((n, D_MODEL), F32),
        compiler_params=_cparams(("arbitrary",)),
        name="moe_combine",
    )(pos, y_sorted, x1, mod3)


def _moe_kernel(grp_ref, on_ref, src_ref, h_hbm, wrg_ref, brg_ref, wre_ref, bre_ref, wg_ref, wu_ref,
                wd_ref, o_ref, hbuf, sem, *, tile):
    i = pl.program_id(0)

    def gather(tile_idx, slot):
        def start(r, c):
            _row_copy(h_hbm, hbuf.at[slot], src_ref[tile_idx * tile + r], r, sem.at[slot]).start()
            return c

        lax.fori_loop(0, tile, start, 0, unroll=32)

    @pl.when(jnp.logical_and(i == 0, on_ref[0] == 1))
    def _():
        gather(0, 0)

    nxt = jnp.minimum(i + 1, pl.num_programs(0) - 1)

    @pl.when(jnp.logical_and(i + 1 < pl.num_programs(0), on_ref[nxt] == 1))
    def _():
        gather(i + 1, (i + 1) % 2)

    @pl.when(on_ref[i] == 0)
    def _():
        o_ref[...] = jnp.zeros_like(o_ref)

    @pl.when(on_ref[i] == 1)
    def _():
        slot = i % 2

        _wait_tile(h_hbm, hbuf.at[slot], sem.at[slot])
        h2_hi, h2_lo = _split_bf16(_load_token_major(hbuf.at[slot], tile))

        logits = _router_logits(jnp.concatenate([wrg_ref[...], wre_ref[0]], axis=0),
                                jnp.concatenate([brg_ref[...], bre_ref[0]], axis=0), h2_hi, h2_lo)
        g = _rows(logits, N_GROUPS)
        gmax = functools.reduce(jnp.maximum, g)
        p_top = 1.0 / functools.reduce(lambda p, q: p + q, [jnp.exp(gj - gmax) for gj in g])
        e = _rows(logits[ROUTER_ROWS:], EXPERTS_PER_GROUP)
        m1 = functools.reduce(jnp.maximum, e)
        i1 = _first_argmax(e, m1)
        rest = [jnp.where(i1 == j, -jnp.inf, e[j]) for j in range(EXPERTS_PER_GROUP)]
        m2 = functools.reduce(jnp.maximum, rest)
        i2 = _first_argmax(rest, m2)
        t = jnp.exp(m2 - m1)
        w1 = p_top / (1.0 + t)
        w2 = p_top * t / (1.0 + t)
        row = lax.broadcasted_iota(jnp.int32, (LANES, tile), 0)
        comb = (jnp.where(row == i1, w1, 0.0) + jnp.where(row == i2, w2, 0.0)).T

        acc = None
        for j in range(EXPERTS_PER_GROUP):
            a = _dot(h2_hi, wg_ref[j])
            u = _dot(h2_hi, wu_ref[j])
            act = (a * _sigmoid(a)) * u * comb[:, j:j + 1]
            y = _dot(act.astype(BF16), wd_ref[j])
            acc = y if acc is None else acc + y
        _store_token_major(o_ref, acc, tile)


def _moe(h2tm, src, maps, wts, w_gate, w_up, w_down, tile):
    n_tiles = src.shape[0] // tile
    const = lambda shape: pl.BlockSpec(shape, lambda i, grp, on, src: (0,) * len(shape))
    by_group = lambda shape: pl.BlockSpec(
        shape, lambda i, grp, on, src: (grp[i],) + (0,) * (len(shape) - 1))
    return pl.pallas_call(
        functools.partial(_moe_kernel, tile=tile),
        grid_spec=pltpu.PrefetchScalarGridSpec(
            num_scalar_prefetch=3,
            grid=(n_tiles,),
            in_specs=[pl.BlockSpec(memory_space=pl.ANY),
                      const((ROUTER_ROWS, D_MODEL)), const((ROUTER_ROWS, 1)),
                      by_group((1, ROUTER_ROWS, D_MODEL)), by_group((1, ROUTER_ROWS, 1)),
                      by_group((EXPERTS_PER_GROUP, D_MODEL, D_EXPERT)),
                      by_group((EXPERTS_PER_GROUP, D_MODEL, D_EXPERT)),
                      by_group((EXPERTS_PER_GROUP, D_EXPERT, D_MODEL))],
            out_specs=pl.BlockSpec((tile * TOKEN_ROWS, LANES), lambda i, grp, on, src: (i, 0)),
            scratch_shapes=[pltpu.VMEM((2, tile * TOKEN_ROWS, LANES), F32),
                            pltpu.SemaphoreType.DMA((2,))]),
        out_shape=jax.ShapeDtypeStruct((n_tiles * tile * TOKEN_ROWS, LANES), F32),
        compiler_params=_cparams(("arbitrary",)),
        name="moe",
    )(*maps, src, h2tm, wts["w_rg"], wts["b_rg"], wts["w_re"], wts["b_re"], w_gate, w_up, w_down)


def _invert_kernel(pos_ref, lo_ref, hi_ref, src_ref, *, n, n_fill):
    def zero(p, c):
        src_ref[p] = 0
        return c

    for s in range(n_fill):
        lax.fori_loop(lo_ref[s], hi_ref[s], zero, 0)

    def put(t, c):
        src_ref[pos_ref[t]] = t
        return c

    lax.fori_loop(0, n, put, 0, unroll=8)


def _invert(pos, fill_lo, fill_hi, n_sorted):
    return pl.pallas_call(
        functools.partial(_invert_kernel, n=pos.shape[0], n_fill=fill_lo.shape[0]),
        grid_spec=pltpu.PrefetchScalarGridSpec(
            num_scalar_prefetch=3, grid=(1,), in_specs=[],
            out_specs=pl.BlockSpec(memory_space=pltpu.SMEM)),
        out_shape=jax.ShapeDtypeStruct((n_sorted,), jnp.int32),
        name="moe_invert",
    )(pos, fill_lo, fill_hi)


def _sort_plan(grp, rank, cnt, tile, n_tiles):
    n_tile_grp = (cnt + tile - 1) // tile
    tile_end = jnp.cumsum(n_tile_grp)
    tile_start = tile_end - n_tile_grp
    total = tile_end[-1]
    pos = ((tile_start * tile)[grp] + rank).astype(jnp.int32)
    fill_lo = jnp.concatenate([tile_start * tile + cnt, total[None] * tile]).astype(jnp.int32)
    fill_hi = jnp.concatenate([tile_end * tile, jnp.full((1,), n_tiles * tile)]).astype(jnp.int32)
    src = _invert(pos, fill_lo, fill_hi, n_tiles * tile)
    i = jnp.arange(n_tiles, dtype=jnp.int32)
    tile_grp = jnp.sum((jnp.minimum(i, total - 1)[:, None] >= tile_end[None, :]).astype(jnp.int32),
                       axis=1)
    return pos, src, (tile_grp.astype(jnp.int32), (i < total).astype(jnp.int32))


def _sparse_moe(x1, h2tm, grp, rank, cnt8, mod3, mod_row_fn, wts, w_gate, w_up, w_down, tile):
    n = x1.shape[0]
    n_tiles = n // tile + N_GROUPS
    pos, src, maps = _sort_plan(grp.reshape(n), rank.reshape(n), cnt8[:N_GROUPS, 0], tile, n_tiles)
    y_sorted = _moe(h2tm, src, maps, wts, w_gate, w_up, w_down, tile)
    return _combine(y_sorted, pos, x1, mod3, mod_row_fn)


def _pack_weights(l, norm1_g, q_a_norm_g, w_q_b, kv_a_norm_g, w_kv_b, q_norm_g, k_norm_g,
                  w_attn_o, w_fnet, w_out, norm2_g, w_router_group, b_router_group,
                  w_router_expert, b_router_expert):
    w_qb = jnp.pad(w_q_b[l].reshape(Q_LORA, N_HEADS, QK_HEAD),
                   ((0, 0), (0, 0), (0, HEAD_SLOT - QK_HEAD))).reshape(Q_LORA, QK_WIDTH)
    wkv = w_kv_b[l].reshape(KV_LORA, N_HEADS, QK_NOPE + V_HEAD)
    w_kb = jnp.pad(wkv[:, :, :QK_NOPE],
                   ((0, 0), (0, 0), (0, HEAD_SLOT - QK_NOPE))).reshape(KV_LORA, QK_WIDTH)
    w_vb = wkv[:, :, QK_NOPE:].reshape(KV_LORA, V_WIDTH)
    pad_g = lambda g: jnp.pad(g, (0, HEAD_SLOT - QK_HEAD)).reshape(1, HEAD_SLOT)
    w_rg = jnp.pad(w_router_group[l].T, ((0, ROUTER_ROWS - N_GROUPS), (0, 0)))
    b_rg = jnp.pad(b_router_group[l], (0, ROUTER_ROWS - N_GROUPS)).reshape(ROUTER_ROWS, 1)
    w_re = jnp.pad(w_router_expert[l].T.reshape(N_GROUPS, EXPERTS_PER_GROUP, D_MODEL),
                   ((0, 0), (0, ROUTER_ROWS - EXPERTS_PER_GROUP), (0, 0)))
    b_re = jnp.pad(b_router_expert[l].reshape(N_GROUPS, EXPERTS_PER_GROUP),
                   ((0, 0), (0, ROUTER_ROWS - EXPERTS_PER_GROUP))).reshape(N_GROUPS, ROUTER_ROWS, 1)
    dft_c, dft_ns = _dft_tables(FN_GROUP_W)
    return {
        "g1": norm1_g[l].reshape(1, D_MODEL),
        "qag": q_a_norm_g[l].reshape(1, Q_LORA),
        "w_qb": w_qb.astype(BF16),
        "kvg": kv_a_norm_g[l].reshape(1, KV_LORA),
        "w_kb": w_kb.astype(BF16),
        "w_vb": w_vb.T.astype(BF16),
        "qg": pad_g(q_norm_g[l]),
        "kg": pad_g(k_norm_g[l]),
        "dft_c": jnp.concatenate([jnp.asarray(dft_c), -jnp.asarray(dft_ns)], axis=1).astype(BF16),
        "w_ao": w_attn_o[l].astype(BF16),
        "w_fn": w_fnet[l].astype(BF16),
        "w_out": w_out[l].astype(BF16),
        "g2": norm2_g[l].reshape(1, D_MODEL),
        "w_rg": w_rg,
        "b_rg": b_rg,
        "w_re": w_re,
        "b_re": b_re,
    }


def _layer(xp, xs, cache_ckv_l, cache_kpe_l, mod3, wts, experts):
    bp, sp, _ = xp.shape
    bs, ss, _ = xs.shape
    past = cache_ckv_l.shape[1]
    ctx_row = lambda i: 0
    lat_row = lambda i: 1 + i // (ss // TM)

    xp2 = xp.reshape(bp * sp, D_MODEL)
    ride = (bp * sp) // TM == N_EXPERTS
    cs, ns = (jnp.asarray(t).astype(BF16) for t in _dft_tables(sp))
    outs = _inproj(xp2, mod3, ctx_row, wts, None, True, experts if ride else (), (sp, cs, ns))
    attn, fm, sga, sgf, ckv, kpe = outs[:6]
    w_gate, w_up, w_down = outs[6:] if ride else (w.astype(BF16) for w in experts)
    x1, h2tm, grp, rank, cnt = _merge(xp2, attn, fm, sga, sgf, mod3, ctx_row, wts)
    yp = _sparse_moe(x1, h2tm, grp, rank, cnt, mod3, ctx_row, wts, w_gate, w_up, w_down,
                     MOE_TILE).reshape(bp, sp, D_MODEL)

    xs2 = xs.reshape(bs * ss, D_MODEL)
    rope_tabs = tuple(jnp.asarray(t) for t in _rope_tables(ss))
    q, k, v, fcs, sga, sgf = _inproj(xs2, mod3, lat_row, wts, rope_tabs, False)
    kpe_slot = jnp.pad(cache_kpe_l, ((0, 0), (0, 0), (QK_NOPE, LANES - QK_HEAD)))
    kc, vc = _cache_kv(cache_ckv_l.reshape(bs * past, KV_LORA), kpe_slot.reshape(bs * past, LANES), wts)
    attn = _attention(q.reshape(bs, ss, QK_WIDTH),
                      [kc.reshape(bs, past, QK_WIDTH), k.reshape(bs, ss, QK_WIDTH)],
                      [vc, v], TQ_LAT, "attn_lat")
    cs, ns = (jnp.asarray(t).astype(BF16) for t in _dft_tables(ss))
    fm = _fourier(fcs.reshape(bs, ss, 2 * FN_WIDTH), cs, ns, TQ_LAT, 1, "fourier_lat")
    x1, h2tm, grp, rank, cnt = _merge(xs2, attn.reshape(bs * ss, V_WIDTH),
                                      fm.reshape(bs * ss, FN_WIDTH), sga, sgf, mod3, lat_row, wts)
    ys = _sparse_moe(x1, h2tm, grp, rank, cnt, mod3, lat_row, wts, w_gate, w_up, w_down,
                     MOE_TILE).reshape(bs, ss, D_MODEL)

    return yp, ys, ckv.reshape(bp, sp, KV_LORA), jnp.swapaxes(kpe, 1, 2)


def kernel(x_prompt, x_sample, cache_ckv, cache_kpe, c, c_ctx, w_mod, b_mod, norm1_g, w_in, q_a_norm_g, w_q_b, kv_a_norm_g, w_kv_b, q_norm_g, k_norm_g, w_attn_o, w_fnet, w_out, norm2_g, w_router_group, b_router_group, w_router_expert, b_router_expert, w_exp_gate, w_exp_up, w_exp_down):
    depth = w_mod.shape[0]
    n_lat = c.shape[0]
    assert 1 + n_lat <= MOD_ROWS
    cond8 = jnp.concatenate([c_ctx[None, :], c, jnp.zeros((MOD_ROWS - 1 - n_lat, D_MODEL), F32)], axis=0)
    xp, xs = x_prompt, x_sample
    ckv_layers, kpe_layers = [], []
    for l in range(depth):
        mod, w_in_p = _adaln(cond8, w_mod[l], b_mod[l].reshape(1, -1), jnp.swapaxes(w_in[l], 0, 1))
        mod3 = mod.reshape(MOD_ROWS, 6, D_MODEL)
        wts = _pack_weights(l, norm1_g, q_a_norm_g, w_q_b, kv_a_norm_g, w_kv_b, q_norm_g,
                            k_norm_g, w_attn_o, w_fnet, w_out, norm2_g, w_router_group,
                            b_router_group, w_router_expert, b_router_expert)
        wts["w_in"] = w_in_p
        xp, xs, ckv, kpe = _layer(xp, xs, cache_ckv[:, l], cache_kpe[:, l], mod3, wts,
                                  (w_exp_gate[l], w_exp_up[l], w_exp_down[l]))
        ckv_layers.append(ckv)
        kpe_layers.append(kpe)
    return xp, xs, jnp.stack(ckv_layers, axis=1), jnp.stack(kpe_layers, axis=1)
```

```python
import functools
import math

import numpy as np
import jax
import jax.numpy as jnp
from jax import lax
from jax.experimental import pallas as pl
from jax.experimental.pallas import tpu as pltpu

D_MODEL = 1024
GRID_W = 64
N_HEADS = 8
Q_LORA = 512
KV_LORA = 256
QK_NOPE = 64
QK_ROPE = 32
V_HEAD = 64
QK_HEAD = QK_NOPE + QK_ROPE
ATTN_SCALE = QK_HEAD ** -0.5
ROPE_BASE = 10000.0
FN_GROUPS = 4
FN_GROUP_W = 128
FN_WIDTH = FN_GROUPS * FN_GROUP_W
N_GROUPS = 4
EXPERTS_PER_GROUP = 4
N_EXPERTS = N_GROUPS * EXPERTS_PER_GROUP
D_EXPERT = 512
EPS = 1e-6

LANES = 128
HEAD_SLOT = LANES
QK_WIDTH = N_HEADS * HEAD_SLOT
V_WIDTH = N_HEADS * V_HEAD
C_QA = 0
C_KVA = C_QA + Q_LORA
C_KPE = C_KVA + KV_LORA
C_FN = C_KPE + LANES
C_GA = C_FN + FN_WIDTH
C_GF = C_GA + D_MODEL
IN_PACKED = C_GF + D_MODEL
SUBLANES = 8
ROUTER_ROWS = SUBLANES
MOD_ROWS = SUBLANES
VMEM_LIMIT = 56 * 1024 * 1024

TM = 512
TQ_LAT = 256
TR_LAT = 512
SEG_ROWS = SUBLANES
TOKEN_ROWS = D_MODEL // LANES
MOE_TILE = 256
ADALN_STEPS = 8

BF16 = jnp.bfloat16
F32 = jnp.float32


def _cparams(sem):
    return pltpu.CompilerParams(dimension_semantics=sem, vmem_limit_bytes=VMEM_LIMIT)


def _dot(a, b):
    return jnp.dot(a, b, preferred_element_type=F32)


def _dot_nt(a, b):
    return lax.dot_general(a, b, (((1,), (1,)), ((), ())), preferred_element_type=F32)


def _sigmoid(x):
    return 1.0 / (1.0 + jnp.exp(-x))


def _split_bf16(x):
    hi = x.astype(BF16)
    return hi, (x - hi.astype(F32)).astype(BF16)


@functools.lru_cache(maxsize=None)
def _rope_tables(n_pos):
    half = QK_ROPE // 2
    quarter = half // 2
    freqs = ROPE_BASE ** (-np.arange(quarter, dtype=np.float64) / quarter)
    pos = np.arange(n_pos)
    row = (pos // GRID_W).astype(np.float64)
    col = (pos % GRID_W).astype(np.float64)
    cos_t = np.ones((n_pos, LANES), np.float64)
    sin_a = np.zeros((n_pos, LANES), np.float64)
    sin_b = np.zeros((n_pos, LANES), np.float64)
    for base, p in ((QK_NOPE, row), (QK_NOPE + half, col)):
        ang = p[:, None] * freqs[None, :]
        cos_t[:, base:base + quarter] = np.cos(ang)
        cos_t[:, base + quarter:base + half] = np.cos(ang)
        sin_a[:, base:base + quarter] = -np.sin(ang)
        sin_b[:, base + quarter:base + half] = np.sin(ang)
    return (cos_t.astype(np.float32), sin_a.astype(np.float32), sin_b.astype(np.float32))


@functools.lru_cache(maxsize=None)
def _dft_tables(n):
    k = np.arange(n)
    ang = 2.0 * np.pi * ((k[:, None] * k[None, :]) % n) / n
    s = 1.0 / math.sqrt(n)
    return (np.cos(ang) * s).astype(np.float32), (-np.sin(ang) * s).astype(np.float32)


def _adaln_kernel(cond_ref, w_ref, b_ref, win_ref, o_ref, winp_ref):
    c = cond_ref[...]
    s_hi, s_lo = _split_bf16(c * _sigmoid(c))
    w_hi, w_lo = _split_bf16(w_ref[...])
    y = _dot(jnp.concatenate([s_hi, s_lo], axis=0), w_hi)
    o_ref[...] = (y[:MOD_ROWS] + y[MOD_ROWS:]) + _dot(s_hi, w_lo) + b_ref[...]

    winp_ref[:C_KPE, :] = win_ref[:C_KPE, :].astype(BF16)
    winp_ref[C_KPE:C_FN, :] = jnp.zeros((LANES, winp_ref.shape[1]), BF16)
    winp_ref[C_KPE + QK_NOPE:C_KPE + QK_HEAD, :] = win_ref[C_KPE:C_KPE + QK_ROPE, :].astype(BF16)
    winp_ref[C_FN:, :] = win_ref[C_KPE + QK_ROPE:, :].astype(BF16)


def _adaln(cond8, w_mod, b_mod, w_in_t):
    n = w_mod.shape[1]
    steps = ADALN_STEPS
    tn, tr = n // steps, D_MODEL // steps
    return pl.pallas_call(
        _adaln_kernel,
        grid=(steps,),
        in_specs=[pl.BlockSpec((MOD_ROWS, D_MODEL), lambda j: (0, 0)),
                  pl.BlockSpec((D_MODEL, tn), lambda j: (0, j)),
                  pl.BlockSpec((1, tn), lambda j: (0, j)),
                  pl.BlockSpec((w_in_t.shape[0], tr), lambda j: (0, j))],
        out_specs=[pl.BlockSpec((MOD_ROWS, tn), lambda j: (0, j)),
                   pl.BlockSpec((IN_PACKED, tr), lambda j: (0, j))],
        out_shape=[jax.ShapeDtypeStruct((MOD_ROWS, n), F32),
                   jax.ShapeDtypeStruct((IN_PACKED, D_MODEL), BF16)],
        compiler_params=_cparams(("arbitrary",)),
        name="adaln",
    )(cond8, w_mod, b_mod, w_in_t)


def _rms(x, width):
    return lax.rsqrt(jnp.sum(x * x, axis=-1, keepdims=True) * (1.0 / width) + EPS)


def _rope(x, cos_t, sin_a, sin_b):
    return x * cos_t + pltpu.roll(x, LANES - 8, 1) * sin_a + pltpu.roll(x, 8, 1) * sin_b


def _inproj_kernel(*refs, rope, emit_cache, n_cast, seq_len):
    it = iter(refs)
    x_ref, mod_ref, g1_ref, win_ref, qag_ref, wqb_ref, kvg_ref, wkb_ref, wvb_ref = (
        next(it) for _ in range(9))
    qg_ref, kg_ref, dft_ref = next(it), next(it), next(it)
    if rope:
        cos_ref, sa_ref, sb_ref = next(it), next(it), next(it)
    if seq_len:
        seq_cs_ref, seq_ns_ref = next(it), next(it)
    cast_in = [next(it) for _ in range(n_cast)]
    if seq_len:
        attn_ref, fm_ref, sga_ref, sgf_ref = (next(it) for _ in range(4))
    else:
        q_ref, k_ref, v_ref, fcs_ref, sga_ref, sgf_ref = (next(it) for _ in range(6))
    if emit_cache:
        ckv_ref, kpe_ref = next(it), next(it)
    cast_out = [next(it) for _ in range(n_cast)]
    if seq_len:
        q_ref, k_ref, v_ref, fcs_ref = (next(it) for _ in range(4))

    for src, dst in zip(cast_in, cast_out):
        dst[...] = src[...].astype(BF16)

    x = x_ref[...]
    shift = mod_ref[0, 0:1, :]
    scale = mod_ref[0, 1:2, :]
    h = (x * _rms(x, D_MODEL) * g1_ref[...]) * (1.0 + scale) + shift
    hb = h.astype(BF16)

    if rope:
        cos_t, sin_a, sin_b = cos_ref[...], sa_ref[...], sb_ref[...]

    qa = _dot_nt(hb, win_ref[C_QA:C_QA + Q_LORA, :])
    qn = (qa * _rms(qa, Q_LORA) * qag_ref[...]).astype(BF16)
    q = _dot(qn, wqb_ref[...])
    qg = qg_ref[...] * ATTN_SCALE
    for hd in range(N_HEADS):
        qh = q[:, hd * HEAD_SLOT:(hd + 1) * HEAD_SLOT]
        qh = qh * _rms(qh, QK_HEAD) * qg
        if rope:
            qh = _rope(qh, cos_t, sin_a, sin_b)
        q_ref[:, hd * HEAD_SLOT:(hd + 1) * HEAD_SLOT] = qh.astype(BF16)

    kva = _dot_nt(hb, win_ref[C_KVA:C_KVA + KV_LORA, :])
    ckv = kva * _rms(kva, KV_LORA) * kvg_ref[...]
    kpe = _dot_nt(hb, win_ref[C_KPE:C_KPE + LANES, :])
    if emit_cache:
        ckv_ref[...] = ckv
        kpe_t = kpe.T[QK_NOPE:QK_NOPE + QK_ROPE, :]
        for j in range(x.shape[0] // seq_len):
            kpe_ref[j] = kpe_t[:, j * seq_len:(j + 1) * seq_len]
    _emit_kv(ckv.astype(BF16), kpe, wkb_ref, wvb_ref, kg_ref,
             (cos_t, sin_a, sin_b) if rope else None, k_ref, v_ref)

    fn = _dot_nt(hb, win_ref[C_FN:C_FN + FN_WIDTH, :]).astype(BF16)
    for g in range(FN_GROUPS):
        cs = _dot(fn[:, g * FN_GROUP_W:(g + 1) * FN_GROUP_W], dft_ref[...])
        fcs_ref[:, g * FN_GROUP_W:(g + 1) * FN_GROUP_W] = cs[:, :FN_GROUP_W].astype(BF16)
        fcs_ref[:, FN_WIDTH + g * FN_GROUP_W:FN_WIDTH + (g + 1) * FN_GROUP_W] = (
            cs[:, FN_GROUP_W:].astype(BF16))

    sga_ref[...] = _sigmoid(_dot_nt(hb, win_ref[C_GA:C_GA + D_MODEL, :])).astype(BF16)
    sgf_ref[...] = _sigmoid(_dot_nt(hb, win_ref[C_GF:C_GF + D_MODEL, :])).astype(BF16)

    if seq_len:
        for s0 in range(0, x.shape[0], seq_len):
            rows = pl.ds(s0, seq_len)
            attn_ref[rows, :] = _attend(q_ref.at[rows], [k_ref.at[rows]], [v_ref.at[:, rows]])
            fm_ref[rows, :] = (_dot(seq_cs_ref[...], fcs_ref[rows, :FN_WIDTH])
                               + _dot(seq_ns_ref[...], fcs_ref[rows, FN_WIDTH:])).astype(BF16)


def _emit_kv(ckvb, kpe, wkb_ref, wvb_ref, kg_ref, rope_tabs, k_ref, v_ref):
    kg = kg_ref[...]
    v_ref[...] = _dot_nt(wvb_ref[...], ckvb).astype(BF16)
    kn = _dot(ckvb, wkb_ref[...])
    pe_ss = jnp.sum(kpe * kpe, axis=-1, keepdims=True)
    pe_g = kpe * kg
    if rope_tabs is not None:
        pe_g = _rope(pe_g, *rope_tabs)
    for hd in range(N_HEADS):
        knh = kn[:, hd * HEAD_SLOT:(hd + 1) * HEAD_SLOT]
        ss = jnp.sum(knh * knh, axis=-1, keepdims=True) + pe_ss
        r = lax.rsqrt(ss * (1.0 / QK_HEAD) + EPS)
        k_ref[:, hd * HEAD_SLOT:(hd + 1) * HEAD_SLOT] = ((knh * kg + pe_g) * r).astype(BF16)


def _const_spec(shape):
    return pl.BlockSpec(shape, lambda i: (0,) * len(shape))


def _inproj(x2d, mod3, mod_row_fn, wts, rope_tabs, emit_cache, cast=(), seq=None):
    n = x2d.shape[0]
    assert all(w.shape[0] == n // TM for w in cast)
    assert seq is None or TM % seq[0] == 0
    rope = rope_tabs is not None
    tiles_per_seq = None if not rope else rope_tabs[0].shape[0] // TM
    in_specs = [pl.BlockSpec((TM, D_MODEL), lambda i: (i, 0)),
                pl.BlockSpec((1, 6, D_MODEL), lambda i: (mod_row_fn(i), 0, 0)),
                _const_spec((1, D_MODEL)),
                _const_spec((IN_PACKED, D_MODEL)),
                _const_spec((1, Q_LORA)),
                _const_spec((Q_LORA, QK_WIDTH)),
                _const_spec((1, KV_LORA)),
                _const_spec((KV_LORA, QK_WIDTH)),
                _const_spec((V_WIDTH, KV_LORA)),
                _const_spec((1, HEAD_SLOT)),
                _const_spec((1, HEAD_SLOT)),
                _const_spec((FN_GROUP_W, 2 * FN_GROUP_W))]
    args = [x2d, mod3, wts["g1"], wts["w_in"], wts["qag"], wts["w_qb"], wts["kvg"],
            wts["w_kb"], wts["w_vb"], wts["qg"], wts["kg"], wts["dft_c"]]
    if rope:
        in_specs += [pl.BlockSpec((TM, LANES), lambda i: (i % tiles_per_seq, 0))] * 3
        args += list(rope_tabs)
    if seq is not None:
        in_specs += [_const_spec((seq[0], seq[0]))] * 2
        args += [seq[1], seq[2]]
    cast_specs = [pl.BlockSpec((1,) + w.shape[1:], lambda i: (i, 0, 0)) for w in cast]
    in_specs += cast_specs
    args += list(cast)
    qkvf = [((TM, QK_WIDTH), (n, QK_WIDTH)), ((TM, QK_WIDTH), (n, QK_WIDTH)),
            ((V_WIDTH, TM), (V_WIDTH, n)), ((TM, 2 * FN_WIDTH), (n, 2 * FN_WIDTH))]
    if seq is None:
        widths = [None] * 4 + [D_MODEL, D_MODEL]
    else:
        widths = [V_WIDTH, FN_WIDTH, D_MODEL, D_MODEL]
    out_shape, out_specs = [], []
    for j, w in enumerate(widths):
        if w is None:
            out_shape.append(jax.ShapeDtypeStruct(qkvf[j][1], BF16))
            out_specs.append(pl.BlockSpec(qkvf[j][0], (lambda i: (0, i)) if j == 2 else (lambda i: (i, 0))))
        else:
            out_shape.append(jax.ShapeDtypeStruct((n, w), BF16))
            out_specs.append(pl.BlockSpec((TM, w), lambda i: (i, 0)))
    if emit_cache:
        assert seq is not None
        out_shape += [jax.ShapeDtypeStruct((n, KV_LORA), F32),
                      jax.ShapeDtypeStruct((n // seq[0], QK_ROPE, seq[0]), F32)]
        out_specs += [pl.BlockSpec((TM, KV_LORA), lambda i: (i, 0)),
                      pl.BlockSpec((TM // seq[0], QK_ROPE, seq[0]), lambda i: (i, 0, 0))]
    out_shape += [jax.ShapeDtypeStruct(w.shape, BF16) for w in cast]
    out_specs += cast_specs
    scratch = [] if seq is None else [pltpu.VMEM(blk, BF16) for blk, _ in qkvf]
    return pl.pallas_call(
        functools.partial(_inproj_kernel, rope=rope, emit_cache=emit_cache, n_cast=len(cast),
                          seq_len=None if seq is None else seq[0]),
        grid=(n // TM,),
        in_specs=in_specs,
        out_specs=out_specs,
        out_shape=out_shape,
        scratch_shapes=scratch,
        compiler_params=_cparams(("parallel",)),
        name="inproj_lat" if rope else "inproj_ctx",
    )(*args)


def _cache_kv_kernel(ckv_ref, kpe_ref, wkb_ref, wvb_ref, kg_ref, k_ref, v_ref):
    _emit_kv(ckv_ref[...].astype(BF16), kpe_ref[...], wkb_ref, wvb_ref, kg_ref, None, k_ref, v_ref)


def _cache_kv(ckv2d, kpe_slot2d, wts):
    n = ckv2d.shape[0]
    return pl.pallas_call(
        _cache_kv_kernel,
        grid=(n // TM,),
        in_specs=[pl.BlockSpec((TM, KV_LORA), lambda i: (i, 0)),
                  pl.BlockSpec((TM, LANES), lambda i: (i, 0)),
                  _const_spec((KV_LORA, QK_WIDTH)),
                  _const_spec((V_WIDTH, KV_LORA)),
                  _const_spec((1, HEAD_SLOT))],
        out_specs=[pl.BlockSpec((TM, QK_WIDTH), lambda i: (i, 0)),
                   pl.BlockSpec((V_WIDTH, TM), lambda i: (0, i))],
        out_shape=[jax.ShapeDtypeStruct((n, QK_WIDTH), BF16),
                   jax.ShapeDtypeStruct((V_WIDTH, n), BF16)],
        compiler_params=_cparams(("parallel",)),
        name="cache_kv",
    )(ckv2d, kpe_slot2d, wts["w_kb"], wts["w_vb"], wts["kg"])


def _attend(q, ks, vts):
    head = lambda hd: slice(hd * HEAD_SLOT, (hd + 1) * HEAD_SLOT)
    st = [jnp.stack([_dot_nt(k[:, head(hd)], q[:, head(hd)]) for hd in range(N_HEADS)]) for k in ks]
    m = functools.reduce(jnp.maximum, [sj.max(axis=1, keepdims=True) for sj in st])
    p = [jnp.exp(sj - m) for sj in st]
    l = functools.reduce(lambda a, b: a + b, [pj.sum(axis=1, keepdims=True) for pj in p])
    outs = []
    for hd in range(N_HEADS):
        o = functools.reduce(lambda a, b: a + b,
                             [_dot(vt[hd * V_HEAD:(hd + 1) * V_HEAD, :], pj[hd].astype(BF16))
                              for vt, pj in zip(vts, p)])
        outs.append(o / l[hd])
    return jnp.concatenate(outs, axis=0).T.astype(BF16)


def _attn_kernel(*refs, n_kv):
    q_ref = refs[0]
    k_refs = refs[1:1 + n_kv]
    vt_refs = refs[1 + n_kv:1 + 2 * n_kv]
    o_ref = refs[1 + 2 * n_kv]
    o_ref[0] = _attend(q_ref.at[0], [k.at[0] for k in k_refs], vt_refs)


def _attention(q3, ks, vts, tq, name):
    b, sq, _ = q3.shape
    n_kv = len(ks)
    in_specs = [pl.BlockSpec((1, tq, QK_WIDTH), lambda bi, qi: (bi, qi, 0))]
    in_specs += [pl.BlockSpec((1, k.shape[1], QK_WIDTH), lambda bi, qi: (bi, 0, 0)) for k in ks]
    in_specs += [pl.BlockSpec((V_WIDTH, k.shape[1]), lambda bi, qi: (0, bi)) for k in ks]
    return pl.pallas_call(
        functools.partial(_attn_kernel, n_kv=n_kv),
        grid=(b, sq // tq),
        in_specs=in_specs,
        out_specs=pl.BlockSpec((1, tq, V_WIDTH), lambda bi, qi: (bi, qi, 0)),
        out_shape=jax.ShapeDtypeStruct((b, sq, V_WIDTH), BF16),
        compiler_params=_cparams(("parallel", "parallel")),
        name=name,
    )(q3, *ks, *vts)


def _fourier_kernel(cs_ref, ns_ref, f_ref, o_ref, *, n_b):
    for b in range(n_b):
        xc = f_ref[b, :, :FN_WIDTH]
        xs = f_ref[b, :, FN_WIDTH:]
        o_ref[b] = (_dot(cs_ref[...], xc) + _dot(ns_ref[...], xs)).astype(BF16)


def _fourier(fcs3, cs, ns, tr, n_b, name):
    b, s, _ = fcs3.shape
    return pl.pallas_call(
        functools.partial(_fourier_kernel, n_b=n_b),
        grid=(b // n_b, s // tr),
        in_specs=[pl.BlockSpec((tr, s), lambda bi, ri: (ri, 0)),
                  pl.BlockSpec((tr, s), lambda bi, ri: (ri, 0)),
                  pl.BlockSpec((n_b, s, 2 * FN_WIDTH), lambda bi, ri: (bi, 0, 0))],
        out_specs=pl.BlockSpec((n_b, tr, FN_WIDTH), lambda bi, ri: (bi, ri, 0)),
        out_shape=jax.ShapeDtypeStruct((b, s, FN_WIDTH), BF16),
        compiler_params=_cparams(("parallel", "parallel")),
        name=name,
    )(cs, ns, fcs3)


def _router_logits(w, b, h_hi, h_lo):
    rows = w.shape[0]
    w_hi, w_lo = _split_bf16(w)
    y = _dot_nt(jnp.concatenate([w_hi, w_lo], axis=0), h_hi)
    return (y[:rows] + y[rows:]) + _dot_nt(w_hi, h_lo) + b


def _rows(x, n):
    return [x[j:j + 1, :] for j in range(n)]


def _first_argmax(rows, top):
    idx = jnp.full(top.shape, len(rows) - 1, jnp.int32)
    for j in range(len(rows) - 2, -1, -1):
        idx = jnp.where(rows[j] == top, j, idx)
    return idx


def _store_token_major(ref, x, tm):
    for s in range(TOKEN_ROWS):
        ref[pl.ds(s, tm, stride=TOKEN_ROWS), :] = x[:, s * LANES:(s + 1) * LANES]


def _load_token_major(ref, tm):
    return jnp.concatenate([ref[pl.ds(s, tm, stride=TOKEN_ROWS), :] for s in range(TOKEN_ROWS)],
                           axis=1)


def _merge_kernel(x_ref, attn_ref, fm_ref, sga_ref, sgf_ref, mod_ref, wao_ref, wfn_ref, wout_ref,
                  g2_ref, wrg_ref, brg_ref, x1_ref, h2_ref, grp_ref, rank_ref, cnt_ref, carry_ref):
    i = pl.program_id(0)

    @pl.when(i == 0)
    def _():
        carry_ref[...] = jnp.zeros_like(carry_ref)

    a = _dot(attn_ref[...], wao_ref[...])
    f = _dot(fm_ref[...], wfn_ref[...])
    u = sga_ref[...].astype(F32) * a + sgf_ref[...].astype(F32) * f
    y = _dot(u.astype(BF16), wout_ref[...])
    x1 = x_ref[...] + mod_ref[0, 2:3, :] * y
    x1_ref[...] = x1
    tm = x1.shape[0]
    h2 = (x1 * _rms(x1, D_MODEL) * g2_ref[...]) * (1.0 + mod_ref[0, 4:5, :]) + mod_ref[0, 3:4, :]
    _store_token_major(h2_ref, h2, tm)

    h2_hi, h2_lo = _split_bf16(h2)
    g = _rows(_router_logits(wrg_ref[...], brg_ref[...], h2_hi, h2_lo), N_GROUPS)
    gidx = _first_argmax(g, functools.reduce(jnp.maximum, g))

    onehot = jnp.where(lax.broadcasted_iota(jnp.int32, (SEG_ROWS, tm), 0) == gidx, 1.0, 0.0)
    before = (lax.broadcasted_iota(jnp.int32, (tm, tm), 0)
              < lax.broadcasted_iota(jnp.int32, (tm, tm), 1))
    prefix = _dot(onehot.astype(BF16), jnp.where(before, 1.0, 0.0).astype(BF16))
    carry = carry_ref[...]
    rank = jnp.sum(onehot * (prefix + carry[:, 0:1]), axis=0, keepdims=True)
    grp_ref[...] = gidx
    rank_ref[...] = rank.astype(jnp.int32)
    carry = carry + jnp.sum(onehot, axis=1, keepdims=True)
    carry_ref[...] = carry
    cnt_ref[...] = carry.astype(jnp.int32)


def _merge(x2d, attn2d, fm2d, sga, sgf, mod3, mod_row_fn, wts):
    n = x2d.shape[0]
    tok = lambda w: pl.BlockSpec((TM, w), lambda i: (i, 0))
    return pl.pallas_call(
        _merge_kernel,
        grid=(n // TM,),
        in_specs=[tok(D_MODEL), tok(V_WIDTH), tok(FN_WIDTH), tok(D_MODEL), tok(D_MODEL),
                  pl.BlockSpec((1, 6, D_MODEL), lambda i: (mod_row_fn(i), 0, 0)),
                  _const_spec((V_WIDTH, D_MODEL)),
                  _const_spec((FN_WIDTH, D_MODEL)),
                  _const_spec((D_MODEL, D_MODEL)),
                  _const_spec((1, D_MODEL)),
                  _const_spec((ROUTER_ROWS, D_MODEL)),
                  _const_spec((ROUTER_ROWS, 1))],
        out_specs=[tok(D_MODEL),
                   pl.BlockSpec((TM * TOKEN_ROWS, LANES), lambda i: (i, 0)),
                   pl.BlockSpec((1, TM), lambda i: (0, i)),
                   pl.BlockSpec((1, TM), lambda i: (0, i)), _const_spec((SEG_ROWS, LANES))],
        out_shape=[jax.ShapeDtypeStruct((n, D_MODEL), F32),
                   jax.ShapeDtypeStruct((n * TOKEN_ROWS, LANES), F32),
                   jax.ShapeDtypeStruct((1, n), jnp.int32),
                   jax.ShapeDtypeStruct((1, n), jnp.int32),
                   jax.ShapeDtypeStruct((SEG_ROWS, LANES), jnp.int32)],
        scratch_shapes=[pltpu.VMEM((SEG_ROWS, LANES), F32)],
        compiler_params=_cparams(("arbitrary",)),
        name="merge",
    )(x2d, attn2d, fm2d, sga, sgf, mod3, wts["w_ao"], wts["w_fn"], wts["w_out"], wts["g2"],
      wts["w_rg"], wts["b_rg"])


def _token_rows(ref, t):
    start = t * TOKEN_ROWS
    if not isinstance(t, int):
        start = pl.multiple_of(start, TOKEN_ROWS)
    return ref.at[pl.ds(start, TOKEN_ROWS)]


def _row_copy(src, dst, src_tok, dst_tok, sem):
    return pltpu.make_async_copy(_token_rows(src, src_tok), _token_rows(dst, dst_tok), sem)


def _wait_tile(src_hbm, buf, sem):
    pltpu.make_async_copy(src_hbm.at[pl.ds(0, buf.shape[0])], buf, sem).wait()


def _combine_kernel(pos_ref, y_hbm, x1_ref, mod_ref, o_ref, buf, sem):
    i = pl.program_id(0)
    tm = o_ref.shape[0]

    def gather(tile_idx, slot):
        def start(r, c):
            _row_copy(y_hbm, buf.at[slot], pos_ref[tile_idx * tm + r], r, sem.at[slot]).start()
            return c

        lax.fori_loop(0, tm, start, 0, unroll=32)

    @pl.when(i == 0)
    def _():
        gather(0, 0)

    @pl.when(i + 1 < pl.num_programs(0))
    def _():
        gather(i + 1, (i + 1) % 2)

    slot = i % 2

    _wait_tile(y_hbm, buf.at[slot], sem.at[slot])
    o_ref[...] = x1_ref[...] + mod_ref[0, 5:6, :] * _load_token_major(buf.at[slot], tm)


def _combine(y_sorted, pos, x1, mod3, mod_row_fn):
    n = x1.shape[0]
    return pl.pallas_call(
        _combine_kernel,
        grid_spec=pltpu.PrefetchScalarGridSpec(
            num_scalar_prefetch=1,
            grid=(n // TM,),
            in_specs=[pl.BlockSpec(memory_space=pl.ANY),
                      pl.BlockSpec((TM, D_MODEL), lambda i, pos: (i, 0)),
                      pl.BlockSpec((1, 6, D_MODEL), lambda i, pos: (mod_row_fn(i), 0, 0))],
            out_specs=pl.BlockSpec((TM, D_MODEL), lambda i, pos: (i, 0)),
            scratch_shapes=[pltpu.VMEM((2, TM * TOKEN_ROWS, LANES), F32),
                            pltpu.SemaphoreType.DMA((2,))]),
        out_shape=jax.ShapeDtypeStruct((n, D_MODEL), F32),
        compiler_params=_cparams(("arbitrary",)),
        name="moe_combine",
    )(pos, y_sorted, x1, mod3)


def _moe_kernel(grp_ref, on_ref, src_ref, h_hbm, wrg_ref, brg_ref, wre_ref, bre_ref, wg_ref, wu_ref,
                wd_ref, o_ref, hbuf, sem, *, tile):
    i = pl.program_id(0)

    def gather(tile_idx, slot):
        def start(r, c):
            _row_copy(h_hbm, hbuf.at[slot], src_ref[tile_idx * tile + r], r, sem.at[slot]).start()
            return c

        lax.fori_loop(0, tile, start, 0, unroll=32)

    @pl.when(jnp.logical_and(i == 0, on_ref[0] == 1))
    def _():
        gather(0, 0)

    nxt = jnp.minimum(i + 1, pl.num_programs(0) - 1)

    @pl.when(jnp.logical_and(i + 1 < pl.num_programs(0), on_ref[nxt] == 1))
    def _():
        gather(i + 1, (i + 1) % 2)

    @pl.when(on_ref[i] == 0)
    def _():
        o_ref[...] = jnp.zeros_like(o_ref)

    @pl.when(on_ref[i] == 1)
    def _():
        slot = i % 2

        _wait_tile(h_hbm, hbuf.at[slot], sem.at[slot])
        h2_hi, h2_lo = _split_bf16(_load_token_major(hbuf.at[slot], tile))

        logits = _router_logits(jnp.concatenate([wrg_ref[...], wre_ref[0]], axis=0),
                                jnp.concatenate([brg_ref[...], bre_ref[0]], axis=0), h2_hi, h2_lo)
        g = _rows(logits, N_GROUPS)
        gmax = functools.reduce(jnp.maximum, g)
        p_top = 1.0 / functools.reduce(lambda p, q: p + q, [jnp.exp(gj - gmax) for gj in g])
        e = _rows(logits[ROUTER_ROWS:], EXPERTS_PER_GROUP)
        m1 = functools.reduce(jnp.maximum, e)
        i1 = _first_argmax(e, m1)
        rest = [jnp.where(i1 == j, -jnp.inf, e[j]) for j in range(EXPERTS_PER_GROUP)]
        m2 = functools.reduce(jnp.maximum, rest)
        i2 = _first_argmax(rest, m2)
        t = jnp.exp(m2 - m1)
        w1 = p_top / (1.0 + t)
        w2 = p_top * t / (1.0 + t)
        row = lax.broadcasted_iota(jnp.int32, (LANES, tile), 0)
        comb = (jnp.where(row == i1, w1, 0.0) + jnp.where(row == i2, w2, 0.0)).T

        gates = [_dot(h2_hi, wg_ref[j]) for j in range(EXPERTS_PER_GROUP)]
        ups = [_dot(h2_hi, wu_ref[j]) for j in range(EXPERTS_PER_GROUP)]
        acts = [((a * _sigmoid(a)) * u * comb[:, j:j + 1]).astype(BF16)
                for j, (a, u) in enumerate(zip(gates, ups))]
        acc = functools.reduce(lambda p, q: p + q,
                               [_dot(acts[j], wd_ref[j]) for j in range(EXPERTS_PER_GROUP)])
        _store_token_major(o_ref, acc, tile)


def _moe(h2tm, src, maps, wts, w_gate, w_up, w_down, tile):
    n_tiles = src.shape[0] // tile
    const = lambda shape: pl.BlockSpec(shape, lambda i, grp, on, src: (0,) * len(shape))
    by_group = lambda shape: pl.BlockSpec(
        shape, lambda i, grp, on, src: (grp[i],) + (0,) * (len(shape) - 1))
    return pl.pallas_call(
        functools.partial(_moe_kernel, tile=tile),
        grid_spec=pltpu.PrefetchScalarGridSpec(
            num_scalar_prefetch=3,
            grid=(n_tiles,),
            in_specs=[pl.BlockSpec(memory_space=pl.ANY),
                      const((ROUTER_ROWS, D_MODEL)), const((ROUTER_ROWS, 1)),
                      by_group((1, ROUTER_ROWS, D_MODEL)), by_group((1, ROUTER_ROWS, 1)),
                      by_group((EXPERTS_PER_GROUP, D_MODEL, D_EXPERT)),
                      by_group((EXPERTS_PER_GROUP, D_MODEL, D_EXPERT)),
                      by_group((EXPERTS_PER_GROUP, D_EXPERT, D_MODEL))],
            out_specs=pl.BlockSpec((tile * TOKEN_ROWS, LANES), lambda i, grp, on, src: (i, 0)),
            scratch_shapes=[pltpu.VMEM((2, tile * TOKEN_ROWS, LANES), F32),
                            pltpu.SemaphoreType.DMA((2,))]),
        out_shape=jax.ShapeDtypeStruct((n_tiles * tile * TOKEN_ROWS, LANES), F32),
        compiler_params=_cparams(("arbitrary",)),
        name="moe",
    )(*maps, src, h2tm, wts["w_rg"], wts["b_rg"], wts["w_re"], wts["b_re"], w_gate, w_up, w_down)


def _invert_kernel(pos_ref, lo_ref, hi_ref, src_ref, *, n, n_fill):
    def zero(p, c):
        src_ref[p] = 0
        return c

    for s in range(n_fill):
        lax.fori_loop(lo_ref[s], hi_ref[s], zero, 0)

    def put(t, c):
        src_ref[pos_ref[t]] = t
        return c

    lax.fori_loop(0, n, put, 0, unroll=8)


def _invert(pos, fill_lo, fill_hi, n_sorted):
    return pl.pallas_call(
        functools.partial(_invert_kernel, n=pos.shape[0], n_fill=fill_lo.shape[0]),
        grid_spec=pltpu.PrefetchScalarGridSpec(
            num_scalar_prefetch=3, grid=(1,), in_specs=[],
            out_specs=pl.BlockSpec(memory_space=pltpu.SMEM)),
        out_shape=jax.ShapeDtypeStruct((n_sorted,), jnp.int32),
        name="moe_invert",
    )(pos, fill_lo, fill_hi)


def _sort_plan(grp, rank, cnt, tile, n_tiles):
    n_tile_grp = (cnt + tile - 1) // tile
    tile_end = jnp.cumsum(n_tile_grp)
    tile_start = tile_end - n_tile_grp
    total = tile_end[-1]
    pos = ((tile_start * tile)[grp] + rank).astype(jnp.int32)
    fill_lo = jnp.concatenate([tile_start * tile + cnt, total[None] * tile]).astype(jnp.int32)
    fill_hi = jnp.concatenate([tile_end * tile, jnp.full((1,), n_tiles * tile)]).astype(jnp.int32)
    src = _invert(pos, fill_lo, fill_hi, n_tiles * tile)
    i = jnp.arange(n_tiles, dtype=jnp.int32)
    tile_grp = jnp.sum((jnp.minimum(i, total - 1)[:, None] >= tile_end[None, :]).astype(jnp.int32),
                       axis=1)
    return pos, src, (tile_grp.astype(jnp.int32), (i < total).astype(jnp.int32))


def _sparse_moe(x1, h2tm, grp, rank, cnt8, mod3, mod_row_fn, wts, w_gate, w_up, w_down, tile):
    n = x1.shape[0]
    n_tiles = n // tile + N_GROUPS
    pos, src, maps = _sort_plan(grp.reshape(n), rank.reshape(n), cnt8[:N_GROUPS, 0], tile, n_tiles)
    y_sorted = _moe(h2tm, src, maps, wts, w_gate, w_up, w_down, tile)
    return _combine(y_sorted, pos, x1, mod3, mod_row_fn)


def _pack_weights(l, norm1_g, q_a_norm_g, w_q_b, kv_a_norm_g, w_kv_b, q_norm_g, k_norm_g,
                  w_attn_o, w_fnet, w_out, norm2_g, w_router_group, b_router_group,
                  w_router_expert, b_router_expert):
    w_qb = jnp.pad(w_q_b[l].reshape(Q_LORA, N_HEADS, QK_HEAD),
                   ((0, 0), (0, 0), (0, HEAD_SLOT - QK_HEAD))).reshape(Q_LORA, QK_WIDTH)
    wkv = w_kv_b[l].reshape(KV_LORA, N_HEADS, QK_NOPE + V_HEAD)
    w_kb = jnp.pad(wkv[:, :, :QK_NOPE],
                   ((0, 0), (0, 0), (0, HEAD_SLOT - QK_NOPE))).reshape(KV_LORA, QK_WIDTH)
    w_vb = wkv[:, :, QK_NOPE:].reshape(KV_LORA, V_WIDTH)
    pad_g = lambda g: jnp.pad(g, (0, HEAD_SLOT - QK_HEAD)).reshape(1, HEAD_SLOT)
    w_rg = jnp.pad(w_router_group[l].T, ((0, ROUTER_ROWS - N_GROUPS), (0, 0)))
    b_rg = jnp.pad(b_router_group[l], (0, ROUTER_ROWS - N_GROUPS)).reshape(ROUTER_ROWS, 1)
    w_re = jnp.pad(w_router_expert[l].T.reshape(N_GROUPS, EXPERTS_PER_GROUP, D_MODEL),
                   ((0, 0), (0, ROUTER_ROWS - EXPERTS_PER_GROUP), (0, 0)))
    b_re = jnp.pad(b_router_expert[l].reshape(N_GROUPS, EXPERTS_PER_GROUP),
                   ((0, 0), (0, ROUTER_ROWS - EXPERTS_PER_GROUP))).reshape(N_GROUPS, ROUTER_ROWS, 1)
    dft_c, dft_ns = _dft_tables(FN_GROUP_W)
    return {
        "g1": norm1_g[l].reshape(1, D_MODEL),
        "qag": q_a_norm_g[l].reshape(1, Q_LORA),
        "w_qb": w_qb.astype(BF16),
        "kvg": kv_a_norm_g[l].reshape(1, KV_LORA),
        "w_kb": w_kb.astype(BF16),
        "w_vb": w_vb.T.astype(BF16),
        "qg": pad_g(q_norm_g[l]),
        "kg": pad_g(k_norm_g[l]),
        "dft_c": jnp.concatenate([jnp.asarray(dft_c), -jnp.asarray(dft_ns)], axis=1).astype(BF16),
        "w_ao": w_attn_o[l].astype(BF16),
        "w_fn": w_fnet[l].astype(BF16),
        "w_out": w_out[l].astype(BF16),
        "g2": norm2_g[l].reshape(1, D_MODEL),
        "w_rg": w_rg,
        "b_rg": b_rg,
        "w_re": w_re,
        "b_re": b_re,
    }


def _layer(xp, xs, cache_ckv_l, cache_kpe_l, mod3, wts, experts):
    bp, sp, _ = xp.shape
    bs, ss, _ = xs.shape
    past = cache_ckv_l.shape[1]
    ctx_row = lambda i: 0
    lat_row = lambda i: 1 + i // (ss // TM)

    xp2 = xp.reshape(bp * sp, D_MODEL)
    ride = (bp * sp) // TM == N_EXPERTS
    cs, ns = (jnp.asarray(t).astype(BF16) for t in _dft_tables(sp))
    outs = _inproj(xp2, mod3, ctx_row, wts, None, True, experts if ride else (), (sp, cs, ns))
    attn, fm, sga, sgf, ckv, kpe = outs[:6]
    w_gate, w_up, w_down = outs[6:] if ride else (w.astype(BF16) for w in experts)
    x1, h2tm, grp, rank, cnt = _merge(xp2, attn, fm, sga, sgf, mod3, ctx_row, wts)
    yp = _sparse_moe(x1, h2tm, grp, rank, cnt, mod3, ctx_row, wts, w_gate, w_up, w_down,
                     MOE_TILE).reshape(bp, sp, D_MODEL)

    xs2 = xs.reshape(bs * ss, D_MODEL)
    rope_tabs = tuple(jnp.asarray(t) for t in _rope_tables(ss))
    q, k, v, fcs, sga, sgf = _inproj(xs2, mod3, lat_row, wts, rope_tabs, False)
    kpe_slot = jnp.pad(cache_kpe_l, ((0, 0), (0, 0), (QK_NOPE, LANES - QK_HEAD)))
    kc, vc = _cache_kv(cache_ckv_l.reshape(bs * past, KV_LORA), kpe_slot.reshape(bs * past, LANES), wts)
    attn = _attention(q.reshape(bs, ss, QK_WIDTH),
                      [kc.reshape(bs, past, QK_WIDTH), k.reshape(bs, ss, QK_WIDTH)],
                      [vc, v], TQ_LAT, "attn_lat")
    cs, ns = (jnp.asarray(t).astype(BF16) for t in _dft_tables(ss))
    fm = _fourier(fcs.reshape(bs, ss, 2 * FN_WIDTH), cs, ns, TR_LAT, 1, "fourier_lat")
    x1, h2tm, grp, rank, cnt = _merge(xs2, attn.reshape(bs * ss, V_WIDTH),
                                      fm.reshape(bs * ss, FN_WIDTH), sga, sgf, mod3, lat_row, wts)
    ys = _sparse_moe(x1, h2tm, grp, rank, cnt, mod3, lat_row, wts, w_gate, w_up, w_down,
                     MOE_TILE).reshape(bs, ss, D_MODEL)

    return yp, ys, ckv.reshape(bp, sp, KV_LORA), jnp.swapaxes(kpe, 1, 2)


def kernel(x_prompt, x_sample, cache_ckv, cache_kpe, c, c_ctx, w_mod, b_mod, norm1_g, w_in, q_a_norm_g, w_q_b, kv_a_norm_g, w_kv_b, q_norm_g, k_norm_g, w_attn_o, w_fnet, w_out, norm2_g, w_router_group, b_router_group, w_router_expert, b_router_expert, w_exp_gate, w_exp_up, w_exp_down):
    depth = w_mod.shape[0]
    n_lat = c.shape[0]
    assert 1 + n_lat <= MOD_ROWS
    cond8 = jnp.concatenate([c_ctx[None, :], c, jnp.zeros((MOD_ROWS - 1 - n_lat, D_MODEL), F32)], axis=0)
    xp, xs = x_prompt, x_sample
    ckv_layers, kpe_layers = [], []
    for l in range(depth):
        mod, w_in_p = _adaln(cond8, w_mod[l], b_mod[l].reshape(1, -1), jnp.swapaxes(w_in[l], 0, 1))
        mod3 = mod.reshape(MOD_ROWS, 6, D_MODEL)
        wts = _pack_weights(l, norm1_g, q_a_norm_g, w_q_b, kv_a_norm_g, w_kv_b, q_norm_g,
                            k_norm_g, w_attn_o, w_fnet, w_out, norm2_g, w_router_group,
                            b_router_group, w_router_expert, b_router_expert)
        wts["w_in"] = w_in_p
        xp, xs, ckv, kpe = _layer(xp, xs, cache_ckv[:, l], cache_kpe[:, l], mod3, wts,
                                  (w_exp_gate[l], w_exp_up[l], w_exp_down[l]))
        ckv_layers.append(ckv)
        kpe_layers.append(kpe)
    return xp, xs, jnp.stack(ckv_layers, axis=1), jnp.stack(kpe_layers, axis=1)
```

```python
import functools
import math

import numpy as np
import jax
import jax.numpy as jnp
from jax import lax
from jax.experimental import pallas as pl
from jax.experimental.pallas import tpu as pltpu

D_MODEL = 1024
GRID_W = 64
N_HEADS = 8
Q_LORA = 512
KV_LORA = 256
QK_NOPE = 64
QK_ROPE = 32
V_HEAD = 64
QK_HEAD = QK_NOPE + QK_ROPE
ATTN_SCALE = QK_HEAD ** -0.5
ROPE_BASE = 10000.0
FN_GROUPS = 4
FN_GROUP_W = 128
FN_WIDTH = FN_GROUPS * FN_GROUP_W
N_GROUPS = 4
EXPERTS_PER_GROUP = 4
N_EXPERTS = N_GROUPS * EXPERTS_PER_GROUP
D_EXPERT = 512
EPS = 1e-6

LANES = 128
HEAD_SLOT = LANES
QK_WIDTH = N_HEADS * HEAD_SLOT
V_WIDTH = N_HEADS * V_HEAD
C_QA = 0
C_KVA = C_QA + Q_LORA
C_KPE = C_KVA + KV_LORA
C_FN = C_KPE + LANES
C_GA = C_FN + FN_WIDTH
C_GF = C_GA + D_MODEL
IN_PACKED = C_GF + D_MODEL
SUBLANES = 8
ROUTER_ROWS = SUBLANES
MOD_ROWS = SUBLANES
VMEM_LIMIT = 56 * 1024 * 1024

TM = 512
TQ_LAT = 256
TR_LAT = 512
SEG_ROWS = SUBLANES
TOKEN_ROWS = D_MODEL // LANES
MOE_TILE = 256
ADALN_STEPS = 8

BF16 = jnp.bfloat16
F32 = jnp.float32


def _cparams(sem):
    return pltpu.CompilerParams(dimension_semantics=sem, vmem_limit_bytes=VMEM_LIMIT)


def _dot(a, b):
    return jnp.dot(a, b, preferred_element_type=F32)


def _dot_nt(a, b):
    return lax.dot_general(a, b, (((1,), (1,)), ((), ())), preferred_element_type=F32)


def _sigmoid(x):
    return 1.0 / (1.0 + jnp.exp(-x))


def _split_bf16(x):
    hi = x.astype(BF16)
    return hi, (x - hi.astype(F32)).astype(BF16)


@functools.lru_cache(maxsize=None)
def _rope_tables(n_pos):
    half = QK_ROPE // 2
    quarter = half // 2
    freqs = ROPE_BASE ** (-np.arange(quarter, dtype=np.float64) / quarter)
    pos = np.arange(n_pos)
    row = (pos // GRID_W).astype(np.float64)
    col = (pos % GRID_W).astype(np.float64)
    cos_t = np.ones((n_pos, LANES), np.float64)
    sin_a = np.zeros((n_pos, LANES), np.float64)
    sin_b = np.zeros((n_pos, LANES), np.float64)
    for base, p in ((QK_NOPE, row), (QK_NOPE + half, col)):
        ang = p[:, None] * freqs[None, :]
        cos_t[:, base:base + quarter] = np.cos(ang)
        cos_t[:, base + quarter:base + half] = np.cos(ang)
        sin_a[:, base:base + quarter] = -np.sin(ang)
        sin_b[:, base + quarter:base + half] = np.sin(ang)
    return (cos_t.astype(np.float32), sin_a.astype(np.float32), sin_b.astype(np.float32))


@functools.lru_cache(maxsize=None)
def _dft_tables(n):
    k = np.arange(n)
    ang = 2.0 * np.pi * ((k[:, None] * k[None, :]) % n) / n
    s = 1.0 / math.sqrt(n)
    return (np.cos(ang) * s).astype(np.float32), (-np.sin(ang) * s).astype(np.float32)


def _adaln_kernel(cond_ref, w_ref, b_ref, win_ref, o_ref, winp_ref):
    c = cond_ref[...]
    s_hi, s_lo = _split_bf16(c * _sigmoid(c))
    w_hi, w_lo = _split_bf16(w_ref[...])
    y = _dot(jnp.concatenate([s_hi, s_lo], axis=0), w_hi)
    o_ref[...] = (y[:MOD_ROWS] + y[MOD_ROWS:]) + _dot(s_hi, w_lo) + b_ref[...]

    winp_ref[:C_KPE, :] = win_ref[:C_KPE, :].astype(BF16)
    winp_ref[C_KPE:C_FN, :] = jnp.zeros((LANES, winp_ref.shape[1]), BF16)
    winp_ref[C_KPE + QK_NOPE:C_KPE + QK_HEAD, :] = win_ref[C_KPE:C_KPE + QK_ROPE, :].astype(BF16)
    winp_ref[C_FN:, :] = win_ref[C_KPE + QK_ROPE:, :].astype(BF16)


def _adaln(cond8, w_mod, b_mod, w_in_t):
    n = w_mod.shape[1]
    steps = ADALN_STEPS
    tn, tr = n // steps, D_MODEL // steps
    return pl.pallas_call(
        _adaln_kernel,
        grid=(steps,),
        in_specs=[pl.BlockSpec((MOD_ROWS, D_MODEL), lambda j: (0, 0)),
                  pl.BlockSpec((D_MODEL, tn), lambda j: (0, j)),
                  pl.BlockSpec((1, tn), lambda j: (0, j)),
                  pl.BlockSpec((w_in_t.shape[0], tr), lambda j: (0, j))],
        out_specs=[pl.BlockSpec((MOD_ROWS, tn), lambda j: (0, j)),
                   pl.BlockSpec((IN_PACKED, tr), lambda j: (0, j))],
        out_shape=[jax.ShapeDtypeStruct((MOD_ROWS, n), F32),
                   jax.ShapeDtypeStruct((IN_PACKED, D_MODEL), BF16)],
        compiler_params=_cparams(("arbitrary",)),
        name="adaln",
    )(cond8, w_mod, b_mod, w_in_t)


def _rms(x, width):
    return lax.rsqrt(jnp.sum(x * x, axis=-1, keepdims=True) * (1.0 / width) + EPS)


def _rope(x, cos_t, sin_a, sin_b):
    return x * cos_t + pltpu.roll(x, LANES - 8, 1) * sin_a + pltpu.roll(x, 8, 1) * sin_b


def _inproj_kernel(*refs, rope, emit_cache, n_cast, seq_len):
    it = iter(refs)
    x_ref, mod_ref, g1_ref, win_ref, qag_ref, wqb_ref, kvg_ref, wkb_ref, wvb_ref = (
        next(it) for _ in range(9))
    qg_ref, kg_ref, dft_ref = next(it), next(it), next(it)
    if rope:
        cos_ref, sa_ref, sb_ref = next(it), next(it), next(it)
    if seq_len:
        seq_cs_ref, seq_ns_ref = next(it), next(it)
    cast_in = [next(it) for _ in range(n_cast)]
    if seq_len:
        attn_ref, fm_ref, sga_ref, sgf_ref = (next(it) for _ in range(4))
    else:
        q_ref, k_ref, v_ref, fcs_ref, sga_ref, sgf_ref = (next(it) for _ in range(6))
    if emit_cache:
        ckv_ref, kpe_ref = next(it), next(it)
    cast_out = [next(it) for _ in range(n_cast)]
    if seq_len:
        q_ref, k_ref, v_ref, fcs_ref = (next(it) for _ in range(4))

    for src, dst in zip(cast_in, cast_out):
        dst[...] = src[...].astype(BF16)

    x = x_ref[...]
    shift = mod_ref[0, 0:1, :]
    scale = mod_ref[0, 1:2, :]
    h = (x * _rms(x, D_MODEL) * g1_ref[...]) * (1.0 + scale) + shift
    hb = h.astype(BF16)

    if rope:
        cos_t, sin_a, sin_b = cos_ref[...], sa_ref[...], sb_ref[...]

    qa = _dot_nt(hb, win_ref[C_QA:C_QA + Q_LORA, :])
    qn = (qa * _rms(qa, Q_LORA) * qag_ref[...]).astype(BF16)
    q = _dot(qn, wqb_ref[...])
    qg = qg_ref[...] * ATTN_SCALE
    for hd in range(N_HEADS):
        qh = q[:, hd * HEAD_SLOT:(hd + 1) * HEAD_SLOT]
        qh = qh * _rms(qh, QK_HEAD) * qg
        if rope:
            qh = _rope(qh, cos_t, sin_a, sin_b)
        q_ref[:, hd * HEAD_SLOT:(hd + 1) * HEAD_SLOT] = qh.astype(BF16)

    kva = _dot_nt(hb, win_ref[C_KVA:C_KVA + KV_LORA, :])
    ckv = kva * _rms(kva, KV_LORA) * kvg_ref[...]
    kpe = _dot_nt(hb, win_ref[C_KPE:C_KPE + LANES, :])
    if emit_cache:
        ckv_ref[...] = ckv
        kpe_t = kpe.T[QK_NOPE:QK_NOPE + QK_ROPE, :]
        for j in range(x.shape[0] // seq_len):
            kpe_ref[j] = kpe_t[:, j * seq_len:(j + 1) * seq_len]
    _emit_kv(ckv.astype(BF16), kpe, wkb_ref, wvb_ref, kg_ref,
             (cos_t, sin_a, sin_b) if rope else None, k_ref, v_ref)

    fn = _dot_nt(hb, win_ref[C_FN:C_FN + FN_WIDTH, :]).astype(BF16)
    for g in range(FN_GROUPS):
        cs = _dot(fn[:, g * FN_GROUP_W:(g + 1) * FN_GROUP_W], dft_ref[...])
        fcs_ref[:, g * FN_GROUP_W:(g + 1) * FN_GROUP_W] = cs[:, :FN_GROUP_W].astype(BF16)
        fcs_ref[:, FN_WIDTH + g * FN_GROUP_W:FN_WIDTH + (g + 1) * FN_GROUP_W] = (
            cs[:, FN_GROUP_W:].astype(BF16))

    sga_ref[...] = _sigmoid(_dot_nt(hb, win_ref[C_GA:C_GA + D_MODEL, :])).astype(BF16)
    sgf_ref[...] = _sigmoid(_dot_nt(hb, win_ref[C_GF:C_GF + D_MODEL, :])).astype(BF16)

    if seq_len:
        for s0 in range(0, x.shape[0], seq_len):
            rows = pl.ds(s0, seq_len)
            attn_ref[rows, :] = _attend(q_ref.at[rows], [k_ref.at[rows]], [v_ref.at[:, rows]])
            fm_ref[rows, :] = (_dot(seq_cs_ref[...], fcs_ref[rows, :FN_WIDTH])
                               + _dot(seq_ns_ref[...], fcs_ref[rows, FN_WIDTH:])).astype(BF16)


def _emit_kv(ckvb, kpe, wkb_ref, wvb_ref, kg_ref, rope_tabs, k_ref, v_ref):
    kg = kg_ref[...]
    v_ref[...] = _dot_nt(wvb_ref[...], ckvb).astype(BF16)
    kn = _dot(ckvb, wkb_ref[...])
    pe_ss = jnp.sum(kpe * kpe, axis=-1, keepdims=True)
    pe_g = kpe * kg
    if rope_tabs is not None:
        pe_g = _rope(pe_g, *rope_tabs)
    for hd in range(N_HEADS):
        knh = kn[:, hd * HEAD_SLOT:(hd + 1) * HEAD_SLOT]
        ss = jnp.sum(knh * knh, axis=-1, keepdims=True) + pe_ss
        r = lax.rsqrt(ss * (1.0 / QK_HEAD) + EPS)
        k_ref[:, hd * HEAD_SLOT:(hd + 1) * HEAD_SLOT] = ((knh * kg + pe_g) * r).astype(BF16)


def _const_spec(shape):
    return pl.BlockSpec(shape, lambda i: (0,) * len(shape))


def _inproj(x2d, mod3, mod_row_fn, wts, rope_tabs, emit_cache, cast=(), seq=None):
    n = x2d.shape[0]
    assert all(w.shape[0] == n // TM for w in cast)
    assert seq is None or TM % seq[0] == 0
    rope = rope_tabs is not None
    tiles_per_seq = None if not rope else rope_tabs[0].shape[0] // TM
    in_specs = [pl.BlockSpec((TM, D_MODEL), lambda i: (i, 0)),
                pl.BlockSpec((1, 6, D_MODEL), lambda i: (mod_row_fn(i), 0, 0)),
                _const_spec((1, D_MODEL)),
                _const_spec((IN_PACKED, D_MODEL)),
                _const_spec((1, Q_LORA)),
                _const_spec((Q_LORA, QK_WIDTH)),
                _const_spec((1, KV_LORA)),
                _const_spec((KV_LORA, QK_WIDTH)),
                _const_spec((V_WIDTH, KV_LORA)),
                _const_spec((1, HEAD_SLOT)),
                _const_spec((1, HEAD_SLOT)),
                _const_spec((FN_GROUP_W, 2 * FN_GROUP_W))]
    args = [x2d, mod3, wts["g1"], wts["w_in"], wts["qag"], wts["w_qb"], wts["kvg"],
            wts["w_kb"], wts["w_vb"], wts["qg"], wts["kg"], wts["dft_c"]]
    if rope:
        in_specs += [pl.BlockSpec((TM, LANES), lambda i: (i % tiles_per_seq, 0))] * 3
        args += list(rope_tabs)
    if seq is not None:
        in_specs += [_const_spec((seq[0], seq[0]))] * 2
        args += [seq[1], seq[2]]
    cast_specs = [pl.BlockSpec((1,) + w.shape[1:], lambda i: (i, 0, 0)) for w in cast]
    in_specs += cast_specs
    args += list(cast)
    qkvf = [((TM, QK_WIDTH), (n, QK_WIDTH)), ((TM, QK_WIDTH), (n, QK_WIDTH)),
            ((V_WIDTH, TM), (V_WIDTH, n)), ((TM, 2 * FN_WIDTH), (n, 2 * FN_WIDTH))]
    if seq is None:
        widths = [None] * 4 + [D_MODEL, D_MODEL]
    else:
        widths = [V_WIDTH, FN_WIDTH, D_MODEL, D_MODEL]
    out_shape, out_specs = [], []
    for j, w in enumerate(widths):
        if w is None:
            out_shape.append(jax.ShapeDtypeStruct(qkvf[j][1], BF16))
            out_specs.append(pl.BlockSpec(qkvf[j][0], (lambda i: (0, i)) if j == 2 else (lambda i: (i, 0))))
        else:
            out_shape.append(jax.ShapeDtypeStruct((n, w), BF16))
            out_specs.append(pl.BlockSpec((TM, w), lambda i: (i, 0)))
    if emit_cache:
        assert seq is not None
        out_shape += [jax.ShapeDtypeStruct((n, KV_LORA), F32),
                      jax.ShapeDtypeStruct((n // seq[0], QK_ROPE, seq[0]), F32)]
        out_specs += [pl.BlockSpec((TM, KV_LORA), lambda i: (i, 0)),
                      pl.BlockSpec((TM // seq[0], QK_ROPE, seq[0]), lambda i: (i, 0, 0))]
    out_shape += [jax.ShapeDtypeStruct(w.shape, BF16) for w in cast]
    out_specs += cast_specs
    scratch = [] if seq is None else [pltpu.VMEM(blk, BF16) for blk, _ in qkvf]
    return pl.pallas_call(
        functools.partial(_inproj_kernel, rope=rope, emit_cache=emit_cache, n_cast=len(cast),
                          seq_len=None if seq is None else seq[0]),
        grid=(n // TM,),
        in_specs=in_specs,
        out_specs=out_specs,
        out_shape=out_shape,
        scratch_shapes=scratch,
        compiler_params=_cparams(("parallel",)),
        name="inproj_lat" if rope else "inproj_ctx",
    )(*args)


def _cache_kv_kernel(ckv_ref, kpe_ref, wkb_ref, wvb_ref, kg_ref, k_ref, v_ref):
    _emit_kv(ckv_ref[...].astype(BF16), kpe_ref[...], wkb_ref, wvb_ref, kg_ref, None, k_ref, v_ref)


def _cache_kv(ckv2d, kpe_slot2d, wts):
    n = ckv2d.shape[0]
    return pl.pallas_call(
        _cache_kv_kernel,
        grid=(n // TM,),
        in_specs=[pl.BlockSpec((TM, KV_LORA), lambda i: (i, 0)),
                  pl.BlockSpec((TM, LANES), lambda i: (i, 0)),
                  _const_spec((KV_LORA, QK_WIDTH)),
                  _const_spec((V_WIDTH, KV_LORA)),
                  _const_spec((1, HEAD_SLOT))],
        out_specs=[pl.BlockSpec((TM, QK_WIDTH), lambda i: (i, 0)),
                   pl.BlockSpec((V_WIDTH, TM), lambda i: (0, i))],
        out_shape=[jax.ShapeDtypeStruct((n, QK_WIDTH), BF16),
                   jax.ShapeDtypeStruct((V_WIDTH, n), BF16)],
        compiler_params=_cparams(("parallel",)),
        name="cache_kv",
    )(ckv2d, kpe_slot2d, wts["w_kb"], wts["w_vb"], wts["kg"])


def _attend(q, ks, vts):
    head = lambda hd: slice(hd * HEAD_SLOT, (hd + 1) * HEAD_SLOT)
    st = [jnp.stack([_dot_nt(k[:, head(hd)], q[:, head(hd)]) for hd in range(N_HEADS)]) for k in ks]
    m = functools.reduce(jnp.maximum, [sj.max(axis=1, keepdims=True) for sj in st])
    p = [jnp.exp(sj - m) for sj in st]
    l = functools.reduce(lambda a, b: a + b, [pj.sum(axis=1, keepdims=True) for pj in p])
    outs = []
    for hd in range(N_HEADS):
        o = functools.reduce(lambda a, b: a + b,
                             [_dot(vt[hd * V_HEAD:(hd + 1) * V_HEAD, :], pj[hd].astype(BF16))
                              for vt, pj in zip(vts, p)])
        outs.append(o / l[hd])
    return jnp.concatenate(outs, axis=0).T.astype(BF16)


def _attn_kernel(*refs, n_kv):
    q_ref = refs[0]
    k_refs = refs[1:1 + n_kv]
    vt_refs = refs[1 + n_kv:1 + 2 * n_kv]
    o_ref = refs[1 + 2 * n_kv]
    o_ref[0] = _attend(q_ref.at[0], [k.at[0] for k in k_refs], vt_refs)


def _attention(q3, ks, vts, tq, name):
    b, sq, _ = q3.shape
    n_kv = len(ks)
    in_specs = [pl.BlockSpec((1, tq, QK_WIDTH), lambda bi, qi: (bi, qi, 0))]
    in_specs += [pl.BlockSpec((1, k.shape[1], QK_WIDTH), lambda bi, qi: (bi, 0, 0)) for k in ks]
    in_specs += [pl.BlockSpec((V_WIDTH, k.shape[1]), lambda bi, qi: (0, bi)) for k in ks]
    return pl.pallas_call(
        functools.partial(_attn_kernel, n_kv=n_kv),
        grid=(b, sq // tq),
        in_specs=in_specs,
        out_specs=pl.BlockSpec((1, tq, V_WIDTH), lambda bi, qi: (bi, qi, 0)),
        out_shape=jax.ShapeDtypeStruct((b, sq, V_WIDTH), BF16),
        compiler_params=_cparams(("parallel", "parallel")),
        name=name,
    )(q3, *ks, *vts)


def _fourier_kernel(cs_ref, ns_ref, f_ref, o_ref, *, n_b):
    for b in range(n_b):
        xc = f_ref[b, :, :FN_WIDTH]
        xs = f_ref[b, :, FN_WIDTH:]
        o_ref[b] = (_dot(cs_ref[...], xc) + _dot(ns_ref[...], xs)).astype(BF16)


def _fourier(fcs3, cs, ns, tr, n_b, name):
    b, s, _ = fcs3.shape
    return pl.pallas_call(
        functools.partial(_fourier_kernel, n_b=n_b),
        grid=(b // n_b, s // tr),
        in_specs=[pl.BlockSpec((tr, s), lambda bi, ri: (ri, 0)),
                  pl.BlockSpec((tr, s), lambda bi, ri: (ri, 0)),
                  pl.BlockSpec((n_b, s, 2 * FN_WIDTH), lambda bi, ri: (bi, 0, 0))],
        out_specs=pl.BlockSpec((n_b, tr, FN_WIDTH), lambda bi, ri: (bi, ri, 0)),
        out_shape=jax.ShapeDtypeStruct((b, s, FN_WIDTH), BF16),
        compiler_params=_cparams(("parallel", "parallel")),
        name=name,
    )(cs, ns, fcs3)


def _router_logits(w, b, h_hi, h_lo):
    rows = w.shape[0]
    w_hi, w_lo = _split_bf16(w)
    y = _dot_nt(jnp.concatenate([w_hi, w_lo], axis=0), h_hi)
    return (y[:rows] + y[rows:]) + _dot_nt(w_hi, h_lo) + b


def _rows(x, n):
    return [x[j:j + 1, :] for j in range(n)]


def _first_argmax(rows, top):
    idx = jnp.full(top.shape, len(rows) - 1, jnp.int32)
    for j in range(len(rows) - 2, -1, -1):
        idx = jnp.where(rows[j] == top, j, idx)
    return idx


def _store_token_major(ref, x, tm):
    for s in range(TOKEN_ROWS):
        ref[pl.ds(s, tm, stride=TOKEN_ROWS), :] = x[:, s * LANES:(s + 1) * LANES]


def _load_token_major(ref, tm):
    return jnp.concatenate([ref[pl.ds(s, tm, stride=TOKEN_ROWS), :] for s in range(TOKEN_ROWS)],
                           axis=1)


def _merge_kernel(*refs, n_first):
    first, second = refs[0:5], refs[5:10]
    (mod_ref, wao_ref, wfn_ref, wout_ref, g2_ref, wrg_ref, brg_ref, x1_ref, h2_ref, grp_ref, rank_ref,
     cnt_ref, carry_ref) = refs[10:]
    i = pl.program_id(0)

    @pl.when(i == 0)
    def _():
        carry_ref[...] = jnp.zeros_like(carry_ref)

    x, attn, fm, sga, sgf = (jnp.where(i < n_first, p[...], q[...]) for p, q in zip(first, second))
    a = _dot(attn, wao_ref[...])
    f = _dot(fm, wfn_ref[...])
    u = sga.astype(F32) * a + sgf.astype(F32) * f
    y = _dot(u.astype(BF16), wout_ref[...])
    x1 = x + mod_ref[0, 2:3, :] * y
    x1_ref[...] = x1
    tm = x1.shape[0]
    h2 = (x1 * _rms(x1, D_MODEL) * g2_ref[...]) * (1.0 + mod_ref[0, 4:5, :]) + mod_ref[0, 3:4, :]
    _store_token_major(h2_ref, h2, tm)

    h2_hi, h2_lo = _split_bf16(h2)
    g = _rows(_router_logits(wrg_ref[...], brg_ref[...], h2_hi, h2_lo), N_GROUPS)
    gidx = _first_argmax(g, functools.reduce(jnp.maximum, g))

    onehot = jnp.where(lax.broadcasted_iota(jnp.int32, (SEG_ROWS, tm), 0) == gidx, 1.0, 0.0)
    before = (lax.broadcasted_iota(jnp.int32, (tm, tm), 0)
              < lax.broadcasted_iota(jnp.int32, (tm, tm), 1))
    prefix = _dot(onehot.astype(BF16), jnp.where(before, 1.0, 0.0).astype(BF16))
    carry = carry_ref[...]
    rank = jnp.sum(onehot * (prefix + carry[:, 0:1]), axis=0, keepdims=True)
    grp_ref[...] = gidx
    rank_ref[...] = rank.astype(jnp.int32)
    carry = carry + jnp.sum(onehot, axis=1, keepdims=True)
    carry_ref[...] = carry
    cnt_ref[...] = carry.astype(jnp.int32)


def _merge(first, second, mod3, mod_row_fn, wts):
    n_first = first[0].shape[0] // TM
    n = first[0].shape[0] + second[0].shape[0]
    tok = lambda w: pl.BlockSpec((TM, w), lambda i: (i, 0))
    widths = (D_MODEL, V_WIDTH, FN_WIDTH, D_MODEL, D_MODEL)
    in_first = [pl.BlockSpec((TM, w), lambda i: (jnp.minimum(i, n_first - 1), 0)) for w in widths]
    in_second = [pl.BlockSpec((TM, w), lambda i: (jnp.maximum(i - n_first, 0), 0)) for w in widths]
    return pl.pallas_call(
        functools.partial(_merge_kernel, n_first=n_first),
        grid=(n // TM,),
        in_specs=in_first + in_second + [
                  pl.BlockSpec((1, 6, D_MODEL), lambda i: (mod_row_fn(i), 0, 0)),
                  _const_spec((V_WIDTH, D_MODEL)),
                  _const_spec((FN_WIDTH, D_MODEL)),
                  _const_spec((D_MODEL, D_MODEL)),
                  _const_spec((1, D_MODEL)),
                  _const_spec((ROUTER_ROWS, D_MODEL)),
                  _const_spec((ROUTER_ROWS, 1))],
        out_specs=[tok(D_MODEL),
                   pl.BlockSpec((TM * TOKEN_ROWS, LANES), lambda i: (i, 0)),
                   pl.BlockSpec((1, TM), lambda i: (0, i)),
                   pl.BlockSpec((1, TM), lambda i: (0, i)), _const_spec((SEG_ROWS, LANES))],
        out_shape=[jax.ShapeDtypeStruct((n, D_MODEL), F32),
                   jax.ShapeDtypeStruct((n * TOKEN_ROWS, LANES), F32),
                   jax.ShapeDtypeStruct((1, n), jnp.int32),
                   jax.ShapeDtypeStruct((1, n), jnp.int32),
                   jax.ShapeDtypeStruct((SEG_ROWS, LANES), jnp.int32)],
        scratch_shapes=[pltpu.VMEM((SEG_ROWS, LANES), F32)],
        compiler_params=_cparams(("arbitrary",)),
        name="merge",
    )(*first, *second, mod3, wts["w_ao"], wts["w_fn"], wts["w_out"], wts["g2"],
      wts["w_rg"], wts["b_rg"])


def _token_rows(ref, t):
    start = t * TOKEN_ROWS
    if not isinstance(t, int):
        start = pl.multiple_of(start, TOKEN_ROWS)
    return ref.at[pl.ds(start, TOKEN_ROWS)]


def _row_copy(src, dst, src_tok, dst_tok, sem):
    return pltpu.make_async_copy(_token_rows(src, src_tok), _token_rows(dst, dst_tok), sem)


def _wait_tile(src_hbm, buf, sem):
    pltpu.make_async_copy(src_hbm.at[pl.ds(0, buf.shape[0])], buf, sem).wait()


def _combine_kernel(pos_ref, y_hbm, x1_ref, mod_ref, o1_ref, o2_ref, buf, sem, *, n_first):
    i = pl.program_id(0)
    tm = o1_ref.shape[0]

    def gather(tile_idx, slot):
        def start(r, c):
            _row_copy(y_hbm, buf.at[slot], pos_ref[tile_idx * tm + r], r, sem.at[slot]).start()
            return c

        lax.fori_loop(0, tm, start, 0, unroll=32)

    @pl.when(i == 0)
    def _():
        gather(0, 0)

    @pl.when(i + 1 < pl.num_programs(0))
    def _():
        gather(i + 1, (i + 1) % 2)

    slot = i % 2

    _wait_tile(y_hbm, buf.at[slot], sem.at[slot])
    out = x1_ref[...] + mod_ref[0, 5:6, :] * _load_token_major(buf.at[slot], tm)

    @pl.when(i < n_first)
    def _():
        o1_ref[...] = out

    @pl.when(i >= n_first)
    def _():
        o2_ref[...] = out


def _combine(y_sorted, pos, x1, mod3, mod_row_fn, n_first_tokens):
    n = x1.shape[0]
    n_first = n_first_tokens // TM
    return pl.pallas_call(
        functools.partial(_combine_kernel, n_first=n_first),
        grid_spec=pltpu.PrefetchScalarGridSpec(
            num_scalar_prefetch=1,
            grid=(n // TM,),
            in_specs=[pl.BlockSpec(memory_space=pl.ANY),
                      pl.BlockSpec((TM, D_MODEL), lambda i, pos: (i, 0)),
                      pl.BlockSpec((1, 6, D_MODEL), lambda i, pos: (mod_row_fn(i), 0, 0))],
            out_specs=[pl.BlockSpec((TM, D_MODEL), lambda i, pos: (jnp.minimum(i, n_first - 1), 0)),
                       pl.BlockSpec((TM, D_MODEL), lambda i, pos: (jnp.maximum(i - n_first, 0), 0))],
            scratch_shapes=[pltpu.VMEM((2, TM * TOKEN_ROWS, LANES), F32),
                            pltpu.SemaphoreType.DMA((2,))]),
        out_shape=[jax.ShapeDtypeStruct((n_first_tokens, D_MODEL), F32),
                   jax.ShapeDtypeStruct((n - n_first_tokens, D_MODEL), F32)],
        compiler_params=_cparams(("arbitrary",)),
        name="moe_combine",
    )(pos, y_sorted, x1, mod3)


def _moe_kernel(grp_ref, on_ref, src_ref, h_hbm, wrg_ref, brg_ref, wre_ref, bre_ref, wg_ref, wu_ref,
                wd_ref, o_ref, hbuf, sem, *, tile):
    i = pl.program_id(0)

    def gather(tile_idx, slot):
        def start(r, c):
            _row_copy(h_hbm, hbuf.at[slot], src_ref[tile_idx * tile + r], r, sem.at[slot]).start()
            return c

        lax.fori_loop(0, tile, start, 0, unroll=32)

    @pl.when(jnp.logical_and(i == 0, on_ref[0] == 1))
    def _():
        gather(0, 0)

    nxt = jnp.minimum(i + 1, pl.num_programs(0) - 1)

    @pl.when(jnp.logical_and(i + 1 < pl.num_programs(0), on_ref[nxt] == 1))
    def _():
        gather(i + 1, (i + 1) % 2)

    @pl.when(on_ref[i] == 0)
    def _():
        o_ref[...] = jnp.zeros_like(o_ref)

    @pl.when(on_ref[i] == 1)
    def _():
        slot = i % 2

        _wait_tile(h_hbm, hbuf.at[slot], sem.at[slot])
        h2_hi, h2_lo = _split_bf16(_load_token_major(hbuf.at[slot], tile))

        logits = _router_logits(jnp.concatenate([wrg_ref[...], wre_ref[0]], axis=0),
                                jnp.concatenate([brg_ref[...], bre_ref[0]], axis=0), h2_hi, h2_lo)
        g = _rows(logits, N_GROUPS)
        gmax = functools.reduce(jnp.maximum, g)
        p_top = 1.0 / functools.reduce(lambda p, q: p + q, [jnp.exp(gj - gmax) for gj in g])
        e = _rows(logits[ROUTER_ROWS:], EXPERTS_PER_GROUP)
        m1 = functools.reduce(jnp.maximum, e)
        i1 = _first_argmax(e, m1)
        rest = [jnp.where(i1 == j, -jnp.inf, e[j]) for j in range(EXPERTS_PER_GROUP)]
        m2 = functools.reduce(jnp.maximum, rest)
        i2 = _first_argmax(rest, m2)
        t = jnp.exp(m2 - m1)
        w1 = p_top / (1.0 + t)
        w2 = p_top * t / (1.0 + t)
        row = lax.broadcasted_iota(jnp.int32, (LANES, tile), 0)
        comb = (jnp.where(row == i1, w1, 0.0) + jnp.where(row == i2, w2, 0.0)).T

        gates = [_dot(h2_hi, wg_ref[j]) for j in range(EXPERTS_PER_GROUP)]
        ups = [_dot(h2_hi, wu_ref[j]) for j in range(EXPERTS_PER_GROUP)]
        acts = [((a * _sigmoid(a)) * u * comb[:, j:j + 1]).astype(BF16)
                for j, (a, u) in enumerate(zip(gates, ups))]
        acc = functools.reduce(lambda p, q: p + q,
                               [_dot(acts[j], wd_ref[j]) for j in range(EXPERTS_PER_GROUP)])
        _store_token_major(o_ref, acc, tile)


def _moe(h2tm, src, maps, wts, w_gate, w_up, w_down, tile):
    n_tiles = src.shape[0] // tile
    const = lambda shape: pl.BlockSpec(shape, lambda i, grp, on, src: (0,) * len(shape))
    by_group = lambda shape: pl.BlockSpec(
        shape, lambda i, grp, on, src: (grp[i],) + (0,) * (len(shape) - 1))
    return pl.pallas_call(
        functools.partial(_moe_kernel, tile=tile),
        grid_spec=pltpu.PrefetchScalarGridSpec(
            num_scalar_prefetch=3,
            grid=(n_tiles,),
            in_specs=[pl.BlockSpec(memory_space=pl.ANY),
                      const((ROUTER_ROWS, D_MODEL)), const((ROUTER_ROWS, 1)),
                      by_group((1, ROUTER_ROWS, D_MODEL)), by_group((1, ROUTER_ROWS, 1)),
                      by_group((EXPERTS_PER_GROUP, D_MODEL, D_EXPERT)),
                      by_group((EXPERTS_PER_GROUP, D_MODEL, D_EXPERT)),
                      by_group((EXPERTS_PER_GROUP, D_EXPERT, D_MODEL))],
            out_specs=pl.BlockSpec((tile * TOKEN_ROWS, LANES), lambda i, grp, on, src: (i, 0)),
            scratch_shapes=[pltpu.VMEM((2, tile * TOKEN_ROWS, LANES), F32),
                            pltpu.SemaphoreType.DMA((2,))]),
        out_shape=jax.ShapeDtypeStruct((n_tiles * tile * TOKEN_ROWS, LANES), F32),
        compiler_params=_cparams(("arbitrary",)),
        name="moe",
    )(*maps, src, h2tm, wts["w_rg"], wts["b_rg"], wts["w_re"], wts["b_re"], w_gate, w_up, w_down)


def _invert_kernel(pos_ref, lo_ref, hi_ref, src_ref, *, n, n_fill):
    def zero(p, c):
        src_ref[p] = 0
        return c

    for s in range(n_fill):
        lax.fori_loop(lo_ref[s], hi_ref[s], zero, 0)

    def put(t, c):
        src_ref[pos_ref[t]] = t
        return c

    lax.fori_loop(0, n, put, 0, unroll=8)


def _invert(pos, fill_lo, fill_hi, n_sorted):
    return pl.pallas_call(
        functools.partial(_invert_kernel, n=pos.shape[0], n_fill=fill_lo.shape[0]),
        grid_spec=pltpu.PrefetchScalarGridSpec(
            num_scalar_prefetch=3, grid=(1,), in_specs=[],
            out_specs=pl.BlockSpec(memory_space=pltpu.SMEM)),
        out_shape=jax.ShapeDtypeStruct((n_sorted,), jnp.int32),
        name="moe_invert",
    )(pos, fill_lo, fill_hi)


def _sort_plan(grp, rank, cnt, tile, n_tiles):
    n_tile_grp = (cnt + tile - 1) // tile
    tile_end = jnp.cumsum(n_tile_grp)
    tile_start = tile_end - n_tile_grp
    total = tile_end[-1]
    pos = ((tile_start * tile)[grp] + rank).astype(jnp.int32)
    fill_lo = jnp.concatenate([tile_start * tile + cnt, total[None] * tile]).astype(jnp.int32)
    fill_hi = jnp.concatenate([tile_end * tile, jnp.full((1,), n_tiles * tile)]).astype(jnp.int32)
    src = _invert(pos, fill_lo, fill_hi, n_tiles * tile)
    i = jnp.arange(n_tiles, dtype=jnp.int32)
    tile_grp = jnp.sum((jnp.minimum(i, total - 1)[:, None] >= tile_end[None, :]).astype(jnp.int32),
                       axis=1)
    return pos, src, (tile_grp.astype(jnp.int32), (i < total).astype(jnp.int32))


def _sparse_moe(x1, h2tm, grp, rank, cnt8, mod3, mod_row_fn, wts, w_gate, w_up, w_down, tile,
                n_first_tokens):
    n = x1.shape[0]
    n_tiles = n // tile + N_GROUPS
    pos, src, maps = _sort_plan(grp.reshape(n), rank.reshape(n), cnt8[:N_GROUPS, 0], tile, n_tiles)
    y_sorted = _moe(h2tm, src, maps, wts, w_gate, w_up, w_down, tile)
    return _combine(y_sorted, pos, x1, mod3, mod_row_fn, n_first_tokens)


def _pack_weights(l, norm1_g, q_a_norm_g, w_q_b, kv_a_norm_g, w_kv_b, q_norm_g, k_norm_g,
                  w_attn_o, w_fnet, w_out, norm2_g, w_router_group, b_router_group,
                  w_router_expert, b_router_expert):
    w_qb = jnp.pad(w_q_b[l].reshape(Q_LORA, N_HEADS, QK_HEAD),
                   ((0, 0), (0, 0), (0, HEAD_SLOT - QK_HEAD))).reshape(Q_LORA, QK_WIDTH)
    wkv = w_kv_b[l].reshape(KV_LORA, N_HEADS, QK_NOPE + V_HEAD)
    w_kb = jnp.pad(wkv[:, :, :QK_NOPE],
                   ((0, 0), (0, 0), (0, HEAD_SLOT - QK_NOPE))).reshape(KV_LORA, QK_WIDTH)
    w_vb = wkv[:, :, QK_NOPE:].reshape(KV_LORA, V_WIDTH)
    pad_g = lambda g: jnp.pad(g, (0, HEAD_SLOT - QK_HEAD)).reshape(1, HEAD_SLOT)
    w_rg = jnp.pad(w_router_group[l].T, ((0, ROUTER_ROWS - N_GROUPS), (0, 0)))
    b_rg = jnp.pad(b_router_group[l], (0, ROUTER_ROWS - N_GROUPS)).reshape(ROUTER_ROWS, 1)
    w_re = jnp.pad(w_router_expert[l].T.reshape(N_GROUPS, EXPERTS_PER_GROUP, D_MODEL),
                   ((0, 0), (0, ROUTER_ROWS - EXPERTS_PER_GROUP), (0, 0)))
    b_re = jnp.pad(b_router_expert[l].reshape(N_GROUPS, EXPERTS_PER_GROUP),
                   ((0, 0), (0, ROUTER_ROWS - EXPERTS_PER_GROUP))).reshape(N_GROUPS, ROUTER_ROWS, 1)
    dft_c, dft_ns = _dft_tables(FN_GROUP_W)
    return {
        "g1": norm1_g[l].reshape(1, D_MODEL),
        "qag": q_a_norm_g[l].reshape(1, Q_LORA),
        "w_qb": w_qb.astype(BF16),
        "kvg": kv_a_norm_g[l].reshape(1, KV_LORA),
        "w_kb": w_kb.astype(BF16),
        "w_vb": w_vb.T.astype(BF16),
        "qg": pad_g(q_norm_g[l]),
        "kg": pad_g(k_norm_g[l]),
        "dft_c": jnp.concatenate([jnp.asarray(dft_c), -jnp.asarray(dft_ns)], axis=1).astype(BF16),
        "w_ao": w_attn_o[l].astype(BF16),
        "w_fn": w_fnet[l].astype(BF16),
        "w_out": w_out[l].astype(BF16),
        "g2": norm2_g[l].reshape(1, D_MODEL),
        "w_rg": w_rg,
        "b_rg": b_rg,
        "w_re": w_re,
        "b_re": b_re,
    }


def _layer(xp, xs, cache_ckv_l, cache_kpe_l, mod3, wts, experts):
    bp, sp, _ = xp.shape
    bs, ss, _ = xs.shape
    past = cache_ckv_l.shape[1]
    ctx_row = lambda i: 0
    lat_row = lambda i: 1 + i // (ss // TM)

    xp2 = xp.reshape(bp * sp, D_MODEL)
    ride = (bp * sp) // TM == N_EXPERTS
    cs, ns = (jnp.asarray(t).astype(BF16) for t in _dft_tables(sp))
    outs = _inproj(xp2, mod3, ctx_row, wts, None, True, experts if ride else (), (sp, cs, ns))
    attn, fm, sga, sgf, ckv, kpe = outs[:6]
    w_gate, w_up, w_down = outs[6:] if ride else (w.astype(BF16) for w in experts)
    ctx_set = (xp2, attn, fm, sga, sgf)

    xs2 = xs.reshape(bs * ss, D_MODEL)
    rope_tabs = tuple(jnp.asarray(t) for t in _rope_tables(ss))
    q, k, v, fcs, sga, sgf = _inproj(xs2, mod3, lat_row, wts, rope_tabs, False)
    kpe_slot = jnp.pad(cache_kpe_l, ((0, 0), (0, 0), (QK_NOPE, LANES - QK_HEAD)))
    kc, vc = _cache_kv(cache_ckv_l.reshape(bs * past, KV_LORA), kpe_slot.reshape(bs * past, LANES), wts)
    attn = _attention(q.reshape(bs, ss, QK_WIDTH),
                      [kc.reshape(bs, past, QK_WIDTH), k.reshape(bs, ss, QK_WIDTH)],
                      [vc, v], TQ_LAT, "attn_lat")
    cs, ns = (jnp.asarray(t).astype(BF16) for t in _dft_tables(ss))
    fm = _fourier(fcs.reshape(bs, ss, 2 * FN_WIDTH), cs, ns, TR_LAT, 1, "fourier_lat")
    lat_set = (xs2, attn.reshape(bs * ss, V_WIDTH), fm.reshape(bs * ss, FN_WIDTH), sga, sgf)

    n_ctx_tiles = (bp * sp) // TM
    all_row = lambda i: jnp.where(i < n_ctx_tiles, 0, lat_row(i - n_ctx_tiles))
    x1, h2tm, grp, rank, cnt = _merge(ctx_set, lat_set, mod3, all_row, wts)
    yp, ys = _sparse_moe(x1, h2tm, grp, rank, cnt, mod3, all_row, wts, w_gate, w_up, w_down, MOE_TILE,
                         bp * sp)
    yp, ys = yp.reshape(bp, sp, D_MODEL), ys.reshape(bs, ss, D_MODEL)

    return yp, ys, ckv.reshape(bp, sp, KV_LORA), jnp.swapaxes(kpe, 1, 2)


def kernel(x_prompt, x_sample, cache_ckv, cache_kpe, c, c_ctx, w_mod, b_mod, norm1_g, w_in, q_a_norm_g, w_q_b, kv_a_norm_g, w_kv_b, q_norm_g, k_norm_g, w_attn_o, w_fnet, w_out, norm2_g, w_router_group, b_router_group, w_router_expert, b_router_expert, w_exp_gate, w_exp_up, w_exp_down):
    depth = w_mod.shape[0]
    n_lat = c.shape[0]
    assert 1 + n_lat <= MOD_ROWS
    cond8 = jnp.concatenate([c_ctx[None, :], c, jnp.zeros((MOD_ROWS - 1 - n_lat, D_MODEL), F32)], axis=0)
    xp, xs = x_prompt, x_sample
    ckv_layers, kpe_layers = [], []
    for l in range(depth):
        mod, w_in_p = _adaln(cond8, w_mod[l], b_mod[l].reshape(1, -1), jnp.swapaxes(w_in[l], 0, 1))
        mod3 = mod.reshape(MOD_ROWS, 6, D_MODEL)
        wts = _pack_weights(l, norm1_g, q_a_norm_g, w_q_b, kv_a_norm_g, w_kv_b, q_norm_g,
                            k_norm_g, w_attn_o, w_fnet, w_out, norm2_g, w_router_group,
                            b_router_group, w_router_expert, b_router_expert)
        wts["w_in"] = w_in_p
        xp, xs, ckv, kpe = _layer(xp, xs, cache_ckv[:, l], cache_kpe[:, l], mod3, wts,
                                  (w_exp_gate[l], w_exp_up[l], w_exp_down[l]))
        ckv_layers.append(ckv)
        kpe_layers.append(kpe)
    return xp, xs, jnp.stack(ckv_layers, axis=1), jnp.stack(kpe_layers, axis=1)
```

```python
import functools
import math

import numpy as np
import jax
import jax.numpy as jnp
from jax import lax
from jax.experimental import pallas as pl
from jax.experimental.pallas import tpu as pltpu

D_MODEL = 1024
GRID_W = 64
N_HEADS = 8
Q_LORA = 512
KV_LORA = 256
QK_NOPE = 64
QK_ROPE = 32
V_HEAD = 64
QK_HEAD = QK_NOPE + QK_ROPE
ATTN_SCALE = QK_HEAD ** -0.5
ROPE_BASE = 10000.0
FN_GROUPS = 4
FN_GROUP_W = 128
FN_WIDTH = FN_GROUPS * FN_GROUP_W
N_GROUPS = 4
EXPERTS_PER_GROUP = 4
N_EXPERTS = N_GROUPS * EXPERTS_PER_GROUP
D_EXPERT = 512
EPS = 1e-6

LANES = 128
HEAD_SLOT = LANES
QK_WIDTH = N_HEADS * HEAD_SLOT
V_WIDTH = N_HEADS * V_HEAD
C_QA = 0
C_KVA = C_QA + Q_LORA
C_KPE = C_KVA + KV_LORA
C_FN = C_KPE + LANES
C_GA = C_FN + FN_WIDTH
C_GF = C_GA + D_MODEL
IN_PACKED = C_GF + D_MODEL
SUBLANES = 8
ROUTER_ROWS = SUBLANES
MOD_ROWS = SUBLANES
VMEM_LIMIT = 56 * 1024 * 1024

TM = 512
TQ_LAT = 256
TR_LAT = 512
SEG_ROWS = SUBLANES
TOKEN_ROWS = D_MODEL // LANES
MOE_TILE = 256
ADALN_STEPS = 8

BF16 = jnp.bfloat16
F32 = jnp.float32


def _cparams(sem):
    return pltpu.CompilerParams(dimension_semantics=sem, vmem_limit_bytes=VMEM_LIMIT)


def _dot(a, b):
    return jnp.dot(a, b, preferred_element_type=F32)


def _dot_nt(a, b):
    return lax.dot_general(a, b, (((1,), (1,)), ((), ())), preferred_element_type=F32)


def _sigmoid(x):
    return 1.0 / (1.0 + jnp.exp(-x))


def _split_bf16(x):
    hi = x.astype(BF16)
    return hi, (x - hi.astype(F32)).astype(BF16)


@functools.lru_cache(maxsize=None)
def _rope_tables(n_pos):
    half = QK_ROPE // 2
    quarter = half // 2
    freqs = ROPE_BASE ** (-np.arange(quarter, dtype=np.float64) / quarter)
    pos = np.arange(n_pos)
    row = (pos // GRID_W).astype(np.float64)
    col = (pos % GRID_W).astype(np.float64)
    cos_t = np.ones((n_pos, LANES), np.float64)
    sin_a = np.zeros((n_pos, LANES), np.float64)
    sin_b = np.zeros((n_pos, LANES), np.float64)
    for base, p in ((QK_NOPE, row), (QK_NOPE + half, col)):
        ang = p[:, None] * freqs[None, :]
        cos_t[:, base:base + quarter] = np.cos(ang)
        cos_t[:, base + quarter:base + half] = np.cos(ang)
        sin_a[:, base:base + quarter] = -np.sin(ang)
        sin_b[:, base + quarter:base + half] = np.sin(ang)
    return (cos_t.astype(np.float32), sin_a.astype(np.float32), sin_b.astype(np.float32))


@functools.lru_cache(maxsize=None)
def _dft_tables(n):
    k = np.arange(n)
    ang = 2.0 * np.pi * ((k[:, None] * k[None, :]) % n) / n
    s = 1.0 / math.sqrt(n)
    return (np.cos(ang) * s).astype(np.float32), (-np.sin(ang) * s).astype(np.float32)


def _adaln_kernel(cond_ref, w_ref, b_ref, win_ref, o_ref, winp_ref):
    c = cond_ref[...]
    s_hi, s_lo = _split_bf16(c * _sigmoid(c))
    w_hi, w_lo = _split_bf16(w_ref[...])
    y = _dot(jnp.concatenate([s_hi, s_lo], axis=0), w_hi)
    o_ref[...] = (y[:MOD_ROWS] + y[MOD_ROWS:]) + _dot(s_hi, w_lo) + b_ref[...]

    winp_ref[:C_KPE, :] = win_ref[:C_KPE, :].astype(BF16)
    winp_ref[C_KPE:C_FN, :] = jnp.zeros((LANES, winp_ref.shape[1]), BF16)
    winp_ref[C_KPE + QK_NOPE:C_KPE + QK_HEAD, :] = win_ref[C_KPE:C_KPE + QK_ROPE, :].astype(BF16)
    winp_ref[C_FN:, :] = win_ref[C_KPE + QK_ROPE:, :].astype(BF16)


def _adaln(cond8, w_mod, b_mod, w_in_t):
    n = w_mod.shape[1]
    steps = ADALN_STEPS
    tn, tr = n // steps, D_MODEL // steps
    return pl.pallas_call(
        _adaln_kernel,
        grid=(steps,),
        in_specs=[pl.BlockSpec((MOD_ROWS, D_MODEL), lambda j: (0, 0)),
                  pl.BlockSpec((D_MODEL, tn), lambda j: (0, j)),
                  pl.BlockSpec((1, tn), lambda j: (0, j)),
                  pl.BlockSpec((w_in_t.shape[0], tr), lambda j: (0, j))],
        out_specs=[pl.BlockSpec((MOD_ROWS, tn), lambda j: (0, j)),
                   pl.BlockSpec((IN_PACKED, tr), lambda j: (0, j))],
        out_shape=[jax.ShapeDtypeStruct((MOD_ROWS, n), F32),
                   jax.ShapeDtypeStruct((IN_PACKED, D_MODEL), BF16)],
        compiler_params=_cparams(("arbitrary",)),
        name="adaln",
    )(cond8, w_mod, b_mod, w_in_t)


def _rms(x, width):
    return lax.rsqrt(jnp.sum(x * x, axis=-1, keepdims=True) * (1.0 / width) + EPS)


def _rope(x, cos_t, sin_a, sin_b):
    return x * cos_t + pltpu.roll(x, LANES - 8, 1) * sin_a + pltpu.roll(x, 8, 1) * sin_b


def _inproj_kernel(*refs, rope, emit_cache, n_cast, seq_len):
    it = iter(refs)
    x_ref, mod_ref, g1_ref, win_ref, qag_ref, wqb_ref, kvg_ref, wkb_ref, wvb_ref = (
        next(it) for _ in range(9))
    qg_ref, kg_ref, dft_ref = next(it), next(it), next(it)
    if rope:
        cos_ref, sa_ref, sb_ref = next(it), next(it), next(it)
    if seq_len:
        seq_cs_ref, seq_ns_ref = next(it), next(it)
    cast_in = [next(it) for _ in range(n_cast)]
    if seq_len:
        attn_ref, fm_ref, sga_ref, sgf_ref = (next(it) for _ in range(4))
    else:
        q_ref, k_ref, v_ref, fcs_ref, sga_ref, sgf_ref = (next(it) for _ in range(6))
    if emit_cache:
        ckv_ref, kpe_ref = next(it), next(it)
    cast_out = [next(it) for _ in range(n_cast)]
    if seq_len:
        q_ref, k_ref, v_ref, fcs_ref = (next(it) for _ in range(4))

    for src, dst in zip(cast_in, cast_out):
        dst[...] = src[...].astype(BF16)

    x = x_ref[...]
    shift = mod_ref[0, 0:1, :]
    scale = mod_ref[0, 1:2, :]
    h = (x * _rms(x, D_MODEL) * g1_ref[...]) * (1.0 + scale) + shift
    hb = h.astype(BF16)

    if rope:
        cos_t, sin_a, sin_b = cos_ref[...], sa_ref[...], sb_ref[...]

    qa = _dot_nt(hb, win_ref[C_QA:C_QA + Q_LORA, :])
    qn = (qa * _rms(qa, Q_LORA) * qag_ref[...]).astype(BF16)
    q = _dot(qn, wqb_ref[...])
    qg = qg_ref[...] * ATTN_SCALE
    for hd in range(N_HEADS):
        qh = q[:, hd * HEAD_SLOT:(hd + 1) * HEAD_SLOT]
        qh = qh * _rms(qh, QK_HEAD) * qg
        if rope:
            qh = _rope(qh, cos_t, sin_a, sin_b)
        q_ref[:, hd * HEAD_SLOT:(hd + 1) * HEAD_SLOT] = qh.astype(BF16)

    kva = _dot_nt(hb, win_ref[C_KVA:C_KVA + KV_LORA, :])
    ckv = kva * _rms(kva, KV_LORA) * kvg_ref[...]
    kpe = _dot_nt(hb, win_ref[C_KPE:C_KPE + LANES, :])
    if emit_cache:
        ckv_ref[...] = ckv
        kpe_t = kpe.T[QK_NOPE:QK_NOPE + QK_ROPE, :]
        for j in range(x.shape[0] // seq_len):
            kpe_ref[j] = kpe_t[:, j * seq_len:(j + 1) * seq_len]
    _emit_kv(ckv.astype(BF16), kpe, wkb_ref, wvb_ref, kg_ref,
             (cos_t, sin_a, sin_b) if rope else None, k_ref, v_ref)

    fn = _dot_nt(hb, win_ref[C_FN:C_FN + FN_WIDTH, :]).astype(BF16)
    for g in range(FN_GROUPS):
        cs = _dot(fn[:, g * FN_GROUP_W:(g + 1) * FN_GROUP_W], dft_ref[...])
        fcs_ref[:, g * FN_GROUP_W:(g + 1) * FN_GROUP_W] = cs[:, :FN_GROUP_W].astype(BF16)
        fcs_ref[:, FN_WIDTH + g * FN_GROUP_W:FN_WIDTH + (g + 1) * FN_GROUP_W] = (
            cs[:, FN_GROUP_W:].astype(BF16))

    sga_ref[...] = _sigmoid(_dot_nt(hb, win_ref[C_GA:C_GA + D_MODEL, :])).astype(BF16)
    sgf_ref[...] = _sigmoid(_dot_nt(hb, win_ref[C_GF:C_GF + D_MODEL, :])).astype(BF16)

    if seq_len:
        for s0 in range(0, x.shape[0], seq_len):
            rows = pl.ds(s0, seq_len)
            attn_ref[rows, :] = _attend(q_ref.at[rows], [k_ref.at[rows]], [v_ref.at[:, rows]])
            fm_ref[rows, :] = (_dot(seq_cs_ref[...], fcs_ref[rows, :FN_WIDTH])
                               + _dot(seq_ns_ref[...], fcs_ref[rows, FN_WIDTH:])).astype(BF16)


def _emit_kv(ckvb, kpe, wkb_ref, wvb_ref, kg_ref, rope_tabs, k_ref, v_ref):
    kg = kg_ref[...]
    v_ref[...] = _dot_nt(wvb_ref[...], ckvb).astype(BF16)
    kn = _dot(ckvb, wkb_ref[...])
    pe_ss = jnp.sum(kpe * kpe, axis=-1, keepdims=True)
    pe_g = kpe * kg
    if rope_tabs is not None:
        pe_g = _rope(pe_g, *rope_tabs)
    for hd in range(N_HEADS):
        knh = kn[:, hd * HEAD_SLOT:(hd + 1) * HEAD_SLOT]
        ss = jnp.sum(knh * knh, axis=-1, keepdims=True) + pe_ss
        r = lax.rsqrt(ss * (1.0 / QK_HEAD) + EPS)
        k_ref[:, hd * HEAD_SLOT:(hd + 1) * HEAD_SLOT] = ((knh * kg + pe_g) * r).astype(BF16)


def _const_spec(shape):
    return pl.BlockSpec(shape, lambda i: (0,) * len(shape))


def _inproj(x2d, mod3, mod_row_fn, wts, rope_tabs, emit_cache, cast=(), seq=None):
    n = x2d.shape[0]
    assert all(w.shape[0] == n // TM for w in cast)
    assert seq is None or TM % seq[0] == 0
    rope = rope_tabs is not None
    tiles_per_seq = None if not rope else rope_tabs[0].shape[0] // TM
    in_specs = [pl.BlockSpec((TM, D_MODEL), lambda i: (i, 0)),
                pl.BlockSpec((1, 6, D_MODEL), lambda i: (mod_row_fn(i), 0, 0)),
                _const_spec((1, D_MODEL)),
                _const_spec((IN_PACKED, D_MODEL)),
                _const_spec((1, Q_LORA)),
                _const_spec((Q_LORA, QK_WIDTH)),
                _const_spec((1, KV_LORA)),
                _const_spec((KV_LORA, QK_WIDTH)),
                _const_spec((V_WIDTH, KV_LORA)),
                _const_spec((1, HEAD_SLOT)),
                _const_spec((1, HEAD_SLOT)),
                _const_spec((FN_GROUP_W, 2 * FN_GROUP_W))]
    args = [x2d, mod3, wts["g1"], wts["w_in"], wts["qag"], wts["w_qb"], wts["kvg"],
            wts["w_kb"], wts["w_vb"], wts["qg"], wts["kg"], wts["dft_c"]]
    if rope:
        in_specs += [pl.BlockSpec((TM, LANES), lambda i: (i % tiles_per_seq, 0))] * 3
        args += list(rope_tabs)
    if seq is not None:
        in_specs += [_const_spec((seq[0], seq[0]))] * 2
        args += [seq[1], seq[2]]
    cast_specs = [pl.BlockSpec((1,) + w.shape[1:], lambda i: (i, 0, 0)) for w in cast]
    in_specs += cast_specs
    args += list(cast)
    qkvf = [((TM, QK_WIDTH), (n, QK_WIDTH)), ((TM, QK_WIDTH), (n, QK_WIDTH)),
            ((V_WIDTH, TM), (V_WIDTH, n)), ((TM, 2 * FN_WIDTH), (n, 2 * FN_WIDTH))]
    if seq is None:
        widths = [None] * 4 + [D_MODEL, D_MODEL]
    else:
        widths = [V_WIDTH, FN_WIDTH, D_MODEL, D_MODEL]
    out_shape, out_specs = [], []
    for j, w in enumerate(widths):
        if w is None:
            out_shape.append(jax.ShapeDtypeStruct(qkvf[j][1], BF16))
            out_specs.append(pl.BlockSpec(qkvf[j][0], (lambda i: (0, i)) if j == 2 else (lambda i: (i, 0))))
        else:
            out_shape.append(jax.ShapeDtypeStruct((n, w), BF16))
            out_specs.append(pl.BlockSpec((TM, w), lambda i: (i, 0)))
    if emit_cache:
        assert seq is not None
        out_shape += [jax.ShapeDtypeStruct((n, KV_LORA), F32),
                      jax.ShapeDtypeStruct((n // seq[0], QK_ROPE, seq[0]), F32)]
        out_specs += [pl.BlockSpec((TM, KV_LORA), lambda i: (i, 0)),
                      pl.BlockSpec((TM // seq[0], QK_ROPE, seq[0]), lambda i: (i, 0, 0))]
    out_shape += [jax.ShapeDtypeStruct(w.shape, BF16) for w in cast]
    out_specs += cast_specs
    scratch = [] if seq is None else [pltpu.VMEM(blk, BF16) for blk, _ in qkvf]
    return pl.pallas_call(
        functools.partial(_inproj_kernel, rope=rope, emit_cache=emit_cache, n_cast=len(cast),
                          seq_len=None if seq is None else seq[0]),
        grid=(n // TM,),
        in_specs=in_specs,
        out_specs=out_specs,
        out_shape=out_shape,
        scratch_shapes=scratch,
        compiler_params=_cparams(("parallel",)),
        name="inproj_lat" if rope else "inproj_ctx",
    )(*args)


def _cache_kv_kernel(ckv_ref, kpe_ref, wkb_ref, wvb_ref, kg_ref, k_ref, v_ref):
    _emit_kv(ckv_ref[...].astype(BF16), kpe_ref[...], wkb_ref, wvb_ref, kg_ref, None, k_ref, v_ref)


def _cache_kv(ckv2d, kpe_slot2d, wts):
    n = ckv2d.shape[0]
    return pl.pallas_call(
        _cache_kv_kernel,
        grid=(n // TM,),
        in_specs=[pl.BlockSpec((TM, KV_LORA), lambda i: (i, 0)),
                  pl.BlockSpec((TM, LANES), lambda i: (i, 0)),
                  _const_spec((KV_LORA, QK_WIDTH)),
                  _const_spec((V_WIDTH, KV_LORA)),
                  _const_spec((1, HEAD_SLOT))],
        out_specs=[pl.BlockSpec((TM, QK_WIDTH), lambda i: (i, 0)),
                   pl.BlockSpec((V_WIDTH, TM), lambda i: (0, i))],
        out_shape=[jax.ShapeDtypeStruct((n, QK_WIDTH), BF16),
                   jax.ShapeDtypeStruct((V_WIDTH, n), BF16)],
        compiler_params=_cparams(("parallel",)),
        name="cache_kv",
    )(ckv2d, kpe_slot2d, wts["w_kb"], wts["w_vb"], wts["kg"])


def _attend(q, ks, vts):
    head = lambda hd: slice(hd * HEAD_SLOT, (hd + 1) * HEAD_SLOT)
    st = [jnp.stack([_dot_nt(k[:, head(hd)], q[:, head(hd)]) for hd in range(N_HEADS)]) for k in ks]
    m = functools.reduce(jnp.maximum, [sj.max(axis=1, keepdims=True) for sj in st])
    p = [jnp.exp(sj - m) for sj in st]
    l = functools.reduce(lambda a, b: a + b, [pj.sum(axis=1, keepdims=True) for pj in p])
    outs = []
    for hd in range(N_HEADS):
        o = functools.reduce(lambda a, b: a + b,
                             [_dot(vt[hd * V_HEAD:(hd + 1) * V_HEAD, :], pj[hd].astype(BF16))
                              for vt, pj in zip(vts, p)])
        outs.append(o / l[hd])
    return jnp.concatenate(outs, axis=0).T.astype(BF16)


def _attn_kernel(*refs, n_kv):
    q_ref = refs[0]
    k_refs = refs[1:1 + n_kv]
    vt_refs = refs[1 + n_kv:1 + 2 * n_kv]
    o_ref = refs[1 + 2 * n_kv]
    o_ref[0] = _attend(q_ref.at[0], [k.at[0] for k in k_refs], vt_refs)


def _attention(q3, ks, vts, tq, name):
    b, sq, _ = q3.shape
    n_kv = len(ks)
    in_specs = [pl.BlockSpec((1, tq, QK_WIDTH), lambda bi, qi: (bi, qi, 0))]
    in_specs += [pl.BlockSpec((1, k.shape[1], QK_WIDTH), lambda bi, qi: (bi, 0, 0)) for k in ks]
    in_specs += [pl.BlockSpec((V_WIDTH, k.shape[1]), lambda bi, qi: (0, bi)) for k in ks]
    return pl.pallas_call(
        functools.partial(_attn_kernel, n_kv=n_kv),
        grid=(b, sq // tq),
        in_specs=in_specs,
        out_specs=pl.BlockSpec((1, tq, V_WIDTH), lambda bi, qi: (bi, qi, 0)),
        out_shape=jax.ShapeDtypeStruct((b, sq, V_WIDTH), BF16),
        compiler_params=_cparams(("parallel", "parallel")),
        name=name,
    )(q3, *ks, *vts)


def _fourier_kernel(cs_ref, ns_ref, f_ref, o_ref, *, n_b):
    for b in range(n_b):
        xc = f_ref[b, :, :FN_WIDTH]
        xs = f_ref[b, :, FN_WIDTH:]
        o_ref[b] = (_dot(cs_ref[...], xc) + _dot(ns_ref[...], xs)).astype(BF16)


def _fourier(fcs3, cs, ns, tr, n_b, name):
    b, s, _ = fcs3.shape
    return pl.pallas_call(
        functools.partial(_fourier_kernel, n_b=n_b),
        grid=(b // n_b, s // tr),
        in_specs=[pl.BlockSpec((tr, s), lambda bi, ri: (ri, 0)),
                  pl.BlockSpec((tr, s), lambda bi, ri: (ri, 0)),
                  pl.BlockSpec((n_b, s, 2 * FN_WIDTH), lambda bi, ri: (bi, 0, 0))],
        out_specs=pl.BlockSpec((n_b, tr, FN_WIDTH), lambda bi, ri: (bi, ri, 0)),
        out_shape=jax.ShapeDtypeStruct((b, s, FN_WIDTH), BF16),
        compiler_params=_cparams(("parallel", "parallel")),
        name=name,
    )(cs, ns, fcs3)


def _router_logits(w, b, h_hi, h_lo):
    rows = w.shape[0]
    w_hi, w_lo = _split_bf16(w)
    y = _dot_nt(jnp.concatenate([w_hi, w_lo], axis=0), h_hi)
    return (y[:rows] + y[rows:]) + _dot_nt(w_hi, h_lo) + b


def _rows(x, n):
    return [x[j:j + 1, :] for j in range(n)]


def _first_argmax(rows, top):
    idx = jnp.full(top.shape, len(rows) - 1, jnp.int32)
    for j in range(len(rows) - 2, -1, -1):
        idx = jnp.where(rows[j] == top, j, idx)
    return idx


def _store_token_major(ref, x, tm):
    for s in range(TOKEN_ROWS):
        ref[pl.ds(s, tm, stride=TOKEN_ROWS), :] = x[:, s * LANES:(s + 1) * LANES]


def _load_token_major(ref, tm):
    return jnp.concatenate([ref[pl.ds(s, tm, stride=TOKEN_ROWS), :] for s in range(TOKEN_ROWS)],
                           axis=1)


def _merge_kernel(*refs, n_first):
    first, second = refs[0:5], refs[5:10]
    (mod_ref, wao_ref, wfn_ref, wout_ref, g2_ref, wrg_ref, brg_ref, x1_ref, h2_ref, grp_ref, rank_ref,
     cnt_ref, carry_ref) = refs[10:]
    i = pl.program_id(0)

    @pl.when(i == 0)
    def _():
        carry_ref[...] = jnp.zeros_like(carry_ref)

    def tile(src):
        x, attn, fm, sga, sgf = (r[...] for r in src)
        a = _dot(attn, wao_ref[...])
        f = _dot(fm, wfn_ref[...])
        u = sga.astype(F32) * a + sgf.astype(F32) * f
        y = _dot(u.astype(BF16), wout_ref[...])
        x1 = x + mod_ref[0, 2:3, :] * y
        x1_ref[...] = x1
        tm = x1.shape[0]
        h2 = (x1 * _rms(x1, D_MODEL) * g2_ref[...]) * (1.0 + mod_ref[0, 4:5, :]) + mod_ref[0, 3:4, :]
        _store_token_major(h2_ref, h2, tm)

        h2_hi, h2_lo = _split_bf16(h2)
        g = _rows(_router_logits(wrg_ref[...], brg_ref[...], h2_hi, h2_lo), N_GROUPS)
        gidx = _first_argmax(g, functools.reduce(jnp.maximum, g))

        onehot = jnp.where(lax.broadcasted_iota(jnp.int32, (SEG_ROWS, tm), 0) == gidx, 1.0, 0.0)
        before = (lax.broadcasted_iota(jnp.int32, (tm, tm), 0)
                  < lax.broadcasted_iota(jnp.int32, (tm, tm), 1))
        prefix = _dot(onehot.astype(BF16), jnp.where(before, 1.0, 0.0).astype(BF16))
        carry = carry_ref[...]
        rank = jnp.sum(onehot * (prefix + carry[:, 0:1]), axis=0, keepdims=True)
        grp_ref[...] = gidx
        rank_ref[...] = rank.astype(jnp.int32)
        carry = carry + jnp.sum(onehot, axis=1, keepdims=True)
        carry_ref[...] = carry
        cnt_ref[...] = carry.astype(jnp.int32)

    @pl.when(i < n_first)
    def _():
        tile(first)

    @pl.when(i >= n_first)
    def _():
        tile(second)


def _merge(first, second, mod3, mod_row_fn, wts):
    n_first = first[0].shape[0] // TM
    n = first[0].shape[0] + second[0].shape[0]
    tok = lambda w: pl.BlockSpec((TM, w), lambda i: (i, 0))
    widths = (D_MODEL, V_WIDTH, FN_WIDTH, D_MODEL, D_MODEL)
    in_first = [pl.BlockSpec((TM, w), lambda i: (jnp.minimum(i, n_first - 1), 0)) for w in widths]
    in_second = [pl.BlockSpec((TM, w), lambda i: (jnp.maximum(i - n_first, 0), 0)) for w in widths]
    return pl.pallas_call(
        functools.partial(_merge_kernel, n_first=n_first),
        grid=(n // TM,),
        in_specs=in_first + in_second + [
                  pl.BlockSpec((1, 6, D_MODEL), lambda i: (mod_row_fn(i), 0, 0)),
                  _const_spec((V_WIDTH, D_MODEL)),
                  _const_spec((FN_WIDTH, D_MODEL)),
                  _const_spec((D_MODEL, D_MODEL)),
                  _const_spec((1, D_MODEL)),
                  _const_spec((ROUTER_ROWS, D_MODEL)),
                  _const_spec((ROUTER_ROWS, 1))],
        out_specs=[tok(D_MODEL),
                   pl.BlockSpec((TM * TOKEN_ROWS, LANES), lambda i: (i, 0)),
                   pl.BlockSpec((1, TM), lambda i: (0, i)),
                   pl.BlockSpec((1, TM), lambda i: (0, i)), _const_spec((SEG_ROWS, LANES))],
        out_shape=[jax.ShapeDtypeStruct((n, D_MODEL), F32),
                   jax.ShapeDtypeStruct((n * TOKEN_ROWS, LANES), F32),
                   jax.ShapeDtypeStruct((1, n), jnp.int32),
                   jax.ShapeDtypeStruct((1, n), jnp.int32),
                   jax.ShapeDtypeStruct((SEG_ROWS, LANES), jnp.int32)],
        scratch_shapes=[pltpu.VMEM((SEG_ROWS, LANES), F32)],
        compiler_params=_cparams(("arbitrary",)),
        name="merge",
    )(*first, *second, mod3, wts["w_ao"], wts["w_fn"], wts["w_out"], wts["g2"],
      wts["w_rg"], wts["b_rg"])


def _token_rows(ref, t):
    start = t * TOKEN_ROWS
    if not isinstance(t, int):
        start = pl.multiple_of(start, TOKEN_ROWS)
    return ref.at[pl.ds(start, TOKEN_ROWS)]


def _row_copy(src, dst, src_tok, dst_tok, sem):
    return pltpu.make_async_copy(_token_rows(src, src_tok), _token_rows(dst, dst_tok), sem)


def _wait_tile(src_hbm, buf, sem):
    pltpu.make_async_copy(src_hbm.at[pl.ds(0, buf.shape[0])], buf, sem).wait()


def _combine_kernel(pos_ref, y_hbm, x1_ref, mod_ref, o1_ref, o2_ref, buf, sem, *, n_first):
    i = pl.program_id(0)
    tm = o1_ref.shape[0]

    def gather(tile_idx, slot):
        def start(r, c):
            _row_copy(y_hbm, buf.at[slot], pos_ref[tile_idx * tm + r], r, sem.at[slot]).start()
            return c

        lax.fori_loop(0, tm, start, 0, unroll=32)

    @pl.when(i == 0)
    def _():
        gather(0, 0)

    @pl.when(i + 1 < pl.num_programs(0))
    def _():
        gather(i + 1, (i + 1) % 2)

    slot = i % 2

    _wait_tile(y_hbm, buf.at[slot], sem.at[slot])
    out = x1_ref[...] + mod_ref[0, 5:6, :] * _load_token_major(buf.at[slot], tm)

    @pl.when(i < n_first)
    def _():
        o1_ref[...] = out

    @pl.when(i >= n_first)
    def _():
        o2_ref[...] = out


def _combine(y_sorted, pos, x1, mod3, mod_row_fn, n_first_tokens):
    n = x1.shape[0]
    n_first = n_first_tokens // TM
    return pl.pallas_call(
        functools.partial(_combine_kernel, n_first=n_first),
        grid_spec=pltpu.PrefetchScalarGridSpec(
            num_scalar_prefetch=1,
            grid=(n // TM,),
            in_specs=[pl.BlockSpec(memory_space=pl.ANY),
                      pl.BlockSpec((TM, D_MODEL), lambda i, pos: (i, 0)),
                      pl.BlockSpec((1, 6, D_MODEL), lambda i, pos: (mod_row_fn(i), 0, 0))],
            out_specs=[pl.BlockSpec((TM, D_MODEL), lambda i, pos: (jnp.minimum(i, n_first - 1), 0)),
                       pl.BlockSpec((TM, D_MODEL), lambda i, pos: (jnp.maximum(i - n_first, 0), 0))],
            scratch_shapes=[pltpu.VMEM((2, TM * TOKEN_ROWS, LANES), F32),
                            pltpu.SemaphoreType.DMA((2,))]),
        out_shape=[jax.ShapeDtypeStruct((n_first_tokens, D_MODEL), F32),
                   jax.ShapeDtypeStruct((n - n_first_tokens, D_MODEL), F32)],
        compiler_params=_cparams(("arbitrary",)),
        name="moe_combine",
    )(pos, y_sorted, x1, mod3)


def _moe_kernel(grp_ref, on_ref, src_ref, h_hbm, wrg_ref, brg_ref, wre_ref, bre_ref, wg_ref, wu_ref,
                wd_ref, o_ref, hbuf, sem, *, tile):
    i = pl.program_id(0)

    def gather(tile_idx, slot):
        def start(r, c):
            _row_copy(h_hbm, hbuf.at[slot], src_ref[tile_idx * tile + r], r, sem.at[slot]).start()
            return c

        lax.fori_loop(0, tile, start, 0, unroll=32)

    @pl.when(jnp.logical_and(i == 0, on_ref[0] == 1))
    def _():
        gather(0, 0)

    nxt = jnp.minimum(i + 1, pl.num_programs(0) - 1)

    @pl.when(jnp.logical_and(i + 1 < pl.num_programs(0), on_ref[nxt] == 1))
    def _():
        gather(i + 1, (i + 1) % 2)

    @pl.when(on_ref[i] == 0)
    def _():
        o_ref[...] = jnp.zeros_like(o_ref)

    @pl.when(on_ref[i] == 1)
    def _():
        slot = i % 2

        _wait_tile(h_hbm, hbuf.at[slot], sem.at[slot])
        h2_hi, h2_lo = _split_bf16(_load_token_major(hbuf.at[slot], tile))

        logits = _router_logits(jnp.concatenate([wrg_ref[...], wre_ref[0]], axis=0),
                                jnp.concatenate([brg_ref[...], bre_ref[0]], axis=0), h2_hi, h2_lo)
        g = _rows(logits, N_GROUPS)
        gmax = functools.reduce(jnp.maximum, g)
        p_top = 1.0 / functools.reduce(lambda p, q: p + q, [jnp.exp(gj - gmax) for gj in g])
        e = _rows(logits[ROUTER_ROWS:], EXPERTS_PER_GROUP)
        m1 = functools.reduce(jnp.maximum, e)
        i1 = _first_argmax(e, m1)
        rest = [jnp.where(i1 == j, -jnp.inf, e[j]) for j in range(EXPERTS_PER_GROUP)]
        m2 = functools.reduce(jnp.maximum, rest)
        i2 = _first_argmax(rest, m2)
        t = jnp.exp(m2 - m1)
        w1 = p_top / (1.0 + t)
        w2 = p_top * t / (1.0 + t)
        row = lax.broadcasted_iota(jnp.int32, (LANES, tile), 0)
        comb = (jnp.where(row == i1, w1, 0.0) + jnp.where(row == i2, w2, 0.0)).T

        gates = [_dot(h2_hi, wg_ref[j]) for j in range(EXPERTS_PER_GROUP)]
        ups = [_dot(h2_hi, wu_ref[j]) for j in range(EXPERTS_PER_GROUP)]
        acts = [((a * _sigmoid(a)) * u * comb[:, j:j + 1]).astype(BF16)
                for j, (a, u) in enumerate(zip(gates, ups))]
        acc = functools.reduce(lambda p, q: p + q,
                               [_dot(acts[j], wd_ref[j]) for j in range(EXPERTS_PER_GROUP)])
        _store_token_major(o_ref, acc, tile)


def _moe(h2tm, src, maps, wts, w_gate, w_up, w_down, tile):
    n_tiles = src.shape[0] // tile
    const = lambda shape: pl.BlockSpec(shape, lambda i, grp, on, src: (0,) * len(shape))
    by_group = lambda shape: pl.BlockSpec(
        shape, lambda i, grp, on, src: (grp[i],) + (0,) * (len(shape) - 1))
    return pl.pallas_call(
        functools.partial(_moe_kernel, tile=tile),
        grid_spec=pltpu.PrefetchScalarGridSpec(
            num_scalar_prefetch=3,
            grid=(n_tiles,),
            in_specs=[pl.BlockSpec(memory_space=pl.ANY),
                      const((ROUTER_ROWS, D_MODEL)), const((ROUTER_ROWS, 1)),
                      by_group((1, ROUTER_ROWS, D_MODEL)), by_group((1, ROUTER_ROWS, 1)),
                      by_group((EXPERTS_PER_GROUP, D_MODEL, D_EXPERT)),
                      by_group((EXPERTS_PER_GROUP, D_MODEL, D_EXPERT)),
                      by_group((EXPERTS_PER_GROUP, D_EXPERT, D_MODEL))],
            out_specs=pl.BlockSpec((tile * TOKEN_ROWS, LANES), lambda i, grp, on, src: (i, 0)),
            scratch_shapes=[pltpu.VMEM((2, tile * TOKEN_ROWS, LANES), F32),
                            pltpu.SemaphoreType.DMA((2,))]),
        out_shape=jax.ShapeDtypeStruct((n_tiles * tile * TOKEN_ROWS, LANES), F32),
        compiler_params=_cparams(("arbitrary",)),
        name="moe",
    )(*maps, src, h2tm, wts["w_rg"], wts["b_rg"], wts["w_re"], wts["b_re"], w_gate, w_up, w_down)


def _invert_kernel(pos_ref, lo_ref, hi_ref, src_ref, *, n, n_fill):
    def zero(p, c):
        src_ref[p] = 0
        return c

    for s in range(n_fill):
        lax.fori_loop(lo_ref[s], hi_ref[s], zero, 0)

    def put(t, c):
        src_ref[pos_ref[t]] = t
        return c

    lax.fori_loop(0, n, put, 0, unroll=8)


def _invert(pos, fill_lo, fill_hi, n_sorted):
    return pl.pallas_call(
        functools.partial(_invert_kernel, n=pos.shape[0], n_fill=fill_lo.shape[0]),
        grid_spec=pltpu.PrefetchScalarGridSpec(
            num_scalar_prefetch=3, grid=(1,), in_specs=[],
            out_specs=pl.BlockSpec(memory_space=pltpu.SMEM)),
        out_shape=jax.ShapeDtypeStruct((n_sorted,), jnp.int32),
        name="moe_invert",
    )(pos, fill_lo, fill_hi)


def _sort_plan(grp, rank, cnt, tile, n_tiles):
    n_tile_grp = (cnt + tile - 1) // tile
    tile_end = jnp.cumsum(n_tile_grp)
    tile_start = tile_end - n_tile_grp
    total = tile_end[-1]
    pos = ((tile_start * tile)[grp] + rank).astype(jnp.int32)
    fill_lo = jnp.concatenate([tile_start * tile + cnt, total[None] * tile]).astype(jnp.int32)
    fill_hi = jnp.concatenate([tile_end * tile, jnp.full((1,), n_tiles * tile)]).astype(jnp.int32)
    src = _invert(pos, fill_lo, fill_hi, n_tiles * tile)
    i = jnp.arange(n_tiles, dtype=jnp.int32)
    tile_grp = jnp.sum((jnp.minimum(i, total - 1)[:, None] >= tile_end[None, :]).astype(jnp.int32),
                       axis=1)
    return pos, src, (tile_grp.astype(jnp.int32), (i < total).astype(jnp.int32))


def _sparse_moe(x1, h2tm, grp, rank, cnt8, mod3, mod_row_fn, wts, w_gate, w_up, w_down, tile,
                n_first_tokens):
    n = x1.shape[0]
    n_tiles = n // tile + N_GROUPS
    pos, src, maps = _sort_plan(grp.reshape(n), rank.reshape(n), cnt8[:N_GROUPS, 0], tile, n_tiles)
    y_sorted = _moe(h2tm, src, maps, wts, w_gate, w_up, w_down, tile)
    return _combine(y_sorted, pos, x1, mod3, mod_row_fn, n_first_tokens)


def _pack_weights(l, norm1_g, q_a_norm_g, w_q_b, kv_a_norm_g, w_kv_b, q_norm_g, k_norm_g,
                  w_attn_o, w_fnet, w_out, norm2_g, w_router_group, b_router_group,
                  w_router_expert, b_router_expert):
    w_qb = jnp.pad(w_q_b[l].reshape(Q_LORA, N_HEADS, QK_HEAD),
                   ((0, 0), (0, 0), (0, HEAD_SLOT - QK_HEAD))).reshape(Q_LORA, QK_WIDTH)
    wkv = w_kv_b[l].reshape(KV_LORA, N_HEADS, QK_NOPE + V_HEAD)
    w_kb = jnp.pad(wkv[:, :, :QK_NOPE],
                   ((0, 0), (0, 0), (0, HEAD_SLOT - QK_NOPE))).reshape(KV_LORA, QK_WIDTH)
    w_vb = wkv[:, :, QK_NOPE:].reshape(KV_LORA, V_WIDTH)
    pad_g = lambda g: jnp.pad(g, (0, HEAD_SLOT - QK_HEAD)).reshape(1, HEAD_SLOT)
    w_rg = jnp.pad(w_router_group[l].T, ((0, ROUTER_ROWS - N_GROUPS), (0, 0)))
    b_rg = jnp.pad(b_router_group[l], (0, ROUTER_ROWS - N_GROUPS)).reshape(ROUTER_ROWS, 1)
    w_re = jnp.pad(w_router_expert[l].T.reshape(N_GROUPS, EXPERTS_PER_GROUP, D_MODEL),
                   ((0, 0), (0, ROUTER_ROWS - EXPERTS_PER_GROUP), (0, 0)))
    b_re = jnp.pad(b_router_expert[l].reshape(N_GROUPS, EXPERTS_PER_GROUP),
                   ((0, 0), (0, ROUTER_ROWS - EXPERTS_PER_GROUP))).reshape(N_GROUPS, ROUTER_ROWS, 1)
    dft_c, dft_ns = _dft_tables(FN_GROUP_W)
    return {
        "g1": norm1_g[l].reshape(1, D_MODEL),
        "qag": q_a_norm_g[l].reshape(1, Q_LORA),
        "w_qb": w_qb.astype(BF16),
        "kvg": kv_a_norm_g[l].reshape(1, KV_LORA),
        "w_kb": w_kb.astype(BF16),
        "w_vb": w_vb.T.astype(BF16),
        "qg": pad_g(q_norm_g[l]),
        "kg": pad_g(k_norm_g[l]),
        "dft_c": jnp.concatenate([jnp.asarray(dft_c), -jnp.asarray(dft_ns)], axis=1).astype(BF16),
        "w_ao": w_attn_o[l].astype(BF16),
        "w_fn": w_fnet[l].astype(BF16),
        "w_out": w_out[l].astype(BF16),
        "g2": norm2_g[l].reshape(1, D_MODEL),
        "w_rg": w_rg,
        "b_rg": b_rg,
        "w_re": w_re,
        "b_re": b_re,
    }


def _layer(xp, xs, cache_ckv_l, cache_kpe_l, mod3, wts, experts):
    bp, sp, _ = xp.shape
    bs, ss, _ = xs.shape
    past = cache_ckv_l.shape[1]
    ctx_row = lambda i: 0
    lat_row = lambda i: 1 + i // (ss // TM)

    xp2 = xp.reshape(bp * sp, D_MODEL)
    ride = (bp * sp) // TM == N_EXPERTS
    cs, ns = (jnp.asarray(t).astype(BF16) for t in _dft_tables(sp))
    outs = _inproj(xp2, mod3, ctx_row, wts, None, True, experts if ride else (), (sp, cs, ns))
    attn, fm, sga, sgf, ckv, kpe = outs[:6]
    w_gate, w_up, w_down = outs[6:] if ride else (w.astype(BF16) for w in experts)
    ctx_set = (xp2, attn, fm, sga, sgf)

    xs2 = xs.reshape(bs * ss, D_MODEL)
    rope_tabs = tuple(jnp.asarray(t) for t in _rope_tables(ss))
    q, k, v, fcs, sga, sgf = _inproj(xs2, mod3, lat_row, wts, rope_tabs, False)
    kpe_slot = jnp.pad(cache_kpe_l, ((0, 0), (0, 0), (QK_NOPE, LANES - QK_HEAD)))
    kc, vc = _cache_kv(cache_ckv_l.reshape(bs * past, KV_LORA), kpe_slot.reshape(bs * past, LANES), wts)
    attn = _attention(q.reshape(bs, ss, QK_WIDTH),
                      [kc.reshape(bs, past, QK_WIDTH), k.reshape(bs, ss, QK_WIDTH)],
                      [vc, v], TQ_LAT, "attn_lat")
    cs, ns = (jnp.asarray(t).astype(BF16) for t in _dft_tables(ss))
    fm = _fourier(fcs.reshape(bs, ss, 2 * FN_WIDTH), cs, ns, TR_LAT, 1, "fourier_lat")
    lat_set = (xs2, attn.reshape(bs * ss, V_WIDTH), fm.reshape(bs * ss, FN_WIDTH), sga, sgf)

    n_ctx_tiles = (bp * sp) // TM
    all_row = lambda i: jnp.where(i < n_ctx_tiles, 0, lat_row(i - n_ctx_tiles))
    x1, h2tm, grp, rank, cnt = _merge(ctx_set, lat_set, mod3, all_row, wts)
    yp, ys = _sparse_moe(x1, h2tm, grp, rank, cnt, mod3, all_row, wts, w_gate, w_up, w_down, MOE_TILE,
                         bp * sp)
    yp, ys = yp.reshape(bp, sp, D_MODEL), ys.reshape(bs, ss, D_MODEL)

    return yp, ys, ckv.reshape(bp, sp, KV_LORA), jnp.swapaxes(kpe, 1, 2)


def kernel(x_prompt, x_sample, cache_ckv, cache_kpe, c, c_ctx, w_mod, b_mod, norm1_g, w_in, q_a_norm_g, w_q_b, kv_a_norm_g, w_kv_b, q_norm_g, k_norm_g, w_attn_o, w_fnet, w_out, norm2_g, w_router_group, b_router_group, w_router_expert, b_router_expert, w_exp_gate, w_exp_up, w_exp_down):
    depth = w_mod.shape[0]
    n_lat = c.shape[0]
    assert 1 + n_lat <= MOD_ROWS
    cond8 = jnp.concatenate([c_ctx[None, :], c, jnp.zeros((MOD_ROWS - 1 - n_lat, D_MODEL), F32)], axis=0)
    xp, xs = x_prompt, x_sample
    ckv_layers, kpe_layers = [], []
    for l in range(depth):
        mod, w_in_p = _adaln(cond8, w_mod[l], b_mod[l].reshape(1, -1), jnp.swapaxes(w_in[l], 0, 1))
        mod3 = mod.reshape(MOD_ROWS, 6, D_MODEL)
        wts = _pack_weights(l, norm1_g, q_a_norm_g, w_q_b, kv_a_norm_g, w_kv_b, q_norm_g,
                            k_norm_g, w_attn_o, w_fnet, w_out, norm2_g, w_router_group,
                            b_router_group, w_router_expert, b_router_expert)
        wts["w_in"] = w_in_p
        xp, xs, ckv, kpe = _layer(xp, xs, cache_ckv[:, l], cache_kpe[:, l], mod3, wts,
                                  (w_exp_gate[l], w_exp_up[l], w_exp_down[l]))
        ckv_layers.append(ckv)
        kpe_layers.append(kpe)
    return xp, xs, jnp.stack(ckv_layers, axis=1), jnp.stack(kpe_layers, axis=1)
```

```python
import functools
import math

import numpy as np
import jax
import jax.numpy as jnp
from jax import lax
from jax.experimental import pallas as pl
from jax.experimental.pallas import tpu as pltpu

D_MODEL = 1024
GRID_W = 64
N_HEADS = 8
Q_LORA = 512
KV_LORA = 256
QK_NOPE = 64
QK_ROPE = 32
V_HEAD = 64
QK_HEAD = QK_NOPE + QK_ROPE
ATTN_SCALE = QK_HEAD ** -0.5
ROPE_BASE = 10000.0
FN_GROUPS = 4
FN_GROUP_W = 128
FN_WIDTH = FN_GROUPS * FN_GROUP_W
N_GROUPS = 4
EXPERTS_PER_GROUP = 4
N_EXPERTS = N_GROUPS * EXPERTS_PER_GROUP
D_EXPERT = 512
EPS = 1e-6

LANES = 128
HEAD_SLOT = LANES
QK_WIDTH = N_HEADS * HEAD_SLOT
V_WIDTH = N_HEADS * V_HEAD
C_QA = 0
C_KVA = C_QA + Q_LORA
C_KPE = C_KVA + KV_LORA
C_FN = C_KPE + LANES
C_GA = C_FN + FN_WIDTH
C_GF = C_GA + D_MODEL
IN_PACKED = C_GF + D_MODEL
SUBLANES = 8
ROUTER_ROWS = SUBLANES
MOD_ROWS = SUBLANES
VMEM_LIMIT = 56 * 1024 * 1024

TM = 512
TQ_LAT = 256
SEG_ROWS = SUBLANES
TOKEN_ROWS = D_MODEL // LANES
MOE_TILE = 256
ADALN_STEPS = 8

BF16 = jnp.bfloat16
F32 = jnp.float32


def _cparams(sem):
    return pltpu.CompilerParams(dimension_semantics=sem, vmem_limit_bytes=VMEM_LIMIT)


def _dot(a, b):
    return jnp.dot(a, b, preferred_element_type=F32)


def _dot_nt(a, b):
    return lax.dot_general(a, b, (((1,), (1,)), ((), ())), preferred_element_type=F32)


def _sigmoid(x):
    return 1.0 / (1.0 + jnp.exp(-x))


def _split_bf16(x):
    hi = x.astype(BF16)
    return hi, (x - hi.astype(F32)).astype(BF16)


@functools.lru_cache(maxsize=None)
def _rope_tables(n_pos):
    half = QK_ROPE // 2
    quarter = half // 2
    freqs = ROPE_BASE ** (-np.arange(quarter, dtype=np.float64) / quarter)
    pos = np.arange(n_pos)
    row = (pos // GRID_W).astype(np.float64)
    col = (pos % GRID_W).astype(np.float64)
    cos_t = np.ones((n_pos, LANES), np.float64)
    sin_a = np.zeros((n_pos, LANES), np.float64)
    sin_b = np.zeros((n_pos, LANES), np.float64)
    for base, p in ((QK_NOPE, row), (QK_NOPE + half, col)):
        ang = p[:, None] * freqs[None, :]
        cos_t[:, base:base + quarter] = np.cos(ang)
        cos_t[:, base + quarter:base + half] = np.cos(ang)
        sin_a[:, base:base + quarter] = -np.sin(ang)
        sin_b[:, base + quarter:base + half] = np.sin(ang)
    return (cos_t.astype(np.float32), sin_a.astype(np.float32), sin_b.astype(np.float32))


@functools.lru_cache(maxsize=None)
def _dft_tables(n):
    k = np.arange(n)
    ang = 2.0 * np.pi * ((k[:, None] * k[None, :]) % n) / n
    s = 1.0 / math.sqrt(n)
    return (np.cos(ang) * s).astype(np.float32), (-np.sin(ang) * s).astype(np.float32)


def _adaln_kernel(cond_ref, w_ref, b_ref, win_ref, o_ref, winp_ref):
    c = cond_ref[...]
    s_hi, s_lo = _split_bf16(c * _sigmoid(c))
    w_hi, w_lo = _split_bf16(w_ref[...])
    y = _dot(jnp.concatenate([s_hi, s_lo], axis=0), w_hi)
    o_ref[...] = (y[:MOD_ROWS] + y[MOD_ROWS:]) + _dot(s_hi, w_lo) + b_ref[...]

    winp_ref[:C_KPE, :] = win_ref[:C_KPE, :].astype(BF16)
    winp_ref[C_KPE:C_FN, :] = jnp.zeros((LANES, winp_ref.shape[1]), BF16)
    winp_ref[C_KPE + QK_NOPE:C_KPE + QK_HEAD, :] = win_ref[C_KPE:C_KPE + QK_ROPE, :].astype(BF16)
    winp_ref[C_FN:, :] = win_ref[C_KPE + QK_ROPE:, :].astype(BF16)


def _adaln(cond8, w_mod, b_mod, w_in_t):
    n = w_mod.shape[1]
    steps = ADALN_STEPS
    tn, tr = n // steps, D_MODEL // steps
    return pl.pallas_call(
        _adaln_kernel,
        grid=(steps,),
        in_specs=[pl.BlockSpec((MOD_ROWS, D_MODEL), lambda j: (0, 0)),
                  pl.BlockSpec((D_MODEL, tn), lambda j: (0, j)),
                  pl.BlockSpec((1, tn), lambda j: (0, j)),
                  pl.BlockSpec((w_in_t.shape[0], tr), lambda j: (0, j))],
        out_specs=[pl.BlockSpec((MOD_ROWS, tn), lambda j: (0, j)),
                   pl.BlockSpec((IN_PACKED, tr), lambda j: (0, j))],
        out_shape=[jax.ShapeDtypeStruct((MOD_ROWS, n), F32),
                   jax.ShapeDtypeStruct((IN_PACKED, D_MODEL), BF16)],
        compiler_params=_cparams(("arbitrary",)),
        name="adaln",
    )(cond8, w_mod, b_mod, w_in_t)


def _rms(x, width):
    return lax.rsqrt(jnp.sum(x * x, axis=-1, keepdims=True) * (1.0 / width) + EPS)


def _rope(x, cos_t, sin_a, sin_b):
    return x * cos_t + pltpu.roll(x, LANES - 8, 1) * sin_a + pltpu.roll(x, 8, 1) * sin_b


def _inproj_kernel(*refs, rope, emit_cache, n_cast, seq_len):
    it = iter(refs)
    x_ref, mod_ref, g1_ref, win_ref, qag_ref, wqb_ref, kvg_ref, wkb_ref, wvb_ref = (
        next(it) for _ in range(9))
    qg_ref, kg_ref, dft_ref = next(it), next(it), next(it)
    if rope:
        cos_ref, sa_ref, sb_ref = next(it), next(it), next(it)
    if seq_len:
        seq_cs_ref, seq_ns_ref = next(it), next(it)
    cast_in = [next(it) for _ in range(n_cast)]
    if seq_len:
        attn_ref, fm_ref, sga_ref, sgf_ref = (next(it) for _ in range(4))
    else:
        q_ref, k_ref, v_ref, fcs_ref, sga_ref, sgf_ref = (next(it) for _ in range(6))
    if emit_cache:
        ckv_ref, kpe_ref = next(it), next(it)
    cast_out = [next(it) for _ in range(n_cast)]
    if seq_len:
        q_ref, k_ref, v_ref, fcs_ref = (next(it) for _ in range(4))

    for src, dst in zip(cast_in, cast_out):
        dst[...] = src[...].astype(BF16)

    x = x_ref[...]
    shift = mod_ref[0, 0:1, :]
    scale = mod_ref[0, 1:2, :]
    h = (x * _rms(x, D_MODEL) * g1_ref[...]) * (1.0 + scale) + shift
    hb = h.astype(BF16)

    if rope:
        cos_t, sin_a, sin_b = cos_ref[...], sa_ref[...], sb_ref[...]

    qa = _dot_nt(hb, win_ref[C_QA:C_QA + Q_LORA, :])
    qn = (qa * _rms(qa, Q_LORA) * qag_ref[...]).astype(BF16)
    q = _dot(qn, wqb_ref[...])
    qg = qg_ref[...] * ATTN_SCALE
    for hd in range(N_HEADS):
        qh = q[:, hd * HEAD_SLOT:(hd + 1) * HEAD_SLOT]
        qh = qh * _rms(qh, QK_HEAD) * qg
        if rope:
            qh = _rope(qh, cos_t, sin_a, sin_b)
        q_ref[:, hd * HEAD_SLOT:(hd + 1) * HEAD_SLOT] = qh.astype(BF16)

    kva = _dot_nt(hb, win_ref[C_KVA:C_KVA + KV_LORA, :])
    ckv = kva * _rms(kva, KV_LORA) * kvg_ref[...]
    kpe = _dot_nt(hb, win_ref[C_KPE:C_KPE + LANES, :])
    if emit_cache:
        ckv_ref[...] = ckv
        kpe_t = kpe.T[QK_NOPE:QK_NOPE + QK_ROPE, :]
        for j in range(x.shape[0] // seq_len):
            kpe_ref[j] = kpe_t[:, j * seq_len:(j + 1) * seq_len]
    _emit_kv(ckv.astype(BF16), kpe, wkb_ref, wvb_ref, kg_ref,
             (cos_t, sin_a, sin_b) if rope else None, k_ref, v_ref)

    fn = _dot_nt(hb, win_ref[C_FN:C_FN + FN_WIDTH, :]).astype(BF16)
    for g in range(FN_GROUPS):
        cs = _dot(fn[:, g * FN_GROUP_W:(g + 1) * FN_GROUP_W], dft_ref[...])
        fcs_ref[:, g * FN_GROUP_W:(g + 1) * FN_GROUP_W] = cs[:, :FN_GROUP_W].astype(BF16)
        fcs_ref[:, FN_WIDTH + g * FN_GROUP_W:FN_WIDTH + (g + 1) * FN_GROUP_W] = (
            cs[:, FN_GROUP_W:].astype(BF16))

    sga_ref[...] = _sigmoid(_dot_nt(hb, win_ref[C_GA:C_GA + D_MODEL, :])).astype(BF16)
    sgf_ref[...] = _sigmoid(_dot_nt(hb, win_ref[C_GF:C_GF + D_MODEL, :])).astype(BF16)

    if seq_len:
        for s0 in range(0, x.shape[0], seq_len):
            rows = pl.ds(s0, seq_len)
            attn_ref[rows, :] = _attend(q_ref.at[rows], [k_ref.at[rows]], [v_ref.at[:, rows]])
            fm_ref[rows, :] = (_dot(seq_cs_ref[...], fcs_ref[rows, :FN_WIDTH])
                               + _dot(seq_ns_ref[...], fcs_ref[rows, FN_WIDTH:])).astype(BF16)


def _emit_kv(ckvb, kpe, wkb_ref, wvb_ref, kg_ref, rope_tabs, k_ref, v_ref):
    kg = kg_ref[...]
    v_ref[...] = _dot_nt(wvb_ref[...], ckvb).astype(BF16)
    kn = _dot(ckvb, wkb_ref[...])
    pe_ss = jnp.sum(kpe * kpe, axis=-1, keepdims=True)
    pe_g = kpe * kg
    if rope_tabs is not None:
        pe_g = _rope(pe_g, *rope_tabs)
    for hd in range(N_HEADS):
        knh = kn[:, hd * HEAD_SLOT:(hd + 1) * HEAD_SLOT]
        ss = jnp.sum(knh * knh, axis=-1, keepdims=True) + pe_ss
        r = lax.rsqrt(ss * (1.0 / QK_HEAD) + EPS)
        k_ref[:, hd * HEAD_SLOT:(hd + 1) * HEAD_SLOT] = ((knh * kg + pe_g) * r).astype(BF16)


def _const_spec(shape):
    return pl.BlockSpec(shape, lambda i: (0,) * len(shape))


def _inproj(x2d, mod3, mod_row_fn, wts, rope_tabs, emit_cache, cast=(), seq=None):
    n = x2d.shape[0]
    assert all(w.shape[0] == n // TM for w in cast)
    assert seq is None or TM % seq[0] == 0
    rope = rope_tabs is not None
    tiles_per_seq = None if not rope else rope_tabs[0].shape[0] // TM
    in_specs = [pl.BlockSpec((TM, D_MODEL), lambda i: (i, 0)),
                pl.BlockSpec((1, 6, D_MODEL), lambda i: (mod_row_fn(i), 0, 0)),
                _const_spec((1, D_MODEL)),
                _const_spec((IN_PACKED, D_MODEL)),
                _const_spec((1, Q_LORA)),
                _const_spec((Q_LORA, QK_WIDTH)),
                _const_spec((1, KV_LORA)),
                _const_spec((KV_LORA, QK_WIDTH)),
                _const_spec((V_WIDTH, KV_LORA)),
                _const_spec((1, HEAD_SLOT)),
                _const_spec((1, HEAD_SLOT)),
                _const_spec((FN_GROUP_W, 2 * FN_GROUP_W))]
    args = [x2d, mod3, wts["g1"], wts["w_in"], wts["qag"], wts["w_qb"], wts["kvg"],
            wts["w_kb"], wts["w_vb"], wts["qg"], wts["kg"], wts["dft_c"]]
    if rope:
        in_specs += [pl.BlockSpec((TM, LANES), lambda i: (i % tiles_per_seq, 0))] * 3
        args += list(rope_tabs)
    if seq is not None:
        in_specs += [_const_spec((seq[0], seq[0]))] * 2
        args += [seq[1], seq[2]]
    cast_specs = [pl.BlockSpec((1,) + w.shape[1:], lambda i: (i, 0, 0)) for w in cast]
    in_specs += cast_specs
    args += list(cast)
    qkvf = [((TM, QK_WIDTH), (n, QK_WIDTH)), ((TM, QK_WIDTH), (n, QK_WIDTH)),
            ((V_WIDTH, TM), (V_WIDTH, n)), ((TM, 2 * FN_WIDTH), (n, 2 * FN_WIDTH))]
    if seq is None:
        widths = [None] * 4 + [D_MODEL, D_MODEL]
    else:
        widths = [V_WIDTH, FN_WIDTH, D_MODEL, D_MODEL]
    out_shape, out_specs = [], []
    for j, w in enumerate(widths):
        if w is None:
            out_shape.append(jax.ShapeDtypeStruct(qkvf[j][1], BF16))
            out_specs.append(pl.BlockSpec(qkvf[j][0], (lambda i: (0, i)) if j == 2 else (lambda i: (i, 0))))
        else:
            out_shape.append(jax.ShapeDtypeStruct((n, w), BF16))
            out_specs.append(pl.BlockSpec((TM, w), lambda i: (i, 0)))
    if emit_cache:
        assert seq is not None
        out_shape += [jax.ShapeDtypeStruct((n, KV_LORA), F32),
                      jax.ShapeDtypeStruct((n // seq[0], QK_ROPE, seq[0]), F32)]
        out_specs += [pl.BlockSpec((TM, KV_LORA), lambda i: (i, 0)),
                      pl.BlockSpec((TM // seq[0], QK_ROPE, seq[0]), lambda i: (i, 0, 0))]
    out_shape += [jax.ShapeDtypeStruct(w.shape, BF16) for w in cast]
    out_specs += cast_specs
    scratch = [] if seq is None else [pltpu.VMEM(blk, BF16) for blk, _ in qkvf]
    return pl.pallas_call(
        functools.partial(_inproj_kernel, rope=rope, emit_cache=emit_cache, n_cast=len(cast),
                          seq_len=None if seq is None else seq[0]),
        grid=(n // TM,),
        in_specs=in_specs,
        out_specs=out_specs,
        out_shape=out_shape,
        scratch_shapes=scratch,
        compiler_params=_cparams(("parallel",)),
        name="inproj_lat" if rope else "inproj_ctx",
    )(*args)


def _attend(q, ks, vts):
    head = lambda hd: slice(hd * HEAD_SLOT, (hd + 1) * HEAD_SLOT)
    st = [jnp.stack([_dot_nt(k[:, head(hd)], q[:, head(hd)]) for hd in range(N_HEADS)]) for k in ks]
    m = functools.reduce(jnp.maximum, [sj.max(axis=1, keepdims=True) for sj in st])
    p = [jnp.exp(sj - m) for sj in st]
    l = functools.reduce(lambda a, b: a + b, [pj.sum(axis=1, keepdims=True) for pj in p])
    outs = []
    for hd in range(N_HEADS):
        o = functools.reduce(lambda a, b: a + b,
                             [_dot(vt[hd * V_HEAD:(hd + 1) * V_HEAD, :], pj[hd].astype(BF16))
                              for vt, pj in zip(vts, p)])
        outs.append(o / l[hd])
    return jnp.concatenate(outs, axis=0).T.astype(BF16)


def _attn_kernel(q_ref, k_ref, vt_ref, cckv_ref, ckpe_ref, wkb_ref, wvb_ref, kg_ref, cs_ref, ns_ref,
                 f_ref, o_ref, fm_ref, kc_ref, vtc_ref):
    @pl.when(pl.program_id(1) == 0)
    def _():
        _emit_kv(cckv_ref[0].astype(BF16), ckpe_ref[0], wkb_ref, wvb_ref, kg_ref, None, kc_ref, vtc_ref)

    o_ref[0] = _attend(q_ref.at[0], [kc_ref, k_ref.at[0]], [vtc_ref, vt_ref])
    fm_ref[0] = (_dot(cs_ref[...], f_ref[0, :, :FN_WIDTH])
                 + _dot(ns_ref[...], f_ref[0, :, FN_WIDTH:])).astype(BF16)


def _attention(q3, k3, vt, cache_ckv, cache_kpe_slot, wts, fcs3, cs, ns, tq, name):
    b, sq, _ = q3.shape
    past = cache_ckv.shape[1]
    seq = lambda shape: pl.BlockSpec(shape, lambda bi, qi: (bi, 0, 0))
    const = lambda shape: pl.BlockSpec(shape, lambda bi, qi: (0,) * len(shape))
    return pl.pallas_call(
        _attn_kernel,
        grid=(b, sq // tq),
        in_specs=[pl.BlockSpec((1, tq, QK_WIDTH), lambda bi, qi: (bi, qi, 0)),
                  seq((1, sq, QK_WIDTH)),
                  pl.BlockSpec((V_WIDTH, sq), lambda bi, qi: (0, bi)),
                  seq((1, past, KV_LORA)), seq((1, past, LANES)),
                  const((KV_LORA, QK_WIDTH)), const((V_WIDTH, KV_LORA)), const((1, HEAD_SLOT)),
                  pl.BlockSpec((tq, sq), lambda bi, qi: (qi, 0)),
                  pl.BlockSpec((tq, sq), lambda bi, qi: (qi, 0)),
                  seq((1, sq, 2 * FN_WIDTH))],
        out_specs=[pl.BlockSpec((1, tq, V_WIDTH), lambda bi, qi: (bi, qi, 0)),
                   pl.BlockSpec((1, tq, FN_WIDTH), lambda bi, qi: (bi, qi, 0))],
        out_shape=[jax.ShapeDtypeStruct((b, sq, V_WIDTH), BF16),
                   jax.ShapeDtypeStruct((b, sq, FN_WIDTH), BF16)],
        scratch_shapes=[pltpu.VMEM((past, QK_WIDTH), BF16), pltpu.VMEM((V_WIDTH, past), BF16)],
        compiler_params=_cparams(("parallel", "arbitrary")),
        name=name,
    )(q3, k3, vt, cache_ckv, cache_kpe_slot, wts["w_kb"], wts["w_vb"], wts["kg"], cs, ns, fcs3)


def _router_logits(w, b, h_hi, h_lo):
    rows = w.shape[0]
    w_hi, w_lo = _split_bf16(w)
    y = _dot_nt(jnp.concatenate([w_hi, w_lo], axis=0), h_hi)
    return (y[:rows] + y[rows:]) + _dot_nt(w_hi, h_lo) + b


def _rows(x, n):
    return [x[j:j + 1, :] for j in range(n)]


def _first_argmax(rows, top):
    idx = jnp.full(top.shape, len(rows) - 1, jnp.int32)
    for j in range(len(rows) - 2, -1, -1):
        idx = jnp.where(rows[j] == top, j, idx)
    return idx


def _store_token_major(ref, x, tm):
    for s in range(TOKEN_ROWS):
        ref[pl.ds(s, tm, stride=TOKEN_ROWS), :] = x[:, s * LANES:(s + 1) * LANES]


def _load_token_major(ref, tm):
    return jnp.concatenate([ref[pl.ds(s, tm, stride=TOKEN_ROWS), :] for s in range(TOKEN_ROWS)],
                           axis=1)


def _merge_kernel(*refs, n_first):
    first, second = refs[0:5], refs[5:10]
    (mod_ref, wao_ref, wfn_ref, wout_ref, g2_ref, wrg_ref, brg_ref, x1_ref, h2_ref, grp_ref, rank_ref,
     cnt_ref, carry_ref) = refs[10:]
    i = pl.program_id(0)

    @pl.when(i == 0)
    def _():
        carry_ref[...] = jnp.zeros_like(carry_ref)

    def tile(src):
        x, attn, fm, sga, sgf = (r[...] for r in src)
        a = _dot(attn, wao_ref[...])
        f = _dot(fm, wfn_ref[...])
        u = sga.astype(F32) * a + sgf.astype(F32) * f
        y = _dot(u.astype(BF16), wout_ref[...])
        x1 = x + mod_ref[0, 2:3, :] * y
        x1_ref[...] = x1
        tm = x1.shape[0]
        h2 = (x1 * _rms(x1, D_MODEL) * g2_ref[...]) * (1.0 + mod_ref[0, 4:5, :]) + mod_ref[0, 3:4, :]
        _store_token_major(h2_ref, h2, tm)

        h2_hi, h2_lo = _split_bf16(h2)
        g = _rows(_router_logits(wrg_ref[...], brg_ref[...], h2_hi, h2_lo), N_GROUPS)
        gidx = _first_argmax(g, functools.reduce(jnp.maximum, g))

        onehot = jnp.where(lax.broadcasted_iota(jnp.int32, (SEG_ROWS, tm), 0) == gidx, 1.0, 0.0)
        before = (lax.broadcasted_iota(jnp.int32, (tm, tm), 0)
                  < lax.broadcasted_iota(jnp.int32, (tm, tm), 1))
        prefix = _dot(onehot.astype(BF16), jnp.where(before, 1.0, 0.0).astype(BF16))
        carry = carry_ref[...]
        rank = jnp.sum(onehot * (prefix + carry[:, 0:1]), axis=0, keepdims=True)
        grp_ref[...] = gidx
        rank_ref[...] = rank.astype(jnp.int32)
        carry = carry + jnp.sum(onehot, axis=1, keepdims=True)
        carry_ref[...] = carry
        cnt_ref[...] = carry.astype(jnp.int32)

    @pl.when(i < n_first)
    def _():
        tile(first)

    @pl.when(i >= n_first)
    def _():
        tile(second)


def _merge(first, second, mod3, mod_row_fn, wts):
    n_first = first[0].shape[0] // TM
    n = first[0].shape[0] + second[0].shape[0]
    tok = lambda w: pl.BlockSpec((TM, w), lambda i: (i, 0))
    widths = (D_MODEL, V_WIDTH, FN_WIDTH, D_MODEL, D_MODEL)
    in_first = [pl.BlockSpec((TM, w), lambda i: (jnp.minimum(i, n_first - 1), 0)) for w in widths]
    in_second = [pl.BlockSpec((TM, w), lambda i: (jnp.maximum(i - n_first, 0), 0)) for w in widths]
    return pl.pallas_call(
        functools.partial(_merge_kernel, n_first=n_first),
        grid=(n // TM,),
        in_specs=in_first + in_second + [
                  pl.BlockSpec((1, 6, D_MODEL), lambda i: (mod_row_fn(i), 0, 0)),
                  _const_spec((V_WIDTH, D_MODEL)),
                  _const_spec((FN_WIDTH, D_MODEL)),
                  _const_spec((D_MODEL, D_MODEL)),
                  _const_spec((1, D_MODEL)),
                  _const_spec((ROUTER_ROWS, D_MODEL)),
                  _const_spec((ROUTER_ROWS, 1))],
        out_specs=[tok(D_MODEL),
                   pl.BlockSpec((TM * TOKEN_ROWS, LANES), lambda i: (i, 0)),
                   pl.BlockSpec((1, TM), lambda i: (0, i)),
                   pl.BlockSpec((1, TM), lambda i: (0, i)), _const_spec((SEG_ROWS, LANES))],
        out_shape=[jax.ShapeDtypeStruct((n, D_MODEL), F32),
                   jax.ShapeDtypeStruct((n * TOKEN_ROWS, LANES), F32),
                   jax.ShapeDtypeStruct((1, n), jnp.int32),
                   jax.ShapeDtypeStruct((1, n), jnp.int32),
                   jax.ShapeDtypeStruct((SEG_ROWS, LANES), jnp.int32)],
        scratch_shapes=[pltpu.VMEM((SEG_ROWS, LANES), F32)],
        compiler_params=_cparams(("arbitrary",)),
        name="merge",
    )(*first, *second, mod3, wts["w_ao"], wts["w_fn"], wts["w_out"], wts["g2"],
      wts["w_rg"], wts["b_rg"])


def _token_rows(ref, t):
    start = t * TOKEN_ROWS
    if not isinstance(t, int):
        start = pl.multiple_of(start, TOKEN_ROWS)
    return ref.at[pl.ds(start, TOKEN_ROWS)]


def _row_copy(src, dst, src_tok, dst_tok, sem):
    return pltpu.make_async_copy(_token_rows(src, src_tok), _token_rows(dst, dst_tok), sem)


def _wait_tile(src_hbm, buf, sem):
    pltpu.make_async_copy(src_hbm.at[pl.ds(0, buf.shape[0])], buf, sem).wait()


def _combine_kernel(pos_ref, y_hbm, x1_ref, mod_ref, o1_ref, o2_ref, buf, sem, *, n_first):
    i = pl.program_id(0)
    tm = o1_ref.shape[0]

    def gather(tile_idx, slot):
        def start(r, c):
            _row_copy(y_hbm, buf.at[slot], pos_ref[tile_idx * tm + r], r, sem.at[slot]).start()
            return c

        lax.fori_loop(0, tm, start, 0, unroll=32)

    @pl.when(i == 0)
    def _():
        gather(0, 0)

    @pl.when(i + 1 < pl.num_programs(0))
    def _():
        gather(i + 1, (i + 1) % 2)

    slot = i % 2

    _wait_tile(y_hbm, buf.at[slot], sem.at[slot])
    out = x1_ref[...] + mod_ref[0, 5:6, :] * _load_token_major(buf.at[slot], tm)

    @pl.when(i < n_first)
    def _():
        o1_ref[...] = out

    @pl.when(i >= n_first)
    def _():
        o2_ref[...] = out


def _combine(y_sorted, pos, x1, mod3, mod_row_fn, n_first_tokens):
    n = x1.shape[0]
    n_first = n_first_tokens // TM
    return pl.pallas_call(
        functools.partial(_combine_kernel, n_first=n_first),
        grid_spec=pltpu.PrefetchScalarGridSpec(
            num_scalar_prefetch=1,
            grid=(n // TM,),
            in_specs=[pl.BlockSpec(memory_space=pl.ANY),
                      pl.BlockSpec((TM, D_MODEL), lambda i, pos: (i, 0)),
                      pl.BlockSpec((1, 6, D_MODEL), lambda i, pos: (mod_row_fn(i), 0, 0))],
            out_specs=[pl.BlockSpec((TM, D_MODEL), lambda i, pos: (jnp.minimum(i, n_first - 1), 0)),
                       pl.BlockSpec((TM, D_MODEL), lambda i, pos: (jnp.maximum(i - n_first, 0), 0))],
            scratch_shapes=[pltpu.VMEM((2, TM * TOKEN_ROWS, LANES), F32),
                            pltpu.SemaphoreType.DMA((2,))]),
        out_shape=[jax.ShapeDtypeStruct((n_first_tokens, D_MODEL), F32),
                   jax.ShapeDtypeStruct((n - n_first_tokens, D_MODEL), F32)],
        compiler_params=_cparams(("arbitrary",)),
        name="moe_combine",
    )(pos, y_sorted, x1, mod3)


def _moe_kernel(grp_ref, on_ref, src_ref, h_hbm, wrg_ref, brg_ref, wre_ref, bre_ref, wg_ref, wu_ref,
                wd_ref, o_ref, hbuf, sem, *, tile):
    i = pl.program_id(0)

    def gather(tile_idx, slot):
        def start(r, c):
            _row_copy(h_hbm, hbuf.at[slot], src_ref[tile_idx * tile + r], r, sem.at[slot]).start()
            return c

        lax.fori_loop(0, tile, start, 0, unroll=32)

    @pl.when(jnp.logical_and(i == 0, on_ref[0] == 1))
    def _():
        gather(0, 0)

    nxt = jnp.minimum(i + 1, pl.num_programs(0) - 1)

    @pl.when(jnp.logical_and(i + 1 < pl.num_programs(0), on_ref[nxt] == 1))
    def _():
        gather(i + 1, (i + 1) % 2)

    @pl.when(on_ref[i] == 0)
    def _():
        o_ref[...] = jnp.zeros_like(o_ref)

    @pl.when(on_ref[i] == 1)
    def _():
        slot = i % 2

        _wait_tile(h_hbm, hbuf.at[slot], sem.at[slot])
        h2_hi, h2_lo = _split_bf16(_load_token_major(hbuf.at[slot], tile))

        logits = _router_logits(jnp.concatenate([wrg_ref[...], wre_ref[0]], axis=0),
                                jnp.concatenate([brg_ref[...], bre_ref[0]], axis=0), h2_hi, h2_lo)
        g = _rows(logits, N_GROUPS)
        gmax = functools.reduce(jnp.maximum, g)
        p_top = 1.0 / functools.reduce(lambda p, q: p + q, [jnp.exp(gj - gmax) for gj in g])
        e = _rows(logits[ROUTER_ROWS:], EXPERTS_PER_GROUP)
        m1 = functools.reduce(jnp.maximum, e)
        i1 = _first_argmax(e, m1)
        rest = [jnp.where(i1 == j, -jnp.inf, e[j]) for j in range(EXPERTS_PER_GROUP)]
        m2 = functools.reduce(jnp.maximum, rest)
        i2 = _first_argmax(rest, m2)
        t = jnp.exp(m2 - m1)
        w1 = p_top / (1.0 + t)
        w2 = p_top * t / (1.0 + t)
        row = lax.broadcasted_iota(jnp.int32, (LANES, tile), 0)
        comb = (jnp.where(row == i1, w1, 0.0) + jnp.where(row == i2, w2, 0.0)).T

        gates = [_dot(h2_hi, wg_ref[j]) for j in range(EXPERTS_PER_GROUP)]
        ups = [_dot(h2_hi, wu_ref[j]) for j in range(EXPERTS_PER_GROUP)]
        acts = [((a * _sigmoid(a)) * u * comb[:, j:j + 1]).astype(BF16)
                for j, (a, u) in enumerate(zip(gates, ups))]
        acc = functools.reduce(lambda p, q: p + q,
                               [_dot(acts[j], wd_ref[j]) for j in range(EXPERTS_PER_GROUP)])
        _store_token_major(o_ref, acc, tile)


def _moe(h2tm, src, maps, wts, w_gate, w_up, w_down, tile):
    n_tiles = src.shape[0] // tile
    const = lambda shape: pl.BlockSpec(shape, lambda i, grp, on, src: (0,) * len(shape))
    by_group = lambda shape: pl.BlockSpec(
        shape, lambda i, grp, on, src: (grp[i],) + (0,) * (len(shape) - 1))
    return pl.pallas_call(
        functools.partial(_moe_kernel, tile=tile),
        grid_spec=pltpu.PrefetchScalarGridSpec(
            num_scalar_prefetch=3,
            grid=(n_tiles,),
            in_specs=[pl.BlockSpec(memory_space=pl.ANY),
                      const((ROUTER_ROWS, D_MODEL)), const((ROUTER_ROWS, 1)),
                      by_group((1, ROUTER_ROWS, D_MODEL)), by_group((1, ROUTER_ROWS, 1)),
                      by_group((EXPERTS_PER_GROUP, D_MODEL, D_EXPERT)),
                      by_group((EXPERTS_PER_GROUP, D_MODEL, D_EXPERT)),
                      by_group((EXPERTS_PER_GROUP, D_EXPERT, D_MODEL))],
            out_specs=pl.BlockSpec((tile * TOKEN_ROWS, LANES), lambda i, grp, on, src: (i, 0)),
            scratch_shapes=[pltpu.VMEM((2, tile * TOKEN_ROWS, LANES), F32),
                            pltpu.SemaphoreType.DMA((2,))]),
        out_shape=jax.ShapeDtypeStruct((n_tiles * tile * TOKEN_ROWS, LANES), F32),
        compiler_params=_cparams(("arbitrary",)),
        name="moe",
    )(*maps, src, h2tm, wts["w_rg"], wts["b_rg"], wts["w_re"], wts["b_re"], w_gate, w_up, w_down)


def _invert_kernel(pos_ref, lo_ref, hi_ref, src_ref, *, n, n_fill):
    def zero(p, c):
        src_ref[p] = 0
        return c

    for s in range(n_fill):
        lax.fori_loop(lo_ref[s], hi_ref[s], zero, 0)

    def put(t, c):
        src_ref[pos_ref[t]] = t
        return c

    lax.fori_loop(0, n, put, 0, unroll=8)


def _invert(pos, fill_lo, fill_hi, n_sorted):
    return pl.pallas_call(
        functools.partial(_invert_kernel, n=pos.shape[0], n_fill=fill_lo.shape[0]),
        grid_spec=pltpu.PrefetchScalarGridSpec(
            num_scalar_prefetch=3, grid=(1,), in_specs=[],
            out_specs=pl.BlockSpec(memory_space=pltpu.SMEM)),
        out_shape=jax.ShapeDtypeStruct((n_sorted,), jnp.int32),
        name="moe_invert",
    )(pos, fill_lo, fill_hi)


def _sort_plan(grp, rank, cnt, tile, n_tiles):
    n_tile_grp = (cnt + tile - 1) // tile
    tile_end = jnp.cumsum(n_tile_grp)
    tile_start = tile_end - n_tile_grp
    total = tile_end[-1]
    pos = ((tile_start * tile)[grp] + rank).astype(jnp.int32)
    fill_lo = jnp.concatenate([tile_start * tile + cnt, total[None] * tile]).astype(jnp.int32)
    fill_hi = jnp.concatenate([tile_end * tile, jnp.full((1,), n_tiles * tile)]).astype(jnp.int32)
    src = _invert(pos, fill_lo, fill_hi, n_tiles * tile)
    i = jnp.arange(n_tiles, dtype=jnp.int32)
    tile_grp = jnp.sum((jnp.minimum(i, total - 1)[:, None] >= tile_end[None, :]).astype(jnp.int32),
                       axis=1)
    return pos, src, (tile_grp.astype(jnp.int32), (i < total).astype(jnp.int32))


def _sparse_moe(x1, h2tm, grp, rank, cnt8, mod3, mod_row_fn, wts, w_gate, w_up, w_down, tile,
                n_first_tokens):
    n = x1.shape[0]
    n_tiles = n // tile + N_GROUPS
    pos, src, maps = _sort_plan(grp.reshape(n), rank.reshape(n), cnt8[:N_GROUPS, 0], tile, n_tiles)
    y_sorted = _moe(h2tm, src, maps, wts, w_gate, w_up, w_down, tile)
    return _combine(y_sorted, pos, x1, mod3, mod_row_fn, n_first_tokens)


def _pack_weights(l, norm1_g, q_a_norm_g, w_q_b, kv_a_norm_g, w_kv_b, q_norm_g, k_norm_g,
                  w_attn_o, w_fnet, w_out, norm2_g, w_router_group, b_router_group,
                  w_router_expert, b_router_expert):
    w_qb = jnp.pad(w_q_b[l].reshape(Q_LORA, N_HEADS, QK_HEAD),
                   ((0, 0), (0, 0), (0, HEAD_SLOT - QK_HEAD))).reshape(Q_LORA, QK_WIDTH)
    wkv = w_kv_b[l].reshape(KV_LORA, N_HEADS, QK_NOPE + V_HEAD)
    w_kb = jnp.pad(wkv[:, :, :QK_NOPE],
                   ((0, 0), (0, 0), (0, HEAD_SLOT - QK_NOPE))).reshape(KV_LORA, QK_WIDTH)
    w_vb = wkv[:, :, QK_NOPE:].reshape(KV_LORA, V_WIDTH)
    pad_g = lambda g: jnp.pad(g, (0, HEAD_SLOT - QK_HEAD)).reshape(1, HEAD_SLOT)
    w_rg = jnp.pad(w_router_group[l].T, ((0, ROUTER_ROWS - N_GROUPS), (0, 0)))
    b_rg = jnp.pad(b_router_group[l], (0, ROUTER_ROWS - N_GROUPS)).reshape(ROUTER_ROWS, 1)
    w_re = jnp.pad(w_router_expert[l].T.reshape(N_GROUPS, EXPERTS_PER_GROUP, D_MODEL),
                   ((0, 0), (0, ROUTER_ROWS - EXPERTS_PER_GROUP), (0, 0)))
    b_re = jnp.pad(b_router_expert[l].reshape(N_GROUPS, EXPERTS_PER_GROUP),
                   ((0, 0), (0, ROUTER_ROWS - EXPERTS_PER_GROUP))).reshape(N_GROUPS, ROUTER_ROWS, 1)
    dft_c, dft_ns = _dft_tables(FN_GROUP_W)
    return {
        "g1": norm1_g[l].reshape(1, D_MODEL),
        "qag": q_a_norm_g[l].reshape(1, Q_LORA),
        "w_qb": w_qb.astype(BF16),
        "kvg": kv_a_norm_g[l].reshape(1, KV_LORA),
        "w_kb": w_kb.astype(BF16),
        "w_vb": w_vb.T.astype(BF16),
        "qg": pad_g(q_norm_g[l]),
        "kg": pad_g(k_norm_g[l]),
        "dft_c": jnp.concatenate([jnp.asarray(dft_c), -jnp.asarray(dft_ns)], axis=1).astype(BF16),
        "w_ao": w_attn_o[l].astype(BF16),
        "w_fn": w_fnet[l].astype(BF16),
        "w_out": w_out[l].astype(BF16),
        "g2": norm2_g[l].reshape(1, D_MODEL),
        "w_rg": w_rg,
        "b_rg": b_rg,
        "w_re": w_re,
        "b_re": b_re,
    }


def _layer(xp, xs, cache_ckv_l, cache_kpe_l, mod3, wts, experts):
    bp, sp, _ = xp.shape
    bs, ss, _ = xs.shape
    ctx_row = lambda i: 0
    lat_row = lambda i: 1 + i // (ss // TM)

    xp2 = xp.reshape(bp * sp, D_MODEL)
    ride = (bp * sp) // TM == N_EXPERTS
    cs, ns = (jnp.asarray(t).astype(BF16) for t in _dft_tables(sp))
    outs = _inproj(xp2, mod3, ctx_row, wts, None, True, experts if ride else (), (sp, cs, ns))
    attn, fm, sga, sgf, ckv, kpe = outs[:6]
    w_gate, w_up, w_down = outs[6:] if ride else (w.astype(BF16) for w in experts)
    ctx_set = (xp2, attn, fm, sga, sgf)

    xs2 = xs.reshape(bs * ss, D_MODEL)
    rope_tabs = tuple(jnp.asarray(t) for t in _rope_tables(ss))
    q, k, v, fcs, sga, sgf = _inproj(xs2, mod3, lat_row, wts, rope_tabs, False)
    kpe_slot = jnp.pad(cache_kpe_l, ((0, 0), (0, 0), (QK_NOPE, LANES - QK_HEAD)))
    cs, ns = (jnp.asarray(t).astype(BF16) for t in _dft_tables(ss))
    attn, fm = _attention(q.reshape(bs, ss, QK_WIDTH), k.reshape(bs, ss, QK_WIDTH), v, cache_ckv_l,
                          kpe_slot, wts, fcs.reshape(bs, ss, 2 * FN_WIDTH), cs, ns, TQ_LAT, "attn_lat")
    lat_set = (xs2, attn.reshape(bs * ss, V_WIDTH), fm.reshape(bs * ss, FN_WIDTH), sga, sgf)

    n_ctx_tiles = (bp * sp) // TM
    all_row = lambda i: jnp.where(i < n_ctx_tiles, 0, lat_row(i - n_ctx_tiles))
    x1, h2tm, grp, rank, cnt = _merge(ctx_set, lat_set, mod3, all_row, wts)
    yp, ys = _sparse_moe(x1, h2tm, grp, rank, cnt, mod3, all_row, wts, w_gate, w_up, w_down, MOE_TILE,
                         bp * sp)
    yp, ys = yp.reshape(bp, sp, D_MODEL), ys.reshape(bs, ss, D_MODEL)

    return yp, ys, ckv.reshape(bp, sp, KV_LORA), jnp.swapaxes(kpe, 1, 2)


def kernel(x_prompt, x_sample, cache_ckv, cache_kpe, c, c_ctx, w_mod, b_mod, norm1_g, w_in, q_a_norm_g, w_q_b, kv_a_norm_g, w_kv_b, q_norm_g, k_norm_g, w_attn_o, w_fnet, w_out, norm2_g, w_router_group, b_router_group, w_router_expert, b_router_expert, w_exp_gate, w_exp_up, w_exp_down):
    depth = w_mod.shape[0]
    n_lat = c.shape[0]
    assert 1 + n_lat <= MOD_ROWS
    cond8 = jnp.concatenate([c_ctx[None, :], c, jnp.zeros((MOD_ROWS - 1 - n_lat, D_MODEL), F32)], axis=0)
    xp, xs = x_prompt, x_sample
    ckv_layers, kpe_layers = [], []
    for l in range(depth):
        mod, w_in_p = _adaln(cond8, w_mod[l], b_mod[l].reshape(1, -1), jnp.swapaxes(w_in[l], 0, 1))
        mod3 = mod.reshape(MOD_ROWS, 6, D_MODEL)
        wts = _pack_weights(l, norm1_g, q_a_norm_g, w_q_b, kv_a_norm_g, w_kv_b, q_norm_g,
                            k_norm_g, w_attn_o, w_fnet, w_out, norm2_g, w_router_group,
                            b_router_group, w_router_expert, b_router_expert)
        wts["w_in"] = w_in_p
        xp, xs, ckv, kpe = _layer(xp, xs, cache_ckv[:, l], cache_kpe[:, l], mod3, wts,
                                  (w_exp_gate[l], w_exp_up[l], w_exp_down[l]))
        ckv_layers.append(ckv)
        kpe_layers.append(kpe)
    return xp, xs, jnp.stack(ckv_layers, axis=1), jnp.stack(kpe_layers, axis=1)
```

```python
import functools
import math

import numpy as np
import jax
import jax.numpy as jnp
from jax import lax
from jax.experimental import pallas as pl
from jax.experimental.pallas import tpu as pltpu

D_MODEL = 1024
GRID_W = 64
N_HEADS = 8
Q_LORA = 512
KV_LORA = 256
QK_NOPE = 64
QK_ROPE = 32
V_HEAD = 64
QK_HEAD = QK_NOPE + QK_ROPE
ATTN_SCALE = QK_HEAD ** -0.5
ROPE_BASE = 10000.0
FN_GROUPS = 4
FN_GROUP_W = 128
FN_WIDTH = FN_GROUPS * FN_GROUP_W
N_GROUPS = 4
EXPERTS_PER_GROUP = 4
N_EXPERTS = N_GROUPS * EXPERTS_PER_GROUP
D_EXPERT = 512
EPS = 1e-6

LANES = 128
HEAD_SLOT = LANES
QK_WIDTH = N_HEADS * HEAD_SLOT
V_WIDTH = N_HEADS * V_HEAD
C_QA = 0
C_KVA = C_QA + Q_LORA
C_KPE = C_KVA + KV_LORA
C_FN = C_KPE + LANES
C_GA = C_FN + FN_WIDTH
C_GF = C_GA + D_MODEL
IN_PACKED = C_GF + D_MODEL
SUBLANES = 8
ROUTER_ROWS = SUBLANES
MOD_ROWS = SUBLANES
VMEM_LIMIT = 56 * 1024 * 1024

TM = 512
TQ_LAT = 256
SEG_ROWS = SUBLANES
TOKEN_ROWS = D_MODEL // LANES
MOE_TILE = 256
ADALN_STEPS = 4

BF16 = jnp.bfloat16
F32 = jnp.float32


def _cparams(sem):
    return pltpu.CompilerParams(dimension_semantics=sem, vmem_limit_bytes=VMEM_LIMIT)


def _dot(a, b):
    return jnp.dot(a, b, preferred_element_type=F32)


def _dot_nt(a, b):
    return lax.dot_general(a, b, (((1,), (1,)), ((), ())), preferred_element_type=F32)


def _sigmoid(x):
    return 1.0 / (1.0 + jnp.exp(-x))


def _split_bf16(x):
    hi = x.astype(BF16)
    return hi, (x - hi.astype(F32)).astype(BF16)


@functools.lru_cache(maxsize=None)
def _rope_tables(n_pos):
    half = QK_ROPE // 2
    quarter = half // 2
    freqs = ROPE_BASE ** (-np.arange(quarter, dtype=np.float64) / quarter)
    pos = np.arange(n_pos)
    row = (pos // GRID_W).astype(np.float64)
    col = (pos % GRID_W).astype(np.float64)
    cos_t = np.ones((n_pos, LANES), np.float64)
    sin_a = np.zeros((n_pos, LANES), np.float64)
    sin_b = np.zeros((n_pos, LANES), np.float64)
    for base, p in ((QK_NOPE, row), (QK_NOPE + half, col)):
        ang = p[:, None] * freqs[None, :]
        cos_t[:, base:base + quarter] = np.cos(ang)
        cos_t[:, base + quarter:base + half] = np.cos(ang)
        sin_a[:, base:base + quarter] = -np.sin(ang)
        sin_b[:, base + quarter:base + half] = np.sin(ang)
    return (cos_t.astype(np.float32), sin_a.astype(np.float32), sin_b.astype(np.float32))


@functools.lru_cache(maxsize=None)
def _dft_tables(n):
    k = np.arange(n)
    ang = 2.0 * np.pi * ((k[:, None] * k[None, :]) % n) / n
    s = 1.0 / math.sqrt(n)
    return (np.cos(ang) * s).astype(np.float32), (-np.sin(ang) * s).astype(np.float32)


def _adaln_kernel(cond_ref, w_ref, b_ref, win_ref, o_ref, winp_ref):
    c = cond_ref[...]
    s_hi, s_lo = _split_bf16(c * _sigmoid(c))
    w_hi, w_lo = _split_bf16(w_ref[...])
    y = _dot(jnp.concatenate([s_hi, s_lo], axis=0), w_hi)
    o_ref[...] = (y[:MOD_ROWS] + y[MOD_ROWS:]) + _dot(s_hi, w_lo) + b_ref[...]

    winp_ref[:C_KPE, :] = win_ref[:C_KPE, :].astype(BF16)
    winp_ref[C_KPE:C_FN, :] = jnp.zeros((LANES, winp_ref.shape[1]), BF16)
    winp_ref[C_KPE + QK_NOPE:C_KPE + QK_HEAD, :] = win_ref[C_KPE:C_KPE + QK_ROPE, :].astype(BF16)
    winp_ref[C_FN:, :] = win_ref[C_KPE + QK_ROPE:, :].astype(BF16)


def _adaln(cond8, w_mod, b_mod, w_in_t):
    n = w_mod.shape[1]
    steps = ADALN_STEPS
    tn, tr = n // steps, D_MODEL // steps
    return pl.pallas_call(
        _adaln_kernel,
        grid=(steps,),
        in_specs=[pl.BlockSpec((MOD_ROWS, D_MODEL), lambda j: (0, 0)),
                  pl.BlockSpec((D_MODEL, tn), lambda j: (0, j)),
                  pl.BlockSpec((1, tn), lambda j: (0, j)),
                  pl.BlockSpec((w_in_t.shape[0], tr), lambda j: (0, j))],
        out_specs=[pl.BlockSpec((MOD_ROWS, tn), lambda j: (0, j)),
                   pl.BlockSpec((IN_PACKED, tr), lambda j: (0, j))],
        out_shape=[jax.ShapeDtypeStruct((MOD_ROWS, n), F32),
                   jax.ShapeDtypeStruct((IN_PACKED, D_MODEL), BF16)],
        compiler_params=_cparams(("arbitrary",)),
        name="adaln",
    )(cond8, w_mod, b_mod, w_in_t)


def _rms(x, width):
    return lax.rsqrt(jnp.sum(x * x, axis=-1, keepdims=True) * (1.0 / width) + EPS)


def _rope(x, cos_t, sin_a, sin_b):
    return x * cos_t + pltpu.roll(x, LANES - 8, 1) * sin_a + pltpu.roll(x, 8, 1) * sin_b


def _inproj_kernel(*refs, rope, emit_cache, n_cast, seq_len):
    it = iter(refs)
    x_ref, mod_ref, g1_ref, win_ref, qag_ref, wqb_ref, kvg_ref, wkb_ref, wvb_ref = (
        next(it) for _ in range(9))
    qg_ref, kg_ref, dft_ref = next(it), next(it), next(it)
    if rope:
        cos_ref, sa_ref, sb_ref = next(it), next(it), next(it)
    if seq_len:
        seq_cs_ref, seq_ns_ref = next(it), next(it)
    cast_in = [next(it) for _ in range(n_cast)]
    if seq_len:
        attn_ref, fm_ref, sga_ref, sgf_ref = (next(it) for _ in range(4))
    else:
        q_ref, k_ref, v_ref, fcs_ref, sga_ref, sgf_ref = (next(it) for _ in range(6))
    if emit_cache:
        ckv_ref, kpe_ref = next(it), next(it)
    cast_out = [next(it) for _ in range(n_cast)]
    if seq_len:
        q_ref, k_ref, v_ref, fcs_ref = (next(it) for _ in range(4))

    for src, dst in zip(cast_in, cast_out):
        dst[...] = src[...].astype(BF16)

    x = x_ref[...]
    shift = mod_ref[0, 0:1, :]
    scale = mod_ref[0, 1:2, :]
    h = (x * _rms(x, D_MODEL) * g1_ref[...]) * (1.0 + scale) + shift
    hb = h.astype(BF16)

    if rope:
        cos_t, sin_a, sin_b = cos_ref[...], sa_ref[...], sb_ref[...]

    qa = _dot_nt(hb, win_ref[C_QA:C_QA + Q_LORA, :])
    qn = (qa * _rms(qa, Q_LORA) * qag_ref[...]).astype(BF16)
    q = _dot(qn, wqb_ref[...])
    qg = qg_ref[...] * ATTN_SCALE
    for hd in range(N_HEADS):
        qh = q[:, hd * HEAD_SLOT:(hd + 1) * HEAD_SLOT]
        qh = qh * _rms(qh, QK_HEAD) * qg
        if rope:
            qh = _rope(qh, cos_t, sin_a, sin_b)
        q_ref[:, hd * HEAD_SLOT:(hd + 1) * HEAD_SLOT] = qh.astype(BF16)

    kva = _dot_nt(hb, win_ref[C_KVA:C_KVA + KV_LORA, :])
    ckv = kva * _rms(kva, KV_LORA) * kvg_ref[...]
    kpe = _dot_nt(hb, win_ref[C_KPE:C_KPE + LANES, :])
    if emit_cache:
        ckv_ref[...] = ckv
        kpe_t = kpe.T[QK_NOPE:QK_NOPE + QK_ROPE, :]
        for j in range(x.shape[0] // seq_len):
            kpe_ref[j] = kpe_t[:, j * seq_len:(j + 1) * seq_len]
    _emit_kv(ckv.astype(BF16), kpe, wkb_ref, wvb_ref, kg_ref,
             (cos_t, sin_a, sin_b) if rope else None, k_ref, v_ref)

    fn = _dot_nt(hb, win_ref[C_FN:C_FN + FN_WIDTH, :]).astype(BF16)
    for g in range(FN_GROUPS):
        cs = _dot(fn[:, g * FN_GROUP_W:(g + 1) * FN_GROUP_W], dft_ref[...])
        fcs_ref[:, g * FN_GROUP_W:(g + 1) * FN_GROUP_W] = cs[:, :FN_GROUP_W].astype(BF16)
        fcs_ref[:, FN_WIDTH + g * FN_GROUP_W:FN_WIDTH + (g + 1) * FN_GROUP_W] = (
            cs[:, FN_GROUP_W:].astype(BF16))

    sga_ref[...] = _sigmoid(_dot_nt(hb, win_ref[C_GA:C_GA + D_MODEL, :])).astype(BF16)
    sgf_ref[...] = _sigmoid(_dot_nt(hb, win_ref[C_GF:C_GF + D_MODEL, :])).astype(BF16)

    if seq_len:
        seqs = [pl.ds(s0, seq_len) for s0 in range(0, x.shape[0], seq_len)]
        attn = _attend([(q_ref.at[rows], [k_ref.at[rows]], [v_ref.at[:, rows]]) for rows in seqs])
        for rows, attn_seq in zip(seqs, attn):
            attn_ref[rows, :] = attn_seq
            fm_ref[rows, :] = (_dot(seq_cs_ref[...], fcs_ref[rows, :FN_WIDTH])
                               + _dot(seq_ns_ref[...], fcs_ref[rows, FN_WIDTH:])).astype(BF16)


def _emit_kv(ckvb, kpe, wkb_ref, wvb_ref, kg_ref, rope_tabs, k_ref, v_ref):
    kg = kg_ref[...]
    v_ref[...] = _dot_nt(wvb_ref[...], ckvb).astype(BF16)
    kn = _dot(ckvb, wkb_ref[...])
    pe_ss = jnp.sum(kpe * kpe, axis=-1, keepdims=True)
    pe_g = kpe * kg
    if rope_tabs is not None:
        pe_g = _rope(pe_g, *rope_tabs)
    for hd in range(N_HEADS):
        knh = kn[:, hd * HEAD_SLOT:(hd + 1) * HEAD_SLOT]
        ss = jnp.sum(knh * knh, axis=-1, keepdims=True) + pe_ss
        r = lax.rsqrt(ss * (1.0 / QK_HEAD) + EPS)
        k_ref[:, hd * HEAD_SLOT:(hd + 1) * HEAD_SLOT] = ((knh * kg + pe_g) * r).astype(BF16)


def _const_spec(shape):
    return pl.BlockSpec(shape, lambda i: (0,) * len(shape))


def _inproj(x2d, mod3, mod_row_fn, wts, rope_tabs, emit_cache, cast=(), seq=None):
    n = x2d.shape[0]
    assert all(w.shape[0] == n // TM for w in cast)
    assert seq is None or TM % seq[0] == 0
    rope = rope_tabs is not None
    tiles_per_seq = None if not rope else rope_tabs[0].shape[0] // TM
    in_specs = [pl.BlockSpec((TM, D_MODEL), lambda i: (i, 0)),
                pl.BlockSpec((1, 6, D_MODEL), lambda i: (mod_row_fn(i), 0, 0)),
                _const_spec((1, D_MODEL)),
                _const_spec((IN_PACKED, D_MODEL)),
                _const_spec((1, Q_LORA)),
                _const_spec((Q_LORA, QK_WIDTH)),
                _const_spec((1, KV_LORA)),
                _const_spec((KV_LORA, QK_WIDTH)),
                _const_spec((V_WIDTH, KV_LORA)),
                _const_spec((1, HEAD_SLOT)),
                _const_spec((1, HEAD_SLOT)),
                _const_spec((FN_GROUP_W, 2 * FN_GROUP_W))]
    args = [x2d, mod3, wts["g1"], wts["w_in"], wts["qag"], wts["w_qb"], wts["kvg"],
            wts["w_kb"], wts["w_vb"], wts["qg"], wts["kg"], wts["dft_c"]]
    if rope:
        in_specs += [pl.BlockSpec((TM, LANES), lambda i: (i % tiles_per_seq, 0))] * 3
        args += list(rope_tabs)
    if seq is not None:
        in_specs += [_const_spec((seq[0], seq[0]))] * 2
        args += [seq[1], seq[2]]
    cast_specs = [pl.BlockSpec((1,) + w.shape[1:], lambda i: (i, 0, 0)) for w in cast]
    in_specs += cast_specs
    args += list(cast)
    qkvf = [((TM, QK_WIDTH), (n, QK_WIDTH)), ((TM, QK_WIDTH), (n, QK_WIDTH)),
            ((V_WIDTH, TM), (V_WIDTH, n)), ((TM, 2 * FN_WIDTH), (n, 2 * FN_WIDTH))]
    if seq is None:
        widths = [None] * 4 + [D_MODEL, D_MODEL]
    else:
        widths = [V_WIDTH, FN_WIDTH, D_MODEL, D_MODEL]
    out_shape, out_specs = [], []
    for j, w in enumerate(widths):
        if w is None:
            out_shape.append(jax.ShapeDtypeStruct(qkvf[j][1], BF16))
            out_specs.append(pl.BlockSpec(qkvf[j][0], (lambda i: (0, i)) if j == 2 else (lambda i: (i, 0))))
        else:
            out_shape.append(jax.ShapeDtypeStruct((n, w), BF16))
            out_specs.append(pl.BlockSpec((TM, w), lambda i: (i, 0)))
    if emit_cache:
        assert seq is not None
        out_shape += [jax.ShapeDtypeStruct((n, KV_LORA), F32),
                      jax.ShapeDtypeStruct((n // seq[0], QK_ROPE, seq[0]), F32)]
        out_specs += [pl.BlockSpec((TM, KV_LORA), lambda i: (i, 0)),
                      pl.BlockSpec((TM // seq[0], QK_ROPE, seq[0]), lambda i: (i, 0, 0))]
    out_shape += [jax.ShapeDtypeStruct(w.shape, BF16) for w in cast]
    out_specs += cast_specs
    scratch = [] if seq is None else [pltpu.VMEM(blk, BF16) for blk, _ in qkvf]
    return pl.pallas_call(
        functools.partial(_inproj_kernel, rope=rope, emit_cache=emit_cache, n_cast=len(cast),
                          seq_len=None if seq is None else seq[0]),
        grid=(n // TM,),
        in_specs=in_specs,
        out_specs=out_specs,
        out_shape=out_shape,
        scratch_shapes=scratch,
        compiler_params=_cparams(("parallel",)),
        name="inproj_lat" if rope else "inproj_ctx",
    )(*args)


def _attend(problems):
    head = lambda hd: slice(hd * HEAD_SLOT, (hd + 1) * HEAD_SLOT)
    n_sets = len(problems[0][1])
    st = [jnp.stack([_dot_nt(ks[j][:, head(hd)], q[:, head(hd)])
                     for q, ks, _ in problems for hd in range(N_HEADS)]) for j in range(n_sets)]
    m = functools.reduce(jnp.maximum, [sj.max(axis=1, keepdims=True) for sj in st])
    p = [jnp.exp(sj - m) for sj in st]
    l = functools.reduce(lambda a, b: a + b, [pj.sum(axis=1, keepdims=True) for pj in p])
    results = []
    for i, (_, _, vts) in enumerate(problems):
        outs = []
        for hd in range(N_HEADS):
            r = i * N_HEADS + hd
            o = functools.reduce(lambda a, b: a + b,
                                 [_dot(vt[hd * V_HEAD:(hd + 1) * V_HEAD, :], pj[r].astype(BF16))
                                  for vt, pj in zip(vts, p)])
            outs.append(o / l[r])
        results.append(jnp.concatenate(outs, axis=0).T.astype(BF16))
    return results


def _attn_kernel(q_ref, k_ref, vt_ref, cckv_ref, ckpe_ref, wkb_ref, wvb_ref, kg_ref, cs_ref, ns_ref,
                 f_ref, o_ref, fm_ref, kc_ref, vtc_ref):
    @pl.when(pl.program_id(1) == 0)
    def _():
        _emit_kv(cckv_ref[0].astype(BF16), ckpe_ref[0], wkb_ref, wvb_ref, kg_ref, None, kc_ref, vtc_ref)

    o_ref[0], = _attend([(q_ref.at[0], [kc_ref, k_ref.at[0]], [vtc_ref, vt_ref])])
    fm_ref[0] = (_dot(cs_ref[...], f_ref[0, :, :FN_WIDTH])
                 + _dot(ns_ref[...], f_ref[0, :, FN_WIDTH:])).astype(BF16)


def _attention(q3, k3, vt, cache_ckv, cache_kpe_slot, wts, fcs3, cs, ns, tq, name):
    b, sq, _ = q3.shape
    past = cache_ckv.shape[1]
    seq = lambda shape: pl.BlockSpec(shape, lambda bi, qi: (bi, 0, 0))
    const = lambda shape: pl.BlockSpec(shape, lambda bi, qi: (0,) * len(shape))
    return pl.pallas_call(
        _attn_kernel,
        grid=(b, sq // tq),
        in_specs=[pl.BlockSpec((1, tq, QK_WIDTH), lambda bi, qi: (bi, qi, 0)),
                  seq((1, sq, QK_WIDTH)),
                  pl.BlockSpec((V_WIDTH, sq), lambda bi, qi: (0, bi)),
                  seq((1, past, KV_LORA)), seq((1, past, LANES)),
                  const((KV_LORA, QK_WIDTH)), const((V_WIDTH, KV_LORA)), const((1, HEAD_SLOT)),
                  pl.BlockSpec((tq, sq), lambda bi, qi: (qi, 0)),
                  pl.BlockSpec((tq, sq), lambda bi, qi: (qi, 0)),
                  seq((1, sq, 2 * FN_WIDTH))],
        out_specs=[pl.BlockSpec((1, tq, V_WIDTH), lambda bi, qi: (bi, qi, 0)),
                   pl.BlockSpec((1, tq, FN_WIDTH), lambda bi, qi: (bi, qi, 0))],
        out_shape=[jax.ShapeDtypeStruct((b, sq, V_WIDTH), BF16),
                   jax.ShapeDtypeStruct((b, sq, FN_WIDTH), BF16)],
        scratch_shapes=[pltpu.VMEM((past, QK_WIDTH), BF16), pltpu.VMEM((V_WIDTH, past), BF16)],
        compiler_params=_cparams(("parallel", "arbitrary")),
        name=name,
    )(q3, k3, vt, cache_ckv, cache_kpe_slot, wts["w_kb"], wts["w_vb"], wts["kg"], cs, ns, fcs3)


def _router_logits(w, b, h_hi, h_lo):
    rows = w.shape[0]
    w_hi, w_lo = _split_bf16(w)
    y = _dot_nt(jnp.concatenate([w_hi, w_lo], axis=0), h_hi)
    return (y[:rows] + y[rows:]) + _dot_nt(w_hi, h_lo) + b


def _rows(x, n):
    return [x[j:j + 1, :] for j in range(n)]


def _first_argmax(rows, top):
    idx = jnp.full(top.shape, len(rows) - 1, jnp.int32)
    for j in range(len(rows) - 2, -1, -1):
        idx = jnp.where(rows[j] == top, j, idx)
    return idx


def _store_token_major(ref, x, tm):
    for s in range(TOKEN_ROWS):
        ref[pl.ds(s, tm, stride=TOKEN_ROWS), :] = x[:, s * LANES:(s + 1) * LANES]


def _load_token_major(ref, tm):
    return jnp.concatenate([ref[pl.ds(s, tm, stride=TOKEN_ROWS), :] for s in range(TOKEN_ROWS)],
                           axis=1)


def _merge_kernel(*refs, n_first):
    first, second = refs[0:5], refs[5:10]
    (mod_ref, wao_ref, wfn_ref, wout_ref, g2_ref, wrg_ref, brg_ref, x1_ref, h2_ref, grp_ref, rank_ref,
     cnt_ref, carry_ref) = refs[10:]
    i = pl.program_id(0)

    @pl.when(i == 0)
    def _():
        carry_ref[...] = jnp.zeros_like(carry_ref)

    def tile(src):
        x, attn, fm, sga, sgf = (r[...] for r in src)
        a = _dot(attn, wao_ref[...])
        f = _dot(fm, wfn_ref[...])
        u = sga.astype(F32) * a + sgf.astype(F32) * f
        y = _dot(u.astype(BF16), wout_ref[...])
        x1 = x + mod_ref[0, 2:3, :] * y
        x1_ref[...] = x1
        tm = x1.shape[0]
        h2 = (x1 * _rms(x1, D_MODEL) * g2_ref[...]) * (1.0 + mod_ref[0, 4:5, :]) + mod_ref[0, 3:4, :]
        _store_token_major(h2_ref, h2, tm)

        h2_hi, h2_lo = _split_bf16(h2)
        g = _rows(_router_logits(wrg_ref[...], brg_ref[...], h2_hi, h2_lo), N_GROUPS)
        gidx = _first_argmax(g, functools.reduce(jnp.maximum, g))

        onehot = jnp.where(lax.broadcasted_iota(jnp.int32, (SEG_ROWS, tm), 0) == gidx, 1.0, 0.0)
        before = (lax.broadcasted_iota(jnp.int32, (tm, tm), 0)
                  < lax.broadcasted_iota(jnp.int32, (tm, tm), 1))
        prefix = _dot(onehot.astype(BF16), jnp.where(before, 1.0, 0.0).astype(BF16))
        carry = carry_ref[...]
        rank = jnp.sum(onehot * (prefix + carry[:, 0:1]), axis=0, keepdims=True)
        grp_ref[...] = gidx
        rank_ref[...] = rank.astype(jnp.int32)
        carry = carry + jnp.sum(onehot, axis=1, keepdims=True)
        carry_ref[...] = carry
        cnt_ref[...] = carry.astype(jnp.int32)

    @pl.when(i < n_first)
    def _():
        tile(first)

    @pl.when(i >= n_first)
    def _():
        tile(second)


def _merge(first, second, mod3, mod_row_fn, wts):
    n_first = first[0].shape[0] // TM
    n = first[0].shape[0] + second[0].shape[0]
    tok = lambda w: pl.BlockSpec((TM, w), lambda i: (i, 0))
    widths = (D_MODEL, V_WIDTH, FN_WIDTH, D_MODEL, D_MODEL)
    in_first = [pl.BlockSpec((TM, w), lambda i: (jnp.minimum(i, n_first - 1), 0)) for w in widths]
    in_second = [pl.BlockSpec((TM, w), lambda i: (jnp.maximum(i - n_first, 0), 0)) for w in widths]
    return pl.pallas_call(
        functools.partial(_merge_kernel, n_first=n_first),
        grid=(n // TM,),
        in_specs=in_first + in_second + [
                  pl.BlockSpec((1, 6, D_MODEL), lambda i: (mod_row_fn(i), 0, 0)),
                  _const_spec((V_WIDTH, D_MODEL)),
                  _const_spec((FN_WIDTH, D_MODEL)),
                  _const_spec((D_MODEL, D_MODEL)),
                  _const_spec((1, D_MODEL)),
                  _const_spec((ROUTER_ROWS, D_MODEL)),
                  _const_spec((ROUTER_ROWS, 1))],
        out_specs=[tok(D_MODEL),
                   pl.BlockSpec((TM * TOKEN_ROWS, LANES), lambda i: (i, 0)),
                   pl.BlockSpec((1, TM), lambda i: (0, i)),
                   pl.BlockSpec((1, TM), lambda i: (0, i)), _const_spec((SEG_ROWS, LANES))],
        out_shape=[jax.ShapeDtypeStruct((n, D_MODEL), F32),
                   jax.ShapeDtypeStruct((n * TOKEN_ROWS, LANES), F32),
                   jax.ShapeDtypeStruct((1, n), jnp.int32),
                   jax.ShapeDtypeStruct((1, n), jnp.int32),
                   jax.ShapeDtypeStruct((SEG_ROWS, LANES), jnp.int32)],
        scratch_shapes=[pltpu.VMEM((SEG_ROWS, LANES), F32)],
        compiler_params=_cparams(("arbitrary",)),
        name="merge",
    )(*first, *second, mod3, wts["w_ao"], wts["w_fn"], wts["w_out"], wts["g2"],
      wts["w_rg"], wts["b_rg"])


def _token_rows(ref, t):
    start = t * TOKEN_ROWS
    if not isinstance(t, int):
        start = pl.multiple_of(start, TOKEN_ROWS)
    return ref.at[pl.ds(start, TOKEN_ROWS)]


def _row_copy(src, dst, src_tok, dst_tok, sem):
    return pltpu.make_async_copy(_token_rows(src, src_tok), _token_rows(dst, dst_tok), sem)


def _wait_tile(src_hbm, buf, sem):
    pltpu.make_async_copy(src_hbm.at[pl.ds(0, buf.shape[0])], buf, sem).wait()


def _combine_kernel(pos_ref, y_hbm, x1_ref, mod_ref, o1_ref, o2_ref, buf, sem, *, n_first):
    i = pl.program_id(0)
    tm = o1_ref.shape[0]

    def gather(tile_idx, slot):
        def start(r, c):
            _row_copy(y_hbm, buf.at[slot], pos_ref[tile_idx * tm + r], r, sem.at[slot]).start()
            return c

        lax.fori_loop(0, tm, start, 0, unroll=32)

    @pl.when(i == 0)
    def _():
        gather(0, 0)

    @pl.when(i + 1 < pl.num_programs(0))
    def _():
        gather(i + 1, (i + 1) % 2)

    slot = i % 2

    _wait_tile(y_hbm, buf.at[slot], sem.at[slot])
    out = x1_ref[...] + mod_ref[0, 5:6, :] * _load_token_major(buf.at[slot], tm)

    @pl.when(i < n_first)
    def _():
        o1_ref[...] = out

    @pl.when(i >= n_first)
    def _():
        o2_ref[...] = out


def _combine(y_sorted, pos, x1, mod3, mod_row_fn, n_first_tokens):
    n = x1.shape[0]
    n_first = n_first_tokens // TM
    return pl.pallas_call(
        functools.partial(_combine_kernel, n_first=n_first),
        grid_spec=pltpu.PrefetchScalarGridSpec(
            num_scalar_prefetch=1,
            grid=(n // TM,),
            in_specs=[pl.BlockSpec(memory_space=pl.ANY),
                      pl.BlockSpec((TM, D_MODEL), lambda i, pos: (i, 0)),
                      pl.BlockSpec((1, 6, D_MODEL), lambda i, pos: (mod_row_fn(i), 0, 0))],
            out_specs=[pl.BlockSpec((TM, D_MODEL), lambda i, pos: (jnp.minimum(i, n_first - 1), 0)),
                       pl.BlockSpec((TM, D_MODEL), lambda i, pos: (jnp.maximum(i - n_first, 0), 0))],
            scratch_shapes=[pltpu.VMEM((2, TM * TOKEN_ROWS, LANES), F32),
                            pltpu.SemaphoreType.DMA((2,))]),
        out_shape=[jax.ShapeDtypeStruct((n_first_tokens, D_MODEL), F32),
                   jax.ShapeDtypeStruct((n - n_first_tokens, D_MODEL), F32)],
        compiler_params=_cparams(("arbitrary",)),
        name="moe_combine",
    )(pos, y_sorted, x1, mod3)


def _moe_kernel(grp_ref, on_ref, src_ref, h_hbm, wrg_ref, brg_ref, wre_ref, bre_ref, wg_ref, wu_ref,
                wd_ref, o_ref, hbuf, sem, *, tile):
    i = pl.program_id(0)

    def gather(tile_idx, slot):
        def start(r, c):
            _row_copy(h_hbm, hbuf.at[slot], src_ref[tile_idx * tile + r], r, sem.at[slot]).start()
            return c

        lax.fori_loop(0, tile, start, 0, unroll=32)

    @pl.when(jnp.logical_and(i == 0, on_ref[0] == 1))
    def _():
        gather(0, 0)

    nxt = jnp.minimum(i + 1, pl.num_programs(0) - 1)

    @pl.when(jnp.logical_and(i + 1 < pl.num_programs(0), on_ref[nxt] == 1))
    def _():
        gather(i + 1, (i + 1) % 2)

    @pl.when(on_ref[i] == 0)
    def _():
        o_ref[...] = jnp.zeros_like(o_ref)

    @pl.when(on_ref[i] == 1)
    def _():
        slot = i % 2

        _wait_tile(h_hbm, hbuf.at[slot], sem.at[slot])
        h2_hi, h2_lo = _split_bf16(_load_token_major(hbuf.at[slot], tile))

        logits = _router_logits(jnp.concatenate([wrg_ref[...], wre_ref[0]], axis=0),
                                jnp.concatenate([brg_ref[...], bre_ref[0]], axis=0), h2_hi, h2_lo)
        g = _rows(logits, N_GROUPS)
        gmax = functools.reduce(jnp.maximum, g)
        p_top = 1.0 / functools.reduce(lambda p, q: p + q, [jnp.exp(gj - gmax) for gj in g])
        e = _rows(logits[ROUTER_ROWS:], EXPERTS_PER_GROUP)
        m1 = functools.reduce(jnp.maximum, e)
        i1 = _first_argmax(e, m1)
        rest = [jnp.where(i1 == j, -jnp.inf, e[j]) for j in range(EXPERTS_PER_GROUP)]
        m2 = functools.reduce(jnp.maximum, rest)
        i2 = _first_argmax(rest, m2)
        t = jnp.exp(m2 - m1)
        w1 = p_top / (1.0 + t)
        w2 = p_top * t / (1.0 + t)
        row = lax.broadcasted_iota(jnp.int32, (LANES, tile), 0)
        comb = (jnp.where(row == i1, w1, 0.0) + jnp.where(row == i2, w2, 0.0)).T

        gates = [_dot(h2_hi, wg_ref[j]) for j in range(EXPERTS_PER_GROUP)]
        ups = [_dot(h2_hi, wu_ref[j]) for j in range(EXPERTS_PER_GROUP)]
        acts = [((a * _sigmoid(a)) * u * comb[:, j:j + 1]).astype(BF16)
                for j, (a, u) in enumerate(zip(gates, ups))]
        acc = functools.reduce(lambda p, q: p + q,
                               [_dot(acts[j], wd_ref[j]) for j in range(EXPERTS_PER_GROUP)])
        _store_token_major(o_ref, acc, tile)


def _moe(h2tm, src, maps, wts, w_gate, w_up, w_down, tile):
    n_tiles = src.shape[0] // tile
    const = lambda shape: pl.BlockSpec(shape, lambda i, grp, on, src: (0,) * len(shape))
    by_group = lambda shape: pl.BlockSpec(
        shape, lambda i, grp, on, src: (grp[i],) + (0,) * (len(shape) - 1))
    return pl.pallas_call(
        functools.partial(_moe_kernel, tile=tile),
        grid_spec=pltpu.PrefetchScalarGridSpec(
            num_scalar_prefetch=3,
            grid=(n_tiles,),
            in_specs=[pl.BlockSpec(memory_space=pl.ANY),
                      const((ROUTER_ROWS, D_MODEL)), const((ROUTER_ROWS, 1)),
                      by_group((1, ROUTER_ROWS, D_MODEL)), by_group((1, ROUTER_ROWS, 1)),
                      by_group((EXPERTS_PER_GROUP, D_MODEL, D_EXPERT)),
                      by_group((EXPERTS_PER_GROUP, D_MODEL, D_EXPERT)),
                      by_group((EXPERTS_PER_GROUP, D_EXPERT, D_MODEL))],
            out_specs=pl.BlockSpec((tile * TOKEN_ROWS, LANES), lambda i, grp, on, src: (i, 0)),
            scratch_shapes=[pltpu.VMEM((2, tile * TOKEN_ROWS, LANES), F32),
                            pltpu.SemaphoreType.DMA((2,))]),
        out_shape=jax.ShapeDtypeStruct((n_tiles * tile * TOKEN_ROWS, LANES), F32),
        compiler_params=_cparams(("arbitrary",)),
        name="moe",
    )(*maps, src, h2tm, wts["w_rg"], wts["b_rg"], wts["w_re"], wts["b_re"], w_gate, w_up, w_down)


def _invert_kernel(pos_ref, lo_ref, hi_ref, src_ref, *, n, n_fill):
    def zero(p, c):
        src_ref[p] = 0
        return c

    for s in range(n_fill):
        lax.fori_loop(lo_ref[s], hi_ref[s], zero, 0)

    def put(t, c):
        src_ref[pos_ref[t]] = t
        return c

    lax.fori_loop(0, n, put, 0, unroll=8)


def _invert(pos, fill_lo, fill_hi, n_sorted):
    return pl.pallas_call(
        functools.partial(_invert_kernel, n=pos.shape[0], n_fill=fill_lo.shape[0]),
        grid_spec=pltpu.PrefetchScalarGridSpec(
            num_scalar_prefetch=3, grid=(1,), in_specs=[],
            out_specs=pl.BlockSpec(memory_space=pltpu.SMEM)),
        out_shape=jax.ShapeDtypeStruct((n_sorted,), jnp.int32),
        name="moe_invert",
    )(pos, fill_lo, fill_hi)


def _sort_plan(grp, rank, cnt, tile, n_tiles):
    n_tile_grp = (cnt + tile - 1) // tile
    tile_end = jnp.cumsum(n_tile_grp)
    tile_start = tile_end - n_tile_grp
    total = tile_end[-1]
    pos = ((tile_start * tile)[grp] + rank).astype(jnp.int32)
    fill_lo = jnp.concatenate([tile_start * tile + cnt, total[None] * tile]).astype(jnp.int32)
    fill_hi = jnp.concatenate([tile_end * tile, jnp.full((1,), n_tiles * tile)]).astype(jnp.int32)
    src = _invert(pos, fill_lo, fill_hi, n_tiles * tile)
    i = jnp.arange(n_tiles, dtype=jnp.int32)
    tile_grp = jnp.sum((jnp.minimum(i, total - 1)[:, None] >= tile_end[None, :]).astype(jnp.int32),
                       axis=1)
    return pos, src, (tile_grp.astype(jnp.int32), (i < total).astype(jnp.int32))


def _sparse_moe(x1, h2tm, grp, rank, cnt8, mod3, mod_row_fn, wts, w_gate, w_up, w_down, tile,
                n_first_tokens):
    n = x1.shape[0]
    n_tiles = n // tile + N_GROUPS
    pos, src, maps = _sort_plan(grp.reshape(n), rank.reshape(n), cnt8[:N_GROUPS, 0], tile, n_tiles)
    y_sorted = _moe(h2tm, src, maps, wts, w_gate, w_up, w_down, tile)
    return _combine(y_sorted, pos, x1, mod3, mod_row_fn, n_first_tokens)


def _pack_weights(l, norm1_g, q_a_norm_g, w_q_b, kv_a_norm_g, w_kv_b, q_norm_g, k_norm_g,
                  w_attn_o, w_fnet, w_out, norm2_g, w_router_group, b_router_group,
                  w_router_expert, b_router_expert):
    w_qb = jnp.pad(w_q_b[l].reshape(Q_LORA, N_HEADS, QK_HEAD),
                   ((0, 0), (0, 0), (0, HEAD_SLOT - QK_HEAD))).reshape(Q_LORA, QK_WIDTH)
    wkv = w_kv_b[l].reshape(KV_LORA, N_HEADS, QK_NOPE + V_HEAD)
    w_kb = jnp.pad(wkv[:, :, :QK_NOPE],
                   ((0, 0), (0, 0), (0, HEAD_SLOT - QK_NOPE))).reshape(KV_LORA, QK_WIDTH)
    w_vb = wkv[:, :, QK_NOPE:].reshape(KV_LORA, V_WIDTH)
    pad_g = lambda g: jnp.pad(g, (0, HEAD_SLOT - QK_HEAD)).reshape(1, HEAD_SLOT)
    w_rg = jnp.pad(w_router_group[l].T, ((0, ROUTER_ROWS - N_GROUPS), (0, 0)))
    b_rg = jnp.pad(b_router_group[l], (0, ROUTER_ROWS - N_GROUPS)).reshape(ROUTER_ROWS, 1)
    w_re = jnp.pad(w_router_expert[l].T.reshape(N_GROUPS, EXPERTS_PER_GROUP, D_MODEL),
                   ((0, 0), (0, ROUTER_ROWS - EXPERTS_PER_GROUP), (0, 0)))
    b_re = jnp.pad(b_router_expert[l].reshape(N_GROUPS, EXPERTS_PER_GROUP),
                   ((0, 0), (0, ROUTER_ROWS - EXPERTS_PER_GROUP))).reshape(N_GROUPS, ROUTER_ROWS, 1)
    dft_c, dft_ns = _dft_tables(FN_GROUP_W)
    return {
        "g1": norm1_g[l].reshape(1, D_MODEL),
        "qag": q_a_norm_g[l].reshape(1, Q_LORA),
        "w_qb": w_qb.astype(BF16),
        "kvg": kv_a_norm_g[l].reshape(1, KV_LORA),
        "w_kb": w_kb.astype(BF16),
        "w_vb": w_vb.T.astype(BF16),
        "qg": pad_g(q_norm_g[l]),
        "kg": pad_g(k_norm_g[l]),
        "dft_c": jnp.concatenate([jnp.asarray(dft_c), -jnp.asarray(dft_ns)], axis=1).astype(BF16),
        "w_ao": w_attn_o[l].astype(BF16),
        "w_fn": w_fnet[l].astype(BF16),
        "w_out": w_out[l].astype(BF16),
        "g2": norm2_g[l].reshape(1, D_MODEL),
        "w_rg": w_rg,
        "b_rg": b_rg,
        "w_re": w_re,
        "b_re": b_re,
    }


def _layer(xp, xs, cache_ckv_l, cache_kpe_l, mod3, wts, experts):
    bp, sp, _ = xp.shape
    bs, ss, _ = xs.shape
    ctx_row = lambda i: 0
    lat_row = lambda i: 1 + i // (ss // TM)

    xp2 = xp.reshape(bp * sp, D_MODEL)
    ride = (bp * sp) // TM == N_EXPERTS
    cs, ns = (jnp.asarray(t).astype(BF16) for t in _dft_tables(sp))
    outs = _inproj(xp2, mod3, ctx_row, wts, None, True, experts if ride else (), (sp, cs, ns))
    attn, fm, sga, sgf, ckv, kpe = outs[:6]
    w_gate, w_up, w_down = outs[6:] if ride else (w.astype(BF16) for w in experts)
    ctx_set = (xp2, attn, fm, sga, sgf)

    xs2 = xs.reshape(bs * ss, D_MODEL)
    rope_tabs = tuple(jnp.asarray(t) for t in _rope_tables(ss))
    q, k, v, fcs, sga, sgf = _inproj(xs2, mod3, lat_row, wts, rope_tabs, False)
    kpe_slot = jnp.pad(cache_kpe_l, ((0, 0), (0, 0), (QK_NOPE, LANES - QK_HEAD)))
    cs, ns = (jnp.asarray(t).astype(BF16) for t in _dft_tables(ss))
    attn, fm = _attention(q.reshape(bs, ss, QK_WIDTH), k.reshape(bs, ss, QK_WIDTH), v, cache_ckv_l,
                          kpe_slot, wts, fcs.reshape(bs, ss, 2 * FN_WIDTH), cs, ns, TQ_LAT, "attn_lat")
    lat_set = (xs2, attn.reshape(bs * ss, V_WIDTH), fm.reshape(bs * ss, FN_WIDTH), sga, sgf)

    n_ctx_tiles = (bp * sp) // TM
    all_row = lambda i: jnp.where(i < n_ctx_tiles, 0, lat_row(i - n_ctx_tiles))
    x1, h2tm, grp, rank, cnt = _merge(ctx_set, lat_set, mod3, all_row, wts)
    yp, ys = _sparse_moe(x1, h2tm, grp, rank, cnt, mod3, all_row, wts, w_gate, w_up, w_down, MOE_TILE,
                         bp * sp)
    yp, ys = yp.reshape(bp, sp, D_MODEL), ys.reshape(bs, ss, D_MODEL)

    return yp, ys, ckv.reshape(bp, sp, KV_LORA), jnp.swapaxes(kpe, 1, 2)


def kernel(x_prompt, x_sample, cache_ckv, cache_kpe, c, c_ctx, w_mod, b_mod, norm1_g, w_in, q_a_norm_g, w_q_b, kv_a_norm_g, w_kv_b, q_norm_g, k_norm_g, w_attn_o, w_fnet, w_out, norm2_g, w_router_group, b_router_group, w_router_expert, b_router_expert, w_exp_gate, w_exp_up, w_exp_down):
    depth = w_mod.shape[0]
    n_lat = c.shape[0]
    assert 1 + n_lat <= MOD_ROWS
    cond8 = jnp.concatenate([c_ctx[None, :], c, jnp.zeros((MOD_ROWS - 1 - n_lat, D_MODEL), F32)], axis=0)
    xp, xs = x_prompt, x_sample
    ckv_layers, kpe_layers = [], []
    for l in range(depth):
        mod, w_in_p = _adaln(cond8, w_mod[l], b_mod[l].reshape(1, -1), jnp.swapaxes(w_in[l], 0, 1))
        mod3 = mod.reshape(MOD_ROWS, 6, D_MODEL)
        wts = _pack_weights(l, norm1_g, q_a_norm_g, w_q_b, kv_a_norm_g, w_kv_b, q_norm_g,
                            k_norm_g, w_attn_o, w_fnet, w_out, norm2_g, w_router_group,
                            b_router_group, w_router_expert, b_router_expert)
        wts["w_in"] = w_in_p
        xp, xs, ckv, kpe = _layer(xp, xs, cache_ckv[:, l], cache_kpe[:, l], mod3, wts,
                                  (w_exp_gate[l], w_exp_up[l], w_exp_down[l]))
        ckv_layers.append(ckv)
        kpe_layers.append(kpe)
    return xp, xs, jnp.stack(ckv_layers, axis=1), jnp.stack(kpe_layers, axis=1)
```

```python
import functools
import math

import numpy as np
import jax
import jax.numpy as jnp
from jax import lax
from jax.experimental import pallas as pl
from jax.experimental.pallas import tpu as pltpu

D_MODEL = 1024
GRID_W = 64
N_HEADS = 8
Q_LORA = 512
KV_LORA = 256
QK_NOPE = 64
QK_ROPE = 32
V_HEAD = 64
QK_HEAD = QK_NOPE + QK_ROPE
ATTN_SCALE = QK_HEAD ** -0.5
ROPE_BASE = 10000.0
FN_GROUPS = 4
FN_GROUP_W = 128
FN_WIDTH = FN_GROUPS * FN_GROUP_W
N_GROUPS = 4
EXPERTS_PER_GROUP = 4
N_EXPERTS = N_GROUPS * EXPERTS_PER_GROUP
D_EXPERT = 512
EPS = 1e-6

LANES = 128
HEAD_SLOT = LANES
QK_WIDTH = N_HEADS * HEAD_SLOT
V_WIDTH = N_HEADS * V_HEAD
C_QA = 0
C_KVA = C_QA + Q_LORA
C_KPE = C_KVA + KV_LORA
C_FN = C_KPE + LANES
C_GA = C_FN + FN_WIDTH
C_GF = C_GA + D_MODEL
IN_PACKED = C_GF + D_MODEL
SUBLANES = 8
ROUTER_ROWS = SUBLANES
MOD_ROWS = SUBLANES
VMEM_LIMIT = 56 * 1024 * 1024

TM = 512
TQ_LAT = 256
SEG_ROWS = SUBLANES
TOKEN_ROWS = D_MODEL // LANES
MOE_TILE = 256
GATHER_AHEAD = 2
ADALN_STEPS = 8

BF16 = jnp.bfloat16
F32 = jnp.float32


def _cparams(sem):
    return pltpu.CompilerParams(dimension_semantics=sem, vmem_limit_bytes=VMEM_LIMIT)


def _dot(a, b):
    return jnp.dot(a, b, preferred_element_type=F32)


def _dot_nt(a, b):
    return lax.dot_general(a, b, (((1,), (1,)), ((), ())), preferred_element_type=F32)


def _sigmoid(x):
    return 1.0 / (1.0 + jnp.exp(-x))


def _split_bf16(x):
    hi = x.astype(BF16)
    return hi, (x - hi.astype(F32)).astype(BF16)


@functools.lru_cache(maxsize=None)
def _rope_tables(n_pos):
    half = QK_ROPE // 2
    quarter = half // 2
    freqs = ROPE_BASE ** (-np.arange(quarter, dtype=np.float64) / quarter)
    pos = np.arange(n_pos)
    row = (pos // GRID_W).astype(np.float64)
    col = (pos % GRID_W).astype(np.float64)
    cos_t = np.ones((n_pos, LANES), np.float64)
    sin_a = np.zeros((n_pos, LANES), np.float64)
    sin_b = np.zeros((n_pos, LANES), np.float64)
    for base, p in ((QK_NOPE, row), (QK_NOPE + half, col)):
        ang = p[:, None] * freqs[None, :]
        cos_t[:, base:base + quarter] = np.cos(ang)
        cos_t[:, base + quarter:base + half] = np.cos(ang)
        sin_a[:, base:base + quarter] = -np.sin(ang)
        sin_b[:, base + quarter:base + half] = np.sin(ang)
    return (cos_t.astype(np.float32), sin_a.astype(np.float32), sin_b.astype(np.float32))


@functools.lru_cache(maxsize=None)
def _dft_tables(n):
    k = np.arange(n)
    ang = 2.0 * np.pi * ((k[:, None] * k[None, :]) % n) / n
    s = 1.0 / math.sqrt(n)
    return (np.cos(ang) * s).astype(np.float32), (-np.sin(ang) * s).astype(np.float32)


def _adaln_kernel(cond_ref, w_ref, b_ref, win_ref, o_ref, winp_ref):
    c = cond_ref[...]
    s_hi, s_lo = _split_bf16(c * _sigmoid(c))
    w_hi, w_lo = _split_bf16(w_ref[...])
    y = _dot(jnp.concatenate([s_hi, s_lo], axis=0), w_hi)
    o_ref[...] = (y[:MOD_ROWS] + y[MOD_ROWS:]) + _dot(s_hi, w_lo) + b_ref[...]

    winp_ref[:C_KPE, :] = win_ref[:C_KPE, :].astype(BF16)
    winp_ref[C_KPE:C_FN, :] = jnp.zeros((LANES, winp_ref.shape[1]), BF16)
    winp_ref[C_KPE + QK_NOPE:C_KPE + QK_HEAD, :] = win_ref[C_KPE:C_KPE + QK_ROPE, :].astype(BF16)
    winp_ref[C_FN:, :] = win_ref[C_KPE + QK_ROPE:, :].astype(BF16)


def _adaln(cond8, w_mod, b_mod, w_in_t):
    n = w_mod.shape[1]
    steps = ADALN_STEPS
    tn, tr = n // steps, D_MODEL // steps
    return pl.pallas_call(
        _adaln_kernel,
        grid=(steps,),
        in_specs=[pl.BlockSpec((MOD_ROWS, D_MODEL), lambda j: (0, 0)),
                  pl.BlockSpec((D_MODEL, tn), lambda j: (0, j)),
                  pl.BlockSpec((1, tn), lambda j: (0, j)),
                  pl.BlockSpec((w_in_t.shape[0], tr), lambda j: (0, j))],
        out_specs=[pl.BlockSpec((MOD_ROWS, tn), lambda j: (0, j)),
                   pl.BlockSpec((IN_PACKED, tr), lambda j: (0, j))],
        out_shape=[jax.ShapeDtypeStruct((MOD_ROWS, n), F32),
                   jax.ShapeDtypeStruct((IN_PACKED, D_MODEL), BF16)],
        compiler_params=_cparams(("arbitrary",)),
        name="adaln",
    )(cond8, w_mod, b_mod, w_in_t)


def _rms(x, width):
    return lax.rsqrt(jnp.sum(x * x, axis=-1, keepdims=True) * (1.0 / width) + EPS)


def _rope(x, cos_t, sin_a, sin_b):
    return x * cos_t + pltpu.roll(x, LANES - 8, 1) * sin_a + pltpu.roll(x, 8, 1) * sin_b


def _inproj_kernel(*refs, rope, emit_cache, n_cast, seq_len):
    it = iter(refs)
    x_ref, mod_ref, g1_ref, win_ref, qag_ref, wqb_ref, kvg_ref, wkb_ref, wvb_ref = (
        next(it) for _ in range(9))
    qg_ref, kg_ref, dft_ref = next(it), next(it), next(it)
    if rope:
        cos_ref, sa_ref, sb_ref = next(it), next(it), next(it)
    if seq_len:
        seq_cs_ref, seq_ns_ref = next(it), next(it)
    cast_in = [next(it) for _ in range(n_cast)]
    if seq_len:
        attn_ref, fm_ref, sga_ref, sgf_ref = (next(it) for _ in range(4))
    else:
        q_ref, k_ref, v_ref, fcs_ref, sga_ref, sgf_ref = (next(it) for _ in range(6))
    if emit_cache:
        ckv_ref, kpe_ref = next(it), next(it)
    cast_out = [next(it) for _ in range(n_cast)]
    if seq_len:
        q_ref, k_ref, v_ref, fcs_ref = (next(it) for _ in range(4))

    for src, dst in zip(cast_in, cast_out):
        dst[...] = src[...].astype(BF16)

    x = x_ref[...]
    shift = mod_ref[0, 0:1, :]
    scale = mod_ref[0, 1:2, :]
    h = (x * _rms(x, D_MODEL) * g1_ref[...]) * (1.0 + scale) + shift
    hb = h.astype(BF16)

    if rope:
        cos_t, sin_a, sin_b = cos_ref[...], sa_ref[...], sb_ref[...]

    qa = _dot_nt(hb, win_ref[C_QA:C_QA + Q_LORA, :])
    qn = (qa * _rms(qa, Q_LORA) * qag_ref[...]).astype(BF16)
    q = _dot(qn, wqb_ref[...])
    qg = qg_ref[...] * ATTN_SCALE
    for hd in range(N_HEADS):
        qh = q[:, hd * HEAD_SLOT:(hd + 1) * HEAD_SLOT]
        qh = qh * _rms(qh, QK_HEAD) * qg
        if rope:
            qh = _rope(qh, cos_t, sin_a, sin_b)
        q_ref[:, hd * HEAD_SLOT:(hd + 1) * HEAD_SLOT] = qh.astype(BF16)

    kva = _dot_nt(hb, win_ref[C_KVA:C_KVA + KV_LORA, :])
    ckv = kva * _rms(kva, KV_LORA) * kvg_ref[...]
    kpe = _dot_nt(hb, win_ref[C_KPE:C_KPE + LANES, :])
    if emit_cache:
        ckv_ref[...] = ckv
        kpe_t = kpe.T[QK_NOPE:QK_NOPE + QK_ROPE, :]
        for j in range(x.shape[0] // seq_len):
            kpe_ref[j] = kpe_t[:, j * seq_len:(j + 1) * seq_len]
    _emit_kv(ckv.astype(BF16), kpe, wkb_ref, wvb_ref, kg_ref,
             (cos_t, sin_a, sin_b) if rope else None, k_ref, v_ref)

    fn = _dot_nt(hb, win_ref[C_FN:C_FN + FN_WIDTH, :]).astype(BF16)
    for g in range(FN_GROUPS):
        cs = _dot(fn[:, g * FN_GROUP_W:(g + 1) * FN_GROUP_W], dft_ref[...])
        fcs_ref[:, g * FN_GROUP_W:(g + 1) * FN_GROUP_W] = cs[:, :FN_GROUP_W].astype(BF16)
        fcs_ref[:, FN_WIDTH + g * FN_GROUP_W:FN_WIDTH + (g + 1) * FN_GROUP_W] = (
            cs[:, FN_GROUP_W:].astype(BF16))

    sga_ref[...] = _sigmoid(_dot_nt(hb, win_ref[C_GA:C_GA + D_MODEL, :])).astype(BF16)
    sgf_ref[...] = _sigmoid(_dot_nt(hb, win_ref[C_GF:C_GF + D_MODEL, :])).astype(BF16)

    if seq_len:
        for s0 in range(0, x.shape[0], seq_len):
            rows = pl.ds(s0, seq_len)
            attn_ref[rows, :] = _attend(q_ref.at[rows], [k_ref.at[rows]], [v_ref.at[:, rows]])
            fm_ref[rows, :] = (_dot(seq_cs_ref[...], fcs_ref[rows, :FN_WIDTH])
                               + _dot(seq_ns_ref[...], fcs_ref[rows, FN_WIDTH:])).astype(BF16)


def _emit_kv(ckvb, kpe, wkb_ref, wvb_ref, kg_ref, rope_tabs, k_ref, v_ref):
    kg = kg_ref[...]
    v_ref[...] = _dot_nt(wvb_ref[...], ckvb).astype(BF16)
    kn = _dot(ckvb, wkb_ref[...])
    pe_ss = jnp.sum(kpe * kpe, axis=-1, keepdims=True)
    pe_g = kpe * kg
    if rope_tabs is not None:
        pe_g = _rope(pe_g, *rope_tabs)
    for hd in range(N_HEADS):
        knh = kn[:, hd * HEAD_SLOT:(hd + 1) * HEAD_SLOT]
        ss = jnp.sum(knh * knh, axis=-1, keepdims=True) + pe_ss
        r = lax.rsqrt(ss * (1.0 / QK_HEAD) + EPS)
        k_ref[:, hd * HEAD_SLOT:(hd + 1) * HEAD_SLOT] = ((knh * kg + pe_g) * r).astype(BF16)


def _const_spec(shape):
    return pl.BlockSpec(shape, lambda i: (0,) * len(shape))


def _inproj(x2d, mod3, mod_row_fn, wts, rope_tabs, emit_cache, cast=(), seq=None):
    n = x2d.shape[0]
    assert all(w.shape[0] == n // TM for w in cast)
    assert seq is None or TM % seq[0] == 0
    rope = rope_tabs is not None
    tiles_per_seq = None if not rope else rope_tabs[0].shape[0] // TM
    in_specs = [pl.BlockSpec((TM, D_MODEL), lambda i: (i, 0)),
                pl.BlockSpec((1, 6, D_MODEL), lambda i: (mod_row_fn(i), 0, 0)),
                _const_spec((1, D_MODEL)),
                _const_spec((IN_PACKED, D_MODEL)),
                _const_spec((1, Q_LORA)),
                _const_spec((Q_LORA, QK_WIDTH)),
                _const_spec((1, KV_LORA)),
                _const_spec((KV_LORA, QK_WIDTH)),
                _const_spec((V_WIDTH, KV_LORA)),
                _const_spec((1, HEAD_SLOT)),
                _const_spec((1, HEAD_SLOT)),
                _const_spec((FN_GROUP_W, 2 * FN_GROUP_W))]
    args = [x2d, mod3, wts["g1"], wts["w_in"], wts["qag"], wts["w_qb"], wts["kvg"],
            wts["w_kb"], wts["w_vb"], wts["qg"], wts["kg"], wts["dft_c"]]
    if rope:
        in_specs += [pl.BlockSpec((TM, LANES), lambda i: (i % tiles_per_seq, 0))] * 3
        args += list(rope_tabs)
    if seq is not None:
        in_specs += [_const_spec((seq[0], seq[0]))] * 2
        args += [seq[1], seq[2]]
    cast_specs = [pl.BlockSpec((1,) + w.shape[1:], lambda i: (i, 0, 0)) for w in cast]
    in_specs += cast_specs
    args += list(cast)
    qkvf = [((TM, QK_WIDTH), (n, QK_WIDTH)), ((TM, QK_WIDTH), (n, QK_WIDTH)),
            ((V_WIDTH, TM), (V_WIDTH, n)), ((TM, 2 * FN_WIDTH), (n, 2 * FN_WIDTH))]
    if seq is None:
        widths = [None] * 4 + [D_MODEL, D_MODEL]
    else:
        widths = [V_WIDTH, FN_WIDTH, D_MODEL, D_MODEL]
    out_shape, out_specs = [], []
    for j, w in enumerate(widths):
        if w is None:
            out_shape.append(jax.ShapeDtypeStruct(qkvf[j][1], BF16))
            out_specs.append(pl.BlockSpec(qkvf[j][0], (lambda i: (0, i)) if j == 2 else (lambda i: (i, 0))))
        else:
            out_shape.append(jax.ShapeDtypeStruct((n, w), BF16))
            out_specs.append(pl.BlockSpec((TM, w), lambda i: (i, 0)))
    if emit_cache:
        assert seq is not None
        out_shape += [jax.ShapeDtypeStruct((n, KV_LORA), F32),
                      jax.ShapeDtypeStruct((n // seq[0], QK_ROPE, seq[0]), F32)]
        out_specs += [pl.BlockSpec((TM, KV_LORA), lambda i: (i, 0)),
                      pl.BlockSpec((TM // seq[0], QK_ROPE, seq[0]), lambda i: (i, 0, 0))]
    out_shape += [jax.ShapeDtypeStruct(w.shape, BF16) for w in cast]
    out_specs += cast_specs
    scratch = [] if seq is None else [pltpu.VMEM(blk, BF16) for blk, _ in qkvf]
    return pl.pallas_call(
        functools.partial(_inproj_kernel, rope=rope, emit_cache=emit_cache, n_cast=len(cast),
                          seq_len=None if seq is None else seq[0]),
        grid=(n // TM,),
        in_specs=in_specs,
        out_specs=out_specs,
        out_shape=out_shape,
        scratch_shapes=scratch,
        compiler_params=_cparams(("parallel",)),
        name="inproj_lat" if rope else "inproj_ctx",
    )(*args)


def _attend(q, ks, vts):
    head = lambda hd: slice(hd * HEAD_SLOT, (hd + 1) * HEAD_SLOT)
    st = [jnp.stack([_dot_nt(k[:, head(hd)], q[:, head(hd)]) for hd in range(N_HEADS)]) for k in ks]
    m = functools.reduce(jnp.maximum, [sj.max(axis=1, keepdims=True) for sj in st])
    p = [jnp.exp(sj - m) for sj in st]
    l = functools.reduce(lambda a, b: a + b, [pj.sum(axis=1, keepdims=True) for pj in p])
    outs = []
    for hd in range(N_HEADS):
        o = functools.reduce(lambda a, b: a + b,
                             [_dot(vt[hd * V_HEAD:(hd + 1) * V_HEAD, :], pj[hd].astype(BF16))
                              for vt, pj in zip(vts, p)])
        outs.append(o / l[hd])
    return jnp.concatenate(outs, axis=0).T.astype(BF16)


def _attn_kernel(q_ref, k_ref, vt_ref, cckv_ref, ckpe_ref, wkb_ref, wvb_ref, kg_ref, cs_ref, ns_ref,
                 f_ref, o_ref, fm_ref, kc_ref, vtc_ref):
    @pl.when(pl.program_id(1) == 0)
    def _():
        _emit_kv(cckv_ref[0].astype(BF16), ckpe_ref[0], wkb_ref, wvb_ref, kg_ref, None, kc_ref, vtc_ref)

    o_ref[0] = _attend(q_ref.at[0], [kc_ref, k_ref.at[0]], [vtc_ref, vt_ref])
    fm_ref[0] = (_dot(cs_ref[...], f_ref[0, :, :FN_WIDTH])
                 + _dot(ns_ref[...], f_ref[0, :, FN_WIDTH:])).astype(BF16)


def _attention(q3, k3, vt, cache_ckv, cache_kpe_slot, wts, fcs3, cs, ns, tq, name):
    b, sq, _ = q3.shape
    past = cache_ckv.shape[1]
    seq = lambda shape: pl.BlockSpec(shape, lambda bi, qi: (bi, 0, 0))
    const = lambda shape: pl.BlockSpec(shape, lambda bi, qi: (0,) * len(shape))
    return pl.pallas_call(
        _attn_kernel,
        grid=(b, sq // tq),
        in_specs=[pl.BlockSpec((1, tq, QK_WIDTH), lambda bi, qi: (bi, qi, 0)),
                  seq((1, sq, QK_WIDTH)),
                  pl.BlockSpec((V_WIDTH, sq), lambda bi, qi: (0, bi)),
                  seq((1, past, KV_LORA)), seq((1, past, LANES)),
                  const((KV_LORA, QK_WIDTH)), const((V_WIDTH, KV_LORA)), const((1, HEAD_SLOT)),
                  pl.BlockSpec((tq, sq), lambda bi, qi: (qi, 0)),
                  pl.BlockSpec((tq, sq), lambda bi, qi: (qi, 0)),
                  seq((1, sq, 2 * FN_WIDTH))],
        out_specs=[pl.BlockSpec((1, tq, V_WIDTH), lambda bi, qi: (bi, qi, 0)),
                   pl.BlockSpec((1, tq, FN_WIDTH), lambda bi, qi: (bi, qi, 0))],
        out_shape=[jax.ShapeDtypeStruct((b, sq, V_WIDTH), BF16),
                   jax.ShapeDtypeStruct((b, sq, FN_WIDTH), BF16)],
        scratch_shapes=[pltpu.VMEM((past, QK_WIDTH), BF16), pltpu.VMEM((V_WIDTH, past), BF16)],
        compiler_params=_cparams(("parallel", "arbitrary")),
        name=name,
    )(q3, k3, vt, cache_ckv, cache_kpe_slot, wts["w_kb"], wts["w_vb"], wts["kg"], cs, ns, fcs3)


def _router_logits(w, b, h_hi, h_lo):
    rows = w.shape[0]
    w_hi, w_lo = _split_bf16(w)
    y = _dot_nt(jnp.concatenate([w_hi, w_lo], axis=0), h_hi)
    return (y[:rows] + y[rows:]) + _dot_nt(w_hi, h_lo) + b


def _rows(x, n):
    return [x[j:j + 1, :] for j in range(n)]


def _first_argmax(rows, top):
    idx = jnp.full(top.shape, len(rows) - 1, jnp.int32)
    for j in range(len(rows) - 2, -1, -1):
        idx = jnp.where(rows[j] == top, j, idx)
    return idx


def _store_token_major(ref, x, tm):
    for s in range(TOKEN_ROWS):
        ref[pl.ds(s, tm, stride=TOKEN_ROWS), :] = x[:, s * LANES:(s + 1) * LANES]


def _load_token_major(ref, tm):
    return jnp.concatenate([ref[pl.ds(s, tm, stride=TOKEN_ROWS), :] for s in range(TOKEN_ROWS)],
                           axis=1)


def _merge_kernel(*refs, n_first):
    first, second = refs[0:5], refs[5:10]
    (mod_ref, wao_ref, wfn_ref, wout_ref, g2_ref, wrg_ref, brg_ref, x1_ref, h2_ref, grp_ref, rank_ref,
     cnt_ref, carry_ref) = refs[10:]
    i = pl.program_id(0)

    @pl.when(i == 0)
    def _():
        carry_ref[...] = jnp.zeros_like(carry_ref)

    def tile(src):
        x, attn, fm, sga, sgf = (r[...] for r in src)
        a = _dot(attn, wao_ref[...])
        f = _dot(fm, wfn_ref[...])
        u = sga.astype(F32) * a + sgf.astype(F32) * f
        y = _dot(u.astype(BF16), wout_ref[...])
        x1 = x + mod_ref[0, 2:3, :] * y
        x1_ref[...] = x1
        tm = x1.shape[0]
        h2 = (x1 * _rms(x1, D_MODEL) * g2_ref[...]) * (1.0 + mod_ref[0, 4:5, :]) + mod_ref[0, 3:4, :]
        _store_token_major(h2_ref, h2, tm)

        h2_hi, h2_lo = _split_bf16(h2)
        g = _rows(_router_logits(wrg_ref[...], brg_ref[...], h2_hi, h2_lo), N_GROUPS)
        gidx = _first_argmax(g, functools.reduce(jnp.maximum, g))

        onehot = jnp.where(lax.broadcasted_iota(jnp.int32, (SEG_ROWS, tm), 0) == gidx, 1.0, 0.0)
        before = (lax.broadcasted_iota(jnp.int32, (tm, tm), 0)
                  < lax.broadcasted_iota(jnp.int32, (tm, tm), 1))
        prefix = _dot(onehot.astype(BF16), jnp.where(before, 1.0, 0.0).astype(BF16))
        carry = carry_ref[...]
        rank = jnp.sum(onehot * (prefix + carry[:, 0:1]), axis=0, keepdims=True)
        grp_ref[...] = gidx
        rank_ref[...] = rank.astype(jnp.int32)
        carry = carry + jnp.sum(onehot, axis=1, keepdims=True)
        carry_ref[...] = carry
        cnt_ref[...] = carry.astype(jnp.int32)

    @pl.when(i < n_first)
    def _():
        tile(first)

    @pl.when(i >= n_first)
    def _():
        tile(second)


def _merge(first, second, mod3, mod_row_fn, wts):
    n_first = first[0].shape[0] // TM
    n = first[0].shape[0] + second[0].shape[0]
    tok = lambda w: pl.BlockSpec((TM, w), lambda i: (i, 0))
    widths = (D_MODEL, V_WIDTH, FN_WIDTH, D_MODEL, D_MODEL)
    in_first = [pl.BlockSpec((TM, w), lambda i: (jnp.minimum(i, n_first - 1), 0)) for w in widths]
    in_second = [pl.BlockSpec((TM, w), lambda i: (jnp.maximum(i - n_first, 0), 0)) for w in widths]
    return pl.pallas_call(
        functools.partial(_merge_kernel, n_first=n_first),
        grid=(n // TM,),
        in_specs=in_first + in_second + [
                  pl.BlockSpec((1, 6, D_MODEL), lambda i: (mod_row_fn(i), 0, 0)),
                  _const_spec((V_WIDTH, D_MODEL)),
                  _const_spec((FN_WIDTH, D_MODEL)),
                  _const_spec((D_MODEL, D_MODEL)),
                  _const_spec((1, D_MODEL)),
                  _const_spec((ROUTER_ROWS, D_MODEL)),
                  _const_spec((ROUTER_ROWS, 1))],
        out_specs=[tok(D_MODEL),
                   pl.BlockSpec((TM * TOKEN_ROWS, LANES), lambda i: (i, 0)),
                   pl.BlockSpec((1, TM), lambda i: (0, i)),
                   pl.BlockSpec((1, TM), lambda i: (0, i)), _const_spec((SEG_ROWS, LANES))],
        out_shape=[jax.ShapeDtypeStruct((n, D_MODEL), F32),
                   jax.ShapeDtypeStruct((n * TOKEN_ROWS, LANES), F32),
                   jax.ShapeDtypeStruct((1, n), jnp.int32),
                   jax.ShapeDtypeStruct((1, n), jnp.int32),
                   jax.ShapeDtypeStruct((SEG_ROWS, LANES), jnp.int32)],
        scratch_shapes=[pltpu.VMEM((SEG_ROWS, LANES), F32)],
        compiler_params=_cparams(("arbitrary",)),
        name="merge",
    )(*first, *second, mod3, wts["w_ao"], wts["w_fn"], wts["w_out"], wts["g2"],
      wts["w_rg"], wts["b_rg"])


def _token_rows(ref, t):
    start = t * TOKEN_ROWS
    if not isinstance(t, int):
        start = pl.multiple_of(start, TOKEN_ROWS)
    return ref.at[pl.ds(start, TOKEN_ROWS)]


def _row_copy(src, dst, src_tok, dst_tok, sem):
    return pltpu.make_async_copy(_token_rows(src, src_tok), _token_rows(dst, dst_tok), sem)


def _wait_tile(src_hbm, buf, sem):
    pltpu.make_async_copy(src_hbm.at[pl.ds(0, buf.shape[0])], buf, sem).wait()


def _combine_kernel(pos_ref, y_hbm, x1_ref, mod_ref, o1_ref, o2_ref, buf, sem, *, n_first):
    i = pl.program_id(0)
    tm = o1_ref.shape[0]

    def gather(tile_idx, slot):
        def start(r, c):
            _row_copy(y_hbm, buf.at[slot], pos_ref[tile_idx * tm + r], r, sem.at[slot]).start()
            return c

        lax.fori_loop(0, tm, start, 0, unroll=32)

    @pl.when(i == 0)
    def _():
        gather(0, 0)

    @pl.when(i + 1 < pl.num_programs(0))
    def _():
        gather(i + 1, (i + 1) % 2)

    slot = i % 2

    _wait_tile(y_hbm, buf.at[slot], sem.at[slot])
    out = x1_ref[...] + mod_ref[0, 5:6, :] * _load_token_major(buf.at[slot], tm)

    @pl.when(i < n_first)
    def _():
        o1_ref[...] = out

    @pl.when(i >= n_first)
    def _():
        o2_ref[...] = out


def _combine(y_sorted, pos, x1, mod3, mod_row_fn, n_first_tokens):
    n = x1.shape[0]
    n_first = n_first_tokens // TM
    return pl.pallas_call(
        functools.partial(_combine_kernel, n_first=n_first),
        grid_spec=pltpu.PrefetchScalarGridSpec(
            num_scalar_prefetch=1,
            grid=(n // TM,),
            in_specs=[pl.BlockSpec(memory_space=pl.ANY),
                      pl.BlockSpec((TM, D_MODEL), lambda i, pos: (i, 0)),
                      pl.BlockSpec((1, 6, D_MODEL), lambda i, pos: (mod_row_fn(i), 0, 0))],
            out_specs=[pl.BlockSpec((TM, D_MODEL), lambda i, pos: (jnp.minimum(i, n_first - 1), 0)),
                       pl.BlockSpec((TM, D_MODEL), lambda i, pos: (jnp.maximum(i - n_first, 0), 0))],
            scratch_shapes=[pltpu.VMEM((2, TM * TOKEN_ROWS, LANES), F32),
                            pltpu.SemaphoreType.DMA((2,))]),
        out_shape=[jax.ShapeDtypeStruct((n_first_tokens, D_MODEL), F32),
                   jax.ShapeDtypeStruct((n - n_first_tokens, D_MODEL), F32)],
        compiler_params=_cparams(("arbitrary",)),
        name="moe_combine",
    )(pos, y_sorted, x1, mod3)


def _moe_kernel(grp_ref, on_ref, src_ref, h_hbm, wrg_ref, brg_ref, wre_ref, bre_ref, wg_ref, wu_ref,
                wd_ref, o_ref, hbuf, sem, *, tile):
    i = pl.program_id(0)
    n_tiles = pl.num_programs(0)

    def start(tile_idx, r):
        slot = tile_idx % (GATHER_AHEAD + 1)
        _row_copy(h_hbm, hbuf.at[slot], src_ref[tile_idx * tile + r], r, sem.at[slot]).start()

    for first in range(GATHER_AHEAD):
        @pl.when(jnp.logical_and(i == 0, on_ref[first] == 1))
        def _():
            def body(r, c):
                start(first, r)
                return c

            lax.fori_loop(0, tile, body, 0, unroll=32)

    def run(fetch_ahead):
        def issue(chunk, n_chunks):
            if fetch_ahead:
                per = tile // n_chunks
                for r in range(chunk * per, (chunk + 1) * per):
                    start(i + GATHER_AHEAD, r)

        slot = i % (GATHER_AHEAD + 1)
        _wait_tile(h_hbm, hbuf.at[slot], sem.at[slot])
        h2_hi, h2_lo = _split_bf16(_load_token_major(hbuf.at[slot], tile))

        logits = _router_logits(jnp.concatenate([wrg_ref[...], wre_ref[0]], axis=0),
                                jnp.concatenate([brg_ref[...], bre_ref[0]], axis=0), h2_hi, h2_lo)
        g = _rows(logits, N_GROUPS)
        gmax = functools.reduce(jnp.maximum, g)
        p_top = 1.0 / functools.reduce(lambda p, q: p + q, [jnp.exp(gj - gmax) for gj in g])
        e = _rows(logits[ROUTER_ROWS:], EXPERTS_PER_GROUP)
        m1 = functools.reduce(jnp.maximum, e)
        i1 = _first_argmax(e, m1)
        rest = [jnp.where(i1 == j, -jnp.inf, e[j]) for j in range(EXPERTS_PER_GROUP)]
        m2 = functools.reduce(jnp.maximum, rest)
        i2 = _first_argmax(rest, m2)
        t = jnp.exp(m2 - m1)
        w1 = p_top / (1.0 + t)
        w2 = p_top * t / (1.0 + t)
        row = lax.broadcasted_iota(jnp.int32, (LANES, tile), 0)
        comb = (jnp.where(row == i1, w1, 0.0) + jnp.where(row == i2, w2, 0.0)).T

        gates, ups = [], []
        for j in range(EXPERTS_PER_GROUP):
            gates.append(_dot(h2_hi, wg_ref[j]))
            issue(2 * j, 2 * EXPERTS_PER_GROUP)
            ups.append(_dot(h2_hi, wu_ref[j]))
            issue(2 * j + 1, 2 * EXPERTS_PER_GROUP)
        acts = [((a * _sigmoid(a)) * u * comb[:, j:j + 1]).astype(BF16)
                for j, (a, u) in enumerate(zip(gates, ups))]
        acc = functools.reduce(lambda p, q: p + q,
                               [_dot(acts[j], wd_ref[j]) for j in range(EXPERTS_PER_GROUP)])
        _store_token_major(o_ref, acc, tile)

    @pl.when(on_ref[i] == 0)
    def _():
        o_ref[...] = jnp.zeros_like(o_ref)

    ahead = jnp.minimum(i + GATHER_AHEAD, n_tiles - 1)
    fetch = jnp.logical_and(i + GATHER_AHEAD < n_tiles, on_ref[ahead] == 1)

    @pl.when(jnp.logical_and(on_ref[i] == 1, fetch))
    def _():
        run(True)

    @pl.when(jnp.logical_and(on_ref[i] == 1, jnp.logical_not(fetch)))
    def _():
        run(False)


def _moe(h2tm, src, maps, wts, w_gate, w_up, w_down, tile):
    n_tiles = src.shape[0] // tile
    const = lambda shape: pl.BlockSpec(shape, lambda i, grp, on, src: (0,) * len(shape))
    by_group = lambda shape: pl.BlockSpec(
        shape, lambda i, grp, on, src: (grp[i],) + (0,) * (len(shape) - 1))
    return pl.pallas_call(
        functools.partial(_moe_kernel, tile=tile),
        grid_spec=pltpu.PrefetchScalarGridSpec(
            num_scalar_prefetch=3,
            grid=(n_tiles,),
            in_specs=[pl.BlockSpec(memory_space=pl.ANY),
                      const((ROUTER_ROWS, D_MODEL)), const((ROUTER_ROWS, 1)),
                      by_group((1, ROUTER_ROWS, D_MODEL)), by_group((1, ROUTER_ROWS, 1)),
                      by_group((EXPERTS_PER_GROUP, D_MODEL, D_EXPERT)),
                      by_group((EXPERTS_PER_GROUP, D_MODEL, D_EXPERT)),
                      by_group((EXPERTS_PER_GROUP, D_EXPERT, D_MODEL))],
            out_specs=pl.BlockSpec((tile * TOKEN_ROWS, LANES), lambda i, grp, on, src: (i, 0)),
            scratch_shapes=[pltpu.VMEM((GATHER_AHEAD + 1, tile * TOKEN_ROWS, LANES), F32),
                            pltpu.SemaphoreType.DMA((GATHER_AHEAD + 1,))]),
        out_shape=jax.ShapeDtypeStruct((n_tiles * tile * TOKEN_ROWS, LANES), F32),
        compiler_params=_cparams(("arbitrary",)),
        name="moe",
    )(*maps, src, h2tm, wts["w_rg"], wts["b_rg"], wts["w_re"], wts["b_re"], w_gate, w_up, w_down)


def _invert_kernel(pos_ref, lo_ref, hi_ref, src_ref, *, n, n_fill):
    def zero(p, c):
        src_ref[p] = 0
        return c

    for s in range(n_fill):
        lax.fori_loop(lo_ref[s], hi_ref[s], zero, 0)

    def put(t, c):
        src_ref[pos_ref[t]] = t
        return c

    lax.fori_loop(0, n, put, 0, unroll=8)


def _invert(pos, fill_lo, fill_hi, n_sorted):
    return pl.pallas_call(
        functools.partial(_invert_kernel, n=pos.shape[0], n_fill=fill_lo.shape[0]),
        grid_spec=pltpu.PrefetchScalarGridSpec(
            num_scalar_prefetch=3, grid=(1,), in_specs=[],
            out_specs=pl.BlockSpec(memory_space=pltpu.SMEM)),
        out_shape=jax.ShapeDtypeStruct((n_sorted,), jnp.int32),
        name="moe_invert",
    )(pos, fill_lo, fill_hi)


def _sort_plan(grp, rank, cnt, tile, n_tiles):
    n_tile_grp = (cnt + tile - 1) // tile
    tile_end = jnp.cumsum(n_tile_grp)
    tile_start = tile_end - n_tile_grp
    total = tile_end[-1]
    pos = ((tile_start * tile)[grp] + rank).astype(jnp.int32)
    fill_lo = jnp.concatenate([tile_start * tile + cnt, total[None] * tile]).astype(jnp.int32)
    fill_hi = jnp.concatenate([tile_end * tile, jnp.full((1,), n_tiles * tile)]).astype(jnp.int32)
    src = _invert(pos, fill_lo, fill_hi, n_tiles * tile)
    i = jnp.arange(n_tiles, dtype=jnp.int32)
    tile_grp = jnp.sum((jnp.minimum(i, total - 1)[:, None] >= tile_end[None, :]).astype(jnp.int32),
                       axis=1)
    return pos, src, (tile_grp.astype(jnp.int32), (i < total).astype(jnp.int32))


def _sparse_moe(x1, h2tm, grp, rank, cnt8, mod3, mod_row_fn, wts, w_gate, w_up, w_down, tile,
                n_first_tokens):
    n = x1.shape[0]
    n_tiles = n // tile + N_GROUPS
    pos, src, maps = _sort_plan(grp.reshape(n), rank.reshape(n), cnt8[:N_GROUPS, 0], tile, n_tiles)
    y_sorted = _moe(h2tm, src, maps, wts, w_gate, w_up, w_down, tile)
    return _combine(y_sorted, pos, x1, mod3, mod_row_fn, n_first_tokens)


def _pack_weights(l, norm1_g, q_a_norm_g, w_q_b, kv_a_norm_g, w_kv_b, q_norm_g, k_norm_g,
                  w_attn_o, w_fnet, w_out, norm2_g, w_router_group, b_router_group,
                  w_router_expert, b_router_expert):
    w_qb = jnp.pad(w_q_b[l].reshape(Q_LORA, N_HEADS, QK_HEAD),
                   ((0, 0), (0, 0), (0, HEAD_SLOT - QK_HEAD))).reshape(Q_LORA, QK_WIDTH)
    wkv = w_kv_b[l].reshape(KV_LORA, N_HEADS, QK_NOPE + V_HEAD)
    w_kb = jnp.pad(wkv[:, :, :QK_NOPE],
                   ((0, 0), (0, 0), (0, HEAD_SLOT - QK_NOPE))).reshape(KV_LORA, QK_WIDTH)
    w_vb = wkv[:, :, QK_NOPE:].reshape(KV_LORA, V_WIDTH)
    pad_g = lambda g: jnp.pad(g, (0, HEAD_SLOT - QK_HEAD)).reshape(1, HEAD_SLOT)
    w_rg = jnp.pad(w_router_group[l].T, ((0, ROUTER_ROWS - N_GROUPS), (0, 0)))
    b_rg = jnp.pad(b_router_group[l], (0, ROUTER_ROWS - N_GROUPS)).reshape(ROUTER_ROWS, 1)
    w_re = jnp.pad(w_router_expert[l].T.reshape(N_GROUPS, EXPERTS_PER_GROUP, D_MODEL),
                   ((0, 0), (0, ROUTER_ROWS - EXPERTS_PER_GROUP), (0, 0)))
    b_re = jnp.pad(b_router_expert[l].reshape(N_GROUPS, EXPERTS_PER_GROUP),
                   ((0, 0), (0, ROUTER_ROWS - EXPERTS_PER_GROUP))).reshape(N_GROUPS, ROUTER_ROWS, 1)
    dft_c, dft_ns = _dft_tables(FN_GROUP_W)
    return {
        "g1": norm1_g[l].reshape(1, D_MODEL),
        "qag": q_a_norm_g[l].reshape(1, Q_LORA),
        "w_qb": w_qb.astype(BF16),
        "kvg": kv_a_norm_g[l].reshape(1, KV_LORA),
        "w_kb": w_kb.astype(BF16),
        "w_vb": w_vb.T.astype(BF16),
        "qg": pad_g(q_norm_g[l]),
        "kg": pad_g(k_norm_g[l]),
        "dft_c": jnp.concatenate([jnp.asarray(dft_c), -jnp.asarray(dft_ns)], axis=1).astype(BF16),
        "w_ao": w_attn_o[l].astype(BF16),
        "w_fn": w_fnet[l].astype(BF16),
        "w_out": w_out[l].astype(BF16),
        "g2": norm2_g[l].reshape(1, D_MODEL),
        "w_rg": w_rg,
        "b_rg": b_rg,
        "w_re": w_re,
        "b_re": b_re,
    }


def _layer(xp, xs, cache_ckv_l, cache_kpe_l, mod3, wts, experts):
    bp, sp, _ = xp.shape
    bs, ss, _ = xs.shape
    ctx_row = lambda i: 0
    lat_row = lambda i: 1 + i // (ss // TM)

    xp2 = xp.reshape(bp * sp, D_MODEL)
    ride = (bp * sp) // TM == N_EXPERTS
    cs, ns = (jnp.asarray(t).astype(BF16) for t in _dft_tables(sp))
    outs = _inproj(xp2, mod3, ctx_row, wts, None, True, experts if ride else (), (sp, cs, ns))
    attn, fm, sga, sgf, ckv, kpe = outs[:6]
    w_gate, w_up, w_down = outs[6:] if ride else (w.astype(BF16) for w in experts)
    ctx_set = (xp2, attn, fm, sga, sgf)

    xs2 = xs.reshape(bs * ss, D_MODEL)
    rope_tabs = tuple(jnp.asarray(t) for t in _rope_tables(ss))
    q, k, v, fcs, sga, sgf = _inproj(xs2, mod3, lat_row, wts, rope_tabs, False)
    kpe_slot = jnp.pad(cache_kpe_l, ((0, 0), (0, 0), (QK_NOPE, LANES - QK_HEAD)))
    cs, ns = (jnp.asarray(t).astype(BF16) for t in _dft_tables(ss))
    attn, fm = _attention(q.reshape(bs, ss, QK_WIDTH), k.reshape(bs, ss, QK_WIDTH), v, cache_ckv_l,
                          kpe_slot, wts, fcs.reshape(bs, ss, 2 * FN_WIDTH), cs, ns, TQ_LAT, "attn_lat")
    lat_set = (xs2, attn.reshape(bs * ss, V_WIDTH), fm.reshape(bs * ss, FN_WIDTH), sga, sgf)

    n_ctx_tiles = (bp * sp) // TM
    all_row = lambda i: jnp.where(i < n_ctx_tiles, 0, lat_row(i - n_ctx_tiles))
    x1, h2tm, grp, rank, cnt = _merge(ctx_set, lat_set, mod3, all_row, wts)
    yp, ys = _sparse_moe(x1, h2tm, grp, rank, cnt, mod3, all_row, wts, w_gate, w_up, w_down, MOE_TILE,
                         bp * sp)
    yp, ys = yp.reshape(bp, sp, D_MODEL), ys.reshape(bs, ss, D_MODEL)

    return yp, ys, ckv.reshape(bp, sp, KV_LORA), jnp.swapaxes(kpe, 1, 2)


def kernel(x_prompt, x_sample, cache_ckv, cache_kpe, c, c_ctx, w_mod, b_mod, norm1_g, w_in, q_a_norm_g, w_q_b, kv_a_norm_g, w_kv_b, q_norm_g, k_norm_g, w_attn_o, w_fnet, w_out, norm2_g, w_router_group, b_router_group, w_router_expert, b_router_expert, w_exp_gate, w_exp_up, w_exp_down):
    depth = w_mod.shape[0]
    n_lat = c.shape[0]
    assert 1 + n_lat <= MOD_ROWS
    cond8 = jnp.concatenate([c_ctx[None, :], c, jnp.zeros((MOD_ROWS - 1 - n_lat, D_MODEL), F32)], axis=0)
    xp, xs = x_prompt, x_sample
    ckv_layers, kpe_layers = [], []
    for l in range(depth):
        mod, w_in_p = _adaln(cond8, w_mod[l], b_mod[l].reshape(1, -1), jnp.swapaxes(w_in[l], 0, 1))
        mod3 = mod.reshape(MOD_ROWS, 6, D_MODEL)
        wts = _pack_weights(l, norm1_g, q_a_norm_g, w_q_b, kv_a_norm_g, w_kv_b, q_norm_g,
                            k_norm_g, w_attn_o, w_fnet, w_out, norm2_g, w_router_group,
                            b_router_group, w_router_expert, b_router_expert)
        wts["w_in"] = w_in_p
        xp, xs, ckv, kpe = _layer(xp, xs, cache_ckv[:, l], cache_kpe[:, l], mod3, wts,
                                  (w_exp_gate[l], w_exp_up[l], w_exp_down[l]))
        ckv_layers.append(ckv)
        kpe_layers.append(kpe)
    return xp, xs, jnp.stack(ckv_layers, axis=1), jnp.stack(kpe_layers, axis=1)
```

```python
import functools
import math

import numpy as np
import jax
import jax.numpy as jnp
from jax import lax
from jax.experimental import pallas as pl
from jax.experimental.pallas import tpu as pltpu

D_MODEL = 1024
GRID_W = 64
N_HEADS = 8
Q_LORA = 512
KV_LORA = 256
QK_NOPE = 64
QK_ROPE = 32
V_HEAD = 64
QK_HEAD = QK_NOPE + QK_ROPE
ATTN_SCALE = QK_HEAD ** -0.5
ROPE_BASE = 10000.0
FN_GROUPS = 4
FN_GROUP_W = 128
FN_WIDTH = FN_GROUPS * FN_GROUP_W
N_GROUPS = 4
EXPERTS_PER_GROUP = 4
N_EXPERTS = N_GROUPS * EXPERTS_PER_GROUP
D_EXPERT = 512
EPS = 1e-6

LANES = 128
HEAD_SLOT = LANES
QK_WIDTH = N_HEADS * HEAD_SLOT
V_WIDTH = N_HEADS * V_HEAD
C_QA = 0
C_KVA = C_QA + Q_LORA
C_KPE = C_KVA + KV_LORA
C_FN = C_KPE + LANES
C_GA = C_FN + FN_WIDTH
C_GF = C_GA + D_MODEL
IN_PACKED = C_GF + D_MODEL
SUBLANES = 8
ROUTER_ROWS = SUBLANES
MOD_ROWS = SUBLANES
VMEM_LIMIT = 56 * 1024 * 1024

TM = 512
TQ_LAT = 256
SEG_ROWS = SUBLANES
TOKEN_ROWS = D_MODEL // LANES
MOE_TILE = 256
GATHER_AHEAD = 2
ADALN_STEPS = 8

BF16 = jnp.bfloat16
F32 = jnp.float32


def _cparams(sem):
    return pltpu.CompilerParams(dimension_semantics=sem, vmem_limit_bytes=VMEM_LIMIT)


def _dot(a, b):
    return jnp.dot(a, b, preferred_element_type=F32)


def _dot_nt(a, b):
    return lax.dot_general(a, b, (((1,), (1,)), ((), ())), preferred_element_type=F32)


def _sigmoid(x):
    return 1.0 / (1.0 + jnp.exp(-x))


def _split_bf16(x):
    hi = x.astype(BF16)
    return hi, (x - hi.astype(F32)).astype(BF16)


@functools.lru_cache(maxsize=None)
def _rope_tables(n_pos):
    half = QK_ROPE // 2
    quarter = half // 2
    freqs = ROPE_BASE ** (-np.arange(quarter, dtype=np.float64) / quarter)
    pos = np.arange(n_pos)
    row = (pos // GRID_W).astype(np.float64)
    col = (pos % GRID_W).astype(np.float64)
    cos_t = np.ones((n_pos, LANES), np.float64)
    sin_a = np.zeros((n_pos, LANES), np.float64)
    sin_b = np.zeros((n_pos, LANES), np.float64)
    for base, p in ((QK_NOPE, row), (QK_NOPE + half, col)):
        ang = p[:, None] * freqs[None, :]
        cos_t[:, base:base + quarter] = np.cos(ang)
        cos_t[:, base + quarter:base + half] = np.cos(ang)
        sin_a[:, base:base + quarter] = -np.sin(ang)
        sin_b[:, base + quarter:base + half] = np.sin(ang)
    return (cos_t.astype(np.float32), sin_a.astype(np.float32), sin_b.astype(np.float32))


@functools.lru_cache(maxsize=None)
def _dft_tables(n):
    k = np.arange(n)
    ang = 2.0 * np.pi * ((k[:, None] * k[None, :]) % n) / n
    s = 1.0 / math.sqrt(n)
    return (np.cos(ang) * s).astype(np.float32), (-np.sin(ang) * s).astype(np.float32)


def _adaln_kernel(cond_ref, w_ref, b_ref, win_ref, o_ref, winp_ref):
    c = cond_ref[...]
    s_hi, s_lo = _split_bf16(c * _sigmoid(c))
    w_hi, w_lo = _split_bf16(w_ref[...])
    y = _dot(jnp.concatenate([s_hi, s_lo], axis=0), w_hi)
    o_ref[...] = (y[:MOD_ROWS] + y[MOD_ROWS:]) + _dot(s_hi, w_lo) + b_ref[...]

    winp_ref[:C_KPE, :] = win_ref[:C_KPE, :].astype(BF16)
    winp_ref[C_KPE:C_FN, :] = jnp.zeros((LANES, winp_ref.shape[1]), BF16)
    winp_ref[C_KPE + QK_NOPE:C_KPE + QK_HEAD, :] = win_ref[C_KPE:C_KPE + QK_ROPE, :].astype(BF16)
    winp_ref[C_FN:, :] = win_ref[C_KPE + QK_ROPE:, :].astype(BF16)


def _adaln(cond8, w_mod, b_mod, w_in_t):
    n = w_mod.shape[1]
    steps = ADALN_STEPS
    tn, tr = n // steps, D_MODEL // steps
    return pl.pallas_call(
        _adaln_kernel,
        grid=(steps,),
        in_specs=[pl.BlockSpec((MOD_ROWS, D_MODEL), lambda j: (0, 0)),
                  pl.BlockSpec((D_MODEL, tn), lambda j: (0, j)),
                  pl.BlockSpec((1, tn), lambda j: (0, j)),
                  pl.BlockSpec((w_in_t.shape[0], tr), lambda j: (0, j))],
        out_specs=[pl.BlockSpec((MOD_ROWS, tn), lambda j: (0, j)),
                   pl.BlockSpec((IN_PACKED, tr), lambda j: (0, j))],
        out_shape=[jax.ShapeDtypeStruct((MOD_ROWS, n), F32),
                   jax.ShapeDtypeStruct((IN_PACKED, D_MODEL), BF16)],
        compiler_params=_cparams(("arbitrary",)),
        name="adaln",
    )(cond8, w_mod, b_mod, w_in_t)


def _rms(x, width):
    return lax.rsqrt(jnp.sum(x * x, axis=-1, keepdims=True) * (1.0 / width) + EPS)


def _rope(x, cos_t, sin_a, sin_b):
    return x * cos_t + pltpu.roll(x, LANES - 8, 1) * sin_a + pltpu.roll(x, 8, 1) * sin_b


def _inproj_kernel(*refs, rope, emit_cache, n_cast, seq_len):
    it = iter(refs)
    x_ref, mod_ref, g1_ref, win_ref, qag_ref, wqb_ref, kvg_ref, wkb_ref, wvb_ref = (
        next(it) for _ in range(9))
    qg_ref, kg_ref, dft_ref = next(it), next(it), next(it)
    if rope:
        cos_ref, sa_ref, sb_ref = next(it), next(it), next(it)
    if seq_len:
        seq_cs_ref, seq_ns_ref = next(it), next(it)
    cast_in = [next(it) for _ in range(n_cast)]
    if seq_len:
        attn_ref, fm_ref, sga_ref, sgf_ref = (next(it) for _ in range(4))
    else:
        q_ref, k_ref, v_ref, fcs_ref, sga_ref, sgf_ref = (next(it) for _ in range(6))
    if emit_cache:
        ckv_ref, kpe_ref = next(it), next(it)
    cast_out = [next(it) for _ in range(n_cast)]
    if seq_len:
        q_ref, k_ref, v_ref, fcs_ref = (next(it) for _ in range(4))

    for src, dst in zip(cast_in, cast_out):
        dst[...] = src[...].astype(BF16)

    x = x_ref[...]
    shift = mod_ref[0, 0:1, :]
    scale = mod_ref[0, 1:2, :]
    h = (x * _rms(x, D_MODEL) * g1_ref[...]) * (1.0 + scale) + shift
    hb = h.astype(BF16)

    if rope:
        cos_t, sin_a, sin_b = cos_ref[...], sa_ref[...], sb_ref[...]

    qa = _dot_nt(hb, win_ref[C_QA:C_QA + Q_LORA, :])
    qn = (qa * _rms(qa, Q_LORA) * qag_ref[...]).astype(BF16)
    q = _dot(qn, wqb_ref[...])
    qg = qg_ref[...] * ATTN_SCALE
    for hd in range(N_HEADS):
        qh = q[:, hd * HEAD_SLOT:(hd + 1) * HEAD_SLOT]
        qh = qh * _rms(qh, QK_HEAD) * qg
        if rope:
            qh = _rope(qh, cos_t, sin_a, sin_b)
        q_ref[:, hd * HEAD_SLOT:(hd + 1) * HEAD_SLOT] = qh.astype(BF16)

    kva = _dot_nt(hb, win_ref[C_KVA:C_KVA + KV_LORA, :])
    ckv = kva * _rms(kva, KV_LORA) * kvg_ref[...]
    kpe = _dot_nt(hb, win_ref[C_KPE:C_KPE + LANES, :])
    if emit_cache:
        ckv_ref[...] = ckv
        kpe_t = kpe.T[QK_NOPE:QK_NOPE + QK_ROPE, :]
        for j in range(x.shape[0] // seq_len):
            kpe_ref[j] = kpe_t[:, j * seq_len:(j + 1) * seq_len]
    _emit_kv(ckv.astype(BF16), kpe, wkb_ref, wvb_ref, kg_ref,
             (cos_t, sin_a, sin_b) if rope else None, k_ref, v_ref)

    fn = _dot_nt(hb, win_ref[C_FN:C_FN + FN_WIDTH, :]).astype(BF16)
    for g in range(FN_GROUPS):
        cs = _dot(fn[:, g * FN_GROUP_W:(g + 1) * FN_GROUP_W], dft_ref[...])
        fcs_ref[:, g * FN_GROUP_W:(g + 1) * FN_GROUP_W] = cs[:, :FN_GROUP_W].astype(BF16)
        fcs_ref[:, FN_WIDTH + g * FN_GROUP_W:FN_WIDTH + (g + 1) * FN_GROUP_W] = (
            cs[:, FN_GROUP_W:].astype(BF16))

    def gate(ref, c0, half):
        def run():
            w = D_MODEL // 2
            ref[:, pl.ds(half * w, w)] = _sigmoid(
                _dot_nt(hb, win_ref[pl.ds(c0 + half * w, w), :])).astype(BF16)
        return run

    gates = [gate(ref, c0, half) for ref, c0 in ((sga_ref, C_GA), (sgf_ref, C_GF)) for half in range(2)]

    if seq_len:
        n_seq = x.shape[0] // seq_len
        for j in range(n_seq):
            rows = pl.ds(j * seq_len, seq_len)
            fill = gates[j * len(gates) // n_seq:(j + 1) * len(gates) // n_seq]
            attn_ref[rows, :] = _attend(q_ref.at[rows], [k_ref.at[rows]], [v_ref.at[:, rows]], fill)
            fm_ref[rows, :] = (_dot(seq_cs_ref[...], fcs_ref[rows, :FN_WIDTH])
                               + _dot(seq_ns_ref[...], fcs_ref[rows, FN_WIDTH:])).astype(BF16)
    else:
        for run_gate in gates:
            run_gate()


def _emit_kv(ckvb, kpe, wkb_ref, wvb_ref, kg_ref, rope_tabs, k_ref, v_ref):
    kg = kg_ref[...]
    v_ref[...] = _dot_nt(wvb_ref[...], ckvb).astype(BF16)
    kn = _dot(ckvb, wkb_ref[...])
    pe_ss = jnp.sum(kpe * kpe, axis=-1, keepdims=True)
    pe_g = kpe * kg
    if rope_tabs is not None:
        pe_g = _rope(pe_g, *rope_tabs)
    for hd in range(N_HEADS):
        knh = kn[:, hd * HEAD_SLOT:(hd + 1) * HEAD_SLOT]
        ss = jnp.sum(knh * knh, axis=-1, keepdims=True) + pe_ss
        r = lax.rsqrt(ss * (1.0 / QK_HEAD) + EPS)
        k_ref[:, hd * HEAD_SLOT:(hd + 1) * HEAD_SLOT] = ((knh * kg + pe_g) * r).astype(BF16)


def _const_spec(shape):
    return pl.BlockSpec(shape, lambda i: (0,) * len(shape))


def _inproj(x2d, mod3, mod_row_fn, wts, rope_tabs, emit_cache, cast=(), seq=None):
    n = x2d.shape[0]
    assert all(w.shape[0] == n // TM for w in cast)
    assert seq is None or TM % seq[0] == 0
    rope = rope_tabs is not None
    tiles_per_seq = None if not rope else rope_tabs[0].shape[0] // TM
    in_specs = [pl.BlockSpec((TM, D_MODEL), lambda i: (i, 0)),
                pl.BlockSpec((1, 6, D_MODEL), lambda i: (mod_row_fn(i), 0, 0)),
                _const_spec((1, D_MODEL)),
                _const_spec((IN_PACKED, D_MODEL)),
                _const_spec((1, Q_LORA)),
                _const_spec((Q_LORA, QK_WIDTH)),
                _const_spec((1, KV_LORA)),
                _const_spec((KV_LORA, QK_WIDTH)),
                _const_spec((V_WIDTH, KV_LORA)),
                _const_spec((1, HEAD_SLOT)),
                _const_spec((1, HEAD_SLOT)),
                _const_spec((FN_GROUP_W, 2 * FN_GROUP_W))]
    args = [x2d, mod3, wts["g1"], wts["w_in"], wts["qag"], wts["w_qb"], wts["kvg"],
            wts["w_kb"], wts["w_vb"], wts["qg"], wts["kg"], wts["dft_c"]]
    if rope:
        in_specs += [pl.BlockSpec((TM, LANES), lambda i: (i % tiles_per_seq, 0))] * 3
        args += list(rope_tabs)
    if seq is not None:
        in_specs += [_const_spec((seq[0], seq[0]))] * 2
        args += [seq[1], seq[2]]
    cast_specs = [pl.BlockSpec((1,) + w.shape[1:], lambda i: (i, 0, 0)) for w in cast]
    in_specs += cast_specs
    args += list(cast)
    qkvf = [((TM, QK_WIDTH), (n, QK_WIDTH)), ((TM, QK_WIDTH), (n, QK_WIDTH)),
            ((V_WIDTH, TM), (V_WIDTH, n)), ((TM, 2 * FN_WIDTH), (n, 2 * FN_WIDTH))]
    if seq is None:
        widths = [None] * 4 + [D_MODEL, D_MODEL]
    else:
        widths = [V_WIDTH, FN_WIDTH, D_MODEL, D_MODEL]
    out_shape, out_specs = [], []
    for j, w in enumerate(widths):
        if w is None:
            out_shape.append(jax.ShapeDtypeStruct(qkvf[j][1], BF16))
            out_specs.append(pl.BlockSpec(qkvf[j][0], (lambda i: (0, i)) if j == 2 else (lambda i: (i, 0))))
        else:
            out_shape.append(jax.ShapeDtypeStruct((n, w), BF16))
            out_specs.append(pl.BlockSpec((TM, w), lambda i: (i, 0)))
    if emit_cache:
        assert seq is not None
        out_shape += [jax.ShapeDtypeStruct((n, KV_LORA), F32),
                      jax.ShapeDtypeStruct((n // seq[0], QK_ROPE, seq[0]), F32)]
        out_specs += [pl.BlockSpec((TM, KV_LORA), lambda i: (i, 0)),
                      pl.BlockSpec((TM // seq[0], QK_ROPE, seq[0]), lambda i: (i, 0, 0))]
    out_shape += [jax.ShapeDtypeStruct(w.shape, BF16) for w in cast]
    out_specs += cast_specs
    scratch = [] if seq is None else [pltpu.VMEM(blk, BF16) for blk, _ in qkvf]
    return pl.pallas_call(
        functools.partial(_inproj_kernel, rope=rope, emit_cache=emit_cache, n_cast=len(cast),
                          seq_len=None if seq is None else seq[0]),
        grid=(n // TM,),
        in_specs=in_specs,
        out_specs=out_specs,
        out_shape=out_shape,
        scratch_shapes=scratch,
        compiler_params=_cparams(("parallel",)),
        name="inproj_lat" if rope else "inproj_ctx",
    )(*args)


def _attend(q, ks, vts, fillers=()):
    head = lambda hd: slice(hd * HEAD_SLOT, (hd + 1) * HEAD_SLOT)
    st = [jnp.stack([_dot_nt(k[:, head(hd)], q[:, head(hd)]) for hd in range(N_HEADS)]) for k in ks]
    m = functools.reduce(jnp.maximum, [sj.max(axis=1, keepdims=True) for sj in st])
    fillers = list(fillers)
    every = N_HEADS // max(1, len(fillers))
    p = []
    for hd in range(N_HEADS):
        p.append([jnp.exp(sj[hd] - m[hd]) for sj in st])
        if fillers and hd % every == 0:
            fillers.pop(0)()
    assert not fillers
    outs = []
    for hd in range(N_HEADS):
        l = functools.reduce(lambda a, b: a + b, [pj.sum(axis=0, keepdims=True) for pj in p[hd]])
        o = functools.reduce(lambda a, b: a + b,
                             [_dot(vt[hd * V_HEAD:(hd + 1) * V_HEAD, :], pj.astype(BF16))
                              for vt, pj in zip(vts, p[hd])])
        outs.append(o / l)
    return jnp.concatenate(outs, axis=0).T.astype(BF16)


def _attn_kernel(q_ref, k_ref, vt_ref, cckv_ref, ckpe_ref, wkb_ref, wvb_ref, kg_ref, cs_ref, ns_ref,
                 f_ref, o_ref, fm_ref, kc_ref, vtc_ref):
    @pl.when(pl.program_id(1) == 0)
    def _():
        _emit_kv(cckv_ref[0].astype(BF16), ckpe_ref[0], wkb_ref, wvb_ref, kg_ref, None, kc_ref, vtc_ref)

    def dft(c0, width):
        def run():
            fm_ref[0, :, pl.ds(c0, width)] = (
                _dot(cs_ref[...], f_ref[0, :, pl.ds(c0, width)])
                + _dot(ns_ref[...], f_ref[0, :, pl.ds(FN_WIDTH + c0, width)])).astype(BF16)
        return run

    half = FN_WIDTH // 2
    o_ref[0] = _attend(q_ref.at[0], [kc_ref, k_ref.at[0]], [vtc_ref, vt_ref],
                       [dft(0, half), dft(half, half)])


def _attention(q3, k3, vt, cache_ckv, cache_kpe_slot, wts, fcs3, cs, ns, tq, name):
    b, sq, _ = q3.shape
    past = cache_ckv.shape[1]
    seq = lambda shape: pl.BlockSpec(shape, lambda bi, qi: (bi, 0, 0))
    const = lambda shape: pl.BlockSpec(shape, lambda bi, qi: (0,) * len(shape))
    return pl.pallas_call(
        _attn_kernel,
        grid=(b, sq // tq),
        in_specs=[pl.BlockSpec((1, tq, QK_WIDTH), lambda bi, qi: (bi, qi, 0)),
                  seq((1, sq, QK_WIDTH)),
                  pl.BlockSpec((V_WIDTH, sq), lambda bi, qi: (0, bi)),
                  seq((1, past, KV_LORA)), seq((1, past, LANES)),
                  const((KV_LORA, QK_WIDTH)), const((V_WIDTH, KV_LORA)), const((1, HEAD_SLOT)),
                  pl.BlockSpec((tq, sq), lambda bi, qi: (qi, 0)),
                  pl.BlockSpec((tq, sq), lambda bi, qi: (qi, 0)),
                  seq((1, sq, 2 * FN_WIDTH))],
        out_specs=[pl.BlockSpec((1, tq, V_WIDTH), lambda bi, qi: (bi, qi, 0)),
                   pl.BlockSpec((1, tq, FN_WIDTH), lambda bi, qi: (bi, qi, 0))],
        out_shape=[jax.ShapeDtypeStruct((b, sq, V_WIDTH), BF16),
                   jax.ShapeDtypeStruct((b, sq, FN_WIDTH), BF16)],
        scratch_shapes=[pltpu.VMEM((past, QK_WIDTH), BF16), pltpu.VMEM((V_WIDTH, past), BF16)],
        compiler_params=_cparams(("parallel", "arbitrary")),
        name=name,
    )(q3, k3, vt, cache_ckv, cache_kpe_slot, wts["w_kb"], wts["w_vb"], wts["kg"], cs, ns, fcs3)


def _router_logits(w, b, h_hi, h_lo):
    rows = w.shape[0]
    w_hi, w_lo = _split_bf16(w)
    y = _dot_nt(jnp.concatenate([w_hi, w_lo], axis=0), h_hi)
    return (y[:rows] + y[rows:]) + _dot_nt(w_hi, h_lo) + b


def _rows(x, n):
    return [x[j:j + 1, :] for j in range(n)]


def _first_argmax(rows, top):
    idx = jnp.full(top.shape, len(rows) - 1, jnp.int32)
    for j in range(len(rows) - 2, -1, -1):
        idx = jnp.where(rows[j] == top, j, idx)
    return idx


def _store_token_major(ref, x, tm):
    for s in range(TOKEN_ROWS):
        ref[pl.ds(s, tm, stride=TOKEN_ROWS), :] = x[:, s * LANES:(s + 1) * LANES]


def _load_token_major(ref, tm):
    return jnp.concatenate([ref[pl.ds(s, tm, stride=TOKEN_ROWS), :] for s in range(TOKEN_ROWS)],
                           axis=1)


def _merge_kernel(*refs, n_first):
    first, second = refs[0:5], refs[5:10]
    (mod_ref, wao_ref, wfn_ref, wout_ref, g2_ref, wrg_ref, brg_ref, x1_ref, h2_ref, grp_ref, rank_ref,
     cnt_ref, carry_ref) = refs[10:]
    i = pl.program_id(0)

    @pl.when(i == 0)
    def _():
        carry_ref[...] = jnp.zeros_like(carry_ref)

    def tile(src):
        x, attn, fm, sga, sgf = (r[...] for r in src)
        a = _dot(attn, wao_ref[...])
        f = _dot(fm, wfn_ref[...])
        u = sga.astype(F32) * a + sgf.astype(F32) * f
        y = _dot(u.astype(BF16), wout_ref[...])
        x1 = x + mod_ref[0, 2:3, :] * y
        x1_ref[...] = x1
        tm = x1.shape[0]
        h2 = (x1 * _rms(x1, D_MODEL) * g2_ref[...]) * (1.0 + mod_ref[0, 4:5, :]) + mod_ref[0, 3:4, :]
        _store_token_major(h2_ref, h2, tm)

        h2_hi, h2_lo = _split_bf16(h2)
        g = _rows(_router_logits(wrg_ref[...], brg_ref[...], h2_hi, h2_lo), N_GROUPS)
        gidx = _first_argmax(g, functools.reduce(jnp.maximum, g))

        onehot = jnp.where(lax.broadcasted_iota(jnp.int32, (SEG_ROWS, tm), 0) == gidx, 1.0, 0.0)
        before = (lax.broadcasted_iota(jnp.int32, (tm, tm), 0)
                  < lax.broadcasted_iota(jnp.int32, (tm, tm), 1))
        prefix = _dot(onehot.astype(BF16), jnp.where(before, 1.0, 0.0).astype(BF16))
        carry = carry_ref[...]
        rank = jnp.sum(onehot * (prefix + carry[:, 0:1]), axis=0, keepdims=True)
        grp_ref[...] = gidx
        rank_ref[...] = rank.astype(jnp.int32)
        carry = carry + jnp.sum(onehot, axis=1, keepdims=True)
        carry_ref[...] = carry
        cnt_ref[...] = carry.astype(jnp.int32)

    @pl.when(i < n_first)
    def _():
        tile(first)

    @pl.when(i >= n_first)
    def _():
        tile(second)


def _merge(first, second, mod3, mod_row_fn, wts):
    n_first = first[0].shape[0] // TM
    n = first[0].shape[0] + second[0].shape[0]
    tok = lambda w: pl.BlockSpec((TM, w), lambda i: (i, 0))
    widths = (D_MODEL, V_WIDTH, FN_WIDTH, D_MODEL, D_MODEL)
    in_first = [pl.BlockSpec((TM, w), lambda i: (jnp.minimum(i, n_first - 1), 0)) for w in widths]
    in_second = [pl.BlockSpec((TM, w), lambda i: (jnp.maximum(i - n_first, 0), 0)) for w in widths]
    return pl.pallas_call(
        functools.partial(_merge_kernel, n_first=n_first),
        grid=(n // TM,),
        in_specs=in_first + in_second + [
                  pl.BlockSpec((1, 6, D_MODEL), lambda i: (mod_row_fn(i), 0, 0)),
                  _const_spec((V_WIDTH, D_MODEL)),
                  _const_spec((FN_WIDTH, D_MODEL)),
                  _const_spec((D_MODEL, D_MODEL)),
                  _const_spec((1, D_MODEL)),
                  _const_spec((ROUTER_ROWS, D_MODEL)),
                  _const_spec((ROUTER_ROWS, 1))],
        out_specs=[tok(D_MODEL),
                   pl.BlockSpec((TM * TOKEN_ROWS, LANES), lambda i: (i, 0)),
                   pl.BlockSpec((1, TM), lambda i: (0, i)),
                   pl.BlockSpec((1, TM), lambda i: (0, i)), _const_spec((SEG_ROWS, LANES))],
        out_shape=[jax.ShapeDtypeStruct((n, D_MODEL), F32),
                   jax.ShapeDtypeStruct((n * TOKEN_ROWS, LANES), F32),
                   jax.ShapeDtypeStruct((1, n), jnp.int32),
                   jax.ShapeDtypeStruct((1, n), jnp.int32),
                   jax.ShapeDtypeStruct((SEG_ROWS, LANES), jnp.int32)],
        scratch_shapes=[pltpu.VMEM((SEG_ROWS, LANES), F32)],
        compiler_params=_cparams(("arbitrary",)),
        name="merge",
    )(*first, *second, mod3, wts["w_ao"], wts["w_fn"], wts["w_out"], wts["g2"],
      wts["w_rg"], wts["b_rg"])


def _token_rows(ref, t):
    start = t * TOKEN_ROWS
    if not isinstance(t, int):
        start = pl.multiple_of(start, TOKEN_ROWS)
    return ref.at[pl.ds(start, TOKEN_ROWS)]


def _row_copy(src, dst, src_tok, dst_tok, sem):
    return pltpu.make_async_copy(_token_rows(src, src_tok), _token_rows(dst, dst_tok), sem)


def _wait_tile(src_hbm, buf, sem):
    pltpu.make_async_copy(src_hbm.at[pl.ds(0, buf.shape[0])], buf, sem).wait()


def _combine_kernel(pos_ref, y_hbm, x1_ref, mod_ref, o1_ref, o2_ref, buf, sem, *, n_first):
    i = pl.program_id(0)
    tm = o1_ref.shape[0]

    def gather(tile_idx, slot):
        def start(r, c):
            _row_copy(y_hbm, buf.at[slot], pos_ref[tile_idx * tm + r], r, sem.at[slot]).start()
            return c

        lax.fori_loop(0, tm, start, 0, unroll=32)

    @pl.when(i == 0)
    def _():
        gather(0, 0)

    @pl.when(i + 1 < pl.num_programs(0))
    def _():
        gather(i + 1, (i + 1) % 2)

    slot = i % 2

    _wait_tile(y_hbm, buf.at[slot], sem.at[slot])
    out = x1_ref[...] + mod_ref[0, 5:6, :] * _load_token_major(buf.at[slot], tm)

    @pl.when(i < n_first)
    def _():
        o1_ref[...] = out

    @pl.when(i >= n_first)
    def _():
        o2_ref[...] = out


def _combine(y_sorted, pos, x1, mod3, mod_row_fn, n_first_tokens):
    n = x1.shape[0]
    n_first = n_first_tokens // TM
    return pl.pallas_call(
        functools.partial(_combine_kernel, n_first=n_first),
        grid_spec=pltpu.PrefetchScalarGridSpec(
            num_scalar_prefetch=1,
            grid=(n // TM,),
            in_specs=[pl.BlockSpec(memory_space=pl.ANY),
                      pl.BlockSpec((TM, D_MODEL), lambda i, pos: (i, 0)),
                      pl.BlockSpec((1, 6, D_MODEL), lambda i, pos: (mod_row_fn(i), 0, 0))],
            out_specs=[pl.BlockSpec((TM, D_MODEL), lambda i, pos: (jnp.minimum(i, n_first - 1), 0)),
                       pl.BlockSpec((TM, D_MODEL), lambda i, pos: (jnp.maximum(i - n_first, 0), 0))],
            scratch_shapes=[pltpu.VMEM((2, TM * TOKEN_ROWS, LANES), F32),
                            pltpu.SemaphoreType.DMA((2,))]),
        out_shape=[jax.ShapeDtypeStruct((n_first_tokens, D_MODEL), F32),
                   jax.ShapeDtypeStruct((n - n_first_tokens, D_MODEL), F32)],
        compiler_params=_cparams(("arbitrary",)),
        name="moe_combine",
    )(pos, y_sorted, x1, mod3)


def _moe_kernel(grp_ref, on_ref, src_ref, h_hbm, wrg_ref, brg_ref, wre_ref, bre_ref, wg_ref, wu_ref,
                wd_ref, o_ref, hbuf, sem, *, tile):
    i = pl.program_id(0)
    n_tiles = pl.num_programs(0)

    def start(tile_idx, r):
        slot = tile_idx % (GATHER_AHEAD + 1)
        _row_copy(h_hbm, hbuf.at[slot], src_ref[tile_idx * tile + r], r, sem.at[slot]).start()

    for first in range(GATHER_AHEAD):
        @pl.when(jnp.logical_and(i == 0, on_ref[first] == 1))
        def _():
            def body(r, c):
                start(first, r)
                return c

            lax.fori_loop(0, tile, body, 0, unroll=32)

    def run(fetch_ahead):
        def issue(chunk, n_chunks):
            if fetch_ahead:
                per = tile // n_chunks
                for r in range(chunk * per, (chunk + 1) * per):
                    start(i + GATHER_AHEAD, r)

        slot = i % (GATHER_AHEAD + 1)
        _wait_tile(h_hbm, hbuf.at[slot], sem.at[slot])
        h2_hi, h2_lo = _split_bf16(_load_token_major(hbuf.at[slot], tile))

        logits = _router_logits(jnp.concatenate([wrg_ref[...], wre_ref[0]], axis=0),
                                jnp.concatenate([brg_ref[...], bre_ref[0]], axis=0), h2_hi, h2_lo)
        g = _rows(logits, N_GROUPS)
        gmax = functools.reduce(jnp.maximum, g)
        p_top = 1.0 / functools.reduce(lambda p, q: p + q, [jnp.exp(gj - gmax) for gj in g])
        e = _rows(logits[ROUTER_ROWS:], EXPERTS_PER_GROUP)
        m1 = functools.reduce(jnp.maximum, e)
        i1 = _first_argmax(e, m1)
        rest = [jnp.where(i1 == j, -jnp.inf, e[j]) for j in range(EXPERTS_PER_GROUP)]
        m2 = functools.reduce(jnp.maximum, rest)
        i2 = _first_argmax(rest, m2)
        t = jnp.exp(m2 - m1)
        w1 = p_top / (1.0 + t)
        w2 = p_top * t / (1.0 + t)
        row = lax.broadcasted_iota(jnp.int32, (LANES, tile), 0)
        comb = (jnp.where(row == i1, w1, 0.0) + jnp.where(row == i2, w2, 0.0)).T

        gates, ups = [], []
        for j in range(EXPERTS_PER_GROUP):
            gates.append(_dot(h2_hi, wg_ref[j]))
            issue(2 * j, 2 * EXPERTS_PER_GROUP)
            ups.append(_dot(h2_hi, wu_ref[j]))
            issue(2 * j + 1, 2 * EXPERTS_PER_GROUP)
        acts = [((a * _sigmoid(a)) * u * comb[:, j:j + 1]).astype(BF16)
                for j, (a, u) in enumerate(zip(gates, ups))]
        acc = functools.reduce(lambda p, q: p + q,
                               [_dot(acts[j], wd_ref[j]) for j in range(EXPERTS_PER_GROUP)])
        _store_token_major(o_ref, acc, tile)

    @pl.when(on_ref[i] == 0)
    def _():
        o_ref[...] = jnp.zeros_like(o_ref)

    ahead = jnp.minimum(i + GATHER_AHEAD, n_tiles - 1)
    fetch = jnp.logical_and(i + GATHER_AHEAD < n_tiles, on_ref[ahead] == 1)

    @pl.when(jnp.logical_and(on_ref[i] == 1, fetch))
    def _():
        run(True)

    @pl.when(jnp.logical_and(on_ref[i] == 1, jnp.logical_not(fetch)))
    def _():
        run(False)


def _moe(h2tm, src, maps, wts, w_gate, w_up, w_down, tile):
    n_tiles = src.shape[0] // tile
    const = lambda shape: pl.BlockSpec(shape, lambda i, grp, on, src: (0,) * len(shape))
    by_group = lambda shape: pl.BlockSpec(
        shape, lambda i, grp, on, src: (grp[i],) + (0,) * (len(shape) - 1))
    return pl.pallas_call(
        functools.partial(_moe_kernel, tile=tile),
        grid_spec=pltpu.PrefetchScalarGridSpec(
            num_scalar_prefetch=3,
            grid=(n_tiles,),
            in_specs=[pl.BlockSpec(memory_space=pl.ANY),
                      const((ROUTER_ROWS, D_MODEL)), const((ROUTER_ROWS, 1)),
                      by_group((1, ROUTER_ROWS, D_MODEL)), by_group((1, ROUTER_ROWS, 1)),
                      by_group((EXPERTS_PER_GROUP, D_MODEL, D_EXPERT)),
                      by_group((EXPERTS_PER_GROUP, D_MODEL, D_EXPERT)),
                      by_group((EXPERTS_PER_GROUP, D_EXPERT, D_MODEL))],
            out_specs=pl.BlockSpec((tile * TOKEN_ROWS, LANES), lambda i, grp, on, src: (i, 0)),
            scratch_shapes=[pltpu.VMEM((GATHER_AHEAD + 1, tile * TOKEN_ROWS, LANES), F32),
                            pltpu.SemaphoreType.DMA((GATHER_AHEAD + 1,))]),
        out_shape=jax.ShapeDtypeStruct((n_tiles * tile * TOKEN_ROWS, LANES), F32),
        compiler_params=_cparams(("arbitrary",)),
        name="moe",
    )(*maps, src, h2tm, wts["w_rg"], wts["b_rg"], wts["w_re"], wts["b_re"], w_gate, w_up, w_down)


def _invert_kernel(pos_ref, lo_ref, hi_ref, src_ref, *, n, n_fill):
    def zero(p, c):
        src_ref[p] = 0
        return c

    for s in range(n_fill):
        lax.fori_loop(lo_ref[s], hi_ref[s], zero, 0)

    def put(t, c):
        src_ref[pos_ref[t]] = t
        return c

    lax.fori_loop(0, n, put, 0, unroll=8)


def _invert(pos, fill_lo, fill_hi, n_sorted):
    return pl.pallas_call(
        functools.partial(_invert_kernel, n=pos.shape[0], n_fill=fill_lo.shape[0]),
        grid_spec=pltpu.PrefetchScalarGridSpec(
            num_scalar_prefetch=3, grid=(1,), in_specs=[],
            out_specs=pl.BlockSpec(memory_space=pltpu.SMEM)),
        out_shape=jax.ShapeDtypeStruct((n_sorted,), jnp.int32),
        name="moe_invert",
    )(pos, fill_lo, fill_hi)


def _sort_plan(grp, rank, cnt, tile, n_tiles):
    n_tile_grp = (cnt + tile - 1) // tile
    tile_end = jnp.cumsum(n_tile_grp)
    tile_start = tile_end - n_tile_grp
    total = tile_end[-1]
    pos = ((tile_start * tile)[grp] + rank).astype(jnp.int32)
    fill_lo = jnp.concatenate([tile_start * tile + cnt, total[None] * tile]).astype(jnp.int32)
    fill_hi = jnp.concatenate([tile_end * tile, jnp.full((1,), n_tiles * tile)]).astype(jnp.int32)
    src = _invert(pos, fill_lo, fill_hi, n_tiles * tile)
    i = jnp.arange(n_tiles, dtype=jnp.int32)
    tile_grp = jnp.sum((jnp.minimum(i, total - 1)[:, None] >= tile_end[None, :]).astype(jnp.int32),
                       axis=1)
    return pos, src, (tile_grp.astype(jnp.int32), (i < total).astype(jnp.int32))


def _sparse_moe(x1, h2tm, grp, rank, cnt8, mod3, mod_row_fn, wts, w_gate, w_up, w_down, tile,
                n_first_tokens):
    n = x1.shape[0]
    n_tiles = n // tile + N_GROUPS
    pos, src, maps = _sort_plan(grp.reshape(n), rank.reshape(n), cnt8[:N_GROUPS, 0], tile, n_tiles)
    y_sorted = _moe(h2tm, src, maps, wts, w_gate, w_up, w_down, tile)
    return _combine(y_sorted, pos, x1, mod3, mod_row_fn, n_first_tokens)


def _pack_weights(l, norm1_g, q_a_norm_g, w_q_b, kv_a_norm_g, w_kv_b, q_norm_g, k_norm_g,
                  w_attn_o, w_fnet, w_out, norm2_g, w_router_group, b_router_group,
                  w_router_expert, b_router_expert):
    w_qb = jnp.pad(w_q_b[l].reshape(Q_LORA, N_HEADS, QK_HEAD),
                   ((0, 0), (0, 0), (0, HEAD_SLOT - QK_HEAD))).reshape(Q_LORA, QK_WIDTH)
    wkv = w_kv_b[l].reshape(KV_LORA, N_HEADS, QK_NOPE + V_HEAD)
    w_kb = jnp.pad(wkv[:, :, :QK_NOPE],
                   ((0, 0), (0, 0), (0, HEAD_SLOT - QK_NOPE))).reshape(KV_LORA, QK_WIDTH)
    w_vb = wkv[:, :, QK_NOPE:].reshape(KV_LORA, V_WIDTH)
    pad_g = lambda g: jnp.pad(g, (0, HEAD_SLOT - QK_HEAD)).reshape(1, HEAD_SLOT)
    w_rg = jnp.pad(w_router_group[l].T, ((0, ROUTER_ROWS - N_GROUPS), (0, 0)))
    b_rg = jnp.pad(b_router_group[l], (0, ROUTER_ROWS - N_GROUPS)).reshape(ROUTER_ROWS, 1)
    w_re = jnp.pad(w_router_expert[l].T.reshape(N_GROUPS, EXPERTS_PER_GROUP, D_MODEL),
                   ((0, 0), (0, ROUTER_ROWS - EXPERTS_PER_GROUP), (0, 0)))
    b_re = jnp.pad(b_router_expert[l].reshape(N_GROUPS, EXPERTS_PER_GROUP),
                   ((0, 0), (0, ROUTER_ROWS - EXPERTS_PER_GROUP))).reshape(N_GROUPS, ROUTER_ROWS, 1)
    dft_c, dft_ns = _dft_tables(FN_GROUP_W)
    return {
        "g1": norm1_g[l].reshape(1, D_MODEL),
        "qag": q_a_norm_g[l].reshape(1, Q_LORA),
        "w_qb": w_qb.astype(BF16),
        "kvg": kv_a_norm_g[l].reshape(1, KV_LORA),
        "w_kb": w_kb.astype(BF16),
        "w_vb": w_vb.T.astype(BF16),
        "qg": pad_g(q_norm_g[l]),
        "kg": pad_g(k_norm_g[l]),
        "dft_c": jnp.concatenate([jnp.asarray(dft_c), -jnp.asarray(dft_ns)], axis=1).astype(BF16),
        "w_ao": w_attn_o[l].astype(BF16),
        "w_fn": w_fnet[l].astype(BF16),
        "w_out": w_out[l].astype(BF16),
        "g2": norm2_g[l].reshape(1, D_MODEL),
        "w_rg": w_rg,
        "b_rg": b_rg,
        "w_re": w_re,
        "b_re": b_re,
    }


def _layer(xp, xs, cache_ckv_l, cache_kpe_l, mod3, wts, experts):
    bp, sp, _ = xp.shape
    bs, ss, _ = xs.shape
    ctx_row = lambda i: 0
    lat_row = lambda i: 1 + i // (ss // TM)

    xp2 = xp.reshape(bp * sp, D_MODEL)
    ride = (bp * sp) // TM == N_EXPERTS
    cs, ns = (jnp.asarray(t).astype(BF16) for t in _dft_tables(sp))
    outs = _inproj(xp2, mod3, ctx_row, wts, None, True, experts if ride else (), (sp, cs, ns))
    attn, fm, sga, sgf, ckv, kpe = outs[:6]
    w_gate, w_up, w_down = outs[6:] if ride else (w.astype(BF16) for w in experts)
    ctx_set = (xp2, attn, fm, sga, sgf)

    xs2 = xs.reshape(bs * ss, D_MODEL)
    rope_tabs = tuple(jnp.asarray(t) for t in _rope_tables(ss))
    q, k, v, fcs, sga, sgf = _inproj(xs2, mod3, lat_row, wts, rope_tabs, False)
    kpe_slot = jnp.pad(cache_kpe_l, ((0, 0), (0, 0), (QK_NOPE, LANES - QK_HEAD)))
    cs, ns = (jnp.asarray(t).astype(BF16) for t in _dft_tables(ss))
    attn, fm = _attention(q.reshape(bs, ss, QK_WIDTH), k.reshape(bs, ss, QK_WIDTH), v, cache_ckv_l,
                          kpe_slot, wts, fcs.reshape(bs, ss, 2 * FN_WIDTH), cs, ns, TQ_LAT, "attn_lat")
    lat_set = (xs2, attn.reshape(bs * ss, V_WIDTH), fm.reshape(bs * ss, FN_WIDTH), sga, sgf)

    n_ctx_tiles = (bp * sp) // TM
    all_row = lambda i: jnp.where(i < n_ctx_tiles, 0, lat_row(i - n_ctx_tiles))
    x1, h2tm, grp, rank, cnt = _merge(ctx_set, lat_set, mod3, all_row, wts)
    yp, ys = _sparse_moe(x1, h2tm, grp, rank, cnt, mod3, all_row, wts, w_gate, w_up, w_down, MOE_TILE,
                         bp * sp)
    yp, ys = yp.reshape(bp, sp, D_MODEL), ys.reshape(bs, ss, D_MODEL)

    return yp, ys, ckv.reshape(bp, sp, KV_LORA), jnp.swapaxes(kpe, 1, 2)


def kernel(x_prompt, x_sample, cache_ckv, cache_kpe, c, c_ctx, w_mod, b_mod, norm1_g, w_in, q_a_norm_g, w_q_b, kv_a_norm_g, w_kv_b, q_norm_g, k_norm_g, w_attn_o, w_fnet, w_out, norm2_g, w_router_group, b_router_group, w_router_expert, b_router_expert, w_exp_gate, w_exp_up, w_exp_down):
    depth = w_mod.shape[0]
    n_lat = c.shape[0]
    assert 1 + n_lat <= MOD_ROWS
    cond8 = jnp.concatenate([c_ctx[None, :], c, jnp.zeros((MOD_ROWS - 1 - n_lat, D_MODEL), F32)], axis=0)
    xp, xs = x_prompt, x_sample
    ckv_layers, kpe_layers = [], []
    for l in range(depth):
        mod, w_in_p = _adaln(cond8, w_mod[l], b_mod[l].reshape(1, -1), jnp.swapaxes(w_in[l], 0, 1))
        mod3 = mod.reshape(MOD_ROWS, 6, D_MODEL)
        wts = _pack_weights(l, norm1_g, q_a_norm_g, w_q_b, kv_a_norm_g, w_kv_b, q_norm_g,
                            k_norm_g, w_attn_o, w_fnet, w_out, norm2_g, w_router_group,
                            b_router_group, w_router_expert, b_router_expert)
        wts["w_in"] = w_in_p
        xp, xs, ckv, kpe = _layer(xp, xs, cache_ckv[:, l], cache_kpe[:, l], mod3, wts,
                                  (w_exp_gate[l], w_exp_up[l], w_exp_down[l]))
        ckv_layers.append(ckv)
        kpe_layers.append(kpe)
    return xp, xs, jnp.stack(ckv_layers, axis=1), jnp.stack(kpe_layers, axis=1)
```

```python
import functools
import math

import numpy as np
import jax
import jax.numpy as jnp
from jax import lax
from jax.experimental import pallas as pl
from jax.experimental.pallas import tpu as pltpu

D_MODEL = 1024
GRID_W = 64
N_HEADS = 8
Q_LORA = 512
KV_LORA = 256
QK_NOPE = 64
QK_ROPE = 32
V_HEAD = 64
QK_HEAD = QK_NOPE + QK_ROPE
ATTN_SCALE = QK_HEAD ** -0.5
ROPE_BASE = 10000.0
FN_GROUPS = 4
FN_GROUP_W = 128
FN_WIDTH = FN_GROUPS * FN_GROUP_W
N_GROUPS = 4
EXPERTS_PER_GROUP = 4
N_EXPERTS = N_GROUPS * EXPERTS_PER_GROUP
D_EXPERT = 512
EPS = 1e-6

LANES = 128
HEAD_SLOT = LANES
QK_WIDTH = N_HEADS * HEAD_SLOT
V_WIDTH = N_HEADS * V_HEAD
C_QA = 0
C_KVA = C_QA + Q_LORA
C_KPE = C_KVA + KV_LORA
C_FN = C_KPE + LANES
C_GA = C_FN + FN_WIDTH
C_GF = C_GA + D_MODEL
IN_PACKED = C_GF + D_MODEL
SUBLANES = 8
ROUTER_ROWS = SUBLANES
MOD_ROWS = SUBLANES
VMEM_LIMIT = 56 * 1024 * 1024

TM = 512
TQ_LAT = 256
SEG_ROWS = SUBLANES
TOKEN_ROWS = D_MODEL // LANES
MOE_TILE = 256
MERGE_SPLIT = 2
GATHER_AHEAD = 2
ADALN_STEPS = 8

BF16 = jnp.bfloat16
F32 = jnp.float32


def _cparams(sem):
    return pltpu.CompilerParams(dimension_semantics=sem, vmem_limit_bytes=VMEM_LIMIT)


def _dot(a, b):
    return jnp.dot(a, b, preferred_element_type=F32)


def _dot_nt(a, b):
    return lax.dot_general(a, b, (((1,), (1,)), ((), ())), preferred_element_type=F32)


def _sigmoid(x):
    return 1.0 / (1.0 + jnp.exp(-x))


def _split_bf16(x):
    hi = x.astype(BF16)
    return hi, (x - hi.astype(F32)).astype(BF16)


@functools.lru_cache(maxsize=None)
def _rope_tables(n_pos):
    half = QK_ROPE // 2
    quarter = half // 2
    freqs = ROPE_BASE ** (-np.arange(quarter, dtype=np.float64) / quarter)
    pos = np.arange(n_pos)
    row = (pos // GRID_W).astype(np.float64)
    col = (pos % GRID_W).astype(np.float64)
    cos_t = np.ones((n_pos, LANES), np.float64)
    sin_a = np.zeros((n_pos, LANES), np.float64)
    sin_b = np.zeros((n_pos, LANES), np.float64)
    for base, p in ((QK_NOPE, row), (QK_NOPE + half, col)):
        ang = p[:, None] * freqs[None, :]
        cos_t[:, base:base + quarter] = np.cos(ang)
        cos_t[:, base + quarter:base + half] = np.cos(ang)
        sin_a[:, base:base + quarter] = -np.sin(ang)
        sin_b[:, base + quarter:base + half] = np.sin(ang)
    return (cos_t.astype(np.float32), sin_a.astype(np.float32), sin_b.astype(np.float32))


@functools.lru_cache(maxsize=None)
def _dft_tables(n):
    k = np.arange(n)
    ang = 2.0 * np.pi * ((k[:, None] * k[None, :]) % n) / n
    s = 1.0 / math.sqrt(n)
    return (np.cos(ang) * s).astype(np.float32), (-np.sin(ang) * s).astype(np.float32)


def _adaln_kernel(cond_ref, w_ref, b_ref, win_ref, o_ref, winp_ref):
    c = cond_ref[...]
    s_hi, s_lo = _split_bf16(c * _sigmoid(c))
    w_hi, w_lo = _split_bf16(w_ref[...])
    y = _dot(jnp.concatenate([s_hi, s_lo], axis=0), w_hi)
    o_ref[...] = (y[:MOD_ROWS] + y[MOD_ROWS:]) + _dot(s_hi, w_lo) + b_ref[...]

    winp_ref[:C_KPE, :] = win_ref[:C_KPE, :].astype(BF16)
    winp_ref[C_KPE:C_FN, :] = jnp.zeros((LANES, winp_ref.shape[1]), BF16)
    winp_ref[C_KPE + QK_NOPE:C_KPE + QK_HEAD, :] = win_ref[C_KPE:C_KPE + QK_ROPE, :].astype(BF16)
    winp_ref[C_FN:, :] = win_ref[C_KPE + QK_ROPE:, :].astype(BF16)


def _adaln(cond8, w_mod, b_mod, w_in_t):
    n = w_mod.shape[1]
    steps = ADALN_STEPS
    tn, tr = n // steps, D_MODEL // steps
    return pl.pallas_call(
        _adaln_kernel,
        grid=(steps,),
        in_specs=[pl.BlockSpec((MOD_ROWS, D_MODEL), lambda j: (0, 0)),
                  pl.BlockSpec((D_MODEL, tn), lambda j: (0, j)),
                  pl.BlockSpec((1, tn), lambda j: (0, j)),
                  pl.BlockSpec((w_in_t.shape[0], tr), lambda j: (0, j))],
        out_specs=[pl.BlockSpec((MOD_ROWS, tn), lambda j: (0, j)),
                   pl.BlockSpec((IN_PACKED, tr), lambda j: (0, j))],
        out_shape=[jax.ShapeDtypeStruct((MOD_ROWS, n), F32),
                   jax.ShapeDtypeStruct((IN_PACKED, D_MODEL), BF16)],
        compiler_params=_cparams(("arbitrary",)),
        name="adaln",
    )(cond8, w_mod, b_mod, w_in_t)


def _rms(x, width):
    return lax.rsqrt(jnp.sum(x * x, axis=-1, keepdims=True) * (1.0 / width) + EPS)


def _rope(x, cos_t, sin_a, sin_b):
    return x * cos_t + pltpu.roll(x, LANES - 8, 1) * sin_a + pltpu.roll(x, 8, 1) * sin_b


def _inproj_kernel(*refs, rope, emit_cache, n_cast, seq_len):
    it = iter(refs)
    x_ref, mod_ref, g1_ref, win_ref, qag_ref, wqb_ref, kvg_ref, wkb_ref, wvb_ref = (
        next(it) for _ in range(9))
    qg_ref, kg_ref, dft_ref = next(it), next(it), next(it)
    if rope:
        cos_ref, sa_ref, sb_ref = next(it), next(it), next(it)
    if seq_len:
        seq_cs_ref, seq_ns_ref = next(it), next(it)
    cast_in = [next(it) for _ in range(n_cast)]
    if seq_len:
        attn_ref, fm_ref, sga_ref, sgf_ref = (next(it) for _ in range(4))
    else:
        q_ref, k_ref, v_ref, fcs_ref, sga_ref, sgf_ref = (next(it) for _ in range(6))
    if emit_cache:
        ckv_ref, kpe_ref = next(it), next(it)
    cast_out = [next(it) for _ in range(n_cast)]
    if seq_len:
        q_ref, k_ref, v_ref, fcs_ref = (next(it) for _ in range(4))

    for src, dst in zip(cast_in, cast_out):
        dst[...] = src[...].astype(BF16)

    x = x_ref[...]
    shift = mod_ref[0, 0:1, :]
    scale = mod_ref[0, 1:2, :]
    h = (x * _rms(x, D_MODEL) * g1_ref[...]) * (1.0 + scale) + shift
    hb = h.astype(BF16)

    if rope:
        cos_t, sin_a, sin_b = cos_ref[...], sa_ref[...], sb_ref[...]

    qa = _dot_nt(hb, win_ref[C_QA:C_QA + Q_LORA, :])
    qn = (qa * _rms(qa, Q_LORA) * qag_ref[...]).astype(BF16)
    q = _dot(qn, wqb_ref[...])
    qg = qg_ref[...] * ATTN_SCALE
    for hd in range(N_HEADS):
        qh = q[:, hd * HEAD_SLOT:(hd + 1) * HEAD_SLOT]
        qh = qh * _rms(qh, QK_HEAD) * qg
        if rope:
            qh = _rope(qh, cos_t, sin_a, sin_b)
        q_ref[:, hd * HEAD_SLOT:(hd + 1) * HEAD_SLOT] = qh.astype(BF16)

    kva = _dot_nt(hb, win_ref[C_KVA:C_KVA + KV_LORA, :])
    ckv = kva * _rms(kva, KV_LORA) * kvg_ref[...]
    kpe = _dot_nt(hb, win_ref[C_KPE:C_KPE + LANES, :])
    if emit_cache:
        ckv_ref[...] = ckv
        kpe_t = kpe.T[QK_NOPE:QK_NOPE + QK_ROPE, :]
        for j in range(x.shape[0] // seq_len):
            kpe_ref[j] = kpe_t[:, j * seq_len:(j + 1) * seq_len]
    _emit_kv(ckv.astype(BF16), kpe, wkb_ref, wvb_ref, kg_ref,
             (cos_t, sin_a, sin_b) if rope else None, k_ref, v_ref)

    fn = _dot_nt(hb, win_ref[C_FN:C_FN + FN_WIDTH, :]).astype(BF16)
    for g in range(FN_GROUPS):
        cs = _dot(fn[:, g * FN_GROUP_W:(g + 1) * FN_GROUP_W], dft_ref[...])
        fcs_ref[:, g * FN_GROUP_W:(g + 1) * FN_GROUP_W] = cs[:, :FN_GROUP_W].astype(BF16)
        fcs_ref[:, FN_WIDTH + g * FN_GROUP_W:FN_WIDTH + (g + 1) * FN_GROUP_W] = (
            cs[:, FN_GROUP_W:].astype(BF16))

    def gate(ref, c0, half):
        def run():
            w = D_MODEL // 2
            ref[:, pl.ds(half * w, w)] = _sigmoid(
                _dot_nt(hb, win_ref[pl.ds(c0 + half * w, w), :])).astype(BF16)
        return run

    gates = [gate(ref, c0, half) for ref, c0 in ((sga_ref, C_GA), (sgf_ref, C_GF)) for half in range(2)]

    if seq_len:
        n_seq = x.shape[0] // seq_len
        for j in range(n_seq):
            rows = pl.ds(j * seq_len, seq_len)
            fill = gates[j * len(gates) // n_seq:(j + 1) * len(gates) // n_seq]
            attn_ref[rows, :] = _attend(q_ref.at[rows], [k_ref.at[rows]], [v_ref.at[:, rows]], fill)
            fm_ref[rows, :] = (_dot(seq_cs_ref[...], fcs_ref[rows, :FN_WIDTH])
                               + _dot(seq_ns_ref[...], fcs_ref[rows, FN_WIDTH:])).astype(BF16)
    else:
        for run_gate in gates:
            run_gate()


def _emit_kv(ckvb, kpe, wkb_ref, wvb_ref, kg_ref, rope_tabs, k_ref, v_ref):
    kg = kg_ref[...]
    v_ref[...] = _dot_nt(wvb_ref[...], ckvb).astype(BF16)
    kn = _dot(ckvb, wkb_ref[...])
    pe_ss = jnp.sum(kpe * kpe, axis=-1, keepdims=True)
    pe_g = kpe * kg
    if rope_tabs is not None:
        pe_g = _rope(pe_g, *rope_tabs)
    for hd in range(N_HEADS):
        knh = kn[:, hd * HEAD_SLOT:(hd + 1) * HEAD_SLOT]
        ss = jnp.sum(knh * knh, axis=-1, keepdims=True) + pe_ss
        r = lax.rsqrt(ss * (1.0 / QK_HEAD) + EPS)
        k_ref[:, hd * HEAD_SLOT:(hd + 1) * HEAD_SLOT] = ((knh * kg + pe_g) * r).astype(BF16)


def _const_spec(shape):
    return pl.BlockSpec(shape, lambda i: (0,) * len(shape))


def _inproj(x2d, mod3, mod_row_fn, wts, rope_tabs, emit_cache, cast=(), seq=None):
    n = x2d.shape[0]
    assert all(w.shape[0] == n // TM for w in cast)
    assert seq is None or TM % seq[0] == 0
    rope = rope_tabs is not None
    tiles_per_seq = None if not rope else rope_tabs[0].shape[0] // TM
    in_specs = [pl.BlockSpec((TM, D_MODEL), lambda i: (i, 0)),
                pl.BlockSpec((1, 6, D_MODEL), lambda i: (mod_row_fn(i), 0, 0)),
                _const_spec((1, D_MODEL)),
                _const_spec((IN_PACKED, D_MODEL)),
                _const_spec((1, Q_LORA)),
                _const_spec((Q_LORA, QK_WIDTH)),
                _const_spec((1, KV_LORA)),
                _const_spec((KV_LORA, QK_WIDTH)),
                _const_spec((V_WIDTH, KV_LORA)),
                _const_spec((1, HEAD_SLOT)),
                _const_spec((1, HEAD_SLOT)),
                _const_spec((FN_GROUP_W, 2 * FN_GROUP_W))]
    args = [x2d, mod3, wts["g1"], wts["w_in"], wts["qag"], wts["w_qb"], wts["kvg"],
            wts["w_kb"], wts["w_vb"], wts["qg"], wts["kg"], wts["dft_c"]]
    if rope:
        in_specs += [pl.BlockSpec((TM, LANES), lambda i: (i % tiles_per_seq, 0))] * 3
        args += list(rope_tabs)
    if seq is not None:
        in_specs += [_const_spec((seq[0], seq[0]))] * 2
        args += [seq[1], seq[2]]
    cast_specs = [pl.BlockSpec((1,) + w.shape[1:], lambda i: (i, 0, 0)) for w in cast]
    in_specs += cast_specs
    args += list(cast)
    qkvf = [((TM, QK_WIDTH), (n, QK_WIDTH)), ((TM, QK_WIDTH), (n, QK_WIDTH)),
            ((V_WIDTH, TM), (V_WIDTH, n)), ((TM, 2 * FN_WIDTH), (n, 2 * FN_WIDTH))]
    if seq is None:
        widths = [None] * 4 + [D_MODEL, D_MODEL]
    else:
        widths = [V_WIDTH, FN_WIDTH, D_MODEL, D_MODEL]
    out_shape, out_specs = [], []
    for j, w in enumerate(widths):
        if w is None:
            out_shape.append(jax.ShapeDtypeStruct(qkvf[j][1], BF16))
            out_specs.append(pl.BlockSpec(qkvf[j][0], (lambda i: (0, i)) if j == 2 else (lambda i: (i, 0))))
        else:
            out_shape.append(jax.ShapeDtypeStruct((n, w), BF16))
            out_specs.append(pl.BlockSpec((TM, w), lambda i: (i, 0)))
    if emit_cache:
        assert seq is not None
        out_shape += [jax.ShapeDtypeStruct((n, KV_LORA), F32),
                      jax.ShapeDtypeStruct((n // seq[0], QK_ROPE, seq[0]), F32)]
        out_specs += [pl.BlockSpec((TM, KV_LORA), lambda i: (i, 0)),
                      pl.BlockSpec((TM // seq[0], QK_ROPE, seq[0]), lambda i: (i, 0, 0))]
    out_shape += [jax.ShapeDtypeStruct(w.shape, BF16) for w in cast]
    out_specs += cast_specs
    scratch = [] if seq is None else [pltpu.VMEM(blk, BF16) for blk, _ in qkvf]
    return pl.pallas_call(
        functools.partial(_inproj_kernel, rope=rope, emit_cache=emit_cache, n_cast=len(cast),
                          seq_len=None if seq is None else seq[0]),
        grid=(n // TM,),
        in_specs=in_specs,
        out_specs=out_specs,
        out_shape=out_shape,
        scratch_shapes=scratch,
        compiler_params=_cparams(("parallel",)),
        name="inproj_lat" if rope else "inproj_ctx",
    )(*args)


def _attend(q, ks, vts, fillers=()):
    head = lambda hd: slice(hd * HEAD_SLOT, (hd + 1) * HEAD_SLOT)
    st = [jnp.stack([_dot_nt(k[:, head(hd)], q[:, head(hd)]) for hd in range(N_HEADS)]) for k in ks]
    m = functools.reduce(jnp.maximum, [sj.max(axis=1, keepdims=True) for sj in st])
    fillers = list(fillers)
    every = N_HEADS // max(1, len(fillers))
    p = []
    for hd in range(N_HEADS):
        p.append([jnp.exp(sj[hd] - m[hd]) for sj in st])
        if fillers and hd % every == 0:
            fillers.pop(0)()
    assert not fillers
    outs = []
    for hd in range(N_HEADS):
        l = functools.reduce(lambda a, b: a + b, [pj.sum(axis=0, keepdims=True) for pj in p[hd]])
        o = functools.reduce(lambda a, b: a + b,
                             [_dot(vt[hd * V_HEAD:(hd + 1) * V_HEAD, :], pj.astype(BF16))
                              for vt, pj in zip(vts, p[hd])])
        outs.append(o / l)
    return jnp.concatenate(outs, axis=0).T.astype(BF16)


def _attn_kernel(q_ref, k_ref, vt_ref, cckv_ref, ckpe_ref, wkb_ref, wvb_ref, kg_ref, cs_ref, ns_ref,
                 f_ref, o_ref, fm_ref, kc_ref, vtc_ref):
    @pl.when(pl.program_id(1) == 0)
    def _():
        _emit_kv(cckv_ref[0].astype(BF16), ckpe_ref[0], wkb_ref, wvb_ref, kg_ref, None, kc_ref, vtc_ref)

    def dft(c0, width):
        def run():
            fm_ref[0, :, pl.ds(c0, width)] = (
                _dot(cs_ref[...], f_ref[0, :, pl.ds(c0, width)])
                + _dot(ns_ref[...], f_ref[0, :, pl.ds(FN_WIDTH + c0, width)])).astype(BF16)
        return run

    half = FN_WIDTH // 2
    o_ref[0] = _attend(q_ref.at[0], [kc_ref, k_ref.at[0]], [vtc_ref, vt_ref],
                       [dft(0, half), dft(half, half)])


def _attention(q3, k3, vt, cache_ckv, cache_kpe_slot, wts, fcs3, cs, ns, tq, name):
    b, sq, _ = q3.shape
    past = cache_ckv.shape[1]
    seq = lambda shape: pl.BlockSpec(shape, lambda bi, qi: (bi, 0, 0))
    const = lambda shape: pl.BlockSpec(shape, lambda bi, qi: (0,) * len(shape))
    return pl.pallas_call(
        _attn_kernel,
        grid=(b, sq // tq),
        in_specs=[pl.BlockSpec((1, tq, QK_WIDTH), lambda bi, qi: (bi, qi, 0)),
                  seq((1, sq, QK_WIDTH)),
                  pl.BlockSpec((V_WIDTH, sq), lambda bi, qi: (0, bi)),
                  seq((1, past, KV_LORA)), seq((1, past, LANES)),
                  const((KV_LORA, QK_WIDTH)), const((V_WIDTH, KV_LORA)), const((1, HEAD_SLOT)),
                  pl.BlockSpec((tq, sq), lambda bi, qi: (qi, 0)),
                  pl.BlockSpec((tq, sq), lambda bi, qi: (qi, 0)),
                  seq((1, sq, 2 * FN_WIDTH))],
        out_specs=[pl.BlockSpec((1, tq, V_WIDTH), lambda bi, qi: (bi, qi, 0)),
                   pl.BlockSpec((1, tq, FN_WIDTH), lambda bi, qi: (bi, qi, 0))],
        out_shape=[jax.ShapeDtypeStruct((b, sq, V_WIDTH), BF16),
                   jax.ShapeDtypeStruct((b, sq, FN_WIDTH), BF16)],
        scratch_shapes=[pltpu.VMEM((past, QK_WIDTH), BF16), pltpu.VMEM((V_WIDTH, past), BF16)],
        compiler_params=_cparams(("parallel", "arbitrary")),
        name=name,
    )(q3, k3, vt, cache_ckv, cache_kpe_slot, wts["w_kb"], wts["w_vb"], wts["kg"], cs, ns, fcs3)


def _router_logits(w, b, h_hi, h_lo):
    rows = w.shape[0]
    w_hi, w_lo = _split_bf16(w)
    y = _dot_nt(jnp.concatenate([w_hi, w_lo], axis=0), h_hi)
    return (y[:rows] + y[rows:]) + _dot_nt(w_hi, h_lo) + b


def _rows(x, n):
    return [x[j:j + 1, :] for j in range(n)]


def _first_argmax(rows, top):
    idx = jnp.full(top.shape, len(rows) - 1, jnp.int32)
    for j in range(len(rows) - 2, -1, -1):
        idx = jnp.where(rows[j] == top, j, idx)
    return idx


def _store_token_major(ref, x, tm, first=0):
    for s in range(TOKEN_ROWS):
        ref[pl.ds(first * TOKEN_ROWS + s, tm, stride=TOKEN_ROWS), :] = x[:, s * LANES:(s + 1) * LANES]


def _load_token_major(ref, tm):
    return jnp.concatenate([ref[pl.ds(s, tm, stride=TOKEN_ROWS), :] for s in range(TOKEN_ROWS)],
                           axis=1)


def _merge_kernel(*refs, n_first):
    first, second = refs[0:5], refs[5:10]
    (mod_ref, wao_ref, wfn_ref, wout_ref, g2_ref, wrg_ref, brg_ref, x1_ref, h2_ref, grp_ref, rank_ref,
     cnt_ref, carry_ref) = refs[10:]
    i = pl.program_id(0)

    @pl.when(i == 0)
    def _():
        carry_ref[...] = jnp.zeros_like(carry_ref)

    def tile(src):
        x_ref, attn_ref, fm_ref, sga_ref, sgf_ref = src
        hm = x_ref.shape[0] // MERGE_SPLIT
        rows = [pl.ds(k * hm, hm) for k in range(MERGE_SPLIT)]
        live = [dict() for _ in range(MERGE_SPLIT)]

        def branches(k):
            live[k]["a"] = _dot(attn_ref[rows[k], :], wao_ref[...])
            live[k]["f"] = _dot(fm_ref[rows[k], :], wfn_ref[...])

        def gated(k):
            u = (sga_ref[rows[k], :].astype(F32) * live[k].pop("a")
                 + sgf_ref[rows[k], :].astype(F32) * live[k].pop("f"))
            live[k]["u"] = u.astype(BF16)

        def project(k):
            live[k]["y"] = _dot(live[k].pop("u"), wout_ref[...])

        def residual(k):
            x1 = x_ref[rows[k], :] + mod_ref[0, 2:3, :] * live[k].pop("y")
            x1_ref[rows[k], :] = x1
            h2 = (x1 * _rms(x1, D_MODEL) * g2_ref[...]) * (1.0 + mod_ref[0, 4:5, :]) + mod_ref[0, 3:4, :]
            _store_token_major(h2_ref, h2, hm, k * hm)
            live[k]["h2"] = _split_bf16(h2)

        def route(k):
            g = _rows(_router_logits(wrg_ref[...], brg_ref[...], *live[k].pop("h2")), N_GROUPS)
            gidx = _first_argmax(g, functools.reduce(jnp.maximum, g))
            onehot = jnp.where(lax.broadcasted_iota(jnp.int32, (SEG_ROWS, hm), 0) == gidx, 1.0, 0.0)
            before = (lax.broadcasted_iota(jnp.int32, (hm, hm), 0)
                      < lax.broadcasted_iota(jnp.int32, (hm, hm), 1))
            prefix = _dot(onehot.astype(BF16), jnp.where(before, 1.0, 0.0).astype(BF16))
            carry = carry_ref[...]
            rank = jnp.sum(onehot * (prefix + carry[:, 0:1]), axis=0, keepdims=True)
            grp_ref[:, rows[k]] = gidx
            rank_ref[:, rows[k]] = rank.astype(jnp.int32)
            carry_ref[...] = carry + jnp.sum(onehot, axis=1, keepdims=True)

        phases = [branches, gated, project, residual, route]
        for t in range(len(phases) + MERGE_SPLIT - 1):
            for k in range(MERGE_SPLIT):
                if 0 <= t - k < len(phases):
                    phases[t - k](k)
        cnt_ref[...] = carry_ref[...].astype(jnp.int32)

    @pl.when(i < n_first)
    def _():
        tile(first)

    @pl.when(i >= n_first)
    def _():
        tile(second)


def _merge(first, second, mod3, mod_row_fn, wts):
    n_first = first[0].shape[0] // TM
    n = first[0].shape[0] + second[0].shape[0]
    tok = lambda w: pl.BlockSpec((TM, w), lambda i: (i, 0))
    widths = (D_MODEL, V_WIDTH, FN_WIDTH, D_MODEL, D_MODEL)
    in_first = [pl.BlockSpec((TM, w), lambda i: (jnp.minimum(i, n_first - 1), 0)) for w in widths]
    in_second = [pl.BlockSpec((TM, w), lambda i: (jnp.maximum(i - n_first, 0), 0)) for w in widths]
    return pl.pallas_call(
        functools.partial(_merge_kernel, n_first=n_first),
        grid=(n // TM,),
        in_specs=in_first + in_second + [
                  pl.BlockSpec((1, 6, D_MODEL), lambda i: (mod_row_fn(i), 0, 0)),
                  _const_spec((V_WIDTH, D_MODEL)),
                  _const_spec((FN_WIDTH, D_MODEL)),
                  _const_spec((D_MODEL, D_MODEL)),
                  _const_spec((1, D_MODEL)),
                  _const_spec((ROUTER_ROWS, D_MODEL)),
                  _const_spec((ROUTER_ROWS, 1))],
        out_specs=[tok(D_MODEL),
                   pl.BlockSpec((TM * TOKEN_ROWS, LANES), lambda i: (i, 0)),
                   pl.BlockSpec((1, TM), lambda i: (0, i)),
                   pl.BlockSpec((1, TM), lambda i: (0, i)), _const_spec((SEG_ROWS, LANES))],
        out_shape=[jax.ShapeDtypeStruct((n, D_MODEL), F32),
                   jax.ShapeDtypeStruct((n * TOKEN_ROWS, LANES), F32),
                   jax.ShapeDtypeStruct((1, n), jnp.int32),
                   jax.ShapeDtypeStruct((1, n), jnp.int32),
                   jax.ShapeDtypeStruct((SEG_ROWS, LANES), jnp.int32)],
        scratch_shapes=[pltpu.VMEM((SEG_ROWS, LANES), F32)],
        compiler_params=_cparams(("arbitrary",)),
        name="merge",
    )(*first, *second, mod3, wts["w_ao"], wts["w_fn"], wts["w_out"], wts["g2"],
      wts["w_rg"], wts["b_rg"])


def _token_rows(ref, t):
    start = t * TOKEN_ROWS
    if not isinstance(t, int):
        start = pl.multiple_of(start, TOKEN_ROWS)
    return ref.at[pl.ds(start, TOKEN_ROWS)]


def _row_copy(src, dst, src_tok, dst_tok, sem):
    return pltpu.make_async_copy(_token_rows(src, src_tok), _token_rows(dst, dst_tok), sem)


def _wait_tile(src_hbm, buf, sem):
    pltpu.make_async_copy(src_hbm.at[pl.ds(0, buf.shape[0])], buf, sem).wait()


def _combine_kernel(pos_ref, y_hbm, x1_ref, mod_ref, o1_ref, o2_ref, buf, sem, *, n_first):
    i = pl.program_id(0)
    tm = o1_ref.shape[0]

    def gather(tile_idx, slot):
        def start(r, c):
            _row_copy(y_hbm, buf.at[slot], pos_ref[tile_idx * tm + r], r, sem.at[slot]).start()
            return c

        lax.fori_loop(0, tm, start, 0, unroll=32)

    @pl.when(i == 0)
    def _():
        gather(0, 0)

    @pl.when(i + 1 < pl.num_programs(0))
    def _():
        gather(i + 1, (i + 1) % 2)

    slot = i % 2

    _wait_tile(y_hbm, buf.at[slot], sem.at[slot])
    out = x1_ref[...] + mod_ref[0, 5:6, :] * _load_token_major(buf.at[slot], tm)

    @pl.when(i < n_first)
    def _():
        o1_ref[...] = out

    @pl.when(i >= n_first)
    def _():
        o2_ref[...] = out


def _combine(y_sorted, pos, x1, mod3, mod_row_fn, n_first_tokens):
    n = x1.shape[0]
    n_first = n_first_tokens // TM
    return pl.pallas_call(
        functools.partial(_combine_kernel, n_first=n_first),
        grid_spec=pltpu.PrefetchScalarGridSpec(
            num_scalar_prefetch=1,
            grid=(n // TM,),
            in_specs=[pl.BlockSpec(memory_space=pl.ANY),
                      pl.BlockSpec((TM, D_MODEL), lambda i, pos: (i, 0)),
                      pl.BlockSpec((1, 6, D_MODEL), lambda i, pos: (mod_row_fn(i), 0, 0))],
            out_specs=[pl.BlockSpec((TM, D_MODEL), lambda i, pos: (jnp.minimum(i, n_first - 1), 0)),
                       pl.BlockSpec((TM, D_MODEL), lambda i, pos: (jnp.maximum(i - n_first, 0), 0))],
            scratch_shapes=[pltpu.VMEM((2, TM * TOKEN_ROWS, LANES), F32),
                            pltpu.SemaphoreType.DMA((2,))]),
        out_shape=[jax.ShapeDtypeStruct((n_first_tokens, D_MODEL), F32),
                   jax.ShapeDtypeStruct((n - n_first_tokens, D_MODEL), F32)],
        compiler_params=_cparams(("arbitrary",)),
        name="moe_combine",
    )(pos, y_sorted, x1, mod3)


def _moe_kernel(grp_ref, on_ref, src_ref, h_hbm, wrg_ref, brg_ref, wre_ref, bre_ref, wg_ref, wu_ref,
                wd_ref, o_ref, hbuf, sem, *, tile):
    i = pl.program_id(0)
    n_tiles = pl.num_programs(0)

    def start(tile_idx, r):
        slot = tile_idx % (GATHER_AHEAD + 1)
        _row_copy(h_hbm, hbuf.at[slot], src_ref[tile_idx * tile + r], r, sem.at[slot]).start()

    for first in range(GATHER_AHEAD):
        @pl.when(jnp.logical_and(i == 0, on_ref[first] == 1))
        def _():
            def body(r, c):
                start(first, r)
                return c

            lax.fori_loop(0, tile, body, 0, unroll=32)

    def run(fetch_ahead):
        def issue(chunk, n_chunks):
            if fetch_ahead:
                per = tile // n_chunks
                for r in range(chunk * per, (chunk + 1) * per):
                    start(i + GATHER_AHEAD, r)

        slot = i % (GATHER_AHEAD + 1)
        _wait_tile(h_hbm, hbuf.at[slot], sem.at[slot])
        h2_hi, h2_lo = _split_bf16(_load_token_major(hbuf.at[slot], tile))

        logits = _router_logits(jnp.concatenate([wrg_ref[...], wre_ref[0]], axis=0),
                                jnp.concatenate([brg_ref[...], bre_ref[0]], axis=0), h2_hi, h2_lo)
        g = _rows(logits, N_GROUPS)
        gmax = functools.reduce(jnp.maximum, g)
        p_top = 1.0 / functools.reduce(lambda p, q: p + q, [jnp.exp(gj - gmax) for gj in g])
        e = _rows(logits[ROUTER_ROWS:], EXPERTS_PER_GROUP)
        m1 = functools.reduce(jnp.maximum, e)
        i1 = _first_argmax(e, m1)
        rest = [jnp.where(i1 == j, -jnp.inf, e[j]) for j in range(EXPERTS_PER_GROUP)]
        m2 = functools.reduce(jnp.maximum, rest)
        i2 = _first_argmax(rest, m2)
        t = jnp.exp(m2 - m1)
        w1 = p_top / (1.0 + t)
        w2 = p_top * t / (1.0 + t)
        row = lax.broadcasted_iota(jnp.int32, (LANES, tile), 0)
        comb = (jnp.where(row == i1, w1, 0.0) + jnp.where(row == i2, w2, 0.0)).T

        gates, ups = [], []
        for j in range(EXPERTS_PER_GROUP):
            gates.append(_dot(h2_hi, wg_ref[j]))
            issue(2 * j, 2 * EXPERTS_PER_GROUP)
            ups.append(_dot(h2_hi, wu_ref[j]))
            issue(2 * j + 1, 2 * EXPERTS_PER_GROUP)
        acts = [((a * _sigmoid(a)) * u * comb[:, j:j + 1]).astype(BF16)
                for j, (a, u) in enumerate(zip(gates, ups))]
        acc = functools.reduce(lambda p, q: p + q,
                               [_dot(acts[j], wd_ref[j]) for j in range(EXPERTS_PER_GROUP)])
        _store_token_major(o_ref, acc, tile)

    @pl.when(on_ref[i] == 0)
    def _():
        o_ref[...] = jnp.zeros_like(o_ref)

    ahead = jnp.minimum(i + GATHER_AHEAD, n_tiles - 1)
    fetch = jnp.logical_and(i + GATHER_AHEAD < n_tiles, on_ref[ahead] == 1)

    @pl.when(jnp.logical_and(on_ref[i] == 1, fetch))
    def _():
        run(True)

    @pl.when(jnp.logical_and(on_ref[i] == 1, jnp.logical_not(fetch)))
    def _():
        run(False)


def _moe(h2tm, src, maps, wts, w_gate, w_up, w_down, tile):
    n_tiles = src.shape[0] // tile
    const = lambda shape: pl.BlockSpec(shape, lambda i, grp, on, src: (0,) * len(shape))
    by_group = lambda shape: pl.BlockSpec(
        shape, lambda i, grp, on, src: (grp[i],) + (0,) * (len(shape) - 1))
    return pl.pallas_call(
        functools.partial(_moe_kernel, tile=tile),
        grid_spec=pltpu.PrefetchScalarGridSpec(
            num_scalar_prefetch=3,
            grid=(n_tiles,),
            in_specs=[pl.BlockSpec(memory_space=pl.ANY),
                      const((ROUTER_ROWS, D_MODEL)), const((ROUTER_ROWS, 1)),
                      by_group((1, ROUTER_ROWS, D_MODEL)), by_group((1, ROUTER_ROWS, 1)),
                      by_group((EXPERTS_PER_GROUP, D_MODEL, D_EXPERT)),
                      by_group((EXPERTS_PER_GROUP, D_MODEL, D_EXPERT)),
                      by_group((EXPERTS_PER_GROUP, D_EXPERT, D_MODEL))],
            out_specs=pl.BlockSpec((tile * TOKEN_ROWS, LANES), lambda i, grp, on, src: (i, 0)),
            scratch_shapes=[pltpu.VMEM((GATHER_AHEAD + 1, tile * TOKEN_ROWS, LANES), F32),
                            pltpu.SemaphoreType.DMA((GATHER_AHEAD + 1,))]),
        out_shape=jax.ShapeDtypeStruct((n_tiles * tile * TOKEN_ROWS, LANES), F32),
        compiler_params=_cparams(("arbitrary",)),
        name="moe",
    )(*maps, src, h2tm, wts["w_rg"], wts["b_rg"], wts["w_re"], wts["b_re"], w_gate, w_up, w_down)


def _invert_kernel(pos_ref, lo_ref, hi_ref, src_ref, *, n, n_fill):
    def zero(p, c):
        src_ref[p] = 0
        return c

    for s in range(n_fill):
        lax.fori_loop(lo_ref[s], hi_ref[s], zero, 0)

    def put(t, c):
        src_ref[pos_ref[t]] = t
        return c

    lax.fori_loop(0, n, put, 0, unroll=8)


def _invert(pos, fill_lo, fill_hi, n_sorted):
    return pl.pallas_call(
        functools.partial(_invert_kernel, n=pos.shape[0], n_fill=fill_lo.shape[0]),
        grid_spec=pltpu.PrefetchScalarGridSpec(
            num_scalar_prefetch=3, grid=(1,), in_specs=[],
            out_specs=pl.BlockSpec(memory_space=pltpu.SMEM)),
        out_shape=jax.ShapeDtypeStruct((n_sorted,), jnp.int32),
        name="moe_invert",
    )(pos, fill_lo, fill_hi)


def _sort_plan(grp, rank, cnt, tile, n_tiles):
    n_tile_grp = (cnt + tile - 1) // tile
    tile_end = jnp.cumsum(n_tile_grp)
    tile_start = tile_end - n_tile_grp
    total = tile_end[-1]
    pos = ((tile_start * tile)[grp] + rank).astype(jnp.int32)
    fill_lo = jnp.concatenate([tile_start * tile + cnt, total[None] * tile]).astype(jnp.int32)
    fill_hi = jnp.concatenate([tile_end * tile, jnp.full((1,), n_tiles * tile)]).astype(jnp.int32)
    src = _invert(pos, fill_lo, fill_hi, n_tiles * tile)
    i = jnp.arange(n_tiles, dtype=jnp.int32)
    tile_grp = jnp.sum((jnp.minimum(i, total - 1)[:, None] >= tile_end[None, :]).astype(jnp.int32),
                       axis=1)
    return pos, src, (tile_grp.astype(jnp.int32), (i < total).astype(jnp.int32))


def _sparse_moe(x1, h2tm, grp, rank, cnt8, mod3, mod_row_fn, wts, w_gate, w_up, w_down, tile,
                n_first_tokens):
    n = x1.shape[0]
    n_tiles = n // tile + N_GROUPS
    pos, src, maps = _sort_plan(grp.reshape(n), rank.reshape(n), cnt8[:N_GROUPS, 0], tile, n_tiles)
    y_sorted = _moe(h2tm, src, maps, wts, w_gate, w_up, w_down, tile)
    return _combine(y_sorted, pos, x1, mod3, mod_row_fn, n_first_tokens)


def _pack_weights(l, norm1_g, q_a_norm_g, w_q_b, kv_a_norm_g, w_kv_b, q_norm_g, k_norm_g,
                  w_attn_o, w_fnet, w_out, norm2_g, w_router_group, b_router_group,
                  w_router_expert, b_router_expert):
    w_qb = jnp.pad(w_q_b[l].reshape(Q_LORA, N_HEADS, QK_HEAD),
                   ((0, 0), (0, 0), (0, HEAD_SLOT - QK_HEAD))).reshape(Q_LORA, QK_WIDTH)
    wkv = w_kv_b[l].reshape(KV_LORA, N_HEADS, QK_NOPE + V_HEAD)
    w_kb = jnp.pad(wkv[:, :, :QK_NOPE],
                   ((0, 0), (0, 0), (0, HEAD_SLOT - QK_NOPE))).reshape(KV_LORA, QK_WIDTH)
    w_vb = wkv[:, :, QK_NOPE:].reshape(KV_LORA, V_WIDTH)
    pad_g = lambda g: jnp.pad(g, (0, HEAD_SLOT - QK_HEAD)).reshape(1, HEAD_SLOT)
    w_rg = jnp.pad(w_router_group[l].T, ((0, ROUTER_ROWS - N_GROUPS), (0, 0)))
    b_rg = jnp.pad(b_router_group[l], (0, ROUTER_ROWS - N_GROUPS)).reshape(ROUTER_ROWS, 1)
    w_re = jnp.pad(w_router_expert[l].T.reshape(N_GROUPS, EXPERTS_PER_GROUP, D_MODEL),
                   ((0, 0), (0, ROUTER_ROWS - EXPERTS_PER_GROUP), (0, 0)))
    b_re = jnp.pad(b_router_expert[l].reshape(N_GROUPS, EXPERTS_PER_GROUP),
                   ((0, 0), (0, ROUTER_ROWS - EXPERTS_PER_GROUP))).reshape(N_GROUPS, ROUTER_ROWS, 1)
    dft_c, dft_ns = _dft_tables(FN_GROUP_W)
    return {
        "g1": norm1_g[l].reshape(1, D_MODEL),
        "qag": q_a_norm_g[l].reshape(1, Q_LORA),
        "w_qb": w_qb.astype(BF16),
        "kvg": kv_a_norm_g[l].reshape(1, KV_LORA),
        "w_kb": w_kb.astype(BF16),
        "w_vb": w_vb.T.astype(BF16),
        "qg": pad_g(q_norm_g[l]),
        "kg": pad_g(k_norm_g[l]),
        "dft_c": jnp.concatenate([jnp.asarray(dft_c), -jnp.asarray(dft_ns)], axis=1).astype(BF16),
        "w_ao": w_attn_o[l].astype(BF16),
        "w_fn": w_fnet[l].astype(BF16),
        "w_out": w_out[l].astype(BF16),
        "g2": norm2_g[l].reshape(1, D_MODEL),
        "w_rg": w_rg,
        "b_rg": b_rg,
        "w_re": w_re,
        "b_re": b_re,
    }


def _layer(xp, xs, cache_ckv_l, cache_kpe_l, mod3, wts, experts):
    bp, sp, _ = xp.shape
    bs, ss, _ = xs.shape
    ctx_row = lambda i: 0
    lat_row = lambda i: 1 + i // (ss // TM)

    xp2 = xp.reshape(bp * sp, D_MODEL)
    ride = (bp * sp) // TM == N_EXPERTS
    cs, ns = (jnp.asarray(t).astype(BF16) for t in _dft_tables(sp))
    outs = _inproj(xp2, mod3, ctx_row, wts, None, True, experts if ride else (), (sp, cs, ns))
    attn, fm, sga, sgf, ckv, kpe = outs[:6]
    w_gate, w_up, w_down = outs[6:] if ride else (w.astype(BF16) for w in experts)
    ctx_set = (xp2, attn, fm, sga, sgf)

    xs2 = xs.reshape(bs * ss, D_MODEL)
    rope_tabs = tuple(jnp.asarray(t) for t in _rope_tables(ss))
    q, k, v, fcs, sga, sgf = _inproj(xs2, mod3, lat_row, wts, rope_tabs, False)
    kpe_slot = jnp.pad(cache_kpe_l, ((0, 0), (0, 0), (QK_NOPE, LANES - QK_HEAD)))
    cs, ns = (jnp.asarray(t).astype(BF16) for t in _dft_tables(ss))
    attn, fm = _attention(q.reshape(bs, ss, QK_WIDTH), k.reshape(bs, ss, QK_WIDTH), v, cache_ckv_l,
                          kpe_slot, wts, fcs.reshape(bs, ss, 2 * FN_WIDTH), cs, ns, TQ_LAT, "attn_lat")
    lat_set = (xs2, attn.reshape(bs * ss, V_WIDTH), fm.reshape(bs * ss, FN_WIDTH), sga, sgf)

    n_ctx_tiles = (bp * sp) // TM
    all_row = lambda i: jnp.where(i < n_ctx_tiles, 0, lat_row(i - n_ctx_tiles))
    x1, h2tm, grp, rank, cnt = _merge(ctx_set, lat_set, mod3, all_row, wts)
    yp, ys = _sparse_moe(x1, h2tm, grp, rank, cnt, mod3, all_row, wts, w_gate, w_up, w_down, MOE_TILE,
                         bp * sp)
    yp, ys = yp.reshape(bp, sp, D_MODEL), ys.reshape(bs, ss, D_MODEL)

    return yp, ys, ckv.reshape(bp, sp, KV_LORA), jnp.swapaxes(kpe, 1, 2)


def kernel(x_prompt, x_sample, cache_ckv, cache_kpe, c, c_ctx, w_mod, b_mod, norm1_g, w_in, q_a_norm_g, w_q_b, kv_a_norm_g, w_kv_b, q_norm_g, k_norm_g, w_attn_o, w_fnet, w_out, norm2_g, w_router_group, b_router_group, w_router_expert, b_router_expert, w_exp_gate, w_exp_up, w_exp_down):
    depth = w_mod.shape[0]
    n_lat = c.shape[0]
    assert 1 + n_lat <= MOD_ROWS
    cond8 = jnp.concatenate([c_ctx[None, :], c, jnp.zeros((MOD_ROWS - 1 - n_lat, D_MODEL), F32)], axis=0)
    xp, xs = x_prompt, x_sample
    ckv_layers, kpe_layers = [], []
    for l in range(depth):
        mod, w_in_p = _adaln(cond8, w_mod[l], b_mod[l].reshape(1, -1), jnp.swapaxes(w_in[l], 0, 1))
        mod3 = mod.reshape(MOD_ROWS, 6, D_MODEL)
        wts = _pack_weights(l, norm1_g, q_a_norm_g, w_q_b, kv_a_norm_g, w_kv_b, q_norm_g,
                            k_norm_g, w_attn_o, w_fnet, w_out, norm2_g, w_router_group,
                            b_router_group, w_router_expert, b_router_expert)
        wts["w_in"] = w_in_p
        xp, xs, ckv, kpe = _layer(xp, xs, cache_ckv[:, l], cache_kpe[:, l], mod3, wts,
                                  (w_exp_gate[l], w_exp_up[l], w_exp_down[l]))
        ckv_layers.append(ckv)
        kpe_layers.append(kpe)
    return xp, xs, jnp.stack(ckv_layers, axis=1), jnp.stack(kpe_layers, axis=1)
```

```python
import functools
import math

import numpy as np
import jax
import jax.numpy as jnp
from jax import lax
from jax.experimental import pallas as pl
from jax.experimental.pallas import tpu as pltpu

D_MODEL = 1024
GRID_W = 64
N_HEADS = 8
Q_LORA = 512
KV_LORA = 256
QK_NOPE = 64
QK_ROPE = 32
V_HEAD = 64
QK_HEAD = QK_NOPE + QK_ROPE
ATTN_SCALE = QK_HEAD ** -0.5
ROPE_BASE = 10000.0
FN_GROUPS = 4
FN_GROUP_W = 128
FN_WIDTH = FN_GROUPS * FN_GROUP_W
N_GROUPS = 4
EXPERTS_PER_GROUP = 4
N_EXPERTS = N_GROUPS * EXPERTS_PER_GROUP
D_EXPERT = 512
EPS = 1e-6

LANES = 128
HEAD_SLOT = LANES
QK_WIDTH = N_HEADS * HEAD_SLOT
V_WIDTH = N_HEADS * V_HEAD
C_QA = 0
C_KVA = C_QA + Q_LORA
C_KPE = C_KVA + KV_LORA
C_FN = C_KPE + LANES
C_GA = C_FN + FN_WIDTH
C_GF = C_GA + D_MODEL
IN_PACKED = C_GF + D_MODEL
SUBLANES = 8
ROUTER_ROWS = SUBLANES
MOD_ROWS = SUBLANES
VMEM_LIMIT = 56 * 1024 * 1024

TM = 512
TQ_LAT = 256
SEG_ROWS = SUBLANES
TOKEN_ROWS = D_MODEL // LANES
MOE_TILE = 256
MERGE_SPLIT = 2
GATHER_AHEAD = 2
ADALN_STEPS = 8

BF16 = jnp.bfloat16
F32 = jnp.float32


def _cparams(sem):
    return pltpu.CompilerParams(dimension_semantics=sem, vmem_limit_bytes=VMEM_LIMIT)


def _dot(a, b):
    return jnp.dot(a, b, preferred_element_type=F32)


def _dot_nt(a, b):
    return lax.dot_general(a, b, (((1,), (1,)), ((), ())), preferred_element_type=F32)


def _sigmoid(x):
    return 1.0 / (1.0 + jnp.exp(-x))


def _split_bf16(x):
    hi = x.astype(BF16)
    return hi, (x - hi.astype(F32)).astype(BF16)


@functools.lru_cache(maxsize=None)
def _rope_tables(n_pos):
    half = QK_ROPE // 2
    quarter = half // 2
    freqs = ROPE_BASE ** (-np.arange(quarter, dtype=np.float64) / quarter)
    pos = np.arange(n_pos)
    row = (pos // GRID_W).astype(np.float64)
    col = (pos % GRID_W).astype(np.float64)
    cos_t = np.ones((n_pos, LANES), np.float64)
    sin_a = np.zeros((n_pos, LANES), np.float64)
    sin_b = np.zeros((n_pos, LANES), np.float64)
    for base, p in ((QK_NOPE, row), (QK_NOPE + half, col)):
        ang = p[:, None] * freqs[None, :]
        cos_t[:, base:base + quarter] = np.cos(ang)
        cos_t[:, base + quarter:base + half] = np.cos(ang)
        sin_a[:, base:base + quarter] = -np.sin(ang)
        sin_b[:, base + quarter:base + half] = np.sin(ang)
    return (cos_t.astype(np.float32), sin_a.astype(np.float32), sin_b.astype(np.float32))


@functools.lru_cache(maxsize=None)
def _dft_tables(n):
    k = np.arange(n)
    ang = 2.0 * np.pi * ((k[:, None] * k[None, :]) % n) / n
    s = 1.0 / math.sqrt(n)
    return (np.cos(ang) * s).astype(np.float32), (-np.sin(ang) * s).astype(np.float32)


def _adaln_kernel(cond_ref, w_ref, b_ref, win_ref, o_ref, winp_ref):
    c = cond_ref[...]
    s_hi, s_lo = _split_bf16(c * _sigmoid(c))
    w_hi, w_lo = _split_bf16(w_ref[...])
    y = _dot(jnp.concatenate([s_hi, s_lo], axis=0), w_hi)
    o_ref[...] = (y[:MOD_ROWS] + y[MOD_ROWS:]) + _dot(s_hi, w_lo) + b_ref[...]

    winp_ref[:C_KPE, :] = win_ref[:C_KPE, :].astype(BF16)
    winp_ref[C_KPE:C_FN, :] = jnp.zeros((LANES, winp_ref.shape[1]), BF16)
    winp_ref[C_KPE + QK_NOPE:C_KPE + QK_HEAD, :] = win_ref[C_KPE:C_KPE + QK_ROPE, :].astype(BF16)
    winp_ref[C_FN:, :] = win_ref[C_KPE + QK_ROPE:, :].astype(BF16)


def _adaln(cond8, w_mod, b_mod, w_in_t):
    n = w_mod.shape[1]
    steps = ADALN_STEPS
    tn, tr = n // steps, D_MODEL // steps
    return pl.pallas_call(
        _adaln_kernel,
        grid=(steps,),
        in_specs=[pl.BlockSpec((MOD_ROWS, D_MODEL), lambda j: (0, 0)),
                  pl.BlockSpec((D_MODEL, tn), lambda j: (0, j)),
                  pl.BlockSpec((1, tn), lambda j: (0, j)),
                  pl.BlockSpec((w_in_t.shape[0], tr), lambda j: (0, j))],
        out_specs=[pl.BlockSpec((MOD_ROWS, tn), lambda j: (0, j)),
                   pl.BlockSpec((IN_PACKED, tr), lambda j: (0, j))],
        out_shape=[jax.ShapeDtypeStruct((MOD_ROWS, n), F32),
                   jax.ShapeDtypeStruct((IN_PACKED, D_MODEL), BF16)],
        compiler_params=_cparams(("arbitrary",)),
        name="adaln",
    )(cond8, w_mod, b_mod, w_in_t)


def _rms(x, width):
    return lax.rsqrt(jnp.sum(x * x, axis=-1, keepdims=True) * (1.0 / width) + EPS)


def _rope(x, cos_t, sin_a, sin_b):
    return x * cos_t + pltpu.roll(x, LANES - 8, 1) * sin_a + pltpu.roll(x, 8, 1) * sin_b


def _inproj_kernel(*refs, rope, emit_cache, n_cast, seq_len):
    it = iter(refs)
    x_ref, mod_ref, g1_ref, win_ref, qag_ref, wqb_ref, kvg_ref, wkb_ref, wvb_ref = (
        next(it) for _ in range(9))
    qg_ref, kg_ref, dft_ref = next(it), next(it), next(it)
    if rope:
        cos_ref, sa_ref, sb_ref = next(it), next(it), next(it)
    if seq_len:
        seq_cs_ref, seq_ns_ref = next(it), next(it)
    cast_in = [next(it) for _ in range(n_cast)]
    if seq_len:
        attn_ref, fm_ref, sga_ref, sgf_ref = (next(it) for _ in range(4))
    else:
        q_ref, k_ref, v_ref, fcs_ref, sga_ref, sgf_ref = (next(it) for _ in range(6))
    if emit_cache:
        ckv_ref, kpe_ref = next(it), next(it)
    cast_out = [next(it) for _ in range(n_cast)]
    if seq_len:
        q_ref, k_ref, v_ref, fcs_ref = (next(it) for _ in range(4))

    for src, dst in zip(cast_in, cast_out):
        dst[...] = src[...].astype(BF16)

    x = x_ref[...]
    shift = mod_ref[0, 0:1, :]
    scale = mod_ref[0, 1:2, :]
    h = (x * _rms(x, D_MODEL) * g1_ref[...]) * (1.0 + scale) + shift
    hb = h.astype(BF16)

    if rope:
        cos_t, sin_a, sin_b = cos_ref[...], sa_ref[...], sb_ref[...]

    qa = _dot_nt(hb, win_ref[C_QA:C_QA + Q_LORA, :])
    qn = (qa * _rms(qa, Q_LORA) * qag_ref[...]).astype(BF16)
    q = _dot(qn, wqb_ref[...])
    qg = qg_ref[...] * ATTN_SCALE
    for hd in range(N_HEADS):
        qh = q[:, hd * HEAD_SLOT:(hd + 1) * HEAD_SLOT]
        qh = qh * _rms(qh, QK_HEAD) * qg
        if rope:
            qh = _rope(qh, cos_t, sin_a, sin_b)
        q_ref[:, hd * HEAD_SLOT:(hd + 1) * HEAD_SLOT] = qh.astype(BF16)

    kva = _dot_nt(hb, win_ref[C_KVA:C_KVA + KV_LORA, :])
    ckv = kva * _rms(kva, KV_LORA) * kvg_ref[...]
    kpe = _dot_nt(hb, win_ref[C_KPE:C_KPE + LANES, :])
    if emit_cache:
        ckv_ref[...] = ckv
        kpe_t = kpe.T[QK_NOPE:QK_NOPE + QK_ROPE, :]
        for j in range(x.shape[0] // seq_len):
            kpe_ref[j] = kpe_t[:, j * seq_len:(j + 1) * seq_len]
    _emit_kv(ckv.astype(BF16), kpe, wkb_ref, wvb_ref, kg_ref,
             (cos_t, sin_a, sin_b) if rope else None, k_ref, v_ref)

    fn = _dot_nt(hb, win_ref[C_FN:C_FN + FN_WIDTH, :]).astype(BF16)
    for g in range(FN_GROUPS):
        cs = _dot(fn[:, g * FN_GROUP_W:(g + 1) * FN_GROUP_W], dft_ref[...])
        fcs_ref[:, g * FN_GROUP_W:(g + 1) * FN_GROUP_W] = cs[:, :FN_GROUP_W].astype(BF16)
        fcs_ref[:, FN_WIDTH + g * FN_GROUP_W:FN_WIDTH + (g + 1) * FN_GROUP_W] = (
            cs[:, FN_GROUP_W:].astype(BF16))

    def gate(ref, c0, half):
        def run():
            w = D_MODEL // 2
            ref[:, pl.ds(half * w, w)] = _sigmoid(
                _dot_nt(hb, win_ref[pl.ds(c0 + half * w, w), :])).astype(BF16)
        return run

    gates = [gate(ref, c0, half) for ref, c0 in ((sga_ref, C_GA), (sgf_ref, C_GF)) for half in range(2)]

    if seq_len:
        n_seq = x.shape[0] // seq_len
        for j in range(n_seq):
            rows = pl.ds(j * seq_len, seq_len)
            fill = gates[j * len(gates) // n_seq:(j + 1) * len(gates) // n_seq]
            attn_ref[rows, :] = _attend(q_ref.at[rows], [k_ref.at[rows]], [v_ref.at[:, rows]], fill)
            fm_ref[rows, :] = (_dot(seq_cs_ref[...], fcs_ref[rows, :FN_WIDTH])
                               + _dot(seq_ns_ref[...], fcs_ref[rows, FN_WIDTH:])).astype(BF16)
    else:
        for run_gate in gates:
            run_gate()


def _emit_kv(ckvb, kpe, wkb_ref, wvb_ref, kg_ref, rope_tabs, k_ref, v_ref):
    kg = kg_ref[...]
    v_ref[...] = _dot_nt(wvb_ref[...], ckvb).astype(BF16)
    kn = _dot(ckvb, wkb_ref[...])
    pe_ss = jnp.sum(kpe * kpe, axis=-1, keepdims=True)
    pe_g = kpe * kg
    if rope_tabs is not None:
        pe_g = _rope(pe_g, *rope_tabs)
    for hd in range(N_HEADS):
        knh = kn[:, hd * HEAD_SLOT:(hd + 1) * HEAD_SLOT]
        ss = jnp.sum(knh * knh, axis=-1, keepdims=True) + pe_ss
        r = lax.rsqrt(ss * (1.0 / QK_HEAD) + EPS)
        k_ref[:, hd * HEAD_SLOT:(hd + 1) * HEAD_SLOT] = ((knh * kg + pe_g) * r).astype(BF16)


def _const_spec(shape):
    return pl.BlockSpec(shape, lambda i: (0,) * len(shape))


def _inproj(x2d, mod3, mod_row_fn, wts, rope_tabs, emit_cache, cast=(), seq=None):
    n = x2d.shape[0]
    assert all(w.shape[0] == n // TM for w in cast)
    assert seq is None or TM % seq[0] == 0
    rope = rope_tabs is not None
    tiles_per_seq = None if not rope else rope_tabs[0].shape[0] // TM
    in_specs = [pl.BlockSpec((TM, D_MODEL), lambda i: (i, 0)),
                pl.BlockSpec((1, 6, D_MODEL), lambda i: (mod_row_fn(i), 0, 0)),
                _const_spec((1, D_MODEL)),
                _const_spec((IN_PACKED, D_MODEL)),
                _const_spec((1, Q_LORA)),
                _const_spec((Q_LORA, QK_WIDTH)),
                _const_spec((1, KV_LORA)),
                _const_spec((KV_LORA, QK_WIDTH)),
                _const_spec((V_WIDTH, KV_LORA)),
                _const_spec((1, HEAD_SLOT)),
                _const_spec((1, HEAD_SLOT)),
                _const_spec((FN_GROUP_W, 2 * FN_GROUP_W))]
    args = [x2d, mod3, wts["g1"], wts["w_in"], wts["qag"], wts["w_qb"], wts["kvg"],
            wts["w_kb"], wts["w_vb"], wts["qg"], wts["kg"], wts["dft_c"]]
    if rope:
        in_specs += [pl.BlockSpec((TM, LANES), lambda i: (i % tiles_per_seq, 0))] * 3
        args += list(rope_tabs)
    if seq is not None:
        in_specs += [_const_spec((seq[0], seq[0]))] * 2
        args += [seq[1], seq[2]]
    cast_specs = [pl.BlockSpec((1,) + w.shape[1:], lambda i: (i, 0, 0)) for w in cast]
    in_specs += cast_specs
    args += list(cast)
    qkvf = [((TM, QK_WIDTH), (n, QK_WIDTH)), ((TM, QK_WIDTH), (n, QK_WIDTH)),
            ((V_WIDTH, TM), (V_WIDTH, n)), ((TM, 2 * FN_WIDTH), (n, 2 * FN_WIDTH))]
    if seq is None:
        widths = [None] * 4 + [D_MODEL, D_MODEL]
    else:
        widths = [V_WIDTH, FN_WIDTH, D_MODEL, D_MODEL]
    out_shape, out_specs = [], []
    for j, w in enumerate(widths):
        if w is None:
            out_shape.append(jax.ShapeDtypeStruct(qkvf[j][1], BF16))
            out_specs.append(pl.BlockSpec(qkvf[j][0], (lambda i: (0, i)) if j == 2 else (lambda i: (i, 0))))
        else:
            out_shape.append(jax.ShapeDtypeStruct((n, w), BF16))
            out_specs.append(pl.BlockSpec((TM, w), lambda i: (i, 0)))
    if emit_cache:
        assert seq is not None
        out_shape += [jax.ShapeDtypeStruct((n, KV_LORA), F32),
                      jax.ShapeDtypeStruct((n // seq[0], QK_ROPE, seq[0]), F32)]
        out_specs += [pl.BlockSpec((TM, KV_LORA), lambda i: (i, 0)),
                      pl.BlockSpec((TM // seq[0], QK_ROPE, seq[0]), lambda i: (i, 0, 0))]
    out_shape += [jax.ShapeDtypeStruct(w.shape, BF16) for w in cast]
    out_specs += cast_specs
    scratch = [] if seq is None else [pltpu.VMEM(blk, BF16) for blk, _ in qkvf]
    return pl.pallas_call(
        functools.partial(_inproj_kernel, rope=rope, emit_cache=emit_cache, n_cast=len(cast),
                          seq_len=None if seq is None else seq[0]),
        grid=(n // TM,),
        in_specs=in_specs,
        out_specs=out_specs,
        out_shape=out_shape,
        scratch_shapes=scratch,
        compiler_params=_cparams(("parallel",)),
        name="inproj_lat" if rope else "inproj_ctx",
    )(*args)


def _attend(q, ks, vts, fillers=()):
    head = lambda hd: slice(hd * HEAD_SLOT, (hd + 1) * HEAD_SLOT)
    st = [jnp.stack([_dot_nt(k[:, head(hd)], q[:, head(hd)]) for hd in range(N_HEADS)]) for k in ks]
    m = functools.reduce(jnp.maximum, [sj.max(axis=1, keepdims=True) for sj in st])
    fillers = list(fillers)
    every = N_HEADS // max(1, len(fillers))
    p = []
    for hd in range(N_HEADS):
        p.append([jnp.exp(sj[hd] - m[hd]) for sj in st])
        if fillers and hd % every == 0:
            fillers.pop(0)()
    assert not fillers
    outs = []
    for hd in range(N_HEADS):
        l = functools.reduce(lambda a, b: a + b, [pj.sum(axis=0, keepdims=True) for pj in p[hd]])
        o = functools.reduce(lambda a, b: a + b,
                             [_dot(vt[hd * V_HEAD:(hd + 1) * V_HEAD, :], pj.astype(BF16))
                              for vt, pj in zip(vts, p[hd])])
        outs.append(o / l)
    return jnp.concatenate(outs, axis=0).T.astype(BF16)


def _attn_kernel(q_ref, k_ref, vt_ref, cckv_ref, ckpe_ref, wkb_ref, wvb_ref, kg_ref, cs_ref, ns_ref,
                 f_ref, o_ref, fm_ref, kc_ref, vtc_ref):
    @pl.when(pl.program_id(1) == 0)
    def _():
        _emit_kv(cckv_ref[0].astype(BF16), ckpe_ref[0], wkb_ref, wvb_ref, kg_ref, None, kc_ref, vtc_ref)

    def dft(c0, width):
        def run():
            fm_ref[0, :, pl.ds(c0, width)] = (
                _dot(cs_ref[...], f_ref[0, :, pl.ds(c0, width)])
                + _dot(ns_ref[...], f_ref[0, :, pl.ds(FN_WIDTH + c0, width)])).astype(BF16)
        return run

    half = FN_WIDTH // 2
    o_ref[0] = _attend(q_ref.at[0], [kc_ref, k_ref.at[0]], [vtc_ref, vt_ref],
                       [dft(0, half), dft(half, half)])


def _attention(q3, k3, vt, cache_ckv, cache_kpe_slot, wts, fcs3, cs, ns, tq, name):
    b, sq, _ = q3.shape
    past = cache_ckv.shape[1]
    seq = lambda shape: pl.BlockSpec(shape, lambda bi, qi: (bi, 0, 0))
    const = lambda shape: pl.BlockSpec(shape, lambda bi, qi: (0,) * len(shape))
    return pl.pallas_call(
        _attn_kernel,
        grid=(b, sq // tq),
        in_specs=[pl.BlockSpec((1, tq, QK_WIDTH), lambda bi, qi: (bi, qi, 0)),
                  seq((1, sq, QK_WIDTH)),
                  pl.BlockSpec((V_WIDTH, sq), lambda bi, qi: (0, bi)),
                  seq((1, past, KV_LORA)), seq((1, past, LANES)),
                  const((KV_LORA, QK_WIDTH)), const((V_WIDTH, KV_LORA)), const((1, HEAD_SLOT)),
                  pl.BlockSpec((tq, sq), lambda bi, qi: (qi, 0)),
                  pl.BlockSpec((tq, sq), lambda bi, qi: (qi, 0)),
                  seq((1, sq, 2 * FN_WIDTH))],
        out_specs=[pl.BlockSpec((1, tq, V_WIDTH), lambda bi, qi: (bi, qi, 0)),
                   pl.BlockSpec((1, tq, FN_WIDTH), lambda bi, qi: (bi, qi, 0))],
        out_shape=[jax.ShapeDtypeStruct((b, sq, V_WIDTH), BF16),
                   jax.ShapeDtypeStruct((b, sq, FN_WIDTH), BF16)],
        scratch_shapes=[pltpu.VMEM((past, QK_WIDTH), BF16), pltpu.VMEM((V_WIDTH, past), BF16)],
        compiler_params=_cparams(("parallel", "arbitrary")),
        name=name,
    )(q3, k3, vt, cache_ckv, cache_kpe_slot, wts["w_kb"], wts["w_vb"], wts["kg"], cs, ns, fcs3)


def _router_logits(w, b, h_hi, h_lo):
    rows = w.shape[0]
    w_hi, w_lo = _split_bf16(w)
    y = _dot_nt(jnp.concatenate([w_hi, w_lo], axis=0), h_hi)
    return (y[:rows] + y[rows:]) + _dot_nt(w_hi, h_lo) + b


def _rows(x, n):
    return [x[j:j + 1, :] for j in range(n)]


def _first_argmax(rows, top):
    idx = jnp.full(top.shape, len(rows) - 1, jnp.int32)
    for j in range(len(rows) - 2, -1, -1):
        idx = jnp.where(rows[j] == top, j, idx)
    return idx


def _store_token_major(ref, x, tm, first=0):
    for s in range(TOKEN_ROWS):
        ref[pl.ds(first * TOKEN_ROWS + s, tm, stride=TOKEN_ROWS), :] = x[:, s * LANES:(s + 1) * LANES]


def _load_token_major(ref, tm):
    return jnp.concatenate([ref[pl.ds(s, tm, stride=TOKEN_ROWS), :] for s in range(TOKEN_ROWS)],
                           axis=1)


def _merge_kernel(*refs, n_first):
    first, second = refs[0:5], refs[5:10]
    (mod_ref, wao_ref, wfn_ref, wout_ref, g2_ref, wrg_ref, brg_ref, x1_ref, h2_ref, grp_ref, rank_ref,
     cnt_ref, carry_ref) = refs[10:]
    i = pl.program_id(0)

    @pl.when(i == 0)
    def _():
        carry_ref[...] = jnp.zeros_like(carry_ref)

    def tile(src):
        x_ref, attn_ref, fm_ref, sga_ref, sgf_ref = src
        hm = x_ref.shape[0] // MERGE_SPLIT
        rows = [pl.ds(k * hm, hm) for k in range(MERGE_SPLIT)]
        live = [dict() for _ in range(MERGE_SPLIT)]

        def branches(k):
            live[k]["a"] = _dot(attn_ref[rows[k], :], wao_ref[...])
            live[k]["f"] = _dot(fm_ref[rows[k], :], wfn_ref[...])

        def gated(k):
            u = (sga_ref[rows[k], :].astype(F32) * live[k].pop("a")
                 + sgf_ref[rows[k], :].astype(F32) * live[k].pop("f"))
            live[k]["u"] = u.astype(BF16)

        def project(k):
            live[k]["y"] = _dot(live[k].pop("u"), wout_ref[...])

        def residual(k):
            x1 = x_ref[rows[k], :] + mod_ref[0, 2:3, :] * live[k].pop("y")
            x1_ref[rows[k], :] = x1
            h2 = (x1 * _rms(x1, D_MODEL) * g2_ref[...]) * (1.0 + mod_ref[0, 4:5, :]) + mod_ref[0, 3:4, :]
            _store_token_major(h2_ref, h2, hm, k * hm)
            live[k]["h2"] = _split_bf16(h2)

        def route(k):
            g = _rows(_router_logits(wrg_ref[...], brg_ref[...], *live[k].pop("h2")), N_GROUPS)
            gidx = _first_argmax(g, functools.reduce(jnp.maximum, g))
            onehot = jnp.where(lax.broadcasted_iota(jnp.int32, (SEG_ROWS, hm), 0) == gidx, 1.0, 0.0)
            before = (lax.broadcasted_iota(jnp.int32, (hm, hm), 0)
                      < lax.broadcasted_iota(jnp.int32, (hm, hm), 1))
            prefix = _dot(onehot.astype(BF16), jnp.where(before, 1.0, 0.0).astype(BF16))
            carry = carry_ref[...]
            rank = jnp.sum(onehot * (prefix + carry[:, 0:1]), axis=0, keepdims=True)
            grp_ref[:, rows[k]] = gidx
            rank_ref[:, rows[k]] = rank.astype(jnp.int32)
            carry_ref[...] = carry + jnp.sum(onehot, axis=1, keepdims=True)

        phases = [branches, gated, project, residual, route]
        for t in range(len(phases) + MERGE_SPLIT - 1):
            for k in range(MERGE_SPLIT):
                if 0 <= t - k < len(phases):
                    phases[t - k](k)
        cnt_ref[...] = carry_ref[...].astype(jnp.int32)

    @pl.when(i < n_first)
    def _():
        tile(first)

    @pl.when(i >= n_first)
    def _():
        tile(second)


def _merge(first, second, mod3, mod_row_fn, wts):
    n_first = first[0].shape[0] // TM
    n = first[0].shape[0] + second[0].shape[0]
    tok = lambda w: pl.BlockSpec((TM, w), lambda i: (i, 0))
    widths = (D_MODEL, V_WIDTH, FN_WIDTH, D_MODEL, D_MODEL)
    in_first = [pl.BlockSpec((TM, w), lambda i: (jnp.minimum(i, n_first - 1), 0)) for w in widths]
    in_second = [pl.BlockSpec((TM, w), lambda i: (jnp.maximum(i - n_first, 0), 0)) for w in widths]
    return pl.pallas_call(
        functools.partial(_merge_kernel, n_first=n_first),
        grid=(n // TM,),
        in_specs=in_first + in_second + [
                  pl.BlockSpec((1, 6, D_MODEL), lambda i: (mod_row_fn(i), 0, 0)),
                  _const_spec((V_WIDTH, D_MODEL)),
                  _const_spec((FN_WIDTH, D_MODEL)),
                  _const_spec((D_MODEL, D_MODEL)),
                  _const_spec((1, D_MODEL)),
                  _const_spec((ROUTER_ROWS, D_MODEL)),
                  _const_spec((ROUTER_ROWS, 1))],
        out_specs=[tok(D_MODEL),
                   pl.BlockSpec((TM * TOKEN_ROWS, LANES), lambda i: (i, 0)),
                   pl.BlockSpec((1, TM), lambda i: (0, i)),
                   pl.BlockSpec((1, TM), lambda i: (0, i)), _const_spec((SEG_ROWS, LANES))],
        out_shape=[jax.ShapeDtypeStruct((n, D_MODEL), F32),
                   jax.ShapeDtypeStruct((n * TOKEN_ROWS, LANES), F32),
                   jax.ShapeDtypeStruct((1, n), jnp.int32),
                   jax.ShapeDtypeStruct((1, n), jnp.int32),
                   jax.ShapeDtypeStruct((SEG_ROWS, LANES), jnp.int32)],
        scratch_shapes=[pltpu.VMEM((SEG_ROWS, LANES), F32)],
        compiler_params=_cparams(("arbitrary",)),
        name="merge",
    )(*first, *second, mod3, wts["w_ao"], wts["w_fn"], wts["w_out"], wts["g2"],
      wts["w_rg"], wts["b_rg"])


def _token_rows(ref, t):
    start = t * TOKEN_ROWS
    if not isinstance(t, int):
        start = pl.multiple_of(start, TOKEN_ROWS)
    return ref.at[pl.ds(start, TOKEN_ROWS)]


def _row_copy(src, dst, src_tok, dst_tok, sem):
    return pltpu.make_async_copy(_token_rows(src, src_tok), _token_rows(dst, dst_tok), sem)


def _wait_tile(src_hbm, buf, sem):
    pltpu.make_async_copy(src_hbm.at[pl.ds(0, buf.shape[0])], buf, sem).wait()


def _combine_kernel(pos_ref, y_hbm, x1_ref, mod_ref, o1_ref, o2_ref, buf, sem, *, n_first):
    i = pl.program_id(0)
    n_tiles = pl.num_programs(0)
    tm = o1_ref.shape[0]

    def start(tile_idx, r):
        slot = tile_idx % (GATHER_AHEAD + 1)
        _row_copy(y_hbm, buf.at[slot], pos_ref[tile_idx * tm + r], r, sem.at[slot]).start()

    for first in range(GATHER_AHEAD):
        @pl.when(i == 0)
        def _():
            def body(r, c):
                start(first, r)
                return c

            lax.fori_loop(0, tm, body, 0, unroll=32)

    def run(fetch_ahead, o_ref):
        slot = i % (GATHER_AHEAD + 1)
        _wait_tile(y_hbm, buf.at[slot], sem.at[slot])
        lane_tiles = [slice(s * LANES, (s + 1) * LANES) for s in range(TOKEN_ROWS)]
        for s, cols in enumerate(lane_tiles):
            o_ref[:, cols] = mod_ref[0, 5:6, cols] * buf.at[slot][pl.ds(s, tm, stride=TOKEN_ROWS), :]
        per = tm // TOKEN_ROWS
        for s, cols in enumerate(lane_tiles):
            o_ref[:, cols] = x1_ref[:, cols] + o_ref[:, cols]
            if fetch_ahead:
                for r in range(s * per, (s + 1) * per):
                    start(i + GATHER_AHEAD, r)

    fetch = i + GATHER_AHEAD < n_tiles
    for fetch_ahead in (True, False):
        cond = fetch if fetch_ahead else jnp.logical_not(fetch)
        for o_ref, mine in ((o1_ref, i < n_first), (o2_ref, i >= n_first)):
            @pl.when(jnp.logical_and(cond, mine))
            def _():
                run(fetch_ahead, o_ref)


def _combine(y_sorted, pos, x1, mod3, mod_row_fn, n_first_tokens):
    n = x1.shape[0]
    n_first = n_first_tokens // TM
    assert n // TM >= GATHER_AHEAD
    return pl.pallas_call(
        functools.partial(_combine_kernel, n_first=n_first),
        grid_spec=pltpu.PrefetchScalarGridSpec(
            num_scalar_prefetch=1,
            grid=(n // TM,),
            in_specs=[pl.BlockSpec(memory_space=pl.ANY),
                      pl.BlockSpec((TM, D_MODEL), lambda i, pos: (i, 0)),
                      pl.BlockSpec((1, 6, D_MODEL), lambda i, pos: (mod_row_fn(i), 0, 0))],
            out_specs=[pl.BlockSpec((TM, D_MODEL), lambda i, pos: (jnp.minimum(i, n_first - 1), 0)),
                       pl.BlockSpec((TM, D_MODEL), lambda i, pos: (jnp.maximum(i - n_first, 0), 0))],
            scratch_shapes=[pltpu.VMEM((GATHER_AHEAD + 1, TM * TOKEN_ROWS, LANES), F32),
                            pltpu.SemaphoreType.DMA((GATHER_AHEAD + 1,))]),
        out_shape=[jax.ShapeDtypeStruct((n_first_tokens, D_MODEL), F32),
                   jax.ShapeDtypeStruct((n - n_first_tokens, D_MODEL), F32)],
        compiler_params=_cparams(("arbitrary",)),
        name="moe_combine",
    )(pos, y_sorted, x1, mod3)


def _moe_kernel(grp_ref, on_ref, src_ref, h_hbm, wrg_ref, brg_ref, wre_ref, bre_ref, wg_ref, wu_ref,
                wd_ref, o_ref, hbuf, sem, *, tile):
    i = pl.program_id(0)
    n_tiles = pl.num_programs(0)

    def start(tile_idx, r):
        slot = tile_idx % (GATHER_AHEAD + 1)
        _row_copy(h_hbm, hbuf.at[slot], src_ref[tile_idx * tile + r], r, sem.at[slot]).start()

    for first in range(GATHER_AHEAD):
        @pl.when(jnp.logical_and(i == 0, on_ref[first] == 1))
        def _():
            def body(r, c):
                start(first, r)
                return c

            lax.fori_loop(0, tile, body, 0, unroll=32)

    def run(fetch_ahead):
        def issue(chunk, n_chunks):
            if fetch_ahead:
                per = tile // n_chunks
                for r in range(chunk * per, (chunk + 1) * per):
                    start(i + GATHER_AHEAD, r)

        slot = i % (GATHER_AHEAD + 1)
        _wait_tile(h_hbm, hbuf.at[slot], sem.at[slot])
        h2_hi, h2_lo = _split_bf16(_load_token_major(hbuf.at[slot], tile))

        logits = _router_logits(jnp.concatenate([wrg_ref[...], wre_ref[0]], axis=0),
                                jnp.concatenate([brg_ref[...], bre_ref[0]], axis=0), h2_hi, h2_lo)
        g = _rows(logits, N_GROUPS)
        gmax = functools.reduce(jnp.maximum, g)
        p_top = 1.0 / functools.reduce(lambda p, q: p + q, [jnp.exp(gj - gmax) for gj in g])
        e = _rows(logits[ROUTER_ROWS:], EXPERTS_PER_GROUP)
        m1 = functools.reduce(jnp.maximum, e)
        i1 = _first_argmax(e, m1)
        rest = [jnp.where(i1 == j, -jnp.inf, e[j]) for j in range(EXPERTS_PER_GROUP)]
        m2 = functools.reduce(jnp.maximum, rest)
        i2 = _first_argmax(rest, m2)
        t = jnp.exp(m2 - m1)
        w1 = p_top / (1.0 + t)
        w2 = p_top * t / (1.0 + t)
        row = lax.broadcasted_iota(jnp.int32, (LANES, tile), 0)
        comb = (jnp.where(row == i1, w1, 0.0) + jnp.where(row == i2, w2, 0.0)).T

        gates, ups = [], []
        for j in range(EXPERTS_PER_GROUP):
            gates.append(_dot(h2_hi, wg_ref[j]))
            issue(2 * j, 2 * EXPERTS_PER_GROUP)
            ups.append(_dot(h2_hi, wu_ref[j]))
            issue(2 * j + 1, 2 * EXPERTS_PER_GROUP)
        acts = [((a * _sigmoid(a)) * u * comb[:, j:j + 1]).astype(BF16)
                for j, (a, u) in enumerate(zip(gates, ups))]
        acc = functools.reduce(lambda p, q: p + q,
                               [_dot(acts[j], wd_ref[j]) for j in range(EXPERTS_PER_GROUP)])
        _store_token_major(o_ref, acc, tile)

    @pl.when(on_ref[i] == 0)
    def _():
        o_ref[...] = jnp.zeros_like(o_ref)

    ahead = jnp.minimum(i + GATHER_AHEAD, n_tiles - 1)
    fetch = jnp.logical_and(i + GATHER_AHEAD < n_tiles, on_ref[ahead] == 1)

    @pl.when(jnp.logical_and(on_ref[i] == 1, fetch))
    def _():
        run(True)

    @pl.when(jnp.logical_and(on_ref[i] == 1, jnp.logical_not(fetch)))
    def _():
        run(False)


def _moe(h2tm, src, maps, wts, w_gate, w_up, w_down, tile):
    n_tiles = src.shape[0] // tile
    const = lambda shape: pl.BlockSpec(shape, lambda i, grp, on, src: (0,) * len(shape))
    by_group = lambda shape: pl.BlockSpec(
        shape, lambda i, grp, on, src: (grp[i],) + (0,) * (len(shape) - 1))
    return pl.pallas_call(
        functools.partial(_moe_kernel, tile=tile),
        grid_spec=pltpu.PrefetchScalarGridSpec(
            num_scalar_prefetch=3,
            grid=(n_tiles,),
            in_specs=[pl.BlockSpec(memory_space=pl.ANY),
                      const((ROUTER_ROWS, D_MODEL)), const((ROUTER_ROWS, 1)),
                      by_group((1, ROUTER_ROWS, D_MODEL)), by_group((1, ROUTER_ROWS, 1)),
                      by_group((EXPERTS_PER_GROUP, D_MODEL, D_EXPERT)),
                      by_group((EXPERTS_PER_GROUP, D_MODEL, D_EXPERT)),
                      by_group((EXPERTS_PER_GROUP, D_EXPERT, D_MODEL))],
            out_specs=pl.BlockSpec((tile * TOKEN_ROWS, LANES), lambda i, grp, on, src: (i, 0)),
            scratch_shapes=[pltpu.VMEM((GATHER_AHEAD + 1, tile * TOKEN_ROWS, LANES), F32),
                            pltpu.SemaphoreType.DMA((GATHER_AHEAD + 1,))]),
        out_shape=jax.ShapeDtypeStruct((n_tiles * tile * TOKEN_ROWS, LANES), F32),
        compiler_params=_cparams(("arbitrary",)),
        name="moe",
    )(*maps, src, h2tm, wts["w_rg"], wts["b_rg"], wts["w_re"], wts["b_re"], w_gate, w_up, w_down)


def _invert_kernel(pos_ref, lo_ref, hi_ref, src_ref, *, n, n_fill):
    def zero(p, c):
        src_ref[p] = 0
        return c

    for s in range(n_fill):
        lax.fori_loop(lo_ref[s], hi_ref[s], zero, 0)

    def put(t, c):
        src_ref[pos_ref[t]] = t
        return c

    lax.fori_loop(0, n, put, 0, unroll=8)


def _invert(pos, fill_lo, fill_hi, n_sorted):
    return pl.pallas_call(
        functools.partial(_invert_kernel, n=pos.shape[0], n_fill=fill_lo.shape[0]),
        grid_spec=pltpu.PrefetchScalarGridSpec(
            num_scalar_prefetch=3, grid=(1,), in_specs=[],
            out_specs=pl.BlockSpec(memory_space=pltpu.SMEM)),
        out_shape=jax.ShapeDtypeStruct((n_sorted,), jnp.int32),
        name="moe_invert",
    )(pos, fill_lo, fill_hi)


def _sort_plan(grp, rank, cnt, tile, n_tiles):
    n_tile_grp = (cnt + tile - 1) // tile
    tile_end = jnp.cumsum(n_tile_grp)
    tile_start = tile_end - n_tile_grp
    total = tile_end[-1]
    pos = ((tile_start * tile)[grp] + rank).astype(jnp.int32)
    fill_lo = jnp.concatenate([tile_start * tile + cnt, total[None] * tile]).astype(jnp.int32)
    fill_hi = jnp.concatenate([tile_end * tile, jnp.full((1,), n_tiles * tile)]).astype(jnp.int32)
    src = _invert(pos, fill_lo, fill_hi, n_tiles * tile)
    i = jnp.arange(n_tiles, dtype=jnp.int32)
    tile_grp = jnp.sum((jnp.minimum(i, total - 1)[:, None] >= tile_end[None, :]).astype(jnp.int32),
                       axis=1)
    return pos, src, (tile_grp.astype(jnp.int32), (i < total).astype(jnp.int32))


def _sparse_moe(x1, h2tm, grp, rank, cnt8, mod3, mod_row_fn, wts, w_gate, w_up, w_down, tile,
                n_first_tokens):
    n = x1.shape[0]
    n_tiles = n // tile + N_GROUPS
    pos, src, maps = _sort_plan(grp.reshape(n), rank.reshape(n), cnt8[:N_GROUPS, 0], tile, n_tiles)
    y_sorted = _moe(h2tm, src, maps, wts, w_gate, w_up, w_down, tile)
    return _combine(y_sorted, pos, x1, mod3, mod_row_fn, n_first_tokens)


def _pack_weights(l, norm1_g, q_a_norm_g, w_q_b, kv_a_norm_g, w_kv_b, q_norm_g, k_norm_g,
                  w_attn_o, w_fnet, w_out, norm2_g, w_router_group, b_router_group,
                  w_router_expert, b_router_expert):
    w_qb = jnp.pad(w_q_b[l].reshape(Q_LORA, N_HEADS, QK_HEAD),
                   ((0, 0), (0, 0), (0, HEAD_SLOT - QK_HEAD))).reshape(Q_LORA, QK_WIDTH)
    wkv = w_kv_b[l].reshape(KV_LORA, N_HEADS, QK_NOPE + V_HEAD)
    w_kb = jnp.pad(wkv[:, :, :QK_NOPE],
                   ((0, 0), (0, 0), (0, HEAD_SLOT - QK_NOPE))).reshape(KV_LORA, QK_WIDTH)
    w_vb = wkv[:, :, QK_NOPE:].reshape(KV_LORA, V_WIDTH)
    pad_g = lambda g: jnp.pad(g, (0, HEAD_SLOT - QK_HEAD)).reshape(1, HEAD_SLOT)
    w_rg = jnp.pad(w_router_group[l].T, ((0, ROUTER_ROWS - N_GROUPS), (0, 0)))
    b_rg = jnp.pad(b_router_group[l], (0, ROUTER_ROWS - N_GROUPS)).reshape(ROUTER_ROWS, 1)
    w_re = jnp.pad(w_router_expert[l].T.reshape(N_GROUPS, EXPERTS_PER_GROUP, D_MODEL),
                   ((0, 0), (0, ROUTER_ROWS - EXPERTS_PER_GROUP), (0, 0)))
    b_re = jnp.pad(b_router_expert[l].reshape(N_GROUPS, EXPERTS_PER_GROUP),
                   ((0, 0), (0, ROUTER_ROWS - EXPERTS_PER_GROUP))).reshape(N_GROUPS, ROUTER_ROWS, 1)
    dft_c, dft_ns = _dft_tables(FN_GROUP_W)
    return {
        "g1": norm1_g[l].reshape(1, D_MODEL),
        "qag": q_a_norm_g[l].reshape(1, Q_LORA),
        "w_qb": w_qb.astype(BF16),
        "kvg": kv_a_norm_g[l].reshape(1, KV_LORA),
        "w_kb": w_kb.astype(BF16),
        "w_vb": w_vb.T.astype(BF16),
        "qg": pad_g(q_norm_g[l]),
        "kg": pad_g(k_norm_g[l]),
        "dft_c": jnp.concatenate([jnp.asarray(dft_c), -jnp.asarray(dft_ns)], axis=1).astype(BF16),
        "w_ao": w_attn_o[l].astype(BF16),
        "w_fn": w_fnet[l].astype(BF16),
        "w_out": w_out[l].astype(BF16),
        "g2": norm2_g[l].reshape(1, D_MODEL),
        "w_rg": w_rg,
        "b_rg": b_rg,
        "w_re": w_re,
        "b_re": b_re,
    }


def _layer(xp, xs, cache_ckv_l, cache_kpe_l, mod3, wts, experts):
    bp, sp, _ = xp.shape
    bs, ss, _ = xs.shape
    ctx_row = lambda i: 0
    lat_row = lambda i: 1 + i // (ss // TM)

    xp2 = xp.reshape(bp * sp, D_MODEL)
    ride = (bp * sp) // TM == N_EXPERTS
    cs, ns = (jnp.asarray(t).astype(BF16) for t in _dft_tables(sp))
    outs = _inproj(xp2, mod3, ctx_row, wts, None, True, experts if ride else (), (sp, cs, ns))
    attn, fm, sga, sgf, ckv, kpe = outs[:6]
    w_gate, w_up, w_down = outs[6:] if ride else (w.astype(BF16) for w in experts)
    ctx_set = (xp2, attn, fm, sga, sgf)

    xs2 = xs.reshape(bs * ss, D_MODEL)
    rope_tabs = tuple(jnp.asarray(t) for t in _rope_tables(ss))
    q, k, v, fcs, sga, sgf = _inproj(xs2, mod3, lat_row, wts, rope_tabs, False)
    kpe_slot = jnp.pad(cache_kpe_l, ((0, 0), (0, 0), (QK_NOPE, LANES - QK_HEAD)))
    cs, ns = (jnp.asarray(t).astype(BF16) for t in _dft_tables(ss))
    attn, fm = _attention(q.reshape(bs, ss, QK_WIDTH), k.reshape(bs, ss, QK_WIDTH), v, cache_ckv_l,
                          kpe_slot, wts, fcs.reshape(bs, ss, 2 * FN_WIDTH), cs, ns, TQ_LAT, "attn_lat")
    lat_set = (xs2, attn.reshape(bs * ss, V_WIDTH), fm.reshape(bs * ss, FN_WIDTH), sga, sgf)

    n_ctx_tiles = (bp * sp) // TM
    all_row = lambda i: jnp.where(i < n_ctx_tiles, 0, lat_row(i - n_ctx_tiles))
    x1, h2tm, grp, rank, cnt = _merge(ctx_set, lat_set, mod3, all_row, wts)
    yp, ys = _sparse_moe(x1, h2tm, grp, rank, cnt, mod3, all_row, wts, w_gate, w_up, w_down, MOE_TILE,
                         bp * sp)
    yp, ys = yp.reshape(bp, sp, D_MODEL), ys.reshape(bs, ss, D_MODEL)

    return yp, ys, ckv.reshape(bp, sp, KV_LORA), jnp.swapaxes(kpe, 1, 2)


def kernel(x_prompt, x_sample, cache_ckv, cache_kpe, c, c_ctx, w_mod, b_mod, norm1_g, w_in, q_a_norm_g, w_q_b, kv_a_norm_g, w_kv_b, q_norm_g, k_norm_g, w_attn_o, w_fnet, w_out, norm2_g, w_router_group, b_router_group, w_router_expert, b_router_expert, w_exp_gate, w_exp_up, w_exp_down):
    depth = w_mod.shape[0]
    n_lat = c.shape[0]
    assert 1 + n_lat <= MOD_ROWS
    cond8 = jnp.concatenate([c_ctx[None, :], c, jnp.zeros((MOD_ROWS - 1 - n_lat, D_MODEL), F32)], axis=0)
    xp, xs = x_prompt, x_sample
    ckv_layers, kpe_layers = [], []
    for l in range(depth):
        mod, w_in_p = _adaln(cond8, w_mod[l], b_mod[l].reshape(1, -1), jnp.swapaxes(w_in[l], 0, 1))
        mod3 = mod.reshape(MOD_ROWS, 6, D_MODEL)
        wts = _pack_weights(l, norm1_g, q_a_norm_g, w_q_b, kv_a_norm_g, w_kv_b, q_norm_g,
                            k_norm_g, w_attn_o, w_fnet, w_out, norm2_g, w_router_group,
                            b_router_group, w_router_expert, b_router_expert)
        wts["w_in"] = w_in_p
        xp, xs, ckv, kpe = _layer(xp, xs, cache_ckv[:, l], cache_kpe[:, l], mod3, wts,
                                  (w_exp_gate[l], w_exp_up[l], w_exp_down[l]))
        ckv_layers.append(ckv)
        kpe_layers.append(kpe)
    return xp, xs, jnp.stack(ckv_layers, axis=1), jnp.stack(kpe_layers, axis=1)
```

```python
import functools
import math

import numpy as np
import jax
import jax.numpy as jnp
from jax import lax
from jax.experimental import pallas as pl
from jax.experimental.pallas import tpu as pltpu

D_MODEL = 1024
GRID_W = 64
N_HEADS = 8
Q_LORA = 512
KV_LORA = 256
QK_NOPE = 64
QK_ROPE = 32
V_HEAD = 64
QK_HEAD = QK_NOPE + QK_ROPE
ATTN_SCALE = QK_HEAD ** -0.5
ROPE_BASE = 10000.0
FN_GROUPS = 4
FN_GROUP_W = 128
FN_WIDTH = FN_GROUPS * FN_GROUP_W
N_GROUPS = 4
EXPERTS_PER_GROUP = 4
N_EXPERTS = N_GROUPS * EXPERTS_PER_GROUP
D_EXPERT = 512
EPS = 1e-6

LANES = 128
HEAD_SLOT = LANES
QK_WIDTH = N_HEADS * HEAD_SLOT
V_WIDTH = N_HEADS * V_HEAD
C_QA = 0
C_KVA = C_QA + Q_LORA
C_KPE = C_KVA + KV_LORA
C_FN = C_KPE + LANES
C_GA = C_FN + FN_WIDTH
C_GF = C_GA + D_MODEL
IN_PACKED = C_GF + D_MODEL
SUBLANES = 8
ROUTER_ROWS = SUBLANES
MOD_ROWS = SUBLANES
VMEM_LIMIT = 56 * 1024 * 1024

TM = 512
TQ_LAT = 256
SEG_ROWS = SUBLANES
TOKEN_ROWS = D_MODEL // LANES
MOE_TILE = 256
MERGE_SPLIT = 2
DMA_QUEUES = 2
GATHER_AHEAD = 2
ADALN_STEPS = 8

BF16 = jnp.bfloat16
F32 = jnp.float32


def _cparams(sem):
    return pltpu.CompilerParams(dimension_semantics=sem, vmem_limit_bytes=VMEM_LIMIT)


def _dot(a, b):
    return jnp.dot(a, b, preferred_element_type=F32)


def _dot_nt(a, b):
    return lax.dot_general(a, b, (((1,), (1,)), ((), ())), preferred_element_type=F32)


def _sigmoid(x):
    return 1.0 / (1.0 + jnp.exp(-x))


def _split_bf16(x):
    hi = x.astype(BF16)
    return hi, (x - hi.astype(F32)).astype(BF16)


@functools.lru_cache(maxsize=None)
def _rope_tables(n_pos):
    half = QK_ROPE // 2
    quarter = half // 2
    freqs = ROPE_BASE ** (-np.arange(quarter, dtype=np.float64) / quarter)
    pos = np.arange(n_pos)
    row = (pos // GRID_W).astype(np.float64)
    col = (pos % GRID_W).astype(np.float64)
    cos_t = np.ones((n_pos, LANES), np.float64)
    sin_a = np.zeros((n_pos, LANES), np.float64)
    sin_b = np.zeros((n_pos, LANES), np.float64)
    for base, p in ((QK_NOPE, row), (QK_NOPE + half, col)):
        ang = p[:, None] * freqs[None, :]
        cos_t[:, base:base + quarter] = np.cos(ang)
        cos_t[:, base + quarter:base + half] = np.cos(ang)
        sin_a[:, base:base + quarter] = -np.sin(ang)
        sin_b[:, base + quarter:base + half] = np.sin(ang)
    return (cos_t.astype(np.float32), sin_a.astype(np.float32), sin_b.astype(np.float32))


@functools.lru_cache(maxsize=None)
def _dft_tables(n):
    k = np.arange(n)
    ang = 2.0 * np.pi * ((k[:, None] * k[None, :]) % n) / n
    s = 1.0 / math.sqrt(n)
    return (np.cos(ang) * s).astype(np.float32), (-np.sin(ang) * s).astype(np.float32)


def _adaln_kernel(cond_ref, w_ref, b_ref, win_ref, o_ref, winp_ref):
    c = cond_ref[...]
    s_hi, s_lo = _split_bf16(c * _sigmoid(c))
    w_hi, w_lo = _split_bf16(w_ref[...])
    y = _dot(jnp.concatenate([s_hi, s_lo], axis=0), w_hi)
    o_ref[...] = (y[:MOD_ROWS] + y[MOD_ROWS:]) + _dot(s_hi, w_lo) + b_ref[...]

    winp_ref[:C_KPE, :] = win_ref[:C_KPE, :].astype(BF16)
    winp_ref[C_KPE:C_FN, :] = jnp.zeros((LANES, winp_ref.shape[1]), BF16)
    winp_ref[C_KPE + QK_NOPE:C_KPE + QK_HEAD, :] = win_ref[C_KPE:C_KPE + QK_ROPE, :].astype(BF16)
    winp_ref[C_FN:, :] = win_ref[C_KPE + QK_ROPE:, :].astype(BF16)


def _adaln(cond8, w_mod, b_mod, w_in_t):
    n = w_mod.shape[1]
    steps = ADALN_STEPS
    tn, tr = n // steps, D_MODEL // steps
    return pl.pallas_call(
        _adaln_kernel,
        grid=(steps,),
        in_specs=[pl.BlockSpec((MOD_ROWS, D_MODEL), lambda j: (0, 0)),
                  pl.BlockSpec((D_MODEL, tn), lambda j: (0, j)),
                  pl.BlockSpec((1, tn), lambda j: (0, j)),
                  pl.BlockSpec((w_in_t.shape[0], tr), lambda j: (0, j))],
        out_specs=[pl.BlockSpec((MOD_ROWS, tn), lambda j: (0, j)),
                   pl.BlockSpec((IN_PACKED, tr), lambda j: (0, j))],
        out_shape=[jax.ShapeDtypeStruct((MOD_ROWS, n), F32),
                   jax.ShapeDtypeStruct((IN_PACKED, D_MODEL), BF16)],
        compiler_params=_cparams(("arbitrary",)),
        name="adaln",
    )(cond8, w_mod, b_mod, w_in_t)


def _rms(x, width):
    return lax.rsqrt(jnp.sum(x * x, axis=-1, keepdims=True) * (1.0 / width) + EPS)


def _rope(x, cos_t, sin_a, sin_b):
    return x * cos_t + pltpu.roll(x, LANES - 8, 1) * sin_a + pltpu.roll(x, 8, 1) * sin_b


def _inproj_kernel(*refs, rope, emit_cache, n_cast, seq_len):
    it = iter(refs)
    x_ref, mod_ref, g1_ref, win_ref, qag_ref, wqb_ref, kvg_ref, wkb_ref, wvb_ref = (
        next(it) for _ in range(9))
    qg_ref, kg_ref, dft_ref = next(it), next(it), next(it)
    if rope:
        cos_ref, sa_ref, sb_ref = next(it), next(it), next(it)
    if seq_len:
        seq_cs_ref, seq_ns_ref = next(it), next(it)
    cast_in = [next(it) for _ in range(n_cast)]
    if seq_len:
        attn_ref, fm_ref, sga_ref, sgf_ref = (next(it) for _ in range(4))
    else:
        q_ref, k_ref, v_ref, fcs_ref, sga_ref, sgf_ref = (next(it) for _ in range(6))
    if emit_cache:
        ckv_ref, kpe_ref = next(it), next(it)
    cast_out = [next(it) for _ in range(n_cast)]
    if seq_len:
        q_ref, k_ref, v_ref, fcs_ref = (next(it) for _ in range(4))

    for src, dst in zip(cast_in, cast_out):
        dst[...] = src[...].astype(BF16)

    x = x_ref[...]
    shift = mod_ref[0, 0:1, :]
    scale = mod_ref[0, 1:2, :]
    h = (x * _rms(x, D_MODEL) * g1_ref[...]) * (1.0 + scale) + shift
    hb = h.astype(BF16)

    if rope:
        cos_t, sin_a, sin_b = cos_ref[...], sa_ref[...], sb_ref[...]

    qa = _dot_nt(hb, win_ref[C_QA:C_QA + Q_LORA, :])
    qn = (qa * _rms(qa, Q_LORA) * qag_ref[...]).astype(BF16)
    q = _dot(qn, wqb_ref[...])
    qg = qg_ref[...] * ATTN_SCALE
    for hd in range(N_HEADS):
        qh = q[:, hd * HEAD_SLOT:(hd + 1) * HEAD_SLOT]
        qh = qh * _rms(qh, QK_HEAD) * qg
        if rope:
            qh = _rope(qh, cos_t, sin_a, sin_b)
        q_ref[:, hd * HEAD_SLOT:(hd + 1) * HEAD_SLOT] = qh.astype(BF16)

    kva = _dot_nt(hb, win_ref[C_KVA:C_KVA + KV_LORA, :])
    ckv = kva * _rms(kva, KV_LORA) * kvg_ref[...]
    kpe = _dot_nt(hb, win_ref[C_KPE:C_KPE + LANES, :])
    if emit_cache:
        ckv_ref[...] = ckv
        kpe_t = kpe.T[QK_NOPE:QK_NOPE + QK_ROPE, :]
        for j in range(x.shape[0] // seq_len):
            kpe_ref[j] = kpe_t[:, j * seq_len:(j + 1) * seq_len]
    _emit_kv(ckv.astype(BF16), kpe, wkb_ref, wvb_ref, kg_ref,
             (cos_t, sin_a, sin_b) if rope else None, k_ref, v_ref)

    fn = _dot_nt(hb, win_ref[C_FN:C_FN + FN_WIDTH, :]).astype(BF16)
    for g in range(FN_GROUPS):
        cs = _dot(fn[:, g * FN_GROUP_W:(g + 1) * FN_GROUP_W], dft_ref[...])
        fcs_ref[:, g * FN_GROUP_W:(g + 1) * FN_GROUP_W] = cs[:, :FN_GROUP_W].astype(BF16)
        fcs_ref[:, FN_WIDTH + g * FN_GROUP_W:FN_WIDTH + (g + 1) * FN_GROUP_W] = (
            cs[:, FN_GROUP_W:].astype(BF16))

    def gate(ref, c0, half):
        def run():
            w = D_MODEL // 2
            ref[:, pl.ds(half * w, w)] = _sigmoid(
                _dot_nt(hb, win_ref[pl.ds(c0 + half * w, w), :])).astype(BF16)
        return run

    gates = [gate(ref, c0, half) for ref, c0 in ((sga_ref, C_GA), (sgf_ref, C_GF)) for half in range(2)]

    if seq_len:
        n_seq = x.shape[0] // seq_len
        for j in range(n_seq):
            rows = pl.ds(j * seq_len, seq_len)
            fill = gates[j * len(gates) // n_seq:(j + 1) * len(gates) // n_seq]
            attn_ref[rows, :] = _attend(q_ref.at[rows], [k_ref.at[rows]], [v_ref.at[:, rows]], fill)
            fm_ref[rows, :] = (_dot(seq_cs_ref[...], fcs_ref[rows, :FN_WIDTH])
                               + _dot(seq_ns_ref[...], fcs_ref[rows, FN_WIDTH:])).astype(BF16)
    else:
        for run_gate in gates:
            run_gate()


def _emit_kv(ckvb, kpe, wkb_ref, wvb_ref, kg_ref, rope_tabs, k_ref, v_ref):
    kg = kg_ref[...]
    v_ref[...] = _dot_nt(wvb_ref[...], ckvb).astype(BF16)
    kn = _dot(ckvb, wkb_ref[...])
    pe_ss = jnp.sum(kpe * kpe, axis=-1, keepdims=True)
    pe_g = kpe * kg
    if rope_tabs is not None:
        pe_g = _rope(pe_g, *rope_tabs)
    for hd in range(N_HEADS):
        knh = kn[:, hd * HEAD_SLOT:(hd + 1) * HEAD_SLOT]
        ss = jnp.sum(knh * knh, axis=-1, keepdims=True) + pe_ss
        r = lax.rsqrt(ss * (1.0 / QK_HEAD) + EPS)
        k_ref[:, hd * HEAD_SLOT:(hd + 1) * HEAD_SLOT] = ((knh * kg + pe_g) * r).astype(BF16)


def _const_spec(shape):
    return pl.BlockSpec(shape, lambda i: (0,) * len(shape))


def _inproj(x2d, mod3, mod_row_fn, wts, rope_tabs, emit_cache, cast=(), seq=None):
    n = x2d.shape[0]
    assert all(w.shape[0] == n // TM for w in cast)
    assert seq is None or TM % seq[0] == 0
    rope = rope_tabs is not None
    tiles_per_seq = None if not rope else rope_tabs[0].shape[0] // TM
    in_specs = [pl.BlockSpec((TM, D_MODEL), lambda i: (i, 0)),
                pl.BlockSpec((1, 6, D_MODEL), lambda i: (mod_row_fn(i), 0, 0)),
                _const_spec((1, D_MODEL)),
                _const_spec((IN_PACKED, D_MODEL)),
                _const_spec((1, Q_LORA)),
                _const_spec((Q_LORA, QK_WIDTH)),
                _const_spec((1, KV_LORA)),
                _const_spec((KV_LORA, QK_WIDTH)),
                _const_spec((V_WIDTH, KV_LORA)),
                _const_spec((1, HEAD_SLOT)),
                _const_spec((1, HEAD_SLOT)),
                _const_spec((FN_GROUP_W, 2 * FN_GROUP_W))]
    args = [x2d, mod3, wts["g1"], wts["w_in"], wts["qag"], wts["w_qb"], wts["kvg"],
            wts["w_kb"], wts["w_vb"], wts["qg"], wts["kg"], wts["dft_c"]]
    if rope:
        in_specs += [pl.BlockSpec((TM, LANES), lambda i: (i % tiles_per_seq, 0))] * 3
        args += list(rope_tabs)
    if seq is not None:
        in_specs += [_const_spec((seq[0], seq[0]))] * 2
        args += [seq[1], seq[2]]
    cast_specs = [pl.BlockSpec((1,) + w.shape[1:], lambda i: (i, 0, 0)) for w in cast]
    in_specs += cast_specs
    args += list(cast)
    qkvf = [((TM, QK_WIDTH), (n, QK_WIDTH)), ((TM, QK_WIDTH), (n, QK_WIDTH)),
            ((V_WIDTH, TM), (V_WIDTH, n)), ((TM, 2 * FN_WIDTH), (n, 2 * FN_WIDTH))]
    if seq is None:
        widths = [None] * 4 + [D_MODEL, D_MODEL]
    else:
        widths = [V_WIDTH, FN_WIDTH, D_MODEL, D_MODEL]
    out_shape, out_specs = [], []
    for j, w in enumerate(widths):
        if w is None:
            out_shape.append(jax.ShapeDtypeStruct(qkvf[j][1], BF16))
            out_specs.append(pl.BlockSpec(qkvf[j][0], (lambda i: (0, i)) if j == 2 else (lambda i: (i, 0))))
        else:
            out_shape.append(jax.ShapeDtypeStruct((n, w), BF16))
            out_specs.append(pl.BlockSpec((TM, w), lambda i: (i, 0)))
    if emit_cache:
        assert seq is not None
        out_shape += [jax.ShapeDtypeStruct((n, KV_LORA), F32),
                      jax.ShapeDtypeStruct((n // seq[0], QK_ROPE, seq[0]), F32)]
        out_specs += [pl.BlockSpec((TM, KV_LORA), lambda i: (i, 0)),
                      pl.BlockSpec((TM // seq[0], QK_ROPE, seq[0]), lambda i: (i, 0, 0))]
    out_shape += [jax.ShapeDtypeStruct(w.shape, BF16) for w in cast]
    out_specs += cast_specs
    scratch = [] if seq is None else [pltpu.VMEM(blk, BF16) for blk, _ in qkvf]
    return pl.pallas_call(
        functools.partial(_inproj_kernel, rope=rope, emit_cache=emit_cache, n_cast=len(cast),
                          seq_len=None if seq is None else seq[0]),
        grid=(n // TM,),
        in_specs=in_specs,
        out_specs=out_specs,
        out_shape=out_shape,
        scratch_shapes=scratch,
        compiler_params=_cparams(("parallel",)),
        name="inproj_lat" if rope else "inproj_ctx",
    )(*args)


def _attend(q, ks, vts, fillers=()):
    head = lambda hd: slice(hd * HEAD_SLOT, (hd + 1) * HEAD_SLOT)
    st = [jnp.stack([_dot_nt(k[:, head(hd)], q[:, head(hd)]) for hd in range(N_HEADS)]) for k in ks]
    m = functools.reduce(jnp.maximum, [sj.max(axis=1, keepdims=True) for sj in st])
    fillers = list(fillers)
    every = N_HEADS // max(1, len(fillers))
    p = []
    for hd in range(N_HEADS):
        p.append([jnp.exp(sj[hd] - m[hd]) for sj in st])
        if fillers and hd % every == 0:
            fillers.pop(0)()
    assert not fillers
    outs = []
    for hd in range(N_HEADS):
        l = functools.reduce(lambda a, b: a + b, [pj.sum(axis=0, keepdims=True) for pj in p[hd]])
        o = functools.reduce(lambda a, b: a + b,
                             [_dot(vt[hd * V_HEAD:(hd + 1) * V_HEAD, :], pj.astype(BF16))
                              for vt, pj in zip(vts, p[hd])])
        outs.append(o / l)
    return jnp.concatenate(outs, axis=0).T.astype(BF16)


def _attn_kernel(q_ref, k_ref, vt_ref, cckv_ref, ckpe_ref, wkb_ref, wvb_ref, kg_ref, cs_ref, ns_ref,
                 f_ref, o_ref, fm_ref, kc_ref, vtc_ref):
    @pl.when(pl.program_id(1) == 0)
    def _():
        _emit_kv(cckv_ref[0].astype(BF16), ckpe_ref[0], wkb_ref, wvb_ref, kg_ref, None, kc_ref, vtc_ref)

    def dft(c0, width):
        def run():
            fm_ref[0, :, pl.ds(c0, width)] = (
                _dot(cs_ref[...], f_ref[0, :, pl.ds(c0, width)])
                + _dot(ns_ref[...], f_ref[0, :, pl.ds(FN_WIDTH + c0, width)])).astype(BF16)
        return run

    half = FN_WIDTH // 2
    o_ref[0] = _attend(q_ref.at[0], [kc_ref, k_ref.at[0]], [vtc_ref, vt_ref],
                       [dft(0, half), dft(half, half)])


def _attention(q3, k3, vt, cache_ckv, cache_kpe_slot, wts, fcs3, cs, ns, tq, name):
    b, sq, _ = q3.shape
    past = cache_ckv.shape[1]
    seq = lambda shape: pl.BlockSpec(shape, lambda bi, qi: (bi, 0, 0))
    const = lambda shape: pl.BlockSpec(shape, lambda bi, qi: (0,) * len(shape))
    return pl.pallas_call(
        _attn_kernel,
        grid=(b, sq // tq),
        in_specs=[pl.BlockSpec((1, tq, QK_WIDTH), lambda bi, qi: (bi, qi, 0)),
                  seq((1, sq, QK_WIDTH)),
                  pl.BlockSpec((V_WIDTH, sq), lambda bi, qi: (0, bi)),
                  seq((1, past, KV_LORA)), seq((1, past, LANES)),
                  const((KV_LORA, QK_WIDTH)), const((V_WIDTH, KV_LORA)), const((1, HEAD_SLOT)),
                  pl.BlockSpec((tq, sq), lambda bi, qi: (qi, 0)),
                  pl.BlockSpec((tq, sq), lambda bi, qi: (qi, 0)),
                  seq((1, sq, 2 * FN_WIDTH))],
        out_specs=[pl.BlockSpec((1, tq, V_WIDTH), lambda bi, qi: (bi, qi, 0)),
                   pl.BlockSpec((1, tq, FN_WIDTH), lambda bi, qi: (bi, qi, 0))],
        out_shape=[jax.ShapeDtypeStruct((b, sq, V_WIDTH), BF16),
                   jax.ShapeDtypeStruct((b, sq, FN_WIDTH), BF16)],
        scratch_shapes=[pltpu.VMEM((past, QK_WIDTH), BF16), pltpu.VMEM((V_WIDTH, past), BF16)],
        compiler_params=_cparams(("parallel", "arbitrary")),
        name=name,
    )(q3, k3, vt, cache_ckv, cache_kpe_slot, wts["w_kb"], wts["w_vb"], wts["kg"], cs, ns, fcs3)


def _router_logits(w, b, h_hi, h_lo):
    rows = w.shape[0]
    w_hi, w_lo = _split_bf16(w)
    y = _dot_nt(jnp.concatenate([w_hi, w_lo], axis=0), h_hi)
    return (y[:rows] + y[rows:]) + _dot_nt(w_hi, h_lo) + b


def _rows(x, n):
    return [x[j:j + 1, :] for j in range(n)]


def _first_argmax(rows, top):
    idx = jnp.full(top.shape, len(rows) - 1, jnp.int32)
    for j in range(len(rows) - 2, -1, -1):
        idx = jnp.where(rows[j] == top, j, idx)
    return idx


def _store_token_major(ref, x, tm, first=0):
    for s in range(TOKEN_ROWS):
        ref[pl.ds(first * TOKEN_ROWS + s, tm, stride=TOKEN_ROWS), :] = x[:, s * LANES:(s + 1) * LANES]


def _load_token_major(ref, tm):
    return jnp.concatenate([ref[pl.ds(s, tm, stride=TOKEN_ROWS), :] for s in range(TOKEN_ROWS)],
                           axis=1)


def _merge_kernel(*refs, n_first):
    first, second = refs[0:5], refs[5:10]
    (mod_ref, wao_ref, wfn_ref, wout_ref, g2_ref, wrg_ref, brg_ref, x1_ref, h2_ref, grp_ref, rank_ref,
     cnt_ref, carry_ref) = refs[10:]
    i = pl.program_id(0)

    @pl.when(i == 0)
    def _():
        carry_ref[...] = jnp.zeros_like(carry_ref)

    def tile(src):
        x_ref, attn_ref, fm_ref, sga_ref, sgf_ref = src
        hm = x_ref.shape[0] // MERGE_SPLIT
        rows = [pl.ds(k * hm, hm) for k in range(MERGE_SPLIT)]
        live = [dict() for _ in range(MERGE_SPLIT)]

        def branches(k):
            live[k]["a"] = _dot(attn_ref[rows[k], :], wao_ref[...])
            live[k]["f"] = _dot(fm_ref[rows[k], :], wfn_ref[...])

        def gated(k):
            u = (sga_ref[rows[k], :].astype(F32) * live[k].pop("a")
                 + sgf_ref[rows[k], :].astype(F32) * live[k].pop("f"))
            live[k]["u"] = u.astype(BF16)

        def project(k):
            live[k]["y"] = _dot(live[k].pop("u"), wout_ref[...])

        def residual(k):
            x1 = x_ref[rows[k], :] + mod_ref[0, 2:3, :] * live[k].pop("y")
            x1_ref[rows[k], :] = x1
            h2 = (x1 * _rms(x1, D_MODEL) * g2_ref[...]) * (1.0 + mod_ref[0, 4:5, :]) + mod_ref[0, 3:4, :]
            _store_token_major(h2_ref, h2, hm, k * hm)
            live[k]["h2"] = _split_bf16(h2)

        def route(k):
            g = _rows(_router_logits(wrg_ref[...], brg_ref[...], *live[k].pop("h2")), N_GROUPS)
            gidx = _first_argmax(g, functools.reduce(jnp.maximum, g))
            onehot = jnp.where(lax.broadcasted_iota(jnp.int32, (SEG_ROWS, hm), 0) == gidx, 1.0, 0.0)
            before = (lax.broadcasted_iota(jnp.int32, (hm, hm), 0)
                      < lax.broadcasted_iota(jnp.int32, (hm, hm), 1))
            prefix = _dot(onehot.astype(BF16), jnp.where(before, 1.0, 0.0).astype(BF16))
            carry = carry_ref[...]
            rank = jnp.sum(onehot * (prefix + carry[:, 0:1]), axis=0, keepdims=True)
            grp_ref[:, rows[k]] = gidx
            rank_ref[:, rows[k]] = rank.astype(jnp.int32)
            carry_ref[...] = carry + jnp.sum(onehot, axis=1, keepdims=True)

        phases = [branches, gated, project, residual, route]
        for t in range(len(phases) + MERGE_SPLIT - 1):
            for k in range(MERGE_SPLIT):
                if 0 <= t - k < len(phases):
                    phases[t - k](k)
        cnt_ref[...] = carry_ref[...].astype(jnp.int32)

    @pl.when(i < n_first)
    def _():
        tile(first)

    @pl.when(i >= n_first)
    def _():
        tile(second)


def _merge(first, second, mod3, mod_row_fn, wts):
    n_first = first[0].shape[0] // TM
    n = first[0].shape[0] + second[0].shape[0]
    tok = lambda w: pl.BlockSpec((TM, w), lambda i: (i, 0))
    widths = (D_MODEL, V_WIDTH, FN_WIDTH, D_MODEL, D_MODEL)
    in_first = [pl.BlockSpec((TM, w), lambda i: (jnp.minimum(i, n_first - 1), 0)) for w in widths]
    in_second = [pl.BlockSpec((TM, w), lambda i: (jnp.maximum(i - n_first, 0), 0)) for w in widths]
    return pl.pallas_call(
        functools.partial(_merge_kernel, n_first=n_first),
        grid=(n // TM,),
        in_specs=in_first + in_second + [
                  pl.BlockSpec((1, 6, D_MODEL), lambda i: (mod_row_fn(i), 0, 0)),
                  _const_spec((V_WIDTH, D_MODEL)),
                  _const_spec((FN_WIDTH, D_MODEL)),
                  _const_spec((D_MODEL, D_MODEL)),
                  _const_spec((1, D_MODEL)),
                  _const_spec((ROUTER_ROWS, D_MODEL)),
                  _const_spec((ROUTER_ROWS, 1))],
        out_specs=[tok(D_MODEL),
                   pl.BlockSpec((TM * TOKEN_ROWS, LANES), lambda i: (i, 0)),
                   pl.BlockSpec((1, TM), lambda i: (0, i)),
                   pl.BlockSpec((1, TM), lambda i: (0, i)), _const_spec((SEG_ROWS, LANES))],
        out_shape=[jax.ShapeDtypeStruct((n, D_MODEL), F32),
                   jax.ShapeDtypeStruct((n * TOKEN_ROWS, LANES), F32),
                   jax.ShapeDtypeStruct((1, n), jnp.int32),
                   jax.ShapeDtypeStruct((1, n), jnp.int32),
                   jax.ShapeDtypeStruct((SEG_ROWS, LANES), jnp.int32)],
        scratch_shapes=[pltpu.VMEM((SEG_ROWS, LANES), F32)],
        compiler_params=_cparams(("arbitrary",)),
        name="merge",
    )(*first, *second, mod3, wts["w_ao"], wts["w_fn"], wts["w_out"], wts["g2"],
      wts["w_rg"], wts["b_rg"])


def _token_rows(ref, t):
    start = t * TOKEN_ROWS
    if not isinstance(t, int):
        start = pl.multiple_of(start, TOKEN_ROWS)
    return ref.at[pl.ds(start, TOKEN_ROWS)]


def _start_row_copy(src, dst, src_tok, dst_tok, sem, queue=0):
    pltpu.async_copy(_token_rows(src, src_tok), _token_rows(dst, dst_tok), sem, priority=queue)


def _wait_tile(src_hbm, buf, sem):
    pltpu.make_async_copy(src_hbm.at[pl.ds(0, buf.shape[0])], buf, sem).wait()


def _combine_kernel(pos_ref, y_hbm, x1_ref, mod_ref, o1_ref, o2_ref, buf, sem, *, n_first):
    i = pl.program_id(0)
    n_tiles = pl.num_programs(0)
    tm = o1_ref.shape[0]

    def start(tile_idx, r):
        slot = tile_idx % (GATHER_AHEAD + 1)
        queue = r % DMA_QUEUES if isinstance(r, int) else 0
        _start_row_copy(y_hbm, buf.at[slot], pos_ref[tile_idx * tm + r], r, sem.at[slot], queue)

    for first in range(GATHER_AHEAD):
        @pl.when(i == 0)
        def _():
            def body(r, c):
                start(first, r)
                return c

            lax.fori_loop(0, tm, body, 0, unroll=32)

    def run(fetch_ahead, o_ref):
        slot = i % (GATHER_AHEAD + 1)
        _wait_tile(y_hbm, buf.at[slot], sem.at[slot])
        lane_tiles = [slice(s * LANES, (s + 1) * LANES) for s in range(TOKEN_ROWS)]
        for s, cols in enumerate(lane_tiles):
            o_ref[:, cols] = mod_ref[0, 5:6, cols] * buf.at[slot][pl.ds(s, tm, stride=TOKEN_ROWS), :]
        per = tm // TOKEN_ROWS
        for s, cols in enumerate(lane_tiles):
            o_ref[:, cols] = x1_ref[:, cols] + o_ref[:, cols]
            if fetch_ahead:
                for r in range(s * per, (s + 1) * per):
                    start(i + GATHER_AHEAD, r)

    fetch = i + GATHER_AHEAD < n_tiles
    for fetch_ahead in (True, False):
        cond = fetch if fetch_ahead else jnp.logical_not(fetch)
        for o_ref, mine in ((o1_ref, i < n_first), (o2_ref, i >= n_first)):
            @pl.when(jnp.logical_and(cond, mine))
            def _():
                run(fetch_ahead, o_ref)


def _combine(y_sorted, pos, x1, mod3, mod_row_fn, n_first_tokens):
    n = x1.shape[0]
    n_first = n_first_tokens // TM
    assert n // TM >= GATHER_AHEAD
    return pl.pallas_call(
        functools.partial(_combine_kernel, n_first=n_first),
        grid_spec=pltpu.PrefetchScalarGridSpec(
            num_scalar_prefetch=1,
            grid=(n // TM,),
            in_specs=[pl.BlockSpec(memory_space=pl.ANY),
                      pl.BlockSpec((TM, D_MODEL), lambda i, pos: (i, 0)),
                      pl.BlockSpec((1, 6, D_MODEL), lambda i, pos: (mod_row_fn(i), 0, 0))],
            out_specs=[pl.BlockSpec((TM, D_MODEL), lambda i, pos: (jnp.minimum(i, n_first - 1), 0)),
                       pl.BlockSpec((TM, D_MODEL), lambda i, pos: (jnp.maximum(i - n_first, 0), 0))],
            scratch_shapes=[pltpu.VMEM((GATHER_AHEAD + 1, TM * TOKEN_ROWS, LANES), F32),
                            pltpu.SemaphoreType.DMA((GATHER_AHEAD + 1,))]),
        out_shape=[jax.ShapeDtypeStruct((n_first_tokens, D_MODEL), F32),
                   jax.ShapeDtypeStruct((n - n_first_tokens, D_MODEL), F32)],
        compiler_params=_cparams(("arbitrary",)),
        name="moe_combine",
    )(pos, y_sorted, x1, mod3)


def _moe_kernel(grp_ref, on_ref, src_ref, h_hbm, wrg_ref, brg_ref, wre_ref, bre_ref, wg_ref, wu_ref,
                wd_ref, o_ref, hbuf, sem, *, tile):
    i = pl.program_id(0)
    n_tiles = pl.num_programs(0)

    def start(tile_idx, r):
        slot = tile_idx % (GATHER_AHEAD + 1)
        _start_row_copy(h_hbm, hbuf.at[slot], src_ref[tile_idx * tile + r], r, sem.at[slot])

    for first in range(GATHER_AHEAD):
        @pl.when(jnp.logical_and(i == 0, on_ref[first] == 1))
        def _():
            def body(r, c):
                start(first, r)
                return c

            lax.fori_loop(0, tile, body, 0, unroll=32)

    def run(fetch_ahead):
        def issue(chunk, n_chunks):
            if fetch_ahead:
                per = tile // n_chunks
                for r in range(chunk * per, (chunk + 1) * per):
                    start(i + GATHER_AHEAD, r)

        slot = i % (GATHER_AHEAD + 1)
        _wait_tile(h_hbm, hbuf.at[slot], sem.at[slot])
        h2_hi, h2_lo = _split_bf16(_load_token_major(hbuf.at[slot], tile))

        logits = _router_logits(jnp.concatenate([wrg_ref[...], wre_ref[0]], axis=0),
                                jnp.concatenate([brg_ref[...], bre_ref[0]], axis=0), h2_hi, h2_lo)
        g = _rows(logits, N_GROUPS)
        gmax = functools.reduce(jnp.maximum, g)
        p_top = 1.0 / functools.reduce(lambda p, q: p + q, [jnp.exp(gj - gmax) for gj in g])
        e = _rows(logits[ROUTER_ROWS:], EXPERTS_PER_GROUP)
        m1 = functools.reduce(jnp.maximum, e)
        i1 = _first_argmax(e, m1)
        rest = [jnp.where(i1 == j, -jnp.inf, e[j]) for j in range(EXPERTS_PER_GROUP)]
        m2 = functools.reduce(jnp.maximum, rest)
        i2 = _first_argmax(rest, m2)
        t = jnp.exp(m2 - m1)
        w1 = p_top / (1.0 + t)
        w2 = p_top * t / (1.0 + t)
        row = lax.broadcasted_iota(jnp.int32, (LANES, tile), 0)
        comb = (jnp.where(row == i1, w1, 0.0) + jnp.where(row == i2, w2, 0.0)).T

        gates, ups = [], []
        for j in range(EXPERTS_PER_GROUP):
            gates.append(_dot(h2_hi, wg_ref[j]))
            issue(2 * j, 2 * EXPERTS_PER_GROUP)
            ups.append(_dot(h2_hi, wu_ref[j]))
            issue(2 * j + 1, 2 * EXPERTS_PER_GROUP)
        acts = [((a * _sigmoid(a)) * u * comb[:, j:j + 1]).astype(BF16)
                for j, (a, u) in enumerate(zip(gates, ups))]
        acc = functools.reduce(lambda p, q: p + q,
                               [_dot(acts[j], wd_ref[j]) for j in range(EXPERTS_PER_GROUP)])
        _store_token_major(o_ref, acc, tile)

    @pl.when(on_ref[i] == 0)
    def _():
        o_ref[...] = jnp.zeros_like(o_ref)

    ahead = jnp.minimum(i + GATHER_AHEAD, n_tiles - 1)
    fetch = jnp.logical_and(i + GATHER_AHEAD < n_tiles, on_ref[ahead] == 1)

    @pl.when(jnp.logical_and(on_ref[i] == 1, fetch))
    def _():
        run(True)

    @pl.when(jnp.logical_and(on_ref[i] == 1, jnp.logical_not(fetch)))
    def _():
        run(False)


def _moe(h2tm, src, maps, wts, w_gate, w_up, w_down, tile):
    n_tiles = src.shape[0] // tile
    const = lambda shape: pl.BlockSpec(shape, lambda i, grp, on, src: (0,) * len(shape))
    by_group = lambda shape: pl.BlockSpec(
        shape, lambda i, grp, on, src: (grp[i],) + (0,) * (len(shape) - 1))
    return pl.pallas_call(
        functools.partial(_moe_kernel, tile=tile),
        grid_spec=pltpu.PrefetchScalarGridSpec(
            num_scalar_prefetch=3,
            grid=(n_tiles,),
            in_specs=[pl.BlockSpec(memory_space=pl.ANY),
                      const((ROUTER_ROWS, D_MODEL)), const((ROUTER_ROWS, 1)),
                      by_group((1, ROUTER_ROWS, D_MODEL)), by_group((1, ROUTER_ROWS, 1)),
                      by_group((EXPERTS_PER_GROUP, D_MODEL, D_EXPERT)),
                      by_group((EXPERTS_PER_GROUP, D_MODEL, D_EXPERT)),
                      by_group((EXPERTS_PER_GROUP, D_EXPERT, D_MODEL))],
            out_specs=pl.BlockSpec((tile * TOKEN_ROWS, LANES), lambda i, grp, on, src: (i, 0)),
            scratch_shapes=[pltpu.VMEM((GATHER_AHEAD + 1, tile * TOKEN_ROWS, LANES), F32),
                            pltpu.SemaphoreType.DMA((GATHER_AHEAD + 1,))]),
        out_shape=jax.ShapeDtypeStruct((n_tiles * tile * TOKEN_ROWS, LANES), F32),
        compiler_params=_cparams(("arbitrary",)),
        name="moe",
    )(*maps, src, h2tm, wts["w_rg"], wts["b_rg"], wts["w_re"], wts["b_re"], w_gate, w_up, w_down)


def _invert_kernel(pos_ref, lo_ref, hi_ref, src_ref, *, n, n_fill):
    def zero(p, c):
        src_ref[p] = 0
        return c

    for s in range(n_fill):
        lax.fori_loop(lo_ref[s], hi_ref[s], zero, 0)

    def put(t, c):
        src_ref[pos_ref[t]] = t
        return c

    lax.fori_loop(0, n, put, 0, unroll=8)


def _invert(pos, fill_lo, fill_hi, n_sorted):
    return pl.pallas_call(
        functools.partial(_invert_kernel, n=pos.shape[0], n_fill=fill_lo.shape[0]),
        grid_spec=pltpu.PrefetchScalarGridSpec(
            num_scalar_prefetch=3, grid=(1,), in_specs=[],
            out_specs=pl.BlockSpec(memory_space=pltpu.SMEM)),
        out_shape=jax.ShapeDtypeStruct((n_sorted,), jnp.int32),
        name="moe_invert",
    )(pos, fill_lo, fill_hi)


def _sort_plan(grp, rank, cnt, tile, n_tiles):
    n_tile_grp = (cnt + tile - 1) // tile
    tile_end = jnp.cumsum(n_tile_grp)
    tile_start = tile_end - n_tile_grp
    total = tile_end[-1]
    pos = ((tile_start * tile)[grp] + rank).astype(jnp.int32)
    fill_lo = jnp.concatenate([tile_start * tile + cnt, total[None] * tile]).astype(jnp.int32)
    fill_hi = jnp.concatenate([tile_end * tile, jnp.full((1,), n_tiles * tile)]).astype(jnp.int32)
    src = _invert(pos, fill_lo, fill_hi, n_tiles * tile)
    i = jnp.arange(n_tiles, dtype=jnp.int32)
    tile_grp = jnp.sum((jnp.minimum(i, total - 1)[:, None] >= tile_end[None, :]).astype(jnp.int32),
                       axis=1)
    return pos, src, (tile_grp.astype(jnp.int32), (i < total).astype(jnp.int32))


def _sparse_moe(x1, h2tm, grp, rank, cnt8, mod3, mod_row_fn, wts, w_gate, w_up, w_down, tile,
                n_first_tokens):
    n = x1.shape[0]
    n_tiles = n // tile + N_GROUPS
    pos, src, maps = _sort_plan(grp.reshape(n), rank.reshape(n), cnt8[:N_GROUPS, 0], tile, n_tiles)
    y_sorted = _moe(h2tm, src, maps, wts, w_gate, w_up, w_down, tile)
    return _combine(y_sorted, pos, x1, mod3, mod_row_fn, n_first_tokens)


def _pack_weights(l, norm1_g, q_a_norm_g, w_q_b, kv_a_norm_g, w_kv_b, q_norm_g, k_norm_g,
                  w_attn_o, w_fnet, w_out, norm2_g, w_router_group, b_router_group,
                  w_router_expert, b_router_expert):
    w_qb = jnp.pad(w_q_b[l].reshape(Q_LORA, N_HEADS, QK_HEAD),
                   ((0, 0), (0, 0), (0, HEAD_SLOT - QK_HEAD))).reshape(Q_LORA, QK_WIDTH)
    wkv = w_kv_b[l].reshape(KV_LORA, N_HEADS, QK_NOPE + V_HEAD)
    w_kb = jnp.pad(wkv[:, :, :QK_NOPE],
                   ((0, 0), (0, 0), (0, HEAD_SLOT - QK_NOPE))).reshape(KV_LORA, QK_WIDTH)
    w_vb = wkv[:, :, QK_NOPE:].reshape(KV_LORA, V_WIDTH)
    pad_g = lambda g: jnp.pad(g, (0, HEAD_SLOT - QK_HEAD)).reshape(1, HEAD_SLOT)
    w_rg = jnp.pad(w_router_group[l].T, ((0, ROUTER_ROWS - N_GROUPS), (0, 0)))
    b_rg = jnp.pad(b_router_group[l], (0, ROUTER_ROWS - N_GROUPS)).reshape(ROUTER_ROWS, 1)
    w_re = jnp.pad(w_router_expert[l].T.reshape(N_GROUPS, EXPERTS_PER_GROUP, D_MODEL),
                   ((0, 0), (0, ROUTER_ROWS - EXPERTS_PER_GROUP), (0, 0)))
    b_re = jnp.pad(b_router_expert[l].reshape(N_GROUPS, EXPERTS_PER_GROUP),
                   ((0, 0), (0, ROUTER_ROWS - EXPERTS_PER_GROUP))).reshape(N_GROUPS, ROUTER_ROWS, 1)
    dft_c, dft_ns = _dft_tables(FN_GROUP_W)
    return {
        "g1": norm1_g[l].reshape(1, D_MODEL),
        "qag": q_a_norm_g[l].reshape(1, Q_LORA),
        "w_qb": w_qb.astype(BF16),
        "kvg": kv_a_norm_g[l].reshape(1, KV_LORA),
        "w_kb": w_kb.astype(BF16),
        "w_vb": w_vb.T.astype(BF16),
        "qg": pad_g(q_norm_g[l]),
        "kg": pad_g(k_norm_g[l]),
        "dft_c": jnp.concatenate([jnp.asarray(dft_c), -jnp.asarray(dft_ns)], axis=1).astype(BF16),
        "w_ao": w_attn_o[l].astype(BF16),
        "w_fn": w_fnet[l].astype(BF16),
        "w_out": w_out[l].astype(BF16),
        "g2": norm2_g[l].reshape(1, D_MODEL),
        "w_rg": w_rg,
        "b_rg": b_rg,
        "w_re": w_re,
        "b_re": b_re,
    }


def _layer(xp, xs, cache_ckv_l, cache_kpe_l, mod3, wts, experts):
    bp, sp, _ = xp.shape
    bs, ss, _ = xs.shape
    ctx_row = lambda i: 0
    lat_row = lambda i: 1 + i // (ss // TM)

    xp2 = xp.reshape(bp * sp, D_MODEL)
    ride = (bp * sp) // TM == N_EXPERTS
    cs, ns = (jnp.asarray(t).astype(BF16) for t in _dft_tables(sp))
    outs = _inproj(xp2, mod3, ctx_row, wts, None, True, experts if ride else (), (sp, cs, ns))
    attn, fm, sga, sgf, ckv, kpe = outs[:6]
    w_gate, w_up, w_down = outs[6:] if ride else (w.astype(BF16) for w in experts)
    ctx_set = (xp2, attn, fm, sga, sgf)

    xs2 = xs.reshape(bs * ss, D_MODEL)
    rope_tabs = tuple(jnp.asarray(t) for t in _rope_tables(ss))
    q, k, v, fcs, sga, sgf = _inproj(xs2, mod3, lat_row, wts, rope_tabs, False)
    kpe_slot = jnp.pad(cache_kpe_l, ((0, 0), (0, 0), (QK_NOPE, LANES - QK_HEAD)))
    cs, ns = (jnp.asarray(t).astype(BF16) for t in _dft_tables(ss))
    attn, fm = _attention(q.reshape(bs, ss, QK_WIDTH), k.reshape(bs, ss, QK_WIDTH), v, cache_ckv_l,
                          kpe_slot, wts, fcs.reshape(bs, ss, 2 * FN_WIDTH), cs, ns, TQ_LAT, "attn_lat")
    lat_set = (xs2, attn.reshape(bs * ss, V_WIDTH), fm.reshape(bs * ss, FN_WIDTH), sga, sgf)

    n_ctx_tiles = (bp * sp) // TM
    all_row = lambda i: jnp.where(i < n_ctx_tiles, 0, lat_row(i - n_ctx_tiles))
    x1, h2tm, grp, rank, cnt = _merge(ctx_set, lat_set, mod3, all_row, wts)
    yp, ys = _sparse_moe(x1, h2tm, grp, rank, cnt, mod3, all_row, wts, w_gate, w_up, w_down, MOE_TILE,
                         bp * sp)
    yp, ys = yp.reshape(bp, sp, D_MODEL), ys.reshape(bs, ss, D_MODEL)

    return yp, ys, ckv.reshape(bp, sp, KV_LORA), jnp.swapaxes(kpe, 1, 2)


def kernel(x_prompt, x_sample, cache_ckv, cache_kpe, c, c_ctx, w_mod, b_mod, norm1_g, w_in, q_a_norm_g, w_q_b, kv_a_norm_g, w_kv_b, q_norm_g, k_norm_g, w_attn_o, w_fnet, w_out, norm2_g, w_router_group, b_router_group, w_router_expert, b_router_expert, w_exp_gate, w_exp_up, w_exp_down):
    depth = w_mod.shape[0]
    n_lat = c.shape[0]
    assert 1 + n_lat <= MOD_ROWS
    cond8 = jnp.concatenate([c_ctx[None, :], c, jnp.zeros((MOD_ROWS - 1 - n_lat, D_MODEL), F32)], axis=0)
    xp, xs = x_prompt, x_sample
    ckv_layers, kpe_layers = [], []
    for l in range(depth):
        mod, w_in_p = _adaln(cond8, w_mod[l], b_mod[l].reshape(1, -1), jnp.swapaxes(w_in[l], 0, 1))
        mod3 = mod.reshape(MOD_ROWS, 6, D_MODEL)
        wts = _pack_weights(l, norm1_g, q_a_norm_g, w_q_b, kv_a_norm_g, w_kv_b, q_norm_g,
                            k_norm_g, w_attn_o, w_fnet, w_out, norm2_g, w_router_group,
                            b_router_group, w_router_expert, b_router_expert)
        wts["w_in"] = w_in_p
        xp, xs, ckv, kpe = _layer(xp, xs, cache_ckv[:, l], cache_kpe[:, l], mod3, wts,
                                  (w_exp_gate[l], w_exp_up[l], w_exp_down[l]))
        ckv_layers.append(ckv)
        kpe_layers.append(kpe)
    return xp, xs, jnp.stack(ckv_layers, axis=1), jnp.stack(kpe_layers, axis=1)
```

```python
import functools
import math

import numpy as np
import jax
import jax.numpy as jnp
from jax import lax
from jax.experimental import pallas as pl
from jax.experimental.pallas import tpu as pltpu

D_MODEL = 1024
GRID_W = 64
N_HEADS = 8
Q_LORA = 512
KV_LORA = 256
QK_NOPE = 64
QK_ROPE = 32
V_HEAD = 64
QK_HEAD = QK_NOPE + QK_ROPE
ATTN_SCALE = QK_HEAD ** -0.5
ROPE_BASE = 10000.0
FN_GROUPS = 4
FN_GROUP_W = 128
FN_WIDTH = FN_GROUPS * FN_GROUP_W
N_GROUPS = 4
EXPERTS_PER_GROUP = 4
N_EXPERTS = N_GROUPS * EXPERTS_PER_GROUP
D_EXPERT = 512
EPS = 1e-6

LANES = 128
HEAD_SLOT = LANES
QK_WIDTH = N_HEADS * HEAD_SLOT
V_WIDTH = N_HEADS * V_HEAD
C_QA = 0
C_KVA = C_QA + Q_LORA
C_KPE = C_KVA + KV_LORA
C_FN = C_KPE + LANES
C_GA = C_FN + FN_WIDTH
C_GF = C_GA + D_MODEL
IN_PACKED = C_GF + D_MODEL
SUBLANES = 8
ROUTER_ROWS = SUBLANES
MOD_ROWS = SUBLANES
VMEM_LIMIT = 56 * 1024 * 1024

TM = 512
TQ_LAT = 256
SEG_ROWS = SUBLANES
TOKEN_ROWS = D_MODEL // LANES
MOE_TILE = 256
MERGE_SPLIT = 2
DMA_QUEUES = 2
GATHER_AHEAD = 2
ADALN_STEPS = 8

BF16 = jnp.bfloat16
F32 = jnp.float32


def _cparams(sem):
    return pltpu.CompilerParams(dimension_semantics=sem, vmem_limit_bytes=VMEM_LIMIT)


def _dot(a, b):
    return jnp.dot(a, b, preferred_element_type=F32)


def _dot_nt(a, b):
    return lax.dot_general(a, b, (((1,), (1,)), ((), ())), preferred_element_type=F32)


def _sigmoid(x):
    return 1.0 / (1.0 + jnp.exp(-x))


def _split_bf16(x):
    hi = x.astype(BF16)
    return hi, (x - hi.astype(F32)).astype(BF16)


@functools.lru_cache(maxsize=None)
def _rope_tables(n_pos):
    half = QK_ROPE // 2
    quarter = half // 2
    freqs = ROPE_BASE ** (-np.arange(quarter, dtype=np.float64) / quarter)
    pos = np.arange(n_pos)
    row = (pos // GRID_W).astype(np.float64)
    col = (pos % GRID_W).astype(np.float64)
    cos_t = np.ones((n_pos, LANES), np.float64)
    sin_a = np.zeros((n_pos, LANES), np.float64)
    sin_b = np.zeros((n_pos, LANES), np.float64)
    for base, p in ((QK_NOPE, row), (QK_NOPE + half, col)):
        ang = p[:, None] * freqs[None, :]
        cos_t[:, base:base + quarter] = np.cos(ang)
        cos_t[:, base + quarter:base + half] = np.cos(ang)
        sin_a[:, base:base + quarter] = -np.sin(ang)
        sin_b[:, base + quarter:base + half] = np.sin(ang)
    return (cos_t.astype(np.float32), sin_a.astype(np.float32), sin_b.astype(np.float32))


@functools.lru_cache(maxsize=None)
def _dft_tables(n):
    k = np.arange(n)
    ang = 2.0 * np.pi * ((k[:, None] * k[None, :]) % n) / n
    s = 1.0 / math.sqrt(n)
    return (np.cos(ang) * s).astype(np.float32), (-np.sin(ang) * s).astype(np.float32)


def _adaln_kernel(cond_ref, w_ref, b_ref, win_ref, o_ref, winp_ref):
    c = cond_ref[...]
    s_hi, s_lo = _split_bf16(c * _sigmoid(c))
    w_hi, w_lo = _split_bf16(w_ref[...])
    y = _dot(jnp.concatenate([s_hi, s_lo], axis=0), w_hi)
    o_ref[...] = (y[:MOD_ROWS] + y[MOD_ROWS:]) + _dot(s_hi, w_lo) + b_ref[...]

    winp_ref[:C_KPE, :] = win_ref[:C_KPE, :].astype(BF16)
    winp_ref[C_KPE:C_FN, :] = jnp.zeros((LANES, winp_ref.shape[1]), BF16)
    winp_ref[C_KPE + QK_NOPE:C_KPE + QK_HEAD, :] = win_ref[C_KPE:C_KPE + QK_ROPE, :].astype(BF16)
    winp_ref[C_FN:, :] = win_ref[C_KPE + QK_ROPE:, :].astype(BF16)


def _adaln(cond8, w_mod, b_mod, w_in_t):
    n = w_mod.shape[1]
    steps = ADALN_STEPS
    tn, tr = n // steps, D_MODEL // steps
    return pl.pallas_call(
        _adaln_kernel,
        grid=(steps,),
        in_specs=[pl.BlockSpec((MOD_ROWS, D_MODEL), lambda j: (0, 0)),
                  pl.BlockSpec((D_MODEL, tn), lambda j: (0, j)),
                  pl.BlockSpec((1, tn), lambda j: (0, j)),
                  pl.BlockSpec((w_in_t.shape[0], tr), lambda j: (0, j))],
        out_specs=[pl.BlockSpec((MOD_ROWS, tn), lambda j: (0, j)),
                   pl.BlockSpec((IN_PACKED, tr), lambda j: (0, j))],
        out_shape=[jax.ShapeDtypeStruct((MOD_ROWS, n), F32),
                   jax.ShapeDtypeStruct((IN_PACKED, D_MODEL), BF16)],
        compiler_params=_cparams(("arbitrary",)),
        name="adaln",
    )(cond8, w_mod, b_mod, w_in_t)


def _rms(x, width):
    return lax.rsqrt(jnp.sum(x * x, axis=-1, keepdims=True) * (1.0 / width) + EPS)


def _rope(x, cos_t, sin_a, sin_b):
    return x * cos_t + pltpu.roll(x, LANES - 8, 1) * sin_a + pltpu.roll(x, 8, 1) * sin_b


def _inproj_kernel(*refs, rope, emit_cache, n_cast, seq_len):
    it = iter(refs)
    x_ref, mod_ref, g1_ref, win_ref, qag_ref, wqb_ref, kvg_ref, wkb_ref, wvb_ref = (
        next(it) for _ in range(9))
    qg_ref, kg_ref, dft_ref = next(it), next(it), next(it)
    if rope:
        cos_ref, sa_ref, sb_ref = next(it), next(it), next(it)
    if seq_len:
        seq_cs_ref, seq_ns_ref = next(it), next(it)
    cast_in = [next(it) for _ in range(n_cast)]
    if seq_len:
        attn_ref, fm_ref, sga_ref, sgf_ref = (next(it) for _ in range(4))
    else:
        q_ref, k_ref, v_ref, fcs_ref, sga_ref, sgf_ref = (next(it) for _ in range(6))
    if emit_cache:
        ckv_ref, kpe_ref = next(it), next(it)
    cast_out = [next(it) for _ in range(n_cast)]
    if seq_len:
        q_ref, k_ref, v_ref, fcs_ref = (next(it) for _ in range(4))

    for src, dst in zip(cast_in, cast_out):
        dst[...] = src[...].astype(BF16)

    x = x_ref[...]
    shift = mod_ref[0, 0:1, :]
    scale = mod_ref[0, 1:2, :]
    h = (x * _rms(x, D_MODEL) * g1_ref[...]) * (1.0 + scale) + shift
    hb = h.astype(BF16)

    if rope:
        cos_t, sin_a, sin_b = cos_ref[...], sa_ref[...], sb_ref[...]

    qa = _dot_nt(hb, win_ref[C_QA:C_QA + Q_LORA, :])
    qn = (qa * _rms(qa, Q_LORA) * qag_ref[...]).astype(BF16)
    q = _dot(qn, wqb_ref[...])
    qg = qg_ref[...] * ATTN_SCALE
    for hd in range(N_HEADS):
        qh = q[:, hd * HEAD_SLOT:(hd + 1) * HEAD_SLOT]
        qh = qh * _rms(qh, QK_HEAD) * qg
        if rope:
            qh = _rope(qh, cos_t, sin_a, sin_b)
        q_ref[:, hd * HEAD_SLOT:(hd + 1) * HEAD_SLOT] = qh.astype(BF16)

    kva = _dot_nt(hb, win_ref[C_KVA:C_KVA + KV_LORA, :])
    ckv = kva * _rms(kva, KV_LORA) * kvg_ref[...]
    kpe = _dot_nt(hb, win_ref[C_KPE:C_KPE + LANES, :])
    if emit_cache:
        ckv_ref[...] = ckv
        kpe_t = kpe.T[QK_NOPE:QK_NOPE + QK_ROPE, :]
        for j in range(x.shape[0] // seq_len):
            kpe_ref[j] = kpe_t[:, j * seq_len:(j + 1) * seq_len]
    _emit_kv(ckv.astype(BF16), kpe, wkb_ref, wvb_ref, kg_ref,
             (cos_t, sin_a, sin_b) if rope else None, k_ref, v_ref)

    fn = _dot_nt(hb, win_ref[C_FN:C_FN + FN_WIDTH, :]).astype(BF16)
    for g in range(FN_GROUPS):
        cs = _dot(fn[:, g * FN_GROUP_W:(g + 1) * FN_GROUP_W], dft_ref[...])
        fcs_ref[:, g * FN_GROUP_W:(g + 1) * FN_GROUP_W] = cs[:, :FN_GROUP_W].astype(BF16)
        fcs_ref[:, FN_WIDTH + g * FN_GROUP_W:FN_WIDTH + (g + 1) * FN_GROUP_W] = (
            cs[:, FN_GROUP_W:].astype(BF16))

    def gate(ref, c0, half):
        def run():
            w = D_MODEL // 2
            ref[:, pl.ds(half * w, w)] = _sigmoid(
                _dot_nt(hb, win_ref[pl.ds(c0 + half * w, w), :])).astype(BF16)
        return run

    gates = [gate(ref, c0, half) for ref, c0 in ((sga_ref, C_GA), (sgf_ref, C_GF)) for half in range(2)]

    if seq_len:
        n_seq = x.shape[0] // seq_len
        for j in range(n_seq):
            rows = pl.ds(j * seq_len, seq_len)
            fill = gates[j * len(gates) // n_seq:(j + 1) * len(gates) // n_seq]
            attn_ref[rows, :] = _attend(q_ref.at[rows], [k_ref.at[rows]], [v_ref.at[:, rows]], fill)
            fm_ref[rows, :] = (_dot(seq_cs_ref[...], fcs_ref[rows, :FN_WIDTH])
                               + _dot(seq_ns_ref[...], fcs_ref[rows, FN_WIDTH:])).astype(BF16)
    else:
        for run_gate in gates:
            run_gate()


def _emit_kv(ckvb, kpe, wkb_ref, wvb_ref, kg_ref, rope_tabs, k_ref, v_ref):
    kg = kg_ref[...]
    v_ref[...] = _dot_nt(wvb_ref[...], ckvb).astype(BF16)
    kn = _dot(ckvb, wkb_ref[...])
    pe_ss = jnp.sum(kpe * kpe, axis=-1, keepdims=True)
    pe_g = kpe * kg
    if rope_tabs is not None:
        pe_g = _rope(pe_g, *rope_tabs)
    for hd in range(N_HEADS):
        knh = kn[:, hd * HEAD_SLOT:(hd + 1) * HEAD_SLOT]
        ss = jnp.sum(knh * knh, axis=-1, keepdims=True) + pe_ss
        r = lax.rsqrt(ss * (1.0 / QK_HEAD) + EPS)
        k_ref[:, hd * HEAD_SLOT:(hd + 1) * HEAD_SLOT] = ((knh * kg + pe_g) * r).astype(BF16)


def _const_spec(shape):
    return pl.BlockSpec(shape, lambda i: (0,) * len(shape))


def _inproj(x2d, mod3, mod_row_fn, wts, rope_tabs, emit_cache, cast=(), seq=None):
    n = x2d.shape[0]
    assert all(w.shape[0] == n // TM for w in cast)
    assert seq is None or TM % seq[0] == 0
    rope = rope_tabs is not None
    tiles_per_seq = None if not rope else rope_tabs[0].shape[0] // TM
    in_specs = [pl.BlockSpec((TM, D_MODEL), lambda i: (i, 0)),
                pl.BlockSpec((1, 6, D_MODEL), lambda i: (mod_row_fn(i), 0, 0)),
                _const_spec((1, D_MODEL)),
                _const_spec((IN_PACKED, D_MODEL)),
                _const_spec((1, Q_LORA)),
                _const_spec((Q_LORA, QK_WIDTH)),
                _const_spec((1, KV_LORA)),
                _const_spec((KV_LORA, QK_WIDTH)),
                _const_spec((V_WIDTH, KV_LORA)),
                _const_spec((1, HEAD_SLOT)),
                _const_spec((1, HEAD_SLOT)),
                _const_spec((FN_GROUP_W, 2 * FN_GROUP_W))]
    args = [x2d, mod3, wts["g1"], wts["w_in"], wts["qag"], wts["w_qb"], wts["kvg"],
            wts["w_kb"], wts["w_vb"], wts["qg"], wts["kg"], wts["dft_c"]]
    if rope:
        in_specs += [pl.BlockSpec((TM, LANES), lambda i: (i % tiles_per_seq, 0))] * 3
        args += list(rope_tabs)
    if seq is not None:
        in_specs += [_const_spec((seq[0], seq[0]))] * 2
        args += [seq[1], seq[2]]
    cast_specs = [pl.BlockSpec((1,) + w.shape[1:], lambda i: (i, 0, 0)) for w in cast]
    in_specs += cast_specs
    args += list(cast)
    qkvf = [((TM, QK_WIDTH), (n, QK_WIDTH)), ((TM, QK_WIDTH), (n, QK_WIDTH)),
            ((V_WIDTH, TM), (V_WIDTH, n)), ((TM, 2 * FN_WIDTH), (n, 2 * FN_WIDTH))]
    if seq is None:
        widths = [None] * 4 + [D_MODEL, D_MODEL]
    else:
        widths = [V_WIDTH, FN_WIDTH, D_MODEL, D_MODEL]
    out_shape, out_specs = [], []
    for j, w in enumerate(widths):
        if w is None:
            out_shape.append(jax.ShapeDtypeStruct(qkvf[j][1], BF16))
            out_specs.append(pl.BlockSpec(qkvf[j][0], (lambda i: (0, i)) if j == 2 else (lambda i: (i, 0))))
        else:
            out_shape.append(jax.ShapeDtypeStruct((n, w), BF16))
            out_specs.append(pl.BlockSpec((TM, w), lambda i: (i, 0)))
    if emit_cache:
        assert seq is not None
        out_shape += [jax.ShapeDtypeStruct((n, KV_LORA), F32),
                      jax.ShapeDtypeStruct((n // seq[0], QK_ROPE, seq[0]), F32)]
        out_specs += [pl.BlockSpec((TM, KV_LORA), lambda i: (i, 0)),
                      pl.BlockSpec((TM // seq[0], QK_ROPE, seq[0]), lambda i: (i, 0, 0))]
    out_shape += [jax.ShapeDtypeStruct(w.shape, BF16) for w in cast]
    out_specs += cast_specs
    scratch = [] if seq is None else [pltpu.VMEM(blk, BF16) for blk, _ in qkvf]
    return pl.pallas_call(
        functools.partial(_inproj_kernel, rope=rope, emit_cache=emit_cache, n_cast=len(cast),
                          seq_len=None if seq is None else seq[0]),
        grid=(n // TM,),
        in_specs=in_specs,
        out_specs=out_specs,
        out_shape=out_shape,
        scratch_shapes=scratch,
        compiler_params=_cparams(("parallel",)),
        name="inproj_lat" if rope else "inproj_ctx",
    )(*args)


def _attend(q, ks, vts, fillers=()):
    head = lambda hd: slice(hd * HEAD_SLOT, (hd + 1) * HEAD_SLOT)
    st = [jnp.stack([_dot_nt(k[:, head(hd)], q[:, head(hd)]) for hd in range(N_HEADS)]) for k in ks]
    m = functools.reduce(jnp.maximum, [sj.max(axis=1, keepdims=True) for sj in st])
    fillers = list(fillers)
    every = N_HEADS // max(1, len(fillers))
    p = []
    for hd in range(N_HEADS):
        p.append([jnp.exp(sj[hd] - m[hd]) for sj in st])
        if fillers and hd % every == 0:
            fillers.pop(0)()
    assert not fillers
    outs = []
    for hd in range(N_HEADS):
        l = functools.reduce(lambda a, b: a + b, [pj.sum(axis=0, keepdims=True) for pj in p[hd]])
        o = functools.reduce(lambda a, b: a + b,
                             [_dot(vt[hd * V_HEAD:(hd + 1) * V_HEAD, :], pj.astype(BF16))
                              for vt, pj in zip(vts, p[hd])])
        outs.append(o / l)
    return jnp.concatenate(outs, axis=0).T.astype(BF16)


def _attn_kernel(q_ref, k_ref, vt_ref, cckv_ref, ckpe_ref, wkb_ref, wvb_ref, kg_ref, cs_ref, ns_ref,
                 f_ref, o_ref, fm_ref, kc_ref, vtc_ref):
    @pl.when(pl.program_id(1) == 0)
    def _():
        _emit_kv(cckv_ref[0].astype(BF16), ckpe_ref[0], wkb_ref, wvb_ref, kg_ref, None, kc_ref, vtc_ref)

    def dft(c0, width):
        def run():
            fm_ref[0, :, pl.ds(c0, width)] = (
                _dot(cs_ref[...], f_ref[0, :, pl.ds(c0, width)])
                + _dot(ns_ref[...], f_ref[0, :, pl.ds(FN_WIDTH + c0, width)])).astype(BF16)
        return run

    half = FN_WIDTH // 2
    o_ref[0] = _attend(q_ref.at[0], [kc_ref, k_ref.at[0]], [vtc_ref, vt_ref],
                       [dft(0, half), dft(half, half)])


def _attention(q3, k3, vt, cache_ckv, cache_kpe_slot, wts, fcs3, cs, ns, tq, name):
    b, sq, _ = q3.shape
    past = cache_ckv.shape[1]
    seq = lambda shape: pl.BlockSpec(shape, lambda bi, qi: (bi, 0, 0))
    const = lambda shape: pl.BlockSpec(shape, lambda bi, qi: (0,) * len(shape))
    return pl.pallas_call(
        _attn_kernel,
        grid=(b, sq // tq),
        in_specs=[pl.BlockSpec((1, tq, QK_WIDTH), lambda bi, qi: (bi, qi, 0)),
                  seq((1, sq, QK_WIDTH)),
                  pl.BlockSpec((V_WIDTH, sq), lambda bi, qi: (0, bi)),
                  seq((1, past, KV_LORA)), seq((1, past, LANES)),
                  const((KV_LORA, QK_WIDTH)), const((V_WIDTH, KV_LORA)), const((1, HEAD_SLOT)),
                  pl.BlockSpec((tq, sq), lambda bi, qi: (qi, 0)),
                  pl.BlockSpec((tq, sq), lambda bi, qi: (qi, 0)),
                  seq((1, sq, 2 * FN_WIDTH))],
        out_specs=[pl.BlockSpec((1, tq, V_WIDTH), lambda bi, qi: (bi, qi, 0)),
                   pl.BlockSpec((1, tq, FN_WIDTH), lambda bi, qi: (bi, qi, 0))],
        out_shape=[jax.ShapeDtypeStruct((b, sq, V_WIDTH), BF16),
                   jax.ShapeDtypeStruct((b, sq, FN_WIDTH), BF16)],
        scratch_shapes=[pltpu.VMEM((past, QK_WIDTH), BF16), pltpu.VMEM((V_WIDTH, past), BF16)],
        compiler_params=_cparams(("parallel", "arbitrary")),
        name=name,
    )(q3, k3, vt, cache_ckv, cache_kpe_slot, wts["w_kb"], wts["w_vb"], wts["kg"], cs, ns, fcs3)


def _router_logits(w, b, h_hi, h_lo):
    rows = w.shape[0]
    w_hi, w_lo = _split_bf16(w)
    y = _dot_nt(jnp.concatenate([w_hi, w_lo], axis=0), h_hi)
    return (y[:rows] + y[rows:]) + _dot_nt(w_hi, h_lo) + b


def _rows(x, n):
    return [x[j:j + 1, :] for j in range(n)]


def _first_argmax(rows, top):
    idx = jnp.full(top.shape, len(rows) - 1, jnp.int32)
    for j in range(len(rows) - 2, -1, -1):
        idx = jnp.where(rows[j] == top, j, idx)
    return idx


def _store_token_major(ref, x, tm, first=0):
    for s in range(TOKEN_ROWS):
        ref[pl.ds(first * TOKEN_ROWS + s, tm, stride=TOKEN_ROWS), :] = x[:, s * LANES:(s + 1) * LANES]


def _load_token_major(ref, tm):
    return jnp.concatenate([ref[pl.ds(s, tm, stride=TOKEN_ROWS), :] for s in range(TOKEN_ROWS)],
                           axis=1)


def _merge_kernel(*refs, n_first):
    first, second = refs[0:5], refs[5:10]
    (mod_ref, wao_ref, wfn_ref, wout_ref, g2_ref, wrg_ref, brg_ref, x1_ref, h2_ref, grp_ref, rank_ref,
     cnt_ref, carry_ref) = refs[10:]
    i = pl.program_id(0)

    @pl.when(i == 0)
    def _():
        carry_ref[...] = jnp.zeros_like(carry_ref)

    def tile(src):
        x_ref, attn_ref, fm_ref, sga_ref, sgf_ref = src
        hm = x_ref.shape[0] // MERGE_SPLIT
        rows = [pl.ds(k * hm, hm) for k in range(MERGE_SPLIT)]
        live = [dict() for _ in range(MERGE_SPLIT)]

        def branches(k):
            live[k]["a"] = _dot(attn_ref[rows[k], :], wao_ref[...])
            live[k]["f"] = _dot(fm_ref[rows[k], :], wfn_ref[...])

        def gated(k):
            u = (sga_ref[rows[k], :].astype(F32) * live[k].pop("a")
                 + sgf_ref[rows[k], :].astype(F32) * live[k].pop("f"))
            live[k]["u"] = u.astype(BF16)

        def project(k):
            live[k]["y"] = _dot(live[k].pop("u"), wout_ref[...])

        def residual(k):
            x1 = x_ref[rows[k], :] + mod_ref[0, 2:3, :] * live[k].pop("y")
            x1_ref[rows[k], :] = x1
            h2 = (x1 * _rms(x1, D_MODEL) * g2_ref[...]) * (1.0 + mod_ref[0, 4:5, :]) + mod_ref[0, 3:4, :]
            _store_token_major(h2_ref, h2, hm, k * hm)
            live[k]["h2"] = _split_bf16(h2)

        def route(k):
            g = _rows(_router_logits(wrg_ref[...], brg_ref[...], *live[k].pop("h2")), N_GROUPS)
            gidx = _first_argmax(g, functools.reduce(jnp.maximum, g))
            onehot = jnp.where(lax.broadcasted_iota(jnp.int32, (SEG_ROWS, hm), 0) == gidx, 1.0, 0.0)
            before = (lax.broadcasted_iota(jnp.int32, (hm, hm), 0)
                      < lax.broadcasted_iota(jnp.int32, (hm, hm), 1))
            prefix = _dot(onehot.astype(BF16), jnp.where(before, 1.0, 0.0).astype(BF16))
            carry = carry_ref[...]
            rank = jnp.sum(onehot * (prefix + carry[:, 0:1]), axis=0, keepdims=True)
            grp_ref[:, rows[k]] = gidx
            rank_ref[:, rows[k]] = rank.astype(jnp.int32)
            carry_ref[...] = carry + jnp.sum(onehot, axis=1, keepdims=True)

        phases = [branches, gated, project, residual, route]
        for t in range(len(phases) + MERGE_SPLIT - 1):
            for k in range(MERGE_SPLIT):
                if 0 <= t - k < len(phases):
                    phases[t - k](k)
        cnt_ref[...] = carry_ref[...].astype(jnp.int32)

    @pl.when(i < n_first)
    def _():
        tile(first)

    @pl.when(i >= n_first)
    def _():
        tile(second)


def _merge(first, second, mod3, mod_row_fn, wts):
    n_first = first[0].shape[0] // TM
    n = first[0].shape[0] + second[0].shape[0]
    tok = lambda w: pl.BlockSpec((TM, w), lambda i: (i, 0))
    widths = (D_MODEL, V_WIDTH, FN_WIDTH, D_MODEL, D_MODEL)
    in_first = [pl.BlockSpec((TM, w), lambda i: (jnp.minimum(i, n_first - 1), 0)) for w in widths]
    in_second = [pl.BlockSpec((TM, w), lambda i: (jnp.maximum(i - n_first, 0), 0)) for w in widths]
    return pl.pallas_call(
        functools.partial(_merge_kernel, n_first=n_first),
        grid=(n // TM,),
        in_specs=in_first + in_second + [
                  pl.BlockSpec((1, 6, D_MODEL), lambda i: (mod_row_fn(i), 0, 0)),
                  _const_spec((V_WIDTH, D_MODEL)),
                  _const_spec((FN_WIDTH, D_MODEL)),
                  _const_spec((D_MODEL, D_MODEL)),
                  _const_spec((1, D_MODEL)),
                  _const_spec((ROUTER_ROWS, D_MODEL)),
                  _const_spec((ROUTER_ROWS, 1))],
        out_specs=[tok(D_MODEL),
                   pl.BlockSpec((TM * TOKEN_ROWS, LANES), lambda i: (i, 0)),
                   pl.BlockSpec((1, TM), lambda i: (0, i)),
                   pl.BlockSpec((1, TM), lambda i: (0, i)), _const_spec((SEG_ROWS, LANES))],
        out_shape=[jax.ShapeDtypeStruct((n, D_MODEL), F32),
                   jax.ShapeDtypeStruct((n * TOKEN_ROWS, LANES), F32),
                   jax.ShapeDtypeStruct((1, n), jnp.int32),
                   jax.ShapeDtypeStruct((1, n), jnp.int32),
                   jax.ShapeDtypeStruct((SEG_ROWS, LANES), jnp.int32)],
        scratch_shapes=[pltpu.VMEM((SEG_ROWS, LANES), F32)],
        compiler_params=_cparams(("arbitrary",)),
        name="merge",
    )(*first, *second, mod3, wts["w_ao"], wts["w_fn"], wts["w_out"], wts["g2"],
      wts["w_rg"], wts["b_rg"])


def _token_rows(ref, t):
    start = t * TOKEN_ROWS
    if not isinstance(t, int):
        start = pl.multiple_of(start, TOKEN_ROWS)
    return ref.at[pl.ds(start, TOKEN_ROWS)]


def _start_row_copy(src, dst, src_tok, dst_tok, sem, queue=0):
    pltpu.async_copy(_token_rows(src, src_tok), _token_rows(dst, dst_tok), sem, priority=queue)


def _wait_tile(src_hbm, buf, sem):
    pltpu.make_async_copy(src_hbm.at[pl.ds(0, buf.shape[0])], buf, sem).wait()


def _combine_kernel(pos_ref, y_hbm, x1_ref, mod_ref, o1_ref, o2_ref, buf, sem, *, n_first):
    i = pl.program_id(0)
    n_tiles = pl.num_programs(0)
    tm = o1_ref.shape[0]

    def start(tile_idx, r):
        slot = tile_idx % (GATHER_AHEAD + 1)
        queue = r % DMA_QUEUES if isinstance(r, int) else 0
        _start_row_copy(y_hbm, buf.at[slot], pos_ref[tile_idx * tm + r], r, sem.at[slot], queue)

    for first in range(GATHER_AHEAD):
        @pl.when(i == 0)
        def _():
            def body(r, c):
                start(first, r)
                return c

            lax.fori_loop(0, tm, body, 0, unroll=32)

    def run(fetch_ahead, o_ref):
        slot = i % (GATHER_AHEAD + 1)
        _wait_tile(y_hbm, buf.at[slot], sem.at[slot])
        lane_tiles = [slice(s * LANES, (s + 1) * LANES) for s in range(TOKEN_ROWS)]
        for s, cols in enumerate(lane_tiles):
            o_ref[:, cols] = mod_ref[0, 5:6, cols] * buf.at[slot][pl.ds(s, tm, stride=TOKEN_ROWS), :]
        per = tm // TOKEN_ROWS
        for s, cols in enumerate(lane_tiles):
            o_ref[:, cols] = x1_ref[:, cols] + o_ref[:, cols]
            if fetch_ahead:
                for r in range(s * per, (s + 1) * per):
                    start(i + GATHER_AHEAD, r)

    fetch = i + GATHER_AHEAD < n_tiles
    for fetch_ahead in (True, False):
        cond = fetch if fetch_ahead else jnp.logical_not(fetch)
        for o_ref, mine in ((o1_ref, i < n_first), (o2_ref, i >= n_first)):
            @pl.when(jnp.logical_and(cond, mine))
            def _():
                run(fetch_ahead, o_ref)


def _combine(y_sorted, pos, x1, mod3, mod_row_fn, n_first_tokens):
    n = x1.shape[0]
    n_first = n_first_tokens // TM
    assert n // TM >= GATHER_AHEAD
    return pl.pallas_call(
        functools.partial(_combine_kernel, n_first=n_first),
        grid_spec=pltpu.PrefetchScalarGridSpec(
            num_scalar_prefetch=1,
            grid=(n // TM,),
            in_specs=[pl.BlockSpec(memory_space=pl.ANY),
                      pl.BlockSpec((TM, D_MODEL), lambda i, pos: (i, 0)),
                      pl.BlockSpec((1, 6, D_MODEL), lambda i, pos: (mod_row_fn(i), 0, 0))],
            out_specs=[pl.BlockSpec((TM, D_MODEL), lambda i, pos: (jnp.minimum(i, n_first - 1), 0)),
                       pl.BlockSpec((TM, D_MODEL), lambda i, pos: (jnp.maximum(i - n_first, 0), 0))],
            scratch_shapes=[pltpu.VMEM((GATHER_AHEAD + 1, TM * TOKEN_ROWS, LANES), F32),
                            pltpu.SemaphoreType.DMA((GATHER_AHEAD + 1,))]),
        out_shape=[jax.ShapeDtypeStruct((n_first_tokens, D_MODEL), F32),
                   jax.ShapeDtypeStruct((n - n_first_tokens, D_MODEL), F32)],
        compiler_params=_cparams(("arbitrary",)),
        name="moe_combine",
    )(pos, y_sorted, x1, mod3)


def _moe_kernel(grp_ref, on_ref, src_ref, h_hbm, wrg_ref, brg_ref, wre_ref, bre_ref, wg_ref, wu_ref,
                wd_ref, o_ref, hbuf, sem, *, tile):
    i = pl.program_id(0)
    n_tiles = pl.num_programs(0)

    def start(tile_idx, r):
        slot = tile_idx % (GATHER_AHEAD + 1)
        queue = r % DMA_QUEUES if isinstance(r, int) else 0
        _start_row_copy(h_hbm, hbuf.at[slot], src_ref[tile_idx * tile + r], r, sem.at[slot], queue)

    for first in range(GATHER_AHEAD):
        @pl.when(jnp.logical_and(i == 0, on_ref[first] == 1))
        def _():
            def body(r, c):
                start(first, r)
                return c

            lax.fori_loop(0, tile, body, 0, unroll=32)

    def run(fetch_ahead):
        def issue(chunk, n_chunks):
            if fetch_ahead:
                per = tile // n_chunks
                for r in range(chunk * per, (chunk + 1) * per):
                    start(i + GATHER_AHEAD, r)

        slot = i % (GATHER_AHEAD + 1)
        _wait_tile(h_hbm, hbuf.at[slot], sem.at[slot])
        h2_hi, h2_lo = _split_bf16(_load_token_major(hbuf.at[slot], tile))

        logits = _router_logits(jnp.concatenate([wrg_ref[...], wre_ref[0]], axis=0),
                                jnp.concatenate([brg_ref[...], bre_ref[0]], axis=0), h2_hi, h2_lo)
        g = _rows(logits, N_GROUPS)
        gmax = functools.reduce(jnp.maximum, g)
        p_top = 1.0 / functools.reduce(lambda p, q: p + q, [jnp.exp(gj - gmax) for gj in g])
        e = _rows(logits[ROUTER_ROWS:], EXPERTS_PER_GROUP)
        m1 = functools.reduce(jnp.maximum, e)
        i1 = _first_argmax(e, m1)
        rest = [jnp.where(i1 == j, -jnp.inf, e[j]) for j in range(EXPERTS_PER_GROUP)]
        m2 = functools.reduce(jnp.maximum, rest)
        i2 = _first_argmax(rest, m2)
        t = jnp.exp(m2 - m1)
        w1 = p_top / (1.0 + t)
        w2 = p_top * t / (1.0 + t)
        row = lax.broadcasted_iota(jnp.int32, (LANES, tile), 0)
        comb = (jnp.where(row == i1, w1, 0.0) + jnp.where(row == i2, w2, 0.0)).T

        gates, ups = [], []
        for j in range(EXPERTS_PER_GROUP):
            gates.append(_dot(h2_hi, wg_ref[j]))
            issue(2 * j, 2 * EXPERTS_PER_GROUP)
            ups.append(_dot(h2_hi, wu_ref[j]))
            issue(2 * j + 1, 2 * EXPERTS_PER_GROUP)
        acts = [((a * _sigmoid(a)) * u * comb[:, j:j + 1]).astype(BF16)
                for j, (a, u) in enumerate(zip(gates, ups))]
        acc = functools.reduce(lambda p, q: p + q,
                               [_dot(acts[j], wd_ref[j]) for j in range(EXPERTS_PER_GROUP)])
        _store_token_major(o_ref, acc, tile)

    @pl.when(on_ref[i] == 0)
    def _():
        o_ref[...] = jnp.zeros_like(o_ref)

    ahead = jnp.minimum(i + GATHER_AHEAD, n_tiles - 1)
    fetch = jnp.logical_and(i + GATHER_AHEAD < n_tiles, on_ref[ahead] == 1)

    @pl.when(jnp.logical_and(on_ref[i] == 1, fetch))
    def _():
        run(True)

    @pl.when(jnp.logical_and(on_ref[i] == 1, jnp.logical_not(fetch)))
    def _():
        run(False)


def _moe(h2tm, src, maps, wts, w_gate, w_up, w_down, tile):
    n_tiles = src.shape[0] // tile
    const = lambda shape: pl.BlockSpec(shape, lambda i, grp, on, src: (0,) * len(shape))
    by_group = lambda shape: pl.BlockSpec(
        shape, lambda i, grp, on, src: (grp[i],) + (0,) * (len(shape) - 1))
    return pl.pallas_call(
        functools.partial(_moe_kernel, tile=tile),
        grid_spec=pltpu.PrefetchScalarGridSpec(
            num_scalar_prefetch=3,
            grid=(n_tiles,),
            in_specs=[pl.BlockSpec(memory_space=pl.ANY),
                      const((ROUTER_ROWS, D_MODEL)), const((ROUTER_ROWS, 1)),
                      by_group((1, ROUTER_ROWS, D_MODEL)), by_group((1, ROUTER_ROWS, 1)),
                      by_group((EXPERTS_PER_GROUP, D_MODEL, D_EXPERT)),
                      by_group((EXPERTS_PER_GROUP, D_MODEL, D_EXPERT)),
                      by_group((EXPERTS_PER_GROUP, D_EXPERT, D_MODEL))],
            out_specs=pl.BlockSpec((tile * TOKEN_ROWS, LANES), lambda i, grp, on, src: (i, 0)),
            scratch_shapes=[pltpu.VMEM((GATHER_AHEAD + 1, tile * TOKEN_ROWS, LANES), F32),
                            pltpu.SemaphoreType.DMA((GATHER_AHEAD + 1,))]),
        out_shape=jax.ShapeDtypeStruct((n_tiles * tile * TOKEN_ROWS, LANES), F32),
        compiler_params=_cparams(("arbitrary",)),
        name="moe",
    )(*maps, src, h2tm, wts["w_rg"], wts["b_rg"], wts["w_re"], wts["b_re"], w_gate, w_up, w_down)


def _invert_kernel(pos_ref, lo_ref, hi_ref, src_ref, *, n, n_fill):
    def zero(p, c):
        src_ref[p] = 0
        return c

    for s in range(n_fill):
        lax.fori_loop(lo_ref[s], hi_ref[s], zero, 0)

    def put(t, c):
        src_ref[pos_ref[t]] = t
        return c

    lax.fori_loop(0, n, put, 0, unroll=8)


def _invert(pos, fill_lo, fill_hi, n_sorted):
    return pl.pallas_call(
        functools.partial(_invert_kernel, n=pos.shape[0], n_fill=fill_lo.shape[0]),
        grid_spec=pltpu.PrefetchScalarGridSpec(
            num_scalar_prefetch=3, grid=(1,), in_specs=[],
            out_specs=pl.BlockSpec(memory_space=pltpu.SMEM)),
        out_shape=jax.ShapeDtypeStruct((n_sorted,), jnp.int32),
        name="moe_invert",
    )(pos, fill_lo, fill_hi)


def _sort_plan(grp, rank, cnt, tile, n_tiles):
    n_tile_grp = (cnt + tile - 1) // tile
    tile_end = jnp.cumsum(n_tile_grp)
    tile_start = tile_end - n_tile_grp
    total = tile_end[-1]
    pos = ((tile_start * tile)[grp] + rank).astype(jnp.int32)
    fill_lo = jnp.concatenate([tile_start * tile + cnt, total[None] * tile]).astype(jnp.int32)
    fill_hi = jnp.concatenate([tile_end * tile, jnp.full((1,), n_tiles * tile)]).astype(jnp.int32)
    src = _invert(pos, fill_lo, fill_hi, n_tiles * tile)
    i = jnp.arange(n_tiles, dtype=jnp.int32)
    tile_grp = jnp.sum((jnp.minimum(i, total - 1)[:, None] >= tile_end[None, :]).astype(jnp.int32),
                       axis=1)
    return pos, src, (tile_grp.astype(jnp.int32), (i < total).astype(jnp.int32))


def _sparse_moe(x1, h2tm, grp, rank, cnt8, mod3, mod_row_fn, wts, w_gate, w_up, w_down, tile,
                n_first_tokens):
    n = x1.shape[0]
    n_tiles = n // tile + N_GROUPS
    pos, src, maps = _sort_plan(grp.reshape(n), rank.reshape(n), cnt8[:N_GROUPS, 0], tile, n_tiles)
    y_sorted = _moe(h2tm, src, maps, wts, w_gate, w_up, w_down, tile)
    return _combine(y_sorted, pos, x1, mod3, mod_row_fn, n_first_tokens)


def _pack_weights(l, norm1_g, q_a_norm_g, w_q_b, kv_a_norm_g, w_kv_b, q_norm_g, k_norm_g,
                  w_attn_o, w_fnet, w_out, norm2_g, w_router_group, b_router_group,
                  w_router_expert, b_router_expert):
    w_qb = jnp.pad(w_q_b[l].reshape(Q_LORA, N_HEADS, QK_HEAD),
                   ((0, 0), (0, 0), (0, HEAD_SLOT - QK_HEAD))).reshape(Q_LORA, QK_WIDTH)
    wkv = w_kv_b[l].reshape(KV_LORA, N_HEADS, QK_NOPE + V_HEAD)
    w_kb = jnp.pad(wkv[:, :, :QK_NOPE],
                   ((0, 0), (0, 0), (0, HEAD_SLOT - QK_NOPE))).reshape(KV_LORA, QK_WIDTH)
    w_vb = wkv[:, :, QK_NOPE:].reshape(KV_LORA, V_WIDTH)
    pad_g = lambda g: jnp.pad(g, (0, HEAD_SLOT - QK_HEAD)).reshape(1, HEAD_SLOT)
    w_rg = jnp.pad(w_router_group[l].T, ((0, ROUTER_ROWS - N_GROUPS), (0, 0)))
    b_rg = jnp.pad(b_router_group[l], (0, ROUTER_ROWS - N_GROUPS)).reshape(ROUTER_ROWS, 1)
    w_re = jnp.pad(w_router_expert[l].T.reshape(N_GROUPS, EXPERTS_PER_GROUP, D_MODEL),
                   ((0, 0), (0, ROUTER_ROWS - EXPERTS_PER_GROUP), (0, 0)))
    b_re = jnp.pad(b_router_expert[l].reshape(N_GROUPS, EXPERTS_PER_GROUP),
                   ((0, 0), (0, ROUTER_ROWS - EXPERTS_PER_GROUP))).reshape(N_GROUPS, ROUTER_ROWS, 1)
    dft_c, dft_ns = _dft_tables(FN_GROUP_W)
    return {
        "g1": norm1_g[l].reshape(1, D_MODEL),
        "qag": q_a_norm_g[l].reshape(1, Q_LORA),
        "w_qb": w_qb.astype(BF16),
        "kvg": kv_a_norm_g[l].reshape(1, KV_LORA),
        "w_kb": w_kb.astype(BF16),
        "w_vb": w_vb.T.astype(BF16),
        "qg": pad_g(q_norm_g[l]),
        "kg": pad_g(k_norm_g[l]),
        "dft_c": jnp.concatenate([jnp.asarray(dft_c), -jnp.asarray(dft_ns)], axis=1).astype(BF16),
        "w_ao": w_attn_o[l].astype(BF16),
        "w_fn": w_fnet[l].astype(BF16),
        "w_out": w_out[l].astype(BF16),
        "g2": norm2_g[l].reshape(1, D_MODEL),
        "w_rg": w_rg,
        "b_rg": b_rg,
        "w_re": w_re,
        "b_re": b_re,
    }


def _layer(xp, xs, cache_ckv_l, cache_kpe_l, mod3, wts, experts):
    bp, sp, _ = xp.shape
    bs, ss, _ = xs.shape
    ctx_row = lambda i: 0
    lat_row = lambda i: 1 + i // (ss // TM)

    xp2 = xp.reshape(bp * sp, D_MODEL)
    ride = (bp * sp) // TM == N_EXPERTS
    cs, ns = (jnp.asarray(t).astype(BF16) for t in _dft_tables(sp))
    outs = _inproj(xp2, mod3, ctx_row, wts, None, True, experts if ride else (), (sp, cs, ns))
    attn, fm, sga, sgf, ckv, kpe = outs[:6]
    w_gate, w_up, w_down = outs[6:] if ride else (w.astype(BF16) for w in experts)
    ctx_set = (xp2, attn, fm, sga, sgf)

    xs2 = xs.reshape(bs * ss, D_MODEL)
    rope_tabs = tuple(jnp.asarray(t) for t in _rope_tables(ss))
    q, k, v, fcs, sga, sgf = _inproj(xs2, mod3, lat_row, wts, rope_tabs, False)
    kpe_slot = jnp.pad(cache_kpe_l, ((0, 0), (0, 0), (QK_NOPE, LANES - QK_HEAD)))
    cs, ns = (jnp.asarray(t).astype(BF16) for t in _dft_tables(ss))
    attn, fm = _attention(q.reshape(bs, ss, QK_WIDTH), k.reshape(bs, ss, QK_WIDTH), v, cache_ckv_l,
                          kpe_slot, wts, fcs.reshape(bs, ss, 2 * FN_WIDTH), cs, ns, TQ_LAT, "attn_lat")
    lat_set = (xs2, attn.reshape(bs * ss, V_WIDTH), fm.reshape(bs * ss, FN_WIDTH), sga, sgf)

    n_ctx_tiles = (bp * sp) // TM
    all_row = lambda i: jnp.where(i < n_ctx_tiles, 0, lat_row(i - n_ctx_tiles))
    x1, h2tm, grp, rank, cnt = _merge(ctx_set, lat_set, mod3, all_row, wts)
    yp, ys = _sparse_moe(x1, h2tm, grp, rank, cnt, mod3, all_row, wts, w_gate, w_up, w_down, MOE_TILE,
                         bp * sp)
    yp, ys = yp.reshape(bp, sp, D_MODEL), ys.reshape(bs, ss, D_MODEL)

    return yp, ys, ckv.reshape(bp, sp, KV_LORA), jnp.swapaxes(kpe, 1, 2)


def kernel(x_prompt, x_sample, cache_ckv, cache_kpe, c, c_ctx, w_mod, b_mod, norm1_g, w_in, q_a_norm_g, w_q_b, kv_a_norm_g, w_kv_b, q_norm_g, k_norm_g, w_attn_o, w_fnet, w_out, norm2_g, w_router_group, b_router_group, w_router_expert, b_router_expert, w_exp_gate, w_exp_up, w_exp_down):
    depth = w_mod.shape[0]
    n_lat = c.shape[0]
    assert 1 + n_lat <= MOD_ROWS
    cond8 = jnp.concatenate([c_ctx[None, :], c, jnp.zeros((MOD_ROWS - 1 - n_lat, D_MODEL), F32)], axis=0)
    xp, xs = x_prompt, x_sample
    ckv_layers, kpe_layers = [], []
    for l in range(depth):
        mod, w_in_p = _adaln(cond8, w_mod[l], b_mod[l].reshape(1, -1), jnp.swapaxes(w_in[l], 0, 1))
        mod3 = mod.reshape(MOD_ROWS, 6, D_MODEL)
        wts = _pack_weights(l, norm1_g, q_a_norm_g, w_q_b, kv_a_norm_g, w_kv_b, q_norm_g,
                            k_norm_g, w_attn_o, w_fnet, w_out, norm2_g, w_router_group,
                            b_router_group, w_router_expert, b_router_expert)
        wts["w_in"] = w_in_p
        xp, xs, ckv, kpe = _layer(xp, xs, cache_ckv[:, l], cache_kpe[:, l], mod3, wts,
                                  (w_exp_gate[l], w_exp_up[l], w_exp_down[l]))
        ckv_layers.append(ckv)
        kpe_layers.append(kpe)
    return xp, xs, jnp.stack(ckv_layers, axis=1), jnp.stack(kpe_layers, axis=1)
```

```python
import functools
import math

import numpy as np
import jax
import jax.numpy as jnp
from jax import lax
from jax.experimental import pallas as pl
from jax.experimental.pallas import tpu as pltpu

D_MODEL = 1024
GRID_W = 64
N_HEADS = 8
Q_LORA = 512
KV_LORA = 256
QK_NOPE = 64
QK_ROPE = 32
V_HEAD = 64
QK_HEAD = QK_NOPE + QK_ROPE
ATTN_SCALE = QK_HEAD ** -0.5
ROPE_BASE = 10000.0
FN_GROUPS = 4
FN_GROUP_W = 128
FN_WIDTH = FN_GROUPS * FN_GROUP_W
N_GROUPS = 4
EXPERTS_PER_GROUP = 4
N_EXPERTS = N_GROUPS * EXPERTS_PER_GROUP
D_EXPERT = 512
EPS = 1e-6

LANES = 128
HEAD_SLOT = LANES
QK_WIDTH = N_HEADS * HEAD_SLOT
V_WIDTH = N_HEADS * V_HEAD
C_QA = 0
C_KVA = C_QA + Q_LORA
C_KPE = C_KVA + KV_LORA
C_FN = C_KPE + LANES
C_GA = C_FN + FN_WIDTH
C_GF = C_GA + D_MODEL
IN_PACKED = C_GF + D_MODEL
SUBLANES = 8
ROUTER_ROWS = SUBLANES
MOD_ROWS = SUBLANES
VMEM_LIMIT = 56 * 1024 * 1024

TM = 512
TQ_LAT = 256
SEG_ROWS = SUBLANES
TOKEN_ROWS = D_MODEL // LANES
MOE_TILE = 256
MERGE_SPLIT = 2
DMA_QUEUES = 2
GATHER_AHEAD = 2
ADALN_STEPS = 8

BF16 = jnp.bfloat16
F32 = jnp.float32


def _cparams(sem):
    return pltpu.CompilerParams(dimension_semantics=sem, vmem_limit_bytes=VMEM_LIMIT)


def _dot(a, b):
    return jnp.dot(a, b, preferred_element_type=F32)


def _dot_nt(a, b):
    return lax.dot_general(a, b, (((1,), (1,)), ((), ())), preferred_element_type=F32)


def _sigmoid(x):
    return 1.0 / (1.0 + jnp.exp(-x))


def _split_bf16(x):
    hi = x.astype(BF16)
    return hi, (x - hi.astype(F32)).astype(BF16)


@functools.lru_cache(maxsize=None)
def _rope_tables(n_pos):
    half = QK_ROPE // 2
    quarter = half // 2
    freqs = ROPE_BASE ** (-np.arange(quarter, dtype=np.float64) / quarter)
    pos = np.arange(n_pos)
    row = (pos // GRID_W).astype(np.float64)
    col = (pos % GRID_W).astype(np.float64)
    cos_t = np.ones((n_pos, LANES), np.float64)
    sin_a = np.zeros((n_pos, LANES), np.float64)
    sin_b = np.zeros((n_pos, LANES), np.float64)
    for base, p in ((QK_NOPE, row), (QK_NOPE + half, col)):
        ang = p[:, None] * freqs[None, :]
        cos_t[:, base:base + quarter] = np.cos(ang)
        cos_t[:, base + quarter:base + half] = np.cos(ang)
        sin_a[:, base:base + quarter] = -np.sin(ang)
        sin_b[:, base + quarter:base + half] = np.sin(ang)
    return (cos_t.astype(np.float32), sin_a.astype(np.float32), sin_b.astype(np.float32))


@functools.lru_cache(maxsize=None)
def _dft_tables(n):
    k = np.arange(n)
    ang = 2.0 * np.pi * ((k[:, None] * k[None, :]) % n) / n
    s = 1.0 / math.sqrt(n)
    return (np.cos(ang) * s).astype(np.float32), (-np.sin(ang) * s).astype(np.float32)


def _adaln_kernel(cond_ref, w_ref, b_ref, win_ref, o_ref, winp_ref):
    c = cond_ref[...]
    s_hi, s_lo = _split_bf16(c * _sigmoid(c))
    w_hi, w_lo = _split_bf16(w_ref[...])
    y = _dot(jnp.concatenate([s_hi, s_lo], axis=0), w_hi)
    o_ref[...] = (y[:MOD_ROWS] + y[MOD_ROWS:]) + _dot(s_hi, w_lo) + b_ref[...]

    winp_ref[:C_KPE, :] = win_ref[:C_KPE, :].astype(BF16)
    winp_ref[C_KPE:C_FN, :] = jnp.zeros((LANES, winp_ref.shape[1]), BF16)
    winp_ref[C_KPE + QK_NOPE:C_KPE + QK_HEAD, :] = win_ref[C_KPE:C_KPE + QK_ROPE, :].astype(BF16)
    winp_ref[C_FN:, :] = win_ref[C_KPE + QK_ROPE:, :].astype(BF16)


def _adaln(cond8, w_mod, b_mod, w_in_t):
    n = w_mod.shape[1]
    steps = ADALN_STEPS
    tn, tr = n // steps, D_MODEL // steps
    return pl.pallas_call(
        _adaln_kernel,
        grid=(steps,),
        in_specs=[pl.BlockSpec((MOD_ROWS, D_MODEL), lambda j: (0, 0)),
                  pl.BlockSpec((D_MODEL, tn), lambda j: (0, j)),
                  pl.BlockSpec((1, tn), lambda j: (0, j)),
                  pl.BlockSpec((w_in_t.shape[0], tr), lambda j: (0, j))],
        out_specs=[pl.BlockSpec((MOD_ROWS, tn), lambda j: (0, j)),
                   pl.BlockSpec((IN_PACKED, tr), lambda j: (0, j))],
        out_shape=[jax.ShapeDtypeStruct((MOD_ROWS, n), F32),
                   jax.ShapeDtypeStruct((IN_PACKED, D_MODEL), BF16)],
        compiler_params=_cparams(("arbitrary",)),
        name="adaln",
    )(cond8, w_mod, b_mod, w_in_t)


def _rms(x, width):
    return lax.rsqrt(jnp.sum(x * x, axis=-1, keepdims=True) * (1.0 / width) + EPS)


def _rope(x, cos_t, sin_a, sin_b):
    return x * cos_t + pltpu.roll(x, LANES - 8, 1) * sin_a + pltpu.roll(x, 8, 1) * sin_b


def _inproj_kernel(*refs, rope, emit_cache, n_cast, seq_len):
    it = iter(refs)
    x_ref, mod_ref, g1_ref, win_ref, qag_ref, wqb_ref, kvg_ref, wkb_ref, wvb_ref = (
        next(it) for _ in range(9))
    qg_ref, kg_ref, dft_ref = next(it), next(it), next(it)
    if rope:
        cos_ref, sa_ref, sb_ref = next(it), next(it), next(it)
    if seq_len:
        seq_cs_ref, seq_ns_ref = next(it), next(it)
    cast_in = [next(it) for _ in range(n_cast)]
    if seq_len:
        attn_ref, fm_ref, sga_ref, sgf_ref = (next(it) for _ in range(4))
    else:
        q_ref, k_ref, v_ref, fcs_ref, sga_ref, sgf_ref = (next(it) for _ in range(6))
    if emit_cache:
        ckv_ref, kpe_ref = next(it), next(it)
    cast_out = [next(it) for _ in range(n_cast)]
    if seq_len:
        q_ref, k_ref, v_ref, fcs_ref = (next(it) for _ in range(4))

    for src, dst in zip(cast_in, cast_out):
        dst[...] = src[...].astype(BF16)

    x = x_ref[...]
    shift = mod_ref[0, 0:1, :]
    scale = mod_ref[0, 1:2, :]
    h = (x * _rms(x, D_MODEL) * g1_ref[...]) * (1.0 + scale) + shift
    hb = h.astype(BF16)

    if rope:
        cos_t, sin_a, sin_b = cos_ref[...], sa_ref[...], sb_ref[...]

    qa = _dot_nt(hb, win_ref[C_QA:C_QA + Q_LORA, :])
    qn = (qa * _rms(qa, Q_LORA) * qag_ref[...]).astype(BF16)
    q = _dot(qn, wqb_ref[...])
    qg = qg_ref[...] * ATTN_SCALE
    for hd in range(N_HEADS):
        qh = q[:, hd * HEAD_SLOT:(hd + 1) * HEAD_SLOT]
        qh = qh * _rms(qh, QK_HEAD) * qg
        if rope:
            qh = _rope(qh, cos_t, sin_a, sin_b)
        q_ref[:, hd * HEAD_SLOT:(hd + 1) * HEAD_SLOT] = qh.astype(BF16)

    kva = _dot_nt(hb, win_ref[C_KVA:C_KVA + KV_LORA, :])
    ckv = kva * _rms(kva, KV_LORA) * kvg_ref[...]
    kpe = _dot_nt(hb, win_ref[C_KPE:C_KPE + LANES, :])
    if emit_cache:
        ckv_ref[...] = ckv
        kpe_t = kpe.T[QK_NOPE:QK_NOPE + QK_ROPE, :]
        for j in range(x.shape[0] // seq_len):
            kpe_ref[j] = kpe_t[:, j * seq_len:(j + 1) * seq_len]
    _emit_kv(ckv.astype(BF16), kpe, wkb_ref, wvb_ref, kg_ref,
             (cos_t, sin_a, sin_b) if rope else None, k_ref, v_ref)

    fn = _dot_nt(hb, win_ref[C_FN:C_FN + FN_WIDTH, :]).astype(BF16)
    for g in range(FN_GROUPS):
        cs = _dot(fn[:, g * FN_GROUP_W:(g + 1) * FN_GROUP_W], dft_ref[...])
        fcs_ref[:, g * FN_GROUP_W:(g + 1) * FN_GROUP_W] = cs[:, :FN_GROUP_W].astype(BF16)
        fcs_ref[:, FN_WIDTH + g * FN_GROUP_W:FN_WIDTH + (g + 1) * FN_GROUP_W] = (
            cs[:, FN_GROUP_W:].astype(BF16))

    def gate(ref, c0, half):
        def run():
            w = D_MODEL // 2
            ref[:, pl.ds(half * w, w)] = _sigmoid(
                _dot_nt(hb, win_ref[pl.ds(c0 + half * w, w), :])).astype(BF16)
        return run

    gates = [gate(ref, c0, half) for ref, c0 in ((sga_ref, C_GA), (sgf_ref, C_GF)) for half in range(2)]

    if seq_len:
        n_seq = x.shape[0] // seq_len
        for j in range(n_seq):
            rows = pl.ds(j * seq_len, seq_len)
            fill = gates[j * len(gates) // n_seq:(j + 1) * len(gates) // n_seq]
            attn_ref[rows, :] = _attend(q_ref.at[rows], [k_ref.at[rows]], [v_ref.at[:, rows]], fill)
            fm_ref[rows, :] = (_dot(seq_cs_ref[...], fcs_ref[rows, :FN_WIDTH])
                               + _dot(seq_ns_ref[...], fcs_ref[rows, FN_WIDTH:])).astype(BF16)
    else:
        for run_gate in gates:
            run_gate()


def _emit_kv(ckvb, kpe, wkb_ref, wvb_ref, kg_ref, rope_tabs, k_ref, v_ref):
    kg = kg_ref[...]
    v_ref[...] = _dot_nt(wvb_ref[...], ckvb).astype(BF16)
    kn = _dot(ckvb, wkb_ref[...])
    pe_ss = jnp.sum(kpe * kpe, axis=-1, keepdims=True)
    pe_g = kpe * kg
    if rope_tabs is not None:
        pe_g = _rope(pe_g, *rope_tabs)
    for hd in range(N_HEADS):
        knh = kn[:, hd * HEAD_SLOT:(hd + 1) * HEAD_SLOT]
        ss = jnp.sum(knh * knh, axis=-1, keepdims=True) + pe_ss
        r = lax.rsqrt(ss * (1.0 / QK_HEAD) + EPS)
        k_ref[:, hd * HEAD_SLOT:(hd + 1) * HEAD_SLOT] = ((knh * kg + pe_g) * r).astype(BF16)


def _const_spec(shape):
    return pl.BlockSpec(shape, lambda i: (0,) * len(shape))


def _inproj(x2d, mod3, mod_row_fn, wts, rope_tabs, emit_cache, cast=(), seq=None):
    n = x2d.shape[0]
    assert all(w.shape[0] == n // TM for w in cast)
    assert seq is None or TM % seq[0] == 0
    rope = rope_tabs is not None
    tiles_per_seq = None if not rope else rope_tabs[0].shape[0] // TM
    in_specs = [pl.BlockSpec((TM, D_MODEL), lambda i: (i, 0)),
                pl.BlockSpec((1, 6, D_MODEL), lambda i: (mod_row_fn(i), 0, 0)),
                _const_spec((1, D_MODEL)),
                _const_spec((IN_PACKED, D_MODEL)),
                _const_spec((1, Q_LORA)),
                _const_spec((Q_LORA, QK_WIDTH)),
                _const_spec((1, KV_LORA)),
                _const_spec((KV_LORA, QK_WIDTH)),
                _const_spec((V_WIDTH, KV_LORA)),
                _const_spec((1, HEAD_SLOT)),
                _const_spec((1, HEAD_SLOT)),
                _const_spec((FN_GROUP_W, 2 * FN_GROUP_W))]
    args = [x2d, mod3, wts["g1"], wts["w_in"], wts["qag"], wts["w_qb"], wts["kvg"],
            wts["w_kb"], wts["w_vb"], wts["qg"], wts["kg"], wts["dft_c"]]
    if rope:
        in_specs += [pl.BlockSpec((TM, LANES), lambda i: (i % tiles_per_seq, 0))] * 3
        args += list(rope_tabs)
    if seq is not None:
        in_specs += [_const_spec((seq[0], seq[0]))] * 2
        args += [seq[1], seq[2]]
    cast_specs = [pl.BlockSpec((1,) + w.shape[1:], lambda i: (i, 0, 0)) for w in cast]
    in_specs += cast_specs
    args += list(cast)
    qkvf = [((TM, QK_WIDTH), (n, QK_WIDTH)), ((TM, QK_WIDTH), (n, QK_WIDTH)),
            ((V_WIDTH, TM), (V_WIDTH, n)), ((TM, 2 * FN_WIDTH), (n, 2 * FN_WIDTH))]
    if seq is None:
        widths = [None] * 4 + [D_MODEL, D_MODEL]
    else:
        widths = [V_WIDTH, FN_WIDTH, D_MODEL, D_MODEL]
    out_shape, out_specs = [], []
    for j, w in enumerate(widths):
        if w is None:
            out_shape.append(jax.ShapeDtypeStruct(qkvf[j][1], BF16))
            out_specs.append(pl.BlockSpec(qkvf[j][0], (lambda i: (0, i)) if j == 2 else (lambda i: (i, 0))))
        else:
            out_shape.append(jax.ShapeDtypeStruct((n, w), BF16))
            out_specs.append(pl.BlockSpec((TM, w), lambda i: (i, 0)))
    if emit_cache:
        assert seq is not None
        out_shape += [jax.ShapeDtypeStruct((n, KV_LORA), F32),
                      jax.ShapeDtypeStruct((n // seq[0], QK_ROPE, seq[0]), F32)]
        out_specs += [pl.BlockSpec((TM, KV_LORA), lambda i: (i, 0)),
                      pl.BlockSpec((TM // seq[0], QK_ROPE, seq[0]), lambda i: (i, 0, 0))]
    out_shape += [jax.ShapeDtypeStruct(w.shape, BF16) for w in cast]
    out_specs += cast_specs
    scratch = [] if seq is None else [pltpu.VMEM(blk, BF16) for blk, _ in qkvf]
    return pl.pallas_call(
        functools.partial(_inproj_kernel, rope=rope, emit_cache=emit_cache, n_cast=len(cast),
                          seq_len=None if seq is None else seq[0]),
        grid=(n // TM,),
        in_specs=in_specs,
        out_specs=out_specs,
        out_shape=out_shape,
        scratch_shapes=scratch,
        compiler_params=_cparams(("parallel",)),
        name="inproj_lat" if rope else "inproj_ctx",
    )(*args)


def _attend(q, ks, vts, fillers=()):
    head = lambda hd: slice(hd * HEAD_SLOT, (hd + 1) * HEAD_SLOT)
    st = [jnp.stack([_dot_nt(k[:, head(hd)], q[:, head(hd)]) for hd in range(N_HEADS)]) for k in ks]
    m = functools.reduce(jnp.maximum, [sj.max(axis=1, keepdims=True) for sj in st])
    fillers = list(fillers)
    every = N_HEADS // max(1, len(fillers))
    p = []
    for hd in range(N_HEADS):
        p.append([jnp.exp(sj[hd] - m[hd]) for sj in st])
        if fillers and hd % every == 0:
            fillers.pop(0)()
    assert not fillers
    outs = []
    for hd in range(N_HEADS):
        l = functools.reduce(lambda a, b: a + b, [pj.sum(axis=0, keepdims=True) for pj in p[hd]])
        o = functools.reduce(lambda a, b: a + b,
                             [_dot(vt[hd * V_HEAD:(hd + 1) * V_HEAD, :], pj.astype(BF16))
                              for vt, pj in zip(vts, p[hd])])
        outs.append(o / l)
    return jnp.concatenate(outs, axis=0).T.astype(BF16)


def _attn_kernel(q_ref, k_ref, vt_ref, cckv_ref, ckpe_ref, wkb_ref, wvb_ref, kg_ref, cs_ref, ns_ref,
                 f_ref, o_ref, fm_ref, kc_ref, vtc_ref):
    @pl.when(pl.program_id(1) == 0)
    def _():
        _emit_kv(cckv_ref[0].astype(BF16), ckpe_ref[0], wkb_ref, wvb_ref, kg_ref, None, kc_ref, vtc_ref)

    def dft(c0, width):
        def run():
            fm_ref[0, :, pl.ds(c0, width)] = (
                _dot(cs_ref[...], f_ref[0, :, pl.ds(c0, width)])
                + _dot(ns_ref[...], f_ref[0, :, pl.ds(FN_WIDTH + c0, width)])).astype(BF16)
        return run

    half = FN_WIDTH // 2
    o_ref[0] = _attend(q_ref.at[0], [kc_ref, k_ref.at[0]], [vtc_ref, vt_ref],
                       [dft(0, half), dft(half, half)])


def _attention(q3, k3, vt, cache_ckv, cache_kpe_slot, wts, fcs3, cs, ns, tq, name):
    b, sq, _ = q3.shape
    past = cache_ckv.shape[1]
    seq = lambda shape: pl.BlockSpec(shape, lambda bi, qi: (bi, 0, 0))
    const = lambda shape: pl.BlockSpec(shape, lambda bi, qi: (0,) * len(shape))
    return pl.pallas_call(
        _attn_kernel,
        grid=(b, sq // tq),
        in_specs=[pl.BlockSpec((1, tq, QK_WIDTH), lambda bi, qi: (bi, qi, 0)),
                  seq((1, sq, QK_WIDTH)),
                  pl.BlockSpec((V_WIDTH, sq), lambda bi, qi: (0, bi)),
                  seq((1, past, KV_LORA)), seq((1, past, LANES)),
                  const((KV_LORA, QK_WIDTH)), const((V_WIDTH, KV_LORA)), const((1, HEAD_SLOT)),
                  pl.BlockSpec((tq, sq), lambda bi, qi: (qi, 0)),
                  pl.BlockSpec((tq, sq), lambda bi, qi: (qi, 0)),
                  seq((1, sq, 2 * FN_WIDTH))],
        out_specs=[pl.BlockSpec((1, tq, V_WIDTH), lambda bi, qi: (bi, qi, 0)),
                   pl.BlockSpec((1, tq, FN_WIDTH), lambda bi, qi: (bi, qi, 0))],
        out_shape=[jax.ShapeDtypeStruct((b, sq, V_WIDTH), BF16),
                   jax.ShapeDtypeStruct((b, sq, FN_WIDTH), BF16)],
        scratch_shapes=[pltpu.VMEM((past, QK_WIDTH), BF16), pltpu.VMEM((V_WIDTH, past), BF16)],
        compiler_params=_cparams(("parallel", "arbitrary")),
        name=name,
    )(q3, k3, vt, cache_ckv, cache_kpe_slot, wts["w_kb"], wts["w_vb"], wts["kg"], cs, ns, fcs3)


def _router_logits(w, b, h_hi, h_lo):
    rows = w.shape[0]
    w_hi, w_lo = _split_bf16(w)
    y = _dot_nt(jnp.concatenate([w_hi, w_lo], axis=0), h_hi)
    return (y[:rows] + y[rows:]) + _dot_nt(w_hi, h_lo) + b


def _rows(x, n):
    return [x[j:j + 1, :] for j in range(n)]


def _first_argmax(rows, top):
    idx = jnp.full(top.shape, len(rows) - 1, jnp.int32)
    for j in range(len(rows) - 2, -1, -1):
        idx = jnp.where(rows[j] == top, j, idx)
    return idx


def _store_token_major(ref, x, tm, first=0):
    for s in range(TOKEN_ROWS):
        ref[pl.ds(first * TOKEN_ROWS + s, tm, stride=TOKEN_ROWS), :] = x[:, s * LANES:(s + 1) * LANES]


def _load_token_major(ref, tm):
    return jnp.concatenate([ref[pl.ds(s, tm, stride=TOKEN_ROWS), :] for s in range(TOKEN_ROWS)],
                           axis=1)


def _merge_kernel(*refs, n_first):
    first, second = refs[0:5], refs[5:10]
    (mod_ref, wao_ref, wfn_ref, wout_ref, g2_ref, wrg_ref, brg_ref, x1_ref, h2_ref, grp_ref, rank_ref,
     cnt_ref, carry_ref) = refs[10:]
    i = pl.program_id(0)

    @pl.when(i == 0)
    def _():
        carry_ref[...] = jnp.zeros_like(carry_ref)

    def tile(src):
        x_ref, attn_ref, fm_ref, sga_ref, sgf_ref = src
        hm = x_ref.shape[0] // MERGE_SPLIT
        rows = [pl.ds(k * hm, hm) for k in range(MERGE_SPLIT)]
        live = [dict() for _ in range(MERGE_SPLIT)]

        def branches(k):
            live[k]["a"] = _dot(attn_ref[rows[k], :], wao_ref[...])
            live[k]["f"] = _dot(fm_ref[rows[k], :], wfn_ref[...])

        def gated(k):
            u = (sga_ref[rows[k], :].astype(F32) * live[k].pop("a")
                 + sgf_ref[rows[k], :].astype(F32) * live[k].pop("f"))
            live[k]["u"] = u.astype(BF16)

        def project(k):
            live[k]["y"] = _dot(live[k].pop("u"), wout_ref[...])

        def residual(k):
            x1 = x_ref[rows[k], :] + mod_ref[0, 2:3, :] * live[k].pop("y")
            x1_ref[rows[k], :] = x1
            h2 = (x1 * _rms(x1, D_MODEL) * g2_ref[...]) * (1.0 + mod_ref[0, 4:5, :]) + mod_ref[0, 3:4, :]
            _store_token_major(h2_ref, h2, hm, k * hm)
            live[k]["h2"] = _split_bf16(h2)

        def route(k):
            g = _rows(_router_logits(wrg_ref[...], brg_ref[...], *live[k].pop("h2")), N_GROUPS)
            gidx = _first_argmax(g, functools.reduce(jnp.maximum, g))
            onehot = jnp.where(lax.broadcasted_iota(jnp.int32, (SEG_ROWS, hm), 0) == gidx, 1.0, 0.0)
            before = (lax.broadcasted_iota(jnp.int32, (hm, hm), 0)
                      < lax.broadcasted_iota(jnp.int32, (hm, hm), 1))
            prefix = _dot(onehot.astype(BF16), jnp.where(before, 1.0, 0.0).astype(BF16))
            carry = carry_ref[...]
            rank = jnp.sum(onehot * (prefix + carry[:, 0:1]), axis=0, keepdims=True)
            grp_ref[:, rows[k]] = gidx
            rank_ref[:, rows[k]] = rank.astype(jnp.int32)
            carry_ref[...] = carry + jnp.sum(onehot, axis=1, keepdims=True)

        phases = [branches, gated, project, residual, route]
        for t in range(len(phases) + MERGE_SPLIT - 1):
            for k in range(MERGE_SPLIT):
                if 0 <= t - k < len(phases):
                    phases[t - k](k)
        cnt_ref[...] = carry_ref[...].astype(jnp.int32)

    @pl.when(i < n_first)
    def _():
        tile(first)

    @pl.when(i >= n_first)
    def _():
        tile(second)


def _merge(first, second, mod3, mod_row_fn, wts):
    n_first = first[0].shape[0] // TM
    n = first[0].shape[0] + second[0].shape[0]
    tok = lambda w: pl.BlockSpec((TM, w), lambda i: (i, 0))
    widths = (D_MODEL, V_WIDTH, FN_WIDTH, D_MODEL, D_MODEL)
    in_first = [pl.BlockSpec((TM, w), lambda i: (jnp.minimum(i, n_first - 1), 0)) for w in widths]
    in_second = [pl.BlockSpec((TM, w), lambda i: (jnp.maximum(i - n_first, 0), 0)) for w in widths]
    return pl.pallas_call(
        functools.partial(_merge_kernel, n_first=n_first),
        grid=(n // TM,),
        in_specs=in_first + in_second + [
                  pl.BlockSpec((1, 6, D_MODEL), lambda i: (mod_row_fn(i), 0, 0)),
                  _const_spec((V_WIDTH, D_MODEL)),
                  _const_spec((FN_WIDTH, D_MODEL)),
                  _const_spec((D_MODEL, D_MODEL)),
                  _const_spec((1, D_MODEL)),
                  _const_spec((ROUTER_ROWS, D_MODEL)),
                  _const_spec((ROUTER_ROWS, 1))],
        out_specs=[tok(D_MODEL),
                   pl.BlockSpec((TM * TOKEN_ROWS, LANES), lambda i: (i, 0)),
                   pl.BlockSpec((1, TM), lambda i: (0, i)),
                   pl.BlockSpec((1, TM), lambda i: (0, i)), _const_spec((SEG_ROWS, LANES))],
        out_shape=[jax.ShapeDtypeStruct((n, D_MODEL), F32),
                   jax.ShapeDtypeStruct((n * TOKEN_ROWS, LANES), F32),
                   jax.ShapeDtypeStruct((1, n), jnp.int32),
                   jax.ShapeDtypeStruct((1, n), jnp.int32),
                   jax.ShapeDtypeStruct((SEG_ROWS, LANES), jnp.int32)],
        scratch_shapes=[pltpu.VMEM((SEG_ROWS, LANES), F32)],
        compiler_params=_cparams(("arbitrary",)),
        name="merge",
    )(*first, *second, mod3, wts["w_ao"], wts["w_fn"], wts["w_out"], wts["g2"],
      wts["w_rg"], wts["b_rg"])


def _token_rows(ref, t):
    start = t * TOKEN_ROWS
    if not isinstance(t, int):
        start = pl.multiple_of(start, TOKEN_ROWS)
    return ref.at[pl.ds(start, TOKEN_ROWS)]


def _start_row_copy(src, dst, src_tok, dst_tok, sem, queue=0):
    pltpu.async_copy(_token_rows(src, src_tok), _token_rows(dst, dst_tok), sem, priority=queue)


def _wait_tile(src_hbm, buf, sem):
    pltpu.make_async_copy(src_hbm.at[pl.ds(0, buf.shape[0])], buf, sem).wait()


def _combine_kernel(pos_ref, y_hbm, x1_ref, mod_ref, o1_ref, o2_ref, buf, sem, *, n_first):
    i = pl.program_id(0)
    n_tiles = pl.num_programs(0)
    tm = o1_ref.shape[0]

    def start(tile_idx, r):
        slot = tile_idx % (GATHER_AHEAD + 1)
        queue = r % DMA_QUEUES if isinstance(r, int) else 0
        _start_row_copy(y_hbm, buf.at[slot], pos_ref[tile_idx * tm + r], r, sem.at[slot], queue)

    for first in range(GATHER_AHEAD):
        @pl.when(i == 0)
        def _():
            def body(r, c):
                start(first, r)
                return c

            lax.fori_loop(0, tm, body, 0, unroll=32)

    def run(fetch_ahead, o_ref):
        slot = i % (GATHER_AHEAD + 1)
        _wait_tile(y_hbm, buf.at[slot], sem.at[slot])
        lane_tiles = [slice(s * LANES, (s + 1) * LANES) for s in range(TOKEN_ROWS)]
        for s, cols in enumerate(lane_tiles):
            o_ref[:, cols] = mod_ref[0, 5:6, cols] * buf.at[slot][pl.ds(s, tm, stride=TOKEN_ROWS), :]
        per = tm // TOKEN_ROWS
        for s, cols in enumerate(lane_tiles):
            o_ref[:, cols] = x1_ref[:, cols] + o_ref[:, cols]
            if fetch_ahead:
                for r in range(s * per, (s + 1) * per):
                    start(i + GATHER_AHEAD, r)

    fetch = i + GATHER_AHEAD < n_tiles
    for fetch_ahead in (True, False):
        cond = fetch if fetch_ahead else jnp.logical_not(fetch)
        for o_ref, mine in ((o1_ref, i < n_first), (o2_ref, i >= n_first)):
            @pl.when(jnp.logical_and(cond, mine))
            def _():
                run(fetch_ahead, o_ref)


def _combine(y_sorted, pos, x1, mod3, mod_row_fn, n_first_tokens):
    n = x1.shape[0]
    n_first = n_first_tokens // TM
    assert n // TM >= GATHER_AHEAD
    return pl.pallas_call(
        functools.partial(_combine_kernel, n_first=n_first),
        grid_spec=pltpu.PrefetchScalarGridSpec(
            num_scalar_prefetch=1,
            grid=(n // TM,),
            in_specs=[pl.BlockSpec(memory_space=pl.ANY),
                      pl.BlockSpec((TM, D_MODEL), lambda i, pos: (i, 0)),
                      pl.BlockSpec((1, 6, D_MODEL), lambda i, pos: (mod_row_fn(i), 0, 0))],
            out_specs=[pl.BlockSpec((TM, D_MODEL), lambda i, pos: (jnp.minimum(i, n_first - 1), 0)),
                       pl.BlockSpec((TM, D_MODEL), lambda i, pos: (jnp.maximum(i - n_first, 0), 0))],
            scratch_shapes=[pltpu.VMEM((GATHER_AHEAD + 1, TM * TOKEN_ROWS, LANES), F32),
                            pltpu.SemaphoreType.DMA((GATHER_AHEAD + 1,))]),
        out_shape=[jax.ShapeDtypeStruct((n_first_tokens, D_MODEL), F32),
                   jax.ShapeDtypeStruct((n - n_first_tokens, D_MODEL), F32)],
        compiler_params=_cparams(("arbitrary",)),
        name="moe_combine",
    )(pos, y_sorted, x1, mod3)


def _moe_kernel(grp_ref, on_ref, src_ref, h_hbm, wrg_ref, brg_ref, wre_ref, bre_ref, wg_ref, wu_ref,
                wd_ref, o_ref, hbuf, sem, *, tile):
    i = pl.program_id(0)
    n_tiles = pl.num_programs(0)

    def start(tile_idx, r):
        slot = tile_idx % (GATHER_AHEAD + 1)
        _start_row_copy(h_hbm, hbuf.at[slot], src_ref[tile_idx * tile + r], r, sem.at[slot])

    for first in range(GATHER_AHEAD):
        @pl.when(jnp.logical_and(i == 0, on_ref[first] == 1))
        def _():
            def body(r, c):
                start(first, r)
                return c

            lax.fori_loop(0, tile, body, 0, unroll=32)

    def run(fetch_ahead):
        def issue(chunk, n_chunks):
            if fetch_ahead:
                per = tile // n_chunks
                for r in range(chunk * per, (chunk + 1) * per):
                    start(i + GATHER_AHEAD, r)

        slot = i % (GATHER_AHEAD + 1)
        _wait_tile(h_hbm, hbuf.at[slot], sem.at[slot])
        h2_hi, h2_lo = _split_bf16(_load_token_major(hbuf.at[slot], tile))

        logits = _router_logits(jnp.concatenate([wrg_ref[...], wre_ref[0]], axis=0),
                                jnp.concatenate([brg_ref[...], bre_ref[0]], axis=0), h2_hi, h2_lo)
        g = _rows(logits, N_GROUPS)
        gmax = functools.reduce(jnp.maximum, g)
        p_top = 1.0 / functools.reduce(lambda p, q: p + q, [jnp.exp(gj - gmax) for gj in g])
        e = _rows(logits[ROUTER_ROWS:], EXPERTS_PER_GROUP)
        m1 = functools.reduce(jnp.maximum, e)
        i1 = _first_argmax(e, m1)
        rest = [jnp.where(i1 == j, -jnp.inf, e[j]) for j in range(EXPERTS_PER_GROUP)]
        m2 = functools.reduce(jnp.maximum, rest)
        i2 = _first_argmax(rest, m2)
        t = jnp.exp(m2 - m1)
        w1 = p_top / (1.0 + t)
        w2 = p_top * t / (1.0 + t)
        row = lax.broadcasted_iota(jnp.int32, (LANES, tile), 0)
        comb = (jnp.where(row == i1, w1, 0.0) + jnp.where(row == i2, w2, 0.0)).T

        gates, ups = [], []
        for j in range(EXPERTS_PER_GROUP):
            gates.append(_dot(h2_hi, wg_ref[j]))
            issue(2 * j, 2 * EXPERTS_PER_GROUP)
            ups.append(_dot(h2_hi, wu_ref[j]))
            issue(2 * j + 1, 2 * EXPERTS_PER_GROUP)
        acts = [((a * _sigmoid(a)) * u * comb[:, j:j + 1]).astype(BF16)
                for j, (a, u) in enumerate(zip(gates, ups))]
        acc = functools.reduce(lambda p, q: p + q,
                               [_dot(acts[j], wd_ref[j]) for j in range(EXPERTS_PER_GROUP)])
        _store_token_major(o_ref, acc, tile)

    @pl.when(on_ref[i] == 0)
    def _():
        o_ref[...] = jnp.zeros_like(o_ref)

    ahead = jnp.minimum(i + GATHER_AHEAD, n_tiles - 1)
    fetch = jnp.logical_and(i + GATHER_AHEAD < n_tiles, on_ref[ahead] == 1)

    @pl.when(jnp.logical_and(on_ref[i] == 1, fetch))
    def _():
        run(True)

    @pl.when(jnp.logical_and(on_ref[i] == 1, jnp.logical_not(fetch)))
    def _():
        run(False)


def _moe(h2tm, src, maps, wts, w_gate, w_up, w_down, tile):
    n_tiles = src.shape[0] // tile
    const = lambda shape: pl.BlockSpec(shape, lambda i, grp, on, src: (0,) * len(shape))
    by_group = lambda shape: pl.BlockSpec(
        shape, lambda i, grp, on, src: (grp[i],) + (0,) * (len(shape) - 1))
    return pl.pallas_call(
        functools.partial(_moe_kernel, tile=tile),
        grid_spec=pltpu.PrefetchScalarGridSpec(
            num_scalar_prefetch=3,
            grid=(n_tiles,),
            in_specs=[pl.BlockSpec(memory_space=pl.ANY),
                      const((ROUTER_ROWS, D_MODEL)), const((ROUTER_ROWS, 1)),
                      by_group((1, ROUTER_ROWS, D_MODEL)), by_group((1, ROUTER_ROWS, 1)),
                      by_group((EXPERTS_PER_GROUP, D_MODEL, D_EXPERT)),
                      by_group((EXPERTS_PER_GROUP, D_MODEL, D_EXPERT)),
                      by_group((EXPERTS_PER_GROUP, D_EXPERT, D_MODEL))],
            out_specs=pl.BlockSpec((tile * TOKEN_ROWS, LANES), lambda i, grp, on, src: (i, 0)),
            scratch_shapes=[pltpu.VMEM((GATHER_AHEAD + 1, tile * TOKEN_ROWS, LANES), F32),
                            pltpu.SemaphoreType.DMA((GATHER_AHEAD + 1,))]),
        out_shape=jax.ShapeDtypeStruct((n_tiles * tile * TOKEN_ROWS, LANES), F32),
        compiler_params=_cparams(("arbitrary",)),
        name="moe",
    )(*maps, src, h2tm, wts["w_rg"], wts["b_rg"], wts["w_re"], wts["b_re"], w_gate, w_up, w_down)


def _invert_kernel(pos_ref, lo_ref, hi_ref, src_ref, *, n, n_fill):
    def zero(p, c):
        src_ref[p] = 0
        return c

    for s in range(n_fill):
        lax.fori_loop(lo_ref[s], hi_ref[s], zero, 0)

    def put(t, c):
        src_ref[pos_ref[t]] = t
        return c

    lax.fori_loop(0, n, put, 0, unroll=8)


def _invert(pos, fill_lo, fill_hi, n_sorted):
    return pl.pallas_call(
        functools.partial(_invert_kernel, n=pos.shape[0], n_fill=fill_lo.shape[0]),
        grid_spec=pltpu.PrefetchScalarGridSpec(
            num_scalar_prefetch=3, grid=(1,), in_specs=[],
            out_specs=pl.BlockSpec(memory_space=pltpu.SMEM)),
        out_shape=jax.ShapeDtypeStruct((n_sorted,), jnp.int32),
        name="moe_invert",
    )(pos, fill_lo, fill_hi)


def _sort_plan(grp, rank, cnt, tile, n_tiles):
    n_tile_grp = (cnt + tile - 1) // tile
    tile_end = jnp.cumsum(n_tile_grp)
    tile_start = tile_end - n_tile_grp
    total = tile_end[-1]
    base = (tile_start * tile).astype(jnp.int32)
    pos = rank + functools.reduce(lambda acc, g: jnp.where(grp == g, base[g], acc),
                                  range(N_GROUPS - 1), base[N_GROUPS - 1])
    fill_lo = jnp.concatenate([tile_start * tile + cnt, total[None] * tile]).astype(jnp.int32)
    fill_hi = jnp.concatenate([tile_end * tile, jnp.full((1,), n_tiles * tile)]).astype(jnp.int32)
    src = _invert(pos, fill_lo, fill_hi, n_tiles * tile)
    i = jnp.arange(n_tiles, dtype=jnp.int32)
    tile_grp = jnp.sum((jnp.minimum(i, total - 1)[:, None] >= tile_end[None, :]).astype(jnp.int32),
                       axis=1)
    return pos, src, (tile_grp.astype(jnp.int32), (i < total).astype(jnp.int32))


def _sparse_moe(x1, h2tm, grp, rank, cnt8, mod3, mod_row_fn, wts, w_gate, w_up, w_down, tile,
                n_first_tokens):
    n = x1.shape[0]
    n_tiles = n // tile + N_GROUPS
    pos, src, maps = _sort_plan(grp.reshape(n), rank.reshape(n), cnt8[:N_GROUPS, 0], tile, n_tiles)
    y_sorted = _moe(h2tm, src, maps, wts, w_gate, w_up, w_down, tile)
    return _combine(y_sorted, pos, x1, mod3, mod_row_fn, n_first_tokens)


def _pack_weights(l, norm1_g, q_a_norm_g, w_q_b, kv_a_norm_g, w_kv_b, q_norm_g, k_norm_g,
                  w_attn_o, w_fnet, w_out, norm2_g, w_router_group, b_router_group,
                  w_router_expert, b_router_expert):
    w_qb = jnp.pad(w_q_b[l].reshape(Q_LORA, N_HEADS, QK_HEAD),
                   ((0, 0), (0, 0), (0, HEAD_SLOT - QK_HEAD))).reshape(Q_LORA, QK_WIDTH)
    wkv = w_kv_b[l].reshape(KV_LORA, N_HEADS, QK_NOPE + V_HEAD)
    w_kb = jnp.pad(wkv[:, :, :QK_NOPE],
                   ((0, 0), (0, 0), (0, HEAD_SLOT - QK_NOPE))).reshape(KV_LORA, QK_WIDTH)
    w_vb = wkv[:, :, QK_NOPE:].reshape(KV_LORA, V_WIDTH)
    pad_g = lambda g: jnp.pad(g, (0, HEAD_SLOT - QK_HEAD)).reshape(1, HEAD_SLOT)
    w_rg = jnp.pad(w_router_group[l].T, ((0, ROUTER_ROWS - N_GROUPS), (0, 0)))
    b_rg = jnp.pad(b_router_group[l], (0, ROUTER_ROWS - N_GROUPS)).reshape(ROUTER_ROWS, 1)
    w_re = jnp.pad(w_router_expert[l].T.reshape(N_GROUPS, EXPERTS_PER_GROUP, D_MODEL),
                   ((0, 0), (0, ROUTER_ROWS - EXPERTS_PER_GROUP), (0, 0)))
    b_re = jnp.pad(b_router_expert[l].reshape(N_GROUPS, EXPERTS_PER_GROUP),
                   ((0, 0), (0, ROUTER_ROWS - EXPERTS_PER_GROUP))).reshape(N_GROUPS, ROUTER_ROWS, 1)
    dft_c, dft_ns = _dft_tables(FN_GROUP_W)
    return {
        "g1": norm1_g[l].reshape(1, D_MODEL),
        "qag": q_a_norm_g[l].reshape(1, Q_LORA),
        "w_qb": w_qb.astype(BF16),
        "kvg": kv_a_norm_g[l].reshape(1, KV_LORA),
        "w_kb": w_kb.astype(BF16),
        "w_vb": w_vb.T.astype(BF16),
        "qg": pad_g(q_norm_g[l]),
        "kg": pad_g(k_norm_g[l]),
        "dft_c": jnp.concatenate([jnp.asarray(dft_c), -jnp.asarray(dft_ns)], axis=1).astype(BF16),
        "w_ao": w_attn_o[l].astype(BF16),
        "w_fn": w_fnet[l].astype(BF16),
        "w_out": w_out[l].astype(BF16),
        "g2": norm2_g[l].reshape(1, D_MODEL),
        "w_rg": w_rg,
        "b_rg": b_rg,
        "w_re": w_re,
        "b_re": b_re,
    }


def _layer(xp, xs, cache_ckv_l, cache_kpe_l, mod3, wts, experts):
    bp, sp, _ = xp.shape
    bs, ss, _ = xs.shape
    ctx_row = lambda i: 0
    lat_row = lambda i: 1 + i // (ss // TM)

    xp2 = xp.reshape(bp * sp, D_MODEL)
    ride = (bp * sp) // TM == N_EXPERTS
    cs, ns = (jnp.asarray(t).astype(BF16) for t in _dft_tables(sp))
    outs = _inproj(xp2, mod3, ctx_row, wts, None, True, experts if ride else (), (sp, cs, ns))
    attn, fm, sga, sgf, ckv, kpe = outs[:6]
    w_gate, w_up, w_down = outs[6:] if ride else (w.astype(BF16) for w in experts)
    ctx_set = (xp2, attn, fm, sga, sgf)

    xs2 = xs.reshape(bs * ss, D_MODEL)
    rope_tabs = tuple(jnp.asarray(t) for t in _rope_tables(ss))
    q, k, v, fcs, sga, sgf = _inproj(xs2, mod3, lat_row, wts, rope_tabs, False)
    kpe_slot = jnp.pad(cache_kpe_l, ((0, 0), (0, 0), (QK_NOPE, LANES - QK_HEAD)))
    cs, ns = (jnp.asarray(t).astype(BF16) for t in _dft_tables(ss))
    attn, fm = _attention(q.reshape(bs, ss, QK_WIDTH), k.reshape(bs, ss, QK_WIDTH), v, cache_ckv_l,
                          kpe_slot, wts, fcs.reshape(bs, ss, 2 * FN_WIDTH), cs, ns, TQ_LAT, "attn_lat")
    lat_set = (xs2, attn.reshape(bs * ss, V_WIDTH), fm.reshape(bs * ss, FN_WIDTH), sga, sgf)

    n_ctx_tiles = (bp * sp) // TM
    all_row = lambda i: jnp.where(i < n_ctx_tiles, 0, lat_row(i - n_ctx_tiles))
    x1, h2tm, grp, rank, cnt = _merge(ctx_set, lat_set, mod3, all_row, wts)
    yp, ys = _sparse_moe(x1, h2tm, grp, rank, cnt, mod3, all_row, wts, w_gate, w_up, w_down, MOE_TILE,
                         bp * sp)
    yp, ys = yp.reshape(bp, sp, D_MODEL), ys.reshape(bs, ss, D_MODEL)

    return yp, ys, ckv.reshape(bp, sp, KV_LORA), jnp.swapaxes(kpe, 1, 2)


def kernel(x_prompt, x_sample, cache_ckv, cache_kpe, c, c_ctx, w_mod, b_mod, norm1_g, w_in, q_a_norm_g, w_q_b, kv_a_norm_g, w_kv_b, q_norm_g, k_norm_g, w_attn_o, w_fnet, w_out, norm2_g, w_router_group, b_router_group, w_router_expert, b_router_expert, w_exp_gate, w_exp_up, w_exp_down):
    depth = w_mod.shape[0]
    n_lat = c.shape[0]
    assert 1 + n_lat <= MOD_ROWS
    cond8 = jnp.concatenate([c_ctx[None, :], c, jnp.zeros((MOD_ROWS - 1 - n_lat, D_MODEL), F32)], axis=0)
    xp, xs = x_prompt, x_sample
    ckv_layers, kpe_layers = [], []
    for l in range(depth):
        mod, w_in_p = _adaln(cond8, w_mod[l], b_mod[l].reshape(1, -1), jnp.swapaxes(w_in[l], 0, 1))
        mod3 = mod.reshape(MOD_ROWS, 6, D_MODEL)
        wts = _pack_weights(l, norm1_g, q_a_norm_g, w_q_b, kv_a_norm_g, w_kv_b, q_norm_g,
                            k_norm_g, w_attn_o, w_fnet, w_out, norm2_g, w_router_group,
                            b_router_group, w_router_expert, b_router_expert)
        wts["w_in"] = w_in_p
        xp, xs, ckv, kpe = _layer(xp, xs, cache_ckv[:, l], cache_kpe[:, l], mod3, wts,
                                  (w_exp_gate[l], w_exp_up[l], w_exp_down[l]))
        ckv_layers.append(ckv)
        kpe_layers.append(kpe)
    return xp, xs, jnp.stack(ckv_layers, axis=1), jnp.stack(kpe_layers, axis=1)
```

```python
import functools
import math

import numpy as np
import jax
import jax.numpy as jnp
from jax import lax
from jax.experimental import pallas as pl
from jax.experimental.pallas import tpu as pltpu

D_MODEL = 1024
GRID_W = 64
N_HEADS = 8
Q_LORA = 512
KV_LORA = 256
QK_NOPE = 64
QK_ROPE = 32
V_HEAD = 64
QK_HEAD = QK_NOPE + QK_ROPE
ATTN_SCALE = QK_HEAD ** -0.5
ROPE_BASE = 10000.0
FN_GROUPS = 4
FN_GROUP_W = 128
FN_WIDTH = FN_GROUPS * FN_GROUP_W
N_GROUPS = 4
EXPERTS_PER_GROUP = 4
N_EXPERTS = N_GROUPS * EXPERTS_PER_GROUP
D_EXPERT = 512
EPS = 1e-6

LANES = 128
HEAD_SLOT = LANES
QK_WIDTH = N_HEADS * HEAD_SLOT
V_WIDTH = N_HEADS * V_HEAD
C_QA = 0
C_KVA = C_QA + Q_LORA
C_KPE = C_KVA + KV_LORA
C_FN = C_KPE + LANES
C_GA = C_FN + FN_WIDTH
C_GF = C_GA + D_MODEL
IN_PACKED = C_GF + D_MODEL
SUBLANES = 8
ROUTER_ROWS = SUBLANES
MOD_ROWS = SUBLANES
VMEM_LIMIT = 56 * 1024 * 1024

TM = 512
TQ_LAT = 256
SEG_ROWS = SUBLANES
TOKEN_ROWS = D_MODEL // LANES
MOE_TILE = 256
MERGE_SPLIT = 2
CAST_PARTS = 2
DMA_QUEUES = 2
GATHER_AHEAD = 2
ADALN_STEPS = 8

BF16 = jnp.bfloat16
F32 = jnp.float32


def _cparams(sem):
    return pltpu.CompilerParams(dimension_semantics=sem, vmem_limit_bytes=VMEM_LIMIT)


def _dot(a, b):
    return jnp.dot(a, b, preferred_element_type=F32)


def _dot_nt(a, b):
    return lax.dot_general(a, b, (((1,), (1,)), ((), ())), preferred_element_type=F32)


def _sigmoid(x):
    return 1.0 / (1.0 + jnp.exp(-x))


def _split_bf16(x):
    hi = x.astype(BF16)
    return hi, (x - hi.astype(F32)).astype(BF16)


@functools.lru_cache(maxsize=None)
def _rope_tables(n_pos):
    half = QK_ROPE // 2
    quarter = half // 2
    freqs = ROPE_BASE ** (-np.arange(quarter, dtype=np.float64) / quarter)
    pos = np.arange(n_pos)
    row = (pos // GRID_W).astype(np.float64)
    col = (pos % GRID_W).astype(np.float64)
    cos_t = np.ones((n_pos, LANES), np.float64)
    sin_a = np.zeros((n_pos, LANES), np.float64)
    sin_b = np.zeros((n_pos, LANES), np.float64)
    for base, p in ((QK_NOPE, row), (QK_NOPE + half, col)):
        ang = p[:, None] * freqs[None, :]
        cos_t[:, base:base + quarter] = np.cos(ang)
        cos_t[:, base + quarter:base + half] = np.cos(ang)
        sin_a[:, base:base + quarter] = -np.sin(ang)
        sin_b[:, base + quarter:base + half] = np.sin(ang)
    return (cos_t.astype(np.float32), sin_a.astype(np.float32), sin_b.astype(np.float32))


@functools.lru_cache(maxsize=None)
def _dft_tables(n):
    k = np.arange(n)
    ang = 2.0 * np.pi * ((k[:, None] * k[None, :]) % n) / n
    s = 1.0 / math.sqrt(n)
    return (np.cos(ang) * s).astype(np.float32), (-np.sin(ang) * s).astype(np.float32)


def _adaln_kernel(cond_ref, w_ref, b_ref, win_ref, o_ref, winp_ref):
    c = cond_ref[...]
    s_hi, s_lo = _split_bf16(c * _sigmoid(c))
    w_hi, w_lo = _split_bf16(w_ref[...])
    y = _dot(jnp.concatenate([s_hi, s_lo], axis=0), w_hi)
    o_ref[...] = (y[:MOD_ROWS] + y[MOD_ROWS:]) + _dot(s_hi, w_lo) + b_ref[...]

    winp_ref[:C_KPE, :] = win_ref[:C_KPE, :].astype(BF16)
    winp_ref[C_KPE:C_FN, :] = jnp.zeros((LANES, winp_ref.shape[1]), BF16)
    winp_ref[C_KPE + QK_NOPE:C_KPE + QK_HEAD, :] = win_ref[C_KPE:C_KPE + QK_ROPE, :].astype(BF16)
    winp_ref[C_FN:, :] = win_ref[C_KPE + QK_ROPE:, :].astype(BF16)


def _adaln(cond8, w_mod, b_mod, w_in_t):
    n = w_mod.shape[1]
    steps = ADALN_STEPS
    tn, tr = n // steps, D_MODEL // steps
    return pl.pallas_call(
        _adaln_kernel,
        grid=(steps,),
        in_specs=[pl.BlockSpec((MOD_ROWS, D_MODEL), lambda j: (0, 0)),
                  pl.BlockSpec((D_MODEL, tn), lambda j: (0, j)),
                  pl.BlockSpec((1, tn), lambda j: (0, j)),
                  pl.BlockSpec((w_in_t.shape[0], tr), lambda j: (0, j))],
        out_specs=[pl.BlockSpec((MOD_ROWS, tn), lambda j: (0, j)),
                   pl.BlockSpec((IN_PACKED, tr), lambda j: (0, j))],
        out_shape=[jax.ShapeDtypeStruct((MOD_ROWS, n), F32),
                   jax.ShapeDtypeStruct((IN_PACKED, D_MODEL), BF16)],
        compiler_params=_cparams(("arbitrary",)),
        name="adaln",
    )(cond8, w_mod, b_mod, w_in_t)


def _rms(x, width):
    return lax.rsqrt(jnp.sum(x * x, axis=-1, keepdims=True) * (1.0 / width) + EPS)


def _rope(x, cos_t, sin_a, sin_b):
    return x * cos_t + pltpu.roll(x, LANES - 8, 1) * sin_a + pltpu.roll(x, 8, 1) * sin_b


def _inproj_kernel(*refs, rope, emit_cache, n_cast, seq_len):
    it = iter(refs)
    x_ref, mod_ref, g1_ref, win_ref, qag_ref, wqb_ref, kvg_ref, wkb_ref, wvb_ref = (
        next(it) for _ in range(9))
    qg_ref, kg_ref, dft_ref = next(it), next(it), next(it)
    if rope:
        cos_ref, sa_ref, sb_ref = next(it), next(it), next(it)
    if seq_len:
        seq_cs_ref, seq_ns_ref = next(it), next(it)
    cast_in = [next(it) for _ in range(n_cast)]
    if seq_len:
        attn_ref, fm_ref, sga_ref, sgf_ref = (next(it) for _ in range(4))
    else:
        q_ref, k_ref, v_ref, fcs_ref, sga_ref, sgf_ref = (next(it) for _ in range(6))
    if emit_cache:
        ckv_ref, kpe_ref = next(it), next(it)
    cast_out = [next(it) for _ in range(n_cast)]
    if seq_len:
        q_ref, k_ref, v_ref, fcs_ref = (next(it) for _ in range(4))
    cast_buf = [next(it) for _ in range(n_cast)]
    cast_sem = next(it) if n_cast else None

    def cast_copy(j, part):
        rows = pl.ds(part * (cast_buf[j].shape[1] // CAST_PARTS), cast_buf[j].shape[1] // CAST_PARTS)
        return pltpu.make_async_copy(cast_buf[j].at[:, rows],
                                     cast_out[j].at[pl.ds(pl.program_id(0), 1), rows],
                                     cast_sem.at[j, part])

    def cast_part(c):
        j, part = divmod(c, CAST_PARTS)
        if j < n_cast:
            rows = pl.ds(part * (cast_buf[j].shape[1] // CAST_PARTS), cast_buf[j].shape[1] // CAST_PARTS)
            cast_buf[j][:, rows, :] = cast_in[j][:, rows, :].astype(BF16)
            cast_copy(j, part).start()

    x = x_ref[...]
    shift = mod_ref[0, 0:1, :]
    scale = mod_ref[0, 1:2, :]
    h = (x * _rms(x, D_MODEL) * g1_ref[...]) * (1.0 + scale) + shift
    hb = h.astype(BF16)

    if rope:
        cos_t, sin_a, sin_b = cos_ref[...], sa_ref[...], sb_ref[...]

    qa = _dot_nt(hb, win_ref[C_QA:C_QA + Q_LORA, :])
    cast_part(0)
    qn = (qa * _rms(qa, Q_LORA) * qag_ref[...]).astype(BF16)
    q = _dot(qn, wqb_ref[...])
    cast_part(1)
    qg = qg_ref[...] * ATTN_SCALE
    for hd in range(N_HEADS):
        qh = q[:, hd * HEAD_SLOT:(hd + 1) * HEAD_SLOT]
        qh = qh * _rms(qh, QK_HEAD) * qg
        if rope:
            qh = _rope(qh, cos_t, sin_a, sin_b)
        q_ref[:, hd * HEAD_SLOT:(hd + 1) * HEAD_SLOT] = qh.astype(BF16)

    kva = _dot_nt(hb, win_ref[C_KVA:C_KVA + KV_LORA, :])
    cast_part(2)
    ckv = kva * _rms(kva, KV_LORA) * kvg_ref[...]
    kpe = _dot_nt(hb, win_ref[C_KPE:C_KPE + LANES, :])
    cast_part(3)
    if emit_cache:
        ckv_ref[...] = ckv
        kpe_t = kpe.T[QK_NOPE:QK_NOPE + QK_ROPE, :]
        for j in range(x.shape[0] // seq_len):
            kpe_ref[j] = kpe_t[:, j * seq_len:(j + 1) * seq_len]
    _emit_kv(ckv.astype(BF16), kpe, wkb_ref, wvb_ref, kg_ref,
             (cos_t, sin_a, sin_b) if rope else None, k_ref, v_ref)

    fn = _dot_nt(hb, win_ref[C_FN:C_FN + FN_WIDTH, :]).astype(BF16)
    cast_part(4)
    for g in range(FN_GROUPS):
        cs = _dot(fn[:, g * FN_GROUP_W:(g + 1) * FN_GROUP_W], dft_ref[...])
        fcs_ref[:, g * FN_GROUP_W:(g + 1) * FN_GROUP_W] = cs[:, :FN_GROUP_W].astype(BF16)
        fcs_ref[:, FN_WIDTH + g * FN_GROUP_W:FN_WIDTH + (g + 1) * FN_GROUP_W] = (
            cs[:, FN_GROUP_W:].astype(BF16))

    cast_part(5)

    def gate(ref, c0, half):
        def run():
            w = D_MODEL // 2
            ref[:, pl.ds(half * w, w)] = _sigmoid(
                _dot_nt(hb, win_ref[pl.ds(c0 + half * w, w), :])).astype(BF16)
        return run

    gates = [gate(ref, c0, half) for ref, c0 in ((sga_ref, C_GA), (sgf_ref, C_GF)) for half in range(2)]

    if seq_len:
        n_seq = x.shape[0] // seq_len
        for j in range(n_seq):
            rows = pl.ds(j * seq_len, seq_len)
            fill = gates[j * len(gates) // n_seq:(j + 1) * len(gates) // n_seq]
            attn_ref[rows, :] = _attend(q_ref.at[rows], [k_ref.at[rows]], [v_ref.at[:, rows]], fill)
            fm_ref[rows, :] = (_dot(seq_cs_ref[...], fcs_ref[rows, :FN_WIDTH])
                               + _dot(seq_ns_ref[...], fcs_ref[rows, FN_WIDTH:])).astype(BF16)
    else:
        for run_gate in gates:
            run_gate()
    for c in range(n_cast * CAST_PARTS):
        cast_copy(*divmod(c, CAST_PARTS)).wait()


def _emit_kv(ckvb, kpe, wkb_ref, wvb_ref, kg_ref, rope_tabs, k_ref, v_ref):
    kg = kg_ref[...]
    v_ref[...] = _dot_nt(wvb_ref[...], ckvb).astype(BF16)
    kn = _dot(ckvb, wkb_ref[...])
    pe_ss = jnp.sum(kpe * kpe, axis=-1, keepdims=True)
    pe_g = kpe * kg
    if rope_tabs is not None:
        pe_g = _rope(pe_g, *rope_tabs)
    for hd in range(N_HEADS):
        knh = kn[:, hd * HEAD_SLOT:(hd + 1) * HEAD_SLOT]
        ss = jnp.sum(knh * knh, axis=-1, keepdims=True) + pe_ss
        r = lax.rsqrt(ss * (1.0 / QK_HEAD) + EPS)
        k_ref[:, hd * HEAD_SLOT:(hd + 1) * HEAD_SLOT] = ((knh * kg + pe_g) * r).astype(BF16)


def _const_spec(shape):
    return pl.BlockSpec(shape, lambda i: (0,) * len(shape))


def _inproj(x2d, mod3, mod_row_fn, wts, rope_tabs, emit_cache, cast=(), seq=None):
    n = x2d.shape[0]
    assert all(w.shape[0] == n // TM for w in cast)
    assert seq is None or TM % seq[0] == 0
    rope = rope_tabs is not None
    tiles_per_seq = None if not rope else rope_tabs[0].shape[0] // TM
    in_specs = [pl.BlockSpec((TM, D_MODEL), lambda i: (i, 0)),
                pl.BlockSpec((1, 6, D_MODEL), lambda i: (mod_row_fn(i), 0, 0)),
                _const_spec((1, D_MODEL)),
                _const_spec((IN_PACKED, D_MODEL)),
                _const_spec((1, Q_LORA)),
                _const_spec((Q_LORA, QK_WIDTH)),
                _const_spec((1, KV_LORA)),
                _const_spec((KV_LORA, QK_WIDTH)),
                _const_spec((V_WIDTH, KV_LORA)),
                _const_spec((1, HEAD_SLOT)),
                _const_spec((1, HEAD_SLOT)),
                _const_spec((FN_GROUP_W, 2 * FN_GROUP_W))]
    args = [x2d, mod3, wts["g1"], wts["w_in"], wts["qag"], wts["w_qb"], wts["kvg"],
            wts["w_kb"], wts["w_vb"], wts["qg"], wts["kg"], wts["dft_c"]]
    if rope:
        in_specs += [pl.BlockSpec((TM, LANES), lambda i: (i % tiles_per_seq, 0))] * 3
        args += list(rope_tabs)
    if seq is not None:
        in_specs += [_const_spec((seq[0], seq[0]))] * 2
        args += [seq[1], seq[2]]
    cast_specs = [pl.BlockSpec((1,) + w.shape[1:], lambda i: (i, 0, 0)) for w in cast]
    in_specs += cast_specs
    args += list(cast)
    qkvf = [((TM, QK_WIDTH), (n, QK_WIDTH)), ((TM, QK_WIDTH), (n, QK_WIDTH)),
            ((V_WIDTH, TM), (V_WIDTH, n)), ((TM, 2 * FN_WIDTH), (n, 2 * FN_WIDTH))]
    if seq is None:
        widths = [None] * 4 + [D_MODEL, D_MODEL]
    else:
        widths = [V_WIDTH, FN_WIDTH, D_MODEL, D_MODEL]
    out_shape, out_specs = [], []
    for j, w in enumerate(widths):
        if w is None:
            out_shape.append(jax.ShapeDtypeStruct(qkvf[j][1], BF16))
            out_specs.append(pl.BlockSpec(qkvf[j][0], (lambda i: (0, i)) if j == 2 else (lambda i: (i, 0))))
        else:
            out_shape.append(jax.ShapeDtypeStruct((n, w), BF16))
            out_specs.append(pl.BlockSpec((TM, w), lambda i: (i, 0)))
    if emit_cache:
        assert seq is not None
        out_shape += [jax.ShapeDtypeStruct((n, KV_LORA), F32),
                      jax.ShapeDtypeStruct((n // seq[0], QK_ROPE, seq[0]), F32)]
        out_specs += [pl.BlockSpec((TM, KV_LORA), lambda i: (i, 0)),
                      pl.BlockSpec((TM // seq[0], QK_ROPE, seq[0]), lambda i: (i, 0, 0))]
    out_shape += [jax.ShapeDtypeStruct(w.shape, BF16) for w in cast]
    out_specs += [pl.BlockSpec(memory_space=pl.ANY) for _ in cast]
    scratch = [] if seq is None else [pltpu.VMEM(blk, BF16) for blk, _ in qkvf]
    if cast:
        scratch += [pltpu.VMEM((1,) + w.shape[1:], BF16) for w in cast]
        scratch += [pltpu.SemaphoreType.DMA((len(cast), CAST_PARTS))]
    return pl.pallas_call(
        functools.partial(_inproj_kernel, rope=rope, emit_cache=emit_cache, n_cast=len(cast),
                          seq_len=None if seq is None else seq[0]),
        grid=(n // TM,),
        in_specs=in_specs,
        out_specs=out_specs,
        out_shape=out_shape,
        scratch_shapes=scratch,
        compiler_params=_cparams(("parallel",)),
        name="inproj_lat" if rope else "inproj_ctx",
    )(*args)


def _attend(q, ks, vts, fillers=()):
    head = lambda hd: slice(hd * HEAD_SLOT, (hd + 1) * HEAD_SLOT)
    st = [jnp.stack([_dot_nt(k[:, head(hd)], q[:, head(hd)]) for hd in range(N_HEADS)]) for k in ks]
    m = functools.reduce(jnp.maximum, [sj.max(axis=1, keepdims=True) for sj in st])
    fillers = list(fillers)
    every = N_HEADS // max(1, len(fillers))
    p = []
    for hd in range(N_HEADS):
        p.append([jnp.exp(sj[hd] - m[hd]) for sj in st])
        if fillers and hd % every == 0:
            fillers.pop(0)()
    assert not fillers
    outs = []
    for hd in range(N_HEADS):
        l = functools.reduce(lambda a, b: a + b, [pj.sum(axis=0, keepdims=True) for pj in p[hd]])
        o = functools.reduce(lambda a, b: a + b,
                             [_dot(vt[hd * V_HEAD:(hd + 1) * V_HEAD, :], pj.astype(BF16))
                              for vt, pj in zip(vts, p[hd])])
        outs.append(o / l)
    return jnp.concatenate(outs, axis=0).T.astype(BF16)


def _attn_kernel(q_ref, k_ref, vt_ref, cckv_ref, ckpe_ref, wkb_ref, wvb_ref, kg_ref, cs_ref, ns_ref,
                 f_ref, o_ref, fm_ref, kc_ref, vtc_ref):
    @pl.when(pl.program_id(1) == 0)
    def _():
        _emit_kv(cckv_ref[0].astype(BF16), ckpe_ref[0], wkb_ref, wvb_ref, kg_ref, None, kc_ref, vtc_ref)

    def dft(c0, width):
        def run():
            fm_ref[0, :, pl.ds(c0, width)] = (
                _dot(cs_ref[...], f_ref[0, :, pl.ds(c0, width)])
                + _dot(ns_ref[...], f_ref[0, :, pl.ds(FN_WIDTH + c0, width)])).astype(BF16)
        return run

    half = FN_WIDTH // 2
    o_ref[0] = _attend(q_ref.at[0], [kc_ref, k_ref.at[0]], [vtc_ref, vt_ref],
                       [dft(0, half), dft(half, half)])


def _attention(q3, k3, vt, cache_ckv, cache_kpe_slot, wts, fcs3, cs, ns, tq, name):
    b, sq, _ = q3.shape
    past = cache_ckv.shape[1]
    seq = lambda shape: pl.BlockSpec(shape, lambda bi, qi: (bi, 0, 0))
    const = lambda shape: pl.BlockSpec(shape, lambda bi, qi: (0,) * len(shape))
    return pl.pallas_call(
        _attn_kernel,
        grid=(b, sq // tq),
        in_specs=[pl.BlockSpec((1, tq, QK_WIDTH), lambda bi, qi: (bi, qi, 0)),
                  seq((1, sq, QK_WIDTH)),
                  pl.BlockSpec((V_WIDTH, sq), lambda bi, qi: (0, bi)),
                  seq((1, past, KV_LORA)), seq((1, past, LANES)),
                  const((KV_LORA, QK_WIDTH)), const((V_WIDTH, KV_LORA)), const((1, HEAD_SLOT)),
                  pl.BlockSpec((tq, sq), lambda bi, qi: (qi, 0)),
                  pl.BlockSpec((tq, sq), lambda bi, qi: (qi, 0)),
                  seq((1, sq, 2 * FN_WIDTH))],
        out_specs=[pl.BlockSpec((1, tq, V_WIDTH), lambda bi, qi: (bi, qi, 0)),
                   pl.BlockSpec((1, tq, FN_WIDTH), lambda bi, qi: (bi, qi, 0))],
        out_shape=[jax.ShapeDtypeStruct((b, sq, V_WIDTH), BF16),
                   jax.ShapeDtypeStruct((b, sq, FN_WIDTH), BF16)],
        scratch_shapes=[pltpu.VMEM((past, QK_WIDTH), BF16), pltpu.VMEM((V_WIDTH, past), BF16)],
        compiler_params=_cparams(("parallel", "arbitrary")),
        name=name,
    )(q3, k3, vt, cache_ckv, cache_kpe_slot, wts["w_kb"], wts["w_vb"], wts["kg"], cs, ns, fcs3)


def _router_logits(w, b, h_hi, h_lo):
    rows = w.shape[0]
    w_hi, w_lo = _split_bf16(w)
    y = _dot_nt(jnp.concatenate([w_hi, w_lo], axis=0), h_hi)
    return (y[:rows] + y[rows:]) + _dot_nt(w_hi, h_lo) + b


def _rows(x, n):
    return [x[j:j + 1, :] for j in range(n)]


def _first_argmax(rows, top):
    idx = jnp.full(top.shape, len(rows) - 1, jnp.int32)
    for j in range(len(rows) - 2, -1, -1):
        idx = jnp.where(rows[j] == top, j, idx)
    return idx


def _store_token_major(ref, x, tm, first=0):
    for s in range(TOKEN_ROWS):
        ref[pl.ds(first * TOKEN_ROWS + s, tm, stride=TOKEN_ROWS), :] = x[:, s * LANES:(s + 1) * LANES]


def _load_token_major(ref, tm):
    return jnp.concatenate([ref[pl.ds(s, tm, stride=TOKEN_ROWS), :] for s in range(TOKEN_ROWS)],
                           axis=1)


def _merge_kernel(*refs, n_first):
    first, second = refs[0:5], refs[5:10]
    (mod_ref, wao_ref, wfn_ref, wout_ref, g2_ref, wrg_ref, brg_ref, x1_ref, h2_ref, grp_ref, rank_ref,
     cnt_ref, carry_ref) = refs[10:]
    i = pl.program_id(0)

    @pl.when(i == 0)
    def _():
        carry_ref[...] = jnp.zeros_like(carry_ref)

    def tile(src):
        x_ref, attn_ref, fm_ref, sga_ref, sgf_ref = src
        hm = x_ref.shape[0] // MERGE_SPLIT
        rows = [pl.ds(k * hm, hm) for k in range(MERGE_SPLIT)]
        live = [dict() for _ in range(MERGE_SPLIT)]

        def branches(k):
            live[k]["a"] = _dot(attn_ref[rows[k], :], wao_ref[...])
            live[k]["f"] = _dot(fm_ref[rows[k], :], wfn_ref[...])

        def gated(k):
            u = (sga_ref[rows[k], :].astype(F32) * live[k].pop("a")
                 + sgf_ref[rows[k], :].astype(F32) * live[k].pop("f"))
            live[k]["u"] = u.astype(BF16)

        def project(k):
            live[k]["y"] = _dot(live[k].pop("u"), wout_ref[...])

        def residual(k):
            x1 = x_ref[rows[k], :] + mod_ref[0, 2:3, :] * live[k].pop("y")
            x1_ref[rows[k], :] = x1
            h2 = (x1 * _rms(x1, D_MODEL) * g2_ref[...]) * (1.0 + mod_ref[0, 4:5, :]) + mod_ref[0, 3:4, :]
            _store_token_major(h2_ref, h2, hm, k * hm)
            live[k]["h2"] = _split_bf16(h2)

        def route(k):
            g = _rows(_router_logits(wrg_ref[...], brg_ref[...], *live[k].pop("h2")), N_GROUPS)
            gidx = _first_argmax(g, functools.reduce(jnp.maximum, g))
            onehot = jnp.where(lax.broadcasted_iota(jnp.int32, (SEG_ROWS, hm), 0) == gidx, 1.0, 0.0)
            before = (lax.broadcasted_iota(jnp.int32, (hm, hm), 0)
                      < lax.broadcasted_iota(jnp.int32, (hm, hm), 1))
            prefix = _dot(onehot.astype(BF16), jnp.where(before, 1.0, 0.0).astype(BF16))
            carry = carry_ref[...]
            rank = jnp.sum(onehot * (prefix + carry[:, 0:1]), axis=0, keepdims=True)
            grp_ref[:, rows[k]] = gidx
            rank_ref[:, rows[k]] = rank.astype(jnp.int32)
            carry_ref[...] = carry + jnp.sum(onehot, axis=1, keepdims=True)

        phases = [branches, gated, project, residual, route]
        for t in range(len(phases) + MERGE_SPLIT - 1):
            for k in range(MERGE_SPLIT):
                if 0 <= t - k < len(phases):
                    phases[t - k](k)
        cnt_ref[...] = carry_ref[...].astype(jnp.int32)

    @pl.when(i < n_first)
    def _():
        tile(first)

    @pl.when(i >= n_first)
    def _():
        tile(second)


def _merge(first, second, mod3, mod_row_fn, wts):
    n_first = first[0].shape[0] // TM
    n = first[0].shape[0] + second[0].shape[0]
    tok = lambda w: pl.BlockSpec((TM, w), lambda i: (i, 0))
    widths = (D_MODEL, V_WIDTH, FN_WIDTH, D_MODEL, D_MODEL)
    in_first = [pl.BlockSpec((TM, w), lambda i: (jnp.minimum(i, n_first - 1), 0)) for w in widths]
    in_second = [pl.BlockSpec((TM, w), lambda i: (jnp.maximum(i - n_first, 0), 0)) for w in widths]
    return pl.pallas_call(
        functools.partial(_merge_kernel, n_first=n_first),
        grid=(n // TM,),
        in_specs=in_first + in_second + [
                  pl.BlockSpec((1, 6, D_MODEL), lambda i: (mod_row_fn(i), 0, 0)),
                  _const_spec((V_WIDTH, D_MODEL)),
                  _const_spec((FN_WIDTH, D_MODEL)),
                  _const_spec((D_MODEL, D_MODEL)),
                  _const_spec((1, D_MODEL)),
                  _const_spec((ROUTER_ROWS, D_MODEL)),
                  _const_spec((ROUTER_ROWS, 1))],
        out_specs=[tok(D_MODEL),
                   pl.BlockSpec((TM * TOKEN_ROWS, LANES), lambda i: (i, 0)),
                   pl.BlockSpec((1, TM), lambda i: (0, i)),
                   pl.BlockSpec((1, TM), lambda i: (0, i)), _const_spec((SEG_ROWS, LANES))],
        out_shape=[jax.ShapeDtypeStruct((n, D_MODEL), F32),
                   jax.ShapeDtypeStruct((n * TOKEN_ROWS, LANES), F32),
                   jax.ShapeDtypeStruct((1, n), jnp.int32),
                   jax.ShapeDtypeStruct((1, n), jnp.int32),
                   jax.ShapeDtypeStruct((SEG_ROWS, LANES), jnp.int32)],
        scratch_shapes=[pltpu.VMEM((SEG_ROWS, LANES), F32)],
        compiler_params=_cparams(("arbitrary",)),
        name="merge",
    )(*first, *second, mod3, wts["w_ao"], wts["w_fn"], wts["w_out"], wts["g2"],
      wts["w_rg"], wts["b_rg"])


def _token_rows(ref, t):
    start = t * TOKEN_ROWS
    if not isinstance(t, int):
        start = pl.multiple_of(start, TOKEN_ROWS)
    return ref.at[pl.ds(start, TOKEN_ROWS)]


def _start_row_copy(src, dst, src_tok, dst_tok, sem, queue=0):
    pltpu.async_copy(_token_rows(src, src_tok), _token_rows(dst, dst_tok), sem, priority=queue)


def _wait_tile(src_hbm, buf, sem):
    pltpu.make_async_copy(src_hbm.at[pl.ds(0, buf.shape[0])], buf, sem).wait()


def _combine_kernel(pos_ref, y_hbm, x1_ref, mod_ref, o1_ref, o2_ref, buf, sem, *, n_first):
    i = pl.program_id(0)
    n_tiles = pl.num_programs(0)
    tm = o1_ref.shape[0]

    def start(tile_idx, r):
        slot = tile_idx % (GATHER_AHEAD + 1)
        queue = r % DMA_QUEUES if isinstance(r, int) else 0
        _start_row_copy(y_hbm, buf.at[slot], pos_ref[tile_idx * tm + r], r, sem.at[slot], queue)

    for first in range(GATHER_AHEAD):
        @pl.when(i == 0)
        def _():
            def body(r, c):
                start(first, r)
                return c

            lax.fori_loop(0, tm, body, 0, unroll=32)

    def run(fetch_ahead, o_ref):
        slot = i % (GATHER_AHEAD + 1)
        _wait_tile(y_hbm, buf.at[slot], sem.at[slot])
        lane_tiles = [slice(s * LANES, (s + 1) * LANES) for s in range(TOKEN_ROWS)]
        for s, cols in enumerate(lane_tiles):
            o_ref[:, cols] = mod_ref[0, 5:6, cols] * buf.at[slot][pl.ds(s, tm, stride=TOKEN_ROWS), :]
        per = tm // TOKEN_ROWS
        for s, cols in enumerate(lane_tiles):
            o_ref[:, cols] = x1_ref[:, cols] + o_ref[:, cols]
            if fetch_ahead:
                for r in range(s * per, (s + 1) * per):
                    start(i + GATHER_AHEAD, r)

    fetch = i + GATHER_AHEAD < n_tiles
    for fetch_ahead in (True, False):
        cond = fetch if fetch_ahead else jnp.logical_not(fetch)
        for o_ref, mine in ((o1_ref, i < n_first), (o2_ref, i >= n_first)):
            @pl.when(jnp.logical_and(cond, mine))
            def _():
                run(fetch_ahead, o_ref)


def _combine(y_sorted, pos, x1, mod3, mod_row_fn, n_first_tokens):
    n = x1.shape[0]
    n_first = n_first_tokens // TM
    assert n // TM >= GATHER_AHEAD
    return pl.pallas_call(
        functools.partial(_combine_kernel, n_first=n_first),
        grid_spec=pltpu.PrefetchScalarGridSpec(
            num_scalar_prefetch=1,
            grid=(n // TM,),
            in_specs=[pl.BlockSpec(memory_space=pl.ANY),
                      pl.BlockSpec((TM, D_MODEL), lambda i, pos: (i, 0)),
                      pl.BlockSpec((1, 6, D_MODEL), lambda i, pos: (mod_row_fn(i), 0, 0))],
            out_specs=[pl.BlockSpec((TM, D_MODEL), lambda i, pos: (jnp.minimum(i, n_first - 1), 0)),
                       pl.BlockSpec((TM, D_MODEL), lambda i, pos: (jnp.maximum(i - n_first, 0), 0))],
            scratch_shapes=[pltpu.VMEM((GATHER_AHEAD + 1, TM * TOKEN_ROWS, LANES), F32),
                            pltpu.SemaphoreType.DMA((GATHER_AHEAD + 1,))]),
        out_shape=[jax.ShapeDtypeStruct((n_first_tokens, D_MODEL), F32),
                   jax.ShapeDtypeStruct((n - n_first_tokens, D_MODEL), F32)],
        compiler_params=_cparams(("arbitrary",)),
        name="moe_combine",
    )(pos, y_sorted, x1, mod3)


def _moe_kernel(grp_ref, on_ref, src_ref, h_hbm, wrg_ref, brg_ref, wre_ref, bre_ref, wg_ref, wu_ref,
                wd_ref, o_ref, hbuf, sem, *, tile):
    i = pl.program_id(0)
    n_tiles = pl.num_programs(0)

    def start(tile_idx, r):
        slot = tile_idx % (GATHER_AHEAD + 1)
        _start_row_copy(h_hbm, hbuf.at[slot], src_ref[tile_idx * tile + r], r, sem.at[slot])

    for first in range(GATHER_AHEAD):
        @pl.when(jnp.logical_and(i == 0, on_ref[first] == 1))
        def _():
            def body(r, c):
                start(first, r)
                return c

            lax.fori_loop(0, tile, body, 0, unroll=32)

    def run(fetch_ahead):
        def issue(chunk, n_chunks):
            if fetch_ahead:
                per = tile // n_chunks
                for r in range(chunk * per, (chunk + 1) * per):
                    start(i + GATHER_AHEAD, r)

        slot = i % (GATHER_AHEAD + 1)
        _wait_tile(h_hbm, hbuf.at[slot], sem.at[slot])
        h2_hi, h2_lo = _split_bf16(_load_token_major(hbuf.at[slot], tile))

        logits = _router_logits(jnp.concatenate([wrg_ref[...], wre_ref[0]], axis=0),
                                jnp.concatenate([brg_ref[...], bre_ref[0]], axis=0), h2_hi, h2_lo)
        g = _rows(logits, N_GROUPS)
        gmax = functools.reduce(jnp.maximum, g)
        p_top = 1.0 / functools.reduce(lambda p, q: p + q, [jnp.exp(gj - gmax) for gj in g])
        e = _rows(logits[ROUTER_ROWS:], EXPERTS_PER_GROUP)
        m1 = functools.reduce(jnp.maximum, e)
        i1 = _first_argmax(e, m1)
        rest = [jnp.where(i1 == j, -jnp.inf, e[j]) for j in range(EXPERTS_PER_GROUP)]
        m2 = functools.reduce(jnp.maximum, rest)
        i2 = _first_argmax(rest, m2)
        t = jnp.exp(m2 - m1)
        w1 = p_top / (1.0 + t)
        w2 = p_top * t / (1.0 + t)
        row = lax.broadcasted_iota(jnp.int32, (LANES, tile), 0)
        comb = (jnp.where(row == i1, w1, 0.0) + jnp.where(row == i2, w2, 0.0)).T

        gates, ups = [], []
        for j in range(EXPERTS_PER_GROUP):
            gates.append(_dot(h2_hi, wg_ref[j]))
            issue(2 * j, 2 * EXPERTS_PER_GROUP)
            ups.append(_dot(h2_hi, wu_ref[j]))
            issue(2 * j + 1, 2 * EXPERTS_PER_GROUP)
        acts = [((a * _sigmoid(a)) * u * comb[:, j:j + 1]).astype(BF16)
                for j, (a, u) in enumerate(zip(gates, ups))]
        acc = functools.reduce(lambda p, q: p + q,
                               [_dot(acts[j], wd_ref[j]) for j in range(EXPERTS_PER_GROUP)])
        _store_token_major(o_ref, acc, tile)

    @pl.when(on_ref[i] == 0)
    def _():
        o_ref[...] = jnp.zeros_like(o_ref)

    ahead = jnp.minimum(i + GATHER_AHEAD, n_tiles - 1)
    fetch = jnp.logical_and(i + GATHER_AHEAD < n_tiles, on_ref[ahead] == 1)

    @pl.when(jnp.logical_and(on_ref[i] == 1, fetch))
    def _():
        run(True)

    @pl.when(jnp.logical_and(on_ref[i] == 1, jnp.logical_not(fetch)))
    def _():
        run(False)


def _moe(h2tm, src, maps, wts, w_gate, w_up, w_down, tile):
    n_tiles = src.shape[0] // tile
    const = lambda shape: pl.BlockSpec(shape, lambda i, grp, on, src: (0,) * len(shape))
    by_group = lambda shape: pl.BlockSpec(
        shape, lambda i, grp, on, src: (grp[i],) + (0,) * (len(shape) - 1))
    return pl.pallas_call(
        functools.partial(_moe_kernel, tile=tile),
        grid_spec=pltpu.PrefetchScalarGridSpec(
            num_scalar_prefetch=3,
            grid=(n_tiles,),
            in_specs=[pl.BlockSpec(memory_space=pl.ANY),
                      const((ROUTER_ROWS, D_MODEL)), const((ROUTER_ROWS, 1)),
                      by_group((1, ROUTER_ROWS, D_MODEL)), by_group((1, ROUTER_ROWS, 1)),
                      by_group((EXPERTS_PER_GROUP, D_MODEL, D_EXPERT)),
                      by_group((EXPERTS_PER_GROUP, D_MODEL, D_EXPERT)),
                      by_group((EXPERTS_PER_GROUP, D_EXPERT, D_MODEL))],
            out_specs=pl.BlockSpec((tile * TOKEN_ROWS, LANES), lambda i, grp, on, src: (i, 0)),
            scratch_shapes=[pltpu.VMEM((GATHER_AHEAD + 1, tile * TOKEN_ROWS, LANES), F32),
                            pltpu.SemaphoreType.DMA((GATHER_AHEAD + 1,))]),
        out_shape=jax.ShapeDtypeStruct((n_tiles * tile * TOKEN_ROWS, LANES), F32),
        compiler_params=_cparams(("arbitrary",)),
        name="moe",
    )(*maps, src, h2tm, wts["w_rg"], wts["b_rg"], wts["w_re"], wts["b_re"], w_gate, w_up, w_down)


def _invert_kernel(pos_ref, lo_ref, hi_ref, src_ref, *, n, n_fill):
    def zero(p, c):
        src_ref[p] = 0
        return c

    for s in range(n_fill):
        lax.fori_loop(lo_ref[s], hi_ref[s], zero, 0)

    def put(t, c):
        src_ref[pos_ref[t]] = t
        return c

    lax.fori_loop(0, n, put, 0, unroll=8)


def _invert(pos, fill_lo, fill_hi, n_sorted):
    return pl.pallas_call(
        functools.partial(_invert_kernel, n=pos.shape[0], n_fill=fill_lo.shape[0]),
        grid_spec=pltpu.PrefetchScalarGridSpec(
            num_scalar_prefetch=3, grid=(1,), in_specs=[],
            out_specs=pl.BlockSpec(memory_space=pltpu.SMEM)),
        out_shape=jax.ShapeDtypeStruct((n_sorted,), jnp.int32),
        name="moe_invert",
    )(pos, fill_lo, fill_hi)


def _sort_plan(grp, rank, cnt, tile, n_tiles):
    n_tile_grp = (cnt + tile - 1) // tile
    tile_end = jnp.cumsum(n_tile_grp)
    tile_start = tile_end - n_tile_grp
    total = tile_end[-1]
    base = (tile_start * tile).astype(jnp.int32)
    pos = rank + functools.reduce(lambda acc, g: jnp.where(grp == g, base[g], acc),
                                  range(N_GROUPS - 1), base[N_GROUPS - 1])
    fill_lo = jnp.concatenate([tile_start * tile + cnt, total[None] * tile]).astype(jnp.int32)
    fill_hi = jnp.concatenate([tile_end * tile, jnp.full((1,), n_tiles * tile)]).astype(jnp.int32)
    src = _invert(pos, fill_lo, fill_hi, n_tiles * tile)
    i = jnp.arange(n_tiles, dtype=jnp.int32)
    tile_grp = jnp.sum((jnp.minimum(i, total - 1)[:, None] >= tile_end[None, :]).astype(jnp.int32),
                       axis=1)
    return pos, src, (tile_grp.astype(jnp.int32), (i < total).astype(jnp.int32))


def _sparse_moe(x1, h2tm, grp, rank, cnt8, mod3, mod_row_fn, wts, w_gate, w_up, w_down, tile,
                n_first_tokens):
    n = x1.shape[0]
    n_tiles = n // tile + N_GROUPS
    pos, src, maps = _sort_plan(grp.reshape(n), rank.reshape(n), cnt8[:N_GROUPS, 0], tile, n_tiles)
    y_sorted = _moe(h2tm, src, maps, wts, w_gate, w_up, w_down, tile)
    return _combine(y_sorted, pos, x1, mod3, mod_row_fn, n_first_tokens)


def _pack_weights(l, norm1_g, q_a_norm_g, w_q_b, kv_a_norm_g, w_kv_b, q_norm_g, k_norm_g,
                  w_attn_o, w_fnet, w_out, norm2_g, w_router_group, b_router_group,
                  w_router_expert, b_router_expert):
    w_qb = jnp.pad(w_q_b[l].reshape(Q_LORA, N_HEADS, QK_HEAD),
                   ((0, 0), (0, 0), (0, HEAD_SLOT - QK_HEAD))).reshape(Q_LORA, QK_WIDTH)
    wkv = w_kv_b[l].reshape(KV_LORA, N_HEADS, QK_NOPE + V_HEAD)
    w_kb = jnp.pad(wkv[:, :, :QK_NOPE],
                   ((0, 0), (0, 0), (0, HEAD_SLOT - QK_NOPE))).reshape(KV_LORA, QK_WIDTH)
    w_vb = wkv[:, :, QK_NOPE:].reshape(KV_LORA, V_WIDTH)
    pad_g = lambda g: jnp.pad(g, (0, HEAD_SLOT - QK_HEAD)).reshape(1, HEAD_SLOT)
    w_rg = jnp.pad(w_router_group[l].T, ((0, ROUTER_ROWS - N_GROUPS), (0, 0)))
    b_rg = jnp.pad(b_router_group[l], (0, ROUTER_ROWS - N_GROUPS)).reshape(ROUTER_ROWS, 1)
    w_re = jnp.pad(w_router_expert[l].T.reshape(N_GROUPS, EXPERTS_PER_GROUP, D_MODEL),
                   ((0, 0), (0, ROUTER_ROWS - EXPERTS_PER_GROUP), (0, 0)))
    b_re = jnp.pad(b_router_expert[l].reshape(N_GROUPS, EXPERTS_PER_GROUP),
                   ((0, 0), (0, ROUTER_ROWS - EXPERTS_PER_GROUP))).reshape(N_GROUPS, ROUTER_ROWS, 1)
    dft_c, dft_ns = _dft_tables(FN_GROUP_W)
    return {
        "g1": norm1_g[l].reshape(1, D_MODEL),
        "qag": q_a_norm_g[l].reshape(1, Q_LORA),
        "w_qb": w_qb.astype(BF16),
        "kvg": kv_a_norm_g[l].reshape(1, KV_LORA),
        "w_kb": w_kb.astype(BF16),
        "w_vb": w_vb.T.astype(BF16),
        "qg": pad_g(q_norm_g[l]),
        "kg": pad_g(k_norm_g[l]),
        "dft_c": jnp.concatenate([jnp.asarray(dft_c), -jnp.asarray(dft_ns)], axis=1).astype(BF16),
        "w_ao": w_attn_o[l].astype(BF16),
        "w_fn": w_fnet[l].astype(BF16),
        "w_out": w_out[l].astype(BF16),
        "g2": norm2_g[l].reshape(1, D_MODEL),
        "w_rg": w_rg,
        "b_rg": b_rg,
        "w_re": w_re,
        "b_re": b_re,
    }


def _layer(xp, xs, cache_ckv_l, cache_kpe_l, mod3, wts, experts):
    bp, sp, _ = xp.shape
    bs, ss, _ = xs.shape
    ctx_row = lambda i: 0
    lat_row = lambda i: 1 + i // (ss // TM)

    xp2 = xp.reshape(bp * sp, D_MODEL)
    ride = (bp * sp) // TM == N_EXPERTS
    cs, ns = (jnp.asarray(t).astype(BF16) for t in _dft_tables(sp))
    outs = _inproj(xp2, mod3, ctx_row, wts, None, True, experts if ride else (), (sp, cs, ns))
    attn, fm, sga, sgf, ckv, kpe = outs[:6]
    w_gate, w_up, w_down = outs[6:] if ride else (w.astype(BF16) for w in experts)
    ctx_set = (xp2, attn, fm, sga, sgf)

    xs2 = xs.reshape(bs * ss, D_MODEL)
    rope_tabs = tuple(jnp.asarray(t) for t in _rope_tables(ss))
    q, k, v, fcs, sga, sgf = _inproj(xs2, mod3, lat_row, wts, rope_tabs, False)
    kpe_slot = jnp.pad(cache_kpe_l, ((0, 0), (0, 0), (QK_NOPE, LANES - QK_HEAD)))
    cs, ns = (jnp.asarray(t).astype(BF16) for t in _dft_tables(ss))
    attn, fm = _attention(q.reshape(bs, ss, QK_WIDTH), k.reshape(bs, ss, QK_WIDTH), v, cache_ckv_l,
                          kpe_slot, wts, fcs.reshape(bs, ss, 2 * FN_WIDTH), cs, ns, TQ_LAT, "attn_lat")
    lat_set = (xs2, attn.reshape(bs * ss, V_WIDTH), fm.reshape(bs * ss, FN_WIDTH), sga, sgf)

    n_ctx_tiles = (bp * sp) // TM
    all_row = lambda i: jnp.where(i < n_ctx_tiles, 0, lat_row(i - n_ctx_tiles))
    x1, h2tm, grp, rank, cnt = _merge(ctx_set, lat_set, mod3, all_row, wts)
    yp, ys = _sparse_moe(x1, h2tm, grp, rank, cnt, mod3, all_row, wts, w_gate, w_up, w_down, MOE_TILE,
                         bp * sp)
    yp, ys = yp.reshape(bp, sp, D_MODEL), ys.reshape(bs, ss, D_MODEL)

    return yp, ys, ckv.reshape(bp, sp, KV_LORA), jnp.swapaxes(kpe, 1, 2)


def kernel(x_prompt, x_sample, cache_ckv, cache_kpe, c, c_ctx, w_mod, b_mod, norm1_g, w_in, q_a_norm_g, w_q_b, kv_a_norm_g, w_kv_b, q_norm_g, k_norm_g, w_attn_o, w_fnet, w_out, norm2_g, w_router_group, b_router_group, w_router_expert, b_router_expert, w_exp_gate, w_exp_up, w_exp_down):
    depth = w_mod.shape[0]
    n_lat = c.shape[0]
    assert 1 + n_lat <= MOD_ROWS
    cond8 = jnp.concatenate([c_ctx[None, :], c, jnp.zeros((MOD_ROWS - 1 - n_lat, D_MODEL), F32)], axis=0)
    xp, xs = x_prompt, x_sample
    ckv_layers, kpe_layers = [], []
    for l in range(depth):
        mod, w_in_p = _adaln(cond8, w_mod[l], b_mod[l].reshape(1, -1), jnp.swapaxes(w_in[l], 0, 1))
        mod3 = mod.reshape(MOD_ROWS, 6, D_MODEL)
        wts = _pack_weights(l, norm1_g, q_a_norm_g, w_q_b, kv_a_norm_g, w_kv_b, q_norm_g,
                            k_norm_g, w_attn_o, w_fnet, w_out, norm2_g, w_router_group,
                            b_router_group, w_router_expert, b_router_expert)
        wts["w_in"] = w_in_p
        xp, xs, ckv, kpe = _layer(xp, xs, cache_ckv[:, l], cache_kpe[:, l], mod3, wts,
                                  (w_exp_gate[l], w_exp_up[l], w_exp_down[l]))
        ckv_layers.append(ckv)
        kpe_layers.append(kpe)
    return xp, xs, jnp.stack(ckv_layers, axis=1), jnp.stack(kpe_layers, axis=1)
```
